```python
import math
import jax, jax.numpy as jnp
from jax import lax
import numpy as np

D_MODEL = 1024
BATCH = 16
SEQ = 256
DEPTH = 1
DEC_BATCH = 2
DEC_SEQ = 4096
PAST_LEN = 512

GRID_W = 64
ATT_HEADS = 4
QK_DIM = 64
V_DIM = 2 * QK_DIM
D_QK = ATT_HEADS * 2 * QK_DIM
D_ATT = ATT_HEADS * V_DIM
ROPE_BASE = 10000.0
Q_BLOCK = 128
D_SSM = D_MODEL // 2
SSM_HEADDIM = 64
SSM_HEADS = D_SSM // SSM_HEADDIM
SSM_GROUPS = 2
D_STATE = 128
CONV_W = 5
CHUNK = 128
XBC_DIM = D_SSM + 2 * SSM_GROUPS * D_STATE
MIX_W = D_ATT + D_SSM
IN_COLS = 2 * D_QK + D_ATT + D_SSM + XBC_DIM + 2 * SSM_HEADS
N_GROUPS = 4
EXPERTS_PER_GROUP = 4
N_EXPERTS = N_GROUPS * EXPERTS_PER_GROUP
TOP_K_IN_GROUP = 2
EXPERT_FF = 256
EPS = 1e-6

kernel_name = "hybrid_diffattn_ssd_hmoe_prefix_dit_step"


def rms_norm(x, g):
    xf = x.astype(jnp.float32)
    y = xf * lax.rsqrt(jnp.mean(xf * xf, axis=-1, keepdims=True) + EPS)
    return y.astype(x.dtype) * g


def ada_mod(cond, w, b):
    mod = (jax.nn.silu(cond) @ w + b)[..., None, :]
    return jnp.split(mod, 6, axis=-1)


def lambda_init(layer):
    return 0.8 - 0.6 * math.exp(-0.3 * layer)


def axial_rope(x, row_ids, col_ids):
    n_freq = QK_DIM // 4
    freqs = ROPE_BASE ** (-jnp.arange(n_freq, dtype=jnp.float32) / n_freq)

    def rotate(xa, pos):
        ang = pos[:, None] * freqs
        cos = jnp.cos(ang)[None, :, None, None, :].astype(xa.dtype)
        sin = jnp.sin(ang)[None, :, None, None, :].astype(xa.dtype)
        x1, x2 = jnp.split(xa, 2, axis=-1)
        return jnp.concatenate([x1 * cos - x2 * sin, x1 * sin + x2 * cos], axis=-1)

    half = QK_DIM // 2
    return jnp.concatenate([rotate(x[..., :half], row_ids), rotate(x[..., half:], col_ids)], axis=-1)


def diff_attention(q, k, v, lam):
    b, tq = q.shape[:2]
    nblk = tq // Q_BLOCK
    qb = jnp.moveaxis(q.reshape(b, nblk, Q_BLOCK, ATT_HEADS, 2, QK_DIM), 1, 0)
    scale = QK_DIM ** -0.5

    def block(qblk):
        s = jnp.einsum('bqhmd,bkhmd->bhmqk', qblk, k).astype(jnp.float32) * scale
        p = jax.nn.softmax(s, axis=-1)
        a = (p[:, :, 0] - lam * p[:, :, 1]).astype(v.dtype)
        return jnp.einsum('bhqk,bkhd->bqhd', a, v)

    o = lax.map(block, qb)
    return jnp.moveaxis(o, 0, 1).reshape(b, tq, ATT_HEADS, V_DIM)


def centred_dwconv(x, w, bias):
    pad = CONV_W // 2
    y = lax.conv_general_dilated(x, w[:, None, :], window_strides=(1,), padding=[(pad, pad)],
                                 dimension_numbers=('NWC', 'WIO', 'NWC'),
                                 feature_group_count=x.shape[-1])
    return y + bias


def ssd_chunked(x, dt, a, bm, cm, h0):
    b, l, h, p = x.shape
    n = bm.shape[-1]
    nc = l // CHUNK
    xd = (x * dt[..., None]).reshape(b, nc, CHUNK, h, p)
    la = (dt * a).reshape(b, nc, CHUNK, h)
    bc = bm.reshape(b, nc, CHUNK, h, n)
    cc = cm.reshape(b, nc, CHUNK, h, n)
    cum = jnp.cumsum(la, axis=2)
    causal = jnp.tril(jnp.ones((CHUNK, CHUNK), dtype=bool))
    seg = cum[:, :, :, None, :] - cum[:, :, None, :, :]
    decay_ij = jnp.exp(jnp.where(causal[None, None, :, :, None], seg, -jnp.inf))
    scores = jnp.einsum('bcihn,bcjhn->bcijh', cc, bc) * decay_ij
    y_diag = jnp.einsum('bcijh,bcjhp->bcihp', scores, xd)
    to_end = jnp.exp(cum[:, :, -1:, :] - cum)
    chunk_states = jnp.einsum('bclhn,bclh,bclhp->bchpn', bc, to_end, xd)
    chunk_decay = jnp.exp(cum[:, :, -1, :])

    def step(state, inp):
        dec, st = inp
        return state * dec[:, :, None, None] + st, state

    final, prev = lax.scan(step, h0, (jnp.moveaxis(chunk_decay, 1, 0), jnp.moveaxis(chunk_states, 1, 0)))
    prev = jnp.moveaxis(prev, 0, 1)
    y_off = jnp.einsum('bclhn,bchpn->bclhp', cc, prev) * jnp.exp(cum)[..., None]
    return (y_diag + y_off).reshape(b, l, h, p), final


def bidir_ssd(xbc_raw, z, dt_raw, lw, h0f, h0b):
    b, l, _ = xbc_raw.shape
    xbc = jax.nn.silu(centred_dwconv(xbc_raw, lw['conv_w'], lw['conv_b']))
    xs, bm, cm = jnp.split(xbc, [D_SSM, D_SSM + SSM_GROUPS * D_STATE], axis=-1)
    xs = xs.reshape(b, l, SSM_HEADS, SSM_HEADDIM)
    rep = SSM_HEADS // SSM_GROUPS
    bm = jnp.repeat(bm.reshape(b, l, SSM_GROUPS, D_STATE), rep, axis=2)
    cm = jnp.repeat(cm.reshape(b, l, SSM_GROUPS, D_STATE), rep, axis=2)
    dtf_raw, dtb_raw = jnp.split(dt_raw, 2, axis=-1)
    dt_f = jax.nn.softplus(dtf_raw + lw['dt_bias_fwd'])
    dt_b = jax.nn.softplus(dtb_raw + lw['dt_bias_bwd'])
    y_f, hf = ssd_chunked(xs, dt_f, -jnp.exp(lw['a_log_fwd']), bm, cm, h0f)
    flip = lambda t: jnp.flip(t, axis=1)
    y_b, hb = ssd_chunked(flip(xs), flip(dt_b), -jnp.exp(lw['a_log_bwd']), flip(bm), flip(cm), h0b)
    y = (y_f + flip(y_b) + xs * lw['ssm_d'][:, None]).reshape(b, l, D_SSM)
    y = rms_norm(y * jax.nn.silu(z), lw['ssm_norm_g'])
    return y, hf, hb


def token_mixer(h, lw, layer, rope_pos, ctx_k, ctx_v, h0f, h0b):
    b, t, _ = h.shape
    splits = [D_QK, 2 * D_QK, 2 * D_QK + D_ATT, 2 * D_QK + D_ATT + D_SSM,
              2 * D_QK + D_ATT + D_SSM + XBC_DIM]
    q, k, v, z, xbc, dt_raw = jnp.split(h @ lw['w_in'], splits, axis=-1)
    q = q.reshape(b, t, ATT_HEADS, 2, QK_DIM)
    k = k.reshape(b, t, ATT_HEADS, 2, QK_DIM)
    v = v.reshape(b, t, ATT_HEADS, V_DIM)
    if rope_pos is None:
        k_all, v_all = k, v
    else:
        q = axial_rope(q, *rope_pos)
        k = axial_rope(k, *rope_pos)
        k_all = jnp.concatenate([k, ctx_k], axis=1)
        v_all = jnp.concatenate([v, ctx_v], axis=1)
    lam0 = lambda_init(layer)
    lam = (jnp.exp(jnp.sum(lw['lambda_q1'] * lw['lambda_k1']).astype(jnp.float32))
           - jnp.exp(jnp.sum(lw['lambda_q2'] * lw['lambda_k2']).astype(jnp.float32)) + lam0)
    o = diff_attention(q, k_all, v_all, lam)
    att = (rms_norm(o, lw['attn_subln_g']) * (1.0 - lam0)).reshape(b, t, D_ATT)
    ssm, hf, hb = bidir_ssd(xbc, z, dt_raw, lw, h0f, h0b)
    out = jnp.concatenate([att, ssm], axis=-1) @ lw['w_out']
    return out, k, v, hf, hb


def hier_moe(h, lw):
    shape = h.shape
    hf = h.reshape(-1, D_MODEL)
    n = hf.shape[0]
    g_logits = (hf @ lw['w_group_router'] + lw['b_group_router']).astype(jnp.float32)
    g_prob = jax.nn.softmax(g_logits, axis=-1)
    g_idx = jnp.argmax(g_logits, axis=-1)
    p_g = jnp.take_along_axis(g_prob, g_idx[:, None], axis=-1)
    e_logits = (hf @ lw['w_expert_router'] + lw['b_expert_router']).astype(jnp.float32)
    e_logits = e_logits.reshape(n, N_GROUPS, EXPERTS_PER_GROUP)
    e_sel = jnp.take_along_axis(e_logits, g_idx[:, None, None], axis=1)[:, 0]
    top_l, top_i = lax.top_k(e_sel, TOP_K_IN_GROUP)
    weights = p_g * jax.nn.softmax(top_l, axis=-1)
    expert_ids = g_idx[:, None] * EXPERTS_PER_GROUP + top_i
    combine = jnp.sum(jax.nn.one_hot(expert_ids, N_EXPERTS, dtype=jnp.float32) * weights[..., None], axis=1)
    hidden = (jax.nn.silu(jnp.einsum('nd,edf->nef', hf, lw['w_exp_gate']))
              * jnp.einsum('nd,edf->nef', hf, lw['w_exp_up']))
    y = jnp.einsum('nef,efd->nd', hidden * combine.astype(hf.dtype)[..., None], lw['w_exp_down'])
    return y.reshape(shape)


def trunk_layer(x, mods, lw, layer, rope_pos, ctx_k, ctx_v, h0f, h0b):
    shift1, scale1, gate1, shift2, scale2, gate2 = mods
    h = rms_norm(x, lw['norm_mix_g']) * (1 + scale1) + shift1
    mix, k, v, hf, hb = token_mixer(h, lw, layer, rope_pos, ctx_k, ctx_v, h0f, h0b)
    x = x + gate1 * mix
    h = rms_norm(x, lw['norm_ffn_g']) * (1 + scale2) + shift2
    x = x + gate2 * hier_moe(h, lw)
    return x, k, v, hf, hb


def setup_inputs(seed: int = 0) -> dict:
    key = jax.random.key(seed)
    ks = iter(jax.random.split(key, 48))
    nrm = lambda shape, scale: jax.random.normal(next(ks), shape, jnp.float32) * scale
    gain = lambda shape: 1.0 + nrm(shape, 0.02)

    def a_log(shape):
        return jnp.log(jax.random.uniform(next(ks), shape, jnp.float32, minval=1.0, maxval=16.0))

    def dt_bias(shape):
        dt = jnp.exp(jax.random.uniform(next(ks), shape, jnp.float32,
                                        minval=math.log(1e-3), maxval=math.log(1e-1)))
        return dt + jnp.log(-jnp.expm1(-dt))

    return {
        "x_prompt": nrm((BATCH, SEQ, D_MODEL), 1.0),
        "x_sample": nrm((DEC_BATCH, DEC_SEQ, D_MODEL), 1.0),
        "cache_k": nrm((DEC_BATCH, DEPTH, PAST_LEN, ATT_HEADS, 2, QK_DIM), 1.0),
        "cache_v": nrm((DEC_BATCH, DEPTH, PAST_LEN, ATT_HEADS, V_DIM), 1.0),
        "state_ssm_fwd": nrm((DEC_BATCH, DEPTH, SSM_HEADS, SSM_HEADDIM, D_STATE), 0.5),
        "state_ssm_bwd": nrm((DEC_BATCH, DEPTH, SSM_HEADS, SSM_HEADDIM, D_STATE), 0.5),
        "c": nrm((DEC_BATCH, D_MODEL), 1.0),
        "c_ctx": nrm((D_MODEL,), 1.0),
        "w_ada": nrm((DEPTH, D_MODEL, 6 * D_MODEL), 0.5 * D_MODEL ** -0.5),
        "b_ada": nrm((DEPTH, 6 * D_MODEL), 0.02),
        "norm_mix_g": gain((DEPTH, D_MODEL)),
        "w_in": nrm((DEPTH, D_MODEL, IN_COLS), D_MODEL ** -0.5),
        "w_out": nrm((DEPTH, MIX_W, D_MODEL), MIX_W ** -0.5),
        "lambda_q1": nrm((DEPTH, QK_DIM), 0.1),
        "lambda_k1": nrm((DEPTH, QK_DIM), 0.1),
        "lambda_q2": nrm((DEPTH, QK_DIM), 0.1),
        "lambda_k2": nrm((DEPTH, QK_DIM), 0.1),
        "attn_subln_g": gain((DEPTH, V_DIM)),
        "conv_w": nrm((DEPTH, CONV_W, XBC_DIM), CONV_W ** -0.5),
        "conv_b": nrm((DEPTH, XBC_DIM), 0.02),
        "a_log_fwd": a_log((DEPTH, SSM_HEADS)),
        "a_log_bwd": a_log((DEPTH, SSM_HEADS)),
        "dt_bias_fwd": dt_bias((DEPTH, SSM_HEADS)),
        "dt_bias_bwd": dt_bias((DEPTH, SSM_HEADS)),
        "ssm_d": gain((DEPTH, SSM_HEADS)),
        "ssm_norm_g": gain((DEPTH, D_SSM)),
        "norm_ffn_g": gain((DEPTH, D_MODEL)),
        "w_group_router": nrm((DEPTH, D_MODEL, N_GROUPS), D_MODEL ** -0.5),
        "b_group_router": nrm((DEPTH, N_GROUPS), 0.01),
        "w_expert_router": nrm((DEPTH, D_MODEL, N_EXPERTS), D_MODEL ** -0.5),
        "b_expert_router": nrm((DEPTH, N_EXPERTS), 0.01),
        "w_exp_gate": nrm((DEPTH, N_EXPERTS, D_MODEL, EXPERT_FF), D_MODEL ** -0.5),
        "w_exp_up": nrm((DEPTH, N_EXPERTS, D_MODEL, EXPERT_FF), D_MODEL ** -0.5),
        "w_exp_down": nrm((DEPTH, N_EXPERTS, EXPERT_FF, D_MODEL), EXPERT_FF ** -0.5),
        "final_norm_g": gain((D_MODEL,)),
    }


def reference(x_prompt, x_sample, cache_k, cache_v, state_ssm_fwd, state_ssm_bwd, c, c_ctx,
              w_ada, b_ada, norm_mix_g, w_in, w_out, lambda_q1, lambda_k1, lambda_q2, lambda_k2,
              attn_subln_g, conv_w, conv_b, a_log_fwd, a_log_bwd, dt_bias_fwd, dt_bias_bwd,
              ssm_d, ssm_norm_g, norm_ffn_g, w_group_router, b_group_router, w_expert_router,
              b_expert_router, w_exp_gate, w_exp_up, w_exp_down, final_norm_g):
    rows = x_sample.shape[1] // GRID_W
    row_ids = jnp.repeat(jnp.arange(rows, dtype=jnp.float32), GRID_W)
    col_ids = jnp.tile(jnp.arange(GRID_W, dtype=jnp.float32), rows)

    xp, xs = x_prompt, x_sample
    bp = x_prompt.shape[0]
    new_k, new_v, new_hf, new_hb = [], [], [], []
    for l in range(DEPTH):
        lw = dict(norm_mix_g=norm_mix_g[l], w_in=w_in[l], w_out=w_out[l],
                  lambda_q1=lambda_q1[l], lambda_k1=lambda_k1[l],
                  lambda_q2=lambda_q2[l], lambda_k2=lambda_k2[l], attn_subln_g=attn_subln_g[l],
                  conv_w=conv_w[l], conv_b=conv_b[l], a_log_fwd=a_log_fwd[l], a_log_bwd=a_log_bwd[l],
                  dt_bias_fwd=dt_bias_fwd[l], dt_bias_bwd=dt_bias_bwd[l], ssm_d=ssm_d[l],
                  ssm_norm_g=ssm_norm_g[l], norm_ffn_g=norm_ffn_g[l],
                  w_group_router=w_group_router[l], b_group_router=b_group_router[l],
                  w_expert_router=w_expert_router[l], b_expert_router=b_expert_router[l],
                  w_exp_gate=w_exp_gate[l], w_exp_up=w_exp_up[l], w_exp_down=w_exp_down[l])
        zeros_state = jnp.zeros((bp, SSM_HEADS, SSM_HEADDIM, D_STATE), xp.dtype)
        mods_ctx = ada_mod(c_ctx[None], w_ada[l], b_ada[l])
        xp, ck, cv, hf, hb = trunk_layer(xp, mods_ctx, lw, l, None, None, None, zeros_state, zeros_state)
        new_k.append(ck)
        new_v.append(cv)
        new_hf.append(hf)
        new_hb.append(hb)
        mods_lat = ada_mod(c, w_ada[l], b_ada[l])
        xs, _, _, _, _ = trunk_layer(xs, mods_lat, lw, l, (row_ids, col_ids), cache_k[:, l], cache_v[:, l],
                                     state_ssm_fwd[:, l], state_ssm_bwd[:, l])
    y_prompt = rms_norm(xp, final_norm_g)
    y_sample = rms_norm(xs, final_norm_g)
    return (y_prompt, y_sample, jnp.stack(new_k, axis=1), jnp.stack(new_v, axis=1),
            jnp.stack(new_hf, axis=1), jnp.stack(new_hb, axis=1))
```

```python
import functools
import math

import jax
import jax.numpy as jnp
from jax import lax
from jax.experimental import pallas as pl
from jax.experimental.pallas import tpu as pltpu

D_MODEL = 1024
GRID_W = 64
ATT_HEADS = 4
QK_DIM = 64
V_DIM = 128
D_QK = 512
D_ATT = 512
ROPE_BASE = 10000.0
D_SSM = 512
SSM_HEADDIM = 64
SSM_HEADS = 8
SSM_GROUPS = 2
D_STATE = 128
CONV_W = 5
CHUNK = 128
XBC_DIM = 1024
N_GROUPS = 4
EXPERTS_PER_GROUP = 4
N_EXPERTS = 16
EXPERT_FF = 256
EPS = 1e-6
MAIN_COLS = 2 * D_QK + D_ATT + D_SSM + XBC_DIM
LANES = 128
HALO = 8
VMEM_LIMIT = 56 * 1024 * 1024

F32 = jnp.float32
BF16 = jnp.bfloat16


def _cparams(sem):
    return pltpu.CompilerParams(dimension_semantics=sem, vmem_limit_bytes=VMEM_LIMIT)


def _sigmoid(x):
    return 1.0 / (1.0 + jnp.exp(-x))


def _silu(x):
    return x * _sigmoid(x)


def _ada_kernel(condT_ref, w_ref, b_ref, o_ref):
    s = _silu(condT_ref[...])
    w = w_ref[...]
    b = b_ref[...]
    o_ref[...] = jnp.zeros_like(o_ref)
    for r in range(3):
        o_ref[r:r + 1, :] = jnp.sum(w * s[:, r:r + 1], axis=0, keepdims=True) + b


def _ada(condT, w_ada, b_ada):
    bn = 512
    n = w_ada.shape[1]
    return pl.pallas_call(
        _ada_kernel,
        grid=(n // bn,),
        in_specs=[pl.BlockSpec((D_MODEL, 8), lambda j: (0, 0)),
                  pl.BlockSpec((D_MODEL, bn), lambda j: (0, j)),
                  pl.BlockSpec((1, bn), lambda j: (0, j))],
        out_specs=pl.BlockSpec((8, bn), lambda j: (0, j)),
        out_shape=jax.ShapeDtypeStruct((8, n), F32),
        compiler_params=_cparams(("arbitrary",)),
        name="ada",
    )(condT, w_ada, b_ada)


def _inproj_kernel(rope, x_ref, mods_ref, g_ref, w_ref, wdt_ref, *rest):
    if rope:
        cos_ref, sa_ref, sb_ref, q_ref, k_ref, v_ref, z_ref, xbc_ref, dt_ref = rest
    else:
        q_ref, k_ref, v_ref, z_ref, xbc_ref, dt_ref = rest
    x = x_ref[...]
    shift = mods_ref[:, 0:D_MODEL]
    scale = mods_ref[:, D_MODEL:2 * D_MODEL]
    y = x * lax.rsqrt(jnp.mean(x * x, axis=-1, keepdims=True) + EPS) * g_ref[...]
    h = (y * (1.0 + scale) + shift).astype(BF16)
    r = jnp.dot(h, w_ref[...], preferred_element_type=F32)
    dt_ref[...] = jnp.dot(h, wdt_ref[...], preferred_element_type=F32)
    q = r[:, 0:D_QK]
    k = r[:, D_QK:2 * D_QK]
    if rope:
        cos = cos_ref[...]
        sa = sa_ref[...]
        sb = sb_ref[...]

        def rot(t):
            parts = []
            for hh in range(ATT_HEADS):
                th = t[:, hh * LANES:(hh + 1) * LANES]
                parts.append(th * cos + pltpu.roll(th, LANES - 16, 1) * sa + pltpu.roll(th, 16, 1) * sb)
            return jnp.concatenate(parts, axis=1)

        q = rot(q)
        k = rot(k)
    q_ref[...] = q
    k_ref[...] = k
    v_ref[...] = r[:, 2 * D_QK:2 * D_QK + D_ATT]
    z_ref[...] = r[:, 2 * D_QK + D_ATT:2 * D_QK + D_ATT + D_SSM]
    xbc_ref[...] = r[:, 2 * D_QK + D_ATT + D_SSM:MAIN_COLS]


def _inproj(x, mods3, mod_row0, mod_tokens, seq_len, g, w_main, w_dt, rope_tabs):
    n = x.shape[0]
    tm = 256
    per_seq = seq_len // tm
    per_mod = mod_tokens // tm
    rope = rope_tabs is not None
    in_specs = [pl.BlockSpec((tm, D_MODEL), lambda i: (i, 0)),
                pl.BlockSpec((None, 1, 6 * D_MODEL), lambda i: (mod_row0 + i // per_mod, 0, 0)),
                pl.BlockSpec((1, D_MODEL), lambda i: (0, 0)),
                pl.BlockSpec((D_MODEL, MAIN_COLS), lambda i: (0, 0)),
                pl.BlockSpec((D_MODEL, LANES), lambda i: (0, 0))]
    args = [x, mods3, g, w_main, w_dt]
    if rope:
        tab_spec = pl.BlockSpec((tm, LANES), lambda i: (i % per_seq, 0))
        in_specs += [tab_spec] * 3
        args += list(rope_tabs)
    widths = (D_QK, D_QK, D_ATT, D_SSM, XBC_DIM, LANES)
    return pl.pallas_call(
        functools.partial(_inproj_kernel, rope),
        grid=(n // tm,),
        in_specs=in_specs,
        out_specs=[pl.BlockSpec((tm, wd), lambda i: (i, 0)) for wd in widths],
        out_shape=[jax.ShapeDtypeStruct((n, wd), F32) for wd in widths],
        compiler_params=_cparams(("parallel",)),
        name="inproj_rope" if rope else "inproj",
    )(*args)


def _attn_kernel(tk, lam0, q_ref, k_ref, v_ref, lamp_ref, g_ref, o_ref):
    tq = q_ref.shape[0]
    n_kv = k_ref.shape[0] // tk
    lp = lamp_ref[...]
    lam = (jnp.exp(jnp.sum(lp[0:1] * lp[1:2], axis=-1, keepdims=True))
           - jnp.exp(jnp.sum(lp[2:3] * lp[3:4], axis=-1, keepdims=True)) + lam0)
    q = q_ref[...] * (QK_DIM ** -0.5)
    lane = lax.broadcasted_iota(jnp.int32, q.shape, 1)
    qq = jnp.concatenate([jnp.where(lane < QK_DIM, q, 0.0), jnp.where(lane >= QK_DIM, q, 0.0)],
                         axis=0).astype(BF16)

    def body(c, carry):
        m, l, acc = carry
        start = pl.multiple_of(c * tk, tk)
        kc = k_ref[pl.ds(start, tk), :]
        vc = v_ref[pl.ds(start, tk), :]
        s = lax.dot_general(qq, kc, (((1,), (1,)), ((), ())), preferred_element_type=F32)
        m_new = jnp.maximum(m, jnp.max(s, axis=-1, keepdims=True))
        alpha = jnp.exp(m - m_new)
        p = jnp.exp(s - m_new)
        l = alpha * l + jnp.sum(p, axis=-1, keepdims=True)
        acc = alpha * acc + jnp.dot(p.astype(BF16), vc, preferred_element_type=F32)
        return m_new, l, acc

    init = (jnp.full((2 * tq, 1), -jnp.inf, F32), jnp.zeros((2 * tq, 1), F32),
            jnp.zeros((2 * tq, V_DIM), F32))
    _, l, acc = lax.fori_loop(0, n_kv, body, init)
    o = acc / l
    o = o[0:tq] - lam * o[tq:2 * tq]
    o = o * lax.rsqrt(jnp.mean(o * o, axis=-1, keepdims=True) + EPS)
    o_ref[...] = o * g_ref[...] * (1.0 - lam0)


def _attention(q, k, v, lamp, g, lam0, tq, tk):
    b, t, _ = q.shape
    t_kv = k.shape[1]
    return pl.pallas_call(
        functools.partial(_attn_kernel, tk, lam0),
        grid=(b, ATT_HEADS, t // tq),
        in_specs=[pl.BlockSpec((None, tq, LANES), lambda bi, h, i: (bi, i, h)),
                  pl.BlockSpec((None, t_kv, LANES), lambda bi, h, i: (bi, 0, h)),
                  pl.BlockSpec((None, t_kv, LANES), lambda bi, h, i: (bi, 0, h)),
                  pl.BlockSpec((4, QK_DIM), lambda bi, h, i: (0, 0)),
                  pl.BlockSpec((1, V_DIM), lambda bi, h, i: (0, 0))],
        out_specs=pl.BlockSpec((None, tq, LANES), lambda bi, h, i: (bi, i, h)),
        out_shape=jax.ShapeDtypeStruct((b, t, D_ATT), F32),
        compiler_params=_cparams(("parallel", "parallel", "arbitrary")),
        name="diff_attn",
    )(q, k, v, lamp, g)


def _expand_heads(v):
    lane = lax.broadcasted_iota(jnp.int32, (v.shape[0], LANES), 1)
    parts = []
    for pr in range(SSM_HEADS // 2):
        a = jnp.broadcast_to(v[:, 2 * pr:2 * pr + 1], (v.shape[0], LANES))
        b = jnp.broadcast_to(v[:, 2 * pr + 1:2 * pr + 2], (v.shape[0], LANES))
        parts.append(jnp.where(lane < SSM_HEADDIM, a, b))
    return jnp.concatenate(parts, axis=1)


def _ssd_kernel(reverse, nc, xc_ref, xp_ref, xn_ref, dt_ref, h0_ref, cw_ref, cb_ref, alog_ref, dtb_ref,
                *rest):
    if reverse:
        yf_ref, z_ref, d_ref, ng_ref, y_ref, hout_ref, state_ref = rest
    else:
        y_ref, hout_ref, state_ref = rest
    c = pl.program_id(1)
    chunk = (nc - 1 - c) if reverse else c

    @pl.when(c == 0)
    def _():
        state_ref[...] = h0_ref[...]

    prev = jnp.where(chunk > 0, xp_ref[...], 0.0)
    nxt = jnp.where(chunk < nc - 1, xn_ref[...], 0.0)
    xpad = jnp.concatenate([prev, xc_ref[...], nxt], axis=0)
    cw = cw_ref[...]
    conv = cb_ref[...]
    for kk in range(CONV_W):
        off = HALO - CONV_W // 2 + kk
        conv = conv + xpad[off:off + CHUNK, :] * cw[kk:kk + 1, :]
    xbc = _silu(conv)
    xs = xbc[:, 0:D_SSM]
    bm = xbc[:, D_SSM:D_SSM + SSM_GROUPS * D_STATE].astype(BF16)
    cm = xbc[:, D_SSM + SSM_GROUPS * D_STATE:XBC_DIM].astype(BF16)

    dt_raw = dt_ref[...]
    if reverse:
        dt_raw = pltpu.roll(dt_raw, LANES - SSM_HEADS, 1)
    xv = dt_raw + dtb_ref[...]
    dt = jnp.maximum(xv, 0.0) + jnp.log(1.0 + jnp.exp(-jnp.abs(xv)))
    la = dt * (-jnp.exp(alog_ref[...]))
    row = lax.broadcasted_iota(jnp.int32, (CHUNK, LANES), 0)
    col = lax.broadcasted_iota(jnp.int32, (CHUNK, LANES), 1)
    cum = la
    sh = 1
    while sh < CHUNK:
        if reverse:
            cum = cum + jnp.where(row < CHUNK - sh, pltpu.roll(cum, CHUNK - sh, 0), 0.0)
        else:
            cum = cum + jnp.where(row >= sh, pltpu.roll(cum, sh, 0), 0.0)
        sh *= 2
    end = CHUNK - 1 if not reverse else 0
    cum_end = cum[end:end + 1, :]
    cum_t = cum.T
    causal = (row <= col) if reverse else (row >= col)

    dt_x = _expand_heads(dt)
    xd = xs * dt_x
    xdw = (xd * _expand_heads(jnp.exp(cum_end - cum))).astype(BF16)
    xd = xd.astype(BF16)
    ecum_x = _expand_heads(jnp.exp(cum))
    lane = col
    rep = SSM_HEADS // SSM_GROUPS
    state = state_ref[...]
    y_parts = []
    new_state = []
    for g in range(SSM_GROUPS):
        bg = bm[:, g * D_STATE:(g + 1) * D_STATE]
        cg = cm[:, g * D_STATE:(g + 1) * D_STATE]
        cbt = lax.dot_general(cg, bg, (((1,), (1,)), ((), ())), preferred_element_type=F32)
        rows = slice(g * rep * SSM_HEADDIM, (g + 1) * rep * SSM_HEADDIM)
        st_g = state[rows, :]
        y_off = lax.dot_general(cg, st_g.astype(BF16), (((1,), (1,)), ((), ())),
                                preferred_element_type=F32)
        cst = lax.dot_general(xdw[:, rows], bg, (((0,), (0,)), ((), ())), preferred_element_type=F32)
        dec = jnp.exp(cum_end)
        for pr in range(rep // 2):
            y_pair = None
            for sub in range(2):
                h = g * rep + 2 * pr + sub
                seg = cum[:, h:h + 1] - cum_t[h:h + 1, :]
                decay = jnp.exp(jnp.where(causal, seg, -jnp.inf))
                sc = (cbt * decay).astype(BF16)
                xd_pair = xd[:, (h // 2) * LANES:(h // 2 + 1) * LANES]
                keep = (lane < SSM_HEADDIM) if sub == 0 else (lane >= SSM_HEADDIM)
                t = jnp.dot(sc, jnp.where(keep, xd_pair, jnp.zeros_like(xd_pair)), preferred_element_type=F32)
                y_pair = t if y_pair is None else y_pair + t
            y_parts.append(y_pair)
        dec_rows = jnp.concatenate(
            [jnp.broadcast_to(dec[:, g * rep + hh:g * rep + hh + 1], (SSM_HEADDIM, D_STATE)) for hh in range(rep)],
            axis=0)
        new_state.append(st_g * dec_rows + cst)
        y_parts[-2] = y_parts[-2] + y_off[:, 0:LANES] * ecum_x[:, rows][:, 0:LANES]
        y_parts[-1] = y_parts[-1] + y_off[:, LANES:2 * LANES] * ecum_x[:, rows][:, LANES:2 * LANES]
    y = jnp.concatenate(y_parts, axis=1)
    state = jnp.concatenate(new_state, axis=0)
    state_ref[...] = state

    @pl.when(c == nc - 1)
    def _():
        hout_ref[...] = state

    if reverse:
        y = y + yf_ref[...] + xs * d_ref[...]
        y = y * _silu(z_ref[...])
        y = y * lax.rsqrt(jnp.mean(y * y, axis=-1, keepdims=True) + EPS)
        y_ref[...] = y * ng_ref[...]
    else:
        y_ref[...] = y


def _ssd(reverse, xbc, dt, h0, conv_w, conv_b, alog, dtb, extra):
    b, l, _ = xbc.shape
    nc = l // CHUNK
    hb = CHUNK // HALO

    def ch(ci):
        return (nc - 1 - ci) if reverse else ci

    in_specs = [pl.BlockSpec((None, CHUNK, XBC_DIM), lambda bi, ci: (bi, ch(ci), 0)),
                pl.BlockSpec((None, HALO, XBC_DIM), lambda bi, ci: (bi, jnp.maximum(ch(ci) * hb - 1, 0), 0)),
                pl.BlockSpec((None, HALO, XBC_DIM),
                             lambda bi, ci: (bi, jnp.minimum((ch(ci) + 1) * hb, l // HALO - 1), 0)),
                pl.BlockSpec((None, CHUNK, LANES), lambda bi, ci: (bi, ch(ci), 0)),
                pl.BlockSpec((None, SSM_HEADS * SSM_HEADDIM, D_STATE), lambda bi, ci: (bi, 0, 0)),
                pl.BlockSpec((CONV_W, XBC_DIM), lambda bi, ci: (0, 0)),
                pl.BlockSpec((1, XBC_DIM), lambda bi, ci: (0, 0)),
                pl.BlockSpec((1, LANES), lambda bi, ci: (0, 0)),
                pl.BlockSpec((1, LANES), lambda bi, ci: (0, 0))]
    args = [xbc, xbc, xbc, dt, h0, conv_w, conv_b, alog, dtb]
    if reverse:
        y_fwd, z, d_x, norm_g = extra
        in_specs += [pl.BlockSpec((None, CHUNK, D_SSM), lambda bi, ci: (bi, ch(ci), 0)),
                     pl.BlockSpec((None, CHUNK, D_SSM), lambda bi, ci: (bi, ch(ci), 0)),
                     pl.BlockSpec((1, D_SSM), lambda bi, ci: (0, 0)),
                     pl.BlockSpec((1, D_SSM), lambda bi, ci: (0, 0))]
        args += [y_fwd, z, d_x, norm_g]
    return pl.pallas_call(
        functools.partial(_ssd_kernel, reverse, nc),
        grid=(b, nc),
        in_specs=in_specs,
        out_specs=[pl.BlockSpec((None, CHUNK, D_SSM), lambda bi, ci: (bi, ch(ci), 0)),
                   pl.BlockSpec((None, SSM_HEADS * SSM_HEADDIM, D_STATE), lambda bi, ci: (bi, 0, 0))],
        out_shape=[jax.ShapeDtypeStruct((b, l, D_SSM), F32),
                   jax.ShapeDtypeStruct((b, SSM_HEADS * SSM_HEADDIM, D_STATE), F32)],
        scratch_shapes=[pltpu.VMEM((SSM_HEADS * SSM_HEADDIM, D_STATE), F32)],
        compiler_params=_cparams(("parallel", "arbitrary")),
        name="ssd_bwd" if reverse else "ssd_fwd",
    )(*args)


def _outproj_kernel(att_ref, ssm_ref, x_ref, mods_ref, wa_ref, ws_ref, g_ref, wr_ref, br_ref,
                    x1_ref, h2_ref, cw_ref):
    mix = (jnp.dot(att_ref[...].astype(BF16), wa_ref[...], preferred_element_type=F32)
           + jnp.dot(ssm_ref[...].astype(BF16), ws_ref[...], preferred_element_type=F32))
    gate1 = mods_ref[:, 2 * D_MODEL:3 * D_MODEL]
    shift2 = mods_ref[:, 3 * D_MODEL:4 * D_MODEL]
    scale2 = mods_ref[:, 4 * D_MODEL:5 * D_MODEL]
    x1 = x_ref[...] + gate1 * mix
    x1_ref[...] = x1
    y = x1 * lax.rsqrt(jnp.mean(x1 * x1, axis=-1, keepdims=True) + EPS) * g_ref[...]
    h2 = y * (1.0 + scale2) + shift2
    h_hi = h2.astype(BF16)
    h2_ref[...] = h_hi
    h_lo = (h2 - h_hi.astype(F32)).astype(BF16)
    w_hi = wr_ref[0]
    w_lo = wr_ref[1]
    logits = (jnp.dot(h_hi, w_hi, preferred_element_type=F32) + jnp.dot(h_lo, w_hi, preferred_element_type=F32)
              + jnp.dot(h_hi, w_lo, preferred_element_type=F32)) + br_ref[...]
    lane = lax.broadcasted_iota(jnp.int32, logits.shape, 1)
    neg = -jnp.inf
    big = jnp.int32(1 << 20)
    is_g = lane < N_GROUPS
    gl = jnp.where(is_g, logits, neg)
    gmax = jnp.max(gl, axis=-1, keepdims=True)
    g_idx = jnp.min(jnp.where(gl == gmax, lane, big), axis=-1, keepdims=True)
    p_g = 1.0 / jnp.sum(jnp.where(is_g, jnp.exp(gl - gmax), 0.0), axis=-1, keepdims=True)
    e_lo = N_GROUPS + g_idx * EXPERTS_PER_GROUP
    in_grp = (lane >= e_lo) & (lane < e_lo + EXPERTS_PER_GROUP)
    el = jnp.where(in_grp, logits, neg)
    m1 = jnp.max(el, axis=-1, keepdims=True)
    i1 = jnp.min(jnp.where(el == m1, lane, big), axis=-1, keepdims=True)
    el2 = jnp.where(lane == i1, neg, el)
    m2 = jnp.max(el2, axis=-1, keepdims=True)
    i2 = jnp.min(jnp.where(el2 == m2, lane, big), axis=-1, keepdims=True)
    e2 = jnp.exp(m2 - m1)
    w1 = p_g / (1.0 + e2)
    w2 = p_g * e2 / (1.0 + e2)
    cw = jnp.where(lane == i1, w1, 0.0) + jnp.where(lane == i2, w2, 0.0)
    cw_ref[...] = pltpu.roll(cw, LANES - N_GROUPS, 1)


def _outproj(att, ssm, x, mods3, mod_row0, mod_tokens, wo_att, wo_ssm, g, w_router, b_router):
    n = x.shape[0]
    tm = 256
    per_mod = mod_tokens // tm
    return pl.pallas_call(
        _outproj_kernel,
        grid=(n // tm,),
        in_specs=[pl.BlockSpec((tm, D_ATT), lambda i: (i, 0)),
                  pl.BlockSpec((tm, D_SSM), lambda i: (i, 0)),
                  pl.BlockSpec((tm, D_MODEL), lambda i: (i, 0)),
                  pl.BlockSpec((None, 1, 6 * D_MODEL), lambda i: (mod_row0 + i // per_mod, 0, 0)),
                  pl.BlockSpec((D_ATT, D_MODEL), lambda i: (0, 0)),
                  pl.BlockSpec((D_SSM, D_MODEL), lambda i: (0, 0)),
                  pl.BlockSpec((1, D_MODEL), lambda i: (0, 0)),
                  pl.BlockSpec((2, D_MODEL, LANES), lambda i: (0, 0, 0)),
                  pl.BlockSpec((1, LANES), lambda i: (0, 0))],
        out_specs=[pl.BlockSpec((tm, D_MODEL), lambda i: (i, 0)),
                   pl.BlockSpec((tm, D_MODEL), lambda i: (i, 0)),
                   pl.BlockSpec((tm, LANES), lambda i: (i, 0))],
        out_shape=[jax.ShapeDtypeStruct((n, D_MODEL), F32),
                   jax.ShapeDtypeStruct((n, D_MODEL), BF16),
                   jax.ShapeDtypeStruct((n, LANES), F32)],
        compiler_params=_cparams(("parallel",)),
        name="outproj_router",
    )(att, ssm, x, mods3, wo_att, wo_ssm, g, w_router, b_router)


def _moe_kernel(h_ref, cw_ref, x1_ref, mods_ref, wg_ref, wu_ref, wd_ref, fg_ref, o_ref, acc_ref):
    e = pl.program_id(1)

    @pl.when(e == 0)
    def _():
        acc_ref[...] = jnp.zeros_like(acc_ref)

    h = h_ref[...]
    a = jnp.dot(h, wg_ref[...], preferred_element_type=F32)
    u = jnp.dot(h, wu_ref[...], preferred_element_type=F32)
    lane = lax.broadcasted_iota(jnp.int32, cw_ref.shape, 1)
    cw_e = jnp.sum(jnp.where(lane == e, cw_ref[...], 0.0), axis=-1, keepdims=True)
    hid = (_silu(a) * u * cw_e).astype(BF16)
    acc_ref[...] += jnp.dot(hid, wd_ref[...], preferred_element_type=F32)

    @pl.when(e == N_EXPERTS - 1)
    def _():
        gate2 = mods_ref[:, 5 * D_MODEL:6 * D_MODEL]
        x2 = x1_ref[...] + gate2 * acc_ref[...]
        o_ref[...] = x2 * lax.rsqrt(jnp.mean(x2 * x2, axis=-1, keepdims=True) + EPS) * fg_ref[...]


def _moe(h2, cw, x1, mods3, mod_row0, mod_tokens, wg, wu, wd, fg):
    n = h2.shape[0]
    tm = 512
    per_mod = mod_tokens // tm
    return pl.pallas_call(
        _moe_kernel,
        grid=(n // tm, N_EXPERTS),
        in_specs=[pl.BlockSpec((tm, D_MODEL), lambda i, e: (i, 0)),
                  pl.BlockSpec((tm, LANES), lambda i, e: (i, 0)),
                  pl.BlockSpec((tm, D_MODEL), lambda i, e: (i, 0)),
                  pl.BlockSpec((None, 1, 6 * D_MODEL), lambda i, e: (mod_row0 + i // per_mod, 0, 0)),
                  pl.BlockSpec((None, D_MODEL, EXPERT_FF), lambda i, e: (e, 0, 0)),
                  pl.BlockSpec((None, D_MODEL, EXPERT_FF), lambda i, e: (e, 0, 0)),
                  pl.BlockSpec((None, EXPERT_FF, D_MODEL), lambda i, e: (e, 0, 0)),
                  pl.BlockSpec((1, D_MODEL), lambda i, e: (0, 0))],
        out_specs=pl.BlockSpec((tm, D_MODEL), lambda i, e: (i, 0)),
        out_shape=jax.ShapeDtypeStruct((n, D_MODEL), F32),
        scratch_shapes=[pltpu.VMEM((tm, D_MODEL), F32)],
        compiler_params=_cparams(("parallel", "arbitrary")),
        name="moe",
    )(h2, cw, x1, mods3, wg, wu, wd, fg)


def _rope_tables(t):
    n_freq = QK_DIM // 4
    freqs = ROPE_BASE ** (-jnp.arange(n_freq, dtype=F32) / n_freq)
    pos = jnp.arange(t)
    row = (pos // GRID_W).astype(F32)
    col = (pos % GRID_W).astype(F32)
    j = jnp.arange(LANES) % QK_DIM
    ang = jnp.where((j < QK_DIM // 2)[None, :], row[:, None], col[:, None]) * freqs[j % n_freq][None, :]
    first = ((j % (QK_DIM // 2)) < n_freq)[None, :]
    sin = jnp.sin(ang)
    return jnp.cos(ang), jnp.where(first, -sin, 0.0), jnp.where(first, 0.0, sin)


def _layer(x, mods3, mod_row0, mod_tokens, rope_tabs, ctx_k, ctx_v, h0f, h0b, lw, layer):
    b, t, _ = x.shape
    n = b * t
    xf = x.reshape(n, D_MODEL)
    q, k, v, z, xbc, dt = _inproj(xf, mods3, mod_row0, mod_tokens, t, lw["norm_mix_g"], lw["w_main"],
                                  lw["w_dt"], rope_tabs)
    k3 = k.reshape(b, t, D_QK)
    v3 = v.reshape(b, t, D_ATT)
    if ctx_k is None:
        k_all, v_all = k3.astype(BF16), v3.astype(BF16)
    else:
        k_all = jnp.concatenate([k3.astype(BF16), ctx_k.astype(BF16)], axis=1)
        v_all = jnp.concatenate([v3.astype(BF16), ctx_v.astype(BF16)], axis=1)
    lam0 = 0.8 - 0.6 * math.exp(-0.3 * layer)
    tq = 256
    tk = 512 if k_all.shape[1] % 512 == 0 else 256
    att = _attention(q.reshape(b, t, D_QK), k_all, v_all, lw["lamp"], lw["attn_subln_g"], lam0, tq, tk)
    xbc3 = xbc.reshape(b, t, XBC_DIM)
    dt3 = dt.reshape(b, t, LANES)
    y_f, hf = _ssd(False, xbc3, dt3, h0f, lw["conv_w"], lw["conv_b"], lw["alog_f"], lw["dtb_f"], None)
    ssm, hb = _ssd(True, xbc3, dt3, h0b, lw["conv_w"], lw["conv_b"], lw["alog_b"], lw["dtb_b"],
                   (y_f, z.reshape(b, t, D_SSM), lw["d_x"], lw["ssm_norm_g"]))
    x1, h2, cw = _outproj(att.reshape(n, D_ATT), ssm.reshape(n, D_SSM), xf, mods3, mod_row0, mod_tokens,
                          lw["wo_att"], lw["wo_ssm"], lw["norm_ffn_g"], lw["w_router"], lw["b_router"])
    y = _moe(h2, cw, x1, mods3, mod_row0, mod_tokens, lw["wg"], lw["wu"], lw["wd"], lw["final_g"])
    return y.reshape(b, t, D_MODEL), k3, v3, hf, hb


def _pad_lanes(v, width=LANES):
    return jnp.pad(v, [(0, 0)] * (v.ndim - 1) + [(0, width - v.shape[-1])])


def kernel(x_prompt, x_sample, cache_k, cache_v, state_ssm_fwd, state_ssm_bwd, c, c_ctx, w_ada, b_ada, norm_mix_g, w_in, w_out, lambda_q1, lambda_k1, lambda_q2, lambda_k2, attn_subln_g, conv_w, conv_b, a_log_fwd, a_log_bwd, dt_bias_fwd, dt_bias_bwd, ssm_d, ssm_norm_g, norm_ffn_g, w_group_router, b_group_router, w_expert_router, b_expert_router, w_exp_gate, w_exp_up, w_exp_down, final_norm_g):
    depth = w_in.shape[0]
    assert depth == 1, "single trunk layer"
    bp, tp, _ = x_prompt.shape
    bs, ts, _ = x_sample.shape
    l = 0
    cond = jnp.concatenate([c_ctx[None], c], axis=0)
    condT = _pad_lanes(cond.T, 8)
    mods = _ada(condT, w_ada[l], b_ada[l][None])
    mods3 = mods.reshape(8, 1, 6 * D_MODEL)

    w_router = _pad_lanes(jnp.concatenate([w_group_router[l], w_expert_router[l]], axis=1))
    wr_hi = w_router.astype(BF16)
    wr_lo = (w_router - wr_hi.astype(F32)).astype(BF16)
    lw = dict(
        norm_mix_g=norm_mix_g[l][None],
        w_main=w_in[l][:, :MAIN_COLS].astype(BF16),
        w_dt=_pad_lanes(w_in[l][:, MAIN_COLS:]).astype(BF16),
        lamp=jnp.stack([lambda_q1[l], lambda_k1[l], lambda_q2[l], lambda_k2[l]]),
        attn_subln_g=attn_subln_g[l][None],
        conv_w=conv_w[l], conv_b=conv_b[l][None],
        alog_f=_pad_lanes(a_log_fwd[l][None]), alog_b=_pad_lanes(a_log_bwd[l][None]),
        dtb_f=_pad_lanes(dt_bias_fwd[l][None]), dtb_b=_pad_lanes(dt_bias_bwd[l][None]),
        d_x=jnp.repeat(ssm_d[l], SSM_HEADDIM)[None], ssm_norm_g=ssm_norm_g[l][None],
        wo_att=w_out[l][:D_ATT].astype(BF16), wo_ssm=w_out[l][D_ATT:].astype(BF16),
        norm_ffn_g=norm_ffn_g[l][None],
        w_router=jnp.stack([wr_hi, wr_lo]),
        b_router=_pad_lanes(jnp.concatenate([b_group_router[l], b_expert_router[l]])[None]),
        wg=w_exp_gate[l].astype(BF16), wu=w_exp_up[l].astype(BF16), wd=w_exp_down[l].astype(BF16),
        final_g=final_norm_g[None],
    )
    n_state = SSM_HEADS * SSM_HEADDIM
    zeros_state = jnp.zeros((bp, n_state, D_STATE), F32)
    yp, ck, cv, hf, hb = _layer(x_prompt, mods3, 0, bp * tp, None, None, None, zeros_state, zeros_state, lw, l)
    ys, _, _, _, _ = _layer(x_sample, mods3, 1, ts, _rope_tables(ts),
                            cache_k[:, l].reshape(bs, -1, D_QK), cache_v[:, l].reshape(bs, -1, D_ATT),
                            state_ssm_fwd[:, l].reshape(bs, n_state, D_STATE),
                            state_ssm_bwd[:, l].reshape(bs, n_state, D_STATE), lw, l)
    new_k = ck.reshape(bp, 1, tp, ATT_HEADS, 2, QK_DIM)
    new_v = cv.reshape(bp, 1, tp, ATT_HEADS, V_DIM)
    new_hf = hf.reshape(bp, 1, SSM_HEADS, SSM_HEADDIM, D_STATE)
    new_hb = hb.reshape(bp, 1, SSM_HEADS, SSM_HEADDIM, D_STATE)
    return yp, ys, new_k, new_v, new_hf, new_hb
```

```python
import functools
import math

import jax
import jax.numpy as jnp
from jax import lax
from jax.experimental import pallas as pl
from jax.experimental.pallas import tpu as pltpu

D_MODEL = 1024
GRID_W = 64
ATT_HEADS = 4
QK_DIM = 64
V_DIM = 128
D_QK = 512
D_ATT = 512
ROPE_BASE = 10000.0
D_SSM = 512
SSM_HEADDIM = 64
SSM_HEADS = 8
SSM_GROUPS = 2
D_STATE = 128
CONV_W = 5
CHUNK = 128
XBC_DIM = 1024
N_GROUPS = 4
EXPERTS_PER_GROUP = 4
N_EXPERTS = 16
EXPERT_FF = 256
EPS = 1e-6
MAIN_COLS = 2 * D_QK + D_ATT + D_SSM + XBC_DIM
H2X_W = D_MODEL + 128
MOE_TILE = 2048
MOE_CHUNK = 256
MOE_SLOTS = MOE_TILE // MOE_CHUNK + N_GROUPS
MOE_ROWS = MOE_SLOTS * MOE_CHUNK
LANES = 128
HALO = 8
VMEM_LIMIT = 56 * 1024 * 1024

F32 = jnp.float32
BF16 = jnp.bfloat16


def _cparams(sem):
    return pltpu.CompilerParams(dimension_semantics=sem, vmem_limit_bytes=VMEM_LIMIT)


def _sigmoid(x):
    return 1.0 / (1.0 + jnp.exp(-x))


def _silu(x):
    return x * _sigmoid(x)


def _ada_kernel(condT_ref, w_ref, b_ref, o_ref):
    s = _silu(condT_ref[...])
    w = w_ref[...]
    b = b_ref[...]
    o_ref[...] = jnp.zeros_like(o_ref)
    for r in range(3):
        o_ref[r:r + 1, :] = jnp.sum(w * s[:, r:r + 1], axis=0, keepdims=True) + b


def _ada(condT, w_ada, b_ada):
    bn = 512
    n = w_ada.shape[1]
    return pl.pallas_call(
        _ada_kernel,
        grid=(n // bn,),
        in_specs=[pl.BlockSpec((D_MODEL, 8), lambda j: (0, 0)),
                  pl.BlockSpec((D_MODEL, bn), lambda j: (0, j)),
                  pl.BlockSpec((1, bn), lambda j: (0, j))],
        out_specs=pl.BlockSpec((8, bn), lambda j: (0, j)),
        out_shape=jax.ShapeDtypeStruct((8, n), F32),
        compiler_params=_cparams(("arbitrary",)),
        name="ada",
    )(condT, w_ada, b_ada)


def _inproj_kernel(rope, x_ref, mods_ref, g_ref, w_ref, wdt_ref, *rest):
    if rope:
        cos_ref, sa_ref, sb_ref, q_ref, k_ref, v_ref, z_ref, xbc_ref, dt_ref = rest
    else:
        q_ref, k_ref, v_ref, z_ref, xbc_ref, dt_ref = rest
    x = x_ref[...]
    shift = mods_ref[:, 0:D_MODEL]
    scale = mods_ref[:, D_MODEL:2 * D_MODEL]
    y = x * lax.rsqrt(jnp.mean(x * x, axis=-1, keepdims=True) + EPS) * g_ref[...]
    h = (y * (1.0 + scale) + shift).astype(BF16)
    r = jnp.dot(h, w_ref[...], preferred_element_type=F32)
    dt_ref[...] = jnp.dot(h, wdt_ref[...], preferred_element_type=F32)
    q = r[:, 0:D_QK]
    k = r[:, D_QK:2 * D_QK]
    if rope:
        cos = cos_ref[...]
        sa = sa_ref[...]
        sb = sb_ref[...]

        def rot(t):
            parts = []
            for hh in range(ATT_HEADS):
                th = t[:, hh * LANES:(hh + 1) * LANES]
                parts.append(th * cos + pltpu.roll(th, LANES - 16, 1) * sa + pltpu.roll(th, 16, 1) * sb)
            return jnp.concatenate(parts, axis=1)

        q = rot(q)
        k = rot(k)
    q_ref[...] = q
    k_ref[...] = k
    v_ref[...] = r[:, 2 * D_QK:2 * D_QK + D_ATT]
    z_ref[...] = r[:, 2 * D_QK + D_ATT:2 * D_QK + D_ATT + D_SSM]
    xbc_ref[...] = r[:, 2 * D_QK + D_ATT + D_SSM:MAIN_COLS]


def _inproj(x, mods3, mod_row0, mod_tokens, seq_len, g, w_main, w_dt, rope_tabs):
    n = x.shape[0]
    tm = 256
    per_seq = seq_len // tm
    per_mod = mod_tokens // tm
    rope = rope_tabs is not None
    in_specs = [pl.BlockSpec((tm, D_MODEL), lambda i: (i, 0)),
                pl.BlockSpec((None, 1, 6 * D_MODEL), lambda i: (mod_row0 + i // per_mod, 0, 0)),
                pl.BlockSpec((1, D_MODEL), lambda i: (0, 0)),
                pl.BlockSpec((D_MODEL, MAIN_COLS), lambda i: (0, 0)),
                pl.BlockSpec((D_MODEL, LANES), lambda i: (0, 0))]
    args = [x, mods3, g, w_main, w_dt]
    if rope:
        tab_spec = pl.BlockSpec((tm, LANES), lambda i: (i % per_seq, 0))
        in_specs += [tab_spec] * 3
        args += list(rope_tabs)
    widths = (D_QK, D_QK, D_ATT, D_SSM, XBC_DIM, LANES)
    return pl.pallas_call(
        functools.partial(_inproj_kernel, rope),
        grid=(n // tm,),
        in_specs=in_specs,
        out_specs=[pl.BlockSpec((tm, wd), lambda i: (i, 0)) for wd in widths],
        out_shape=[jax.ShapeDtypeStruct((n, wd), F32) for wd in widths],
        compiler_params=_cparams(("parallel",)),
        name="inproj_rope" if rope else "inproj",
    )(*args)


def _attn_kernel(tk, lam0, q_ref, k_ref, v_ref, lamp_ref, g_ref, o_ref):
    tq = q_ref.shape[0]
    n_kv = k_ref.shape[0] // tk
    lp = lamp_ref[...]
    lam = (jnp.exp(jnp.sum(lp[0:1] * lp[1:2], axis=-1, keepdims=True))
           - jnp.exp(jnp.sum(lp[2:3] * lp[3:4], axis=-1, keepdims=True)) + lam0)
    q = q_ref[...] * (QK_DIM ** -0.5)
    lane = lax.broadcasted_iota(jnp.int32, q.shape, 1)
    qq = jnp.concatenate([jnp.where(lane < QK_DIM, q, 0.0), jnp.where(lane >= QK_DIM, q, 0.0)],
                         axis=0).astype(BF16)

    def body(c, carry):
        m, l, acc = carry
        start = pl.multiple_of(c * tk, tk)
        kc = k_ref[pl.ds(start, tk), :]
        vc = v_ref[pl.ds(start, tk), :]
        s = lax.dot_general(qq, kc, (((1,), (1,)), ((), ())), preferred_element_type=F32)
        m_new = jnp.maximum(m, jnp.max(s, axis=-1, keepdims=True))
        alpha = jnp.exp(m - m_new)
        p = jnp.exp(s - m_new)
        l = alpha * l + jnp.sum(p, axis=-1, keepdims=True)
        acc = alpha * acc + jnp.dot(p.astype(BF16), vc, preferred_element_type=F32)
        return m_new, l, acc

    init = (jnp.full((2 * tq, 1), -jnp.inf, F32), jnp.zeros((2 * tq, 1), F32),
            jnp.zeros((2 * tq, V_DIM), F32))
    _, l, acc = lax.fori_loop(0, n_kv, body, init)
    o = acc / l
    o = o[0:tq] - lam * o[tq:2 * tq]
    o = o * lax.rsqrt(jnp.mean(o * o, axis=-1, keepdims=True) + EPS)
    o_ref[...] = o * g_ref[...] * (1.0 - lam0)


def _attention(q, k, v, lamp, g, lam0, tq, tk):
    b, t, _ = q.shape
    t_kv = k.shape[1]
    return pl.pallas_call(
        functools.partial(_attn_kernel, tk, lam0),
        grid=(b, ATT_HEADS, t // tq),
        in_specs=[pl.BlockSpec((None, tq, LANES), lambda bi, h, i: (bi, i, h)),
                  pl.BlockSpec((None, t_kv, LANES), lambda bi, h, i: (bi, 0, h)),
                  pl.BlockSpec((None, t_kv, LANES), lambda bi, h, i: (bi, 0, h)),
                  pl.BlockSpec((4, QK_DIM), lambda bi, h, i: (0, 0)),
                  pl.BlockSpec((1, V_DIM), lambda bi, h, i: (0, 0))],
        out_specs=pl.BlockSpec((None, tq, LANES), lambda bi, h, i: (bi, i, h)),
        out_shape=jax.ShapeDtypeStruct((b, t, D_ATT), F32),
        compiler_params=_cparams(("parallel", "parallel", "arbitrary")),
        name="diff_attn",
    )(q, k, v, lamp, g)


def _expand_heads(v):
    lane = lax.broadcasted_iota(jnp.int32, (v.shape[0], LANES), 1)
    parts = []
    for pr in range(SSM_HEADS // 2):
        a = jnp.broadcast_to(v[:, 2 * pr:2 * pr + 1], (v.shape[0], LANES))
        b = jnp.broadcast_to(v[:, 2 * pr + 1:2 * pr + 2], (v.shape[0], LANES))
        parts.append(jnp.where(lane < SSM_HEADDIM, a, b))
    return jnp.concatenate(parts, axis=1)


def _ssd_kernel(reverse, nc, xc_ref, xp_ref, xn_ref, dt_ref, h0_ref, cw_ref, cb_ref, alog_ref, dtb_ref,
                *rest):
    if reverse:
        yf_ref, z_ref, d_ref, ng_ref, y_ref, hout_ref, state_ref = rest
    else:
        y_ref, hout_ref, state_ref = rest
    c = pl.program_id(1)
    chunk = (nc - 1 - c) if reverse else c

    @pl.when(c == 0)
    def _():
        state_ref[...] = h0_ref[...]

    prev = jnp.where(chunk > 0, xp_ref[...], 0.0)
    nxt = jnp.where(chunk < nc - 1, xn_ref[...], 0.0)
    xpad = jnp.concatenate([prev, xc_ref[...], nxt], axis=0)
    cw = cw_ref[...]
    conv = cb_ref[...]
    for kk in range(CONV_W):
        off = HALO - CONV_W // 2 + kk
        conv = conv + xpad[off:off + CHUNK, :] * cw[kk:kk + 1, :]
    xbc = _silu(conv)
    xs = xbc[:, 0:D_SSM]
    bm = xbc[:, D_SSM:D_SSM + SSM_GROUPS * D_STATE].astype(BF16)
    cm = xbc[:, D_SSM + SSM_GROUPS * D_STATE:XBC_DIM].astype(BF16)

    dt_raw = dt_ref[...]
    if reverse:
        dt_raw = pltpu.roll(dt_raw, LANES - SSM_HEADS, 1)
    xv = dt_raw + dtb_ref[...]
    dt = jnp.maximum(xv, 0.0) + jnp.log(1.0 + jnp.exp(-jnp.abs(xv)))
    la = dt * (-jnp.exp(alog_ref[...]))
    row = lax.broadcasted_iota(jnp.int32, (CHUNK, LANES), 0)
    col = lax.broadcasted_iota(jnp.int32, (CHUNK, LANES), 1)
    cum = la
    sh = 1
    while sh < CHUNK:
        if reverse:
            cum = cum + jnp.where(row < CHUNK - sh, pltpu.roll(cum, CHUNK - sh, 0), 0.0)
        else:
            cum = cum + jnp.where(row >= sh, pltpu.roll(cum, sh, 0), 0.0)
        sh *= 2
    end = CHUNK - 1 if not reverse else 0
    cum_end = cum[end:end + 1, :]
    cum_t = cum.T
    causal = (row <= col) if reverse else (row >= col)

    dt_x = _expand_heads(dt)
    xd = xs * dt_x
    xdw = (xd * _expand_heads(jnp.exp(cum_end - cum))).astype(BF16)
    xd = xd.astype(BF16)
    ecum_x = _expand_heads(jnp.exp(cum))
    lane = col
    rep = SSM_HEADS // SSM_GROUPS
    state = state_ref[...]
    y_parts = []
    new_state = []
    for g in range(SSM_GROUPS):
        bg = bm[:, g * D_STATE:(g + 1) * D_STATE]
        cg = cm[:, g * D_STATE:(g + 1) * D_STATE]
        cbt = lax.dot_general(cg, bg, (((1,), (1,)), ((), ())), preferred_element_type=F32)
        rows = slice(g * rep * SSM_HEADDIM, (g + 1) * rep * SSM_HEADDIM)
        st_g = state[rows, :]
        y_off = lax.dot_general(cg, st_g.astype(BF16), (((1,), (1,)), ((), ())),
                                preferred_element_type=F32)
        cst = lax.dot_general(xdw[:, rows], bg, (((0,), (0,)), ((), ())), preferred_element_type=F32)
        dec = jnp.exp(cum_end)
        for pr in range(rep // 2):
            y_pair = None
            for sub in range(2):
                h = g * rep + 2 * pr + sub
                seg = cum[:, h:h + 1] - cum_t[h:h + 1, :]
                decay = jnp.exp(jnp.where(causal, seg, -jnp.inf))
                sc = (cbt * decay).astype(BF16)
                xd_pair = xd[:, (h // 2) * LANES:(h // 2 + 1) * LANES]
                keep = (lane < SSM_HEADDIM) if sub == 0 else (lane >= SSM_HEADDIM)
                t = jnp.dot(sc, jnp.where(keep, xd_pair, jnp.zeros_like(xd_pair)), preferred_element_type=F32)
                y_pair = t if y_pair is None else y_pair + t
            y_parts.append(y_pair)
        dec_rows = jnp.concatenate(
            [jnp.broadcast_to(dec[:, g * rep + hh:g * rep + hh + 1], (SSM_HEADDIM, D_STATE)) for hh in range(rep)],
            axis=0)
        new_state.append(st_g * dec_rows + cst)
        y_parts[-2] = y_parts[-2] + y_off[:, 0:LANES] * ecum_x[:, rows][:, 0:LANES]
        y_parts[-1] = y_parts[-1] + y_off[:, LANES:2 * LANES] * ecum_x[:, rows][:, LANES:2 * LANES]
    y = jnp.concatenate(y_parts, axis=1)
    state = jnp.concatenate(new_state, axis=0)
    state_ref[...] = state

    @pl.when(c == nc - 1)
    def _():
        hout_ref[...] = state

    if reverse:
        y = y + yf_ref[...] + xs * d_ref[...]
        y = y * _silu(z_ref[...])
        y = y * lax.rsqrt(jnp.mean(y * y, axis=-1, keepdims=True) + EPS)
        y_ref[...] = y * ng_ref[...]
    else:
        y_ref[...] = y


def _ssd(reverse, xbc, dt, h0, conv_w, conv_b, alog, dtb, extra):
    b, l, _ = xbc.shape
    nc = l // CHUNK
    hb = CHUNK // HALO

    def ch(ci):
        return (nc - 1 - ci) if reverse else ci

    in_specs = [pl.BlockSpec((None, CHUNK, XBC_DIM), lambda bi, ci: (bi, ch(ci), 0)),
                pl.BlockSpec((None, HALO, XBC_DIM), lambda bi, ci: (bi, jnp.maximum(ch(ci) * hb - 1, 0), 0)),
                pl.BlockSpec((None, HALO, XBC_DIM),
                             lambda bi, ci: (bi, jnp.minimum((ch(ci) + 1) * hb, l // HALO - 1), 0)),
                pl.BlockSpec((None, CHUNK, LANES), lambda bi, ci: (bi, ch(ci), 0)),
                pl.BlockSpec((None, SSM_HEADS * SSM_HEADDIM, D_STATE), lambda bi, ci: (bi, 0, 0)),
                pl.BlockSpec((CONV_W, XBC_DIM), lambda bi, ci: (0, 0)),
                pl.BlockSpec((1, XBC_DIM), lambda bi, ci: (0, 0)),
                pl.BlockSpec((1, LANES), lambda bi, ci: (0, 0)),
                pl.BlockSpec((1, LANES), lambda bi, ci: (0, 0))]
    args = [xbc, xbc, xbc, dt, h0, conv_w, conv_b, alog, dtb]
    if reverse:
        y_fwd, z, d_x, norm_g = extra
        in_specs += [pl.BlockSpec((None, CHUNK, D_SSM), lambda bi, ci: (bi, ch(ci), 0)),
                     pl.BlockSpec((None, CHUNK, D_SSM), lambda bi, ci: (bi, ch(ci), 0)),
                     pl.BlockSpec((1, D_SSM), lambda bi, ci: (0, 0)),
                     pl.BlockSpec((1, D_SSM), lambda bi, ci: (0, 0))]
        args += [y_fwd, z, d_x, norm_g]
    return pl.pallas_call(
        functools.partial(_ssd_kernel, reverse, nc),
        grid=(b, nc),
        in_specs=in_specs,
        out_specs=[pl.BlockSpec((None, CHUNK, D_SSM), lambda bi, ci: (bi, ch(ci), 0)),
                   pl.BlockSpec((None, SSM_HEADS * SSM_HEADDIM, D_STATE), lambda bi, ci: (bi, 0, 0))],
        out_shape=[jax.ShapeDtypeStruct((b, l, D_SSM), F32),
                   jax.ShapeDtypeStruct((b, SSM_HEADS * SSM_HEADDIM, D_STATE), F32)],
        scratch_shapes=[pltpu.VMEM((SSM_HEADS * SSM_HEADDIM, D_STATE), F32)],
        compiler_params=_cparams(("parallel", "arbitrary")),
        name="ssd_bwd" if reverse else "ssd_fwd",
    )(*args)


def _outproj_kernel(att_ref, ssm_ref, x_ref, mods_ref, wa_ref, ws_ref, g_ref, wr_ref, br_ref,
                    x1_ref, h2x_ref):
    mix = (jnp.dot(att_ref[...].astype(BF16), wa_ref[...], preferred_element_type=F32)
           + jnp.dot(ssm_ref[...].astype(BF16), ws_ref[...], preferred_element_type=F32))
    gate1 = mods_ref[:, 2 * D_MODEL:3 * D_MODEL]
    shift2 = mods_ref[:, 3 * D_MODEL:4 * D_MODEL]
    scale2 = mods_ref[:, 4 * D_MODEL:5 * D_MODEL]
    x1 = x_ref[...] + gate1 * mix
    x1_ref[...] = x1
    y = x1 * lax.rsqrt(jnp.mean(x1 * x1, axis=-1, keepdims=True) + EPS) * g_ref[...]
    h2 = y * (1.0 + scale2) + shift2
    h_hi = h2.astype(BF16)
    h2x_ref[:, 0:D_MODEL] = h2
    h_lo = (h2 - h_hi.astype(F32)).astype(BF16)
    w_hi = wr_ref[0]
    w_lo = wr_ref[1]
    logits = (jnp.dot(h_hi, w_hi, preferred_element_type=F32) + jnp.dot(h_lo, w_hi, preferred_element_type=F32)
              + jnp.dot(h_hi, w_lo, preferred_element_type=F32)) + br_ref[...]
    lane = lax.broadcasted_iota(jnp.int32, logits.shape, 1)
    neg = -jnp.inf
    big = jnp.int32(1 << 20)
    is_g = lane < N_GROUPS
    gl = jnp.where(is_g, logits, neg)
    gmax = jnp.max(gl, axis=-1, keepdims=True)
    g_idx = jnp.min(jnp.where(gl == gmax, lane, big), axis=-1, keepdims=True)
    p_g = 1.0 / jnp.sum(jnp.where(is_g, jnp.exp(gl - gmax), 0.0), axis=-1, keepdims=True)
    e_lo = N_GROUPS + g_idx * EXPERTS_PER_GROUP
    in_grp = (lane >= e_lo) & (lane < e_lo + EXPERTS_PER_GROUP)
    el = jnp.where(in_grp, logits, neg)
    m1 = jnp.max(el, axis=-1, keepdims=True)
    i1 = jnp.min(jnp.where(el == m1, lane, big), axis=-1, keepdims=True)
    el2 = jnp.where(lane == i1, neg, el)
    m2 = jnp.max(el2, axis=-1, keepdims=True)
    i2 = jnp.min(jnp.where(el2 == m2, lane, big), axis=-1, keepdims=True)
    e2 = jnp.exp(m2 - m1)
    w1 = p_g / (1.0 + e2)
    w2 = p_g * e2 / (1.0 + e2)
    cw = jnp.where(lane == i1, w1, 0.0) + jnp.where(lane == i2, w2, 0.0)
    slab = jnp.where(lane == EXPERTS_PER_GROUP, g_idx.astype(F32), 0.0)
    for j in range(EXPERTS_PER_GROUP):
        wj = jnp.sum(jnp.where(lane == e_lo + j, cw, 0.0), axis=-1, keepdims=True)
        slab = slab + jnp.where(lane == j, wj, 0.0)
    h2x_ref[:, D_MODEL:H2X_W] = slab


def _outproj(att, ssm, x, mods3, mod_row0, mod_tokens, wo_att, wo_ssm, g, w_router, b_router):
    n = x.shape[0]
    tm = 256
    per_mod = mod_tokens // tm
    return pl.pallas_call(
        _outproj_kernel,
        grid=(n // tm,),
        in_specs=[pl.BlockSpec((tm, D_ATT), lambda i: (i, 0)),
                  pl.BlockSpec((tm, D_SSM), lambda i: (i, 0)),
                  pl.BlockSpec((tm, D_MODEL), lambda i: (i, 0)),
                  pl.BlockSpec((None, 1, 6 * D_MODEL), lambda i: (mod_row0 + i // per_mod, 0, 0)),
                  pl.BlockSpec((D_ATT, D_MODEL), lambda i: (0, 0)),
                  pl.BlockSpec((D_SSM, D_MODEL), lambda i: (0, 0)),
                  pl.BlockSpec((1, D_MODEL), lambda i: (0, 0)),
                  pl.BlockSpec((2, D_MODEL, LANES), lambda i: (0, 0, 0)),
                  pl.BlockSpec((1, LANES), lambda i: (0, 0))],
        out_specs=[pl.BlockSpec((tm, D_MODEL), lambda i: (i, 0)),
                   pl.BlockSpec((tm, H2X_W), lambda i: (i, 0))],
        out_shape=[jax.ShapeDtypeStruct((n, D_MODEL), F32),
                   jax.ShapeDtypeStruct((n, H2X_W), F32)],
        compiler_params=_cparams(("parallel",)),
        name="outproj_router",
    )(att, ssm, x, mods3, wo_att, wo_ssm, g, w_router, b_router)


def _route_kernel(slab_ref, meta_ref):
    t_n = MOE_TILE
    blk = LANES
    slab = slab_ref[...]
    lane = lax.broadcasted_iota(jnp.int32, (t_n, LANES), 1)
    gcol = jnp.sum(jnp.where(lane == EXPERTS_PER_GROUP, slab, 0.0), axis=-1, keepdims=True)
    member = (lane.astype(F32) == gcol) & (lane < N_GROUPS)
    a = jnp.where(member, 1.0, 0.0).astype(BF16)
    r_i = lax.broadcasted_iota(jnp.int32, (blk, blk), 0)
    c_i = lax.broadcasted_iota(jnp.int32, (blk, blk), 1)
    lower = jnp.where(c_i < r_i, 1.0, 0.0).astype(BF16)
    upper = jnp.where(r_i < c_i, 1.0, 0.0).astype(BF16)
    offs = jnp.zeros((1, LANES), F32)
    ranks = []
    for b in range(t_n // blk):
        ab = a[b * blk:(b + 1) * blk]
        rb = jnp.dot(lower, ab, preferred_element_type=F32)
        ranks.append(rb + offs)
        offs = offs + rb[blk - 1:blk] + ab[blk - 1:blk].astype(F32)
    rank = jnp.concatenate(ranks, axis=0)
    n_chunk = jnp.floor((offs + (MOE_CHUNK - 1)) * (1.0 / MOE_CHUNK))
    start = jnp.dot(jnp.broadcast_to(n_chunk, (8, LANES)).astype(BF16), upper,
                    preferred_element_type=F32)[0:1]
    end = start + n_chunk
    dest = jnp.sum(jnp.where(member, start * MOE_CHUNK + rank, 0.0), axis=-1, keepdims=True)
    tok = lax.broadcasted_iota(jnp.int32, (t_n, LANES), 0)
    digits = jnp.where(lane == 0, (tok // blk).astype(F32),
                       jnp.where(lane == 1, (tok % blk).astype(F32), jnp.where(lane == 2, 1.0, 0.0))).astype(BF16)
    sw = 512
    pieces = []
    for sc in range(MOE_ROWS // sw):
        s_id = (lax.broadcasted_iota(jnp.int32, (t_n, sw), 1) + sc * sw).astype(F32)
        hit = jnp.where(dest == s_id, 1.0, 0.0).astype(BF16)
        r = lax.dot_general(digits, hit, (((0,), (0,)), ((), ())), preferred_element_type=F32)
        tok_of = r[0:1] * blk + r[1:2]
        pieces.append(jnp.where(r[2:3] > 0.5, tok_of, float(t_n)))
    perm = jnp.concatenate(pieces, axis=1)
    slot = lax.broadcasted_iota(jnp.int32, (1, MOE_ROWS), 1).astype(F32)
    lane1 = lax.broadcasted_iota(jnp.int32, (1, LANES), 1)
    cg = jnp.zeros((1, MOE_ROWS), F32)
    for g in range(N_GROUPS):
        end_g = jnp.sum(jnp.where(lane1 == g, end, 0.0), axis=-1, keepdims=True)
        cg = cg + jnp.where(slot >= end_g, 1.0, 0.0)
    meta_ref[...] = jnp.concatenate([perm, cg, jnp.zeros((6, MOE_ROWS), F32)], axis=0).astype(jnp.int32)


def _route(h2x):
    n = h2x.shape[0]
    n_tiles = n // MOE_TILE
    return pl.pallas_call(
        _route_kernel,
        grid=(n_tiles,),
        in_specs=[pl.BlockSpec((MOE_TILE, LANES), lambda i: (i, D_MODEL // LANES))],
        out_specs=pl.BlockSpec((None, 8, MOE_ROWS), lambda i: (i, 0, 0)),
        out_shape=jax.ShapeDtypeStruct((n_tiles, 8, MOE_ROWS), jnp.int32),
        compiler_params=_cparams(("parallel",)),
        name="moe_route",
    )(h2x)


ROW_UNROLL = 8


def _moe_kernel(perm_ref, cg_ref, h_ref, wg_ref, wu_ref, wd_ref, y_ref, hs_ref, ys_ref):
    i = pl.program_id(0)
    s = pl.program_id(1)

    @pl.when(s == 0)
    def _():
        y_ref[MOE_TILE:MOE_TILE + 8, :] = jnp.zeros((8, D_MODEL), F32)

    @pl.when(cg_ref[i * MOE_SLOTS + s] < N_GROUPS)
    def _():
        base = i * MOE_ROWS + s * MOE_CHUNK

        def gather(rb, carry):
            r0 = rb * ROW_UNROLL
            rows = [h_ref[pl.ds(jnp.minimum(perm_ref[base + r0 + u], MOE_TILE - 1), 1), :]
                    for u in range(ROW_UNROLL)]
            for u in range(ROW_UNROLL):
                hs_ref[pl.ds(r0 + u, 1), :] = rows[u]
            return carry

        lax.fori_loop(0, MOE_CHUNK // ROW_UNROLL, gather, 0)
        hb = hs_ref[:, 0:D_MODEL].astype(BF16)
        cw = hs_ref[:, D_MODEL:H2X_W]
        acc = None
        for e in range(EXPERTS_PER_GROUP):
            a = jnp.dot(hb, wg_ref[e], preferred_element_type=F32)
            u = jnp.dot(hb, wu_ref[e], preferred_element_type=F32)
            hid = (_silu(a) * u * cw[:, e:e + 1]).astype(BF16)
            t = jnp.dot(hid, wd_ref[e], preferred_element_type=F32)
            acc = t if acc is None else acc + t
        ys_ref[...] = acc

        def scatter(rb, carry):
            r0 = rb * ROW_UNROLL
            rows = [ys_ref[pl.ds(r0 + u, 1), :] for u in range(ROW_UNROLL)]
            for u in range(ROW_UNROLL):
                y_ref[pl.ds(perm_ref[base + r0 + u], 1), :] = rows[u]
            return carry

        lax.fori_loop(0, MOE_CHUNK // ROW_UNROLL, scatter, 0)


def _moe(h2x, perm, cgrp, wg, wu, wd):
    n = h2x.shape[0]
    n_tiles = n // MOE_TILE

    def w_idx(i, s, perm_ref, cg_ref):
        return (jnp.minimum(cg_ref[i * MOE_SLOTS + s], N_GROUPS - 1), 0, 0)

    return pl.pallas_call(
        _moe_kernel,
        grid_spec=pltpu.PrefetchScalarGridSpec(
            num_scalar_prefetch=2,
            grid=(n_tiles, MOE_SLOTS),
            in_specs=[pl.BlockSpec((MOE_TILE, H2X_W), lambda i, s, p, c: (i, 0)),
                      pl.BlockSpec((EXPERTS_PER_GROUP, D_MODEL, EXPERT_FF), w_idx),
                      pl.BlockSpec((EXPERTS_PER_GROUP, D_MODEL, EXPERT_FF), w_idx),
                      pl.BlockSpec((EXPERTS_PER_GROUP, EXPERT_FF, D_MODEL), w_idx)],
            out_specs=pl.BlockSpec((None, MOE_TILE + 8, D_MODEL), lambda i, s, p, c: (i, 0, 0)),
            scratch_shapes=[pltpu.VMEM((MOE_CHUNK, H2X_W), F32), pltpu.VMEM((MOE_CHUNK, D_MODEL), F32)]),
        out_shape=jax.ShapeDtypeStruct((n_tiles, MOE_TILE + 8, D_MODEL), F32),
        compiler_params=_cparams(("parallel", "arbitrary")),
        name="moe_experts",
    )(perm, cgrp, h2x, wg, wu, wd)


def _final_kernel(y_ref, x1_ref, mods_ref, fg_ref, o_ref):
    gate2 = mods_ref[:, 5 * D_MODEL:6 * D_MODEL]
    x2 = x1_ref[...] + gate2 * y_ref[...]
    o_ref[...] = x2 * lax.rsqrt(jnp.mean(x2 * x2, axis=-1, keepdims=True) + EPS) * fg_ref[...]


def _final(y, x1, mods3, mod_row0, mod_tokens, fg):
    n = x1.shape[0]
    tm = 512
    per_mod = mod_tokens // tm
    per_tile = MOE_TILE // tm
    return pl.pallas_call(
        _final_kernel,
        grid=(n // tm,),
        in_specs=[pl.BlockSpec((None, tm, D_MODEL), lambda j: (j // per_tile, j % per_tile, 0)),
                  pl.BlockSpec((tm, D_MODEL), lambda j: (j, 0)),
                  pl.BlockSpec((None, 1, 6 * D_MODEL), lambda j: (mod_row0 + j // per_mod, 0, 0)),
                  pl.BlockSpec((1, D_MODEL), lambda j: (0, 0))],
        out_specs=pl.BlockSpec((tm, D_MODEL), lambda j: (j, 0)),
        out_shape=jax.ShapeDtypeStruct((n, D_MODEL), F32),
        compiler_params=_cparams(("parallel",)),
        name="final_norm",
    )(y, x1, mods3, fg)


def _rope_tables(t):
    n_freq = QK_DIM // 4
    freqs = ROPE_BASE ** (-jnp.arange(n_freq, dtype=F32) / n_freq)
    pos = jnp.arange(t)
    row = (pos // GRID_W).astype(F32)
    col = (pos % GRID_W).astype(F32)
    j = jnp.arange(LANES) % QK_DIM
    ang = jnp.where((j < QK_DIM // 2)[None, :], row[:, None], col[:, None]) * freqs[j % n_freq][None, :]
    first = ((j % (QK_DIM // 2)) < n_freq)[None, :]
    sin = jnp.sin(ang)
    return jnp.cos(ang), jnp.where(first, -sin, 0.0), jnp.where(first, 0.0, sin)


def _layer(x, mods3, mod_row0, mod_tokens, rope_tabs, ctx_k, ctx_v, h0f, h0b, lw, layer):
    b, t, _ = x.shape
    n = b * t
    xf = x.reshape(n, D_MODEL)
    q, k, v, z, xbc, dt = _inproj(xf, mods3, mod_row0, mod_tokens, t, lw["norm_mix_g"], lw["w_main"],
                                  lw["w_dt"], rope_tabs)
    k3 = k.reshape(b, t, D_QK)
    v3 = v.reshape(b, t, D_ATT)
    if ctx_k is None:
        k_all, v_all = k3.astype(BF16), v3.astype(BF16)
    else:
        k_all = jnp.concatenate([k3.astype(BF16), ctx_k.astype(BF16)], axis=1)
        v_all = jnp.concatenate([v3.astype(BF16), ctx_v.astype(BF16)], axis=1)
    lam0 = 0.8 - 0.6 * math.exp(-0.3 * layer)
    tq = 256
    tk = 512 if k_all.shape[1] % 512 == 0 else 256
    att = _attention(q.reshape(b, t, D_QK), k_all, v_all, lw["lamp"], lw["attn_subln_g"], lam0, tq, tk)
    xbc3 = xbc.reshape(b, t, XBC_DIM)
    dt3 = dt.reshape(b, t, LANES)
    y_f, hf = _ssd(False, xbc3, dt3, h0f, lw["conv_w"], lw["conv_b"], lw["alog_f"], lw["dtb_f"], None)
    ssm, hb = _ssd(True, xbc3, dt3, h0b, lw["conv_w"], lw["conv_b"], lw["alog_b"], lw["dtb_b"],
                   (y_f, z.reshape(b, t, D_SSM), lw["d_x"], lw["ssm_norm_g"]))
    x1, h2x = _outproj(att.reshape(n, D_ATT), ssm.reshape(n, D_SSM), xf, mods3, mod_row0, mod_tokens,
                       lw["wo_att"], lw["wo_ssm"], lw["norm_ffn_g"], lw["w_router"], lw["b_router"])
    meta = _route(h2x)
    perm = meta[:, 0, :].reshape(-1)
    cgrp = meta[:, 1, :MOE_SLOTS].reshape(-1)
    y = _moe(h2x, perm, cgrp, lw["wg"], lw["wu"], lw["wd"])
    out = _final(y, x1, mods3, mod_row0, mod_tokens, lw["final_g"])
    return out.reshape(b, t, D_MODEL), k3, v3, hf, hb


def _pad_lanes(v, width=LANES):
    return jnp.pad(v, [(0, 0)] * (v.ndim - 1) + [(0, width - v.shape[-1])])


def kernel(x_prompt, x_sample, cache_k, cache_v, state_ssm_fwd, state_ssm_bwd, c, c_ctx, w_ada, b_ada, norm_mix_g, w_in, w_out, lambda_q1, lambda_k1, lambda_q2, lambda_k2, attn_subln_g, conv_w, conv_b, a_log_fwd, a_log_bwd, dt_bias_fwd, dt_bias_bwd, ssm_d, ssm_norm_g, norm_ffn_g, w_group_router, b_group_router, w_expert_router, b_expert_router, w_exp_gate, w_exp_up, w_exp_down, final_norm_g):
    depth = w_in.shape[0]
    assert depth == 1, "single trunk layer"
    bp, tp, _ = x_prompt.shape
    bs, ts, _ = x_sample.shape
    l = 0
    cond = jnp.concatenate([c_ctx[None], c], axis=0)
    condT = _pad_lanes(cond.T, 8)
    mods = _ada(condT, w_ada[l], b_ada[l][None])
    mods3 = mods.reshape(8, 1, 6 * D_MODEL)

    w_router = _pad_lanes(jnp.concatenate([w_group_router[l], w_expert_router[l]], axis=1))
    wr_hi = w_router.astype(BF16)
    wr_lo = (w_router - wr_hi.astype(F32)).astype(BF16)
    lw = dict(
        norm_mix_g=norm_mix_g[l][None],
        w_main=w_in[l][:, :MAIN_COLS].astype(BF16),
        w_dt=_pad_lanes(w_in[l][:, MAIN_COLS:]).astype(BF16),
        lamp=jnp.stack([lambda_q1[l], lambda_k1[l], lambda_q2[l], lambda_k2[l]]),
        attn_subln_g=attn_subln_g[l][None],
        conv_w=conv_w[l], conv_b=conv_b[l][None],
        alog_f=_pad_lanes(a_log_fwd[l][None]), alog_b=_pad_lanes(a_log_bwd[l][None]),
        dtb_f=_pad_lanes(dt_bias_fwd[l][None]), dtb_b=_pad_lanes(dt_bias_bwd[l][None]),
        d_x=jnp.repeat(ssm_d[l], SSM_HEADDIM)[None], ssm_norm_g=ssm_norm_g[l][None],
        wo_att=w_out[l][:D_ATT].astype(BF16), wo_ssm=w_out[l][D_ATT:].astype(BF16),
        norm_ffn_g=norm_ffn_g[l][None],
        w_router=jnp.stack([wr_hi, wr_lo]),
        b_router=_pad_lanes(jnp.concatenate([b_group_router[l], b_expert_router[l]])[None]),
        wg=w_exp_gate[l].astype(BF16), wu=w_exp_up[l].astype(BF16), wd=w_exp_down[l].astype(BF16),
        final_g=final_norm_g[None],
    )
    n_state = SSM_HEADS * SSM_HEADDIM
    zeros_state = jnp.zeros((bp, n_state, D_STATE), F32)
    yp, ck, cv, hf, hb = _layer(x_prompt, mods3, 0, bp * tp, None, None, None, zeros_state, zeros_state, lw, l)
    ys, _, _, _, _ = _layer(x_sample, mods3, 1, ts, _rope_tables(ts),
                            cache_k[:, l].reshape(bs, -1, D_QK), cache_v[:, l].reshape(bs, -1, D_ATT),
                            state_ssm_fwd[:, l].reshape(bs, n_state, D_STATE),
                            state_ssm_bwd[:, l].reshape(bs, n_state, D_STATE), lw, l)
    new_k = ck.reshape(bp, 1, tp, ATT_HEADS, 2, QK_DIM)
    new_v = cv.reshape(bp, 1, tp, ATT_HEADS, V_DIM)
    new_hf = hf.reshape(bp, 1, SSM_HEADS, SSM_HEADDIM, D_STATE)
    new_hb = hb.reshape(bp, 1, SSM_HEADS, SSM_HEADDIM, D_STATE)
    return yp, ys, new_k, new_v, new_hf, new_hb
```

```python
import functools
import math

import jax
import jax.numpy as jnp
from jax import lax
from jax.experimental import pallas as pl
from jax.experimental.pallas import tpu as pltpu

D_MODEL = 1024
GRID_W = 64
ATT_HEADS = 4
QK_DIM = 64
V_DIM = 128
D_QK = 512
D_ATT = 512
ROPE_BASE = 10000.0
D_SSM = 512
SSM_HEADDIM = 64
SSM_HEADS = 8
SSM_GROUPS = 2
D_STATE = 128
CONV_W = 5
CHUNK = 128
XBC_DIM = 1024
N_GROUPS = 4
EXPERTS_PER_GROUP = 4
N_EXPERTS = 16
EXPERT_FF = 256
EPS = 1e-6
MAIN_COLS = 2 * D_QK + D_ATT + D_SSM + XBC_DIM
H2X_W = D_MODEL + 128
MOE_TILE = 2048
MOE_CHUNK = 256
MOE_SLOTS = MOE_TILE // MOE_CHUNK + N_GROUPS
MOE_ROWS = MOE_SLOTS * MOE_CHUNK
LANES = 128
HALO = 8
VMEM_LIMIT = 56 * 1024 * 1024

LOG2E = math.log2(math.e)
SUM_ROWS = 16
F32 = jnp.float32
BF16 = jnp.bfloat16


def _cparams(sem):
    return pltpu.CompilerParams(dimension_semantics=sem, vmem_limit_bytes=VMEM_LIMIT)


def _sigmoid(x):
    return 1.0 / (1.0 + jnp.exp(-x))


def _silu(x):
    return x * _sigmoid(x)


def _ada_kernel(condT_ref, w_ref, b_ref, o_ref):
    s = _silu(condT_ref[...])
    w = w_ref[...]
    b = b_ref[...]
    o_ref[...] = jnp.zeros_like(o_ref)
    for r in range(3):
        o_ref[r:r + 1, :] = jnp.sum(w * s[:, r:r + 1], axis=0, keepdims=True) + b


def _ada(condT, w_ada, b_ada):
    bn = 512
    n = w_ada.shape[1]
    return pl.pallas_call(
        _ada_kernel,
        grid=(n // bn,),
        in_specs=[pl.BlockSpec((D_MODEL, 8), lambda j: (0, 0)),
                  pl.BlockSpec((D_MODEL, bn), lambda j: (0, j)),
                  pl.BlockSpec((1, bn), lambda j: (0, j))],
        out_specs=pl.BlockSpec((8, bn), lambda j: (0, j)),
        out_shape=jax.ShapeDtypeStruct((8, n), F32),
        compiler_params=_cparams(("arbitrary",)),
        name="ada",
    )(condT, w_ada, b_ada)


def _inproj_kernel(rope, x_ref, mods_ref, g_ref, w_ref, wdt_ref, *rest):
    if rope:
        cos_ref, sa_ref, sb_ref, q_ref, k_ref, v_ref, z_ref, xbc_ref, dt_ref = rest
    else:
        q_ref, k_ref, v_ref, z_ref, xbc_ref, dt_ref = rest
    x = x_ref[...]
    shift = mods_ref[:, 0:D_MODEL]
    scale = mods_ref[:, D_MODEL:2 * D_MODEL]
    y = x * lax.rsqrt(jnp.mean(x * x, axis=-1, keepdims=True) + EPS) * g_ref[...]
    h = (y * (1.0 + scale) + shift).astype(BF16)
    r = jnp.dot(h, w_ref[...], preferred_element_type=F32)
    dt_ref[...] = jnp.dot(h, wdt_ref[...], preferred_element_type=F32)
    q = r[:, 0:D_QK]
    k = r[:, D_QK:2 * D_QK]
    if rope:
        cos = cos_ref[...]
        sa = sa_ref[...]
        sb = sb_ref[...]

        def rot(t):
            parts = []
            for hh in range(ATT_HEADS):
                th = t[:, hh * LANES:(hh + 1) * LANES]
                parts.append(th * cos + pltpu.roll(th, LANES - 16, 1) * sa + pltpu.roll(th, 16, 1) * sb)
            return jnp.concatenate(parts, axis=1)

        q = rot(q)
        k = rot(k)
    q_ref[...] = q
    k_ref[...] = k
    v_ref[...] = r[:, 2 * D_QK:2 * D_QK + D_ATT]
    z_ref[...] = r[:, 2 * D_QK + D_ATT:2 * D_QK + D_ATT + D_SSM]
    xbc_ref[...] = r[:, 2 * D_QK + D_ATT + D_SSM:MAIN_COLS]


def _inproj(x, mods3, mod_row0, mod_tokens, seq_len, g, w_main, w_dt, rope_tabs):
    n = x.shape[0]
    tm = 256
    per_seq = seq_len // tm
    per_mod = mod_tokens // tm
    rope = rope_tabs is not None
    in_specs = [pl.BlockSpec((tm, D_MODEL), lambda i: (i, 0)),
                pl.BlockSpec((None, 1, 6 * D_MODEL), lambda i: (mod_row0 + i // per_mod, 0, 0)),
                pl.BlockSpec((1, D_MODEL), lambda i: (0, 0)),
                pl.BlockSpec((D_MODEL, MAIN_COLS), lambda i: (0, 0)),
                pl.BlockSpec((D_MODEL, LANES), lambda i: (0, 0))]
    args = [x, mods3, g, w_main, w_dt]
    if rope:
        tab_spec = pl.BlockSpec((tm, LANES), lambda i: (i % per_seq, 0))
        in_specs += [tab_spec] * 3
        args += list(rope_tabs)
    widths = (D_QK, D_QK, D_ATT, D_SSM, XBC_DIM, LANES)
    return pl.pallas_call(
        functools.partial(_inproj_kernel, rope),
        grid=(n // tm,),
        in_specs=in_specs,
        out_specs=[pl.BlockSpec((tm, wd), lambda i: (i, 0)) for wd in widths],
        out_shape=[jax.ShapeDtypeStruct((n, wd), F32) for wd in widths],
        compiler_params=_cparams(("parallel",)),
        name="inproj_rope" if rope else "inproj",
    )(*args)


def _attn_kernel(tk, lam0, q_ref, k_ref, vt_ref, lamp_ref, g_ref, o_ref):
    tq = q_ref.shape[0]
    n_kv = k_ref.shape[0] // tk
    lp = lamp_ref[...]
    lam = (jnp.exp(jnp.sum(lp[0:1] * lp[1:2], axis=-1, keepdims=True))
           - jnp.exp(jnp.sum(lp[2:3] * lp[3:4], axis=-1, keepdims=True)) + lam0)
    q = q_ref[...] * (QK_DIM ** -0.5 * LOG2E)
    lane = lax.broadcasted_iota(jnp.int32, q.shape, 1)
    qq_t = jnp.concatenate([jnp.where(lane < QK_DIM, q, 0.0), jnp.where(lane >= QK_DIM, q, 0.0)],
                           axis=0).T.astype(BF16)

    def scores(c):
        start = pl.multiple_of(c * tk, tk)
        return jnp.dot(k_ref[pl.ds(start, tk), :], qq_t, preferred_element_type=F32)

    ones_rows = jnp.ones((SUM_ROWS, tk), BF16)

    def update(s, c, m, acc):
        start = pl.multiple_of(c * tk, tk)
        m_new = jnp.maximum(m, jnp.max(s, axis=0, keepdims=True))
        alpha = jnp.exp2(m - m_new)
        p = jnp.exp2(s - m_new).astype(BF16)
        v_ext = jnp.concatenate([vt_ref[:, pl.ds(start, tk)], ones_rows], axis=0)
        acc = alpha * acc + jnp.dot(v_ext, p, preferred_element_type=F32)
        return m_new, acc

    def body(c, carry):
        s, m, acc = carry
        s_next = scores(c + 1)
        m, acc = update(s, c, m, acc)
        return s_next, m, acc

    init = (scores(0), jnp.full((1, 2 * tq), -jnp.inf, F32), jnp.zeros((V_DIM + SUM_ROWS, 2 * tq), F32))
    s, m, acc = lax.fori_loop(0, n_kv - 1, body, init, unroll=True)
    _, acc = update(s, n_kv - 1, m, acc)
    o = acc[0:V_DIM] / acc[V_DIM:V_DIM + 1]
    o = (o[:, 0:tq] - lam * o[:, tq:2 * tq]).T
    o = o * lax.rsqrt(jnp.mean(o * o, axis=-1, keepdims=True) + EPS)
    o_ref[...] = o * g_ref[...] * (1.0 - lam0)


def _attention(q, k, vt, lamp, g, lam0, tq, tk):
    b, t, _ = q.shape
    t_kv = k.shape[1]
    return pl.pallas_call(
        functools.partial(_attn_kernel, tk, lam0),
        grid=(b, ATT_HEADS, t // tq),
        in_specs=[pl.BlockSpec((None, tq, LANES), lambda bi, h, i: (bi, i, h)),
                  pl.BlockSpec((None, t_kv, LANES), lambda bi, h, i: (bi, 0, h)),
                  pl.BlockSpec((None, LANES, t_kv), lambda bi, h, i: (bi, h, 0)),
                  pl.BlockSpec((4, QK_DIM), lambda bi, h, i: (0, 0)),
                  pl.BlockSpec((1, V_DIM), lambda bi, h, i: (0, 0))],
        out_specs=pl.BlockSpec((None, tq, LANES), lambda bi, h, i: (bi, i, h)),
        out_shape=jax.ShapeDtypeStruct((b, t, D_ATT), F32),
        compiler_params=_cparams(("parallel", "parallel", "arbitrary")),
        name="diff_attn",
    )(q, k, vt, lamp, g)


def _expand_heads(v):
    lane = lax.broadcasted_iota(jnp.int32, (v.shape[0], LANES), 1)
    parts = []
    for pr in range(SSM_HEADS // 2):
        a = jnp.broadcast_to(v[:, 2 * pr:2 * pr + 1], (v.shape[0], LANES))
        b = jnp.broadcast_to(v[:, 2 * pr + 1:2 * pr + 2], (v.shape[0], LANES))
        parts.append(jnp.where(lane < SSM_HEADDIM, a, b))
    return jnp.concatenate(parts, axis=1)


def _ssd_kernel(reverse, nc, xc_ref, xp_ref, xn_ref, dt_ref, h0_ref, cw_ref, cb_ref, alog_ref, dtb_ref,
                *rest):
    if reverse:
        yf_ref, z_ref, d_ref, ng_ref, y_ref, hout_ref, state_ref = rest
    else:
        y_ref, hout_ref, state_ref = rest
    c = pl.program_id(1)
    chunk = (nc - 1 - c) if reverse else c

    @pl.when(c == 0)
    def _():
        state_ref[...] = h0_ref[...]

    prev = jnp.where(chunk > 0, xp_ref[...], 0.0)
    nxt = jnp.where(chunk < nc - 1, xn_ref[...], 0.0)
    xpad = jnp.concatenate([prev, xc_ref[...], nxt], axis=0)
    cw = cw_ref[...]
    conv = cb_ref[...]
    for kk in range(CONV_W):
        off = HALO - CONV_W // 2 + kk
        conv = conv + xpad[off:off + CHUNK, :] * cw[kk:kk + 1, :]
    xbc = _silu(conv)
    xs = xbc[:, 0:D_SSM]
    bm = xbc[:, D_SSM:D_SSM + SSM_GROUPS * D_STATE].astype(BF16)
    cm = xbc[:, D_SSM + SSM_GROUPS * D_STATE:XBC_DIM].astype(BF16)

    dt_raw = dt_ref[...]
    if reverse:
        dt_raw = pltpu.roll(dt_raw, LANES - SSM_HEADS, 1)
    xv = dt_raw + dtb_ref[...]
    dt = jnp.maximum(xv, 0.0) + jnp.log(1.0 + jnp.exp(-jnp.abs(xv)))
    la = dt * (-jnp.exp(alog_ref[...]))
    row = lax.broadcasted_iota(jnp.int32, (CHUNK, LANES), 0)
    col = lax.broadcasted_iota(jnp.int32, (CHUNK, LANES), 1)
    cum = la
    sh = 1
    while sh < CHUNK:
        if reverse:
            cum = cum + jnp.where(row < CHUNK - sh, pltpu.roll(cum, CHUNK - sh, 0), 0.0)
        else:
            cum = cum + jnp.where(row >= sh, pltpu.roll(cum, sh, 0), 0.0)
        sh *= 2
    end = CHUNK - 1 if not reverse else 0
    cum_end = cum[end:end + 1, :]
    cum_t = cum.T
    causal = (row <= col) if reverse else (row >= col)

    dt_x = _expand_heads(dt)
    xd = xs * dt_x
    xdw = (xd * _expand_heads(jnp.exp(cum_end - cum))).astype(BF16)
    xd = xd.astype(BF16)
    ecum_x = _expand_heads(jnp.exp(cum))
    lane = col
    rep = SSM_HEADS // SSM_GROUPS
    state = state_ref[...]
    y_parts = []
    new_state = []
    for g in range(SSM_GROUPS):
        bg = bm[:, g * D_STATE:(g + 1) * D_STATE]
        cg = cm[:, g * D_STATE:(g + 1) * D_STATE]
        cbt = lax.dot_general(cg, bg, (((1,), (1,)), ((), ())), preferred_element_type=F32)
        rows = slice(g * rep * SSM_HEADDIM, (g + 1) * rep * SSM_HEADDIM)
        st_g = state[rows, :]
        y_off = lax.dot_general(cg, st_g.astype(BF16), (((1,), (1,)), ((), ())),
                                preferred_element_type=F32)
        cst = lax.dot_general(xdw[:, rows], bg, (((0,), (0,)), ((), ())), preferred_element_type=F32)
        dec = jnp.exp(cum_end)
        for pr in range(rep // 2):
            y_pair = None
            for sub in range(2):
                h = g * rep + 2 * pr + sub
                seg = cum[:, h:h + 1] - cum_t[h:h + 1, :]
                decay = jnp.exp(jnp.where(causal, seg, -jnp.inf))
                sc = (cbt * decay).astype(BF16)
                xd_pair = xd[:, (h // 2) * LANES:(h // 2 + 1) * LANES]
                keep = (lane < SSM_HEADDIM) if sub == 0 else (lane >= SSM_HEADDIM)
                t = jnp.dot(sc, jnp.where(keep, xd_pair, jnp.zeros_like(xd_pair)), preferred_element_type=F32)
                y_pair = t if y_pair is None else y_pair + t
            y_parts.append(y_pair)
        dec_rows = jnp.concatenate(
            [jnp.broadcast_to(dec[:, g * rep + hh:g * rep + hh + 1], (SSM_HEADDIM, D_STATE)) for hh in range(rep)],
            axis=0)
        new_state.append(st_g * dec_rows + cst)
        y_parts[-2] = y_parts[-2] + y_off[:, 0:LANES] * ecum_x[:, rows][:, 0:LANES]
        y_parts[-1] = y_parts[-1] + y_off[:, LANES:2 * LANES] * ecum_x[:, rows][:, LANES:2 * LANES]
    y = jnp.concatenate(y_parts, axis=1)
    state = jnp.concatenate(new_state, axis=0)
    state_ref[...] = state

    @pl.when(c == nc - 1)
    def _():
        hout_ref[...] = state

    if reverse:
        y = y + yf_ref[...] + xs * d_ref[...]
        y = y * _silu(z_ref[...])
        y = y * lax.rsqrt(jnp.mean(y * y, axis=-1, keepdims=True) + EPS)
        y_ref[...] = y * ng_ref[...]
    else:
        y_ref[...] = y


def _ssd(reverse, xbc, dt, h0, conv_w, conv_b, alog, dtb, extra):
    b, l, _ = xbc.shape
    nc = l // CHUNK
    hb = CHUNK // HALO

    def ch(ci):
        return (nc - 1 - ci) if reverse else ci

    in_specs = [pl.BlockSpec((None, CHUNK, XBC_DIM), lambda bi, ci: (bi, ch(ci), 0)),
                pl.BlockSpec((None, HALO, XBC_DIM), lambda bi, ci: (bi, jnp.maximum(ch(ci) * hb - 1, 0), 0)),
                pl.BlockSpec((None, HALO, XBC_DIM),
                             lambda bi, ci: (bi, jnp.minimum((ch(ci) + 1) * hb, l // HALO - 1), 0)),
                pl.BlockSpec((None, CHUNK, LANES), lambda bi, ci: (bi, ch(ci), 0)),
                pl.BlockSpec((None, SSM_HEADS * SSM_HEADDIM, D_STATE), lambda bi, ci: (bi, 0, 0)),
                pl.BlockSpec((CONV_W, XBC_DIM), lambda bi, ci: (0, 0)),
                pl.BlockSpec((1, XBC_DIM), lambda bi, ci: (0, 0)),
                pl.BlockSpec((1, LANES), lambda bi, ci: (0, 0)),
                pl.BlockSpec((1, LANES), lambda bi, ci: (0, 0))]
    args = [xbc, xbc, xbc, dt, h0, conv_w, conv_b, alog, dtb]
    if reverse:
        y_fwd, z, d_x, norm_g = extra
        in_specs += [pl.BlockSpec((None, CHUNK, D_SSM), lambda bi, ci: (bi, ch(ci), 0)),
                     pl.BlockSpec((None, CHUNK, D_SSM), lambda bi, ci: (bi, ch(ci), 0)),
                     pl.BlockSpec((1, D_SSM), lambda bi, ci: (0, 0)),
                     pl.BlockSpec((1, D_SSM), lambda bi, ci: (0, 0))]
        args += [y_fwd, z, d_x, norm_g]
    return pl.pallas_call(
        functools.partial(_ssd_kernel, reverse, nc),
        grid=(b, nc),
        in_specs=in_specs,
        out_specs=[pl.BlockSpec((None, CHUNK, D_SSM), lambda bi, ci: (bi, ch(ci), 0)),
                   pl.BlockSpec((None, SSM_HEADS * SSM_HEADDIM, D_STATE), lambda bi, ci: (bi, 0, 0))],
        out_shape=[jax.ShapeDtypeStruct((b, l, D_SSM), F32),
                   jax.ShapeDtypeStruct((b, SSM_HEADS * SSM_HEADDIM, D_STATE), F32)],
        scratch_shapes=[pltpu.VMEM((SSM_HEADS * SSM_HEADDIM, D_STATE), F32)],
        compiler_params=_cparams(("parallel", "arbitrary")),
        name="ssd_bwd" if reverse else "ssd_fwd",
    )(*args)


def _outproj_kernel(att_ref, ssm_ref, x_ref, mods_ref, wa_ref, ws_ref, g_ref, wr_ref, br_ref,
                    x1_ref, h2x_ref):
    mix = (jnp.dot(att_ref[...].astype(BF16), wa_ref[...], preferred_element_type=F32)
           + jnp.dot(ssm_ref[...].astype(BF16), ws_ref[...], preferred_element_type=F32))
    gate1 = mods_ref[:, 2 * D_MODEL:3 * D_MODEL]
    shift2 = mods_ref[:, 3 * D_MODEL:4 * D_MODEL]
    scale2 = mods_ref[:, 4 * D_MODEL:5 * D_MODEL]
    x1 = x_ref[...] + gate1 * mix
    x1_ref[...] = x1
    y = x1 * lax.rsqrt(jnp.mean(x1 * x1, axis=-1, keepdims=True) + EPS) * g_ref[...]
    h2 = y * (1.0 + scale2) + shift2
    h_hi = h2.astype(BF16)
    h2x_ref[:, 0:D_MODEL] = h2
    h_lo = (h2 - h_hi.astype(F32)).astype(BF16)
    w_hi = wr_ref[0]
    w_lo = wr_ref[1]
    logits = (jnp.dot(h_hi, w_hi, preferred_element_type=F32) + jnp.dot(h_lo, w_hi, preferred_element_type=F32)
              + jnp.dot(h_hi, w_lo, preferred_element_type=F32)) + br_ref[...]
    lane = lax.broadcasted_iota(jnp.int32, logits.shape, 1)
    neg = -jnp.inf
    big = jnp.int32(1 << 20)
    is_g = lane < N_GROUPS
    gl = jnp.where(is_g, logits, neg)
    gmax = jnp.max(gl, axis=-1, keepdims=True)
    g_idx = jnp.min(jnp.where(gl == gmax, lane, big), axis=-1, keepdims=True)
    p_g = 1.0 / jnp.sum(jnp.where(is_g, jnp.exp(gl - gmax), 0.0), axis=-1, keepdims=True)
    e_lo = N_GROUPS + g_idx * EXPERTS_PER_GROUP
    in_grp = (lane >= e_lo) & (lane < e_lo + EXPERTS_PER_GROUP)
    el = jnp.where(in_grp, logits, neg)
    m1 = jnp.max(el, axis=-1, keepdims=True)
    i1 = jnp.min(jnp.where(el == m1, lane, big), axis=-1, keepdims=True)
    el2 = jnp.where(lane == i1, neg, el)
    m2 = jnp.max(el2, axis=-1, keepdims=True)
    i2 = jnp.min(jnp.where(el2 == m2, lane, big), axis=-1, keepdims=True)
    e2 = jnp.exp(m2 - m1)
    w1 = p_g / (1.0 + e2)
    w2 = p_g * e2 / (1.0 + e2)
    cw = jnp.where(lane == i1, w1, 0.0) + jnp.where(lane == i2, w2, 0.0)
    slab = jnp.where(lane == EXPERTS_PER_GROUP, g_idx.astype(F32), 0.0)
    for j in range(EXPERTS_PER_GROUP):
        wj = jnp.sum(jnp.where(lane == e_lo + j, cw, 0.0), axis=-1, keepdims=True)
        slab = slab + jnp.where(lane == j, wj, 0.0)
    h2x_ref[:, D_MODEL:H2X_W] = slab


def _outproj(att, ssm, x, mods3, mod_row0, mod_tokens, wo_att, wo_ssm, g, w_router, b_router):
    n = x.shape[0]
    tm = 256
    per_mod = mod_tokens // tm
    return pl.pallas_call(
        _outproj_kernel,
        grid=(n // tm,),
        in_specs=[pl.BlockSpec((tm, D_ATT), lambda i: (i, 0)),
                  pl.BlockSpec((tm, D_SSM), lambda i: (i, 0)),
                  pl.BlockSpec((tm, D_MODEL), lambda i: (i, 0)),
                  pl.BlockSpec((None, 1, 6 * D_MODEL), lambda i: (mod_row0 + i // per_mod, 0, 0)),
                  pl.BlockSpec((D_ATT, D_MODEL), lambda i: (0, 0)),
                  pl.BlockSpec((D_SSM, D_MODEL), lambda i: (0, 0)),
                  pl.BlockSpec((1, D_MODEL), lambda i: (0, 0)),
                  pl.BlockSpec((2, D_MODEL, LANES), lambda i: (0, 0, 0)),
                  pl.BlockSpec((1, LANES), lambda i: (0, 0))],
        out_specs=[pl.BlockSpec((tm, D_MODEL), lambda i: (i, 0)),
                   pl.BlockSpec((tm, H2X_W), lambda i: (i, 0))],
        out_shape=[jax.ShapeDtypeStruct((n, D_MODEL), F32),
                   jax.ShapeDtypeStruct((n, H2X_W), F32)],
        compiler_params=_cparams(("parallel",)),
        name="outproj_router",
    )(att, ssm, x, mods3, wo_att, wo_ssm, g, w_router, b_router)


def _route_kernel(slab_ref, meta_ref):
    t_n = MOE_TILE
    blk = LANES
    slab = slab_ref[...]
    lane = lax.broadcasted_iota(jnp.int32, (t_n, LANES), 1)
    gcol = jnp.sum(jnp.where(lane == EXPERTS_PER_GROUP, slab, 0.0), axis=-1, keepdims=True)
    member = (lane.astype(F32) == gcol) & (lane < N_GROUPS)
    a = jnp.where(member, 1.0, 0.0).astype(BF16)
    r_i = lax.broadcasted_iota(jnp.int32, (blk, blk), 0)
    c_i = lax.broadcasted_iota(jnp.int32, (blk, blk), 1)
    lower = jnp.where(c_i < r_i, 1.0, 0.0).astype(BF16)
    upper = jnp.where(r_i < c_i, 1.0, 0.0).astype(BF16)
    offs = jnp.zeros((1, LANES), F32)
    ranks = []
    for b in range(t_n // blk):
        ab = a[b * blk:(b + 1) * blk]
        rb = jnp.dot(lower, ab, preferred_element_type=F32)
        ranks.append(rb + offs)
        offs = offs + rb[blk - 1:blk] + ab[blk - 1:blk].astype(F32)
    rank = jnp.concatenate(ranks, axis=0)
    n_chunk = jnp.floor((offs + (MOE_CHUNK - 1)) * (1.0 / MOE_CHUNK))
    start = jnp.dot(jnp.broadcast_to(n_chunk, (8, LANES)).astype(BF16), upper,
                    preferred_element_type=F32)[0:1]
    end = start + n_chunk
    dest = jnp.sum(jnp.where(member, start * MOE_CHUNK + rank, 0.0), axis=-1, keepdims=True)
    tok = lax.broadcasted_iota(jnp.int32, (t_n, LANES), 0)
    digits = jnp.where(lane == 0, (tok // blk).astype(F32),
                       jnp.where(lane == 1, (tok % blk).astype(F32), jnp.where(lane == 2, 1.0, 0.0))).astype(BF16)
    sw = 512
    pieces = []
    for sc in range(MOE_ROWS // sw):
        s_id = (lax.broadcasted_iota(jnp.int32, (t_n, sw), 1) + sc * sw).astype(F32)
        hit = jnp.where(dest == s_id, 1.0, 0.0).astype(BF16)
        r = lax.dot_general(digits, hit, (((0,), (0,)), ((), ())), preferred_element_type=F32)
        tok_of = r[0:1] * blk + r[1:2]
        pieces.append(jnp.where(r[2:3] > 0.5, tok_of, float(t_n)))
    perm = jnp.concatenate(pieces, axis=1)
    slot = lax.broadcasted_iota(jnp.int32, (1, MOE_ROWS), 1).astype(F32)
    lane1 = lax.broadcasted_iota(jnp.int32, (1, LANES), 1)
    cg = jnp.zeros((1, MOE_ROWS), F32)
    for g in range(N_GROUPS):
        end_g = jnp.sum(jnp.where(lane1 == g, end, 0.0), axis=-1, keepdims=True)
        cg = cg + jnp.where(slot >= end_g, 1.0, 0.0)
    meta_ref[...] = jnp.concatenate([perm, cg, jnp.zeros((6, MOE_ROWS), F32)], axis=0).astype(jnp.int32)


def _route(h2x):
    n = h2x.shape[0]
    n_tiles = n // MOE_TILE
    return pl.pallas_call(
        _route_kernel,
        grid=(n_tiles,),
        in_specs=[pl.BlockSpec((MOE_TILE, LANES), lambda i: (i, D_MODEL // LANES))],
        out_specs=pl.BlockSpec((None, 8, MOE_ROWS), lambda i: (i, 0, 0)),
        out_shape=jax.ShapeDtypeStruct((n_tiles, 8, MOE_ROWS), jnp.int32),
        compiler_params=_cparams(("parallel",)),
        name="moe_route",
    )(h2x)


ROW_UNROLL = 8


def _moe_kernel(perm_ref, cg_ref, h_ref, wg_ref, wu_ref, wd_ref, y_ref, hs_ref, ys_ref):
    i = pl.program_id(0)
    s = pl.program_id(1)

    @pl.when(s == 0)
    def _():
        y_ref[MOE_TILE:MOE_TILE + 8, :] = jnp.zeros((8, D_MODEL), F32)

    @pl.when(cg_ref[i * MOE_SLOTS + s] < N_GROUPS)
    def _():
        base = i * MOE_ROWS + s * MOE_CHUNK

        def gather(rb, carry):
            r0 = rb * ROW_UNROLL
            rows = [h_ref[pl.ds(jnp.minimum(perm_ref[base + r0 + u], MOE_TILE - 1), 1), :]
                    for u in range(ROW_UNROLL)]
            for u in range(ROW_UNROLL):
                hs_ref[pl.ds(r0 + u, 1), :] = rows[u]
            return carry

        lax.fori_loop(0, MOE_CHUNK // ROW_UNROLL, gather, 0)
        hb = hs_ref[:, 0:D_MODEL].astype(BF16)
        cw = hs_ref[:, D_MODEL:H2X_W]
        acc = None
        for e in range(EXPERTS_PER_GROUP):
            a = jnp.dot(hb, wg_ref[e], preferred_element_type=F32)
            u = jnp.dot(hb, wu_ref[e], preferred_element_type=F32)
            hid = (_silu(a) * u * cw[:, e:e + 1]).astype(BF16)
            t = jnp.dot(hid, wd_ref[e], preferred_element_type=F32)
            acc = t if acc is None else acc + t
        ys_ref[...] = acc

        def scatter(rb, carry):
            r0 = rb * ROW_UNROLL
            rows = [ys_ref[pl.ds(r0 + u, 1), :] for u in range(ROW_UNROLL)]
            for u in range(ROW_UNROLL):
                y_ref[pl.ds(perm_ref[base + r0 + u], 1), :] = rows[u]
            return carry

        lax.fori_loop(0, MOE_CHUNK // ROW_UNROLL, scatter, 0)


def _moe(h2x, perm, cgrp, wg, wu, wd):
    n = h2x.shape[0]
    n_tiles = n // MOE_TILE

    def w_idx(i, s, perm_ref, cg_ref):
        return (jnp.minimum(cg_ref[i * MOE_SLOTS + s], N_GROUPS - 1), 0, 0)

    return pl.pallas_call(
        _moe_kernel,
        grid_spec=pltpu.PrefetchScalarGridSpec(
            num_scalar_prefetch=2,
            grid=(n_tiles, MOE_SLOTS),
            in_specs=[pl.BlockSpec((MOE_TILE, H2X_W), lambda i, s, p, c: (i, 0)),
                      pl.BlockSpec((EXPERTS_PER_GROUP, D_MODEL, EXPERT_FF), w_idx),
                      pl.BlockSpec((EXPERTS_PER_GROUP, D_MODEL, EXPERT_FF), w_idx),
                      pl.BlockSpec((EXPERTS_PER_GROUP, EXPERT_FF, D_MODEL), w_idx)],
            out_specs=pl.BlockSpec((None, MOE_TILE + 8, D_MODEL), lambda i, s, p, c: (i, 0, 0)),
            scratch_shapes=[pltpu.VMEM((MOE_CHUNK, H2X_W), F32), pltpu.VMEM((MOE_CHUNK, D_MODEL), F32)]),
        out_shape=jax.ShapeDtypeStruct((n_tiles, MOE_TILE + 8, D_MODEL), F32),
        compiler_params=_cparams(("parallel", "arbitrary")),
        name="moe_experts",
    )(perm, cgrp, h2x, wg, wu, wd)


def _final_kernel(y_ref, x1_ref, mods_ref, fg_ref, o_ref):
    gate2 = mods_ref[:, 5 * D_MODEL:6 * D_MODEL]
    x2 = x1_ref[...] + gate2 * y_ref[...]
    o_ref[...] = x2 * lax.rsqrt(jnp.mean(x2 * x2, axis=-1, keepdims=True) + EPS) * fg_ref[...]


def _final(y, x1, mods3, mod_row0, mod_tokens, fg):
    n = x1.shape[0]
    tm = 512
    per_mod = mod_tokens // tm
    per_tile = MOE_TILE // tm
    return pl.pallas_call(
        _final_kernel,
        grid=(n // tm,),
        in_specs=[pl.BlockSpec((None, tm, D_MODEL), lambda j: (j // per_tile, j % per_tile, 0)),
                  pl.BlockSpec((tm, D_MODEL), lambda j: (j, 0)),
                  pl.BlockSpec((None, 1, 6 * D_MODEL), lambda j: (mod_row0 + j // per_mod, 0, 0)),
                  pl.BlockSpec((1, D_MODEL), lambda j: (0, 0))],
        out_specs=pl.BlockSpec((tm, D_MODEL), lambda j: (j, 0)),
        out_shape=jax.ShapeDtypeStruct((n, D_MODEL), F32),
        compiler_params=_cparams(("parallel",)),
        name="final_norm",
    )(y, x1, mods3, fg)


def _rope_tables(t):
    n_freq = QK_DIM // 4
    freqs = ROPE_BASE ** (-jnp.arange(n_freq, dtype=F32) / n_freq)
    pos = jnp.arange(t)
    row = (pos // GRID_W).astype(F32)
    col = (pos % GRID_W).astype(F32)
    j = jnp.arange(LANES) % QK_DIM
    ang = jnp.where((j < QK_DIM // 2)[None, :], row[:, None], col[:, None]) * freqs[j % n_freq][None, :]
    first = ((j % (QK_DIM // 2)) < n_freq)[None, :]
    sin = jnp.sin(ang)
    return jnp.cos(ang), jnp.where(first, -sin, 0.0), jnp.where(first, 0.0, sin)


def _layer(x, mods3, mod_row0, mod_tokens, rope_tabs, ctx_k, ctx_v, h0f, h0b, lw, layer):
    b, t, _ = x.shape
    n = b * t
    xf = x.reshape(n, D_MODEL)
    q, k, v, z, xbc, dt = _inproj(xf, mods3, mod_row0, mod_tokens, t, lw["norm_mix_g"], lw["w_main"],
                                  lw["w_dt"], rope_tabs)
    k3 = k.reshape(b, t, D_QK)
    v3 = v.reshape(b, t, D_ATT)
    if ctx_k is None:
        k_all, v_all = k3.astype(BF16), v3.astype(BF16)
    else:
        k_all = jnp.concatenate([k3.astype(BF16), ctx_k.astype(BF16)], axis=1)
        v_all = jnp.concatenate([v3.astype(BF16), ctx_v.astype(BF16)], axis=1)
    lam0 = 0.8 - 0.6 * math.exp(-0.3 * layer)
    tq = 256
    tk = 512 if k_all.shape[1] % 512 == 0 else 256
    att = _attention(q.reshape(b, t, D_QK), k_all, jnp.swapaxes(v_all, 1, 2), lw["lamp"], lw["attn_subln_g"],
                     lam0, tq, tk)
    xbc3 = xbc.reshape(b, t, XBC_DIM)
    dt3 = dt.reshape(b, t, LANES)
    y_f, hf = _ssd(False, xbc3, dt3, h0f, lw["conv_w"], lw["conv_b"], lw["alog_f"], lw["dtb_f"], None)
    ssm, hb = _ssd(True, xbc3, dt3, h0b, lw["conv_w"], lw["conv_b"], lw["alog_b"], lw["dtb_b"],
                   (y_f, z.reshape(b, t, D_SSM), lw["d_x"], lw["ssm_norm_g"]))
    x1, h2x = _outproj(att.reshape(n, D_ATT), ssm.reshape(n, D_SSM), xf, mods3, mod_row0, mod_tokens,
                       lw["wo_att"], lw["wo_ssm"], lw["norm_ffn_g"], lw["w_router"], lw["b_router"])
    meta = _route(h2x)
    perm = meta[:, 0, :].reshape(-1)
    cgrp = meta[:, 1, :MOE_SLOTS].reshape(-1)
    y = _moe(h2x, perm, cgrp, lw["wg"], lw["wu"], lw["wd"])
    out = _final(y, x1, mods3, mod_row0, mod_tokens, lw["final_g"])
    return out.reshape(b, t, D_MODEL), k3, v3, hf, hb


def _pad_lanes(v, width=LANES):
    return jnp.pad(v, [(0, 0)] * (v.ndim - 1) + [(0, width - v.shape[-1])])


def kernel(x_prompt, x_sample, cache_k, cache_v, state_ssm_fwd, state_ssm_bwd, c, c_ctx, w_ada, b_ada, norm_mix_g, w_in, w_out, lambda_q1, lambda_k1, lambda_q2, lambda_k2, attn_subln_g, conv_w, conv_b, a_log_fwd, a_log_bwd, dt_bias_fwd, dt_bias_bwd, ssm_d, ssm_norm_g, norm_ffn_g, w_group_router, b_group_router, w_expert_router, b_expert_router, w_exp_gate, w_exp_up, w_exp_down, final_norm_g):
    depth = w_in.shape[0]
    assert depth == 1, "single trunk layer"
    bp, tp, _ = x_prompt.shape
    bs, ts, _ = x_sample.shape
    l = 0
    cond = jnp.concatenate([c_ctx[None], c], axis=0)
    condT = _pad_lanes(cond.T, 8)
    mods = _ada(condT, w_ada[l], b_ada[l][None])
    mods3 = mods.reshape(8, 1, 6 * D_MODEL)

    w_router = _pad_lanes(jnp.concatenate([w_group_router[l], w_expert_router[l]], axis=1))
    wr_hi = w_router.astype(BF16)
    wr_lo = (w_router - wr_hi.astype(F32)).astype(BF16)
    lw = dict(
        norm_mix_g=norm_mix_g[l][None],
        w_main=w_in[l][:, :MAIN_COLS].astype(BF16),
        w_dt=_pad_lanes(w_in[l][:, MAIN_COLS:]).astype(BF16),
        lamp=jnp.stack([lambda_q1[l], lambda_k1[l], lambda_q2[l], lambda_k2[l]]),
        attn_subln_g=attn_subln_g[l][None],
        conv_w=conv_w[l], conv_b=conv_b[l][None],
        alog_f=_pad_lanes(a_log_fwd[l][None]), alog_b=_pad_lanes(a_log_bwd[l][None]),
        dtb_f=_pad_lanes(dt_bias_fwd[l][None]), dtb_b=_pad_lanes(dt_bias_bwd[l][None]),
        d_x=jnp.repeat(ssm_d[l], SSM_HEADDIM)[None], ssm_norm_g=ssm_norm_g[l][None],
        wo_att=w_out[l][:D_ATT].astype(BF16), wo_ssm=w_out[l][D_ATT:].astype(BF16),
        norm_ffn_g=norm_ffn_g[l][None],
        w_router=jnp.stack([wr_hi, wr_lo]),
        b_router=_pad_lanes(jnp.concatenate([b_group_router[l], b_expert_router[l]])[None]),
        wg=w_exp_gate[l].astype(BF16), wu=w_exp_up[l].astype(BF16), wd=w_exp_down[l].astype(BF16),
        final_g=final_norm_g[None],
    )
    n_state = SSM_HEADS * SSM_HEADDIM
    zeros_state = jnp.zeros((bp, n_state, D_STATE), F32)
    yp, ck, cv, hf, hb = _layer(x_prompt, mods3, 0, bp * tp, None, None, None, zeros_state, zeros_state, lw, l)
    ys, _, _, _, _ = _layer(x_sample, mods3, 1, ts, _rope_tables(ts),
                            cache_k[:, l].reshape(bs, -1, D_QK), cache_v[:, l].reshape(bs, -1, D_ATT),
                            state_ssm_fwd[:, l].reshape(bs, n_state, D_STATE),
                            state_ssm_bwd[:, l].reshape(bs, n_state, D_STATE), lw, l)
    new_k = ck.reshape(bp, 1, tp, ATT_HEADS, 2, QK_DIM)
    new_v = cv.reshape(bp, 1, tp, ATT_HEADS, V_DIM)
    new_hf = hf.reshape(bp, 1, SSM_HEADS, SSM_HEADDIM, D_STATE)
    new_hb = hb.reshape(bp, 1, SSM_HEADS, SSM_HEADDIM, D_STATE)
    return yp, ys, new_k, new_v, new_hf, new_hb
```

```python
import functools
import math

import jax
import jax.numpy as jnp
from jax import lax
from jax.experimental import pallas as pl
from jax.experimental.pallas import tpu as pltpu

D_MODEL = 1024
GRID_W = 64
ATT_HEADS = 4
QK_DIM = 64
V_DIM = 128
D_QK = 512
D_ATT = 512
ROPE_BASE = 10000.0
D_SSM = 512
SSM_HEADDIM = 64
SSM_HEADS = 8
SSM_GROUPS = 2
D_STATE = 128
CONV_W = 5
CHUNK = 128
XBC_DIM = 1024
N_GROUPS = 4
EXPERTS_PER_GROUP = 4
N_EXPERTS = 16
EXPERT_FF = 256
EPS = 1e-6
MAIN_COLS = 2 * D_QK + D_ATT + D_SSM + XBC_DIM
H2X_W = D_MODEL + 128
MOE_TILE = 2048
MOE_CHUNK = 256
MOE_SLOTS = MOE_TILE // MOE_CHUNK + N_GROUPS
MOE_ROWS = MOE_SLOTS * MOE_CHUNK
LANES = 128
HALO = 8
VMEM_LIMIT = 56 * 1024 * 1024

LOG2E = math.log2(math.e)
SUM_ROWS = 16
F32 = jnp.float32
BF16 = jnp.bfloat16


def _cparams(sem):
    return pltpu.CompilerParams(dimension_semantics=sem, vmem_limit_bytes=VMEM_LIMIT)


def _sigmoid(x):
    return 1.0 / (1.0 + jnp.exp(-x))


def _silu(x):
    return x * _sigmoid(x)


def _ada_kernel(condT_ref, w_ref, b_ref, o_ref):
    s = _silu(condT_ref[...])
    w = w_ref[...]
    b = b_ref[...]
    o_ref[...] = jnp.zeros_like(o_ref)
    for r in range(3):
        o_ref[r:r + 1, :] = jnp.sum(w * s[:, r:r + 1], axis=0, keepdims=True) + b


def _ada(condT, w_ada, b_ada):
    bn = 512
    n = w_ada.shape[1]
    return pl.pallas_call(
        _ada_kernel,
        grid=(n // bn,),
        in_specs=[pl.BlockSpec((D_MODEL, 8), lambda j: (0, 0)),
                  pl.BlockSpec((D_MODEL, bn), lambda j: (0, j)),
                  pl.BlockSpec((1, bn), lambda j: (0, j))],
        out_specs=pl.BlockSpec((8, bn), lambda j: (0, j)),
        out_shape=jax.ShapeDtypeStruct((8, n), F32),
        compiler_params=_cparams(("arbitrary",)),
        name="ada",
    )(condT, w_ada, b_ada)


def _inproj_kernel(rope, x_ref, mods_ref, g_ref, w_ref, wdt_ref, *rest):
    if rope:
        cos_ref, sa_ref, sb_ref, q_ref, kb_ref, vt_ref, z_ref, xbc_ref, dt_ref = rest
    else:
        q_ref, kb_ref, vt_ref, z_ref, xbc_ref, dt_ref, kf_ref, vf_ref = rest
    x = x_ref[...]
    shift = mods_ref[:, 0:D_MODEL]
    scale = mods_ref[:, D_MODEL:2 * D_MODEL]
    y = x * lax.rsqrt(jnp.mean(x * x, axis=-1, keepdims=True) + EPS) * g_ref[...]
    h = (y * (1.0 + scale) + shift).astype(BF16)
    r = jnp.dot(h, w_ref[...], preferred_element_type=F32)
    dt_ref[...] = jnp.dot(h, wdt_ref[...], preferred_element_type=F32)
    q = r[:, 0:D_QK]
    k = r[:, D_QK:2 * D_QK]
    if rope:
        cos = cos_ref[...]
        sa = sa_ref[...]
        sb = sb_ref[...]

        def rot(t):
            parts = []
            for hh in range(ATT_HEADS):
                th = t[:, hh * LANES:(hh + 1) * LANES]
                parts.append(th * cos + pltpu.roll(th, LANES - 16, 1) * sa + pltpu.roll(th, 16, 1) * sb)
            return jnp.concatenate(parts, axis=1)

        q = rot(q)
        k = rot(k)
    v = r[:, 2 * D_QK:2 * D_QK + D_ATT]
    q_ref[...] = q
    kb_ref[...] = k.astype(BF16)
    vt_ref[...] = v.T.astype(BF16)
    if not rope:
        kf_ref[...] = k
        vf_ref[...] = v
    z_ref[...] = r[:, 2 * D_QK + D_ATT:2 * D_QK + D_ATT + D_SSM]
    xbc_ref[...] = r[:, 2 * D_QK + D_ATT + D_SSM:MAIN_COLS]


def _inproj(x, mods3, mod_row0, mod_tokens, seq_len, g, w_main, w_dt, rope_tabs):
    n = x.shape[0]
    tm = 256
    per_seq = seq_len // tm
    per_mod = mod_tokens // tm
    rope = rope_tabs is not None
    in_specs = [pl.BlockSpec((tm, D_MODEL), lambda i: (i, 0)),
                pl.BlockSpec((None, 1, 6 * D_MODEL), lambda i: (mod_row0 + i // per_mod, 0, 0)),
                pl.BlockSpec((1, D_MODEL), lambda i: (0, 0)),
                pl.BlockSpec((D_MODEL, MAIN_COLS), lambda i: (0, 0)),
                pl.BlockSpec((D_MODEL, LANES), lambda i: (0, 0))]
    args = [x, mods3, g, w_main, w_dt]
    if rope:
        tab_spec = pl.BlockSpec((tm, LANES), lambda i: (i % per_seq, 0))
        in_specs += [tab_spec] * 3
        args += list(rope_tabs)
    def rows(wd, dtype=F32):
        return pl.BlockSpec((tm, wd), lambda i: (i, 0)), jax.ShapeDtypeStruct((n, wd), dtype)

    outs = [rows(D_QK), rows(D_QK, BF16),
            (pl.BlockSpec((None, D_ATT, tm), lambda i: (i // per_seq, 0, i % per_seq)),
             jax.ShapeDtypeStruct((n // seq_len, D_ATT, seq_len), BF16)),
            rows(D_SSM), rows(XBC_DIM), rows(LANES)]
    if not rope:
        outs += [rows(D_QK), rows(D_ATT)]
    return pl.pallas_call(
        functools.partial(_inproj_kernel, rope),
        grid=(n // tm,),
        in_specs=in_specs,
        out_specs=[o[0] for o in outs],
        out_shape=[o[1] for o in outs],
        compiler_params=_cparams(("parallel",)),
        name="inproj_rope" if rope else "inproj",
    )(*args)


def _attn_kernel(tk, lam0, has_cache, q_ref, k_ref, vt_ref, *rest):
    if has_cache:
        ck_ref, cvt_ref, lamp_ref, g_ref, o_ref = rest
    else:
        lamp_ref, g_ref, o_ref = rest
    tq = q_ref.shape[0]
    chunks = [(k_ref, vt_ref, c * tk) for c in range(k_ref.shape[0] // tk)]
    if has_cache:
        chunks += [(ck_ref, cvt_ref, c * tk) for c in range(ck_ref.shape[0] // tk)]
    lp = lamp_ref[...]
    lam = (jnp.exp(jnp.sum(lp[0:1] * lp[1:2], axis=-1, keepdims=True))
           - jnp.exp(jnp.sum(lp[2:3] * lp[3:4], axis=-1, keepdims=True)) + lam0)
    q = q_ref[...] * (QK_DIM ** -0.5 * LOG2E)
    lane = lax.broadcasted_iota(jnp.int32, q.shape, 1)
    qq_t = jnp.concatenate([jnp.where(lane < QK_DIM, q, 0.0), jnp.where(lane >= QK_DIM, q, 0.0)],
                           axis=0).T.astype(BF16)

    def scores(chunk):
        kr, _, start = chunk
        return jnp.dot(kr[start:start + tk, :], qq_t, preferred_element_type=F32)

    ones_rows = jnp.ones((SUM_ROWS, tk), BF16)

    def update(s, chunk, m, acc):
        _, vr, start = chunk
        m_new = jnp.maximum(m, jnp.max(s, axis=0, keepdims=True))
        alpha = jnp.exp2(m - m_new)
        p = jnp.exp2(s - m_new).astype(BF16)
        v_ext = jnp.concatenate([vr[:, start:start + tk], ones_rows], axis=0)
        acc = alpha * acc + jnp.dot(v_ext, p, preferred_element_type=F32)
        return m_new, acc

    m = jnp.full((1, 2 * tq), -jnp.inf, F32)
    acc = jnp.zeros((V_DIM + SUM_ROWS, 2 * tq), F32)
    s = scores(chunks[0])
    for c, chunk in enumerate(chunks):
        s_next = scores(chunks[c + 1]) if c + 1 < len(chunks) else None
        m, acc = update(s, chunk, m, acc)
        s = s_next
    o = acc[0:V_DIM] / acc[V_DIM:V_DIM + 1]
    o = (o[:, 0:tq] - lam * o[:, tq:2 * tq]).T
    o = o * lax.rsqrt(jnp.mean(o * o, axis=-1, keepdims=True) + EPS)
    o_ref[...] = o * g_ref[...] * (1.0 - lam0)


def _attention(q, k, vt, cache, lamp, g, lam0, tq, tk):
    b, t, _ = q.shape

    def kv_specs(length):
        return [pl.BlockSpec((None, length, LANES), lambda bi, h, i: (bi, 0, h)),
                pl.BlockSpec((None, LANES, length), lambda bi, h, i: (bi, h, 0))]

    in_specs = [pl.BlockSpec((None, tq, LANES), lambda bi, h, i: (bi, i, h))] + kv_specs(t)
    args = [q, k, vt]
    if cache is not None:
        in_specs += kv_specs(cache[0].shape[1])
        args += list(cache)
    in_specs += [pl.BlockSpec((4, QK_DIM), lambda bi, h, i: (0, 0)),
                 pl.BlockSpec((1, V_DIM), lambda bi, h, i: (0, 0))]
    return pl.pallas_call(
        functools.partial(_attn_kernel, tk, lam0, cache is not None),
        grid=(b, ATT_HEADS, t // tq),
        in_specs=in_specs,
        out_specs=pl.BlockSpec((None, tq, LANES), lambda bi, h, i: (bi, i, h)),
        out_shape=jax.ShapeDtypeStruct((b, t, D_ATT), F32),
        compiler_params=_cparams(("parallel", "parallel", "arbitrary")),
        name="diff_attn",
    )(*args, lamp, g)


def _expand_heads(v):
    lane = lax.broadcasted_iota(jnp.int32, (v.shape[0], LANES), 1)
    parts = []
    for pr in range(SSM_HEADS // 2):
        a = jnp.broadcast_to(v[:, 2 * pr:2 * pr + 1], (v.shape[0], LANES))
        b = jnp.broadcast_to(v[:, 2 * pr + 1:2 * pr + 2], (v.shape[0], LANES))
        parts.append(jnp.where(lane < SSM_HEADDIM, a, b))
    return jnp.concatenate(parts, axis=1)


def _ssd_kernel(reverse, nc, xc_ref, xp_ref, xn_ref, dt_ref, h0_ref, cw_ref, cb_ref, alog_ref, dtb_ref,
                *rest):
    if reverse:
        yf_ref, z_ref, d_ref, ng_ref, y_ref, hout_ref, state_ref = rest
    else:
        y_ref, hout_ref, state_ref = rest
    c = pl.program_id(1)
    chunk = (nc - 1 - c) if reverse else c

    @pl.when(c == 0)
    def _():
        state_ref[...] = h0_ref[...]

    prev = jnp.where(chunk > 0, xp_ref[...], 0.0)
    nxt = jnp.where(chunk < nc - 1, xn_ref[...], 0.0)
    xpad = jnp.concatenate([prev, xc_ref[...], nxt], axis=0)
    cw = cw_ref[...]
    conv = cb_ref[...]
    for kk in range(CONV_W):
        off = HALO - CONV_W // 2 + kk
        conv = conv + xpad[off:off + CHUNK, :] * cw[kk:kk + 1, :]
    xbc = _silu(conv)
    xs = xbc[:, 0:D_SSM]
    bm = xbc[:, D_SSM:D_SSM + SSM_GROUPS * D_STATE].astype(BF16)
    cm = xbc[:, D_SSM + SSM_GROUPS * D_STATE:XBC_DIM].astype(BF16)

    dt_raw = dt_ref[...]
    if reverse:
        dt_raw = pltpu.roll(dt_raw, LANES - SSM_HEADS, 1)
    xv = dt_raw + dtb_ref[...]
    dt = jnp.maximum(xv, 0.0) + jnp.log(1.0 + jnp.exp(-jnp.abs(xv)))
    la = dt * (-jnp.exp(alog_ref[...]))
    row = lax.broadcasted_iota(jnp.int32, (CHUNK, LANES), 0)
    col = lax.broadcasted_iota(jnp.int32, (CHUNK, LANES), 1)
    cum = la
    sh = 1
    while sh < CHUNK:
        if reverse:
            cum = cum + jnp.where(row < CHUNK - sh, pltpu.roll(cum, CHUNK - sh, 0), 0.0)
        else:
            cum = cum + jnp.where(row >= sh, pltpu.roll(cum, sh, 0), 0.0)
        sh *= 2
    end = CHUNK - 1 if not reverse else 0
    cum_end = cum[end:end + 1, :]
    cum_t = cum.T
    causal = (row <= col) if reverse else (row >= col)

    dt_x = _expand_heads(dt)
    xd = xs * dt_x
    xdw = (xd * _expand_heads(jnp.exp(cum_end - cum))).astype(BF16)
    xd = xd.astype(BF16)
    ecum_x = _expand_heads(jnp.exp(cum))
    lane = col
    rep = SSM_HEADS // SSM_GROUPS
    state = state_ref[...]
    y_parts = []
    new_state = []
    for g in range(SSM_GROUPS):
        bg = bm[:, g * D_STATE:(g + 1) * D_STATE]
        cg = cm[:, g * D_STATE:(g + 1) * D_STATE]
        cbt = lax.dot_general(cg, bg, (((1,), (1,)), ((), ())), preferred_element_type=F32)
        rows = slice(g * rep * SSM_HEADDIM, (g + 1) * rep * SSM_HEADDIM)
        st_g = state[rows, :]
        y_off = lax.dot_general(cg, st_g.astype(BF16), (((1,), (1,)), ((), ())),
                                preferred_element_type=F32)
        cst = lax.dot_general(xdw[:, rows], bg, (((0,), (0,)), ((), ())), preferred_element_type=F32)
        dec = jnp.exp(cum_end)
        for pr in range(rep // 2):
            y_pair = None
            for sub in range(2):
                h = g * rep + 2 * pr + sub
                seg = cum[:, h:h + 1] - cum_t[h:h + 1, :]
                decay = jnp.exp(jnp.where(causal, seg, -jnp.inf))
                sc = (cbt * decay).astype(BF16)
                xd_pair = xd[:, (h // 2) * LANES:(h // 2 + 1) * LANES]
                keep = (lane < SSM_HEADDIM) if sub == 0 else (lane >= SSM_HEADDIM)
                t = jnp.dot(sc, jnp.where(keep, xd_pair, jnp.zeros_like(xd_pair)), preferred_element_type=F32)
                y_pair = t if y_pair is None else y_pair + t
            y_parts.append(y_pair)
        dec_rows = jnp.concatenate(
            [jnp.broadcast_to(dec[:, g * rep + hh:g * rep + hh + 1], (SSM_HEADDIM, D_STATE)) for hh in range(rep)],
            axis=0)
        new_state.append(st_g * dec_rows + cst)
        y_parts[-2] = y_parts[-2] + y_off[:, 0:LANES] * ecum_x[:, rows][:, 0:LANES]
        y_parts[-1] = y_parts[-1] + y_off[:, LANES:2 * LANES] * ecum_x[:, rows][:, LANES:2 * LANES]
    y = jnp.concatenate(y_parts, axis=1)
    state = jnp.concatenate(new_state, axis=0)
    state_ref[...] = state

    @pl.when(c == nc - 1)
    def _():
        hout_ref[...] = state

    if reverse:
        y = y + yf_ref[...] + xs * d_ref[...]
        y = y * _silu(z_ref[...])
        y = y * lax.rsqrt(jnp.mean(y * y, axis=-1, keepdims=True) + EPS)
        y_ref[...] = y * ng_ref[...]
    else:
        y_ref[...] = y


def _ssd(reverse, xbc, dt, h0, conv_w, conv_b, alog, dtb, extra):
    b, l, _ = xbc.shape
    nc = l // CHUNK
    hb = CHUNK // HALO

    def ch(ci):
        return (nc - 1 - ci) if reverse else ci

    in_specs = [pl.BlockSpec((None, CHUNK, XBC_DIM), lambda bi, ci: (bi, ch(ci), 0)),
                pl.BlockSpec((None, HALO, XBC_DIM), lambda bi, ci: (bi, jnp.maximum(ch(ci) * hb - 1, 0), 0)),
                pl.BlockSpec((None, HALO, XBC_DIM),
                             lambda bi, ci: (bi, jnp.minimum((ch(ci) + 1) * hb, l // HALO - 1), 0)),
                pl.BlockSpec((None, CHUNK, LANES), lambda bi, ci: (bi, ch(ci), 0)),
                pl.BlockSpec((None, SSM_HEADS * SSM_HEADDIM, D_STATE), lambda bi, ci: (bi, 0, 0)),
                pl.BlockSpec((CONV_W, XBC_DIM), lambda bi, ci: (0, 0)),
                pl.BlockSpec((1, XBC_DIM), lambda bi, ci: (0, 0)),
                pl.BlockSpec((1, LANES), lambda bi, ci: (0, 0)),
                pl.BlockSpec((1, LANES), lambda bi, ci: (0, 0))]
    args = [xbc, xbc, xbc, dt, h0, conv_w, conv_b, alog, dtb]
    if reverse:
        y_fwd, z, d_x, norm_g = extra
        in_specs += [pl.BlockSpec((None, CHUNK, D_SSM), lambda bi, ci: (bi, ch(ci), 0)),
                     pl.BlockSpec((None, CHUNK, D_SSM), lambda bi, ci: (bi, ch(ci), 0)),
                     pl.BlockSpec((1, D_SSM), lambda bi, ci: (0, 0)),
                     pl.BlockSpec((1, D_SSM), lambda bi, ci: (0, 0))]
        args += [y_fwd, z, d_x, norm_g]
    return pl.pallas_call(
        functools.partial(_ssd_kernel, reverse, nc),
        grid=(b, nc),
        in_specs=in_specs,
        out_specs=[pl.BlockSpec((None, CHUNK, D_SSM), lambda bi, ci: (bi, ch(ci), 0)),
                   pl.BlockSpec((None, SSM_HEADS * SSM_HEADDIM, D_STATE), lambda bi, ci: (bi, 0, 0))],
        out_shape=[jax.ShapeDtypeStruct((b, l, D_SSM), F32),
                   jax.ShapeDtypeStruct((b, SSM_HEADS * SSM_HEADDIM, D_STATE), F32)],
        scratch_shapes=[pltpu.VMEM((SSM_HEADS * SSM_HEADDIM, D_STATE), F32)],
        compiler_params=_cparams(("parallel", "arbitrary")),
        name="ssd_bwd" if reverse else "ssd_fwd",
    )(*args)


def _outproj_kernel(att_ref, ssm_ref, x_ref, mods_ref, wa_ref, ws_ref, g_ref, wr_ref, br_ref,
                    x1_ref, h2x_ref):
    mix = (jnp.dot(att_ref[...].astype(BF16), wa_ref[...], preferred_element_type=F32)
           + jnp.dot(ssm_ref[...].astype(BF16), ws_ref[...], preferred_element_type=F32))
    gate1 = mods_ref[:, 2 * D_MODEL:3 * D_MODEL]
    shift2 = mods_ref[:, 3 * D_MODEL:4 * D_MODEL]
    scale2 = mods_ref[:, 4 * D_MODEL:5 * D_MODEL]
    x1 = x_ref[...] + gate1 * mix
    x1_ref[...] = x1
    y = x1 * lax.rsqrt(jnp.mean(x1 * x1, axis=-1, keepdims=True) + EPS) * g_ref[...]
    h2 = y * (1.0 + scale2) + shift2
    h_hi = h2.astype(BF16)
    h2x_ref[:, 0:D_MODEL] = h2
    h_lo = (h2 - h_hi.astype(F32)).astype(BF16)
    w_hi = wr_ref[0]
    w_lo = wr_ref[1]
    logits = (jnp.dot(h_hi, w_hi, preferred_element_type=F32) + jnp.dot(h_lo, w_hi, preferred_element_type=F32)
              + jnp.dot(h_hi, w_lo, preferred_element_type=F32)) + br_ref[...]
    lane = lax.broadcasted_iota(jnp.int32, logits.shape, 1)
    neg = -jnp.inf
    big = jnp.int32(1 << 20)
    is_g = lane < N_GROUPS
    gl = jnp.where(is_g, logits, neg)
    gmax = jnp.max(gl, axis=-1, keepdims=True)
    g_idx = jnp.min(jnp.where(gl == gmax, lane, big), axis=-1, keepdims=True)
    p_g = 1.0 / jnp.sum(jnp.where(is_g, jnp.exp(gl - gmax), 0.0), axis=-1, keepdims=True)
    e_lo = N_GROUPS + g_idx * EXPERTS_PER_GROUP
    in_grp = (lane >= e_lo) & (lane < e_lo + EXPERTS_PER_GROUP)
    el = jnp.where(in_grp, logits, neg)
    m1 = jnp.max(el, axis=-1, keepdims=True)
    i1 = jnp.min(jnp.where(el == m1, lane, big), axis=-1, keepdims=True)
    el2 = jnp.where(lane == i1, neg, el)
    m2 = jnp.max(el2, axis=-1, keepdims=True)
    i2 = jnp.min(jnp.where(el2 == m2, lane, big), axis=-1, keepdims=True)
    e2 = jnp.exp(m2 - m1)
    w1 = p_g / (1.0 + e2)
    w2 = p_g * e2 / (1.0 + e2)
    cw = jnp.where(lane == i1, w1, 0.0) + jnp.where(lane == i2, w2, 0.0)
    slab = jnp.where(lane == EXPERTS_PER_GROUP, g_idx.astype(F32), 0.0)
    for j in range(EXPERTS_PER_GROUP):
        wj = jnp.sum(jnp.where(lane == e_lo + j, cw, 0.0), axis=-1, keepdims=True)
        slab = slab + jnp.where(lane == j, wj, 0.0)
    h2x_ref[:, D_MODEL:H2X_W] = slab


def _outproj(att, ssm, x, mods3, mod_row0, mod_tokens, wo_att, wo_ssm, g, w_router, b_router):
    n = x.shape[0]
    tm = 256
    per_mod = mod_tokens // tm
    return pl.pallas_call(
        _outproj_kernel,
        grid=(n // tm,),
        in_specs=[pl.BlockSpec((tm, D_ATT), lambda i: (i, 0)),
                  pl.BlockSpec((tm, D_SSM), lambda i: (i, 0)),
                  pl.BlockSpec((tm, D_MODEL), lambda i: (i, 0)),
                  pl.BlockSpec((None, 1, 6 * D_MODEL), lambda i: (mod_row0 + i // per_mod, 0, 0)),
                  pl.BlockSpec((D_ATT, D_MODEL), lambda i: (0, 0)),
                  pl.BlockSpec((D_SSM, D_MODEL), lambda i: (0, 0)),
                  pl.BlockSpec((1, D_MODEL), lambda i: (0, 0)),
                  pl.BlockSpec((2, D_MODEL, LANES), lambda i: (0, 0, 0)),
                  pl.BlockSpec((1, LANES), lambda i: (0, 0))],
        out_specs=[pl.BlockSpec((tm, D_MODEL), lambda i: (i, 0)),
                   pl.BlockSpec((tm, H2X_W), lambda i: (i, 0))],
        out_shape=[jax.ShapeDtypeStruct((n, D_MODEL), F32),
                   jax.ShapeDtypeStruct((n, H2X_W), F32)],
        compiler_params=_cparams(("parallel",)),
        name="outproj_router",
    )(att, ssm, x, mods3, wo_att, wo_ssm, g, w_router, b_router)


def _route_kernel(slab_ref, meta_ref):
    t_n = MOE_TILE
    blk = LANES
    slab = slab_ref[...]
    lane = lax.broadcasted_iota(jnp.int32, (t_n, LANES), 1)
    gcol = jnp.sum(jnp.where(lane == EXPERTS_PER_GROUP, slab, 0.0), axis=-1, keepdims=True)
    member = (lane.astype(F32) == gcol) & (lane < N_GROUPS)
    a = jnp.where(member, 1.0, 0.0).astype(BF16)
    r_i = lax.broadcasted_iota(jnp.int32, (blk, blk), 0)
    c_i = lax.broadcasted_iota(jnp.int32, (blk, blk), 1)
    lower = jnp.where(c_i < r_i, 1.0, 0.0).astype(BF16)
    upper = jnp.where(r_i < c_i, 1.0, 0.0).astype(BF16)
    offs = jnp.zeros((1, LANES), F32)
    ranks = []
    for b in range(t_n // blk):
        ab = a[b * blk:(b + 1) * blk]
        rb = jnp.dot(lower, ab, preferred_element_type=F32)
        ranks.append(rb + offs)
        offs = offs + rb[blk - 1:blk] + ab[blk - 1:blk].astype(F32)
    rank = jnp.concatenate(ranks, axis=0)
    n_chunk = jnp.floor((offs + (MOE_CHUNK - 1)) * (1.0 / MOE_CHUNK))
    start = jnp.dot(jnp.broadcast_to(n_chunk, (8, LANES)).astype(BF16), upper,
                    preferred_element_type=F32)[0:1]
    end = start + n_chunk
    dest = jnp.sum(jnp.where(member, start * MOE_CHUNK + rank, 0.0), axis=-1, keepdims=True)
    tok = lax.broadcasted_iota(jnp.int32, (t_n, LANES), 0)
    digits = jnp.where(lane == 0, (tok // blk).astype(F32),
                       jnp.where(lane == 1, (tok % blk).astype(F32), jnp.where(lane == 2, 1.0, 0.0))).astype(BF16)
    sw = 512
    pieces = []
    for sc in range(MOE_ROWS // sw):
        s_id = (lax.broadcasted_iota(jnp.int32, (t_n, sw), 1) + sc * sw).astype(F32)
        hit = jnp.where(dest == s_id, 1.0, 0.0).astype(BF16)
        r = lax.dot_general(digits, hit, (((0,), (0,)), ((), ())), preferred_element_type=F32)
        tok_of = r[0:1] * blk + r[1:2]
        pieces.append(jnp.where(r[2:3] > 0.5, tok_of, float(t_n)))
    perm = jnp.concatenate(pieces, axis=1)
    slot = lax.broadcasted_iota(jnp.int32, (1, MOE_ROWS), 1).astype(F32)
    lane1 = lax.broadcasted_iota(jnp.int32, (1, LANES), 1)
    cg = jnp.zeros((1, MOE_ROWS), F32)
    for g in range(N_GROUPS):
        end_g = jnp.sum(jnp.where(lane1 == g, end, 0.0), axis=-1, keepdims=True)
        cg = cg + jnp.where(slot >= end_g, 1.0, 0.0)
    meta_ref[...] = jnp.concatenate([perm, cg, jnp.zeros((6, MOE_ROWS), F32)], axis=0).astype(jnp.int32)


def _route(h2x):
    n = h2x.shape[0]
    n_tiles = n // MOE_TILE
    return pl.pallas_call(
        _route_kernel,
        grid=(n_tiles,),
        in_specs=[pl.BlockSpec((MOE_TILE, LANES), lambda i: (i, D_MODEL // LANES))],
        out_specs=pl.BlockSpec((None, 8, MOE_ROWS), lambda i: (i, 0, 0)),
        out_shape=jax.ShapeDtypeStruct((n_tiles, 8, MOE_ROWS), jnp.int32),
        compiler_params=_cparams(("parallel",)),
        name="moe_route",
    )(h2x)


ROW_UNROLL = 8


def _moe_kernel(perm_ref, cg_ref, h_ref, wg_ref, wu_ref, wd_ref, y_ref, hs_ref, ys_ref):
    i = pl.program_id(0)
    s = pl.program_id(1)

    @pl.when(s == 0)
    def _():
        y_ref[MOE_TILE:MOE_TILE + 8, :] = jnp.zeros((8, D_MODEL), F32)

    @pl.when(cg_ref[i * MOE_SLOTS + s] < N_GROUPS)
    def _():
        base = i * MOE_ROWS + s * MOE_CHUNK

        def gather(rb, carry):
            r0 = rb * ROW_UNROLL
            rows = [h_ref[pl.ds(jnp.minimum(perm_ref[base + r0 + u], MOE_TILE - 1), 1), :]
                    for u in range(ROW_UNROLL)]
            for u in range(ROW_UNROLL):
                hs_ref[pl.ds(r0 + u, 1), :] = rows[u]
            return carry

        lax.fori_loop(0, MOE_CHUNK // ROW_UNROLL, gather, 0)
        hb = hs_ref[:, 0:D_MODEL].astype(BF16)
        cw = hs_ref[:, D_MODEL:H2X_W]
        acc = None
        for e in range(EXPERTS_PER_GROUP):
            a = jnp.dot(hb, wg_ref[e], preferred_element_type=F32)
            u = jnp.dot(hb, wu_ref[e], preferred_element_type=F32)
            hid = (_silu(a) * u * cw[:, e:e + 1]).astype(BF16)
            t = jnp.dot(hid, wd_ref[e], preferred_element_type=F32)
            acc = t if acc is None else acc + t
        ys_ref[...] = acc

        def scatter(rb, carry):
            r0 = rb * ROW_UNROLL
            rows = [ys_ref[pl.ds(r0 + u, 1), :] for u in range(ROW_UNROLL)]
            for u in range(ROW_UNROLL):
                y_ref[pl.ds(perm_ref[base + r0 + u], 1), :] = rows[u]
            return carry

        lax.fori_loop(0, MOE_CHUNK // ROW_UNROLL, scatter, 0)


def _moe(h2x, perm, cgrp, wg, wu, wd):
    n = h2x.shape[0]
    n_tiles = n // MOE_TILE

    def w_idx(i, s, perm_ref, cg_ref):
        return (jnp.minimum(cg_ref[i * MOE_SLOTS + s], N_GROUPS - 1), 0, 0)

    return pl.pallas_call(
        _moe_kernel,
        grid_spec=pltpu.PrefetchScalarGridSpec(
            num_scalar_prefetch=2,
            grid=(n_tiles, MOE_SLOTS),
            in_specs=[pl.BlockSpec((MOE_TILE, H2X_W), lambda i, s, p, c: (i, 0)),
                      pl.BlockSpec((EXPERTS_PER_GROUP, D_MODEL, EXPERT_FF), w_idx),
                      pl.BlockSpec((EXPERTS_PER_GROUP, D_MODEL, EXPERT_FF), w_idx),
                      pl.BlockSpec((EXPERTS_PER_GROUP, EXPERT_FF, D_MODEL), w_idx)],
            out_specs=pl.BlockSpec((None, MOE_TILE + 8, D_MODEL), lambda i, s, p, c: (i, 0, 0)),
            scratch_shapes=[pltpu.VMEM((MOE_CHUNK, H2X_W), F32), pltpu.VMEM((MOE_CHUNK, D_MODEL), F32)]),
        out_shape=jax.ShapeDtypeStruct((n_tiles, MOE_TILE + 8, D_MODEL), F32),
        compiler_params=_cparams(("parallel", "arbitrary")),
        name="moe_experts",
    )(perm, cgrp, h2x, wg, wu, wd)


def _final_kernel(y_ref, x1_ref, mods_ref, fg_ref, o_ref):
    gate2 = mods_ref[:, 5 * D_MODEL:6 * D_MODEL]
    x2 = x1_ref[...] + gate2 * y_ref[...]
    o_ref[...] = x2 * lax.rsqrt(jnp.mean(x2 * x2, axis=-1, keepdims=True) + EPS) * fg_ref[...]


def _final(y, x1, mods3, mod_row0, mod_tokens, fg):
    n = x1.shape[0]
    tm = 512
    per_mod = mod_tokens // tm
    per_tile = MOE_TILE // tm
    return pl.pallas_call(
        _final_kernel,
        grid=(n // tm,),
        in_specs=[pl.BlockSpec((None, tm, D_MODEL), lambda j: (j // per_tile, j % per_tile, 0)),
                  pl.BlockSpec((tm, D_MODEL), lambda j: (j, 0)),
                  pl.BlockSpec((None, 1, 6 * D_MODEL), lambda j: (mod_row0 + j // per_mod, 0, 0)),
                  pl.BlockSpec((1, D_MODEL), lambda j: (0, 0))],
        out_specs=pl.BlockSpec((tm, D_MODEL), lambda j: (j, 0)),
        out_shape=jax.ShapeDtypeStruct((n, D_MODEL), F32),
        compiler_params=_cparams(("parallel",)),
        name="final_norm",
    )(y, x1, mods3, fg)


def _rope_tables(t):
    n_freq = QK_DIM // 4
    freqs = ROPE_BASE ** (-jnp.arange(n_freq, dtype=F32) / n_freq)
    pos = jnp.arange(t)
    row = (pos // GRID_W).astype(F32)
    col = (pos % GRID_W).astype(F32)
    j = jnp.arange(LANES) % QK_DIM
    ang = jnp.where((j < QK_DIM // 2)[None, :], row[:, None], col[:, None]) * freqs[j % n_freq][None, :]
    first = ((j % (QK_DIM // 2)) < n_freq)[None, :]
    sin = jnp.sin(ang)
    return jnp.cos(ang), jnp.where(first, -sin, 0.0), jnp.where(first, 0.0, sin)


def _layer(x, mods3, mod_row0, mod_tokens, rope_tabs, ctx_k, ctx_v, h0f, h0b, lw, layer):
    b, t, _ = x.shape
    n = b * t
    xf = x.reshape(n, D_MODEL)
    res = _inproj(xf, mods3, mod_row0, mod_tokens, t, lw["norm_mix_g"], lw["w_main"], lw["w_dt"], rope_tabs)
    q, kb, vt, z, xbc, dt = res[:6]
    if ctx_k is None:
        cache = None
        k3, v3 = res[6].reshape(b, t, D_QK), res[7].reshape(b, t, D_ATT)
    else:
        cache = (ctx_k.astype(BF16), jnp.swapaxes(ctx_v, 1, 2).astype(BF16))
        k3 = v3 = None
    lam0 = 0.8 - 0.6 * math.exp(-0.3 * layer)
    tq = 512 if t % 512 == 0 else 256
    tk = 512 if t % 512 == 0 else 256
    att = _attention(q.reshape(b, t, D_QK), kb.reshape(b, t, D_QK), vt, cache, lw["lamp"], lw["attn_subln_g"],
                     lam0, tq, tk)
    xbc3 = xbc.reshape(b, t, XBC_DIM)
    dt3 = dt.reshape(b, t, LANES)
    y_f, hf = _ssd(False, xbc3, dt3, h0f, lw["conv_w"], lw["conv_b"], lw["alog_f"], lw["dtb_f"], None)
    ssm, hb = _ssd(True, xbc3, dt3, h0b, lw["conv_w"], lw["conv_b"], lw["alog_b"], lw["dtb_b"],
                   (y_f, z.reshape(b, t, D_SSM), lw["d_x"], lw["ssm_norm_g"]))
    x1, h2x = _outproj(att.reshape(n, D_ATT), ssm.reshape(n, D_SSM), xf, mods3, mod_row0, mod_tokens,
                       lw["wo_att"], lw["wo_ssm"], lw["norm_ffn_g"], lw["w_router"], lw["b_router"])
    meta = _route(h2x)
    perm = meta[:, 0, :].reshape(-1)
    cgrp = meta[:, 1, :MOE_SLOTS].reshape(-1)
    y = _moe(h2x, perm, cgrp, lw["wg"], lw["wu"], lw["wd"])
    out = _final(y, x1, mods3, mod_row0, mod_tokens, lw["final_g"])
    return out.reshape(b, t, D_MODEL), k3, v3, hf, hb


def _pad_lanes(v, width=LANES):
    return jnp.pad(v, [(0, 0)] * (v.ndim - 1) + [(0, width - v.shape[-1])])


def kernel(x_prompt, x_sample, cache_k, cache_v, state_ssm_fwd, state_ssm_bwd, c, c_ctx, w_ada, b_ada, norm_mix_g, w_in, w_out, lambda_q1, lambda_k1, lambda_q2, lambda_k2, attn_subln_g, conv_w, conv_b, a_log_fwd, a_log_bwd, dt_bias_fwd, dt_bias_bwd, ssm_d, ssm_norm_g, norm_ffn_g, w_group_router, b_group_router, w_expert_router, b_expert_router, w_exp_gate, w_exp_up, w_exp_down, final_norm_g):
    depth = w_in.shape[0]
    assert depth == 1, "single trunk layer"
    bp, tp, _ = x_prompt.shape
    bs, ts, _ = x_sample.shape
    l = 0
    cond = jnp.concatenate([c_ctx[None], c], axis=0)
    condT = _pad_lanes(cond.T, 8)
    mods = _ada(condT, w_ada[l], b_ada[l][None])
    mods3 = mods.reshape(8, 1, 6 * D_MODEL)

    w_router = _pad_lanes(jnp.concatenate([w_group_router[l], w_expert_router[l]], axis=1))
    wr_hi = w_router.astype(BF16)
    wr_lo = (w_router - wr_hi.astype(F32)).astype(BF16)
    lw = dict(
        norm_mix_g=norm_mix_g[l][None],
        w_main=w_in[l][:, :MAIN_COLS].astype(BF16),
        w_dt=_pad_lanes(w_in[l][:, MAIN_COLS:]).astype(BF16),
        lamp=jnp.stack([lambda_q1[l], lambda_k1[l], lambda_q2[l], lambda_k2[l]]),
        attn_subln_g=attn_subln_g[l][None],
        conv_w=conv_w[l], conv_b=conv_b[l][None],
        alog_f=_pad_lanes(a_log_fwd[l][None]), alog_b=_pad_lanes(a_log_bwd[l][None]),
        dtb_f=_pad_lanes(dt_bias_fwd[l][None]), dtb_b=_pad_lanes(dt_bias_bwd[l][None]),
        d_x=jnp.repeat(ssm_d[l], SSM_HEADDIM)[None], ssm_norm_g=ssm_norm_g[l][None],
        wo_att=w_out[l][:D_ATT].astype(BF16), wo_ssm=w_out[l][D_ATT:].astype(BF16),
        norm_ffn_g=norm_ffn_g[l][None],
        w_router=jnp.stack([wr_hi, wr_lo]),
        b_router=_pad_lanes(jnp.concatenate([b_group_router[l], b_expert_router[l]])[None]),
        wg=w_exp_gate[l].astype(BF16), wu=w_exp_up[l].astype(BF16), wd=w_exp_down[l].astype(BF16),
        final_g=final_norm_g[None],
    )
    n_state = SSM_HEADS * SSM_HEADDIM
    zeros_state = jnp.zeros((bp, n_state, D_STATE), F32)
    yp, ck, cv, hf, hb = _layer(x_prompt, mods3, 0, bp * tp, None, None, None, zeros_state, zeros_state, lw, l)
    ys, _, _, _, _ = _layer(x_sample, mods3, 1, ts, _rope_tables(ts),
                            cache_k[:, l].reshape(bs, -1, D_QK), cache_v[:, l].reshape(bs, -1, D_ATT),
                            state_ssm_fwd[:, l].reshape(bs, n_state, D_STATE),
                            state_ssm_bwd[:, l].reshape(bs, n_state, D_STATE), lw, l)
    new_k = ck.reshape(bp, 1, tp, ATT_HEADS, 2, QK_DIM)
    new_v = cv.reshape(bp, 1, tp, ATT_HEADS, V_DIM)
    new_hf = hf.reshape(bp, 1, SSM_HEADS, SSM_HEADDIM, D_STATE)
    new_hb = hb.reshape(bp, 1, SSM_HEADS, SSM_HEADDIM, D_STATE)
    return yp, ys, new_k, new_v, new_hf, new_hb
```

```python
import functools
import math

import jax
import jax.numpy as jnp
from jax import lax
from jax.experimental import pallas as pl
from jax.experimental.pallas import tpu as pltpu

D_MODEL = 1024
GRID_W = 64
ATT_HEADS = 4
QK_DIM = 64
V_DIM = 128
D_QK = 512
D_ATT = 512
ROPE_BASE = 10000.0
D_SSM = 512
SSM_HEADDIM = 64
SSM_HEADS = 8
SSM_GROUPS = 2
D_STATE = 128
CONV_W = 5
CHUNK = 128
XBC_DIM = 1024
N_GROUPS = 4
EXPERTS_PER_GROUP = 4
N_EXPERTS = 16
EXPERT_FF = 256
EPS = 1e-6
MAIN_COLS = 2 * D_QK + D_ATT + D_SSM + XBC_DIM
H2X_W = D_MODEL + 128
MOE_TILE = 2048
MOE_CHUNK = 256
MOE_SLOTS = MOE_TILE // MOE_CHUNK + N_GROUPS
MOE_ROWS = MOE_SLOTS * MOE_CHUNK
MOE_STEPS = MOE_SLOTS + 1
META_ROWS = MOE_STEPS * MOE_CHUNK
LANES = 128
HALO = 8
VMEM_LIMIT = 56 * 1024 * 1024

LOG2E = math.log2(math.e)
SUM_ROWS = 16
F32 = jnp.float32
BF16 = jnp.bfloat16


def _cparams(sem):
    return pltpu.CompilerParams(dimension_semantics=sem, vmem_limit_bytes=VMEM_LIMIT)


def _sigmoid(x):
    return 1.0 / (1.0 + jnp.exp(-x))


def _silu(x):
    return x * _sigmoid(x)


def _ada_kernel(condT_ref, w_ref, b_ref, o_ref):
    s = _silu(condT_ref[...])
    w = w_ref[...]
    b = b_ref[...]
    o_ref[...] = jnp.zeros_like(o_ref)
    for r in range(3):
        o_ref[r:r + 1, :] = jnp.sum(w * s[:, r:r + 1], axis=0, keepdims=True) + b


def _ada(condT, w_ada, b_ada):
    bn = 512
    n = w_ada.shape[1]
    return pl.pallas_call(
        _ada_kernel,
        grid=(n // bn,),
        in_specs=[pl.BlockSpec((D_MODEL, 8), lambda j: (0, 0)),
                  pl.BlockSpec((D_MODEL, bn), lambda j: (0, j)),
                  pl.BlockSpec((1, bn), lambda j: (0, j))],
        out_specs=pl.BlockSpec((8, bn), lambda j: (0, j)),
        out_shape=jax.ShapeDtypeStruct((8, n), F32),
        compiler_params=_cparams(("arbitrary",)),
        name="ada",
    )(condT, w_ada, b_ada)


def _inproj_kernel(rope, x_ref, mods_ref, g_ref, w_ref, wdt_ref, *rest):
    if rope:
        cos_ref, sa_ref, sb_ref, q_ref, kb_ref, vt_ref, z_ref, xbc_ref, dt_ref = rest
    else:
        q_ref, kb_ref, vt_ref, z_ref, xbc_ref, dt_ref, kf_ref, vf_ref = rest
    x = x_ref[...]
    shift = mods_ref[:, 0:D_MODEL]
    scale = mods_ref[:, D_MODEL:2 * D_MODEL]
    y = x * lax.rsqrt(jnp.mean(x * x, axis=-1, keepdims=True) + EPS) * g_ref[...]
    h = (y * (1.0 + scale) + shift).astype(BF16)
    r = jnp.dot(h, w_ref[...], preferred_element_type=F32)
    dt_ref[...] = jnp.dot(h, wdt_ref[...], preferred_element_type=F32)
    q = r[:, 0:D_QK]
    k = r[:, D_QK:2 * D_QK]
    if rope:
        cos = cos_ref[...]
        sa = sa_ref[...]
        sb = sb_ref[...]

        def rot(t):
            parts = []
            for hh in range(ATT_HEADS):
                th = t[:, hh * LANES:(hh + 1) * LANES]
                parts.append(th * cos + pltpu.roll(th, LANES - 16, 1) * sa + pltpu.roll(th, 16, 1) * sb)
            return jnp.concatenate(parts, axis=1)

        q = rot(q)
        k = rot(k)
    v = r[:, 2 * D_QK:2 * D_QK + D_ATT]
    q_ref[...] = q
    kb_ref[...] = k.astype(BF16)
    vt_ref[...] = v.T.astype(BF16)
    if not rope:
        kf_ref[...] = k
        vf_ref[...] = v
    z_ref[...] = r[:, 2 * D_QK + D_ATT:2 * D_QK + D_ATT + D_SSM]
    xbc_ref[...] = r[:, 2 * D_QK + D_ATT + D_SSM:MAIN_COLS]


def _inproj(x, mods3, mod_row0, mod_tokens, seq_len, g, w_main, w_dt, rope_tabs):
    n = x.shape[0]
    tm = 256
    per_seq = seq_len // tm
    per_mod = mod_tokens // tm
    rope = rope_tabs is not None
    in_specs = [pl.BlockSpec((tm, D_MODEL), lambda i: (i, 0)),
                pl.BlockSpec((None, 1, 6 * D_MODEL), lambda i: (mod_row0 + i // per_mod, 0, 0)),
                pl.BlockSpec((1, D_MODEL), lambda i: (0, 0)),
                pl.BlockSpec((D_MODEL, MAIN_COLS), lambda i: (0, 0)),
                pl.BlockSpec((D_MODEL, LANES), lambda i: (0, 0))]
    args = [x, mods3, g, w_main, w_dt]
    if rope:
        tab_spec = pl.BlockSpec((tm, LANES), lambda i: (i % per_seq, 0))
        in_specs += [tab_spec] * 3
        args += list(rope_tabs)
    def rows(wd, dtype=F32):
        return pl.BlockSpec((tm, wd), lambda i: (i, 0)), jax.ShapeDtypeStruct((n, wd), dtype)

    outs = [rows(D_QK), rows(D_QK, BF16),
            (pl.BlockSpec((None, D_ATT, tm), lambda i: (i // per_seq, 0, i % per_seq)),
             jax.ShapeDtypeStruct((n // seq_len, D_ATT, seq_len), BF16)),
            rows(D_SSM), rows(XBC_DIM), rows(LANES)]
    if not rope:
        outs += [rows(D_QK), rows(D_ATT)]
    return pl.pallas_call(
        functools.partial(_inproj_kernel, rope),
        grid=(n // tm,),
        in_specs=in_specs,
        out_specs=[o[0] for o in outs],
        out_shape=[o[1] for o in outs],
        compiler_params=_cparams(("parallel",)),
        name="inproj_rope" if rope else "inproj",
    )(*args)


def _attn_kernel(tk, lam0, has_cache, q_ref, k_ref, vt_ref, *rest):
    if has_cache:
        ck_ref, cvt_ref, lamp_ref, g_ref, o_ref = rest
    else:
        lamp_ref, g_ref, o_ref = rest
    tq = q_ref.shape[0]
    chunks = [(k_ref, vt_ref, c * tk) for c in range(k_ref.shape[0] // tk)]
    if has_cache:
        chunks += [(ck_ref, cvt_ref, c * tk) for c in range(ck_ref.shape[0] // tk)]
    lp = lamp_ref[...]
    lam = (jnp.exp(jnp.sum(lp[0:1] * lp[1:2], axis=-1, keepdims=True))
           - jnp.exp(jnp.sum(lp[2:3] * lp[3:4], axis=-1, keepdims=True)) + lam0)
    q = q_ref[...] * (QK_DIM ** -0.5 * LOG2E)
    lane = lax.broadcasted_iota(jnp.int32, q.shape, 1)
    qq_t = jnp.concatenate([jnp.where(lane < QK_DIM, q, 0.0), jnp.where(lane >= QK_DIM, q, 0.0)],
                           axis=0).T.astype(BF16)

    def scores(chunk):
        kr, _, start = chunk
        return jnp.dot(kr[start:start + tk, :], qq_t, preferred_element_type=F32)

    ones_rows = jnp.ones((SUM_ROWS, tk), BF16)

    def update(s, chunk, m, acc):
        _, vr, start = chunk
        m_new = jnp.maximum(m, jnp.max(s, axis=0, keepdims=True))
        alpha = jnp.exp2(m - m_new)
        p = jnp.exp2(s - m_new).astype(BF16)
        v_ext = jnp.concatenate([vr[:, start:start + tk], ones_rows], axis=0)
        acc = alpha * acc + jnp.dot(v_ext, p, preferred_element_type=F32)
        return m_new, acc

    m = jnp.full((1, 2 * tq), -jnp.inf, F32)
    acc = jnp.zeros((V_DIM + SUM_ROWS, 2 * tq), F32)
    s = scores(chunks[0])
    for c, chunk in enumerate(chunks):
        s_next = scores(chunks[c + 1]) if c + 1 < len(chunks) else None
        m, acc = update(s, chunk, m, acc)
        s = s_next
    o = acc[0:V_DIM] / acc[V_DIM:V_DIM + 1]
    o = (o[:, 0:tq] - lam * o[:, tq:2 * tq]).T
    o = o * lax.rsqrt(jnp.mean(o * o, axis=-1, keepdims=True) + EPS)
    o_ref[...] = o * g_ref[...] * (1.0 - lam0)


def _attention(q, k, vt, cache, lamp, g, lam0, tq, tk):
    b, t, _ = q.shape

    def kv_specs(length):
        return [pl.BlockSpec((None, length, LANES), lambda bi, h, i: (bi, 0, h)),
                pl.BlockSpec((None, LANES, length), lambda bi, h, i: (bi, h, 0))]

    in_specs = [pl.BlockSpec((None, tq, LANES), lambda bi, h, i: (bi, i, h))] + kv_specs(t)
    args = [q, k, vt]
    if cache is not None:
        in_specs += kv_specs(cache[0].shape[1])
        args += list(cache)
    in_specs += [pl.BlockSpec((4, QK_DIM), lambda bi, h, i: (0, 0)),
                 pl.BlockSpec((1, V_DIM), lambda bi, h, i: (0, 0))]
    return pl.pallas_call(
        functools.partial(_attn_kernel, tk, lam0, cache is not None),
        grid=(b, ATT_HEADS, t // tq),
        in_specs=in_specs,
        out_specs=pl.BlockSpec((None, tq, LANES), lambda bi, h, i: (bi, i, h)),
        out_shape=jax.ShapeDtypeStruct((b, t, D_ATT), F32),
        compiler_params=_cparams(("parallel", "parallel", "arbitrary")),
        name="diff_attn",
    )(*args, lamp, g)


def _expand_heads(v):
    lane = lax.broadcasted_iota(jnp.int32, (v.shape[0], LANES), 1)
    parts = []
    for pr in range(SSM_HEADS // 2):
        a = jnp.broadcast_to(v[:, 2 * pr:2 * pr + 1], (v.shape[0], LANES))
        b = jnp.broadcast_to(v[:, 2 * pr + 1:2 * pr + 2], (v.shape[0], LANES))
        parts.append(jnp.where(lane < SSM_HEADDIM, a, b))
    return jnp.concatenate(parts, axis=1)


def _ssd_kernel(reverse, nc, xc_ref, xp_ref, xn_ref, dt_ref, h0_ref, cw_ref, cb_ref, alog_ref, dtb_ref,
                *rest):
    if reverse:
        yf_ref, z_ref, d_ref, ng_ref, y_ref, hout_ref, state_ref = rest
    else:
        y_ref, hout_ref, state_ref = rest
    c = pl.program_id(1)
    chunk = (nc - 1 - c) if reverse else c

    @pl.when(c == 0)
    def _():
        state_ref[...] = h0_ref[...]

    prev = jnp.where(chunk > 0, xp_ref[...], 0.0)
    nxt = jnp.where(chunk < nc - 1, xn_ref[...], 0.0)
    xpad = jnp.concatenate([prev, xc_ref[...], nxt], axis=0)
    cw = cw_ref[...]
    conv = cb_ref[...]
    for kk in range(CONV_W):
        off = HALO - CONV_W // 2 + kk
        conv = conv + xpad[off:off + CHUNK, :] * cw[kk:kk + 1, :]
    xbc = _silu(conv)
    xs = xbc[:, 0:D_SSM]
    bm = xbc[:, D_SSM:D_SSM + SSM_GROUPS * D_STATE].astype(BF16)
    cm = xbc[:, D_SSM + SSM_GROUPS * D_STATE:XBC_DIM].astype(BF16)

    dt_raw = dt_ref[...]
    if reverse:
        dt_raw = pltpu.roll(dt_raw, LANES - SSM_HEADS, 1)
    xv = dt_raw + dtb_ref[...]
    dt = jnp.maximum(xv, 0.0) + jnp.log(1.0 + jnp.exp(-jnp.abs(xv)))
    la = dt * (-jnp.exp(alog_ref[...]))
    row = lax.broadcasted_iota(jnp.int32, (CHUNK, LANES), 0)
    col = lax.broadcasted_iota(jnp.int32, (CHUNK, LANES), 1)
    cum = la
    sh = 1
    while sh < CHUNK:
        if reverse:
            cum = cum + jnp.where(row < CHUNK - sh, pltpu.roll(cum, CHUNK - sh, 0), 0.0)
        else:
            cum = cum + jnp.where(row >= sh, pltpu.roll(cum, sh, 0), 0.0)
        sh *= 2
    end = CHUNK - 1 if not reverse else 0
    cum_end = cum[end:end + 1, :]
    cum_t = cum.T
    causal = (row <= col) if reverse else (row >= col)

    dt_x = _expand_heads(dt)
    xd = xs * dt_x
    xdw = (xd * _expand_heads(jnp.exp(cum_end - cum))).astype(BF16)
    xd = xd.astype(BF16)
    ecum_x = _expand_heads(jnp.exp(cum))
    lane = col
    rep = SSM_HEADS // SSM_GROUPS
    state = state_ref[...]
    y_parts = []
    new_state = []
    for g in range(SSM_GROUPS):
        bg = bm[:, g * D_STATE:(g + 1) * D_STATE]
        cg = cm[:, g * D_STATE:(g + 1) * D_STATE]
        cbt = lax.dot_general(cg, bg, (((1,), (1,)), ((), ())), preferred_element_type=F32)
        rows = slice(g * rep * SSM_HEADDIM, (g + 1) * rep * SSM_HEADDIM)
        st_g = state[rows, :]
        y_off = lax.dot_general(cg, st_g.astype(BF16), (((1,), (1,)), ((), ())),
                                preferred_element_type=F32)
        cst = lax.dot_general(xdw[:, rows], bg, (((0,), (0,)), ((), ())), preferred_element_type=F32)
        dec = jnp.exp(cum_end)
        for pr in range(rep // 2):
            y_pair = None
            for sub in range(2):
                h = g * rep + 2 * pr + sub
                seg = cum[:, h:h + 1] - cum_t[h:h + 1, :]
                decay = jnp.exp(jnp.where(causal, seg, -jnp.inf))
                sc = (cbt * decay).astype(BF16)
                xd_pair = xd[:, (h // 2) * LANES:(h // 2 + 1) * LANES]
                keep = (lane < SSM_HEADDIM) if sub == 0 else (lane >= SSM_HEADDIM)
                t = jnp.dot(sc, jnp.where(keep, xd_pair, jnp.zeros_like(xd_pair)), preferred_element_type=F32)
                y_pair = t if y_pair is None else y_pair + t
            y_parts.append(y_pair)
        dec_rows = jnp.concatenate(
            [jnp.broadcast_to(dec[:, g * rep + hh:g * rep + hh + 1], (SSM_HEADDIM, D_STATE)) for hh in range(rep)],
            axis=0)
        new_state.append(st_g * dec_rows + cst)
        y_parts[-2] = y_parts[-2] + y_off[:, 0:LANES] * ecum_x[:, rows][:, 0:LANES]
        y_parts[-1] = y_parts[-1] + y_off[:, LANES:2 * LANES] * ecum_x[:, rows][:, LANES:2 * LANES]
    y = jnp.concatenate(y_parts, axis=1)
    state = jnp.concatenate(new_state, axis=0)
    state_ref[...] = state

    @pl.when(c == nc - 1)
    def _():
        hout_ref[...] = state

    if reverse:
        y = y + yf_ref[...] + xs * d_ref[...]
        y = y * _silu(z_ref[...])
        y = y * lax.rsqrt(jnp.mean(y * y, axis=-1, keepdims=True) + EPS)
        y_ref[...] = y * ng_ref[...]
    else:
        y_ref[...] = y


def _ssd(reverse, xbc, dt, h0, conv_w, conv_b, alog, dtb, extra):
    b, l, _ = xbc.shape
    nc = l // CHUNK
    hb = CHUNK // HALO

    def ch(ci):
        return (nc - 1 - ci) if reverse else ci

    in_specs = [pl.BlockSpec((None, CHUNK, XBC_DIM), lambda bi, ci: (bi, ch(ci), 0)),
                pl.BlockSpec((None, HALO, XBC_DIM), lambda bi, ci: (bi, jnp.maximum(ch(ci) * hb - 1, 0), 0)),
                pl.BlockSpec((None, HALO, XBC_DIM),
                             lambda bi, ci: (bi, jnp.minimum((ch(ci) + 1) * hb, l // HALO - 1), 0)),
                pl.BlockSpec((None, CHUNK, LANES), lambda bi, ci: (bi, ch(ci), 0)),
                pl.BlockSpec((None, SSM_HEADS * SSM_HEADDIM, D_STATE), lambda bi, ci: (bi, 0, 0)),
                pl.BlockSpec((CONV_W, XBC_DIM), lambda bi, ci: (0, 0)),
                pl.BlockSpec((1, XBC_DIM), lambda bi, ci: (0, 0)),
                pl.BlockSpec((1, LANES), lambda bi, ci: (0, 0)),
                pl.BlockSpec((1, LANES), lambda bi, ci: (0, 0))]
    args = [xbc, xbc, xbc, dt, h0, conv_w, conv_b, alog, dtb]
    if reverse:
        y_fwd, z, d_x, norm_g = extra
        in_specs += [pl.BlockSpec((None, CHUNK, D_SSM), lambda bi, ci: (bi, ch(ci), 0)),
                     pl.BlockSpec((None, CHUNK, D_SSM), lambda bi, ci: (bi, ch(ci), 0)),
                     pl.BlockSpec((1, D_SSM), lambda bi, ci: (0, 0)),
                     pl.BlockSpec((1, D_SSM), lambda bi, ci: (0, 0))]
        args += [y_fwd, z, d_x, norm_g]
    return pl.pallas_call(
        functools.partial(_ssd_kernel, reverse, nc),
        grid=(b, nc),
        in_specs=in_specs,
        out_specs=[pl.BlockSpec((None, CHUNK, D_SSM), lambda bi, ci: (bi, ch(ci), 0)),
                   pl.BlockSpec((None, SSM_HEADS * SSM_HEADDIM, D_STATE), lambda bi, ci: (bi, 0, 0))],
        out_shape=[jax.ShapeDtypeStruct((b, l, D_SSM), F32),
                   jax.ShapeDtypeStruct((b, SSM_HEADS * SSM_HEADDIM, D_STATE), F32)],
        scratch_shapes=[pltpu.VMEM((SSM_HEADS * SSM_HEADDIM, D_STATE), F32)],
        compiler_params=_cparams(("parallel", "arbitrary")),
        name="ssd_bwd" if reverse else "ssd_fwd",
    )(*args)


def _outproj_kernel(att_ref, ssm_ref, x_ref, mods_ref, wa_ref, ws_ref, g_ref, wr_ref, br_ref,
                    x1_ref, h2x_ref):
    mix = (jnp.dot(att_ref[...].astype(BF16), wa_ref[...], preferred_element_type=F32)
           + jnp.dot(ssm_ref[...].astype(BF16), ws_ref[...], preferred_element_type=F32))
    gate1 = mods_ref[:, 2 * D_MODEL:3 * D_MODEL]
    shift2 = mods_ref[:, 3 * D_MODEL:4 * D_MODEL]
    scale2 = mods_ref[:, 4 * D_MODEL:5 * D_MODEL]
    x1 = x_ref[...] + gate1 * mix
    x1_ref[...] = x1
    y = x1 * lax.rsqrt(jnp.mean(x1 * x1, axis=-1, keepdims=True) + EPS) * g_ref[...]
    h2 = y * (1.0 + scale2) + shift2
    h_hi = h2.astype(BF16)
    h2x_ref[:, 0:D_MODEL] = h2
    h_lo = (h2 - h_hi.astype(F32)).astype(BF16)
    w_hi = wr_ref[0]
    w_lo = wr_ref[1]
    logits = (jnp.dot(h_hi, w_hi, preferred_element_type=F32) + jnp.dot(h_lo, w_hi, preferred_element_type=F32)
              + jnp.dot(h_hi, w_lo, preferred_element_type=F32)) + br_ref[...]
    lane = lax.broadcasted_iota(jnp.int32, logits.shape, 1)
    neg = -jnp.inf
    big = jnp.int32(1 << 20)
    is_g = lane < N_GROUPS
    gl = jnp.where(is_g, logits, neg)
    gmax = jnp.max(gl, axis=-1, keepdims=True)
    g_idx = jnp.min(jnp.where(gl == gmax, lane, big), axis=-1, keepdims=True)
    p_g = 1.0 / jnp.sum(jnp.where(is_g, jnp.exp(gl - gmax), 0.0), axis=-1, keepdims=True)
    e_lo = N_GROUPS + g_idx * EXPERTS_PER_GROUP
    in_grp = (lane >= e_lo) & (lane < e_lo + EXPERTS_PER_GROUP)
    el = jnp.where(in_grp, logits, neg)
    m1 = jnp.max(el, axis=-1, keepdims=True)
    i1 = jnp.min(jnp.where(el == m1, lane, big), axis=-1, keepdims=True)
    el2 = jnp.where(lane == i1, neg, el)
    m2 = jnp.max(el2, axis=-1, keepdims=True)
    i2 = jnp.min(jnp.where(el2 == m2, lane, big), axis=-1, keepdims=True)
    e2 = jnp.exp(m2 - m1)
    w1 = p_g / (1.0 + e2)
    w2 = p_g * e2 / (1.0 + e2)
    cw = jnp.where(lane == i1, w1, 0.0) + jnp.where(lane == i2, w2, 0.0)
    slab = jnp.where(lane == EXPERTS_PER_GROUP, g_idx.astype(F32), 0.0)
    for j in range(EXPERTS_PER_GROUP):
        wj = jnp.sum(jnp.where(lane == e_lo + j, cw, 0.0), axis=-1, keepdims=True)
        slab = slab + jnp.where(lane == j, wj, 0.0)
    h2x_ref[:, D_MODEL:H2X_W] = slab


def _outproj(att, ssm, x, mods3, mod_row0, mod_tokens, wo_att, wo_ssm, g, w_router, b_router):
    n = x.shape[0]
    tm = 256
    per_mod = mod_tokens // tm
    return pl.pallas_call(
        _outproj_kernel,
        grid=(n // tm,),
        in_specs=[pl.BlockSpec((tm, D_ATT), lambda i: (i, 0)),
                  pl.BlockSpec((tm, D_SSM), lambda i: (i, 0)),
                  pl.BlockSpec((tm, D_MODEL), lambda i: (i, 0)),
                  pl.BlockSpec((None, 1, 6 * D_MODEL), lambda i: (mod_row0 + i // per_mod, 0, 0)),
                  pl.BlockSpec((D_ATT, D_MODEL), lambda i: (0, 0)),
                  pl.BlockSpec((D_SSM, D_MODEL), lambda i: (0, 0)),
                  pl.BlockSpec((1, D_MODEL), lambda i: (0, 0)),
                  pl.BlockSpec((2, D_MODEL, LANES), lambda i: (0, 0, 0)),
                  pl.BlockSpec((1, LANES), lambda i: (0, 0))],
        out_specs=[pl.BlockSpec((tm, D_MODEL), lambda i: (i, 0)),
                   pl.BlockSpec((tm, H2X_W), lambda i: (i, 0))],
        out_shape=[jax.ShapeDtypeStruct((n, D_MODEL), F32),
                   jax.ShapeDtypeStruct((n, H2X_W), F32)],
        compiler_params=_cparams(("parallel",)),
        name="outproj_router",
    )(att, ssm, x, mods3, wo_att, wo_ssm, g, w_router, b_router)


def _route_kernel(slab_ref, meta_ref):
    t_n = MOE_TILE
    blk = LANES
    slab = slab_ref[...]
    lane = lax.broadcasted_iota(jnp.int32, (t_n, LANES), 1)
    gcol = jnp.sum(jnp.where(lane == EXPERTS_PER_GROUP, slab, 0.0), axis=-1, keepdims=True)
    member = (lane.astype(F32) == gcol) & (lane < N_GROUPS)
    a = jnp.where(member, 1.0, 0.0).astype(BF16)
    r_i = lax.broadcasted_iota(jnp.int32, (blk, blk), 0)
    c_i = lax.broadcasted_iota(jnp.int32, (blk, blk), 1)
    lower = jnp.where(c_i < r_i, 1.0, 0.0).astype(BF16)
    upper = jnp.where(r_i < c_i, 1.0, 0.0).astype(BF16)
    offs = jnp.zeros((1, LANES), F32)
    ranks = []
    for b in range(t_n // blk):
        ab = a[b * blk:(b + 1) * blk]
        rb = jnp.dot(lower, ab, preferred_element_type=F32)
        ranks.append(rb + offs)
        offs = offs + rb[blk - 1:blk] + ab[blk - 1:blk].astype(F32)
    rank = jnp.concatenate(ranks, axis=0)
    n_chunk = jnp.floor((offs + (MOE_CHUNK - 1)) * (1.0 / MOE_CHUNK))
    start = jnp.dot(jnp.broadcast_to(n_chunk, (8, LANES)).astype(BF16), upper,
                    preferred_element_type=F32)[0:1]
    end = start + n_chunk
    dest = jnp.sum(jnp.where(member, start * MOE_CHUNK + rank, 0.0), axis=-1, keepdims=True)
    tok = lax.broadcasted_iota(jnp.int32, (t_n, LANES), 0)
    digits = jnp.where(lane == 0, (tok // blk).astype(F32),
                       jnp.where(lane == 1, (tok % blk).astype(F32), jnp.where(lane == 2, 1.0, 0.0))).astype(BF16)
    sw = 512
    pieces = []
    for sc in range(MOE_ROWS // sw):
        s_id = (lax.broadcasted_iota(jnp.int32, (t_n, sw), 1) + sc * sw).astype(F32)
        hit = jnp.where(dest == s_id, 1.0, 0.0).astype(BF16)
        r = lax.dot_general(digits, hit, (((0,), (0,)), ((), ())), preferred_element_type=F32)
        tok_of = r[0:1] * blk + r[1:2]
        pieces.append(jnp.where(r[2:3] > 0.5, tok_of, float(t_n)))
    pieces.append(jnp.full((1, META_ROWS - MOE_ROWS), float(t_n), F32))
    perm = jnp.concatenate(pieces, axis=1)
    slot = lax.broadcasted_iota(jnp.int32, (1, META_ROWS), 1).astype(F32)
    lane1 = lax.broadcasted_iota(jnp.int32, (1, LANES), 1)
    cg = jnp.zeros((1, META_ROWS), F32)
    for g in range(N_GROUPS):
        end_g = jnp.sum(jnp.where(lane1 == g, end, 0.0), axis=-1, keepdims=True)
        cg = cg + jnp.where(slot >= end_g, 1.0, 0.0)
    n_act = jnp.broadcast_to(end_g, (1, META_ROWS))
    meta_ref[...] = jnp.concatenate([perm, cg, n_act, jnp.zeros((5, META_ROWS), F32)], axis=0).astype(jnp.int32)


def _route(h2x):
    n = h2x.shape[0]
    n_tiles = n // MOE_TILE
    return pl.pallas_call(
        _route_kernel,
        grid=(n_tiles,),
        in_specs=[pl.BlockSpec((MOE_TILE, LANES), lambda i: (i, D_MODEL // LANES))],
        out_specs=pl.BlockSpec((None, 8, META_ROWS), lambda i: (i, 0, 0)),
        out_shape=jax.ShapeDtypeStruct((n_tiles, 8, META_ROWS), jnp.int32),
        compiler_params=_cparams(("parallel",)),
        name="moe_route",
    )(h2x)


def _moe_kernel(perm_ref, cg_ref, nact_ref, h_ref, wg_ref, wu_ref, wd_ref, y_ref, hs0, hs1, ys0, ys1):
    i = pl.program_id(0)
    s = pl.program_id(1)
    n_act = nact_ref[i]
    tile_base = i * META_ROWS
    hs = (hs0, hs1)
    ys = (ys0, ys1)

    def gather(chunk, dst):
        base = tile_base + chunk * MOE_CHUNK
        for r in range(MOE_CHUNK):
            src = jnp.minimum(perm_ref[base + r], MOE_TILE - 1)
            dst[r:r + 1, :] = h_ref[pl.ds(src, 1), :]

    def scatter(chunk, src):
        base = tile_base + chunk * MOE_CHUNK
        for r in range(MOE_CHUNK):
            y_ref[pl.ds(perm_ref[base + r], 1), :] = src[r:r + 1, :]

    def ffn(src, dst):
        hb = src[:, 0:D_MODEL].astype(BF16)
        cw = src[:, D_MODEL:H2X_W]
        acc = None
        for e in range(EXPERTS_PER_GROUP):
            a = jnp.dot(hb, wg_ref[e], preferred_element_type=F32)
            u = jnp.dot(hb, wu_ref[e], preferred_element_type=F32)
            hid = (_silu(a) * u * cw[:, e:e + 1]).astype(BF16)
            t = jnp.dot(hid, wd_ref[e], preferred_element_type=F32)
            acc = t if acc is None else acc + t
        dst[...] = acc

    @pl.when(s == 0)
    def _():
        y_ref[MOE_TILE:MOE_TILE + 8, :] = jnp.zeros((8, D_MODEL), F32)
        ys1[...] = jnp.zeros_like(ys1)
        gather(0, hs0)

    for par in (0, 1):
        @pl.when((s < n_act) & (s % 2 == par))
        def _():
            gather(s + 1, hs[1 - par])
            ffn(hs[par], ys[par])
            scatter(jnp.maximum(s - 1, 0), ys[1 - par])

        @pl.when((s == n_act) & (s % 2 == par))
        def _():
            scatter(s - 1, ys[1 - par])


def _moe(h2x, perm, cgrp, nact, wg, wu, wd):
    n = h2x.shape[0]
    n_tiles = n // MOE_TILE

    def w_idx(i, s, perm_ref, cg_ref, nact_ref):
        return (jnp.minimum(cg_ref[i * MOE_STEPS + s], N_GROUPS - 1), 0, 0)

    return pl.pallas_call(
        _moe_kernel,
        grid_spec=pltpu.PrefetchScalarGridSpec(
            num_scalar_prefetch=3,
            grid=(n_tiles, MOE_STEPS),
            in_specs=[pl.BlockSpec((MOE_TILE, H2X_W), lambda i, s, p, c, a: (i, 0)),
                      pl.BlockSpec((EXPERTS_PER_GROUP, D_MODEL, EXPERT_FF), w_idx),
                      pl.BlockSpec((EXPERTS_PER_GROUP, D_MODEL, EXPERT_FF), w_idx),
                      pl.BlockSpec((EXPERTS_PER_GROUP, EXPERT_FF, D_MODEL), w_idx)],
            out_specs=pl.BlockSpec((None, MOE_TILE + 8, D_MODEL), lambda i, s, p, c, a: (i, 0, 0)),
            scratch_shapes=[pltpu.VMEM((MOE_CHUNK, H2X_W), F32), pltpu.VMEM((MOE_CHUNK, H2X_W), F32),
                            pltpu.VMEM((MOE_CHUNK, D_MODEL), F32), pltpu.VMEM((MOE_CHUNK, D_MODEL), F32)]),
        out_shape=jax.ShapeDtypeStruct((n_tiles, MOE_TILE + 8, D_MODEL), F32),
        compiler_params=_cparams(("parallel", "arbitrary")),
        name="moe_experts",
    )(perm, cgrp, nact, h2x, wg, wu, wd)


def _final_kernel(y_ref, x1_ref, mods_ref, fg_ref, o_ref):
    gate2 = mods_ref[:, 5 * D_MODEL:6 * D_MODEL]
    x2 = x1_ref[...] + gate2 * y_ref[...]
    o_ref[...] = x2 * lax.rsqrt(jnp.mean(x2 * x2, axis=-1, keepdims=True) + EPS) * fg_ref[...]


def _final(y, x1, mods3, mod_row0, mod_tokens, fg):
    n = x1.shape[0]
    tm = 512
    per_mod = mod_tokens // tm
    per_tile = MOE_TILE // tm
    return pl.pallas_call(
        _final_kernel,
        grid=(n // tm,),
        in_specs=[pl.BlockSpec((None, tm, D_MODEL), lambda j: (j // per_tile, j % per_tile, 0)),
                  pl.BlockSpec((tm, D_MODEL), lambda j: (j, 0)),
                  pl.BlockSpec((None, 1, 6 * D_MODEL), lambda j: (mod_row0 + j // per_mod, 0, 0)),
                  pl.BlockSpec((1, D_MODEL), lambda j: (0, 0))],
        out_specs=pl.BlockSpec((tm, D_MODEL), lambda j: (j, 0)),
        out_shape=jax.ShapeDtypeStruct((n, D_MODEL), F32),
        compiler_params=_cparams(("parallel",)),
        name="final_norm",
    )(y, x1, mods3, fg)


def _rope_tables(t):
    n_freq = QK_DIM // 4
    n_rows = t // GRID_W
    freqs = ROPE_BASE ** (-jnp.arange(n_freq, dtype=F32) / n_freq)
    ang_r = jnp.arange(n_rows, dtype=F32)[:, None] * freqs
    ang_c = jnp.arange(GRID_W, dtype=F32)[:, None] * freqs
    cr, sr, cc, sc = lax.optimization_barrier((jnp.cos(ang_r), jnp.sin(ang_r), jnp.cos(ang_c), jnp.sin(ang_c)))
    by_row = lambda a: jnp.repeat(a, GRID_W, axis=0)
    by_col = lambda a: jnp.tile(a, (n_rows, 1))
    zero = jnp.zeros((t, n_freq), F32)
    one_map = lambda r1, r2, c1, c2: jnp.concatenate([r1, r2, c1, c2], axis=1)
    both = lambda m: jnp.concatenate([m, m], axis=1)
    cos = both(one_map(by_row(cr), by_row(cr), by_col(cc), by_col(cc)))
    sin_a = both(one_map(-by_row(sr), zero, -by_col(sc), zero))
    sin_b = both(one_map(zero, by_row(sr), zero, by_col(sc)))
    return cos, sin_a, sin_b


def _layer(x, mods3, mod_row0, mod_tokens, rope_tabs, ctx_k, ctx_v, h0f, h0b, lw, layer):
    b, t, _ = x.shape
    n = b * t
    xf = x.reshape(n, D_MODEL)
    res = _inproj(xf, mods3, mod_row0, mod_tokens, t, lw["norm_mix_g"], lw["w_main"], lw["w_dt"], rope_tabs)
    q, kb, vt, z, xbc, dt = res[:6]
    if ctx_k is None:
        cache = None
        k3, v3 = res[6].reshape(b, t, D_QK), res[7].reshape(b, t, D_ATT)
    else:
        cache = (ctx_k.astype(BF16), jnp.swapaxes(ctx_v, 1, 2).astype(BF16))
        k3 = v3 = None
    lam0 = 0.8 - 0.6 * math.exp(-0.3 * layer)
    tq = 512 if t % 512 == 0 else 256
    tk = 512 if t % 512 == 0 else 256
    att = _attention(q.reshape(b, t, D_QK), kb.reshape(b, t, D_QK), vt, cache, lw["lamp"], lw["attn_subln_g"],
                     lam0, tq, tk)
    xbc3 = xbc.reshape(b, t, XBC_DIM)
    dt3 = dt.reshape(b, t, LANES)
    y_f, hf = _ssd(False, xbc3, dt3, h0f, lw["conv_w"], lw["conv_b"], lw["alog_f"], lw["dtb_f"], None)
    ssm, hb = _ssd(True, xbc3, dt3, h0b, lw["conv_w"], lw["conv_b"], lw["alog_b"], lw["dtb_b"],
                   (y_f, z.reshape(b, t, D_SSM), lw["d_x"], lw["ssm_norm_g"]))
    x1, h2x = _outproj(att.reshape(n, D_ATT), ssm.reshape(n, D_SSM), xf, mods3, mod_row0, mod_tokens,
                       lw["wo_att"], lw["wo_ssm"], lw["norm_ffn_g"], lw["w_router"], lw["b_router"])
    meta = _route(h2x)
    perm = meta[:, 0, :].reshape(-1)
    cgrp = meta[:, 1, :MOE_STEPS].reshape(-1)
    nact = meta[:, 2, 0]
    y = _moe(h2x, perm, cgrp, nact, lw["wg"], lw["wu"], lw["wd"])
    out = _final(y, x1, mods3, mod_row0, mod_tokens, lw["final_g"])
    return out.reshape(b, t, D_MODEL), k3, v3, hf, hb


def _pad_lanes(v, width=LANES):
    return jnp.pad(v, [(0, 0)] * (v.ndim - 1) + [(0, width - v.shape[-1])])


def kernel(x_prompt, x_sample, cache_k, cache_v, state_ssm_fwd, state_ssm_bwd, c, c_ctx, w_ada, b_ada, norm_mix_g, w_in, w_out, lambda_q1, lambda_k1, lambda_q2, lambda_k2, attn_subln_g, conv_w, conv_b, a_log_fwd, a_log_bwd, dt_bias_fwd, dt_bias_bwd, ssm_d, ssm_norm_g, norm_ffn_g, w_group_router, b_group_router, w_expert_router, b_expert_router, w_exp_gate, w_exp_up, w_exp_down, final_norm_g):
    depth = w_in.shape[0]
    assert depth == 1, "single trunk layer"
    bp, tp, _ = x_prompt.shape
    bs, ts, _ = x_sample.shape
    l = 0
    cond = jnp.concatenate([c_ctx[None], c], axis=0)
    condT = _pad_lanes(cond.T, 8)
    mods = _ada(condT, w_ada[l], b_ada[l][None])
    mods3 = mods.reshape(8, 1, 6 * D_MODEL)

    w_router = _pad_lanes(jnp.concatenate([w_group_router[l], w_expert_router[l]], axis=1))
    wr_hi = w_router.astype(BF16)
    wr_lo = (w_router - wr_hi.astype(F32)).astype(BF16)
    lw = dict(
        norm_mix_g=norm_mix_g[l][None],
        w_main=w_in[l].astype(BF16),
        w_dt=_pad_lanes(w_in[l][:, MAIN_COLS:]).astype(BF16),
        lamp=jnp.stack([lambda_q1[l], lambda_k1[l], lambda_q2[l], lambda_k2[l]]),
        attn_subln_g=attn_subln_g[l][None],
        conv_w=conv_w[l], conv_b=conv_b[l][None],
        alog_f=_pad_lanes(a_log_fwd[l][None]), alog_b=_pad_lanes(a_log_bwd[l][None]),
        dtb_f=_pad_lanes(dt_bias_fwd[l][None]), dtb_b=_pad_lanes(dt_bias_bwd[l][None]),
        d_x=jnp.repeat(ssm_d[l], SSM_HEADDIM)[None], ssm_norm_g=ssm_norm_g[l][None],
        wo_att=w_out[l][:D_ATT].astype(BF16), wo_ssm=w_out[l][D_ATT:].astype(BF16),
        norm_ffn_g=norm_ffn_g[l][None],
        w_router=jnp.stack([wr_hi, wr_lo]),
        b_router=_pad_lanes(jnp.concatenate([b_group_router[l], b_expert_router[l]])[None]),
        wg=w_exp_gate[l].astype(BF16), wu=w_exp_up[l].astype(BF16), wd=w_exp_down[l].astype(BF16),
        final_g=final_norm_g[None],
    )
    n_state = SSM_HEADS * SSM_HEADDIM
    zeros_state = jnp.zeros((bp, n_state, D_STATE), F32)
    yp, ck, cv, hf, hb = _layer(x_prompt, mods3, 0, bp * tp, None, None, None, zeros_state, zeros_state, lw, l)
    ys, _, _, _, _ = _layer(x_sample, mods3, 1, ts, _rope_tables(ts),
                            cache_k[:, l].reshape(bs, -1, D_QK), cache_v[:, l].reshape(bs, -1, D_ATT),
                            state_ssm_fwd[:, l].reshape(bs, n_state, D_STATE),
                            state_ssm_bwd[:, l].reshape(bs, n_state, D_STATE), lw, l)
    new_k = ck.reshape(bp, 1, tp, ATT_HEADS, 2, QK_DIM)
    new_v = cv.reshape(bp, 1, tp, ATT_HEADS, V_DIM)
    new_hf = hf.reshape(bp, 1, SSM_HEADS, SSM_HEADDIM, D_STATE)
    new_hb = hb.reshape(bp, 1, SSM_HEADS, SSM_HEADDIM, D_STATE)
    return yp, ys, new_k, new_v, new_hf, new_hb
```

```python
import functools
import math

import numpy as np
import jax
import jax.numpy as jnp
from jax import lax
from jax.experimental import pallas as pl
from jax.experimental.pallas import tpu as pltpu

D_MODEL = 1024
GRID_W = 64
ATT_HEADS = 4
QK_DIM = 64
V_DIM = 128
D_QK = 512
D_ATT = 512
ROPE_BASE = 10000.0
D_SSM = 512
SSM_HEADDIM = 64
SSM_HEADS = 8
SSM_GROUPS = 2
D_STATE = 128
CONV_W = 5
CHUNK = 128
XBC_DIM = 1024
N_GROUPS = 4
EXPERTS_PER_GROUP = 4
N_EXPERTS = 16
EXPERT_FF = 256
EPS = 1e-6
MAIN_COLS = 2 * D_QK + D_ATT + D_SSM + XBC_DIM
H2X_W = D_MODEL + 128
MOE_TILE = 2048
MOE_CHUNK = 256
MOE_SLOTS = MOE_TILE // MOE_CHUNK + N_GROUPS
MOE_ROWS = MOE_SLOTS * MOE_CHUNK
MOE_STEPS = MOE_SLOTS + 1
META_ROWS = MOE_STEPS * MOE_CHUNK
LANES = 128
HALO = 8
VMEM_LIMIT = 56 * 1024 * 1024

LOG2E = math.log2(math.e)
SUM_ROWS = 16
F32 = jnp.float32
BF16 = jnp.bfloat16


def _cparams(sem):
    return pltpu.CompilerParams(dimension_semantics=sem, vmem_limit_bytes=VMEM_LIMIT)


def _sigmoid(x):
    return 1.0 / (1.0 + jnp.exp(-x))


def _silu(x):
    return x * _sigmoid(x)


def _ada_kernel(condT_ref, w_ref, b_ref, o_ref):
    s = _silu(condT_ref[...])
    w = w_ref[...]
    b = b_ref[...]
    o_ref[...] = jnp.zeros_like(o_ref)
    for r in range(3):
        o_ref[r:r + 1, :] = jnp.sum(w * s[:, r:r + 1], axis=0, keepdims=True) + b


def _ada(condT, w_ada, b_ada):
    bn = 512
    n = w_ada.shape[1]
    return pl.pallas_call(
        _ada_kernel,
        grid=(n // bn,),
        in_specs=[pl.BlockSpec((D_MODEL, 8), lambda j: (0, 0)),
                  pl.BlockSpec((D_MODEL, bn), lambda j: (0, j)),
                  pl.BlockSpec((1, bn), lambda j: (0, j))],
        out_specs=pl.BlockSpec((8, bn), lambda j: (0, j)),
        out_shape=jax.ShapeDtypeStruct((8, n), F32),
        compiler_params=_cparams(("arbitrary",)),
        name="ada",
    )(condT, w_ada, b_ada)


def _inproj_kernel(rope, x_ref, mods_ref, g_ref, w_ref, wdt_ref, *rest):
    if rope:
        cos_ref, sa_ref, sb_ref, q_ref, kb_ref, vt_ref, z_ref, xbc_ref, dt_ref = rest
    else:
        q_ref, kb_ref, vt_ref, z_ref, xbc_ref, dt_ref, kf_ref, vf_ref = rest
    x = x_ref[...]
    shift = mods_ref[:, 0:D_MODEL]
    scale = mods_ref[:, D_MODEL:2 * D_MODEL]
    y = x * lax.rsqrt(jnp.mean(x * x, axis=-1, keepdims=True) + EPS) * g_ref[...]
    h = (y * (1.0 + scale) + shift).astype(BF16)
    r = jnp.dot(h, w_ref[...], preferred_element_type=F32)
    dt_ref[...] = jnp.dot(h, wdt_ref[...], preferred_element_type=F32)
    q = r[:, 0:D_QK]
    k = r[:, D_QK:2 * D_QK]
    if rope:
        cos = cos_ref[...]
        sa = sa_ref[...]
        sb = sb_ref[...]

        def rot(t):
            parts = []
            for hh in range(ATT_HEADS):
                th = t[:, hh * LANES:(hh + 1) * LANES]
                parts.append(th * cos + pltpu.roll(th, LANES - 16, 1) * sa + pltpu.roll(th, 16, 1) * sb)
            return jnp.concatenate(parts, axis=1)

        q = rot(q)
        k = rot(k)
    v = r[:, 2 * D_QK:2 * D_QK + D_ATT]
    q_ref[...] = q
    kb_ref[...] = k.astype(BF16)
    vt_ref[...] = v.T.astype(BF16)
    if not rope:
        kf_ref[...] = k
        vf_ref[...] = v
    z_ref[...] = r[:, 2 * D_QK + D_ATT:2 * D_QK + D_ATT + D_SSM]
    x0 = 2 * D_QK + D_ATT + D_SSM
    for cb in range(XBC_DIM // LANES):
        xbc_ref[cb] = r[:, x0 + cb * LANES:x0 + (cb + 1) * LANES]


def _inproj(x, mods3, mod_row0, mod_tokens, seq_len, g, w_main, w_dt, rope_tabs):
    n = x.shape[0]
    tm = 256
    per_seq = seq_len // tm
    per_mod = mod_tokens // tm
    rope = rope_tabs is not None
    in_specs = [pl.BlockSpec((tm, D_MODEL), lambda i: (i, 0)),
                pl.BlockSpec((None, 1, 6 * D_MODEL), lambda i: (mod_row0 + i // per_mod, 0, 0)),
                pl.BlockSpec((1, D_MODEL), lambda i: (0, 0)),
                pl.BlockSpec((D_MODEL, MAIN_COLS), lambda i: (0, 0)),
                pl.BlockSpec((D_MODEL, LANES), lambda i: (0, 0))]
    args = [x, mods3, g, w_main, w_dt]
    if rope:
        tab_spec = pl.BlockSpec((tm, LANES), lambda i: (i % per_seq, 0))
        in_specs += [tab_spec] * 3
        args += list(rope_tabs)
    def rows(wd, dtype=F32):
        return pl.BlockSpec((tm, wd), lambda i: (i, 0)), jax.ShapeDtypeStruct((n, wd), dtype)

    outs = [rows(D_QK), rows(D_QK, BF16),
            (pl.BlockSpec((None, D_ATT, tm), lambda i: (i // per_seq, 0, i % per_seq)),
             jax.ShapeDtypeStruct((n // seq_len, D_ATT, seq_len), BF16)),
            rows(D_SSM),
            (pl.BlockSpec((None, XBC_DIM // LANES, tm, LANES), lambda i: (i // per_seq, 0, i % per_seq, 0)),
             jax.ShapeDtypeStruct((n // seq_len, XBC_DIM // LANES, seq_len, LANES), F32)),
            rows(LANES)]
    if not rope:
        outs += [rows(D_QK), rows(D_ATT)]
    return pl.pallas_call(
        functools.partial(_inproj_kernel, rope),
        grid=(n // tm,),
        in_specs=in_specs,
        out_specs=[o[0] for o in outs],
        out_shape=[o[1] for o in outs],
        compiler_params=_cparams(("parallel",)),
        name="inproj_rope" if rope else "inproj",
    )(*args)


def _attn_kernel(tk, lam0, has_cache, q_ref, k_ref, vt_ref, *rest):
    if has_cache:
        ck_ref, cvt_ref, lamp_ref, g_ref, o_ref = rest
    else:
        lamp_ref, g_ref, o_ref = rest
    tq = q_ref.shape[0]
    chunks = [(k_ref, vt_ref, c * tk) for c in range(k_ref.shape[0] // tk)]
    if has_cache:
        chunks += [(ck_ref, cvt_ref, c * tk) for c in range(ck_ref.shape[0] // tk)]
    lp = lamp_ref[...]
    lam = (jnp.exp(jnp.sum(lp[0:1] * lp[1:2], axis=-1, keepdims=True))
           - jnp.exp(jnp.sum(lp[2:3] * lp[3:4], axis=-1, keepdims=True)) + lam0)
    q = q_ref[...] * (QK_DIM ** -0.5 * LOG2E)
    lane = lax.broadcasted_iota(jnp.int32, q.shape, 1)
    qq_t = jnp.concatenate([jnp.where(lane < QK_DIM, q, 0.0), jnp.where(lane >= QK_DIM, q, 0.0)],
                           axis=0).T.astype(BF16)

    def scores(chunk):
        kr, _, start = chunk
        return jnp.dot(kr[start:start + tk, :], qq_t, preferred_element_type=F32)

    ones_rows = jnp.ones((SUM_ROWS, tk), BF16)

    def update(s, chunk, m, acc):
        _, vr, start = chunk
        m_new = jnp.maximum(m, jnp.max(s, axis=0, keepdims=True))
        alpha = jnp.exp2(m - m_new)
        p = jnp.exp2(s - m_new).astype(BF16)
        v_ext = jnp.concatenate([vr[:, start:start + tk], ones_rows], axis=0)
        acc = alpha * acc + jnp.dot(v_ext, p, preferred_element_type=F32)
        return m_new, acc

    m = jnp.full((1, 2 * tq), -jnp.inf, F32)
    acc = jnp.zeros((V_DIM + SUM_ROWS, 2 * tq), F32)
    s = scores(chunks[0])
    for c, chunk in enumerate(chunks):
        s_next = scores(chunks[c + 1]) if c + 1 < len(chunks) else None
        m, acc = update(s, chunk, m, acc)
        s = s_next
    o = acc[0:V_DIM] / acc[V_DIM:V_DIM + 1]
    o = (o[:, 0:tq] - lam * o[:, tq:2 * tq]).T
    o = o * lax.rsqrt(jnp.mean(o * o, axis=-1, keepdims=True) + EPS)
    o_ref[...] = o * g_ref[...] * (1.0 - lam0)


def _attention(q, k, vt, cache, lamp, g, lam0, tq, tk):
    b, t, _ = q.shape

    def kv_specs(length):
        return [pl.BlockSpec((None, length, LANES), lambda bi, h, i: (bi, 0, h)),
                pl.BlockSpec((None, LANES, length), lambda bi, h, i: (bi, h, 0))]

    in_specs = [pl.BlockSpec((None, tq, LANES), lambda bi, h, i: (bi, i, h))] + kv_specs(t)
    args = [q, k, vt]
    if cache is not None:
        in_specs += kv_specs(cache[0].shape[1])
        args += list(cache)
    in_specs += [pl.BlockSpec((4, QK_DIM), lambda bi, h, i: (0, 0)),
                 pl.BlockSpec((1, V_DIM), lambda bi, h, i: (0, 0))]
    return pl.pallas_call(
        functools.partial(_attn_kernel, tk, lam0, cache is not None),
        grid=(b, ATT_HEADS, t // tq),
        in_specs=in_specs,
        out_specs=pl.BlockSpec((None, tq, LANES), lambda bi, h, i: (bi, i, h)),
        out_shape=jax.ShapeDtypeStruct((b, t, D_ATT), F32),
        compiler_params=_cparams(("parallel", "parallel", "arbitrary")),
        name="diff_attn",
    )(*args, lamp, g)


N_COLBLK = XBC_DIM // LANES
COL_CUM, COL_DT, COL_ECUM, COL_TOEND = 0, 8, 16, 24


def _expand_heads(cols, lane0):
    lane = lax.broadcasted_iota(jnp.int32, (cols.shape[0], LANES), 1)
    parts = []
    for pr in range(SSM_HEADS // 2):
        a = jnp.broadcast_to(cols[:, lane0 + 2 * pr:lane0 + 2 * pr + 1], (cols.shape[0], LANES))
        b = jnp.broadcast_to(cols[:, lane0 + 2 * pr + 1:lane0 + 2 * pr + 2], (cols.shape[0], LANES))
        parts.append(jnp.where(lane < SSM_HEADDIM, a, b))
    return jnp.concatenate(parts, axis=1)


def _head_scalars(reverse, dt_ref, alog_ref, dtb_ref):
    r0 = SSM_HEADS if reverse else 0
    xv = dt_ref[...].T[r0:r0 + SSM_HEADS, :] + dtb_ref[r0:r0 + SSM_HEADS, :]
    dt = jnp.maximum(xv, 0.0) + jnp.log(1.0 + jnp.exp(-jnp.abs(xv)))
    cum = dt * (-jnp.exp(alog_ref[r0:r0 + SSM_HEADS, :]))
    t_i = lax.broadcasted_iota(jnp.int32, (SSM_HEADS, CHUNK), 1)
    sh = 1
    while sh < CHUNK:
        if reverse:
            cum = cum + jnp.where(t_i < CHUNK - sh, pltpu.roll(cum, CHUNK - sh, 1), 0.0)
        else:
            cum = cum + jnp.where(t_i >= sh, pltpu.roll(cum, sh, 1), 0.0)
        sh *= 2
    end = 0 if reverse else CHUNK - 1
    cum_end = cum[:, end:end + 1]
    rows = jnp.concatenate([cum, dt, jnp.exp(cum), jnp.exp(cum_end - cum),
                            jnp.zeros((LANES - 4 * SSM_HEADS, CHUNK), F32)], axis=0)
    return cum, jnp.exp(cum_end), rows.T


def _ssd_chunk(reverse, xs, bm, cm, state, cum, dec, cols):
    row = lax.broadcasted_iota(jnp.int32, (CHUNK, LANES), 0)
    lane = lax.broadcasted_iota(jnp.int32, (CHUNK, LANES), 1)
    causal = (row <= lane) if reverse else (row >= lane)
    xd = xs * _expand_heads(cols, COL_DT)
    xdw = (xd * _expand_heads(cols, COL_TOEND)).astype(BF16)
    xd = xd.astype(BF16)
    ecum_x = _expand_heads(cols, COL_ECUM)
    rep = SSM_HEADS // SSM_GROUPS
    y_parts = []
    new_state = []
    for g in range(SSM_GROUPS):
        bg = bm[:, g * D_STATE:(g + 1) * D_STATE]
        cg = cm[:, g * D_STATE:(g + 1) * D_STATE]
        cbt = lax.dot_general(cg, bg, (((1,), (1,)), ((), ())), preferred_element_type=F32)
        rows = slice(g * rep * SSM_HEADDIM, (g + 1) * rep * SSM_HEADDIM)
        st_g = state[rows, :]
        y_off = lax.dot_general(cg, st_g.astype(BF16), (((1,), (1,)), ((), ())),
                                preferred_element_type=F32)
        cst = lax.dot_general(xdw[:, rows], bg, (((0,), (0,)), ((), ())), preferred_element_type=F32)
        for pr in range(rep // 2):
            y_pair = None
            for sub in range(2):
                h = g * rep + 2 * pr + sub
                seg = cols[:, COL_CUM + h:COL_CUM + h + 1] - cum[h:h + 1, :]
                decay = jnp.exp(jnp.where(causal, seg, -jnp.inf))
                sc = (cbt * decay).astype(BF16)
                xd_pair = xd[:, (h // 2) * LANES:(h // 2 + 1) * LANES]
                keep = (lane < SSM_HEADDIM) if sub == 0 else (lane >= SSM_HEADDIM)
                t = jnp.dot(sc, jnp.where(keep, xd_pair, jnp.zeros_like(xd_pair)), preferred_element_type=F32)
                y_pair = t if y_pair is None else y_pair + t
            y_parts.append(y_pair)
        dec_rows = jnp.concatenate(
            [jnp.broadcast_to(dec[g * rep + hh:g * rep + hh + 1, :], (SSM_HEADDIM, D_STATE)) for hh in range(rep)],
            axis=0)
        new_state.append(st_g * dec_rows + cst)
        y_parts[-2] = y_parts[-2] + y_off[:, 0:LANES] * ecum_x[:, rows][:, 0:LANES]
        y_parts[-1] = y_parts[-1] + y_off[:, LANES:2 * LANES] * ecum_x[:, rows][:, LANES:2 * LANES]
    return jnp.concatenate(y_parts, axis=1), jnp.concatenate(new_state, axis=0)


SSD_BATCH = 2


def _ssd_kernel(nc, xc_ref, xp_ref, xn_ref, dt_ref, z_ref, h0f_ref, h0b_ref, cw_ref, cb_ref, alog_ref, dtb_ref,
                d_ref, ng_ref, y_ref, hf_ref, hb_ref, state_ref, slab_ref, xs_ref, bc_ref, yf_ref):
    ps = pl.program_id(1)
    c = pl.program_id(2)
    n_bc = SSM_GROUPS * D_STATE

    @pl.when((ps == 0) & (c == 0))
    def _():
        state_ref[...] = h0f_ref[...]

    @pl.when(ps == 0)
    def _():
        t0 = pl.multiple_of(c * CHUNK, CHUNK)
        for bb in range(SSD_BATCH):
            slab_ref[bb, :, 0:HALO, :] = jnp.where(c > 0, xp_ref[bb], 0.0)
            slab_ref[bb, :, HALO:HALO + CHUNK, :] = xc_ref[bb]
            slab_ref[bb, :, HALO + CHUNK:2 * HALO + CHUNK, :] = jnp.where(c < nc - 1, xn_ref[bb], 0.0)
            blocks = []
            for cb in range(N_COLBLK):
                conv = cb_ref[cb:cb + 1, :]
                for kk in range(CONV_W):
                    off = HALO - CONV_W // 2 + kk
                    conv = conv + slab_ref[bb, cb, off:off + CHUNK, :] * cw_ref[kk, cb:cb + 1, :]
                blocks.append(_silu(conv))
            xs = jnp.concatenate(blocks[0:D_SSM // LANES], axis=1)
            bc = jnp.concatenate(blocks[D_SSM // LANES:], axis=1).astype(BF16)
            cum, dec, cols = _head_scalars(False, dt_ref.at[bb], alog_ref, dtb_ref)
            y, state = _ssd_chunk(False, xs, bc[:, 0:n_bc], bc[:, n_bc:], state_ref[bb], cum, dec, cols)
            xs_ref[bb, pl.ds(t0, CHUNK), :] = xs
            bc_ref[bb, pl.ds(t0, CHUNK), :] = bc
            yf_ref[bb, pl.ds(t0, CHUNK), :] = y
            state_ref[bb] = state

    @pl.when((ps == 0) & (c == nc - 1))
    def _():
        hf_ref[...] = state_ref[...]
        state_ref[...] = h0b_ref[...]

    @pl.when(ps == 1)
    def _():
        t0 = pl.multiple_of((nc - 1 - c) * CHUNK, CHUNK)
        for bb in range(SSD_BATCH):
            xs = xs_ref[bb, pl.ds(t0, CHUNK), :]
            bc = bc_ref[bb, pl.ds(t0, CHUNK), :]
            cum, dec, cols = _head_scalars(True, dt_ref.at[bb], alog_ref, dtb_ref)
            y, state = _ssd_chunk(True, xs, bc[:, 0:n_bc], bc[:, n_bc:], state_ref[bb], cum, dec, cols)
            state_ref[bb] = state
            y = y + yf_ref[bb, pl.ds(t0, CHUNK), :] + xs * d_ref[...]
            y = y * _silu(z_ref[bb])
            y = y * lax.rsqrt(jnp.mean(y * y, axis=-1, keepdims=True) + EPS)
            y_ref[bb] = y * ng_ref[...]

    @pl.when((ps == 1) & (c == nc - 1))
    def _():
        hb_ref[...] = state_ref[...]


def _ssd(xbc, dt, z, h0f, h0b, conv_w, conv_b, alog, dtb, d_x, norm_g):
    b, _, l, _ = xbc.shape
    assert b % SSD_BATCH == 0
    nb = SSD_BATCH
    nc = l // CHUNK
    hb = CHUNK // HALO
    n_state = SSM_HEADS * SSM_HEADDIM
    last = nc - 1

    def fwd_chunk(ps, ci):
        return jnp.where(ps == 0, ci, last)

    def any_chunk(ps, ci):
        return jnp.where(ps == 0, ci, last - ci)

    def bwd_chunk(ps, ci):
        return jnp.where(ps == 0, last, last - ci)

    const2 = lambda bi, ps, ci: (0, 0)
    state_spec = pl.BlockSpec((nb, n_state, D_STATE), lambda bi, ps, ci: (bi, 0, 0))
    in_specs = [pl.BlockSpec((nb, N_COLBLK, CHUNK, LANES), lambda bi, ps, ci: (bi, 0, fwd_chunk(ps, ci), 0)),
                pl.BlockSpec((nb, N_COLBLK, HALO, LANES),
                             lambda bi, ps, ci: (bi, 0, jnp.maximum(fwd_chunk(ps, ci) * hb - 1, 0), 0)),
                pl.BlockSpec((nb, N_COLBLK, HALO, LANES),
                             lambda bi, ps, ci: (bi, 0, jnp.minimum((fwd_chunk(ps, ci) + 1) * hb, l // HALO - 1), 0)),
                pl.BlockSpec((nb, CHUNK, LANES), lambda bi, ps, ci: (bi, any_chunk(ps, ci), 0)),
                pl.BlockSpec((nb, CHUNK, D_SSM), lambda bi, ps, ci: (bi, bwd_chunk(ps, ci), 0)),
                state_spec, state_spec,
                pl.BlockSpec((CONV_W, N_COLBLK, LANES), lambda bi, ps, ci: (0, 0, 0)),
                pl.BlockSpec((N_COLBLK, LANES), const2),
                pl.BlockSpec((2 * SSM_HEADS, CHUNK), const2),
                pl.BlockSpec((2 * SSM_HEADS, CHUNK), const2),
                pl.BlockSpec((1, D_SSM), const2),
                pl.BlockSpec((1, D_SSM), const2)]
    return pl.pallas_call(
        functools.partial(_ssd_kernel, nc),
        grid=(b // nb, 2, nc),
        in_specs=in_specs,
        out_specs=[pl.BlockSpec((nb, CHUNK, D_SSM), lambda bi, ps, ci: (bi, bwd_chunk(ps, ci), 0)),
                   state_spec, state_spec],
        out_shape=[jax.ShapeDtypeStruct((b, l, D_SSM), F32),
                   jax.ShapeDtypeStruct((b, n_state, D_STATE), F32),
                   jax.ShapeDtypeStruct((b, n_state, D_STATE), F32)],
        scratch_shapes=[pltpu.VMEM((nb, n_state, D_STATE), F32),
                        pltpu.VMEM((nb, N_COLBLK, CHUNK + 2 * HALO, LANES), F32),
                        pltpu.VMEM((nb, l, D_SSM), F32),
                        pltpu.VMEM((nb, l, 2 * SSM_GROUPS * D_STATE), BF16),
                        pltpu.VMEM((nb, l, D_SSM), F32)],
        compiler_params=_cparams(("parallel", "arbitrary", "arbitrary")),
        name="ssd",
    )(xbc, xbc, xbc, dt, z, h0f, h0b, conv_w, conv_b, alog, dtb, d_x, norm_g)


def _outproj_kernel(att_ref, ssm_ref, x_ref, mods_ref, wa_ref, ws_ref, g_ref, wr_ref, br_ref,
                    x1_ref, h2x_ref):
    mix = (jnp.dot(att_ref[...].astype(BF16), wa_ref[...], preferred_element_type=F32)
           + jnp.dot(ssm_ref[...].astype(BF16), ws_ref[...], preferred_element_type=F32))
    gate1 = mods_ref[:, 2 * D_MODEL:3 * D_MODEL]
    shift2 = mods_ref[:, 3 * D_MODEL:4 * D_MODEL]
    scale2 = mods_ref[:, 4 * D_MODEL:5 * D_MODEL]
    x1 = x_ref[...] + gate1 * mix
    x1_ref[...] = x1
    y = x1 * lax.rsqrt(jnp.mean(x1 * x1, axis=-1, keepdims=True) + EPS) * g_ref[...]
    h2 = y * (1.0 + scale2) + shift2
    h_hi = h2.astype(BF16)
    h2x_ref[:, 0:D_MODEL] = h2
    h_lo = (h2 - h_hi.astype(F32)).astype(BF16)
    w_hi = wr_ref[0]
    w_lo = wr_ref[1]
    logits = (jnp.dot(h_hi, w_hi, preferred_element_type=F32) + jnp.dot(h_lo, w_hi, preferred_element_type=F32)
              + jnp.dot(h_hi, w_lo, preferred_element_type=F32)) + br_ref[...]
    lane = lax.broadcasted_iota(jnp.int32, logits.shape, 1)
    neg = -jnp.inf
    big = jnp.int32(1 << 20)
    is_g = lane < N_GROUPS
    gl = jnp.where(is_g, logits, neg)
    gmax = jnp.max(gl, axis=-1, keepdims=True)
    g_idx = jnp.min(jnp.where(gl == gmax, lane, big), axis=-1, keepdims=True)
    p_g = 1.0 / jnp.sum(jnp.where(is_g, jnp.exp(gl - gmax), 0.0), axis=-1, keepdims=True)
    e_lo = N_GROUPS + g_idx * EXPERTS_PER_GROUP
    in_grp = (lane >= e_lo) & (lane < e_lo + EXPERTS_PER_GROUP)
    el = jnp.where(in_grp, logits, neg)
    m1 = jnp.max(el, axis=-1, keepdims=True)
    i1 = jnp.min(jnp.where(el == m1, lane, big), axis=-1, keepdims=True)
    el2 = jnp.where(lane == i1, neg, el)
    m2 = jnp.max(el2, axis=-1, keepdims=True)
    i2 = jnp.min(jnp.where(el2 == m2, lane, big), axis=-1, keepdims=True)
    e2 = jnp.exp(m2 - m1)
    w1 = p_g / (1.0 + e2)
    w2 = p_g * e2 / (1.0 + e2)
    cw = jnp.where(lane == i1, w1, 0.0) + jnp.where(lane == i2, w2, 0.0)
    slab = jnp.where(lane == EXPERTS_PER_GROUP, g_idx.astype(F32), 0.0)
    for j in range(EXPERTS_PER_GROUP):
        wj = jnp.sum(jnp.where(lane == e_lo + j, cw, 0.0), axis=-1, keepdims=True)
        slab = slab + jnp.where(lane == j, wj, 0.0)
    h2x_ref[:, D_MODEL:H2X_W] = slab


def _outproj(att, ssm, x, mods3, mod_row0, mod_tokens, wo_att, wo_ssm, g, w_router, b_router):
    n = x.shape[0]
    tm = 256
    per_mod = mod_tokens // tm
    return pl.pallas_call(
        _outproj_kernel,
        grid=(n // tm,),
        in_specs=[pl.BlockSpec((tm, D_ATT), lambda i: (i, 0)),
                  pl.BlockSpec((tm, D_SSM), lambda i: (i, 0)),
                  pl.BlockSpec((tm, D_MODEL), lambda i: (i, 0)),
                  pl.BlockSpec((None, 1, 6 * D_MODEL), lambda i: (mod_row0 + i // per_mod, 0, 0)),
                  pl.BlockSpec((D_ATT, D_MODEL), lambda i: (0, 0)),
                  pl.BlockSpec((D_SSM, D_MODEL), lambda i: (0, 0)),
                  pl.BlockSpec((1, D_MODEL), lambda i: (0, 0)),
                  pl.BlockSpec((2, D_MODEL, LANES), lambda i: (0, 0, 0)),
                  pl.BlockSpec((1, LANES), lambda i: (0, 0))],
        out_specs=[pl.BlockSpec((tm, D_MODEL), lambda i: (i, 0)),
                   pl.BlockSpec((tm, H2X_W), lambda i: (i, 0))],
        out_shape=[jax.ShapeDtypeStruct((n, D_MODEL), F32),
                   jax.ShapeDtypeStruct((n, H2X_W), F32)],
        compiler_params=_cparams(("parallel",)),
        name="outproj_router",
    )(att, ssm, x, mods3, wo_att, wo_ssm, g, w_router, b_router)


def _route_kernel(slab_ref, meta_ref):
    t_n = MOE_TILE
    blk = LANES
    slab = slab_ref[...]
    lane = lax.broadcasted_iota(jnp.int32, (t_n, LANES), 1)
    gcol = jnp.sum(jnp.where(lane == EXPERTS_PER_GROUP, slab, 0.0), axis=-1, keepdims=True)
    member = (lane.astype(F32) == gcol) & (lane < N_GROUPS)
    a = jnp.where(member, 1.0, 0.0).astype(BF16)
    r_i = lax.broadcasted_iota(jnp.int32, (blk, blk), 0)
    c_i = lax.broadcasted_iota(jnp.int32, (blk, blk), 1)
    lower = jnp.where(c_i < r_i, 1.0, 0.0).astype(BF16)
    upper = jnp.where(r_i < c_i, 1.0, 0.0).astype(BF16)
    offs = jnp.zeros((1, LANES), F32)
    ranks = []
    for b in range(t_n // blk):
        ab = a[b * blk:(b + 1) * blk]
        rb = jnp.dot(lower, ab, preferred_element_type=F32)
        ranks.append(rb + offs)
        offs = offs + rb[blk - 1:blk] + ab[blk - 1:blk].astype(F32)
    rank = jnp.concatenate(ranks, axis=0)
    n_chunk = jnp.floor((offs + (MOE_CHUNK - 1)) * (1.0 / MOE_CHUNK))
    start = jnp.dot(jnp.broadcast_to(n_chunk, (8, LANES)).astype(BF16), upper,
                    preferred_element_type=F32)[0:1]
    end = start + n_chunk
    dest = jnp.sum(jnp.where(member, start * MOE_CHUNK + rank, 0.0), axis=-1, keepdims=True)
    tok = lax.broadcasted_iota(jnp.int32, (t_n, LANES), 0)
    digits = jnp.where(lane == 0, (tok // blk).astype(F32),
                       jnp.where(lane == 1, (tok % blk).astype(F32), jnp.where(lane == 2, 1.0, 0.0))).astype(BF16)
    sw = 512
    pieces = []
    for sc in range(MOE_ROWS // sw):
        s_id = (lax.broadcasted_iota(jnp.int32, (t_n, sw), 1) + sc * sw).astype(F32)
        hit = jnp.where(dest == s_id, 1.0, 0.0).astype(BF16)
        r = lax.dot_general(digits, hit, (((0,), (0,)), ((), ())), preferred_element_type=F32)
        tok_of = r[0:1] * blk + r[1:2]
        pieces.append(jnp.where(r[2:3] > 0.5, tok_of, float(t_n)))
    pieces.append(jnp.full((1, META_ROWS - MOE_ROWS), float(t_n), F32))
    perm = jnp.concatenate(pieces, axis=1)
    slot = lax.broadcasted_iota(jnp.int32, (1, META_ROWS), 1).astype(F32)
    lane1 = lax.broadcasted_iota(jnp.int32, (1, LANES), 1)
    cg = jnp.zeros((1, META_ROWS), F32)
    for g in range(N_GROUPS):
        end_g = jnp.sum(jnp.where(lane1 == g, end, 0.0), axis=-1, keepdims=True)
        cg = cg + jnp.where(slot >= end_g, 1.0, 0.0)
    n_act = jnp.broadcast_to(end_g, (1, META_ROWS))
    meta_ref[...] = jnp.concatenate([perm, cg, n_act, jnp.zeros((5, META_ROWS), F32)], axis=0).astype(jnp.int32)


def _route(h2x):
    n = h2x.shape[0]
    n_tiles = n // MOE_TILE
    return pl.pallas_call(
        _route_kernel,
        grid=(n_tiles,),
        in_specs=[pl.BlockSpec((MOE_TILE, LANES), lambda i: (i, D_MODEL // LANES))],
        out_specs=pl.BlockSpec((None, 8, META_ROWS), lambda i: (i, 0, 0)),
        out_shape=jax.ShapeDtypeStruct((n_tiles, 8, META_ROWS), jnp.int32),
        compiler_params=_cparams(("parallel",)),
        name="moe_route",
    )(h2x)


def _moe_kernel(perm_ref, cg_ref, nact_ref, h_ref, wg_ref, wu_ref, wd_ref, y_ref, hs0, hs1, ys0, ys1):
    i = pl.program_id(0)
    s = pl.program_id(1)
    n_act = nact_ref[i]
    tile_base = i * META_ROWS
    hs = (hs0, hs1)
    ys = (ys0, ys1)

    def gather(chunk, dst):
        base = tile_base + chunk * MOE_CHUNK
        for r in range(MOE_CHUNK):
            src = jnp.minimum(perm_ref[base + r], MOE_TILE - 1)
            dst[r:r + 1, :] = h_ref[pl.ds(src, 1), :]

    def scatter(chunk, src):
        base = tile_base + chunk * MOE_CHUNK
        for r in range(MOE_CHUNK):
            y_ref[pl.ds(perm_ref[base + r], 1), :] = src[r:r + 1, :]

    def ffn(src, dst):
        hb = src[:, 0:D_MODEL].astype(BF16)
        cw = src[:, D_MODEL:H2X_W]
        acc = None
        for e in range(EXPERTS_PER_GROUP):
            a = jnp.dot(hb, wg_ref[e], preferred_element_type=F32)
            u = jnp.dot(hb, wu_ref[e], preferred_element_type=F32)
            hid = (_silu(a) * u * cw[:, e:e + 1]).astype(BF16)
            t = jnp.dot(hid, wd_ref[e], preferred_element_type=F32)
            acc = t if acc is None else acc + t
        dst[...] = acc

    @pl.when(s == 0)
    def _():
        y_ref[MOE_TILE:MOE_TILE + 8, :] = jnp.zeros((8, D_MODEL), F32)
        ys1[...] = jnp.zeros_like(ys1)
        gather(0, hs0)

    for par in (0, 1):
        @pl.when((s < n_act) & (s % 2 == par))
        def _():
            gather(s + 1, hs[1 - par])
            ffn(hs[par], ys[par])
            scatter(jnp.maximum(s - 1, 0), ys[1 - par])

        @pl.when((s == n_act) & (s % 2 == par))
        def _():
            scatter(s - 1, ys[1 - par])


def _moe(h2x, perm, cgrp, nact, wg, wu, wd):
    n = h2x.shape[0]
    n_tiles = n // MOE_TILE

    def w_idx(i, s, perm_ref, cg_ref, nact_ref):
        return (jnp.minimum(cg_ref[i * MOE_STEPS + s], N_GROUPS - 1), 0, 0)

    return pl.pallas_call(
        _moe_kernel,
        grid_spec=pltpu.PrefetchScalarGridSpec(
            num_scalar_prefetch=3,
            grid=(n_tiles, MOE_STEPS),
            in_specs=[pl.BlockSpec((MOE_TILE, H2X_W), lambda i, s, p, c, a: (i, 0)),
                      pl.BlockSpec((EXPERTS_PER_GROUP, D_MODEL, EXPERT_FF), w_idx),
                      pl.BlockSpec((EXPERTS_PER_GROUP, D_MODEL, EXPERT_FF), w_idx),
                      pl.BlockSpec((EXPERTS_PER_GROUP, EXPERT_FF, D_MODEL), w_idx)],
            out_specs=pl.BlockSpec((None, MOE_TILE + 8, D_MODEL), lambda i, s, p, c, a: (i, 0, 0)),
            scratch_shapes=[pltpu.VMEM((MOE_CHUNK, H2X_W), F32), pltpu.VMEM((MOE_CHUNK, H2X_W), F32),
                            pltpu.VMEM((MOE_CHUNK, D_MODEL), F32), pltpu.VMEM((MOE_CHUNK, D_MODEL), F32)]),
        out_shape=jax.ShapeDtypeStruct((n_tiles, MOE_TILE + 8, D_MODEL), F32),
        compiler_params=_cparams(("parallel", "arbitrary")),
        name="moe_experts",
    )(perm, cgrp, nact, h2x, wg, wu, wd)


def _final_kernel(y_ref, x1_ref, mods_ref, fg_ref, o_ref):
    gate2 = mods_ref[:, 5 * D_MODEL:6 * D_MODEL]
    x2 = x1_ref[...] + gate2 * y_ref[...]
    o_ref[...] = x2 * lax.rsqrt(jnp.mean(x2 * x2, axis=-1, keepdims=True) + EPS) * fg_ref[...]


def _final(y, x1, mods3, mod_row0, mod_tokens, fg):
    n = x1.shape[0]
    tm = 512
    per_mod = mod_tokens // tm
    per_tile = MOE_TILE // tm
    return pl.pallas_call(
        _final_kernel,
        grid=(n // tm,),
        in_specs=[pl.BlockSpec((None, tm, D_MODEL), lambda j: (j // per_tile, j % per_tile, 0)),
                  pl.BlockSpec((tm, D_MODEL), lambda j: (j, 0)),
                  pl.BlockSpec((None, 1, 6 * D_MODEL), lambda j: (mod_row0 + j // per_mod, 0, 0)),
                  pl.BlockSpec((1, D_MODEL), lambda j: (0, 0))],
        out_specs=pl.BlockSpec((tm, D_MODEL), lambda j: (j, 0)),
        out_shape=jax.ShapeDtypeStruct((n, D_MODEL), F32),
        compiler_params=_cparams(("parallel",)),
        name="final_norm",
    )(y, x1, mods3, fg)


def _rope_tables(t):
    n_freq = QK_DIM // 4
    n_rows = t // GRID_W
    freqs = ROPE_BASE ** (-jnp.arange(n_freq, dtype=F32) / n_freq)
    ang_r = jnp.arange(n_rows, dtype=F32)[:, None] * freqs
    ang_c = jnp.arange(GRID_W, dtype=F32)[:, None] * freqs
    cr, sr, cc, sc = lax.optimization_barrier((jnp.cos(ang_r), jnp.sin(ang_r), jnp.cos(ang_c), jnp.sin(ang_c)))
    j = np.arange(LANES) % QK_DIM
    f_idx = j % n_freq
    by_row = (j < QK_DIM // 2)[None, None, :]
    first = ((j % (QK_DIM // 2)) < n_freq)[None, None, :]

    def table(r_small, c_small):
        return jnp.where(by_row, r_small[:, f_idx][:, None, :], c_small[:, f_idx][None, :, :])

    cos = table(cr, cc)
    sin = table(sr, sc)
    return (cos.reshape(t, LANES), jnp.where(first, -sin, 0.0).reshape(t, LANES),
            jnp.where(first, 0.0, sin).reshape(t, LANES))


def _layer(x, mods3, mod_row0, mod_tokens, rope_tabs, ctx_k, ctx_v, h0f, h0b, lw, layer):
    b, t, _ = x.shape
    n = b * t
    xf = x.reshape(n, D_MODEL)
    res = _inproj(xf, mods3, mod_row0, mod_tokens, t, lw["norm_mix_g"], lw["w_main"], lw["w_dt"], rope_tabs)
    q, kb, vt, z, xbc, dt = res[:6]
    if ctx_k is None:
        cache = None
        k3, v3 = res[6].reshape(b, t, D_QK), res[7].reshape(b, t, D_ATT)
    else:
        cache = (ctx_k.astype(BF16), jnp.swapaxes(ctx_v, 1, 2).astype(BF16))
        k3 = v3 = None
    lam0 = 0.8 - 0.6 * math.exp(-0.3 * layer)
    tq = 512 if t % 512 == 0 else 256
    tk = 512 if t % 512 == 0 else 256
    att = _attention(q.reshape(b, t, D_QK), kb.reshape(b, t, D_QK), vt, cache, lw["lamp"], lw["attn_subln_g"],
                     lam0, tq, tk)
    ssm, hf, hb = _ssd(xbc, dt.reshape(b, t, LANES), z.reshape(b, t, D_SSM), h0f, h0b, lw["conv_w"], lw["conv_b"],
                       lw["alog"], lw["dtb"], lw["d_x"], lw["ssm_norm_g"])
    x1, h2x = _outproj(att.reshape(n, D_ATT), ssm.reshape(n, D_SSM), xf, mods3, mod_row0, mod_tokens,
                       lw["wo_att"], lw["wo_ssm"], lw["norm_ffn_g"], lw["w_router"], lw["b_router"])
    meta = _route(h2x)
    perm = meta[:, 0, :].reshape(-1)
    cgrp = meta[:, 1, :MOE_STEPS].reshape(-1)
    nact = meta[:, 2, 0]
    y = _moe(h2x, perm, cgrp, nact, lw["wg"], lw["wu"], lw["wd"])
    out = _final(y, x1, mods3, mod_row0, mod_tokens, lw["final_g"])
    return out.reshape(b, t, D_MODEL), k3, v3, hf, hb


def _pad_lanes(v, width=LANES):
    return jnp.pad(v, [(0, 0)] * (v.ndim - 1) + [(0, width - v.shape[-1])])


def kernel(x_prompt, x_sample, cache_k, cache_v, state_ssm_fwd, state_ssm_bwd, c, c_ctx, w_ada, b_ada, norm_mix_g, w_in, w_out, lambda_q1, lambda_k1, lambda_q2, lambda_k2, attn_subln_g, conv_w, conv_b, a_log_fwd, a_log_bwd, dt_bias_fwd, dt_bias_bwd, ssm_d, ssm_norm_g, norm_ffn_g, w_group_router, b_group_router, w_expert_router, b_expert_router, w_exp_gate, w_exp_up, w_exp_down, final_norm_g):
    depth = w_in.shape[0]
    assert depth == 1, "single trunk layer"
    bp, tp, _ = x_prompt.shape
    bs, ts, _ = x_sample.shape
    l = 0
    cond = jnp.concatenate([c_ctx[None], c], axis=0)
    condT = _pad_lanes(cond.T, 8)
    mods = _ada(condT, w_ada[l], b_ada[l][None])
    mods3 = mods.reshape(8, 1, 6 * D_MODEL)

    w_router = _pad_lanes(jnp.concatenate([w_group_router[l], w_expert_router[l]], axis=1))
    wr_hi = w_router.astype(BF16)
    wr_lo = (w_router - wr_hi.astype(F32)).astype(BF16)
    lw = dict(
        norm_mix_g=norm_mix_g[l][None],
        w_main=w_in[l].astype(BF16),
        w_dt=_pad_lanes(w_in[l][:, MAIN_COLS:]).astype(BF16),
        lamp=jnp.stack([lambda_q1[l], lambda_k1[l], lambda_q2[l], lambda_k2[l]]),
        attn_subln_g=attn_subln_g[l][None],
        conv_w=conv_w[l].reshape(CONV_W, N_COLBLK, LANES), conv_b=conv_b[l].reshape(N_COLBLK, LANES),
        alog=jnp.broadcast_to(jnp.concatenate([a_log_fwd[l], a_log_bwd[l]])[:, None], (2 * SSM_HEADS, CHUNK)),
        dtb=jnp.broadcast_to(jnp.concatenate([dt_bias_fwd[l], dt_bias_bwd[l]])[:, None], (2 * SSM_HEADS, CHUNK)),
        d_x=jnp.repeat(ssm_d[l], SSM_HEADDIM)[None], ssm_norm_g=ssm_norm_g[l][None],
        wo_att=w_out[l][:D_ATT].astype(BF16), wo_ssm=w_out[l][D_ATT:].astype(BF16),
        norm_ffn_g=norm_ffn_g[l][None],
        w_router=jnp.stack([wr_hi, wr_lo]),
        b_router=_pad_lanes(jnp.concatenate([b_group_router[l], b_expert_router[l]])[None]),
        wg=w_exp_gate[l].astype(BF16), wu=w_exp_up[l].astype(BF16), wd=w_exp_down[l].astype(BF16),
        final_g=final_norm_g[None],
    )
    n_state = SSM_HEADS * SSM_HEADDIM
    zeros_state = jnp.zeros((bp, n_state, D_STATE), F32)
    yp, ck, cv, hf, hb = _layer(x_prompt, mods3, 0, bp * tp, None, None, None, zeros_state, zeros_state, lw, l)
    ys, _, _, _, _ = _layer(x_sample, mods3, 1, ts, _rope_tables(ts),
                            cache_k[:, l].reshape(bs, -1, D_QK), cache_v[:, l].reshape(bs, -1, D_ATT),
                            state_ssm_fwd[:, l].reshape(bs, n_state, D_STATE),
                            state_ssm_bwd[:, l].reshape(bs, n_state, D_STATE), lw, l)
    new_k = ck.reshape(bp, 1, tp, ATT_HEADS, 2, QK_DIM)
    new_v = cv.reshape(bp, 1, tp, ATT_HEADS, V_DIM)
    new_hf = hf.reshape(bp, 1, SSM_HEADS, SSM_HEADDIM, D_STATE)
    new_hb = hb.reshape(bp, 1, SSM_HEADS, SSM_HEADDIM, D_STATE)
    return yp, ys, new_k, new_v, new_hf, new_hb
```

```python
import functools
import math

import numpy as np
import jax
import jax.numpy as jnp
from jax import lax
from jax.experimental import pallas as pl
from jax.experimental.pallas import tpu as pltpu

D_MODEL = 1024
GRID_W = 64
ATT_HEADS = 4
QK_DIM = 64
V_DIM = 128
D_QK = 512
D_ATT = 512
ROPE_BASE = 10000.0
D_SSM = 512
SSM_HEADDIM = 64
SSM_HEADS = 8
SSM_GROUPS = 2
D_STATE = 128
CONV_W = 5
CHUNK = 128
XBC_DIM = 1024
N_GROUPS = 4
EXPERTS_PER_GROUP = 4
N_EXPERTS = 16
EXPERT_FF = 256
EPS = 1e-6
MAIN_COLS = 2 * D_QK + D_ATT + D_SSM + XBC_DIM
H2X_W = D_MODEL + 128
MOE_TILE = 2048
MOE_CHUNK = 256
MOE_SLOTS = MOE_TILE // MOE_CHUNK + N_GROUPS
MOE_ROWS = MOE_SLOTS * MOE_CHUNK
MOE_STEPS = MOE_SLOTS + 1
META_ROWS = MOE_STEPS * MOE_CHUNK
LANES = 128
HALO = 8
VMEM_LIMIT = 56 * 1024 * 1024

LOG2E = math.log2(math.e)
SUM_ROWS = 16
F32 = jnp.float32
BF16 = jnp.bfloat16


def _cparams(sem):
    return pltpu.CompilerParams(dimension_semantics=sem, vmem_limit_bytes=VMEM_LIMIT)


def _sigmoid(x):
    return 1.0 / (1.0 + jnp.exp(-x))


def _silu(x):
    return x * _sigmoid(x)


def _ada_kernel(condT_ref, w_ref, b_ref, o_ref):
    s = _silu(condT_ref[...])
    w = w_ref[...]
    b = b_ref[...]
    o_ref[...] = jnp.zeros_like(o_ref)
    for r in range(3):
        o_ref[r:r + 1, :] = jnp.sum(w * s[:, r:r + 1], axis=0, keepdims=True) + b


def _ada(condT, w_ada, b_ada):
    bn = 512
    n = w_ada.shape[1]
    return pl.pallas_call(
        _ada_kernel,
        grid=(n // bn,),
        in_specs=[pl.BlockSpec((D_MODEL, 8), lambda j: (0, 0)),
                  pl.BlockSpec((D_MODEL, bn), lambda j: (0, j)),
                  pl.BlockSpec((1, bn), lambda j: (0, j))],
        out_specs=pl.BlockSpec((8, bn), lambda j: (0, j)),
        out_shape=jax.ShapeDtypeStruct((8, n), F32),
        compiler_params=_cparams(("arbitrary",)),
        name="ada",
    )(condT, w_ada, b_ada)


def _inproj_kernel(rope, x_ref, mods_ref, g_ref, w_ref, wdt_ref, *rest):
    if rope:
        cos_ref, sa_ref, sb_ref, q_ref, kb_ref, vt_ref, z_ref, xbc_ref, dt_ref = rest
    else:
        q_ref, kb_ref, vt_ref, z_ref, xbc_ref, dt_ref, kf_ref, vf_ref = rest
    x = x_ref[...]
    shift = mods_ref[:, 0:D_MODEL]
    scale = mods_ref[:, D_MODEL:2 * D_MODEL]
    y = x * lax.rsqrt(jnp.mean(x * x, axis=-1, keepdims=True) + EPS) * g_ref[...]
    h = (y * (1.0 + scale) + shift).astype(BF16)
    r = jnp.dot(h, w_ref[...], preferred_element_type=F32)
    dt_ref[...] = jnp.dot(h, wdt_ref[...], preferred_element_type=F32)
    q = r[:, 0:D_QK]
    k = r[:, D_QK:2 * D_QK]
    if rope:
        cos = cos_ref[...]
        sa = sa_ref[...]
        sb = sb_ref[...]

        def rot(t):
            parts = []
            for hh in range(ATT_HEADS):
                th = t[:, hh * LANES:(hh + 1) * LANES]
                parts.append(th * cos + pltpu.roll(th, LANES - 16, 1) * sa + pltpu.roll(th, 16, 1) * sb)
            return jnp.concatenate(parts, axis=1)

        q = rot(q)
        k = rot(k)
    v = r[:, 2 * D_QK:2 * D_QK + D_ATT]
    q_ref[...] = q
    kb_ref[...] = k.astype(BF16)
    vt_ref[...] = v.T.astype(BF16)
    if not rope:
        kf_ref[...] = k
        vf_ref[...] = v
    z_ref[...] = r[:, 2 * D_QK + D_ATT:2 * D_QK + D_ATT + D_SSM]
    x0 = 2 * D_QK + D_ATT + D_SSM
    for cb in range(XBC_DIM // LANES):
        xbc_ref[cb] = r[:, x0 + cb * LANES:x0 + (cb + 1) * LANES]


def _inproj(x, mods3, mod_row0, mod_tokens, seq_len, g, w_main, w_dt, rope_tabs):
    n = x.shape[0]
    tm = 256
    per_seq = seq_len // tm
    per_mod = mod_tokens // tm
    rope = rope_tabs is not None
    in_specs = [pl.BlockSpec((tm, D_MODEL), lambda i: (i, 0)),
                pl.BlockSpec((None, 1, 6 * D_MODEL), lambda i: (mod_row0 + i // per_mod, 0, 0)),
                pl.BlockSpec((1, D_MODEL), lambda i: (0, 0)),
                pl.BlockSpec((D_MODEL, MAIN_COLS), lambda i: (0, 0)),
                pl.BlockSpec((D_MODEL, LANES), lambda i: (0, 0))]
    args = [x, mods3, g, w_main, w_dt]
    if rope:
        tab_spec = pl.BlockSpec((tm, LANES), lambda i: (i % per_seq, 0))
        in_specs += [tab_spec] * 3
        args += list(rope_tabs)
    def rows(wd, dtype=F32):
        return pl.BlockSpec((tm, wd), lambda i: (i, 0)), jax.ShapeDtypeStruct((n, wd), dtype)

    outs = [rows(D_QK), rows(D_QK, BF16),
            (pl.BlockSpec((None, D_ATT, tm), lambda i: (i // per_seq, 0, i % per_seq)),
             jax.ShapeDtypeStruct((n // seq_len, D_ATT, seq_len), BF16)),
            rows(D_SSM),
            (pl.BlockSpec((None, XBC_DIM // LANES, tm, LANES), lambda i: (i // per_seq, 0, i % per_seq, 0)),
             jax.ShapeDtypeStruct((n // seq_len, XBC_DIM // LANES, seq_len, LANES), F32)),
            rows(LANES)]
    if not rope:
        outs += [rows(D_QK), rows(D_ATT)]
    return pl.pallas_call(
        functools.partial(_inproj_kernel, rope),
        grid=(n // tm,),
        in_specs=in_specs,
        out_specs=[o[0] for o in outs],
        out_shape=[o[1] for o in outs],
        compiler_params=_cparams(("parallel",)),
        name="inproj_rope" if rope else "inproj",
    )(*args)


def _attn_kernel(tk, lam0, has_cache, q_ref, k_ref, vt_ref, *rest):
    if has_cache:
        ck_ref, cvt_ref, lamp_ref, g_ref, o_ref = rest
    else:
        lamp_ref, g_ref, o_ref = rest
    tq = q_ref.shape[0]
    chunks = [(k_ref, vt_ref, c * tk) for c in range(k_ref.shape[0] // tk)]
    if has_cache:
        chunks += [(ck_ref, cvt_ref, c * tk) for c in range(ck_ref.shape[0] // tk)]
    lp = lamp_ref[...]
    lam = (jnp.exp(jnp.sum(lp[0:1] * lp[1:2], axis=-1, keepdims=True))
           - jnp.exp(jnp.sum(lp[2:3] * lp[3:4], axis=-1, keepdims=True)) + lam0)
    q = q_ref[...] * (QK_DIM ** -0.5 * LOG2E)
    lane = lax.broadcasted_iota(jnp.int32, q.shape, 1)
    qq_t = jnp.concatenate([jnp.where(lane < QK_DIM, q, 0.0), jnp.where(lane >= QK_DIM, q, 0.0)],
                           axis=0).T.astype(BF16)

    def scores(chunk):
        kr, _, start = chunk
        return jnp.dot(kr[start:start + tk, :], qq_t, preferred_element_type=F32)

    ones_rows = jnp.ones((SUM_ROWS, tk), BF16)

    def update(s, chunk, m, acc):
        _, vr, start = chunk
        m_new = jnp.maximum(m, jnp.max(s, axis=0, keepdims=True))
        alpha = jnp.exp2(m - m_new)
        p = jnp.exp2(s - m_new).astype(BF16)
        v_ext = jnp.concatenate([vr[:, start:start + tk], ones_rows], axis=0)
        acc = alpha * acc + jnp.dot(v_ext, p, preferred_element_type=F32)
        return m_new, acc

    m = jnp.full((1, 2 * tq), -jnp.inf, F32)
    acc = jnp.zeros((V_DIM + SUM_ROWS, 2 * tq), F32)
    s = scores(chunks[0])
    for c, chunk in enumerate(chunks):
        s_next = scores(chunks[c + 1]) if c + 1 < len(chunks) else None
        m, acc = update(s, chunk, m, acc)
        s = s_next
    o = acc[0:V_DIM] / acc[V_DIM:V_DIM + 1]
    o = (o[:, 0:tq] - lam * o[:, tq:2 * tq]).T
    o = o * lax.rsqrt(jnp.mean(o * o, axis=-1, keepdims=True) + EPS)
    o_ref[...] = o * g_ref[...] * (1.0 - lam0)


def _attention(q, k, vt, cache, lamp, g, lam0, tq, tk):
    b, t, _ = q.shape

    def kv_specs(length):
        return [pl.BlockSpec((None, length, LANES), lambda bi, h, i: (bi, 0, h)),
                pl.BlockSpec((None, LANES, length), lambda bi, h, i: (bi, h, 0))]

    in_specs = [pl.BlockSpec((None, tq, LANES), lambda bi, h, i: (bi, i, h))] + kv_specs(t)
    args = [q, k, vt]
    if cache is not None:
        in_specs += kv_specs(cache[0].shape[1])
        args += list(cache)
    in_specs += [pl.BlockSpec((4, QK_DIM), lambda bi, h, i: (0, 0)),
                 pl.BlockSpec((1, V_DIM), lambda bi, h, i: (0, 0))]
    return pl.pallas_call(
        functools.partial(_attn_kernel, tk, lam0, cache is not None),
        grid=(b, ATT_HEADS, t // tq),
        in_specs=in_specs,
        out_specs=pl.BlockSpec((None, tq, LANES), lambda bi, h, i: (bi, i, h)),
        out_shape=jax.ShapeDtypeStruct((b, t, D_ATT), F32),
        compiler_params=_cparams(("parallel", "parallel", "arbitrary")),
        name="diff_attn",
    )(*args, lamp, g)


N_COLBLK = XBC_DIM // LANES
COL_CUM, COL_DT, COL_ECUM, COL_TOEND = 0, 8, 16, 24


def _expand_matrix():
    e = np.zeros((LANES, 3 * D_SSM), np.float32)
    for blk, lane0 in enumerate((COL_DT, COL_ECUM, COL_TOEND)):
        for h in range(SSM_HEADS):
            e[lane0 + h, blk * D_SSM + h * SSM_HEADDIM:blk * D_SSM + (h + 1) * SSM_HEADDIM] = 1.0
    return jnp.asarray(np.concatenate([e, e, e], axis=0), BF16)


def _expand_heads(cols, e_ref):
    return jnp.dot(_split3(cols), e_ref[...], preferred_element_type=F32)


def _split3(x):
    hi = x.astype(BF16)
    r1 = x - hi.astype(F32)
    mid = r1.astype(BF16)
    lo = (r1 - mid.astype(F32)).astype(BF16)
    return jnp.concatenate([hi, mid, lo], axis=-1)


def _scan_matrices():
    t = np.arange(CHUNK)
    pre = (t[:, None] <= t[None, :]).astype(np.float32)
    suf = (t[:, None] >= t[None, :]).astype(np.float32)
    return jnp.asarray(np.stack([np.concatenate([pre] * 3, axis=0), np.concatenate([suf] * 3, axis=0)]), BF16)


def _head_scalars(reverse, dt_ref, alog_ref, dtb_ref, scan_ref):
    r0 = SSM_HEADS if reverse else 0
    xv = dt_ref[...].T[r0:r0 + SSM_HEADS, :] + dtb_ref[r0:r0 + SSM_HEADS, :]
    dt = jnp.maximum(xv, 0.0) + jnp.log(1.0 + jnp.exp(-jnp.abs(xv)))
    la = dt * (-jnp.exp(alog_ref[r0:r0 + SSM_HEADS, :]))
    cum = jnp.dot(_split3(la), scan_ref[1 if reverse else 0], preferred_element_type=F32)
    end = 0 if reverse else CHUNK - 1
    cum_end = cum[:, end:end + 1]
    rows = jnp.concatenate([cum, dt, jnp.exp(cum), jnp.exp(cum_end - cum),
                            jnp.zeros((LANES - 4 * SSM_HEADS, CHUNK), F32)], axis=0)
    return cum, jnp.exp(cum_end), rows.T


def _ssd_chunk(reverse, xs, bm, cm, state, cum, dec, cols, e_ref):
    row = lax.broadcasted_iota(jnp.int32, (CHUNK, LANES), 0)
    lane = lax.broadcasted_iota(jnp.int32, (CHUNK, LANES), 1)
    causal = (row <= lane) if reverse else (row >= lane)
    spread = _expand_heads(cols, e_ref)
    xd = xs * spread[:, 0:D_SSM]
    xdw = (xd * spread[:, 2 * D_SSM:3 * D_SSM]).astype(BF16)
    xd = xd.astype(BF16)
    ecum_x = spread[:, D_SSM:2 * D_SSM]
    rep = SSM_HEADS // SSM_GROUPS
    y_parts = []
    new_state = []
    for g in range(SSM_GROUPS):
        bg = bm[:, g * D_STATE:(g + 1) * D_STATE]
        cg = cm[:, g * D_STATE:(g + 1) * D_STATE]
        cbt = lax.dot_general(cg, bg, (((1,), (1,)), ((), ())), preferred_element_type=F32)
        rows = slice(g * rep * SSM_HEADDIM, (g + 1) * rep * SSM_HEADDIM)
        st_g = state[rows, :]
        y_off = lax.dot_general(cg, st_g.astype(BF16), (((1,), (1,)), ((), ())),
                                preferred_element_type=F32)
        cst = lax.dot_general(xdw[:, rows], bg, (((0,), (0,)), ((), ())), preferred_element_type=F32)
        for pr in range(rep // 2):
            y_pair = None
            for sub in range(2):
                h = g * rep + 2 * pr + sub
                seg = cols[:, COL_CUM + h:COL_CUM + h + 1] - cum[h:h + 1, :]
                decay = jnp.exp(jnp.where(causal, seg, -jnp.inf))
                sc = (cbt * decay).astype(BF16)
                xd_pair = xd[:, (h // 2) * LANES:(h // 2 + 1) * LANES]
                keep = (lane < SSM_HEADDIM) if sub == 0 else (lane >= SSM_HEADDIM)
                t = jnp.dot(sc, jnp.where(keep, xd_pair, jnp.zeros_like(xd_pair)), preferred_element_type=F32)
                y_pair = t if y_pair is None else y_pair + t
            y_parts.append(y_pair)
        dec_rows = jnp.concatenate(
            [jnp.broadcast_to(dec[g * rep + hh:g * rep + hh + 1, :], (SSM_HEADDIM, D_STATE)) for hh in range(rep)],
            axis=0)
        new_state.append(st_g * dec_rows + cst)
        y_parts[-2] = y_parts[-2] + y_off[:, 0:LANES] * ecum_x[:, rows][:, 0:LANES]
        y_parts[-1] = y_parts[-1] + y_off[:, LANES:2 * LANES] * ecum_x[:, rows][:, LANES:2 * LANES]
    return jnp.concatenate(y_parts, axis=1), jnp.concatenate(new_state, axis=0)


def _ssd_kernel(nc, xc_ref, xp_ref, xn_ref, dt_ref, z_ref, h0f_ref, h0b_ref, cw_ref, cb_ref, alog_ref, dtb_ref,
                d_ref, ng_ref, e_ref, scan_ref, y_ref, hf_ref, hb_ref, state_ref, slab_ref, xs_ref, bc_ref, yf_ref):
    ps = pl.program_id(1)
    c = pl.program_id(2)
    n_bc = SSM_GROUPS * D_STATE

    @pl.when((ps == 0) & (c == 0))
    def _():
        state_ref[...] = h0f_ref[...]

    @pl.when(ps == 0)
    def _():
        t0 = pl.multiple_of(c * CHUNK, CHUNK)
        for bb in range(state_ref.shape[0]):
            slab_ref[bb, :, 0:HALO, :] = jnp.where(c > 0, xp_ref[bb], 0.0)
            slab_ref[bb, :, HALO:HALO + CHUNK, :] = xc_ref[bb]
            slab_ref[bb, :, HALO + CHUNK:2 * HALO + CHUNK, :] = jnp.where(c < nc - 1, xn_ref[bb], 0.0)
            blocks = []
            for cb in range(N_COLBLK):
                conv = cb_ref[cb:cb + 1, :]
                for kk in range(CONV_W):
                    off = HALO - CONV_W // 2 + kk
                    conv = conv + slab_ref[bb, cb, off:off + CHUNK, :] * cw_ref[kk, cb:cb + 1, :]
                blocks.append(_silu(conv))
            xs = jnp.concatenate(blocks[0:D_SSM // LANES], axis=1)
            bc = jnp.concatenate(blocks[D_SSM // LANES:], axis=1).astype(BF16)
            cum, dec, cols = _head_scalars(False, dt_ref.at[bb], alog_ref, dtb_ref, scan_ref)
            y, state = _ssd_chunk(False, xs, bc[:, 0:n_bc], bc[:, n_bc:], state_ref[bb], cum, dec, cols, e_ref)
            xs_ref[bb, pl.ds(t0, CHUNK), :] = xs
            bc_ref[bb, pl.ds(t0, CHUNK), :] = bc
            yf_ref[bb, pl.ds(t0, CHUNK), :] = y
            state_ref[bb] = state

    @pl.when((ps == 0) & (c == nc - 1))
    def _():
        hf_ref[...] = state_ref[...]
        state_ref[...] = h0b_ref[...]

    @pl.when(ps == 1)
    def _():
        t0 = pl.multiple_of((nc - 1 - c) * CHUNK, CHUNK)
        for bb in range(state_ref.shape[0]):
            xs = xs_ref[bb, pl.ds(t0, CHUNK), :]
            bc = bc_ref[bb, pl.ds(t0, CHUNK), :]
            cum, dec, cols = _head_scalars(True, dt_ref.at[bb], alog_ref, dtb_ref, scan_ref)
            y, state = _ssd_chunk(True, xs, bc[:, 0:n_bc], bc[:, n_bc:], state_ref[bb], cum, dec, cols, e_ref)
            state_ref[bb] = state
            y = y + yf_ref[bb, pl.ds(t0, CHUNK), :] + xs * d_ref[...]
            y = y * _silu(z_ref[bb])
            y = y * lax.rsqrt(jnp.mean(y * y, axis=-1, keepdims=True) + EPS)
            y_ref[bb] = y * ng_ref[...]

    @pl.when((ps == 1) & (c == nc - 1))
    def _():
        hb_ref[...] = state_ref[...]


def _ssd(xbc, dt, z, h0f, h0b, conv_w, conv_b, alog, dtb, d_x, norm_g):
    b, _, l, _ = xbc.shape
    nb = 2
    assert b % nb == 0
    nc = l // CHUNK
    hb = CHUNK // HALO
    n_state = SSM_HEADS * SSM_HEADDIM
    last = nc - 1

    def fwd_chunk(ps, ci):
        return jnp.where(ps == 0, ci, last)

    def any_chunk(ps, ci):
        return jnp.where(ps == 0, ci, last - ci)

    def bwd_chunk(ps, ci):
        return jnp.where(ps == 0, last, last - ci)

    const2 = lambda bi, ps, ci: (0, 0)
    state_spec = pl.BlockSpec((nb, n_state, D_STATE), lambda bi, ps, ci: (bi, 0, 0))
    in_specs = [pl.BlockSpec((nb, N_COLBLK, CHUNK, LANES), lambda bi, ps, ci: (bi, 0, fwd_chunk(ps, ci), 0)),
                pl.BlockSpec((nb, N_COLBLK, HALO, LANES),
                             lambda bi, ps, ci: (bi, 0, jnp.maximum(fwd_chunk(ps, ci) * hb - 1, 0), 0)),
                pl.BlockSpec((nb, N_COLBLK, HALO, LANES),
                             lambda bi, ps, ci: (bi, 0, jnp.minimum((fwd_chunk(ps, ci) + 1) * hb, l // HALO - 1), 0)),
                pl.BlockSpec((nb, CHUNK, LANES), lambda bi, ps, ci: (bi, any_chunk(ps, ci), 0)),
                pl.BlockSpec((nb, CHUNK, D_SSM), lambda bi, ps, ci: (bi, bwd_chunk(ps, ci), 0)),
                state_spec, state_spec,
                pl.BlockSpec((CONV_W, N_COLBLK, LANES), lambda bi, ps, ci: (0, 0, 0)),
                pl.BlockSpec((N_COLBLK, LANES), const2),
                pl.BlockSpec((2 * SSM_HEADS, CHUNK), const2),
                pl.BlockSpec((2 * SSM_HEADS, CHUNK), const2),
                pl.BlockSpec((1, D_SSM), const2),
                pl.BlockSpec((1, D_SSM), const2),
                pl.BlockSpec((3 * LANES, 3 * D_SSM), const2),
                pl.BlockSpec((2, 3 * CHUNK, CHUNK), lambda bi, ps, ci: (0, 0, 0))]
    return pl.pallas_call(
        functools.partial(_ssd_kernel, nc),
        grid=(b // nb, 2, nc),
        in_specs=in_specs,
        out_specs=[pl.BlockSpec((nb, CHUNK, D_SSM), lambda bi, ps, ci: (bi, bwd_chunk(ps, ci), 0)),
                   state_spec, state_spec],
        out_shape=[jax.ShapeDtypeStruct((b, l, D_SSM), F32),
                   jax.ShapeDtypeStruct((b, n_state, D_STATE), F32),
                   jax.ShapeDtypeStruct((b, n_state, D_STATE), F32)],
        scratch_shapes=[pltpu.VMEM((nb, n_state, D_STATE), F32),
                        pltpu.VMEM((nb, N_COLBLK, CHUNK + 2 * HALO, LANES), F32),
                        pltpu.VMEM((nb, l, D_SSM), F32),
                        pltpu.VMEM((nb, l, 2 * SSM_GROUPS * D_STATE), BF16),
                        pltpu.VMEM((nb, l, D_SSM), F32)],
        compiler_params=_cparams(("parallel", "arbitrary", "arbitrary")),
        name="ssd",
    )(xbc, xbc, xbc, dt, z, h0f, h0b, conv_w, conv_b, alog, dtb, d_x, norm_g, _expand_matrix(), _scan_matrices())


def _outproj_kernel(att_ref, ssm_ref, x_ref, mods_ref, wa_ref, ws_ref, g_ref, wr_ref, br_ref,
                    x1_ref, h2x_ref):
    mix = (jnp.dot(att_ref[...].astype(BF16), wa_ref[...], preferred_element_type=F32)
           + jnp.dot(ssm_ref[...].astype(BF16), ws_ref[...], preferred_element_type=F32))
    gate1 = mods_ref[:, 2 * D_MODEL:3 * D_MODEL]
    shift2 = mods_ref[:, 3 * D_MODEL:4 * D_MODEL]
    scale2 = mods_ref[:, 4 * D_MODEL:5 * D_MODEL]
    x1 = x_ref[...] + gate1 * mix
    x1_ref[...] = x1
    y = x1 * lax.rsqrt(jnp.mean(x1 * x1, axis=-1, keepdims=True) + EPS) * g_ref[...]
    h2 = y * (1.0 + scale2) + shift2
    h_hi = h2.astype(BF16)
    h2x_ref[:, 0:D_MODEL] = h2
    h_lo = (h2 - h_hi.astype(F32)).astype(BF16)
    w_hi = wr_ref[0]
    w_lo = wr_ref[1]
    logits = (jnp.dot(h_hi, w_hi, preferred_element_type=F32) + jnp.dot(h_lo, w_hi, preferred_element_type=F32)
              + jnp.dot(h_hi, w_lo, preferred_element_type=F32)) + br_ref[...]
    lane = lax.broadcasted_iota(jnp.int32, logits.shape, 1).astype(F32)
    neg = -jnp.inf
    big = float(1 << 20)
    is_g = lane < N_GROUPS
    gl = jnp.where(is_g, logits, neg)
    gmax = jnp.max(gl, axis=-1, keepdims=True)
    g_idx = jnp.min(jnp.where(gl == gmax, lane, big), axis=-1, keepdims=True)
    p_g = 1.0 / jnp.sum(jnp.where(is_g, jnp.exp(gl - gmax), 0.0), axis=-1, keepdims=True)
    e_lo = N_GROUPS + g_idx * EXPERTS_PER_GROUP
    in_grp = (lane >= e_lo) & (lane < e_lo + EXPERTS_PER_GROUP)
    el = jnp.where(in_grp, logits, neg)
    m1 = jnp.max(el, axis=-1, keepdims=True)
    i1 = jnp.min(jnp.where(el == m1, lane, big), axis=-1, keepdims=True)
    el2 = jnp.where(lane == i1, neg, el)
    m2 = jnp.max(el2, axis=-1, keepdims=True)
    i2 = jnp.min(jnp.where(el2 == m2, lane, big), axis=-1, keepdims=True)
    e2 = jnp.exp(m2 - m1)
    w1 = p_g / (1.0 + e2)
    w2 = p_g * e2 / (1.0 + e2)
    slab = (jnp.where(lane == i1 - e_lo, w1, 0.0) + jnp.where(lane == i2 - e_lo, w2, 0.0)
            + jnp.where(lane == EXPERTS_PER_GROUP, g_idx, 0.0))
    h2x_ref[:, D_MODEL:H2X_W] = slab


def _outproj(att, ssm, x, mods3, mod_row0, mod_tokens, wo_att, wo_ssm, g, w_router, b_router):
    n = x.shape[0]
    tm = 256
    per_mod = mod_tokens // tm
    return pl.pallas_call(
        _outproj_kernel,
        grid=(n // tm,),
        in_specs=[pl.BlockSpec((tm, D_ATT), lambda i: (i, 0)),
                  pl.BlockSpec((tm, D_SSM), lambda i: (i, 0)),
                  pl.BlockSpec((tm, D_MODEL), lambda i: (i, 0)),
                  pl.BlockSpec((None, 1, 6 * D_MODEL), lambda i: (mod_row0 + i // per_mod, 0, 0)),
                  pl.BlockSpec((D_ATT, D_MODEL), lambda i: (0, 0)),
                  pl.BlockSpec((D_SSM, D_MODEL), lambda i: (0, 0)),
                  pl.BlockSpec((1, D_MODEL), lambda i: (0, 0)),
                  pl.BlockSpec((2, D_MODEL, LANES), lambda i: (0, 0, 0)),
                  pl.BlockSpec((1, LANES), lambda i: (0, 0))],
        out_specs=[pl.BlockSpec((tm, D_MODEL), lambda i: (i, 0)),
                   pl.BlockSpec((tm, H2X_W), lambda i: (i, 0))],
        out_shape=[jax.ShapeDtypeStruct((n, D_MODEL), F32),
                   jax.ShapeDtypeStruct((n, H2X_W), F32)],
        compiler_params=_cparams(("parallel",)),
        name="outproj_router",
    )(att, ssm, x, mods3, wo_att, wo_ssm, g, w_router, b_router)


def _route_kernel(slab_ref, meta_ref):
    t_n = MOE_TILE
    blk = LANES
    slab = slab_ref[...]
    lane = lax.broadcasted_iota(jnp.int32, (t_n, LANES), 1)
    gcol = jnp.sum(jnp.where(lane == EXPERTS_PER_GROUP, slab, 0.0), axis=-1, keepdims=True)
    member = (lane.astype(F32) == gcol) & (lane < N_GROUPS)
    a = jnp.where(member, 1.0, 0.0).astype(BF16)
    r_i = lax.broadcasted_iota(jnp.int32, (blk, blk), 0)
    c_i = lax.broadcasted_iota(jnp.int32, (blk, blk), 1)
    lower = jnp.where(c_i < r_i, 1.0, 0.0).astype(BF16)
    upper = jnp.where(r_i < c_i, 1.0, 0.0).astype(BF16)
    offs = jnp.zeros((1, LANES), F32)
    ranks = []
    for b in range(t_n // blk):
        ab = a[b * blk:(b + 1) * blk]
        rb = jnp.dot(lower, ab, preferred_element_type=F32)
        ranks.append(rb + offs)
        offs = offs + rb[blk - 1:blk] + ab[blk - 1:blk].astype(F32)
    rank = jnp.concatenate(ranks, axis=0)
    n_chunk = jnp.floor((offs + (MOE_CHUNK - 1)) * (1.0 / MOE_CHUNK))
    start = jnp.dot(jnp.broadcast_to(n_chunk, (8, LANES)).astype(BF16), upper,
                    preferred_element_type=F32)[0:1]
    end = start + n_chunk
    dest = jnp.sum(jnp.where(member, start * MOE_CHUNK + rank, 0.0), axis=-1, keepdims=True)
    tok = lax.broadcasted_iota(jnp.int32, (t_n, LANES), 0)
    digits = jnp.where(lane == 0, (tok // blk).astype(F32),
                       jnp.where(lane == 1, (tok % blk).astype(F32), jnp.where(lane == 2, 1.0, 0.0))).astype(BF16)
    sw = 512
    pieces = []
    for sc in range(MOE_ROWS // sw):
        s_id = (lax.broadcasted_iota(jnp.int32, (t_n, sw), 1) + sc * sw).astype(F32)
        hit = jnp.where(dest == s_id, 1.0, 0.0).astype(BF16)
        r = lax.dot_general(digits, hit, (((0,), (0,)), ((), ())), preferred_element_type=F32)
        tok_of = r[0:1] * blk + r[1:2]
        pieces.append(jnp.where(r[2:3] > 0.5, tok_of, float(t_n)))
    pieces.append(jnp.full((1, META_ROWS - MOE_ROWS), float(t_n), F32))
    perm = jnp.concatenate(pieces, axis=1)
    slot = lax.broadcasted_iota(jnp.int32, (1, META_ROWS), 1).astype(F32)
    lane1 = lax.broadcasted_iota(jnp.int32, (1, LANES), 1)
    cg = jnp.zeros((1, META_ROWS), F32)
    for g in range(N_GROUPS):
        end_g = jnp.sum(jnp.where(lane1 == g, end, 0.0), axis=-1, keepdims=True)
        cg = cg + jnp.where(slot >= end_g, 1.0, 0.0)
    n_act = jnp.broadcast_to(end_g, (1, META_ROWS))
    meta_ref[...] = jnp.concatenate([perm, cg, n_act, jnp.zeros((5, META_ROWS), F32)], axis=0).astype(jnp.int32)


def _route(h2x):
    n = h2x.shape[0]
    n_tiles = n // MOE_TILE
    return pl.pallas_call(
        _route_kernel,
        grid=(n_tiles,),
        in_specs=[pl.BlockSpec((MOE_TILE, LANES), lambda i: (i, D_MODEL // LANES))],
        out_specs=pl.BlockSpec((None, 8, META_ROWS), lambda i: (i, 0, 0)),
        out_shape=jax.ShapeDtypeStruct((n_tiles, 8, META_ROWS), jnp.int32),
        compiler_params=_cparams(("parallel",)),
        name="moe_route",
    )(h2x)


def _moe_kernel(perm_ref, cg_ref, nact_ref, h_ref, wg_ref, wu_ref, wd_ref, y_ref, hs0, hs1, ys0, ys1):
    i = pl.program_id(0)
    s = pl.program_id(1)
    n_act = nact_ref[i]
    tile_base = i * META_ROWS
    hs = (hs0, hs1)
    ys = (ys0, ys1)

    def gather(chunk, dst):
        base = tile_base + chunk * MOE_CHUNK
        for r in range(MOE_CHUNK):
            src = jnp.minimum(perm_ref[base + r], MOE_TILE - 1)
            dst[r:r + 1, :] = h_ref[pl.ds(src, 1), :]

    def scatter(chunk, src):
        base = tile_base + chunk * MOE_CHUNK
        for r in range(MOE_CHUNK):
            y_ref[pl.ds(perm_ref[base + r], 1), :] = src[r:r + 1, :]

    def ffn(src, dst):
        hb = src[:, 0:D_MODEL].astype(BF16)
        cw = src[:, D_MODEL:H2X_W]
        acc = None
        for e in range(EXPERTS_PER_GROUP):
            a = jnp.dot(hb, wg_ref[e], preferred_element_type=F32)
            u = jnp.dot(hb, wu_ref[e], preferred_element_type=F32)
            hid = (_silu(a) * u * cw[:, e:e + 1]).astype(BF16)
            t = jnp.dot(hid, wd_ref[e], preferred_element_type=F32)
            acc = t if acc is None else acc + t
        dst[...] = acc

    @pl.when(s == 0)
    def _():
        y_ref[MOE_TILE:MOE_TILE + 8, :] = jnp.zeros((8, D_MODEL), F32)
        ys1[...] = jnp.zeros_like(ys1)
        gather(0, hs0)

    for par in (0, 1):
        @pl.when((s < n_act) & (s % 2 == par))
        def _():
            gather(s + 1, hs[1 - par])
            ffn(hs[par], ys[par])
            scatter(jnp.maximum(s - 1, 0), ys[1 - par])

        @pl.when((s == n_act) & (s % 2 == par))
        def _():
            scatter(s - 1, ys[1 - par])


def _moe(h2x, perm, cgrp, nact, wg, wu, wd):
    n = h2x.shape[0]
    n_tiles = n // MOE_TILE

    def w_idx(i, s, perm_ref, cg_ref, nact_ref):
        return (jnp.minimum(cg_ref[i * MOE_STEPS + s], N_GROUPS - 1), 0, 0)

    return pl.pallas_call(
        _moe_kernel,
        grid_spec=pltpu.PrefetchScalarGridSpec(
            num_scalar_prefetch=3,
            grid=(n_tiles, MOE_STEPS),
            in_specs=[pl.BlockSpec((MOE_TILE, H2X_W), lambda i, s, p, c, a: (i, 0)),
                      pl.BlockSpec((EXPERTS_PER_GROUP, D_MODEL, EXPERT_FF), w_idx),
                      pl.BlockSpec((EXPERTS_PER_GROUP, D_MODEL, EXPERT_FF), w_idx),
                      pl.BlockSpec((EXPERTS_PER_GROUP, EXPERT_FF, D_MODEL), w_idx)],
            out_specs=pl.BlockSpec((None, MOE_TILE + 8, D_MODEL), lambda i, s, p, c, a: (i, 0, 0)),
            scratch_shapes=[pltpu.VMEM((MOE_CHUNK, H2X_W), F32), pltpu.VMEM((MOE_CHUNK, H2X_W), F32),
                            pltpu.VMEM((MOE_CHUNK, D_MODEL), F32), pltpu.VMEM((MOE_CHUNK, D_MODEL), F32)]),
        out_shape=jax.ShapeDtypeStruct((n_tiles, MOE_TILE + 8, D_MODEL), F32),
        compiler_params=_cparams(("parallel", "arbitrary")),
        name="moe_experts",
    )(perm, cgrp, nact, h2x, wg, wu, wd)


def _final_kernel(y_ref, x1_ref, mods_ref, fg_ref, o_ref):
    gate2 = mods_ref[:, 5 * D_MODEL:6 * D_MODEL]
    x2 = x1_ref[...] + gate2 * y_ref[...]
    o_ref[...] = x2 * lax.rsqrt(jnp.mean(x2 * x2, axis=-1, keepdims=True) + EPS) * fg_ref[...]


def _final(y, x1, mods3, mod_row0, mod_tokens, fg):
    n = x1.shape[0]
    tm = 512
    per_mod = mod_tokens // tm
    per_tile = MOE_TILE // tm
    return pl.pallas_call(
        _final_kernel,
        grid=(n // tm,),
        in_specs=[pl.BlockSpec((None, tm, D_MODEL), lambda j: (j // per_tile, j % per_tile, 0)),
                  pl.BlockSpec((tm, D_MODEL), lambda j: (j, 0)),
                  pl.BlockSpec((None, 1, 6 * D_MODEL), lambda j: (mod_row0 + j // per_mod, 0, 0)),
                  pl.BlockSpec((1, D_MODEL), lambda j: (0, 0))],
        out_specs=pl.BlockSpec((tm, D_MODEL), lambda j: (j, 0)),
        out_shape=jax.ShapeDtypeStruct((n, D_MODEL), F32),
        compiler_params=_cparams(("parallel",)),
        name="final_norm",
    )(y, x1, mods3, fg)


def _rope_tables(t):
    n_freq = QK_DIM // 4
    n_rows = t // GRID_W
    freqs = ROPE_BASE ** (-jnp.arange(n_freq, dtype=F32) / n_freq)
    ang_r = jnp.arange(n_rows, dtype=F32)[:, None] * freqs
    ang_c = jnp.arange(GRID_W, dtype=F32)[:, None] * freqs
    cr, sr, cc, sc = lax.optimization_barrier((jnp.cos(ang_r), jnp.sin(ang_r), jnp.cos(ang_c), jnp.sin(ang_c)))
    j = np.arange(LANES) % QK_DIM
    f_idx = j % n_freq
    by_row = (j < QK_DIM // 2)[None, None, :]
    first = ((j % (QK_DIM // 2)) < n_freq)[None, None, :]

    def table(r_small, c_small):
        return jnp.where(by_row, r_small[:, f_idx][:, None, :], c_small[:, f_idx][None, :, :])

    cos = table(cr, cc)
    sin = table(sr, sc)
    return (cos.reshape(t, LANES), jnp.where(first, -sin, 0.0).reshape(t, LANES),
            jnp.where(first, 0.0, sin).reshape(t, LANES))


def _layer(x, mods3, mod_row0, mod_tokens, rope_tabs, ctx_k, ctx_v, h0f, h0b, lw, layer):
    b, t, _ = x.shape
    n = b * t
    xf = x.reshape(n, D_MODEL)
    res = _inproj(xf, mods3, mod_row0, mod_tokens, t, lw["norm_mix_g"], lw["w_main"], lw["w_dt"], rope_tabs)
    q, kb, vt, z, xbc, dt = res[:6]
    if ctx_k is None:
        cache = None
        k3, v3 = res[6].reshape(b, t, D_QK), res[7].reshape(b, t, D_ATT)
    else:
        cache = (ctx_k.astype(BF16), jnp.swapaxes(ctx_v, 1, 2).astype(BF16))
        k3 = v3 = None
    lam0 = 0.8 - 0.6 * math.exp(-0.3 * layer)
    tq = 512 if t % 512 == 0 else 256
    tk = 512 if t % 512 == 0 else 256
    att = _attention(q.reshape(b, t, D_QK), kb.reshape(b, t, D_QK), vt, cache, lw["lamp"], lw["attn_subln_g"],
                     lam0, tq, tk)
    ssm, hf, hb = _ssd(xbc, dt.reshape(b, t, LANES), z.reshape(b, t, D_SSM), h0f, h0b, lw["conv_w"], lw["conv_b"],
                       lw["alog"], lw["dtb"], lw["d_x"], lw["ssm_norm_g"])
    x1, h2x = _outproj(att.reshape(n, D_ATT), ssm.reshape(n, D_SSM), xf, mods3, mod_row0, mod_tokens,
                       lw["wo_att"], lw["wo_ssm"], lw["norm_ffn_g"], lw["w_router"], lw["b_router"])
    meta = _route(h2x)
    perm = meta[:, 0, :].reshape(-1)
    cgrp = meta[:, 1, :MOE_STEPS].reshape(-1)
    nact = meta[:, 2, 0]
    y = _moe(h2x, perm, cgrp, nact, lw["wg"], lw["wu"], lw["wd"])
    out = _final(y, x1, mods3, mod_row0, mod_tokens, lw["final_g"])
    return out.reshape(b, t, D_MODEL), k3, v3, hf, hb


def _pad_lanes(v, width=LANES):
    return jnp.pad(v, [(0, 0)] * (v.ndim - 1) + [(0, width - v.shape[-1])])


def kernel(x_prompt, x_sample, cache_k, cache_v, state_ssm_fwd, state_ssm_bwd, c, c_ctx, w_ada, b_ada, norm_mix_g, w_in, w_out, lambda_q1, lambda_k1, lambda_q2, lambda_k2, attn_subln_g, conv_w, conv_b, a_log_fwd, a_log_bwd, dt_bias_fwd, dt_bias_bwd, ssm_d, ssm_norm_g, norm_ffn_g, w_group_router, b_group_router, w_expert_router, b_expert_router, w_exp_gate, w_exp_up, w_exp_down, final_norm_g):
    depth = w_in.shape[0]
    assert depth == 1, "single trunk layer"
    bp, tp, _ = x_prompt.shape
    bs, ts, _ = x_sample.shape
    l = 0
    cond = jnp.concatenate([c_ctx[None], c], axis=0)
    condT = _pad_lanes(cond.T, 8)
    mods = _ada(condT, w_ada[l], b_ada[l][None])
    mods3 = mods.reshape(8, 1, 6 * D_MODEL)

    w_router = _pad_lanes(jnp.concatenate([w_group_router[l], w_expert_router[l]], axis=1))
    wr_hi = w_router.astype(BF16)
    wr_lo = (w_router - wr_hi.astype(F32)).astype(BF16)
    lw = dict(
        norm_mix_g=norm_mix_g[l][None],
        w_main=w_in[l].astype(BF16),
        w_dt=_pad_lanes(w_in[l][:, MAIN_COLS:]).astype(BF16),
        lamp=jnp.stack([lambda_q1[l], lambda_k1[l], lambda_q2[l], lambda_k2[l]]),
        attn_subln_g=attn_subln_g[l][None],
        conv_w=conv_w[l].reshape(CONV_W, N_COLBLK, LANES), conv_b=conv_b[l].reshape(N_COLBLK, LANES),
        alog=jnp.broadcast_to(jnp.concatenate([a_log_fwd[l], a_log_bwd[l]])[:, None], (2 * SSM_HEADS, CHUNK)),
        dtb=jnp.broadcast_to(jnp.concatenate([dt_bias_fwd[l], dt_bias_bwd[l]])[:, None], (2 * SSM_HEADS, CHUNK)),
        d_x=jnp.repeat(ssm_d[l], SSM_HEADDIM)[None], ssm_norm_g=ssm_norm_g[l][None],
        wo_att=w_out[l][:D_ATT].astype(BF16), wo_ssm=w_out[l][D_ATT:].astype(BF16),
        norm_ffn_g=norm_ffn_g[l][None],
        w_router=jnp.stack([wr_hi, wr_lo]),
        b_router=_pad_lanes(jnp.concatenate([b_group_router[l], b_expert_router[l]])[None]),
        wg=w_exp_gate[l].astype(BF16), wu=w_exp_up[l].astype(BF16), wd=w_exp_down[l].astype(BF16),
        final_g=final_norm_g[None],
    )
    n_state = SSM_HEADS * SSM_HEADDIM
    zeros_state = jnp.zeros((bp, n_state, D_STATE), F32)
    yp, ck, cv, hf, hb = _layer(x_prompt, mods3, 0, bp * tp, None, None, None, zeros_state, zeros_state, lw, l)
    ys, _, _, _, _ = _layer(x_sample, mods3, 1, ts, _rope_tables(ts),
                            cache_k[:, l].reshape(bs, -1, D_QK), cache_v[:, l].reshape(bs, -1, D_ATT),
                            state_ssm_fwd[:, l].reshape(bs, n_state, D_STATE),
                            state_ssm_bwd[:, l].reshape(bs, n_state, D_STATE), lw, l)
    new_k = ck.reshape(bp, 1, tp, ATT_HEADS, 2, QK_DIM)
    new_v = cv.reshape(bp, 1, tp, ATT_HEADS, V_DIM)
    new_hf = hf.reshape(bp, 1, SSM_HEADS, SSM_HEADDIM, D_STATE)
    new_hb = hb.reshape(bp, 1, SSM_HEADS, SSM_HEADDIM, D_STATE)
    return yp, ys, new_k, new_v, new_hf, new_hb
```

```python
import functools
import math

import numpy as np
import jax
import jax.numpy as jnp
from jax import lax
from jax.experimental import pallas as pl
from jax.experimental.pallas import tpu as pltpu

D_MODEL = 1024
GRID_W = 64
ATT_HEADS = 4
QK_DIM = 64
V_DIM = 128
D_QK = 512
D_ATT = 512
ROPE_BASE = 10000.0
D_SSM = 512
SSM_HEADDIM = 64
SSM_HEADS = 8
SSM_GROUPS = 2
D_STATE = 128
CONV_W = 5
CHUNK = 128
XBC_DIM = 1024
N_GROUPS = 4
EXPERTS_PER_GROUP = 4
N_EXPERTS = 16
EXPERT_FF = 256
EPS = 1e-6
MAIN_COLS = 2 * D_QK + D_ATT + D_SSM + XBC_DIM
H2X_W = D_MODEL + 128
MOE_TILE = 2048
MOE_CHUNK = 256
MOE_SLOTS = MOE_TILE // MOE_CHUNK + N_GROUPS
MOE_ROWS = MOE_SLOTS * MOE_CHUNK
MOE_STEPS = MOE_SLOTS + 1
META_ROWS = MOE_STEPS * MOE_CHUNK
LANES = 128
HALO = 8
VMEM_LIMIT = 56 * 1024 * 1024

LOG2E = math.log2(math.e)
SUM_ROWS = 16
F32 = jnp.float32
BF16 = jnp.bfloat16


def _cparams(sem):
    return pltpu.CompilerParams(dimension_semantics=sem, vmem_limit_bytes=VMEM_LIMIT)


def _sigmoid(x):
    return 1.0 / (1.0 + jnp.exp(-x))


def _silu(x):
    return x * _sigmoid(x)


def _ada_kernel(condT_ref, w_ref, b_ref, o_ref):
    s = _silu(condT_ref[...])
    w = w_ref[...]
    b = b_ref[...]
    o_ref[...] = jnp.zeros_like(o_ref)
    for r in range(3):
        o_ref[r:r + 1, :] = jnp.sum(w * s[:, r:r + 1], axis=0, keepdims=True) + b


def _ada(condT, w_ada, b_ada):
    bn = 1024
    n = w_ada.shape[1]
    return pl.pallas_call(
        _ada_kernel,
        grid=(n // bn,),
        in_specs=[pl.BlockSpec((D_MODEL, 8), lambda j: (0, 0)),
                  pl.BlockSpec((D_MODEL, bn), lambda j: (0, j)),
                  pl.BlockSpec((1, bn), lambda j: (0, j))],
        out_specs=pl.BlockSpec((8, bn), lambda j: (0, j)),
        out_shape=jax.ShapeDtypeStruct((8, n), F32),
        compiler_params=_cparams(("arbitrary",)),
        name="ada",
    )(condT, w_ada, b_ada)


def _inproj_kernel(rope, x_ref, mods_ref, g_ref, w_ref, wdt_ref, *rest):
    if rope:
        cos_ref, sa_ref, sb_ref, q_ref, kb_ref, vt_ref, z_ref, xbc_ref, dt_ref = rest
    else:
        q_ref, kb_ref, vt_ref, z_ref, xbc_ref, dt_ref, kf_ref, vf_ref = rest
    x = x_ref[...]
    shift = mods_ref[:, 0:D_MODEL]
    scale = mods_ref[:, D_MODEL:2 * D_MODEL]
    y = x * lax.rsqrt(jnp.mean(x * x, axis=-1, keepdims=True) + EPS) * g_ref[...]
    h = (y * (1.0 + scale) + shift).astype(BF16)
    r = jnp.dot(h, w_ref[...], preferred_element_type=F32)
    dt_ref[...] = jnp.dot(h, wdt_ref[...], preferred_element_type=F32)
    q = r[:, 0:D_QK]
    k = r[:, D_QK:2 * D_QK]
    if rope:
        cos = cos_ref[...]
        sa = sa_ref[...]
        sb = sb_ref[...]

        def rot(t):
            parts = []
            for hh in range(ATT_HEADS):
                th = t[:, hh * LANES:(hh + 1) * LANES]
                parts.append(th * cos + pltpu.roll(th, LANES - 16, 1) * sa + pltpu.roll(th, 16, 1) * sb)
            return jnp.concatenate(parts, axis=1)

        q = rot(q)
        k = rot(k)
    v = r[:, 2 * D_QK:2 * D_QK + D_ATT]
    q_ref[...] = q
    kb_ref[...] = k.astype(BF16)
    vt_ref[...] = v.T.astype(BF16)
    if not rope:
        kf_ref[...] = k
        vf_ref[...] = v
    z_ref[...] = r[:, 2 * D_QK + D_ATT:2 * D_QK + D_ATT + D_SSM]
    x0 = 2 * D_QK + D_ATT + D_SSM
    for cb in range(XBC_DIM // LANES):
        xbc_ref[cb] = r[:, x0 + cb * LANES:x0 + (cb + 1) * LANES]


def _inproj(x, mods3, mod_row0, mod_tokens, seq_len, g, w_main, w_dt, rope_tabs):
    n = x.shape[0]
    tm = 256
    per_seq = seq_len // tm
    per_mod = mod_tokens // tm
    rope = rope_tabs is not None
    in_specs = [pl.BlockSpec((tm, D_MODEL), lambda i: (i, 0)),
                pl.BlockSpec((None, 1, 6 * D_MODEL), lambda i: (mod_row0 + i // per_mod, 0, 0)),
                pl.BlockSpec((1, D_MODEL), lambda i: (0, 0)),
                pl.BlockSpec((D_MODEL, MAIN_COLS), lambda i: (0, 0)),
                pl.BlockSpec((D_MODEL, LANES), lambda i: (0, 0))]
    args = [x, mods3, g, w_main, w_dt]
    if rope:
        tab_spec = pl.BlockSpec((tm, LANES), lambda i: (i % per_seq, 0))
        in_specs += [tab_spec] * 3
        args += list(rope_tabs)
    def rows(wd, dtype=F32):
        return pl.BlockSpec((tm, wd), lambda i: (i, 0)), jax.ShapeDtypeStruct((n, wd), dtype)

    outs = [rows(D_QK), rows(D_QK, BF16),
            (pl.BlockSpec((None, D_ATT, tm), lambda i: (i // per_seq, 0, i % per_seq)),
             jax.ShapeDtypeStruct((n // seq_len, D_ATT, seq_len), BF16)),
            rows(D_SSM),
            (pl.BlockSpec((None, XBC_DIM // LANES, tm, LANES), lambda i: (i // per_seq, 0, i % per_seq, 0)),
             jax.ShapeDtypeStruct((n // seq_len, XBC_DIM // LANES, seq_len, LANES), F32)),
            rows(LANES)]
    if not rope:
        outs += [rows(D_QK), rows(D_ATT)]
    return pl.pallas_call(
        functools.partial(_inproj_kernel, rope),
        grid=(n // tm,),
        in_specs=in_specs,
        out_specs=[o[0] for o in outs],
        out_shape=[o[1] for o in outs],
        compiler_params=_cparams(("parallel",)),
        name="inproj_rope" if rope else "inproj",
    )(*args)


def _attn_kernel(tk, lam0, has_cache, q_ref, k_ref, vt_ref, *rest):
    if has_cache:
        ck_ref, cvt_ref, lamp_ref, g_ref, o_ref = rest
    else:
        lamp_ref, g_ref, o_ref = rest
    tq = q_ref.shape[0]
    lp = lamp_ref[...]
    lam = (jnp.exp(jnp.sum(lp[0:1] * lp[1:2], axis=-1, keepdims=True))
           - jnp.exp(jnp.sum(lp[2:3] * lp[3:4], axis=-1, keepdims=True)) + lam0)
    ones_rows = jnp.ones((SUM_ROWS, tk), BF16)
    for hh in range(q_ref.shape[1] // LANES):
        hs = slice(hh * LANES, (hh + 1) * LANES)
        chunks = [(k_ref, vt_ref, c * tk) for c in range(k_ref.shape[0] // tk)]
        if has_cache:
            chunks += [(ck_ref, cvt_ref, c * tk) for c in range(ck_ref.shape[0] // tk)]
        q = q_ref[:, hs] * (QK_DIM ** -0.5 * LOG2E)
        lane = lax.broadcasted_iota(jnp.int32, q.shape, 1)
        qq_t = jnp.concatenate([jnp.where(lane < QK_DIM, q, 0.0), jnp.where(lane >= QK_DIM, q, 0.0)],
                               axis=0).T.astype(BF16)

        def scores(chunk):
            kr, _, start = chunk
            return jnp.dot(kr[start:start + tk, hs], qq_t, preferred_element_type=F32)

        def update(s, chunk, m, acc):
            _, vr, start = chunk
            m_new = jnp.maximum(m, jnp.max(s, axis=0, keepdims=True))
            alpha = jnp.exp2(m - m_new)
            p = jnp.exp2(s - m_new).astype(BF16)
            v_ext = jnp.concatenate([vr[hs, start:start + tk], ones_rows], axis=0)
            acc = alpha * acc + jnp.dot(v_ext, p, preferred_element_type=F32)
            return m_new, acc

        m = jnp.full((1, 2 * tq), -jnp.inf, F32)
        acc = jnp.zeros((V_DIM + SUM_ROWS, 2 * tq), F32)
        s = scores(chunks[0])
        for c, chunk in enumerate(chunks):
            s_next = scores(chunks[c + 1]) if c + 1 < len(chunks) else None
            m, acc = update(s, chunk, m, acc)
            s = s_next
        o = acc[0:V_DIM] / acc[V_DIM:V_DIM + 1]
        o = (o[:, 0:tq] - lam * o[:, tq:2 * tq]).T
        o = o * lax.rsqrt(jnp.mean(o * o, axis=-1, keepdims=True) + EPS)
        o_ref[:, hs] = o * g_ref[...] * (1.0 - lam0)


def _attention(q, k, vt, cache, lamp, g, lam0, tq, tk, nh):
    b, t, _ = q.shape
    wd = nh * LANES

    def kv_specs(length):
        return [pl.BlockSpec((None, length, wd), lambda bi, h, i: (bi, 0, h)),
                pl.BlockSpec((None, wd, length), lambda bi, h, i: (bi, h, 0))]

    in_specs = [pl.BlockSpec((None, tq, wd), lambda bi, h, i: (bi, i, h))] + kv_specs(t)
    args = [q, k, vt]
    if cache is not None:
        assert cache[0].shape[1] % tk == 0
        in_specs += kv_specs(cache[0].shape[1])
        args += list(cache)
    in_specs += [pl.BlockSpec((4, QK_DIM), lambda bi, h, i: (0, 0)),
                 pl.BlockSpec((1, V_DIM), lambda bi, h, i: (0, 0))]
    return pl.pallas_call(
        functools.partial(_attn_kernel, tk, lam0, cache is not None),
        grid=(b, ATT_HEADS // nh, t // tq),
        in_specs=in_specs,
        out_specs=pl.BlockSpec((None, tq, wd), lambda bi, h, i: (bi, i, h)),
        out_shape=jax.ShapeDtypeStruct((b, t, D_ATT), F32),
        compiler_params=_cparams(("parallel", "parallel", "arbitrary")),
        name="diff_attn",
    )(*args, lamp, g)


N_COLBLK = XBC_DIM // LANES
COL_CUM, COL_DT, COL_ECUM, COL_TOEND = 0, 8, 16, 24


def _expand_matrix():
    e = np.zeros((LANES, 3 * D_SSM), np.float32)
    for blk, lane0 in enumerate((COL_DT, COL_ECUM, COL_TOEND)):
        for h in range(SSM_HEADS):
            e[lane0 + h, blk * D_SSM + h * SSM_HEADDIM:blk * D_SSM + (h + 1) * SSM_HEADDIM] = 1.0
    return jnp.asarray(np.concatenate([e, e, e], axis=0), BF16)


def _expand_heads(cols, e_ref):
    return jnp.dot(_split3(cols), e_ref[...], preferred_element_type=F32)


def _split3(x):
    hi = x.astype(BF16)
    r1 = x - hi.astype(F32)
    mid = r1.astype(BF16)
    lo = (r1 - mid.astype(F32)).astype(BF16)
    return jnp.concatenate([hi, mid, lo], axis=-1)


def _scan_matrices():
    t = np.arange(CHUNK)
    pre = (t[:, None] <= t[None, :]).astype(np.float32)
    suf = (t[:, None] >= t[None, :]).astype(np.float32)
    return jnp.asarray(np.stack([np.concatenate([pre] * 3, axis=0), np.concatenate([suf] * 3, axis=0)]), BF16)


def _head_scalars(reverse, dt_ref, alog_ref, dtb_ref, scan_ref):
    r0 = SSM_HEADS if reverse else 0
    xv = dt_ref[...].T[r0:r0 + SSM_HEADS, :] + dtb_ref[r0:r0 + SSM_HEADS, :]
    dt = jnp.maximum(xv, 0.0) + jnp.log(1.0 + jnp.exp(-jnp.abs(xv)))
    la = dt * (-jnp.exp(alog_ref[r0:r0 + SSM_HEADS, :]))
    cum = jnp.dot(_split3(la), scan_ref[1 if reverse else 0], preferred_element_type=F32)
    end = 0 if reverse else CHUNK - 1
    cum_end = cum[:, end:end + 1]
    rows = jnp.concatenate([cum, dt, jnp.exp(cum), jnp.exp(cum_end - cum),
                            jnp.zeros((LANES - 4 * SSM_HEADS, CHUNK), F32)], axis=0)
    return cum, jnp.exp(cum_end), rows.T


def _ssd_chunk(reverse, xs, bm, cm, state, cum, dec, cols, e_ref):
    row = lax.broadcasted_iota(jnp.int32, (CHUNK, LANES), 0)
    lane = lax.broadcasted_iota(jnp.int32, (CHUNK, LANES), 1)
    causal = (row <= lane) if reverse else (row >= lane)
    spread = _expand_heads(cols, e_ref)
    xd = xs * spread[:, 0:D_SSM]
    xdw = (xd * spread[:, 2 * D_SSM:3 * D_SSM]).astype(BF16)
    xd = xd.astype(BF16)
    ecum_x = spread[:, D_SSM:2 * D_SSM]
    rep = SSM_HEADS // SSM_GROUPS
    y_parts = []
    new_state = []
    for g in range(SSM_GROUPS):
        bg = bm[:, g * D_STATE:(g + 1) * D_STATE]
        cg = cm[:, g * D_STATE:(g + 1) * D_STATE]
        cbt = lax.dot_general(cg, bg, (((1,), (1,)), ((), ())), preferred_element_type=F32)
        rows = slice(g * rep * SSM_HEADDIM, (g + 1) * rep * SSM_HEADDIM)
        st_g = state[rows, :]
        y_off = lax.dot_general(cg, st_g.astype(BF16), (((1,), (1,)), ((), ())),
                                preferred_element_type=F32)
        cst = lax.dot_general(xdw[:, rows], bg, (((0,), (0,)), ((), ())), preferred_element_type=F32)
        for pr in range(rep // 2):
            y_pair = None
            for sub in range(2):
                h = g * rep + 2 * pr + sub
                seg = cols[:, COL_CUM + h:COL_CUM + h + 1] - cum[h:h + 1, :]
                decay = jnp.exp(jnp.where(causal, seg, -jnp.inf))
                sc = (cbt * decay).astype(BF16)
                xd_pair = xd[:, (h // 2) * LANES:(h // 2 + 1) * LANES]
                keep = (lane < SSM_HEADDIM) if sub == 0 else (lane >= SSM_HEADDIM)
                t = jnp.dot(sc, jnp.where(keep, xd_pair, jnp.zeros_like(xd_pair)), preferred_element_type=F32)
                y_pair = t if y_pair is None else y_pair + t
            y_parts.append(y_pair)
        dec_rows = jnp.concatenate(
            [jnp.broadcast_to(dec[g * rep + hh:g * rep + hh + 1, :], (SSM_HEADDIM, D_STATE)) for hh in range(rep)],
            axis=0)
        new_state.append(st_g * dec_rows + cst)
        y_parts[-2] = y_parts[-2] + y_off[:, 0:LANES] * ecum_x[:, rows][:, 0:LANES]
        y_parts[-1] = y_parts[-1] + y_off[:, LANES:2 * LANES] * ecum_x[:, rows][:, LANES:2 * LANES]
    return jnp.concatenate(y_parts, axis=1), jnp.concatenate(new_state, axis=0)


def _ssd_kernel(nc, xc_ref, xp_ref, xn_ref, dt_ref, z_ref, h0f_ref, h0b_ref, cw_ref, cb_ref, alog_ref, dtb_ref,
                d_ref, ng_ref, e_ref, scan_ref, y_ref, hf_ref, hb_ref, state_ref, slab_ref, xs_ref, bc_ref, yf_ref):
    ps = pl.program_id(1)
    c = pl.program_id(2)
    n_bc = SSM_GROUPS * D_STATE

    @pl.when((ps == 0) & (c == 0))
    def _():
        state_ref[...] = h0f_ref[...]

    @pl.when(ps == 0)
    def _():
        t0 = pl.multiple_of(c * CHUNK, CHUNK)
        for bb in range(state_ref.shape[0]):
            slab_ref[bb, :, 0:HALO, :] = jnp.where(c > 0, xp_ref[bb], 0.0)
            slab_ref[bb, :, HALO:HALO + CHUNK, :] = xc_ref[bb]
            slab_ref[bb, :, HALO + CHUNK:2 * HALO + CHUNK, :] = jnp.where(c < nc - 1, xn_ref[bb], 0.0)
            blocks = []
            for cb in range(N_COLBLK):
                conv = cb_ref[cb:cb + 1, :]
                for kk in range(CONV_W):
                    off = HALO - CONV_W // 2 + kk
                    conv = conv + slab_ref[bb, cb, off:off + CHUNK, :] * cw_ref[kk, cb:cb + 1, :]
                blocks.append(_silu(conv))
            xs = jnp.concatenate(blocks[0:D_SSM // LANES], axis=1)
            bc = jnp.concatenate(blocks[D_SSM // LANES:], axis=1).astype(BF16)
            cum, dec, cols = _head_scalars(False, dt_ref.at[bb], alog_ref, dtb_ref, scan_ref)
            y, state = _ssd_chunk(False, xs, bc[:, 0:n_bc], bc[:, n_bc:], state_ref[bb], cum, dec, cols, e_ref)
            xs_ref[bb, pl.ds(t0, CHUNK), :] = xs
            bc_ref[bb, pl.ds(t0, CHUNK), :] = bc
            yf_ref[bb, pl.ds(t0, CHUNK), :] = y
            state_ref[bb] = state

    @pl.when((ps == 0) & (c == nc - 1))
    def _():
        hf_ref[...] = state_ref[...]
        state_ref[...] = h0b_ref[...]

    @pl.when(ps == 1)
    def _():
        t0 = pl.multiple_of((nc - 1 - c) * CHUNK, CHUNK)
        for bb in range(state_ref.shape[0]):
            xs = xs_ref[bb, pl.ds(t0, CHUNK), :]
            bc = bc_ref[bb, pl.ds(t0, CHUNK), :]
            cum, dec, cols = _head_scalars(True, dt_ref.at[bb], alog_ref, dtb_ref, scan_ref)
            y, state = _ssd_chunk(True, xs, bc[:, 0:n_bc], bc[:, n_bc:], state_ref[bb], cum, dec, cols, e_ref)
            state_ref[bb] = state
            y = y + yf_ref[bb, pl.ds(t0, CHUNK), :] + xs * d_ref[...]
            y = y * _silu(z_ref[bb])
            y = y * lax.rsqrt(jnp.mean(y * y, axis=-1, keepdims=True) + EPS)
            y_ref[bb] = y * ng_ref[...]

    @pl.when((ps == 1) & (c == nc - 1))
    def _():
        hb_ref[...] = state_ref[...]


def _ssd(xbc, dt, z, h0f, h0b, conv_w, conv_b, alog, dtb, d_x, norm_g):
    b, _, l, _ = xbc.shape
    nb = 2
    assert b % nb == 0
    nc = l // CHUNK
    hb = CHUNK // HALO
    n_state = SSM_HEADS * SSM_HEADDIM
    last = nc - 1

    def fwd_chunk(ps, ci):
        return jnp.where(ps == 0, ci, last)

    def any_chunk(ps, ci):
        return jnp.where(ps == 0, ci, last - ci)

    def bwd_chunk(ps, ci):
        return jnp.where(ps == 0, last, last - ci)

    const2 = lambda bi, ps, ci: (0, 0)
    state_spec = pl.BlockSpec((nb, n_state, D_STATE), lambda bi, ps, ci: (bi, 0, 0))
    in_specs = [pl.BlockSpec((nb, N_COLBLK, CHUNK, LANES), lambda bi, ps, ci: (bi, 0, fwd_chunk(ps, ci), 0)),
                pl.BlockSpec((nb, N_COLBLK, HALO, LANES),
                             lambda bi, ps, ci: (bi, 0, jnp.maximum(fwd_chunk(ps, ci) * hb - 1, 0), 0)),
                pl.BlockSpec((nb, N_COLBLK, HALO, LANES),
                             lambda bi, ps, ci: (bi, 0, jnp.minimum((fwd_chunk(ps, ci) + 1) * hb, l // HALO - 1), 0)),
                pl.BlockSpec((nb, CHUNK, LANES), lambda bi, ps, ci: (bi, any_chunk(ps, ci), 0)),
                pl.BlockSpec((nb, CHUNK, D_SSM), lambda bi, ps, ci: (bi, bwd_chunk(ps, ci), 0)),
                state_spec, state_spec,
                pl.BlockSpec((CONV_W, N_COLBLK, LANES), lambda bi, ps, ci: (0, 0, 0)),
                pl.BlockSpec((N_COLBLK, LANES), const2),
                pl.BlockSpec((2 * SSM_HEADS, CHUNK), const2),
                pl.BlockSpec((2 * SSM_HEADS, CHUNK), const2),
                pl.BlockSpec((1, D_SSM), const2),
                pl.BlockSpec((1, D_SSM), const2),
                pl.BlockSpec((3 * LANES, 3 * D_SSM), const2),
                pl.BlockSpec((2, 3 * CHUNK, CHUNK), lambda bi, ps, ci: (0, 0, 0))]
    return pl.pallas_call(
        functools.partial(_ssd_kernel, nc),
        grid=(b // nb, 2, nc),
        in_specs=in_specs,
        out_specs=[pl.BlockSpec((nb, CHUNK, D_SSM), lambda bi, ps, ci: (bi, bwd_chunk(ps, ci), 0)),
                   state_spec, state_spec],
        out_shape=[jax.ShapeDtypeStruct((b, l, D_SSM), F32),
                   jax.ShapeDtypeStruct((b, n_state, D_STATE), F32),
                   jax.ShapeDtypeStruct((b, n_state, D_STATE), F32)],
        scratch_shapes=[pltpu.VMEM((nb, n_state, D_STATE), F32),
                        pltpu.VMEM((nb, N_COLBLK, CHUNK + 2 * HALO, LANES), F32),
                        pltpu.VMEM((nb, l, D_SSM), F32),
                        pltpu.VMEM((nb, l, 2 * SSM_GROUPS * D_STATE), BF16),
                        pltpu.VMEM((nb, l, D_SSM), F32)],
        compiler_params=_cparams(("parallel", "arbitrary", "arbitrary")),
        name="ssd",
    )(xbc, xbc, xbc, dt, z, h0f, h0b, conv_w, conv_b, alog, dtb, d_x, norm_g, _expand_matrix(), _scan_matrices())


def _outproj_kernel(att_ref, ssm_ref, x_ref, mods_ref, wa_ref, ws_ref, g_ref, wr_ref, br_ref,
                    x1_ref, h2x_ref):
    mix = (jnp.dot(att_ref[...].astype(BF16), wa_ref[...], preferred_element_type=F32)
           + jnp.dot(ssm_ref[...].astype(BF16), ws_ref[...], preferred_element_type=F32))
    gate1 = mods_ref[:, 2 * D_MODEL:3 * D_MODEL]
    shift2 = mods_ref[:, 3 * D_MODEL:4 * D_MODEL]
    scale2 = mods_ref[:, 4 * D_MODEL:5 * D_MODEL]
    x1 = x_ref[...] + gate1 * mix
    x1_ref[...] = x1
    y = x1 * lax.rsqrt(jnp.mean(x1 * x1, axis=-1, keepdims=True) + EPS) * g_ref[...]
    h2 = y * (1.0 + scale2) + shift2
    h_hi = h2.astype(BF16)
    h2x_ref[:, 0:D_MODEL] = h2
    h_lo = (h2 - h_hi.astype(F32)).astype(BF16)
    w_hi = wr_ref[0]
    w_lo = wr_ref[1]
    logits = (jnp.dot(h_hi, w_hi, preferred_element_type=F32) + jnp.dot(h_lo, w_hi, preferred_element_type=F32)
              + jnp.dot(h_hi, w_lo, preferred_element_type=F32)) + br_ref[...]
    lane = lax.broadcasted_iota(jnp.int32, logits.shape, 1).astype(F32)
    neg = -jnp.inf
    big = float(1 << 20)
    is_g = lane < N_GROUPS
    gl = jnp.where(is_g, logits, neg)
    gmax = jnp.max(gl, axis=-1, keepdims=True)
    g_idx = jnp.min(jnp.where(gl == gmax, lane, big), axis=-1, keepdims=True)
    p_g = 1.0 / jnp.sum(jnp.where(is_g, jnp.exp(gl - gmax), 0.0), axis=-1, keepdims=True)
    e_lo = N_GROUPS + g_idx * EXPERTS_PER_GROUP
    in_grp = (lane >= e_lo) & (lane < e_lo + EXPERTS_PER_GROUP)
    el = jnp.where(in_grp, logits, neg)
    m1 = jnp.max(el, axis=-1, keepdims=True)
    i1 = jnp.min(jnp.where(el == m1, lane, big), axis=-1, keepdims=True)
    el2 = jnp.where(lane == i1, neg, el)
    m2 = jnp.max(el2, axis=-1, keepdims=True)
    i2 = jnp.min(jnp.where(el2 == m2, lane, big), axis=-1, keepdims=True)
    e2 = jnp.exp(m2 - m1)
    w1 = p_g / (1.0 + e2)
    w2 = p_g * e2 / (1.0 + e2)
    slab = (jnp.where(lane == i1 - e_lo, w1, 0.0) + jnp.where(lane == i2 - e_lo, w2, 0.0)
            + jnp.where(lane == EXPERTS_PER_GROUP, g_idx, 0.0))
    h2x_ref[:, D_MODEL:H2X_W] = slab


def _outproj(att, ssm, x, mods3, mod_row0, mod_tokens, wo_att, wo_ssm, g, w_router, b_router):
    n = x.shape[0]
    tm = 256
    per_mod = mod_tokens // tm
    return pl.pallas_call(
        _outproj_kernel,
        grid=(n // tm,),
        in_specs=[pl.BlockSpec((tm, D_ATT), lambda i: (i, 0)),
                  pl.BlockSpec((tm, D_SSM), lambda i: (i, 0)),
                  pl.BlockSpec((tm, D_MODEL), lambda i: (i, 0)),
                  pl.BlockSpec((None, 1, 6 * D_MODEL), lambda i: (mod_row0 + i // per_mod, 0, 0)),
                  pl.BlockSpec((D_ATT, D_MODEL), lambda i: (0, 0)),
                  pl.BlockSpec((D_SSM, D_MODEL), lambda i: (0, 0)),
                  pl.BlockSpec((1, D_MODEL), lambda i: (0, 0)),
                  pl.BlockSpec((2, D_MODEL, LANES), lambda i: (0, 0, 0)),
                  pl.BlockSpec((1, LANES), lambda i: (0, 0))],
        out_specs=[pl.BlockSpec((tm, D_MODEL), lambda i: (i, 0)),
                   pl.BlockSpec((tm, H2X_W), lambda i: (i, 0))],
        out_shape=[jax.ShapeDtypeStruct((n, D_MODEL), F32),
                   jax.ShapeDtypeStruct((n, H2X_W), F32)],
        compiler_params=_cparams(("parallel",)),
        name="outproj_router",
    )(att, ssm, x, mods3, wo_att, wo_ssm, g, w_router, b_router)


def _route_kernel(slab_ref, meta_ref):
    t_n = MOE_TILE
    blk = LANES
    slab = slab_ref[...]
    lane = lax.broadcasted_iota(jnp.int32, (t_n, LANES), 1)
    gcol = jnp.sum(jnp.where(lane == EXPERTS_PER_GROUP, slab, 0.0), axis=-1, keepdims=True)
    member = (lane.astype(F32) == gcol) & (lane < N_GROUPS)
    a = jnp.where(member, 1.0, 0.0).astype(BF16)
    r_i = lax.broadcasted_iota(jnp.int32, (blk, blk), 0)
    c_i = lax.broadcasted_iota(jnp.int32, (blk, blk), 1)
    lower = jnp.where(c_i < r_i, 1.0, 0.0).astype(BF16)
    upper = jnp.where(r_i < c_i, 1.0, 0.0).astype(BF16)
    offs = jnp.zeros((1, LANES), F32)
    ranks = []
    for b in range(t_n // blk):
        ab = a[b * blk:(b + 1) * blk]
        rb = jnp.dot(lower, ab, preferred_element_type=F32)
        ranks.append(rb + offs)
        offs = offs + rb[blk - 1:blk] + ab[blk - 1:blk].astype(F32)
    rank = jnp.concatenate(ranks, axis=0)
    n_chunk = jnp.floor((offs + (MOE_CHUNK - 1)) * (1.0 / MOE_CHUNK))
    start = jnp.dot(jnp.broadcast_to(n_chunk, (8, LANES)).astype(BF16), upper,
                    preferred_element_type=F32)[0:1]
    end = start + n_chunk
    dest = jnp.sum(jnp.where(member, start * MOE_CHUNK + rank, 0.0), axis=-1, keepdims=True)
    tok = lax.broadcasted_iota(jnp.int32, (t_n, LANES), 0)
    digits = jnp.where(lane == 0, (tok // blk).astype(F32),
                       jnp.where(lane == 1, (tok % blk).astype(F32), jnp.where(lane == 2, 1.0, 0.0))).astype(BF16)
    sw = 512
    pieces = []
    for sc in range(MOE_ROWS // sw):
        s_id = (lax.broadcasted_iota(jnp.int32, (t_n, sw), 1) + sc * sw).astype(F32)
        hit = jnp.where(dest == s_id, 1.0, 0.0).astype(BF16)
        r = lax.dot_general(digits, hit, (((0,), (0,)), ((), ())), preferred_element_type=F32)
        tok_of = r[0:1] * blk + r[1:2]
        pieces.append(jnp.where(r[2:3] > 0.5, tok_of, float(t_n)))
    pieces.append(jnp.full((1, META_ROWS - MOE_ROWS), float(t_n), F32))
    perm = jnp.concatenate(pieces, axis=1)
    slot = lax.broadcasted_iota(jnp.int32, (1, META_ROWS), 1).astype(F32)
    lane1 = lax.broadcasted_iota(jnp.int32, (1, LANES), 1)
    cg = jnp.zeros((1, META_ROWS), F32)
    for g in range(N_GROUPS):
        end_g = jnp.sum(jnp.where(lane1 == g, end, 0.0), axis=-1, keepdims=True)
        cg = cg + jnp.where(slot >= end_g, 1.0, 0.0)
    n_act = jnp.broadcast_to(end_g, (1, META_ROWS))
    meta_ref[...] = jnp.concatenate([perm, cg, n_act, jnp.zeros((5, META_ROWS), F32)], axis=0).astype(jnp.int32)


def _route(h2x):
    n = h2x.shape[0]
    n_tiles = n // MOE_TILE
    return pl.pallas_call(
        _route_kernel,
        grid=(n_tiles,),
        in_specs=[pl.BlockSpec((MOE_TILE, LANES), lambda i: (i, D_MODEL // LANES))],
        out_specs=pl.BlockSpec((None, 8, META_ROWS), lambda i: (i, 0, 0)),
        out_shape=jax.ShapeDtypeStruct((n_tiles, 8, META_ROWS), jnp.int32),
        compiler_params=_cparams(("parallel",)),
        name="moe_route",
    )(h2x)


def _moe_kernel(perm_ref, cg_ref, nact_ref, h_ref, wg_ref, wu_ref, wd_ref, y_ref, hs0, hs1, ys0, ys1):
    i = pl.program_id(0)
    s = pl.program_id(1)
    n_act = nact_ref[i]
    tile_base = i * META_ROWS
    hs = (hs0, hs1)
    ys = (ys0, ys1)

    def gather(chunk, dst):
        base = tile_base + chunk * MOE_CHUNK
        for r in range(MOE_CHUNK):
            src = jnp.minimum(perm_ref[base + r], MOE_TILE - 1)
            dst[r:r + 1, :] = h_ref[pl.ds(src, 1), :]

    def scatter(chunk, src):
        base = tile_base + chunk * MOE_CHUNK
        for r in range(MOE_CHUNK):
            y_ref[pl.ds(perm_ref[base + r], 1), :] = src[r:r + 1, :]

    def ffn(src, dst):
        hb = src[:, 0:D_MODEL].astype(BF16)
        cw = src[:, D_MODEL:H2X_W]
        hid = []
        for e in range(EXPERTS_PER_GROUP):
            a = jnp.dot(hb, wg_ref[e], preferred_element_type=F32)
            u = jnp.dot(hb, wu_ref[e], preferred_element_type=F32)
            hid.append((_silu(a) * u * cw[:, e:e + 1]).astype(BF16))
        dst[...] = jnp.dot(jnp.concatenate(hid, axis=1), wd_ref[...], preferred_element_type=F32)

    @pl.when(s == 0)
    def _():
        y_ref[MOE_TILE:MOE_TILE + 8, :] = jnp.zeros((8, D_MODEL), F32)
        ys1[...] = jnp.zeros_like(ys1)
        gather(0, hs0)

    for par in (0, 1):
        @pl.when((s < n_act) & (s % 2 == par))
        def _():
            gather(s + 1, hs[1 - par])
            ffn(hs[par], ys[par])
            scatter(jnp.maximum(s - 1, 0), ys[1 - par])

        @pl.when((s == n_act) & (s % 2 == par))
        def _():
            scatter(s - 1, ys[1 - par])


def _moe(h2x, perm, cgrp, nact, wg, wu, wd):
    n = h2x.shape[0]
    n_tiles = n // MOE_TILE

    def w_idx(i, s, perm_ref, cg_ref, nact_ref):
        return (jnp.minimum(cg_ref[i * MOE_STEPS + s], N_GROUPS - 1), 0, 0)

    return pl.pallas_call(
        _moe_kernel,
        grid_spec=pltpu.PrefetchScalarGridSpec(
            num_scalar_prefetch=3,
            grid=(n_tiles, MOE_STEPS),
            in_specs=[pl.BlockSpec((MOE_TILE, H2X_W), lambda i, s, p, c, a: (i, 0)),
                      pl.BlockSpec((EXPERTS_PER_GROUP, D_MODEL, EXPERT_FF), w_idx),
                      pl.BlockSpec((EXPERTS_PER_GROUP, D_MODEL, EXPERT_FF), w_idx),
                      pl.BlockSpec((EXPERTS_PER_GROUP * EXPERT_FF, D_MODEL), lambda i, s, p, c, a: w_idx(i, s, p, c, a)[:2])],
            out_specs=pl.BlockSpec((None, MOE_TILE + 8, D_MODEL), lambda i, s, p, c, a: (i, 0, 0)),
            scratch_shapes=[pltpu.VMEM((MOE_CHUNK, H2X_W), F32), pltpu.VMEM((MOE_CHUNK, H2X_W), F32),
                            pltpu.VMEM((MOE_CHUNK, D_MODEL), F32), pltpu.VMEM((MOE_CHUNK, D_MODEL), F32)]),
        out_shape=jax.ShapeDtypeStruct((n_tiles, MOE_TILE + 8, D_MODEL), F32),
        compiler_params=_cparams(("parallel", "arbitrary")),
        name="moe_experts",
    )(perm, cgrp, nact, h2x, wg, wu, wd)


def _final_kernel(y_ref, x1_ref, mods_ref, fg_ref, o_ref):
    gate2 = mods_ref[:, 5 * D_MODEL:6 * D_MODEL]
    x2 = x1_ref[...] + gate2 * y_ref[...]
    o_ref[...] = x2 * lax.rsqrt(jnp.mean(x2 * x2, axis=-1, keepdims=True) + EPS) * fg_ref[...]


def _final(y, x1, mods3, mod_row0, mod_tokens, fg):
    n = x1.shape[0]
    tm = 512
    per_mod = mod_tokens // tm
    per_tile = MOE_TILE // tm
    return pl.pallas_call(
        _final_kernel,
        grid=(n // tm,),
        in_specs=[pl.BlockSpec((None, tm, D_MODEL), lambda j: (j // per_tile, j % per_tile, 0)),
                  pl.BlockSpec((tm, D_MODEL), lambda j: (j, 0)),
                  pl.BlockSpec((None, 1, 6 * D_MODEL), lambda j: (mod_row0 + j // per_mod, 0, 0)),
                  pl.BlockSpec((1, D_MODEL), lambda j: (0, 0))],
        out_specs=pl.BlockSpec((tm, D_MODEL), lambda j: (j, 0)),
        out_shape=jax.ShapeDtypeStruct((n, D_MODEL), F32),
        compiler_params=_cparams(("parallel",)),
        name="final_norm",
    )(y, x1, mods3, fg)


def _rope_tables(t):
    n_freq = QK_DIM // 4
    n_rows = t // GRID_W
    freqs = ROPE_BASE ** (-jnp.arange(n_freq, dtype=F32) / n_freq)
    ang_r = jnp.arange(n_rows, dtype=F32)[:, None] * freqs
    ang_c = jnp.arange(GRID_W, dtype=F32)[:, None] * freqs
    cr, sr, cc, sc = lax.optimization_barrier((jnp.cos(ang_r), jnp.sin(ang_r), jnp.cos(ang_c), jnp.sin(ang_c)))
    j = np.arange(LANES) % QK_DIM
    f_idx = j % n_freq
    by_row = (j < QK_DIM // 2)[None, None, :]
    first = ((j % (QK_DIM // 2)) < n_freq)[None, None, :]

    def table(r_small, c_small):
        return jnp.where(by_row, r_small[:, f_idx][:, None, :], c_small[:, f_idx][None, :, :])

    cos = table(cr, cc)
    sin = table(sr, sc)
    return (cos.reshape(t, LANES), jnp.where(first, -sin, 0.0).reshape(t, LANES),
            jnp.where(first, 0.0, sin).reshape(t, LANES))


def _layer(x, mods3, mod_row0, mod_tokens, rope_tabs, ctx_k, ctx_v, h0f, h0b, lw, layer):
    b, t, _ = x.shape
    n = b * t
    xf = x.reshape(n, D_MODEL)
    res = _inproj(xf, mods3, mod_row0, mod_tokens, t, lw["norm_mix_g"], lw["w_main"], lw["w_dt"], rope_tabs)
    q, kb, vt, z, xbc, dt = res[:6]
    if ctx_k is None:
        cache = None
        k3, v3 = res[6].reshape(b, t, D_QK), res[7].reshape(b, t, D_ATT)
    else:
        cache = (ctx_k.astype(BF16), jnp.swapaxes(ctx_v, 1, 2).astype(BF16))
        k3 = v3 = None
    lam0 = 0.8 - 0.6 * math.exp(-0.3 * layer)
    tq = 512 if t % 512 == 0 else 256
    tk = 512 if t % 512 == 0 else 256
    att = _attention(q.reshape(b, t, D_QK), kb.reshape(b, t, D_QK), vt, cache, lw["lamp"], lw["attn_subln_g"],
                     lam0, tq, tk, ATT_HEADS if t <= 512 else 1)
    ssm, hf, hb = _ssd(xbc, dt.reshape(b, t, LANES), z.reshape(b, t, D_SSM), h0f, h0b, lw["conv_w"], lw["conv_b"],
                       lw["alog"], lw["dtb"], lw["d_x"], lw["ssm_norm_g"])
    x1, h2x = _outproj(att.reshape(n, D_ATT), ssm.reshape(n, D_SSM), xf, mods3, mod_row0, mod_tokens,
                       lw["wo_att"], lw["wo_ssm"], lw["norm_ffn_g"], lw["w_router"], lw["b_router"])
    meta = _route(h2x)
    perm = meta[:, 0, :].reshape(-1)
    cgrp = meta[:, 1, :MOE_STEPS].reshape(-1)
    nact = meta[:, 2, 0]
    y = _moe(h2x, perm, cgrp, nact, lw["wg"], lw["wu"], lw["wd"])
    out = _final(y, x1, mods3, mod_row0, mod_tokens, lw["final_g"])
    return out.reshape(b, t, D_MODEL), k3, v3, hf, hb


def _pad_lanes(v, width=LANES):
    return jnp.pad(v, [(0, 0)] * (v.ndim - 1) + [(0, width - v.shape[-1])])


def kernel(x_prompt, x_sample, cache_k, cache_v, state_ssm_fwd, state_ssm_bwd, c, c_ctx, w_ada, b_ada, norm_mix_g, w_in, w_out, lambda_q1, lambda_k1, lambda_q2, lambda_k2, attn_subln_g, conv_w, conv_b, a_log_fwd, a_log_bwd, dt_bias_fwd, dt_bias_bwd, ssm_d, ssm_norm_g, norm_ffn_g, w_group_router, b_group_router, w_expert_router, b_expert_router, w_exp_gate, w_exp_up, w_exp_down, final_norm_g):
    depth = w_in.shape[0]
    assert depth == 1, "single trunk layer"
    bp, tp, _ = x_prompt.shape
    bs, ts, _ = x_sample.shape
    l = 0
    cond = jnp.concatenate([c_ctx[None], c], axis=0)
    condT = _pad_lanes(cond.T, 8)
    mods = _ada(condT, w_ada[l], b_ada[l][None])
    mods3 = mods.reshape(8, 1, 6 * D_MODEL)

    w_router = _pad_lanes(jnp.concatenate([w_group_router[l], w_expert_router[l]], axis=1))
    wr_hi = w_router.astype(BF16)
    wr_lo = (w_router - wr_hi.astype(F32)).astype(BF16)
    lw = dict(
        norm_mix_g=norm_mix_g[l][None],
        w_main=w_in[l].astype(BF16),
        w_dt=_pad_lanes(w_in[l][:, MAIN_COLS:]).astype(BF16),
        lamp=jnp.stack([lambda_q1[l], lambda_k1[l], lambda_q2[l], lambda_k2[l]]),
        attn_subln_g=attn_subln_g[l][None],
        conv_w=conv_w[l].reshape(CONV_W, N_COLBLK, LANES), conv_b=conv_b[l].reshape(N_COLBLK, LANES),
        alog=jnp.broadcast_to(jnp.concatenate([a_log_fwd[l], a_log_bwd[l]])[:, None], (2 * SSM_HEADS, CHUNK)),
        dtb=jnp.broadcast_to(jnp.concatenate([dt_bias_fwd[l], dt_bias_bwd[l]])[:, None], (2 * SSM_HEADS, CHUNK)),
        d_x=jnp.repeat(ssm_d[l], SSM_HEADDIM)[None], ssm_norm_g=ssm_norm_g[l][None],
        wo_att=w_out[l][:D_ATT].astype(BF16), wo_ssm=w_out[l][D_ATT:].astype(BF16),
        norm_ffn_g=norm_ffn_g[l][None],
        w_router=jnp.stack([wr_hi, wr_lo]),
        b_router=_pad_lanes(jnp.concatenate([b_group_router[l], b_expert_router[l]])[None]),
        wg=w_exp_gate[l].astype(BF16), wu=w_exp_up[l].astype(BF16), wd=w_exp_down[l].astype(BF16).reshape(N_EXPERTS * EXPERT_FF, D_MODEL),
        final_g=final_norm_g[None],
    )
    n_state = SSM_HEADS * SSM_HEADDIM
    zeros_state = jnp.zeros((bp, n_state, D_STATE), F32)
    yp, ck, cv, hf, hb = _layer(x_prompt, mods3, 0, bp * tp, None, None, None, zeros_state, zeros_state, lw, l)
    ys, _, _, _, _ = _layer(x_sample, mods3, 1, ts, _rope_tables(ts),
                            cache_k[:, l].reshape(bs, -1, D_QK), cache_v[:, l].reshape(bs, -1, D_ATT),
                            state_ssm_fwd[:, l].reshape(bs, n_state, D_STATE),
                            state_ssm_bwd[:, l].reshape(bs, n_state, D_STATE), lw, l)
    new_k = ck.reshape(bp, 1, tp, ATT_HEADS, 2, QK_DIM)
    new_v = cv.reshape(bp, 1, tp, ATT_HEADS, V_DIM)
    new_hf = hf.reshape(bp, 1, SSM_HEADS, SSM_HEADDIM, D_STATE)
    new_hb = hb.reshape(bp, 1, SSM_HEADS, SSM_HEADDIM, D_STATE)
    return yp, ys, new_k, new_v, new_hf, new_hb
```

```python
import functools
import math

import numpy as np
import jax
import jax.numpy as jnp
from jax import lax
from jax.experimental import pallas as pl
from jax.experimental.pallas import tpu as pltpu

D_MODEL = 1024
GRID_W = 64
ATT_HEADS = 4
QK_DIM = 64
V_DIM = 128
D_QK = 512
D_ATT = 512
ROPE_BASE = 10000.0
D_SSM = 512
SSM_HEADDIM = 64
SSM_HEADS = 8
SSM_GROUPS = 2
D_STATE = 128
CONV_W = 5
CHUNK = 128
XBC_DIM = 1024
N_GROUPS = 4
EXPERTS_PER_GROUP = 4
N_EXPERTS = 16
EXPERT_FF = 256
EPS = 1e-6
MAIN_COLS = 2 * D_QK + D_ATT + D_SSM + XBC_DIM
H2X_W = D_MODEL + 128
MOE_TILE = 2048
MOE_CHUNK = 256
MOE_SLOTS = MOE_TILE // MOE_CHUNK + N_GROUPS
MOE_ROWS = MOE_SLOTS * MOE_CHUNK
MOE_STEPS = MOE_SLOTS + 1
META_ROWS = MOE_STEPS * MOE_CHUNK
COL_CUM, COL_DT, COL_ECUM, COL_TOEND = 0, 16, 32, 48
HROWS = 32
LANES = 128
HALO = 8
VMEM_LIMIT = 56 * 1024 * 1024

LOG2E = math.log2(math.e)
SUM_ROWS = 16
F32 = jnp.float32
BF16 = jnp.bfloat16


def _cparams(sem):
    return pltpu.CompilerParams(dimension_semantics=sem, vmem_limit_bytes=VMEM_LIMIT)


def _sigmoid(x):
    return 1.0 / (1.0 + jnp.exp(-x))


def _silu(x):
    return x * _sigmoid(x)


def _ada_kernel(condT_ref, w_ref, b_ref, o_ref):
    s = _silu(condT_ref[...])
    w = w_ref[...]
    b = b_ref[...]
    o_ref[...] = jnp.zeros_like(o_ref)
    for r in range(3):
        o_ref[r:r + 1, :] = jnp.sum(w * s[:, r:r + 1], axis=0, keepdims=True) + b


def _ada(condT, w_ada, b_ada):
    bn = 1024
    n = w_ada.shape[1]
    return pl.pallas_call(
        _ada_kernel,
        grid=(n // bn,),
        in_specs=[pl.BlockSpec((D_MODEL, 8), lambda j: (0, 0)),
                  pl.BlockSpec((D_MODEL, bn), lambda j: (0, j)),
                  pl.BlockSpec((1, bn), lambda j: (0, j))],
        out_specs=pl.BlockSpec((8, bn), lambda j: (0, j)),
        out_shape=jax.ShapeDtypeStruct((8, n), F32),
        compiler_params=_cparams(("arbitrary",)),
        name="ada",
    )(condT, w_ada, b_ada)


def _split3(x):
    hi = x.astype(BF16)
    r1 = x - hi.astype(F32)
    mid = r1.astype(BF16)
    lo = (r1 - mid.astype(F32)).astype(BF16)
    return jnp.concatenate([hi, mid, lo], axis=-1)


def _scan_matrices():
    t = np.arange(CHUNK)
    pre = (t[:, None] <= t[None, :]).astype(np.float32)
    suf = (t[:, None] >= t[None, :]).astype(np.float32)
    return jnp.asarray(np.stack([np.concatenate([pre] * 3, axis=0), np.concatenate([suf] * 3, axis=0)]), BF16)


def _head_scalars(dt_raw, alog_ref, dtb_ref, scan_ref):
    nh2 = 2 * SSM_HEADS
    xv = dt_raw.T[0:nh2, :] + dtb_ref[...]
    dt = jnp.maximum(xv, 0.0) + jnp.log(1.0 + jnp.exp(-jnp.abs(xv)))
    la3 = _split3(dt * (-jnp.exp(alog_ref[...])))
    fwd = lax.broadcasted_iota(jnp.int32, (nh2, CHUNK), 0) < SSM_HEADS
    cum = jnp.where(fwd, jnp.dot(la3, scan_ref[0], preferred_element_type=F32),
                    jnp.dot(la3, scan_ref[1], preferred_element_type=F32))
    cum_end = jnp.where(fwd, cum[:, CHUNK - 1:CHUNK], cum[:, 0:1])
    packed = jnp.concatenate([cum, dt, jnp.exp(cum), jnp.exp(cum_end - cum),
                              jnp.zeros((LANES - 4 * nh2, CHUNK), F32)], axis=0)
    return packed.T, jnp.concatenate([cum, jnp.exp(cum_end)], axis=0)


def _inproj_kernel(rope, x_ref, mods_ref, g_ref, w_ref, wdt_ref, alog_ref, dtb_ref, scan_ref, *rest):
    if rope:
        cos_ref, sa_ref, sb_ref, q_ref, kb_ref, vt_ref, z_ref, xbc_ref, cols_ref, hrow_ref = rest
    else:
        q_ref, kb_ref, vt_ref, z_ref, xbc_ref, cols_ref, hrow_ref, kf_ref, vf_ref = rest
    x = x_ref[...]
    shift = mods_ref[:, 0:D_MODEL]
    scale = mods_ref[:, D_MODEL:2 * D_MODEL]
    y = x * lax.rsqrt(jnp.mean(x * x, axis=-1, keepdims=True) + EPS) * g_ref[...]
    h = (y * (1.0 + scale) + shift).astype(BF16)
    dt_raw = jnp.dot(h, wdt_ref[...], preferred_element_type=F32)
    for ci in range(x.shape[0] // CHUNK):
        cols, hrow = _head_scalars(dt_raw[ci * CHUNK:(ci + 1) * CHUNK, :], alog_ref, dtb_ref, scan_ref)
        cols_ref[ci * CHUNK:(ci + 1) * CHUNK, :] = cols
        hrow_ref[ci * HROWS:(ci + 1) * HROWS, :] = hrow
    r = jnp.dot(h, w_ref[...], preferred_element_type=F32)
    q = r[:, 0:D_QK]
    k = r[:, D_QK:2 * D_QK]
    if rope:
        cos = cos_ref[...]
        sa = sa_ref[...]
        sb = sb_ref[...]

        def rot(t):
            parts = []
            for hh in range(ATT_HEADS):
                th = t[:, hh * LANES:(hh + 1) * LANES]
                parts.append(th * cos + pltpu.roll(th, LANES - 16, 1) * sa + pltpu.roll(th, 16, 1) * sb)
            return jnp.concatenate(parts, axis=1)

        q = rot(q)
        k = rot(k)
    v = r[:, 2 * D_QK:2 * D_QK + D_ATT]
    q_ref[...] = q
    kb_ref[...] = k.astype(BF16)
    vt_ref[...] = v.T.astype(BF16)
    if not rope:
        kf_ref[...] = k
        vf_ref[...] = v
    z_ref[...] = r[:, 2 * D_QK + D_ATT:2 * D_QK + D_ATT + D_SSM]
    x0 = 2 * D_QK + D_ATT + D_SSM
    for cb in range(XBC_DIM // LANES):
        xbc_ref[cb] = r[:, x0 + cb * LANES:x0 + (cb + 1) * LANES]


def _inproj(x, mods3, mod_row0, mod_tokens, seq_len, g, w_main, w_dt, alog, dtb, rope_tabs):
    n = x.shape[0]
    tm = 512 if seq_len % 512 == 0 else 256
    per_seq = seq_len // tm
    per_mod = mod_tokens // tm
    rope = rope_tabs is not None
    in_specs = [pl.BlockSpec((tm, D_MODEL), lambda i: (i, 0)),
                pl.BlockSpec((None, 1, 6 * D_MODEL), lambda i: (mod_row0 + i // per_mod, 0, 0)),
                pl.BlockSpec((1, D_MODEL), lambda i: (0, 0)),
                pl.BlockSpec((D_MODEL, MAIN_COLS), lambda i: (0, 0)),
                pl.BlockSpec((D_MODEL, LANES), lambda i: (0, 0)),
                pl.BlockSpec((2 * SSM_HEADS, CHUNK), lambda i: (0, 0)),
                pl.BlockSpec((2 * SSM_HEADS, CHUNK), lambda i: (0, 0)),
                pl.BlockSpec((2, 3 * CHUNK, CHUNK), lambda i: (0, 0, 0))]
    args = [x, mods3, g, w_main, w_dt, alog, dtb, _scan_matrices()]
    if rope:
        tab_spec = pl.BlockSpec((tm, LANES), lambda i: (i % per_seq, 0))
        in_specs += [tab_spec] * 3
        args += list(rope_tabs)
    def rows(wd, dtype=F32):
        return pl.BlockSpec((tm, wd), lambda i: (i, 0)), jax.ShapeDtypeStruct((n, wd), dtype)

    hr = tm // CHUNK * HROWS
    outs = [rows(D_QK), rows(D_QK, BF16),
            (pl.BlockSpec((None, D_ATT, tm), lambda i: (i // per_seq, 0, i % per_seq)),
             jax.ShapeDtypeStruct((n // seq_len, D_ATT, seq_len), BF16)),
            rows(D_SSM),
            (pl.BlockSpec((None, XBC_DIM // LANES, tm, LANES), lambda i: (i // per_seq, 0, i % per_seq, 0)),
             jax.ShapeDtypeStruct((n // seq_len, XBC_DIM // LANES, seq_len, LANES), F32)),
            rows(LANES),
            (pl.BlockSpec((hr, LANES), lambda i: (i, 0)), jax.ShapeDtypeStruct((n // CHUNK * HROWS, LANES), F32))]
    if not rope:
        outs += [rows(D_QK), rows(D_ATT)]
    return pl.pallas_call(
        functools.partial(_inproj_kernel, rope),
        grid=(n // tm,),
        in_specs=in_specs,
        out_specs=[o[0] for o in outs],
        out_shape=[o[1] for o in outs],
        compiler_params=_cparams(("parallel",)),
        name="inproj_rope" if rope else "inproj",
    )(*args)


def _attn_kernel(tk, lam0, has_cache, q_ref, k_ref, vt_ref, *rest):
    if has_cache:
        ck_ref, cvt_ref, lamp_ref, g_ref, o_ref = rest
    else:
        lamp_ref, g_ref, o_ref = rest
    tq = q_ref.shape[0]
    lp = lamp_ref[...]
    lam = (jnp.exp(jnp.sum(lp[0:1] * lp[1:2], axis=-1, keepdims=True))
           - jnp.exp(jnp.sum(lp[2:3] * lp[3:4], axis=-1, keepdims=True)) + lam0)
    ones_rows = jnp.ones((SUM_ROWS, tk), BF16)
    for hh in range(q_ref.shape[1] // LANES):
        hs = slice(hh * LANES, (hh + 1) * LANES)
        chunks = [(k_ref, vt_ref, c * tk) for c in range(k_ref.shape[0] // tk)]
        if has_cache:
            chunks += [(ck_ref, cvt_ref, c * tk) for c in range(ck_ref.shape[0] // tk)]
        q = q_ref[:, hs] * (QK_DIM ** -0.5 * LOG2E)
        lane = lax.broadcasted_iota(jnp.int32, q.shape, 1)
        qq_t = jnp.concatenate([jnp.where(lane < QK_DIM, q, 0.0), jnp.where(lane >= QK_DIM, q, 0.0)],
                               axis=0).T.astype(BF16)

        def scores(chunk):
            kr, _, start = chunk
            return jnp.dot(kr[start:start + tk, hs], qq_t, preferred_element_type=F32)

        def update(s, chunk, m, acc):
            _, vr, start = chunk
            m_new = jnp.maximum(m, jnp.max(s, axis=0, keepdims=True))
            alpha = jnp.exp2(m - m_new)
            p = jnp.exp2(s - m_new).astype(BF16)
            v_ext = jnp.concatenate([vr[hs, start:start + tk], ones_rows], axis=0)
            acc = alpha * acc + jnp.dot(v_ext, p, preferred_element_type=F32)
            return m_new, acc

        m = jnp.full((1, 2 * tq), -jnp.inf, F32)
        acc = jnp.zeros((V_DIM + SUM_ROWS, 2 * tq), F32)
        s = scores(chunks[0])
        for c, chunk in enumerate(chunks):
            s_next = scores(chunks[c + 1]) if c + 1 < len(chunks) else None
            m, acc = update(s, chunk, m, acc)
            s = s_next
        o = acc[0:V_DIM] / acc[V_DIM:V_DIM + 1]
        o = (o[:, 0:tq] - lam * o[:, tq:2 * tq]).T
        o = o * lax.rsqrt(jnp.mean(o * o, axis=-1, keepdims=True) + EPS)
        o_ref[:, hs] = o * g_ref[...] * (1.0 - lam0)


def _attention(q, k, vt, cache, lamp, g, lam0, tq, tk, nh):
    b, t, _ = q.shape
    wd = nh * LANES

    def kv_specs(length):
        return [pl.BlockSpec((None, length, wd), lambda bi, h, i: (bi, 0, h)),
                pl.BlockSpec((None, wd, length), lambda bi, h, i: (bi, h, 0))]

    in_specs = [pl.BlockSpec((None, tq, wd), lambda bi, h, i: (bi, i, h))] + kv_specs(t)
    args = [q, k, vt]
    if cache is not None:
        assert cache[0].shape[1] % tk == 0
        in_specs += kv_specs(cache[0].shape[1])
        args += list(cache)
    in_specs += [pl.BlockSpec((4, QK_DIM), lambda bi, h, i: (0, 0)),
                 pl.BlockSpec((1, V_DIM), lambda bi, h, i: (0, 0))]
    return pl.pallas_call(
        functools.partial(_attn_kernel, tk, lam0, cache is not None),
        grid=(b, ATT_HEADS // nh, t // tq),
        in_specs=in_specs,
        out_specs=pl.BlockSpec((None, tq, wd), lambda bi, h, i: (bi, i, h)),
        out_shape=jax.ShapeDtypeStruct((b, t, D_ATT), F32),
        compiler_params=_cparams(("parallel", "parallel", "arbitrary")),
        name="diff_attn",
    )(*args, lamp, g)


N_COLBLK = XBC_DIM // LANES
def _expand_matrices():
    out = []
    for d in range(2):
        e = np.zeros((LANES, 3 * D_SSM), np.float32)
        for blk, lane0 in enumerate((COL_DT, COL_ECUM, COL_TOEND)):
            for h in range(SSM_HEADS):
                e[lane0 + d * SSM_HEADS + h,
                  blk * D_SSM + h * SSM_HEADDIM:blk * D_SSM + (h + 1) * SSM_HEADDIM] = 1.0
        out.append(np.concatenate([e, e, e], axis=0))
    return jnp.asarray(np.stack(out), BF16)


def _ssd_chunk(reverse, xs, bm, cm, state, hrow, cols, e_ref):
    d0 = SSM_HEADS if reverse else 0
    cum = hrow[d0:d0 + SSM_HEADS, :]
    dec = hrow[2 * SSM_HEADS + d0:2 * SSM_HEADS + d0 + SSM_HEADS, 0:1]
    row = lax.broadcasted_iota(jnp.int32, (CHUNK, LANES), 0)
    lane = lax.broadcasted_iota(jnp.int32, (CHUNK, LANES), 1)
    causal = (row <= lane) if reverse else (row >= lane)
    lane_g = lax.broadcasted_iota(jnp.int32, (CHUNK, 2 * LANES), 1)
    spread = jnp.dot(_split3(cols), e_ref[1 if reverse else 0], preferred_element_type=F32)
    xd = xs * spread[:, 0:D_SSM]
    xdw = (xd * spread[:, 2 * D_SSM:3 * D_SSM]).astype(BF16)
    xd = xd.astype(BF16)
    ecum_x = spread[:, D_SSM:2 * D_SSM]
    rep = SSM_HEADS // SSM_GROUPS
    y_parts = []
    new_state = []
    for g in range(SSM_GROUPS):
        bg = bm[:, g * D_STATE:(g + 1) * D_STATE]
        cg = cm[:, g * D_STATE:(g + 1) * D_STATE]
        cbt = lax.dot_general(cg, bg, (((1,), (1,)), ((), ())), preferred_element_type=F32)
        rows = slice(g * rep * SSM_HEADDIM, (g + 1) * rep * SSM_HEADDIM)
        st_g = state[rows, :]
        y_off = lax.dot_general(cg, st_g.astype(BF16), (((1,), (1,)), ((), ())),
                                preferred_element_type=F32)
        cst = lax.dot_general(xdw[:, rows], bg, (((0,), (0,)), ((), ())), preferred_element_type=F32)
        xd_g = xd[:, rows]
        scs = []
        blocks = []
        for hh in range(rep):
            h = g * rep + hh
            seg = cols[:, COL_CUM + d0 + h:COL_CUM + d0 + h + 1] - cum[h:h + 1, :]
            decay = jnp.exp(jnp.where(causal, seg, -jnp.inf))
            scs.append((cbt * decay).astype(BF16))
            blocks.append(jnp.where(lane_g // SSM_HEADDIM == hh, xd_g, jnp.zeros_like(xd_g)))
        y_diag = jnp.dot(jnp.concatenate(scs, axis=1), jnp.concatenate(blocks, axis=0),
                         preferred_element_type=F32)
        dec_rows = jnp.concatenate(
            [jnp.broadcast_to(dec[g * rep + hh:g * rep + hh + 1, :], (SSM_HEADDIM, D_STATE)) for hh in range(rep)],
            axis=0)
        new_state.append(st_g * dec_rows + cst)
        y_parts.append(y_diag + y_off * ecum_x[:, rows])
    return jnp.concatenate(y_parts, axis=1), jnp.concatenate(new_state, axis=0)


def _ssd_kernel(nc, xc_ref, xp_ref, xn_ref, cols_ref, hrow_ref, z_ref, h0f_ref, h0b_ref, cw_ref, cb_ref,
                d_ref, ng_ref, e_ref, y_ref, hf_ref, hb_ref, state_ref, slab_ref, xs_ref, bc_ref, yf_ref):
    ps = pl.program_id(1)
    c = pl.program_id(2)
    n_bc = SSM_GROUPS * D_STATE

    @pl.when((ps == 0) & (c == 0))
    def _():
        state_ref[...] = h0f_ref[...]

    @pl.when(ps == 0)
    def _():
        t0 = pl.multiple_of(c * CHUNK, CHUNK)
        for bb in range(state_ref.shape[0]):
            slab_ref[bb, :, 0:HALO, :] = jnp.where(c > 0, xp_ref[bb], 0.0)
            slab_ref[bb, :, HALO:HALO + CHUNK, :] = xc_ref[bb]
            slab_ref[bb, :, HALO + CHUNK:2 * HALO + CHUNK, :] = jnp.where(c < nc - 1, xn_ref[bb], 0.0)
            blocks = []
            for cb in range(N_COLBLK):
                conv = cb_ref[cb:cb + 1, :]
                for kk in range(CONV_W):
                    off = HALO - CONV_W // 2 + kk
                    conv = conv + slab_ref[bb, cb, off:off + CHUNK, :] * cw_ref[kk, cb:cb + 1, :]
                blocks.append(_silu(conv))
            xs = jnp.concatenate(blocks[0:D_SSM // LANES], axis=1)
            bc = jnp.concatenate(blocks[D_SSM // LANES:], axis=1).astype(BF16)
            y, state = _ssd_chunk(False, xs, bc[:, 0:n_bc], bc[:, n_bc:], state_ref[bb], hrow_ref[bb],
                                  cols_ref[bb], e_ref)
            xs_ref[bb, pl.ds(t0, CHUNK), :] = xs
            bc_ref[bb, pl.ds(t0, CHUNK), :] = bc
            yf_ref[bb, pl.ds(t0, CHUNK), :] = y
            state_ref[bb] = state

    @pl.when((ps == 0) & (c == nc - 1))
    def _():
        hf_ref[...] = state_ref[...]
        state_ref[...] = h0b_ref[...]

    @pl.when(ps == 1)
    def _():
        t0 = pl.multiple_of((nc - 1 - c) * CHUNK, CHUNK)
        for bb in range(state_ref.shape[0]):
            xs = xs_ref[bb, pl.ds(t0, CHUNK), :]
            bc = bc_ref[bb, pl.ds(t0, CHUNK), :]
            y, state = _ssd_chunk(True, xs, bc[:, 0:n_bc], bc[:, n_bc:], state_ref[bb], hrow_ref[bb],
                                  cols_ref[bb], e_ref)
            state_ref[bb] = state
            y = y + yf_ref[bb, pl.ds(t0, CHUNK), :] + xs * d_ref[...]
            y = y * _silu(z_ref[bb])
            y = y * lax.rsqrt(jnp.mean(y * y, axis=-1, keepdims=True) + EPS)
            y_ref[bb] = y * ng_ref[...]

    @pl.when((ps == 1) & (c == nc - 1))
    def _():
        hb_ref[...] = state_ref[...]


def _ssd(xbc, cols, hrow, z, h0f, h0b, conv_w, conv_b, d_x, norm_g):
    b, _, l, _ = xbc.shape
    nb = 2
    assert b % nb == 0
    nc = l // CHUNK
    hb = CHUNK // HALO
    n_state = SSM_HEADS * SSM_HEADDIM
    last = nc - 1

    def fwd_chunk(ps, ci):
        return jnp.where(ps == 0, ci, last)

    def any_chunk(ps, ci):
        return jnp.where(ps == 0, ci, last - ci)

    def bwd_chunk(ps, ci):
        return jnp.where(ps == 0, last, last - ci)

    const2 = lambda bi, ps, ci: (0, 0)
    state_spec = pl.BlockSpec((nb, n_state, D_STATE), lambda bi, ps, ci: (bi, 0, 0))
    in_specs = [pl.BlockSpec((nb, N_COLBLK, CHUNK, LANES), lambda bi, ps, ci: (bi, 0, fwd_chunk(ps, ci), 0)),
                pl.BlockSpec((nb, N_COLBLK, HALO, LANES),
                             lambda bi, ps, ci: (bi, 0, jnp.maximum(fwd_chunk(ps, ci) * hb - 1, 0), 0)),
                pl.BlockSpec((nb, N_COLBLK, HALO, LANES),
                             lambda bi, ps, ci: (bi, 0, jnp.minimum((fwd_chunk(ps, ci) + 1) * hb, l // HALO - 1), 0)),
                pl.BlockSpec((nb, CHUNK, LANES), lambda bi, ps, ci: (bi, any_chunk(ps, ci), 0)),
                pl.BlockSpec((nb, HROWS, LANES), lambda bi, ps, ci: (bi, any_chunk(ps, ci), 0)),
                pl.BlockSpec((nb, CHUNK, D_SSM), lambda bi, ps, ci: (bi, bwd_chunk(ps, ci), 0)),
                state_spec, state_spec,
                pl.BlockSpec((CONV_W, N_COLBLK, LANES), lambda bi, ps, ci: (0, 0, 0)),
                pl.BlockSpec((N_COLBLK, LANES), const2),
                pl.BlockSpec((1, D_SSM), const2),
                pl.BlockSpec((1, D_SSM), const2),
                pl.BlockSpec((2, 3 * LANES, 3 * D_SSM), lambda bi, ps, ci: (0, 0, 0))]
    return pl.pallas_call(
        functools.partial(_ssd_kernel, nc),
        grid=(b // nb, 2, nc),
        in_specs=in_specs,
        out_specs=[pl.BlockSpec((nb, CHUNK, D_SSM), lambda bi, ps, ci: (bi, bwd_chunk(ps, ci), 0)),
                   state_spec, state_spec],
        out_shape=[jax.ShapeDtypeStruct((b, l, D_SSM), F32),
                   jax.ShapeDtypeStruct((b, n_state, D_STATE), F32),
                   jax.ShapeDtypeStruct((b, n_state, D_STATE), F32)],
        scratch_shapes=[pltpu.VMEM((nb, n_state, D_STATE), F32),
                        pltpu.VMEM((nb, N_COLBLK, CHUNK + 2 * HALO, LANES), F32),
                        pltpu.VMEM((nb, l, D_SSM), F32),
                        pltpu.VMEM((nb, l, 2 * SSM_GROUPS * D_STATE), BF16),
                        pltpu.VMEM((nb, l, D_SSM), F32)],
        compiler_params=_cparams(("parallel", "arbitrary", "arbitrary")),
        name="ssd",
    )(xbc, xbc, xbc, cols, hrow, z, h0f, h0b, conv_w, conv_b, d_x, norm_g, _expand_matrices())


def _outproj_kernel(att_ref, ssm_ref, x_ref, mods_ref, wa_ref, ws_ref, g_ref, wr_ref, br_ref,
                    x1_ref, h2x_ref):
    mix = (jnp.dot(att_ref[...].astype(BF16), wa_ref[...], preferred_element_type=F32)
           + jnp.dot(ssm_ref[...].astype(BF16), ws_ref[...], preferred_element_type=F32))
    gate1 = mods_ref[:, 2 * D_MODEL:3 * D_MODEL]
    shift2 = mods_ref[:, 3 * D_MODEL:4 * D_MODEL]
    scale2 = mods_ref[:, 4 * D_MODEL:5 * D_MODEL]
    x1 = x_ref[...] + gate1 * mix
    x1_ref[...] = x1
    y = x1 * lax.rsqrt(jnp.mean(x1 * x1, axis=-1, keepdims=True) + EPS) * g_ref[...]
    h2 = y * (1.0 + scale2) + shift2
    h_hi = h2.astype(BF16)
    h2x_ref[:, 0:D_MODEL] = h2
    h_lo = (h2 - h_hi.astype(F32)).astype(BF16)
    both = jnp.dot(h_hi, wr_ref[...], preferred_element_type=F32)
    logits = (both[:, 0:LANES] + both[:, LANES:2 * LANES]
              + jnp.dot(h_lo, wr_ref[:, 0:LANES], preferred_element_type=F32)) + br_ref[...]
    lane = lax.broadcasted_iota(jnp.int32, logits.shape, 1).astype(F32)
    neg = -jnp.inf
    big = float(1 << 20)
    is_g = lane < N_GROUPS
    gl = jnp.where(is_g, logits, neg)
    gmax = jnp.max(gl, axis=-1, keepdims=True)
    g_idx = jnp.min(jnp.where(gl == gmax, lane, big), axis=-1, keepdims=True)
    p_g = 1.0 / jnp.sum(jnp.where(is_g, jnp.exp(gl - gmax), 0.0), axis=-1, keepdims=True)
    e_lo = N_GROUPS + g_idx * EXPERTS_PER_GROUP
    in_grp = (lane >= e_lo) & (lane < e_lo + EXPERTS_PER_GROUP)
    el = jnp.where(in_grp, logits, neg)
    m1 = jnp.max(el, axis=-1, keepdims=True)
    i1 = jnp.min(jnp.where(el == m1, lane, big), axis=-1, keepdims=True)
    el2 = jnp.where(lane == i1, neg, el)
    m2 = jnp.max(el2, axis=-1, keepdims=True)
    i2 = jnp.min(jnp.where(el2 == m2, lane, big), axis=-1, keepdims=True)
    e2 = jnp.exp(m2 - m1)
    w1 = p_g / (1.0 + e2)
    w2 = p_g * e2 / (1.0 + e2)
    slab = (jnp.where(lane == i1 - e_lo, w1, 0.0) + jnp.where(lane == i2 - e_lo, w2, 0.0)
            + jnp.where(lane == EXPERTS_PER_GROUP, g_idx, 0.0))
    h2x_ref[:, D_MODEL:H2X_W] = slab


def _outproj(att, ssm, x, mods3, mod_row0, mod_tokens, wo_att, wo_ssm, g, w_router, b_router):
    n = x.shape[0]
    tm = 512
    per_mod = mod_tokens // tm
    return pl.pallas_call(
        _outproj_kernel,
        grid=(n // tm,),
        in_specs=[pl.BlockSpec((tm, D_ATT), lambda i: (i, 0)),
                  pl.BlockSpec((tm, D_SSM), lambda i: (i, 0)),
                  pl.BlockSpec((tm, D_MODEL), lambda i: (i, 0)),
                  pl.BlockSpec((None, 1, 6 * D_MODEL), lambda i: (mod_row0 + i // per_mod, 0, 0)),
                  pl.BlockSpec((D_ATT, D_MODEL), lambda i: (0, 0)),
                  pl.BlockSpec((D_SSM, D_MODEL), lambda i: (0, 0)),
                  pl.BlockSpec((1, D_MODEL), lambda i: (0, 0)),
                  pl.BlockSpec((D_MODEL, 2 * LANES), lambda i: (0, 0)),
                  pl.BlockSpec((1, LANES), lambda i: (0, 0))],
        out_specs=[pl.BlockSpec((tm, D_MODEL), lambda i: (i, 0)),
                   pl.BlockSpec((tm, H2X_W), lambda i: (i, 0))],
        out_shape=[jax.ShapeDtypeStruct((n, D_MODEL), F32),
                   jax.ShapeDtypeStruct((n, H2X_W), F32)],
        compiler_params=_cparams(("parallel",)),
        name="outproj_router",
    )(att, ssm, x, mods3, wo_att, wo_ssm, g, w_router, b_router)


def _route_kernel(slab_ref, meta_ref):
    t_n = MOE_TILE
    blk = LANES
    slab = slab_ref[...]
    lane = lax.broadcasted_iota(jnp.int32, (t_n, LANES), 1)
    gcol = jnp.sum(jnp.where(lane == EXPERTS_PER_GROUP, slab, 0.0), axis=-1, keepdims=True)
    member = (lane.astype(F32) == gcol) & (lane < N_GROUPS)
    a = jnp.where(member, 1.0, 0.0).astype(BF16)
    r_i = lax.broadcasted_iota(jnp.int32, (blk, blk), 0)
    c_i = lax.broadcasted_iota(jnp.int32, (blk, blk), 1)
    lower = jnp.where(c_i < r_i, 1.0, 0.0).astype(BF16)
    upper = jnp.where(r_i < c_i, 1.0, 0.0).astype(BF16)
    offs = jnp.zeros((1, LANES), F32)
    ranks = []
    for b in range(t_n // blk):
        ab = a[b * blk:(b + 1) * blk]
        rb = jnp.dot(lower, ab, preferred_element_type=F32)
        ranks.append(rb + offs)
        offs = offs + rb[blk - 1:blk] + ab[blk - 1:blk].astype(F32)
    rank = jnp.concatenate(ranks, axis=0)
    n_chunk = jnp.floor((offs + (MOE_CHUNK - 1)) * (1.0 / MOE_CHUNK))
    start = jnp.dot(jnp.broadcast_to(n_chunk, (8, LANES)).astype(BF16), upper,
                    preferred_element_type=F32)[0:1]
    end = start + n_chunk
    dest = jnp.sum(jnp.where(member, start * MOE_CHUNK + rank, 0.0), axis=-1, keepdims=True)
    tok = lax.broadcasted_iota(jnp.int32, (t_n, LANES), 0)
    digits = jnp.where(lane == 0, (tok // blk).astype(F32),
                       jnp.where(lane == 1, (tok % blk).astype(F32), jnp.where(lane == 2, 1.0, 0.0))).astype(BF16)
    sw = 512
    pieces = []
    for sc in range(MOE_ROWS // sw):
        s_id = (lax.broadcasted_iota(jnp.int32, (t_n, sw), 1) + sc * sw).astype(F32)
        hit = jnp.where(dest == s_id, 1.0, 0.0).astype(BF16)
        r = lax.dot_general(digits, hit, (((0,), (0,)), ((), ())), preferred_element_type=F32)
        tok_of = r[0:1] * blk + r[1:2]
        pieces.append(jnp.where(r[2:3] > 0.5, tok_of, float(t_n)))
    pieces.append(jnp.full((1, META_ROWS - MOE_ROWS), float(t_n), F32))
    perm = jnp.concatenate(pieces, axis=1)
    slot = lax.broadcasted_iota(jnp.int32, (1, META_ROWS), 1).astype(F32)
    lane1 = lax.broadcasted_iota(jnp.int32, (1, LANES), 1)
    cg = jnp.zeros((1, META_ROWS), F32)
    for g in range(N_GROUPS):
        end_g = jnp.sum(jnp.where(lane1 == g, end, 0.0), axis=-1, keepdims=True)
        cg = cg + jnp.where(slot >= end_g, 1.0, 0.0)
    n_act = jnp.broadcast_to(end_g, (1, META_ROWS))
    meta_ref[...] = jnp.concatenate([perm, cg, n_act, jnp.zeros((5, META_ROWS), F32)], axis=0).astype(jnp.int32)


def _route(h2x):
    n = h2x.shape[0]
    n_tiles = n // MOE_TILE
    return pl.pallas_call(
        _route_kernel,
        grid=(n_tiles,),
        in_specs=[pl.BlockSpec((MOE_TILE, LANES), lambda i: (i, D_MODEL // LANES))],
        out_specs=pl.BlockSpec((None, 8, META_ROWS), lambda i: (i, 0, 0)),
        out_shape=jax.ShapeDtypeStruct((n_tiles, 8, META_ROWS), jnp.int32),
        compiler_params=_cparams(("parallel",)),
        name="moe_route",
    )(h2x)


def _moe_kernel(perm_ref, cg_ref, nact_ref, h_ref, wg_ref, wu_ref, wd_ref, y_ref, hs0, hs1, ys0, ys1):
    i = pl.program_id(0)
    s = pl.program_id(1)
    n_act = nact_ref[i]
    tile_base = i * META_ROWS
    hs = (hs0, hs1)
    ys = (ys0, ys1)

    def gather(chunk, dst):
        base = tile_base + chunk * MOE_CHUNK
        for r in range(MOE_CHUNK):
            src = jnp.minimum(perm_ref[base + r], MOE_TILE - 1)
            dst[r:r + 1, :] = h_ref[pl.ds(src, 1), :]

    def scatter(chunk, src):
        base = tile_base + chunk * MOE_CHUNK
        for r in range(MOE_CHUNK):
            y_ref[pl.ds(perm_ref[base + r], 1), :] = src[r:r + 1, :]

    def ffn(src, dst):
        hb = src[:, 0:D_MODEL].astype(BF16)
        cw = src[:, D_MODEL:H2X_W]
        hid = []
        for e in range(EXPERTS_PER_GROUP):
            a = jnp.dot(hb, wg_ref[e], preferred_element_type=F32)
            u = jnp.dot(hb, wu_ref[e], preferred_element_type=F32)
            hid.append((_silu(a) * u * cw[:, e:e + 1]).astype(BF16))
        dst[...] = jnp.dot(jnp.concatenate(hid, axis=1), wd_ref[...], preferred_element_type=F32)

    @pl.when(s == 0)
    def _():
        y_ref[MOE_TILE:MOE_TILE + 8, :] = jnp.zeros((8, D_MODEL), F32)
        ys1[...] = jnp.zeros_like(ys1)
        gather(0, hs0)

    for par in (0, 1):
        @pl.when((s < n_act) & (s % 2 == par))
        def _():
            gather(s + 1, hs[1 - par])
            ffn(hs[par], ys[par])
            scatter(jnp.maximum(s - 1, 0), ys[1 - par])

        @pl.when((s == n_act) & (s % 2 == par))
        def _():
            scatter(s - 1, ys[1 - par])


def _moe(h2x, perm, cgrp, nact, wg, wu, wd):
    n = h2x.shape[0]
    n_tiles = n // MOE_TILE

    def w_idx(i, s, perm_ref, cg_ref, nact_ref):
        return (jnp.minimum(cg_ref[i * MOE_STEPS + s], N_GROUPS - 1), 0, 0)

    return pl.pallas_call(
        _moe_kernel,
        grid_spec=pltpu.PrefetchScalarGridSpec(
            num_scalar_prefetch=3,
            grid=(n_tiles, MOE_STEPS),
            in_specs=[pl.BlockSpec((MOE_TILE, H2X_W), lambda i, s, p, c, a: (i, 0)),
                      pl.BlockSpec((EXPERTS_PER_GROUP, D_MODEL, EXPERT_FF), w_idx),
                      pl.BlockSpec((EXPERTS_PER_GROUP, D_MODEL, EXPERT_FF), w_idx),
                      pl.BlockSpec((EXPERTS_PER_GROUP * EXPERT_FF, D_MODEL), lambda i, s, p, c, a: w_idx(i, s, p, c, a)[:2])],
            out_specs=pl.BlockSpec((None, MOE_TILE + 8, D_MODEL), lambda i, s, p, c, a: (i, 0, 0)),
            scratch_shapes=[pltpu.VMEM((MOE_CHUNK, H2X_W), F32), pltpu.VMEM((MOE_CHUNK, H2X_W), F32),
                            pltpu.VMEM((MOE_CHUNK, D_MODEL), F32), pltpu.VMEM((MOE_CHUNK, D_MODEL), F32)]),
        out_shape=jax.ShapeDtypeStruct((n_tiles, MOE_TILE + 8, D_MODEL), F32),
        compiler_params=_cparams(("parallel", "arbitrary")),
        name="moe_experts",
    )(perm, cgrp, nact, h2x, wg, wu, wd)


def _final_kernel(y_ref, x1_ref, mods_ref, fg_ref, o_ref):
    gate2 = mods_ref[:, 5 * D_MODEL:6 * D_MODEL]
    x2 = x1_ref[...] + gate2 * y_ref[...]
    o_ref[...] = x2 * lax.rsqrt(jnp.mean(x2 * x2, axis=-1, keepdims=True) + EPS) * fg_ref[...]


def _final(y, x1, mods3, mod_row0, mod_tokens, fg):
    n = x1.shape[0]
    tm = 512
    per_mod = mod_tokens // tm
    per_tile = MOE_TILE // tm
    return pl.pallas_call(
        _final_kernel,
        grid=(n // tm,),
        in_specs=[pl.BlockSpec((None, tm, D_MODEL), lambda j: (j // per_tile, j % per_tile, 0)),
                  pl.BlockSpec((tm, D_MODEL), lambda j: (j, 0)),
                  pl.BlockSpec((None, 1, 6 * D_MODEL), lambda j: (mod_row0 + j // per_mod, 0, 0)),
                  pl.BlockSpec((1, D_MODEL), lambda j: (0, 0))],
        out_specs=pl.BlockSpec((tm, D_MODEL), lambda j: (j, 0)),
        out_shape=jax.ShapeDtypeStruct((n, D_MODEL), F32),
        compiler_params=_cparams(("parallel",)),
        name="final_norm",
    )(y, x1, mods3, fg)


def _rope_tables(t):
    n_freq = QK_DIM // 4
    n_rows = t // GRID_W
    freqs = ROPE_BASE ** (-jnp.arange(n_freq, dtype=F32) / n_freq)
    ang_r = jnp.arange(n_rows, dtype=F32)[:, None] * freqs
    ang_c = jnp.arange(GRID_W, dtype=F32)[:, None] * freqs
    cr, sr, cc, sc = lax.optimization_barrier((jnp.cos(ang_r), jnp.sin(ang_r), jnp.cos(ang_c), jnp.sin(ang_c)))
    j = np.arange(LANES) % QK_DIM
    f_idx = j % n_freq
    by_row = (j < QK_DIM // 2)[None, None, :]
    first = ((j % (QK_DIM // 2)) < n_freq)[None, None, :]

    def table(r_small, c_small):
        return jnp.where(by_row, r_small[:, f_idx][:, None, :], c_small[:, f_idx][None, :, :])

    cos = table(cr, cc)
    sin = table(sr, sc)
    return (cos.reshape(t, LANES), jnp.where(first, -sin, 0.0).reshape(t, LANES),
            jnp.where(first, 0.0, sin).reshape(t, LANES))


def _layer(x, mods3, mod_row0, mod_tokens, rope_tabs, ctx_k, ctx_v, h0f, h0b, lw, layer):
    b, t, _ = x.shape
    n = b * t
    xf = x.reshape(n, D_MODEL)
    res = _inproj(xf, mods3, mod_row0, mod_tokens, t, lw["norm_mix_g"], lw["w_main"], lw["w_dt"], lw["alog"],
                  lw["dtb"], rope_tabs)
    q, kb, vt, z, xbc, cols, hrow = res[:7]
    if ctx_k is None:
        cache = None
        k3, v3 = res[7].reshape(b, t, D_QK), res[8].reshape(b, t, D_ATT)
    else:
        cache = (ctx_k.astype(BF16), jnp.swapaxes(ctx_v, 1, 2).astype(BF16))
        k3 = v3 = None
    lam0 = 0.8 - 0.6 * math.exp(-0.3 * layer)
    tq = 512 if t % 512 == 0 else 256
    tk = 512 if t % 512 == 0 else 256
    att = _attention(q.reshape(b, t, D_QK), kb.reshape(b, t, D_QK), vt, cache, lw["lamp"], lw["attn_subln_g"],
                     lam0, tq, tk, ATT_HEADS if t <= 512 else 1)
    ssm, hf, hb = _ssd(xbc, cols.reshape(b, t, LANES), hrow.reshape(b, t // CHUNK * HROWS, LANES),
                       z.reshape(b, t, D_SSM), h0f, h0b, lw["conv_w"], lw["conv_b"], lw["d_x"], lw["ssm_norm_g"])
    x1, h2x = _outproj(att.reshape(n, D_ATT), ssm.reshape(n, D_SSM), xf, mods3, mod_row0, mod_tokens,
                       lw["wo_att"], lw["wo_ssm"], lw["norm_ffn_g"], lw["w_router"], lw["b_router"])
    meta = _route(h2x)
    perm = meta[:, 0, :].reshape(-1)
    cgrp = meta[:, 1, :MOE_STEPS].reshape(-1)
    nact = meta[:, 2, 0]
    y = _moe(h2x, perm, cgrp, nact, lw["wg"], lw["wu"], lw["wd"])
    out = _final(y, x1, mods3, mod_row0, mod_tokens, lw["final_g"])
    return out.reshape(b, t, D_MODEL), k3, v3, hf, hb


def _pad_lanes(v, width=LANES):
    return jnp.pad(v, [(0, 0)] * (v.ndim - 1) + [(0, width - v.shape[-1])])


def kernel(x_prompt, x_sample, cache_k, cache_v, state_ssm_fwd, state_ssm_bwd, c, c_ctx, w_ada, b_ada, norm_mix_g, w_in, w_out, lambda_q1, lambda_k1, lambda_q2, lambda_k2, attn_subln_g, conv_w, conv_b, a_log_fwd, a_log_bwd, dt_bias_fwd, dt_bias_bwd, ssm_d, ssm_norm_g, norm_ffn_g, w_group_router, b_group_router, w_expert_router, b_expert_router, w_exp_gate, w_exp_up, w_exp_down, final_norm_g):
    depth = w_in.shape[0]
    assert depth == 1, "single trunk layer"
    bp, tp, _ = x_prompt.shape
    bs, ts, _ = x_sample.shape
    l = 0
    cond = jnp.concatenate([c_ctx[None], c], axis=0)
    condT = _pad_lanes(cond.T, 8)
    mods = _ada(condT, w_ada[l], b_ada[l][None])
    mods3 = mods.reshape(8, 1, 6 * D_MODEL)

    w_router = _pad_lanes(jnp.concatenate([w_group_router[l], w_expert_router[l]], axis=1))
    wr_hi = w_router.astype(BF16)
    wr_lo = (w_router - wr_hi.astype(F32)).astype(BF16)
    lw = dict(
        norm_mix_g=norm_mix_g[l][None],
        w_main=w_in[l].astype(BF16),
        w_dt=_pad_lanes(w_in[l][:, MAIN_COLS:]).astype(BF16),
        lamp=jnp.stack([lambda_q1[l], lambda_k1[l], lambda_q2[l], lambda_k2[l]]),
        attn_subln_g=attn_subln_g[l][None],
        conv_w=conv_w[l].reshape(CONV_W, N_COLBLK, LANES), conv_b=conv_b[l].reshape(N_COLBLK, LANES),
        alog=jnp.broadcast_to(jnp.concatenate([a_log_fwd[l], a_log_bwd[l]])[:, None], (2 * SSM_HEADS, CHUNK)),
        dtb=jnp.broadcast_to(jnp.concatenate([dt_bias_fwd[l], dt_bias_bwd[l]])[:, None], (2 * SSM_HEADS, CHUNK)),
        d_x=jnp.repeat(ssm_d[l], SSM_HEADDIM)[None], ssm_norm_g=ssm_norm_g[l][None],
        wo_att=w_out[l][:D_ATT].astype(BF16), wo_ssm=w_out[l][D_ATT:].astype(BF16),
        norm_ffn_g=norm_ffn_g[l][None],
        w_router=jnp.concatenate([wr_hi, wr_lo], axis=1),
        b_router=_pad_lanes(jnp.concatenate([b_group_router[l], b_expert_router[l]])[None]),
        wg=w_exp_gate[l].astype(BF16), wu=w_exp_up[l].astype(BF16), wd=w_exp_down[l].astype(BF16).reshape(N_EXPERTS * EXPERT_FF, D_MODEL),
        final_g=final_norm_g[None],
    )
    n_state = SSM_HEADS * SSM_HEADDIM
    zeros_state = jnp.zeros((bp, n_state, D_STATE), F32)
    yp, ck, cv, hf, hb = _layer(x_prompt, mods3, 0, bp * tp, None, None, None, zeros_state, zeros_state, lw, l)
    ys, _, _, _, _ = _layer(x_sample, mods3, 1, ts, _rope_tables(ts),
                            cache_k[:, l].reshape(bs, -1, D_QK), cache_v[:, l].reshape(bs, -1, D_ATT),
                            state_ssm_fwd[:, l].reshape(bs, n_state, D_STATE),
                            state_ssm_bwd[:, l].reshape(bs, n_state, D_STATE), lw, l)
    new_k = ck.reshape(bp, 1, tp, ATT_HEADS, 2, QK_DIM)
    new_v = cv.reshape(bp, 1, tp, ATT_HEADS, V_DIM)
    new_hf = hf.reshape(bp, 1, SSM_HEADS, SSM_HEADDIM, D_STATE)
    new_hb = hb.reshape(bp, 1, SSM_HEADS, SSM_HEADDIM, D_STATE)
    return yp, ys, new_k, new_v, new_hf, new_hb
```

```python
import functools
import math

import numpy as np
import jax
import jax.numpy as jnp
from jax import lax
from jax.experimental import pallas as pl
from jax.experimental.pallas import tpu as pltpu

D_MODEL = 1024
GRID_W = 64
ATT_HEADS = 4
QK_DIM = 64
V_DIM = 128
D_QK = 512
D_ATT = 512
ROPE_BASE = 10000.0
D_SSM = 512
SSM_HEADDIM = 64
SSM_HEADS = 8
SSM_GROUPS = 2
D_STATE = 128
CONV_W = 5
CHUNK = 128
XBC_DIM = 1024
N_GROUPS = 4
EXPERTS_PER_GROUP = 4
N_EXPERTS = 16
EXPERT_FF = 256
EPS = 1e-6
MAIN_COLS = 2 * D_QK + D_ATT + D_SSM + XBC_DIM
H2X_W = D_MODEL + 128
MOE_TILE = 2048
MOE_CHUNK = 256
MOE_SLOTS = MOE_TILE // MOE_CHUNK + N_GROUPS
MOE_ROWS = MOE_SLOTS * MOE_CHUNK
MOE_STEPS = MOE_SLOTS + 1
META_ROWS = MOE_STEPS * MOE_CHUNK
COL_CUM, COL_DT, COL_ECUM, COL_TOEND = 0, 16, 32, 48
HROWS = 32
LANES = 128
HALO = 8
VMEM_LIMIT = 56 * 1024 * 1024

LOG2E = math.log2(math.e)
SUM_ROWS = 16
F32 = jnp.float32
BF16 = jnp.bfloat16


def _cparams(sem):
    return pltpu.CompilerParams(dimension_semantics=sem, vmem_limit_bytes=VMEM_LIMIT)


def _sigmoid(x):
    return 1.0 / (1.0 + jnp.exp(-x))


def _silu(x):
    return x * _sigmoid(x)


def _ada_kernel(condT_ref, w_ref, b_ref, o_ref):
    s = _silu(condT_ref[...])
    w = w_ref[...]
    b = b_ref[...]
    o_ref[...] = jnp.zeros_like(o_ref)
    for r in range(3):
        o_ref[r:r + 1, :] = jnp.sum(w * s[:, r:r + 1], axis=0, keepdims=True) + b


def _ada(condT, w_ada, b_ada):
    bn = 1024
    n = w_ada.shape[1]
    return pl.pallas_call(
        _ada_kernel,
        grid=(n // bn,),
        in_specs=[pl.BlockSpec((D_MODEL, 8), lambda j: (0, 0)),
                  pl.BlockSpec((D_MODEL, bn), lambda j: (0, j)),
                  pl.BlockSpec((1, bn), lambda j: (0, j))],
        out_specs=pl.BlockSpec((8, bn), lambda j: (0, j)),
        out_shape=jax.ShapeDtypeStruct((8, n), F32),
        compiler_params=_cparams(("arbitrary",)),
        name="ada",
    )(condT, w_ada, b_ada)


def _split3(x):
    hi = x.astype(BF16)
    r1 = x - hi.astype(F32)
    mid = r1.astype(BF16)
    lo = (r1 - mid.astype(F32)).astype(BF16)
    return jnp.concatenate([hi, mid, lo], axis=-1)


def _scan_matrices():
    t = np.arange(CHUNK)
    pre = (t[:, None] <= t[None, :]).astype(np.float32)
    suf = (t[:, None] >= t[None, :]).astype(np.float32)
    return jnp.asarray(np.stack([np.concatenate([pre] * 3, axis=0), np.concatenate([suf] * 3, axis=0)]), BF16)


def _head_scalars(dt_raw, alog_ref, dtb_ref, scan_ref):
    nh2 = 2 * SSM_HEADS
    xv = dt_raw.T[0:nh2, :] + dtb_ref[...]
    dt = jnp.maximum(xv, 0.0) + jnp.log(1.0 + jnp.exp(-jnp.abs(xv)))
    la3 = _split3(dt * (-jnp.exp(alog_ref[...])))
    fwd = lax.broadcasted_iota(jnp.int32, (nh2, CHUNK), 0) < SSM_HEADS
    cum = jnp.where(fwd, jnp.dot(la3, scan_ref[0], preferred_element_type=F32),
                    jnp.dot(la3, scan_ref[1], preferred_element_type=F32))
    cum_end = jnp.where(fwd, cum[:, CHUNK - 1:CHUNK], cum[:, 0:1])
    packed = jnp.concatenate([cum, dt, jnp.exp(cum), jnp.exp(cum_end - cum),
                              jnp.zeros((LANES - 4 * nh2, CHUNK), F32)], axis=0)
    return packed.T, jnp.concatenate([cum, jnp.exp(cum_end)], axis=0)


def _inproj_kernel(rope, x_ref, mods_ref, g_ref, w_ref, wdt_ref, alog_ref, dtb_ref, scan_ref, *rest):
    if rope:
        cos_ref, sa_ref, sb_ref, q_ref, kb_ref, vt_ref, z_ref, xbc_ref, cols_ref, hrow_ref = rest
    else:
        q_ref, kb_ref, vt_ref, z_ref, xbc_ref, cols_ref, hrow_ref, kf_ref, vf_ref = rest
    x = x_ref[...]
    shift = mods_ref[:, 0:D_MODEL]
    scale = mods_ref[:, D_MODEL:2 * D_MODEL]
    y = x * lax.rsqrt(jnp.mean(x * x, axis=-1, keepdims=True) + EPS) * g_ref[...]
    h = (y * (1.0 + scale) + shift).astype(BF16)
    dt_raw = jnp.dot(h, wdt_ref[...], preferred_element_type=F32)
    for ci in range(x.shape[0] // CHUNK):
        cols, hrow = _head_scalars(dt_raw[ci * CHUNK:(ci + 1) * CHUNK, :], alog_ref, dtb_ref, scan_ref)
        cols_ref[ci * CHUNK:(ci + 1) * CHUNK, :] = cols
        hrow_ref[ci * HROWS:(ci + 1) * HROWS, :] = hrow
    r = jnp.dot(h, w_ref[...], preferred_element_type=F32)
    q = r[:, 0:D_QK]
    k = r[:, D_QK:2 * D_QK]
    if rope:
        cos = cos_ref[...]
        sa = sa_ref[...]
        sb = sb_ref[...]

        def rot(t):
            parts = []
            for hh in range(ATT_HEADS):
                th = t[:, hh * LANES:(hh + 1) * LANES]
                parts.append(th * cos + pltpu.roll(th, LANES - 16, 1) * sa + pltpu.roll(th, 16, 1) * sb)
            return jnp.concatenate(parts, axis=1)

        q = rot(q)
        k = rot(k)
    v = r[:, 2 * D_QK:2 * D_QK + D_ATT]
    q_ref[...] = q
    kb_ref[...] = k.astype(BF16)
    vt_ref[...] = v.T.astype(BF16)
    if not rope:
        tm = k.shape[0]
        for hh in range(ATT_HEADS):
            k_h = k[:, hh * LANES:(hh + 1) * LANES]
            vf_ref[pl.ds(hh, tm, stride=ATT_HEADS), :] = v[:, hh * LANES:(hh + 1) * LANES]
            kf_ref[pl.ds(2 * hh, tm, stride=2 * ATT_HEADS), :] = k_h
            kf_ref[pl.ds(2 * hh + 1, tm, stride=2 * ATT_HEADS), :] = pltpu.roll(k_h, QK_DIM, 1)
    z_ref[...] = r[:, 2 * D_QK + D_ATT:2 * D_QK + D_ATT + D_SSM]
    x0 = 2 * D_QK + D_ATT + D_SSM
    for cb in range(XBC_DIM // LANES):
        xbc_ref[cb] = r[:, x0 + cb * LANES:x0 + (cb + 1) * LANES]


def _inproj(x, mods3, mod_row0, mod_tokens, seq_len, g, w_main, w_dt, alog, dtb, rope_tabs):
    n = x.shape[0]
    tm = 512 if seq_len % 512 == 0 else 256
    per_seq = seq_len // tm
    per_mod = mod_tokens // tm
    rope = rope_tabs is not None
    in_specs = [pl.BlockSpec((tm, D_MODEL), lambda i: (i, 0)),
                pl.BlockSpec((None, 1, 6 * D_MODEL), lambda i: (mod_row0 + i // per_mod, 0, 0)),
                pl.BlockSpec((1, D_MODEL), lambda i: (0, 0)),
                pl.BlockSpec((D_MODEL, MAIN_COLS), lambda i: (0, 0)),
                pl.BlockSpec((D_MODEL, LANES), lambda i: (0, 0)),
                pl.BlockSpec((2 * SSM_HEADS, CHUNK), lambda i: (0, 0)),
                pl.BlockSpec((2 * SSM_HEADS, CHUNK), lambda i: (0, 0)),
                pl.BlockSpec((2, 3 * CHUNK, CHUNK), lambda i: (0, 0, 0))]
    args = [x, mods3, g, w_main, w_dt, alog, dtb, _scan_matrices()]
    if rope:
        tab_spec = pl.BlockSpec((tm, LANES), lambda i: (i % per_seq, 0))
        in_specs += [tab_spec] * 3
        args += list(rope_tabs)
    def rows(wd, dtype=F32):
        return pl.BlockSpec((tm, wd), lambda i: (i, 0)), jax.ShapeDtypeStruct((n, wd), dtype)

    hr = tm // CHUNK * HROWS
    outs = [rows(D_QK), rows(D_QK, BF16),
            (pl.BlockSpec((None, D_ATT, tm), lambda i: (i // per_seq, 0, i % per_seq)),
             jax.ShapeDtypeStruct((n // seq_len, D_ATT, seq_len), BF16)),
            rows(D_SSM),
            (pl.BlockSpec((None, XBC_DIM // LANES, tm, LANES), lambda i: (i // per_seq, 0, i % per_seq, 0)),
             jax.ShapeDtypeStruct((n // seq_len, XBC_DIM // LANES, seq_len, LANES), F32)),
            rows(LANES),
            (pl.BlockSpec((hr, LANES), lambda i: (i, 0)), jax.ShapeDtypeStruct((n // CHUNK * HROWS, LANES), F32))]
    if not rope:
        outs += [(pl.BlockSpec((tm * 2 * ATT_HEADS, LANES), lambda i: (i, 0)),
                  jax.ShapeDtypeStruct((n * 2 * ATT_HEADS, LANES), F32)),
                 (pl.BlockSpec((tm * ATT_HEADS, LANES), lambda i: (i, 0)),
                  jax.ShapeDtypeStruct((n * ATT_HEADS, LANES), F32))]
    return pl.pallas_call(
        functools.partial(_inproj_kernel, rope),
        grid=(n // tm,),
        in_specs=in_specs,
        out_specs=[o[0] for o in outs],
        out_shape=[o[1] for o in outs],
        compiler_params=_cparams(("parallel",)),
        name="inproj_rope" if rope else "inproj",
    )(*args)


def _attn_kernel(tk, lam0, has_cache, q_ref, k_ref, vt_ref, *rest):
    if has_cache:
        ck_ref, cvt_ref, lamp_ref, g_ref, o_ref = rest
    else:
        lamp_ref, g_ref, o_ref = rest
    tq = q_ref.shape[0]
    lp = lamp_ref[...]
    lam = (jnp.exp(jnp.sum(lp[0:1] * lp[1:2], axis=-1, keepdims=True))
           - jnp.exp(jnp.sum(lp[2:3] * lp[3:4], axis=-1, keepdims=True)) + lam0)
    ones_rows = jnp.ones((SUM_ROWS, tk), BF16)
    for hh in range(q_ref.shape[1] // LANES):
        hs = slice(hh * LANES, (hh + 1) * LANES)
        chunks = [(k_ref, vt_ref, c * tk) for c in range(k_ref.shape[0] // tk)]
        if has_cache:
            chunks += [(ck_ref, cvt_ref, c * tk) for c in range(ck_ref.shape[0] // tk)]
        q = q_ref[:, hs] * (QK_DIM ** -0.5 * LOG2E)
        lane = lax.broadcasted_iota(jnp.int32, q.shape, 1)
        qq_t = jnp.concatenate([jnp.where(lane < QK_DIM, q, 0.0), jnp.where(lane >= QK_DIM, q, 0.0)],
                               axis=0).T.astype(BF16)

        def scores(chunk):
            kr, _, start = chunk
            return jnp.dot(kr[start:start + tk, hs], qq_t, preferred_element_type=F32)

        def update(s, chunk, m, acc):
            _, vr, start = chunk
            m_new = jnp.maximum(m, jnp.max(s, axis=0, keepdims=True))
            alpha = jnp.exp2(m - m_new)
            p = jnp.exp2(s - m_new).astype(BF16)
            v_ext = jnp.concatenate([vr[hs, start:start + tk], ones_rows], axis=0)
            acc = alpha * acc + jnp.dot(v_ext, p, preferred_element_type=F32)
            return m_new, acc

        m = jnp.full((1, 2 * tq), -jnp.inf, F32)
        acc = jnp.zeros((V_DIM + SUM_ROWS, 2 * tq), F32)
        s = scores(chunks[0])
        for c, chunk in enumerate(chunks):
            s_next = scores(chunks[c + 1]) if c + 1 < len(chunks) else None
            m, acc = update(s, chunk, m, acc)
            s = s_next
        o = acc[0:V_DIM] / acc[V_DIM:V_DIM + 1]
        o = (o[:, 0:tq] - lam * o[:, tq:2 * tq]).T
        o = o * lax.rsqrt(jnp.mean(o * o, axis=-1, keepdims=True) + EPS)
        o_ref[:, hs] = o * g_ref[...] * (1.0 - lam0)


def _attention(q, k, vt, cache, lamp, g, lam0, tq, tk, nh):
    b, t, _ = q.shape
    wd = nh * LANES

    def kv_specs(length):
        return [pl.BlockSpec((None, length, wd), lambda bi, h, i: (bi, 0, h)),
                pl.BlockSpec((None, wd, length), lambda bi, h, i: (bi, h, 0))]

    in_specs = [pl.BlockSpec((None, tq, wd), lambda bi, h, i: (bi, i, h))] + kv_specs(t)
    args = [q, k, vt]
    if cache is not None:
        assert cache[0].shape[1] % tk == 0
        in_specs += kv_specs(cache[0].shape[1])
        args += list(cache)
    in_specs += [pl.BlockSpec((4, QK_DIM), lambda bi, h, i: (0, 0)),
                 pl.BlockSpec((1, V_DIM), lambda bi, h, i: (0, 0))]
    return pl.pallas_call(
        functools.partial(_attn_kernel, tk, lam0, cache is not None),
        grid=(b, ATT_HEADS // nh, t // tq),
        in_specs=in_specs,
        out_specs=pl.BlockSpec((None, tq, wd), lambda bi, h, i: (bi, i, h)),
        out_shape=jax.ShapeDtypeStruct((b, t, D_ATT), F32),
        compiler_params=_cparams(("parallel", "parallel", "arbitrary")),
        name="diff_attn",
    )(*args, lamp, g)


N_COLBLK = XBC_DIM // LANES
def _expand_matrices():
    out = []
    for d in range(2):
        e = np.zeros((LANES, 3 * D_SSM), np.float32)
        for blk, lane0 in enumerate((COL_DT, COL_ECUM, COL_TOEND)):
            for h in range(SSM_HEADS):
                e[lane0 + d * SSM_HEADS + h,
                  blk * D_SSM + h * SSM_HEADDIM:blk * D_SSM + (h + 1) * SSM_HEADDIM] = 1.0
        out.append(np.concatenate([e, e, e], axis=0))
    return jnp.asarray(np.stack(out), BF16)


def _ssd_chunk(reverse, xs, bm, cm, state, hrow, cols, e_ref):
    d0 = SSM_HEADS if reverse else 0
    cum = hrow[d0:d0 + SSM_HEADS, :]
    dec = hrow[2 * SSM_HEADS + d0:2 * SSM_HEADS + d0 + SSM_HEADS, 0:1]
    row = lax.broadcasted_iota(jnp.int32, (CHUNK, LANES), 0)
    lane = lax.broadcasted_iota(jnp.int32, (CHUNK, LANES), 1)
    causal = (row <= lane) if reverse else (row >= lane)
    lane_g = lax.broadcasted_iota(jnp.int32, (CHUNK, 2 * LANES), 1)
    spread = jnp.dot(_split3(cols), e_ref[1 if reverse else 0], preferred_element_type=F32)
    xd = xs * spread[:, 0:D_SSM]
    xdw = (xd * spread[:, 2 * D_SSM:3 * D_SSM]).astype(BF16)
    xd = xd.astype(BF16)
    ecum_x = spread[:, D_SSM:2 * D_SSM]
    rep = SSM_HEADS // SSM_GROUPS
    y_parts = []
    new_state = []
    for g in range(SSM_GROUPS):
        bg = bm[:, g * D_STATE:(g + 1) * D_STATE]
        cg = cm[:, g * D_STATE:(g + 1) * D_STATE]
        cbt = lax.dot_general(cg, bg, (((1,), (1,)), ((), ())), preferred_element_type=F32)
        rows = slice(g * rep * SSM_HEADDIM, (g + 1) * rep * SSM_HEADDIM)
        st_g = state[rows, :]
        y_off = lax.dot_general(cg, st_g.astype(BF16), (((1,), (1,)), ((), ())),
                                preferred_element_type=F32)
        cst = lax.dot_general(xdw[:, rows], bg, (((0,), (0,)), ((), ())), preferred_element_type=F32)
        xd_g = xd[:, rows]
        scs = []
        blocks = []
        for hh in range(rep):
            h = g * rep + hh
            seg = cols[:, COL_CUM + d0 + h:COL_CUM + d0 + h + 1] - cum[h:h + 1, :]
            decay = jnp.exp(jnp.where(causal, seg, -jnp.inf))
            scs.append((cbt * decay).astype(BF16))
            blocks.append(jnp.where(lane_g // SSM_HEADDIM == hh, xd_g, jnp.zeros_like(xd_g)))
        y_diag = jnp.dot(jnp.concatenate(scs, axis=1), jnp.concatenate(blocks, axis=0),
                         preferred_element_type=F32)
        dec_rows = jnp.concatenate(
            [jnp.broadcast_to(dec[g * rep + hh:g * rep + hh + 1, :], (SSM_HEADDIM, D_STATE)) for hh in range(rep)],
            axis=0)
        new_state.append(st_g * dec_rows + cst)
        y_parts.append(y_diag + y_off * ecum_x[:, rows])
    return jnp.concatenate(y_parts, axis=1), jnp.concatenate(new_state, axis=0)


def _ssd_kernel(nc, xc_ref, xp_ref, xn_ref, cols_ref, hrow_ref, z_ref, h0f_ref, h0b_ref, cw_ref, cb_ref,
                d_ref, ng_ref, e_ref, y_ref, hf_ref, hb_ref, state_ref, slab_ref, xs_ref, bc_ref, yf_ref):
    ps = pl.program_id(1)
    c = pl.program_id(2)
    n_bc = SSM_GROUPS * D_STATE

    @pl.when((ps == 0) & (c == 0))
    def _():
        state_ref[...] = h0f_ref[...]

    @pl.when(ps == 0)
    def _():
        t0 = pl.multiple_of(c * CHUNK, CHUNK)
        for bb in range(state_ref.shape[0]):
            slab_ref[bb, :, 0:HALO, :] = jnp.where(c > 0, xp_ref[bb], 0.0)
            slab_ref[bb, :, HALO:HALO + CHUNK, :] = xc_ref[bb]
            slab_ref[bb, :, HALO + CHUNK:2 * HALO + CHUNK, :] = jnp.where(c < nc - 1, xn_ref[bb], 0.0)
            blocks = []
            for cb in range(N_COLBLK):
                conv = cb_ref[cb:cb + 1, :]
                for kk in range(CONV_W):
                    off = HALO - CONV_W // 2 + kk
                    conv = conv + slab_ref[bb, cb, off:off + CHUNK, :] * cw_ref[kk, cb:cb + 1, :]
                blocks.append(_silu(conv))
            xs = jnp.concatenate(blocks[0:D_SSM // LANES], axis=1)
            bc = jnp.concatenate(blocks[D_SSM // LANES:], axis=1).astype(BF16)
            y, state = _ssd_chunk(False, xs, bc[:, 0:n_bc], bc[:, n_bc:], state_ref[bb], hrow_ref[bb],
                                  cols_ref[bb], e_ref)
            xs_ref[bb, pl.ds(t0, CHUNK), :] = xs
            bc_ref[bb, pl.ds(t0, CHUNK), :] = bc
            yf_ref[bb, pl.ds(t0, CHUNK), :] = y
            state_ref[bb] = state

    @pl.when((ps == 0) & (c == nc - 1))
    def _():
        hf_ref[...] = state_ref[...]
        state_ref[...] = h0b_ref[...]

    @pl.when(ps == 1)
    def _():
        t0 = pl.multiple_of((nc - 1 - c) * CHUNK, CHUNK)
        for bb in range(state_ref.shape[0]):
            xs = xs_ref[bb, pl.ds(t0, CHUNK), :]
            bc = bc_ref[bb, pl.ds(t0, CHUNK), :]
            y, state = _ssd_chunk(True, xs, bc[:, 0:n_bc], bc[:, n_bc:], state_ref[bb], hrow_ref[bb],
                                  cols_ref[bb], e_ref)
            state_ref[bb] = state
            y = y + yf_ref[bb, pl.ds(t0, CHUNK), :] + xs * d_ref[...]
            y = y * _silu(z_ref[bb])
            y = y * lax.rsqrt(jnp.mean(y * y, axis=-1, keepdims=True) + EPS)
            y_ref[bb] = y * ng_ref[...]

    @pl.when((ps == 1) & (c == nc - 1))
    def _():
        hb_ref[...] = state_ref[...]


def _ssd(xbc, cols, hrow, z, h0f, h0b, conv_w, conv_b, d_x, norm_g):
    b, _, l, _ = xbc.shape
    nb = 2
    assert b % nb == 0
    nc = l // CHUNK
    hb = CHUNK // HALO
    n_state = SSM_HEADS * SSM_HEADDIM
    last = nc - 1

    def fwd_chunk(ps, ci):
        return jnp.where(ps == 0, ci, last)

    def any_chunk(ps, ci):
        return jnp.where(ps == 0, ci, last - ci)

    def bwd_chunk(ps, ci):
        return jnp.where(ps == 0, last, last - ci)

    const2 = lambda bi, ps, ci: (0, 0)
    state_spec = pl.BlockSpec((nb, n_state, D_STATE), lambda bi, ps, ci: (bi, 0, 0))
    in_specs = [pl.BlockSpec((nb, N_COLBLK, CHUNK, LANES), lambda bi, ps, ci: (bi, 0, fwd_chunk(ps, ci), 0)),
                pl.BlockSpec((nb, N_COLBLK, HALO, LANES),
                             lambda bi, ps, ci: (bi, 0, jnp.maximum(fwd_chunk(ps, ci) * hb - 1, 0), 0)),
                pl.BlockSpec((nb, N_COLBLK, HALO, LANES),
                             lambda bi, ps, ci: (bi, 0, jnp.minimum((fwd_chunk(ps, ci) + 1) * hb, l // HALO - 1), 0)),
                pl.BlockSpec((nb, CHUNK, LANES), lambda bi, ps, ci: (bi, any_chunk(ps, ci), 0)),
                pl.BlockSpec((nb, HROWS, LANES), lambda bi, ps, ci: (bi, any_chunk(ps, ci), 0)),
                pl.BlockSpec((nb, CHUNK, D_SSM), lambda bi, ps, ci: (bi, bwd_chunk(ps, ci), 0)),
                state_spec, state_spec,
                pl.BlockSpec((CONV_W, N_COLBLK, LANES), lambda bi, ps, ci: (0, 0, 0)),
                pl.BlockSpec((N_COLBLK, LANES), const2),
                pl.BlockSpec((1, D_SSM), const2),
                pl.BlockSpec((1, D_SSM), const2),
                pl.BlockSpec((2, 3 * LANES, 3 * D_SSM), lambda bi, ps, ci: (0, 0, 0))]
    return pl.pallas_call(
        functools.partial(_ssd_kernel, nc),
        grid=(b // nb, 2, nc),
        in_specs=in_specs,
        out_specs=[pl.BlockSpec((nb, CHUNK, D_SSM), lambda bi, ps, ci: (bi, bwd_chunk(ps, ci), 0)),
                   state_spec, state_spec],
        out_shape=[jax.ShapeDtypeStruct((b, l, D_SSM), F32),
                   jax.ShapeDtypeStruct((b, n_state, D_STATE), F32),
                   jax.ShapeDtypeStruct((b, n_state, D_STATE), F32)],
        scratch_shapes=[pltpu.VMEM((nb, n_state, D_STATE), F32),
                        pltpu.VMEM((nb, N_COLBLK, CHUNK + 2 * HALO, LANES), F32),
                        pltpu.VMEM((nb, l, D_SSM), F32),
                        pltpu.VMEM((nb, l, 2 * SSM_GROUPS * D_STATE), BF16),
                        pltpu.VMEM((nb, l, D_SSM), F32)],
        compiler_params=_cparams(("parallel", "arbitrary", "arbitrary")),
        name="ssd",
    )(xbc, xbc, xbc, cols, hrow, z, h0f, h0b, conv_w, conv_b, d_x, norm_g, _expand_matrices())


def _outproj_kernel(att_ref, ssm_ref, x_ref, mods_ref, wa_ref, ws_ref, g_ref, wr_ref, br_ref,
                    x1_ref, h2x_ref):
    mix = (jnp.dot(att_ref[...].astype(BF16), wa_ref[...], preferred_element_type=F32)
           + jnp.dot(ssm_ref[...].astype(BF16), ws_ref[...], preferred_element_type=F32))
    gate1 = mods_ref[:, 2 * D_MODEL:3 * D_MODEL]
    shift2 = mods_ref[:, 3 * D_MODEL:4 * D_MODEL]
    scale2 = mods_ref[:, 4 * D_MODEL:5 * D_MODEL]
    x1 = x_ref[...] + gate1 * mix
    x1_ref[...] = x1
    y = x1 * lax.rsqrt(jnp.mean(x1 * x1, axis=-1, keepdims=True) + EPS) * g_ref[...]
    h2 = y * (1.0 + scale2) + shift2
    h_hi = h2.astype(BF16)
    h2x_ref[:, 0:D_MODEL] = h2
    h_lo = (h2 - h_hi.astype(F32)).astype(BF16)
    both = jnp.dot(h_hi, wr_ref[...], preferred_element_type=F32)
    logits = (both[:, 0:LANES] + both[:, LANES:2 * LANES]
              + jnp.dot(h_lo, wr_ref[:, 0:LANES], preferred_element_type=F32)) + br_ref[...]
    lane = lax.broadcasted_iota(jnp.int32, logits.shape, 1).astype(F32)
    neg = -jnp.inf
    big = float(1 << 20)
    is_g = lane < N_GROUPS
    gl = jnp.where(is_g, logits, neg)
    gmax = jnp.max(gl, axis=-1, keepdims=True)
    g_idx = jnp.min(jnp.where(gl == gmax, lane, big), axis=-1, keepdims=True)
    p_g = 1.0 / jnp.sum(jnp.where(is_g, jnp.exp(gl - gmax), 0.0), axis=-1, keepdims=True)
    e_lo = N_GROUPS + g_idx * EXPERTS_PER_GROUP
    in_grp = (lane >= e_lo) & (lane < e_lo + EXPERTS_PER_GROUP)
    el = jnp.where(in_grp, logits, neg)
    m1 = jnp.max(el, axis=-1, keepdims=True)
    i1 = jnp.min(jnp.where(el == m1, lane, big), axis=-1, keepdims=True)
    el2 = jnp.where(lane == i1, neg, el)
    m2 = jnp.max(el2, axis=-1, keepdims=True)
    i2 = jnp.min(jnp.where(el2 == m2, lane, big), axis=-1, keepdims=True)
    e2 = jnp.exp(m2 - m1)
    w1 = p_g / (1.0 + e2)
    w2 = p_g * e2 / (1.0 + e2)
    slab = (jnp.where(lane == i1 - e_lo, w1, 0.0) + jnp.where(lane == i2 - e_lo, w2, 0.0)
            + jnp.where(lane == EXPERTS_PER_GROUP, g_idx, 0.0))
    h2x_ref[:, D_MODEL:H2X_W] = slab


def _outproj(att, ssm, x, mods3, mod_row0, mod_tokens, wo_att, wo_ssm, g, w_router, b_router):
    n = x.shape[0]
    tm = 512
    per_mod = mod_tokens // tm
    return pl.pallas_call(
        _outproj_kernel,
        grid=(n // tm,),
        in_specs=[pl.BlockSpec((tm, D_ATT), lambda i: (i, 0)),
                  pl.BlockSpec((tm, D_SSM), lambda i: (i, 0)),
                  pl.BlockSpec((tm, D_MODEL), lambda i: (i, 0)),
                  pl.BlockSpec((None, 1, 6 * D_MODEL), lambda i: (mod_row0 + i // per_mod, 0, 0)),
                  pl.BlockSpec((D_ATT, D_MODEL), lambda i: (0, 0)),
                  pl.BlockSpec((D_SSM, D_MODEL), lambda i: (0, 0)),
                  pl.BlockSpec((1, D_MODEL), lambda i: (0, 0)),
                  pl.BlockSpec((D_MODEL, 2 * LANES), lambda i: (0, 0)),
                  pl.BlockSpec((1, LANES), lambda i: (0, 0))],
        out_specs=[pl.BlockSpec((tm, D_MODEL), lambda i: (i, 0)),
                   pl.BlockSpec((tm, H2X_W), lambda i: (i, 0))],
        out_shape=[jax.ShapeDtypeStruct((n, D_MODEL), F32),
                   jax.ShapeDtypeStruct((n, H2X_W), F32)],
        compiler_params=_cparams(("parallel",)),
        name="outproj_router",
    )(att, ssm, x, mods3, wo_att, wo_ssm, g, w_router, b_router)


def _route_kernel(slab_ref, meta_ref):
    t_n = MOE_TILE
    blk = LANES
    slab = slab_ref[...]
    lane = lax.broadcasted_iota(jnp.int32, (t_n, LANES), 1)
    gcol = jnp.sum(jnp.where(lane == EXPERTS_PER_GROUP, slab, 0.0), axis=-1, keepdims=True)
    member = (lane.astype(F32) == gcol) & (lane < N_GROUPS)
    a = jnp.where(member, 1.0, 0.0).astype(BF16)
    r_i = lax.broadcasted_iota(jnp.int32, (blk, blk), 0)
    c_i = lax.broadcasted_iota(jnp.int32, (blk, blk), 1)
    lower = jnp.where(c_i < r_i, 1.0, 0.0).astype(BF16)
    upper = jnp.where(r_i < c_i, 1.0, 0.0).astype(BF16)
    offs = jnp.zeros((1, LANES), F32)
    ranks = []
    for b in range(t_n // blk):
        ab = a[b * blk:(b + 1) * blk]
        rb = jnp.dot(lower, ab, preferred_element_type=F32)
        ranks.append(rb + offs)
        offs = offs + rb[blk - 1:blk] + ab[blk - 1:blk].astype(F32)
    rank = jnp.concatenate(ranks, axis=0)
    n_chunk = jnp.floor((offs + (MOE_CHUNK - 1)) * (1.0 / MOE_CHUNK))
    start = jnp.dot(jnp.broadcast_to(n_chunk, (8, LANES)).astype(BF16), upper,
                    preferred_element_type=F32)[0:1]
    end = start + n_chunk
    dest = jnp.sum(jnp.where(member, start * MOE_CHUNK + rank, 0.0), axis=-1, keepdims=True)
    tok = lax.broadcasted_iota(jnp.int32, (t_n, LANES), 0)
    digits = jnp.where(lane == 0, (tok // blk).astype(F32),
                       jnp.where(lane == 1, (tok % blk).astype(F32), jnp.where(lane == 2, 1.0, 0.0))).astype(BF16)
    sw = 512
    pieces = []
    for sc in range(MOE_ROWS // sw):
        s_id = (lax.broadcasted_iota(jnp.int32, (t_n, sw), 1) + sc * sw).astype(F32)
        hit = jnp.where(dest == s_id, 1.0, 0.0).astype(BF16)
        r = lax.dot_general(digits, hit, (((0,), (0,)), ((), ())), preferred_element_type=F32)
        tok_of = r[0:1] * blk + r[1:2]
        pieces.append(jnp.where(r[2:3] > 0.5, tok_of, float(t_n)))
    pieces.append(jnp.full((1, META_ROWS - MOE_ROWS), float(t_n), F32))
    perm = jnp.concatenate(pieces, axis=1)
    slot = lax.broadcasted_iota(jnp.int32, (1, META_ROWS), 1).astype(F32)
    lane1 = lax.broadcasted_iota(jnp.int32, (1, LANES), 1)
    cg = jnp.zeros((1, META_ROWS), F32)
    for g in range(N_GROUPS):
        end_g = jnp.sum(jnp.where(lane1 == g, end, 0.0), axis=-1, keepdims=True)
        cg = cg + jnp.where(slot >= end_g, 1.0, 0.0)
    n_act = jnp.broadcast_to(end_g, (1, META_ROWS))
    meta_ref[...] = jnp.concatenate([perm, cg, n_act, jnp.zeros((5, META_ROWS), F32)], axis=0).astype(jnp.int32)


def _route(h2x):
    n = h2x.shape[0]
    n_tiles = n // MOE_TILE
    return pl.pallas_call(
        _route_kernel,
        grid=(n_tiles,),
        in_specs=[pl.BlockSpec((MOE_TILE, LANES), lambda i: (i, D_MODEL // LANES))],
        out_specs=pl.BlockSpec((None, 8, META_ROWS), lambda i: (i, 0, 0)),
        out_shape=jax.ShapeDtypeStruct((n_tiles, 8, META_ROWS), jnp.int32),
        compiler_params=_cparams(("parallel",)),
        name="moe_route",
    )(h2x)


def _moe_kernel(perm_ref, cg_ref, nact_ref, h_ref, wg_ref, wu_ref, wd_ref, y_ref, hs0, hs1, ys0, ys1):
    i = pl.program_id(0)
    s = pl.program_id(1)
    n_act = nact_ref[i]
    tile_base = i * META_ROWS
    hs = (hs0, hs1)
    ys = (ys0, ys1)

    def gather(chunk, dst):
        base = tile_base + chunk * MOE_CHUNK
        for r in range(MOE_CHUNK):
            src = jnp.minimum(perm_ref[base + r], MOE_TILE - 1)
            dst[r:r + 1, :] = h_ref[pl.ds(src, 1), :]

    def scatter(chunk, src):
        base = tile_base + chunk * MOE_CHUNK
        for r in range(MOE_CHUNK):
            y_ref[pl.ds(perm_ref[base + r], 1), :] = src[r:r + 1, :]

    def ffn(src, dst):
        hb = src[:, 0:D_MODEL].astype(BF16)
        cw = src[:, D_MODEL:H2X_W]
        hid = []
        for e in range(EXPERTS_PER_GROUP):
            a = jnp.dot(hb, wg_ref[e], preferred_element_type=F32)
            u = jnp.dot(hb, wu_ref[e], preferred_element_type=F32)
            hid.append((_silu(a) * u * cw[:, e:e + 1]).astype(BF16))
        dst[...] = jnp.dot(jnp.concatenate(hid, axis=1), wd_ref[...], preferred_element_type=F32)

    @pl.when(s == 0)
    def _():
        y_ref[MOE_TILE:MOE_TILE + 8, :] = jnp.zeros((8, D_MODEL), F32)
        ys1[...] = jnp.zeros_like(ys1)
        gather(0, hs0)

    for par in (0, 1):
        @pl.when((s < n_act) & (s % 2 == par))
        def _():
            gather(s + 1, hs[1 - par])
            ffn(hs[par], ys[par])
            scatter(jnp.maximum(s - 1, 0), ys[1 - par])

        @pl.when((s == n_act) & (s % 2 == par))
        def _():
            scatter(s - 1, ys[1 - par])


def _moe(h2x, perm, cgrp, nact, wg, wu, wd):
    n = h2x.shape[0]
    n_tiles = n // MOE_TILE

    def w_idx(i, s, perm_ref, cg_ref, nact_ref):
        return (jnp.minimum(cg_ref[i * MOE_STEPS + s], N_GROUPS - 1), 0, 0)

    return pl.pallas_call(
        _moe_kernel,
        grid_spec=pltpu.PrefetchScalarGridSpec(
            num_scalar_prefetch=3,
            grid=(n_tiles, MOE_STEPS),
            in_specs=[pl.BlockSpec((MOE_TILE, H2X_W), lambda i, s, p, c, a: (i, 0)),
                      pl.BlockSpec((EXPERTS_PER_GROUP, D_MODEL, EXPERT_FF), w_idx),
                      pl.BlockSpec((EXPERTS_PER_GROUP, D_MODEL, EXPERT_FF), w_idx),
                      pl.BlockSpec((EXPERTS_PER_GROUP * EXPERT_FF, D_MODEL), lambda i, s, p, c, a: w_idx(i, s, p, c, a)[:2])],
            out_specs=pl.BlockSpec((None, MOE_TILE + 8, D_MODEL), lambda i, s, p, c, a: (i, 0, 0)),
            scratch_shapes=[pltpu.VMEM((MOE_CHUNK, H2X_W), F32), pltpu.VMEM((MOE_CHUNK, H2X_W), F32),
                            pltpu.VMEM((MOE_CHUNK, D_MODEL), F32), pltpu.VMEM((MOE_CHUNK, D_MODEL), F32)]),
        out_shape=jax.ShapeDtypeStruct((n_tiles, MOE_TILE + 8, D_MODEL), F32),
        compiler_params=_cparams(("parallel", "arbitrary")),
        name="moe_experts",
    )(perm, cgrp, nact, h2x, wg, wu, wd)


def _final_kernel(y_ref, x1_ref, mods_ref, fg_ref, o_ref):
    gate2 = mods_ref[:, 5 * D_MODEL:6 * D_MODEL]
    x2 = x1_ref[...] + gate2 * y_ref[...]
    o_ref[...] = x2 * lax.rsqrt(jnp.mean(x2 * x2, axis=-1, keepdims=True) + EPS) * fg_ref[...]


def _final(y, x1, mods3, mod_row0, mod_tokens, fg):
    n = x1.shape[0]
    tm = 512
    per_mod = mod_tokens // tm
    per_tile = MOE_TILE // tm
    return pl.pallas_call(
        _final_kernel,
        grid=(n // tm,),
        in_specs=[pl.BlockSpec((None, tm, D_MODEL), lambda j: (j // per_tile, j % per_tile, 0)),
                  pl.BlockSpec((tm, D_MODEL), lambda j: (j, 0)),
                  pl.BlockSpec((None, 1, 6 * D_MODEL), lambda j: (mod_row0 + j // per_mod, 0, 0)),
                  pl.BlockSpec((1, D_MODEL), lambda j: (0, 0))],
        out_specs=pl.BlockSpec((tm, D_MODEL), lambda j: (j, 0)),
        out_shape=jax.ShapeDtypeStruct((n, D_MODEL), F32),
        compiler_params=_cparams(("parallel",)),
        name="final_norm",
    )(y, x1, mods3, fg)


def _rope_tables(t):
    n_freq = QK_DIM // 4
    n_rows = t // GRID_W
    freqs = ROPE_BASE ** (-jnp.arange(n_freq, dtype=F32) / n_freq)
    ang_r = jnp.arange(n_rows, dtype=F32)[:, None] * freqs
    ang_c = jnp.arange(GRID_W, dtype=F32)[:, None] * freqs
    cr, sr, cc, sc = lax.optimization_barrier((jnp.cos(ang_r), jnp.sin(ang_r), jnp.cos(ang_c), jnp.sin(ang_c)))
    j = np.arange(LANES) % QK_DIM
    f_idx = j % n_freq
    by_row = (j < QK_DIM // 2)[None, None, :]
    first = ((j % (QK_DIM // 2)) < n_freq)[None, None, :]

    def table(r_small, c_small):
        return jnp.where(by_row, r_small[:, f_idx][:, None, :], c_small[:, f_idx][None, :, :])

    cos = table(cr, cc)
    sin = table(sr, sc)
    return (cos.reshape(t, LANES), jnp.where(first, -sin, 0.0).reshape(t, LANES),
            jnp.where(first, 0.0, sin).reshape(t, LANES))


def _layer(x, mods3, mod_row0, mod_tokens, rope_tabs, ctx_k, ctx_v, h0f, h0b, lw, layer):
    b, t, _ = x.shape
    n = b * t
    xf = x.reshape(n, D_MODEL)
    res = _inproj(xf, mods3, mod_row0, mod_tokens, t, lw["norm_mix_g"], lw["w_main"], lw["w_dt"], lw["alog"],
                  lw["dtb"], rope_tabs)
    q, kb, vt, z, xbc, cols, hrow = res[:7]
    if ctx_k is None:
        cache = None
        k3 = res[7].reshape(b, 1, t, ATT_HEADS, 2, LANES)[..., :QK_DIM]
        v3 = res[8].reshape(b, 1, t, ATT_HEADS, V_DIM)
    else:
        cache = (ctx_k.astype(BF16), jnp.swapaxes(ctx_v, 1, 2).astype(BF16))
        k3 = v3 = None
    lam0 = 0.8 - 0.6 * math.exp(-0.3 * layer)
    tq = 512 if t % 512 == 0 else 256
    tk = 512 if t % 512 == 0 else 256
    att = _attention(q.reshape(b, t, D_QK), kb.reshape(b, t, D_QK), vt, cache, lw["lamp"], lw["attn_subln_g"],
                     lam0, tq, tk, ATT_HEADS if t <= 512 else 1)
    ssm, hf, hb = _ssd(xbc, cols.reshape(b, t, LANES), hrow.reshape(b, t // CHUNK * HROWS, LANES),
                       z.reshape(b, t, D_SSM), h0f, h0b, lw["conv_w"], lw["conv_b"], lw["d_x"], lw["ssm_norm_g"])
    x1, h2x = _outproj(att.reshape(n, D_ATT), ssm.reshape(n, D_SSM), xf, mods3, mod_row0, mod_tokens,
                       lw["wo_att"], lw["wo_ssm"], lw["norm_ffn_g"], lw["w_router"], lw["b_router"])
    meta = _route(h2x)
    perm = meta[:, 0, :].reshape(-1)
    cgrp = meta[:, 1, :MOE_STEPS].reshape(-1)
    nact = meta[:, 2, 0]
    y = _moe(h2x, perm, cgrp, nact, lw["wg"], lw["wu"], lw["wd"])
    out = _final(y, x1, mods3, mod_row0, mod_tokens, lw["final_g"])
    return out.reshape(b, t, D_MODEL), k3, v3, hf, hb


def _pad_lanes(v, width=LANES):
    return jnp.pad(v, [(0, 0)] * (v.ndim - 1) + [(0, width - v.shape[-1])])


def kernel(x_prompt, x_sample, cache_k, cache_v, state_ssm_fwd, state_ssm_bwd, c, c_ctx, w_ada, b_ada, norm_mix_g, w_in, w_out, lambda_q1, lambda_k1, lambda_q2, lambda_k2, attn_subln_g, conv_w, conv_b, a_log_fwd, a_log_bwd, dt_bias_fwd, dt_bias_bwd, ssm_d, ssm_norm_g, norm_ffn_g, w_group_router, b_group_router, w_expert_router, b_expert_router, w_exp_gate, w_exp_up, w_exp_down, final_norm_g):
    depth = w_in.shape[0]
    assert depth == 1, "single trunk layer"
    bp, tp, _ = x_prompt.shape
    bs, ts, _ = x_sample.shape
    l = 0
    cond = jnp.concatenate([c_ctx[None], c], axis=0)
    condT = _pad_lanes(cond.T, 8)
    mods = _ada(condT, w_ada[l], b_ada[l][None])
    mods3 = mods.reshape(8, 1, 6 * D_MODEL)

    w_router = _pad_lanes(jnp.concatenate([w_group_router[l], w_expert_router[l]], axis=1))
    wr_hi = w_router.astype(BF16)
    wr_lo = (w_router - wr_hi.astype(F32)).astype(BF16)
    lw = dict(
        norm_mix_g=norm_mix_g[l][None],
        w_main=w_in[l].astype(BF16),
        w_dt=_pad_lanes(w_in[l][:, MAIN_COLS:]).astype(BF16),
        lamp=jnp.stack([lambda_q1[l], lambda_k1[l], lambda_q2[l], lambda_k2[l]]),
        attn_subln_g=attn_subln_g[l][None],
        conv_w=conv_w[l].reshape(CONV_W, N_COLBLK, LANES), conv_b=conv_b[l].reshape(N_COLBLK, LANES),
        alog=jnp.broadcast_to(jnp.concatenate([a_log_fwd[l], a_log_bwd[l]])[:, None], (2 * SSM_HEADS, CHUNK)),
        dtb=jnp.broadcast_to(jnp.concatenate([dt_bias_fwd[l], dt_bias_bwd[l]])[:, None], (2 * SSM_HEADS, CHUNK)),
        d_x=jnp.repeat(ssm_d[l], SSM_HEADDIM)[None], ssm_norm_g=ssm_norm_g[l][None],
        wo_att=w_out[l][:D_ATT].astype(BF16), wo_ssm=w_out[l][D_ATT:].astype(BF16),
        norm_ffn_g=norm_ffn_g[l][None],
        w_router=jnp.concatenate([wr_hi, wr_lo], axis=1),
        b_router=_pad_lanes(jnp.concatenate([b_group_router[l], b_expert_router[l]])[None]),
        wg=w_exp_gate[l].astype(BF16), wu=w_exp_up[l].astype(BF16), wd=w_exp_down[l].astype(BF16).reshape(N_EXPERTS * EXPERT_FF, D_MODEL),
        final_g=final_norm_g[None],
    )
    n_state = SSM_HEADS * SSM_HEADDIM
    zeros_state = jnp.zeros((bp, n_state, D_STATE), F32)
    yp, ck, cv, hf, hb = _layer(x_prompt, mods3, 0, bp * tp, None, None, None, zeros_state, zeros_state, lw, l)
    ys, _, _, _, _ = _layer(x_sample, mods3, 1, ts, _rope_tables(ts),
                            cache_k[:, l].reshape(bs, -1, D_QK), cache_v[:, l].reshape(bs, -1, D_ATT),
                            state_ssm_fwd[:, l].reshape(bs, n_state, D_STATE),
                            state_ssm_bwd[:, l].reshape(bs, n_state, D_STATE), lw, l)
    new_k, new_v = ck, cv
    new_hf = hf.reshape(bp, 1, SSM_HEADS, SSM_HEADDIM, D_STATE)
    new_hb = hb.reshape(bp, 1, SSM_HEADS, SSM_HEADDIM, D_STATE)
    return yp, ys, new_k, new_v, new_hf, new_hb
```

```python
import functools
import math

import numpy as np
import jax
import jax.numpy as jnp
from jax import lax
from jax.experimental import pallas as pl
from jax.experimental.pallas import tpu as pltpu

D_MODEL = 1024
GRID_W = 64
ATT_HEADS = 4
QK_DIM = 64
V_DIM = 128
D_QK = 512
D_ATT = 512
ROPE_BASE = 10000.0
D_SSM = 512
SSM_HEADDIM = 64
SSM_HEADS = 8
SSM_GROUPS = 2
D_STATE = 128
CONV_W = 5
CHUNK = 128
XBC_DIM = 1024
N_GROUPS = 4
EXPERTS_PER_GROUP = 4
N_EXPERTS = 16
EXPERT_FF = 256
EPS = 1e-6
MAIN_COLS = 2 * D_QK + D_ATT + D_SSM + XBC_DIM
H2X_W = D_MODEL + 128
MOE_TILE = 2048
MOE_CHUNK = 256
MOE_SLOTS = MOE_TILE // MOE_CHUNK + N_GROUPS
MOE_ROWS = MOE_SLOTS * MOE_CHUNK
MOE_STEPS = MOE_SLOTS + 1
META_ROWS = MOE_STEPS * MOE_CHUNK
COL_CUM, COL_DT, COL_ECUM, COL_TOEND = 0, 16, 32, 48
HROWS = 32
LANES = 128
HALO = 8
VMEM_LIMIT = 56 * 1024 * 1024

LOG2E = math.log2(math.e)
SUM_ROWS = 16
F32 = jnp.float32
BF16 = jnp.bfloat16


def _cparams(sem):
    return pltpu.CompilerParams(dimension_semantics=sem, vmem_limit_bytes=VMEM_LIMIT)


def _sigmoid(x):
    return 1.0 / (1.0 + jnp.exp(-x))


def _silu(x):
    return x * _sigmoid(x)


def _ada_kernel(condT_ref, w_ref, b_ref, o_ref):
    s = _silu(condT_ref[...])
    w = w_ref[...]
    b = b_ref[...]
    o_ref[...] = jnp.zeros_like(o_ref)
    for r in range(3):
        o_ref[r:r + 1, :] = jnp.sum(w * s[:, r:r + 1], axis=0, keepdims=True) + b


def _ada(condT, w_ada, b_ada):
    bn = 1024
    n = w_ada.shape[1]
    return pl.pallas_call(
        _ada_kernel,
        grid=(n // bn,),
        in_specs=[pl.BlockSpec((D_MODEL, 8), lambda j: (0, 0)),
                  pl.BlockSpec((D_MODEL, bn), lambda j: (0, j)),
                  pl.BlockSpec((1, bn), lambda j: (0, j))],
        out_specs=pl.BlockSpec((8, bn), lambda j: (0, j)),
        out_shape=jax.ShapeDtypeStruct((8, n), F32),
        compiler_params=_cparams(("arbitrary",)),
        name="ada",
    )(condT, w_ada, b_ada)


def _split3(x):
    hi = x.astype(BF16)
    r1 = x - hi.astype(F32)
    mid = r1.astype(BF16)
    lo = (r1 - mid.astype(F32)).astype(BF16)
    return jnp.concatenate([hi, mid, lo], axis=-1)


def _scan_matrices():
    t = np.arange(CHUNK)
    pre = (t[:, None] <= t[None, :]).astype(np.float32)
    suf = (t[:, None] >= t[None, :]).astype(np.float32)
    return jnp.asarray(np.stack([np.concatenate([pre] * 3, axis=0), np.concatenate([suf] * 3, axis=0)]), BF16)


def _head_scalars(dt_raw, alog_ref, dtb_ref, scan_ref):
    nh2 = 2 * SSM_HEADS
    xv = dt_raw.T[0:nh2, :] + dtb_ref[...]
    dt = jnp.maximum(xv, 0.0) + jnp.log(1.0 + jnp.exp(-jnp.abs(xv)))
    la3 = _split3(dt * (-jnp.exp(alog_ref[...])))
    fwd = lax.broadcasted_iota(jnp.int32, (nh2, CHUNK), 0) < SSM_HEADS
    cum = jnp.where(fwd, jnp.dot(la3, scan_ref[0], preferred_element_type=F32),
                    jnp.dot(la3, scan_ref[1], preferred_element_type=F32))
    cum_end = jnp.where(fwd, cum[:, CHUNK - 1:CHUNK], cum[:, 0:1])
    packed = jnp.concatenate([cum, dt, jnp.exp(cum), jnp.exp(cum_end - cum),
                              jnp.zeros((LANES - 4 * nh2, CHUNK), F32)], axis=0)
    return packed.T, jnp.concatenate([cum, jnp.exp(cum_end)], axis=0)


def _inproj_kernel(rope, x_ref, mods_ref, g_ref, w_ref, wdt_ref, alog_ref, dtb_ref, scan_ref, *rest):
    if rope:
        cos_ref, sa_ref, sb_ref, q_ref, kb_ref, vt_ref, z_ref, xbc_ref, cols_ref, hrow_ref = rest
    else:
        q_ref, kb_ref, vt_ref, z_ref, xbc_ref, cols_ref, hrow_ref, kf_ref, vf_ref = rest
    x = x_ref[...]
    shift = mods_ref[:, 0:D_MODEL]
    scale = mods_ref[:, D_MODEL:2 * D_MODEL]
    y = x * lax.rsqrt(jnp.mean(x * x, axis=-1, keepdims=True) + EPS) * g_ref[...]
    h = (y * (1.0 + scale) + shift).astype(BF16)
    dt_raw = jnp.dot(h, wdt_ref[...], preferred_element_type=F32)
    for ci in range(x.shape[0] // CHUNK):
        cols, hrow = _head_scalars(dt_raw[ci * CHUNK:(ci + 1) * CHUNK, :], alog_ref, dtb_ref, scan_ref)
        cols_ref[ci * CHUNK:(ci + 1) * CHUNK, :] = cols
        hrow_ref[ci * HROWS:(ci + 1) * HROWS, :] = hrow
    r = jnp.dot(h, w_ref[...], preferred_element_type=F32)
    q = r[:, 0:D_QK]
    k = r[:, D_QK:2 * D_QK]
    if rope:
        cos = cos_ref[...]
        sa = sa_ref[...]
        sb = sb_ref[...]

        def rot(t):
            parts = []
            for hh in range(ATT_HEADS):
                th = t[:, hh * LANES:(hh + 1) * LANES]
                parts.append(th * cos + pltpu.roll(th, LANES - 16, 1) * sa + pltpu.roll(th, 16, 1) * sb)
            return jnp.concatenate(parts, axis=1)

        q = rot(q)
        k = rot(k)
    v = r[:, 2 * D_QK:2 * D_QK + D_ATT]
    q_ref[...] = q
    kb_ref[...] = k.astype(BF16)
    vt_ref[...] = v.T.astype(BF16)
    if not rope:
        tm = k.shape[0]
        for hh in range(ATT_HEADS):
            k_h = k[:, hh * LANES:(hh + 1) * LANES]
            vf_ref[pl.ds(hh, tm, stride=ATT_HEADS), :] = v[:, hh * LANES:(hh + 1) * LANES]
            kf_ref[pl.ds(2 * hh, tm, stride=2 * ATT_HEADS), :] = k_h
            kf_ref[pl.ds(2 * hh + 1, tm, stride=2 * ATT_HEADS), :] = pltpu.roll(k_h, QK_DIM, 1)
    z_ref[...] = r[:, 2 * D_QK + D_ATT:2 * D_QK + D_ATT + D_SSM]
    x0 = 2 * D_QK + D_ATT + D_SSM
    for cb in range(XBC_DIM // LANES):
        xbc_ref[cb] = r[:, x0 + cb * LANES:x0 + (cb + 1) * LANES]


def _inproj(x, mods3, mod_row0, mod_tokens, seq_len, g, w_main, w_dt, alog, dtb, rope_tabs):
    n = x.shape[0]
    tm = 512 if seq_len % 512 == 0 else 256
    per_seq = seq_len // tm
    per_mod = mod_tokens // tm
    rope = rope_tabs is not None
    in_specs = [pl.BlockSpec((tm, D_MODEL), lambda i: (i, 0)),
                pl.BlockSpec((None, 1, 6 * D_MODEL), lambda i: (mod_row0 + i // per_mod, 0, 0)),
                pl.BlockSpec((1, D_MODEL), lambda i: (0, 0)),
                pl.BlockSpec((D_MODEL, MAIN_COLS), lambda i: (0, 0)),
                pl.BlockSpec((D_MODEL, LANES), lambda i: (0, 0)),
                pl.BlockSpec((2 * SSM_HEADS, CHUNK), lambda i: (0, 0)),
                pl.BlockSpec((2 * SSM_HEADS, CHUNK), lambda i: (0, 0)),
                pl.BlockSpec((2, 3 * CHUNK, CHUNK), lambda i: (0, 0, 0))]
    args = [x, mods3, g, w_main, w_dt, alog, dtb, _scan_matrices()]
    if rope:
        tab_spec = pl.BlockSpec((tm, LANES), lambda i: (i % per_seq, 0))
        in_specs += [tab_spec] * 3
        args += list(rope_tabs)
    def rows(wd, dtype=F32):
        return pl.BlockSpec((tm, wd), lambda i: (i, 0)), jax.ShapeDtypeStruct((n, wd), dtype)

    hr = tm // CHUNK * HROWS
    outs = [rows(D_QK), rows(D_QK, BF16),
            (pl.BlockSpec((None, D_ATT, tm), lambda i: (i // per_seq, 0, i % per_seq)),
             jax.ShapeDtypeStruct((n // seq_len, D_ATT, seq_len), BF16)),
            rows(D_SSM),
            (pl.BlockSpec((None, XBC_DIM // LANES, tm, LANES), lambda i: (i // per_seq, 0, i % per_seq, 0)),
             jax.ShapeDtypeStruct((n // seq_len, XBC_DIM // LANES, seq_len, LANES), F32)),
            rows(LANES),
            (pl.BlockSpec((hr, LANES), lambda i: (i, 0)), jax.ShapeDtypeStruct((n // CHUNK * HROWS, LANES), F32))]
    if not rope:
        outs += [(pl.BlockSpec((tm * 2 * ATT_HEADS, LANES), lambda i: (i, 0)),
                  jax.ShapeDtypeStruct((n * 2 * ATT_HEADS, LANES), F32)),
                 (pl.BlockSpec((tm * ATT_HEADS, LANES), lambda i: (i, 0)),
                  jax.ShapeDtypeStruct((n * ATT_HEADS, LANES), F32))]
    return pl.pallas_call(
        functools.partial(_inproj_kernel, rope),
        grid=(n // tm,),
        in_specs=in_specs,
        out_specs=[o[0] for o in outs],
        out_shape=[o[1] for o in outs],
        compiler_params=_cparams(("parallel",)),
        name="inproj_rope" if rope else "inproj",
    )(*args)


def _attn_kernel(tk, lam0, has_cache, q_ref, k_ref, vt_ref, *rest):
    if has_cache:
        ck_ref, cvt_ref, lamp_ref, g_ref, o_ref = rest
    else:
        lamp_ref, g_ref, o_ref = rest
    tq = q_ref.shape[0]
    lp = lamp_ref[...]
    lam = (jnp.exp(jnp.sum(lp[0:1] * lp[1:2], axis=-1, keepdims=True))
           - jnp.exp(jnp.sum(lp[2:3] * lp[3:4], axis=-1, keepdims=True)) + lam0)
    ones_rows = jnp.ones((SUM_ROWS, tk), BF16)
    for hh in range(q_ref.shape[1] // LANES):
        hs = slice(hh * LANES, (hh + 1) * LANES)
        chunks = [(k_ref, vt_ref, c * tk) for c in range(k_ref.shape[0] // tk)]
        if has_cache:
            chunks += [(ck_ref, cvt_ref, c * tk) for c in range(ck_ref.shape[0] // tk)]
        q = q_ref[:, hs] * (QK_DIM ** -0.5 * LOG2E)
        lane = lax.broadcasted_iota(jnp.int32, q.shape, 1)
        qq_t = jnp.concatenate([jnp.where(lane < QK_DIM, q, 0.0), jnp.where(lane >= QK_DIM, q, 0.0)],
                               axis=0).T.astype(BF16)

        def scores(chunk):
            kr, _, start = chunk
            return jnp.dot(kr[start:start + tk, hs], qq_t, preferred_element_type=F32)

        def update(s, chunk, m, acc):
            _, vr, start = chunk
            m_new = jnp.maximum(m, jnp.max(s, axis=0, keepdims=True))
            alpha = jnp.exp2(m - m_new)
            p = jnp.exp2(s - m_new).astype(BF16)
            v_ext = jnp.concatenate([vr[hs, start:start + tk], ones_rows], axis=0)
            acc = alpha * acc + jnp.dot(v_ext, p, preferred_element_type=F32)
            return m_new, acc

        m = jnp.full((1, 2 * tq), -jnp.inf, F32)
        acc = jnp.zeros((V_DIM + SUM_ROWS, 2 * tq), F32)
        s = scores(chunks[0])
        for c, chunk in enumerate(chunks):
            s_next = scores(chunks[c + 1]) if c + 1 < len(chunks) else None
            m, acc = update(s, chunk, m, acc)
            s = s_next
        o = acc[0:V_DIM] / acc[V_DIM:V_DIM + 1]
        o = (o[:, 0:tq] - lam * o[:, tq:2 * tq]).T
        o = o * lax.rsqrt(jnp.mean(o * o, axis=-1, keepdims=True) + EPS)
        o_ref[:, hs] = o * g_ref[...] * (1.0 - lam0)


def _attention(q, k, vt, cache, lamp, g, lam0, tq, tk, nh):
    b, t, _ = q.shape
    wd = nh * LANES

    def kv_specs(length):
        return [pl.BlockSpec((None, length, wd), lambda bi, h, i: (bi, 0, h)),
                pl.BlockSpec((None, wd, length), lambda bi, h, i: (bi, h, 0))]

    in_specs = [pl.BlockSpec((None, tq, wd), lambda bi, h, i: (bi, i, h))] + kv_specs(t)
    args = [q, k, vt]
    if cache is not None:
        assert cache[0].shape[1] % tk == 0
        in_specs += kv_specs(cache[0].shape[1])
        args += list(cache)
    in_specs += [pl.BlockSpec((4, QK_DIM), lambda bi, h, i: (0, 0)),
                 pl.BlockSpec((1, V_DIM), lambda bi, h, i: (0, 0))]
    return pl.pallas_call(
        functools.partial(_attn_kernel, tk, lam0, cache is not None),
        grid=(b, ATT_HEADS // nh, t // tq),
        in_specs=in_specs,
        out_specs=pl.BlockSpec((None, tq, wd), lambda bi, h, i: (bi, i, h)),
        out_shape=jax.ShapeDtypeStruct((b, t, D_ATT), F32),
        compiler_params=_cparams(("parallel", "parallel", "arbitrary")),
        name="diff_attn",
    )(*args, lamp, g)


N_COLBLK = XBC_DIM // LANES
def _expand_matrices():
    out = []
    for d in range(2):
        e = np.zeros((LANES, 3 * D_SSM), np.float32)
        for blk, lane0 in enumerate((COL_DT, COL_ECUM, COL_TOEND)):
            for h in range(SSM_HEADS):
                e[lane0 + d * SSM_HEADS + h,
                  blk * D_SSM + h * SSM_HEADDIM:blk * D_SSM + (h + 1) * SSM_HEADDIM] = 1.0
        out.append(np.concatenate([e, e, e], axis=0))
    return jnp.asarray(np.stack(out), BF16)


def _ssd_chunk(reverse, xs, bm, cm, state, hrow, cols, e_ref):
    d0 = SSM_HEADS if reverse else 0
    cum = hrow[d0:d0 + SSM_HEADS, :]
    dec = hrow[2 * SSM_HEADS + d0:2 * SSM_HEADS + d0 + SSM_HEADS, 0:1]
    row = lax.broadcasted_iota(jnp.int32, (CHUNK, LANES), 0)
    lane = lax.broadcasted_iota(jnp.int32, (CHUNK, LANES), 1)
    causal = (row <= lane) if reverse else (row >= lane)
    lane_g = lax.broadcasted_iota(jnp.int32, (CHUNK, 2 * LANES), 1)
    spread = jnp.dot(_split3(cols), e_ref[1 if reverse else 0], preferred_element_type=F32)
    xd = xs * spread[:, 0:D_SSM]
    xdw = (xd * spread[:, 2 * D_SSM:3 * D_SSM]).astype(BF16)
    xd = xd.astype(BF16)
    ecum_x = spread[:, D_SSM:2 * D_SSM]
    rep = SSM_HEADS // SSM_GROUPS
    y_parts = []
    new_state = []
    for g in range(SSM_GROUPS):
        bg = bm[:, g * D_STATE:(g + 1) * D_STATE]
        cg = cm[:, g * D_STATE:(g + 1) * D_STATE]
        cbt = lax.dot_general(cg, bg, (((1,), (1,)), ((), ())), preferred_element_type=F32)
        rows = slice(g * rep * SSM_HEADDIM, (g + 1) * rep * SSM_HEADDIM)
        st_g = state[rows, :]
        y_off = lax.dot_general(cg, st_g.astype(BF16), (((1,), (1,)), ((), ())),
                                preferred_element_type=F32)
        cst = lax.dot_general(xdw[:, rows], bg, (((0,), (0,)), ((), ())), preferred_element_type=F32)
        xd_g = xd[:, rows]
        scs = []
        blocks = []
        for hh in range(rep):
            h = g * rep + hh
            seg = cols[:, COL_CUM + d0 + h:COL_CUM + d0 + h + 1] - cum[h:h + 1, :]
            decay = jnp.exp(jnp.where(causal, seg, -jnp.inf))
            scs.append((cbt * decay).astype(BF16))
            blocks.append(jnp.where(lane_g // SSM_HEADDIM == hh, xd_g, jnp.zeros_like(xd_g)))
        y_diag = jnp.dot(jnp.concatenate(scs, axis=1), jnp.concatenate(blocks, axis=0),
                         preferred_element_type=F32)
        dec_rows = jnp.concatenate(
            [jnp.broadcast_to(dec[g * rep + hh:g * rep + hh + 1, :], (SSM_HEADDIM, D_STATE)) for hh in range(rep)],
            axis=0)
        new_state.append(st_g * dec_rows + cst)
        y_parts.append(y_diag + y_off * ecum_x[:, rows])
    return jnp.concatenate(y_parts, axis=1), jnp.concatenate(new_state, axis=0)


def _ssd_kernel(nc, xc_ref, xp_ref, xn_ref, cols_ref, hrow_ref, z_ref, h0f_ref, h0b_ref, cw_ref, cb_ref,
                d_ref, ng_ref, e_ref, y_ref, hf_ref, hb_ref, state_ref, slab_ref, xs_ref, bc_ref, yf_ref):
    ps = pl.program_id(1)
    c = pl.program_id(2)
    n_bc = SSM_GROUPS * D_STATE

    @pl.when((ps == 0) & (c == 0))
    def _():
        state_ref[...] = h0f_ref[...]

    @pl.when(ps == 0)
    def _():
        t0 = pl.multiple_of(c * CHUNK, CHUNK)
        for bb in range(state_ref.shape[0]):
            slab_ref[bb, :, 0:HALO, :] = jnp.where(c > 0, xp_ref[bb], 0.0)
            slab_ref[bb, :, HALO:HALO + CHUNK, :] = xc_ref[bb]
            slab_ref[bb, :, HALO + CHUNK:2 * HALO + CHUNK, :] = jnp.where(c < nc - 1, xn_ref[bb], 0.0)
            blocks = []
            for cb in range(N_COLBLK):
                conv = cb_ref[cb:cb + 1, :]
                for kk in range(CONV_W):
                    off = HALO - CONV_W // 2 + kk
                    conv = conv + slab_ref[bb, cb, off:off + CHUNK, :] * cw_ref[kk, cb:cb + 1, :]
                blocks.append(_silu(conv))
            xs = jnp.concatenate(blocks[0:D_SSM // LANES], axis=1)
            bc = jnp.concatenate(blocks[D_SSM // LANES:], axis=1).astype(BF16)
            y, state = _ssd_chunk(False, xs, bc[:, 0:n_bc], bc[:, n_bc:], state_ref[bb], hrow_ref[bb],
                                  cols_ref[bb], e_ref)
            xs_ref[bb, pl.ds(t0, CHUNK), :] = xs
            bc_ref[bb, pl.ds(t0, CHUNK), :] = bc
            yf_ref[bb, pl.ds(t0, CHUNK), :] = y
            state_ref[bb] = state

    @pl.when((ps == 0) & (c == nc - 1))
    def _():
        hf_ref[...] = state_ref[...]
        state_ref[...] = h0b_ref[...]

    @pl.when(ps == 1)
    def _():
        t0 = pl.multiple_of((nc - 1 - c) * CHUNK, CHUNK)
        for bb in range(state_ref.shape[0]):
            xs = xs_ref[bb, pl.ds(t0, CHUNK), :]
            bc = bc_ref[bb, pl.ds(t0, CHUNK), :]
            y, state = _ssd_chunk(True, xs, bc[:, 0:n_bc], bc[:, n_bc:], state_ref[bb], hrow_ref[bb],
                                  cols_ref[bb], e_ref)
            state_ref[bb] = state
            y = y + yf_ref[bb, pl.ds(t0, CHUNK), :] + xs * d_ref[...]
            y = y * _silu(z_ref[bb])
            y = y * lax.rsqrt(jnp.mean(y * y, axis=-1, keepdims=True) + EPS)
            y_ref[bb] = y * ng_ref[...]

    @pl.when((ps == 1) & (c == nc - 1))
    def _():
        hb_ref[...] = state_ref[...]


def _ssd(xbc, cols, hrow, z, h0f, h0b, conv_w, conv_b, d_x, norm_g):
    b, _, l, _ = xbc.shape
    nb = 2
    assert b % nb == 0
    nc = l // CHUNK
    hb = CHUNK // HALO
    n_state = SSM_HEADS * SSM_HEADDIM
    last = nc - 1

    def fwd_chunk(ps, ci):
        return jnp.where(ps == 0, ci, last)

    def any_chunk(ps, ci):
        return jnp.where(ps == 0, ci, last - ci)

    def bwd_chunk(ps, ci):
        return jnp.where(ps == 0, last, last - ci)

    const2 = lambda bi, ps, ci: (0, 0)
    state_spec = pl.BlockSpec((nb, n_state, D_STATE), lambda bi, ps, ci: (bi, 0, 0))
    in_specs = [pl.BlockSpec((nb, N_COLBLK, CHUNK, LANES), lambda bi, ps, ci: (bi, 0, fwd_chunk(ps, ci), 0)),
                pl.BlockSpec((nb, N_COLBLK, HALO, LANES),
                             lambda bi, ps, ci: (bi, 0, jnp.maximum(fwd_chunk(ps, ci) * hb - 1, 0), 0)),
                pl.BlockSpec((nb, N_COLBLK, HALO, LANES),
                             lambda bi, ps, ci: (bi, 0, jnp.minimum((fwd_chunk(ps, ci) + 1) * hb, l // HALO - 1), 0)),
                pl.BlockSpec((nb, CHUNK, LANES), lambda bi, ps, ci: (bi, any_chunk(ps, ci), 0)),
                pl.BlockSpec((nb, HROWS, LANES), lambda bi, ps, ci: (bi, any_chunk(ps, ci), 0)),
                pl.BlockSpec((nb, CHUNK, D_SSM), lambda bi, ps, ci: (bi, bwd_chunk(ps, ci), 0)),
                state_spec, state_spec,
                pl.BlockSpec((CONV_W, N_COLBLK, LANES), lambda bi, ps, ci: (0, 0, 0)),
                pl.BlockSpec((N_COLBLK, LANES), const2),
                pl.BlockSpec((1, D_SSM), const2),
                pl.BlockSpec((1, D_SSM), const2),
                pl.BlockSpec((2, 3 * LANES, 3 * D_SSM), lambda bi, ps, ci: (0, 0, 0))]
    return pl.pallas_call(
        functools.partial(_ssd_kernel, nc),
        grid=(b // nb, 2, nc),
        in_specs=in_specs,
        out_specs=[pl.BlockSpec((nb, CHUNK, D_SSM), lambda bi, ps, ci: (bi, bwd_chunk(ps, ci), 0)),
                   state_spec, state_spec],
        out_shape=[jax.ShapeDtypeStruct((b, l, D_SSM), F32),
                   jax.ShapeDtypeStruct((b, n_state, D_STATE), F32),
                   jax.ShapeDtypeStruct((b, n_state, D_STATE), F32)],
        scratch_shapes=[pltpu.VMEM((nb, n_state, D_STATE), F32),
                        pltpu.VMEM((nb, N_COLBLK, CHUNK + 2 * HALO, LANES), F32),
                        pltpu.VMEM((nb, l, D_SSM), F32),
                        pltpu.VMEM((nb, l, 2 * SSM_GROUPS * D_STATE), BF16),
                        pltpu.VMEM((nb, l, D_SSM), F32)],
        compiler_params=_cparams(("parallel", "arbitrary", "arbitrary")),
        name="ssd",
    )(xbc, xbc, xbc, cols, hrow, z, h0f, h0b, conv_w, conv_b, d_x, norm_g, _expand_matrices())


def _outproj_kernel(att_ref, ssm_ref, x_ref, mods_ref, wa_ref, ws_ref, g_ref, wr_ref, br_ref,
                    x1_ref, h2x_ref):
    mix = (jnp.dot(att_ref[...].astype(BF16), wa_ref[...], preferred_element_type=F32)
           + jnp.dot(ssm_ref[...].astype(BF16), ws_ref[...], preferred_element_type=F32))
    gate1 = mods_ref[:, 2 * D_MODEL:3 * D_MODEL]
    shift2 = mods_ref[:, 3 * D_MODEL:4 * D_MODEL]
    scale2 = mods_ref[:, 4 * D_MODEL:5 * D_MODEL]
    x1 = x_ref[...] + gate1 * mix
    x1_ref[...] = x1
    y = x1 * lax.rsqrt(jnp.mean(x1 * x1, axis=-1, keepdims=True) + EPS) * g_ref[...]
    h2 = y * (1.0 + scale2) + shift2
    h_hi = h2.astype(BF16)
    h2x_ref[:, 0:D_MODEL] = h2
    h_lo = (h2 - h_hi.astype(F32)).astype(BF16)
    both = jnp.dot(h_hi, wr_ref[...], preferred_element_type=F32)
    logits = (both[:, 0:LANES] + both[:, LANES:2 * LANES]
              + jnp.dot(h_lo, wr_ref[:, 0:LANES], preferred_element_type=F32)) + br_ref[...]
    lane = lax.broadcasted_iota(jnp.int32, logits.shape, 1).astype(F32)
    neg = -jnp.inf
    big = float(1 << 20)
    is_g = lane < N_GROUPS
    gl = jnp.where(is_g, logits, neg)
    gmax = jnp.max(gl, axis=-1, keepdims=True)
    g_idx = jnp.min(jnp.where(gl == gmax, lane, big), axis=-1, keepdims=True)
    p_g = 1.0 / jnp.sum(jnp.where(is_g, jnp.exp(gl - gmax), 0.0), axis=-1, keepdims=True)
    e_lo = N_GROUPS + g_idx * EXPERTS_PER_GROUP
    in_grp = (lane >= e_lo) & (lane < e_lo + EXPERTS_PER_GROUP)
    el = jnp.where(in_grp, logits, neg)
    m1 = jnp.max(el, axis=-1, keepdims=True)
    i1 = jnp.min(jnp.where(el == m1, lane, big), axis=-1, keepdims=True)
    el2 = jnp.where(lane == i1, neg, el)
    m2 = jnp.max(el2, axis=-1, keepdims=True)
    i2 = jnp.min(jnp.where(el2 == m2, lane, big), axis=-1, keepdims=True)
    e2 = jnp.exp(m2 - m1)
    w1 = p_g / (1.0 + e2)
    w2 = p_g * e2 / (1.0 + e2)
    slab = (jnp.where(lane == i1 - e_lo, w1, 0.0) + jnp.where(lane == i2 - e_lo, w2, 0.0)
            + jnp.where(lane == EXPERTS_PER_GROUP, g_idx, 0.0))
    h2x_ref[:, D_MODEL:H2X_W] = slab


def _outproj(att, ssm, x, mods3, mod_row0, mod_tokens, wo_att, wo_ssm, g, w_router, b_router):
    n = x.shape[0]
    tm = 512
    per_mod = mod_tokens // tm
    return pl.pallas_call(
        _outproj_kernel,
        grid=(n // tm,),
        in_specs=[pl.BlockSpec((tm, D_ATT), lambda i: (i, 0)),
                  pl.BlockSpec((tm, D_SSM), lambda i: (i, 0)),
                  pl.BlockSpec((tm, D_MODEL), lambda i: (i, 0)),
                  pl.BlockSpec((None, 1, 6 * D_MODEL), lambda i: (mod_row0 + i // per_mod, 0, 0)),
                  pl.BlockSpec((D_ATT, D_MODEL), lambda i: (0, 0)),
                  pl.BlockSpec((D_SSM, D_MODEL), lambda i: (0, 0)),
                  pl.BlockSpec((1, D_MODEL), lambda i: (0, 0)),
                  pl.BlockSpec((D_MODEL, 2 * LANES), lambda i: (0, 0)),
                  pl.BlockSpec((1, LANES), lambda i: (0, 0))],
        out_specs=[pl.BlockSpec((tm, D_MODEL), lambda i: (i, 0)),
                   pl.BlockSpec((tm, H2X_W), lambda i: (i, 0))],
        out_shape=[jax.ShapeDtypeStruct((n, D_MODEL), F32),
                   jax.ShapeDtypeStruct((n, H2X_W), F32)],
        compiler_params=_cparams(("parallel",)),
        name="outproj_router",
    )(att, ssm, x, mods3, wo_att, wo_ssm, g, w_router, b_router)


def _route_kernel(slab_ref, meta_ref):
    t_n = MOE_TILE
    blk = LANES
    slab = slab_ref[...]
    lane = lax.broadcasted_iota(jnp.int32, (t_n, LANES), 1)
    gcol = jnp.sum(jnp.where(lane == EXPERTS_PER_GROUP, slab, 0.0), axis=-1, keepdims=True)
    member = (lane.astype(F32) == gcol) & (lane < N_GROUPS)
    a = jnp.where(member, 1.0, 0.0).astype(BF16)
    r_i = lax.broadcasted_iota(jnp.int32, (blk, blk), 0)
    c_i = lax.broadcasted_iota(jnp.int32, (blk, blk), 1)
    lower = jnp.where(c_i < r_i, 1.0, 0.0).astype(BF16)
    upper = jnp.where(r_i < c_i, 1.0, 0.0).astype(BF16)
    offs = jnp.zeros((1, LANES), F32)
    ranks = []
    for b in range(t_n // blk):
        ab = a[b * blk:(b + 1) * blk]
        rb = jnp.dot(lower, ab, preferred_element_type=F32)
        ranks.append(rb + offs)
        offs = offs + rb[blk - 1:blk] + ab[blk - 1:blk].astype(F32)
    rank = jnp.concatenate(ranks, axis=0)
    n_chunk = jnp.floor((offs + (MOE_CHUNK - 1)) * (1.0 / MOE_CHUNK))
    start = jnp.dot(jnp.broadcast_to(n_chunk, (8, LANES)).astype(BF16), upper,
                    preferred_element_type=F32)[0:1]
    end = start + n_chunk
    dest = jnp.sum(jnp.where(member, start * MOE_CHUNK + rank, 0.0), axis=-1, keepdims=True)
    tok = lax.broadcasted_iota(jnp.int32, (t_n, LANES), 0)
    digits = jnp.where(lane == 0, (tok // blk).astype(F32),
                       jnp.where(lane == 1, (tok % blk).astype(F32), jnp.where(lane == 2, 1.0, 0.0))).astype(BF16)
    sw = 512
    pieces = []
    for sc in range(MOE_ROWS // sw):
        s_id = (lax.broadcasted_iota(jnp.int32, (t_n, sw), 1) + sc * sw).astype(F32)
        hit = jnp.where(dest == s_id, 1.0, 0.0).astype(BF16)
        r = lax.dot_general(digits, hit, (((0,), (0,)), ((), ())), preferred_element_type=F32)
        tok_of = r[0:1] * blk + r[1:2]
        pieces.append(jnp.where(r[2:3] > 0.5, tok_of, float(t_n)))
    pieces.append(jnp.full((1, META_ROWS - MOE_ROWS), float(t_n), F32))
    perm = jnp.concatenate(pieces, axis=1)
    slot = lax.broadcasted_iota(jnp.int32, (1, META_ROWS), 1).astype(F32)
    lane1 = lax.broadcasted_iota(jnp.int32, (1, LANES), 1)
    cg = jnp.zeros((1, META_ROWS), F32)
    for g in range(N_GROUPS):
        end_g = jnp.sum(jnp.where(lane1 == g, end, 0.0), axis=-1, keepdims=True)
        cg = cg + jnp.where(slot >= end_g, 1.0, 0.0)
    n_act = jnp.broadcast_to(end_g, (1, META_ROWS))
    meta_ref[...] = jnp.concatenate([perm, cg, n_act, jnp.zeros((5, META_ROWS), F32)], axis=0).astype(jnp.int32)


def _route(h2x):
    n = h2x.shape[0]
    n_tiles = n // MOE_TILE
    return pl.pallas_call(
        _route_kernel,
        grid=(n_tiles,),
        in_specs=[pl.BlockSpec((MOE_TILE, LANES), lambda i: (i, D_MODEL // LANES))],
        out_specs=pl.BlockSpec((None, 8, META_ROWS), lambda i: (i, 0, 0)),
        out_shape=jax.ShapeDtypeStruct((n_tiles, 8, META_ROWS), jnp.int32),
        compiler_params=_cparams(("parallel",)),
        name="moe_route",
    )(h2x)


def _moe_kernel(perm_ref, cg_ref, nact_ref, h_ref, wg_ref, wu_ref, wd_ref, y_ref, hs0, hs1, ys0, ys1):
    i = pl.program_id(0)
    s = pl.program_id(1)
    n_act = nact_ref[i]
    tile_base = i * META_ROWS
    hs = (hs0, hs1)
    ys = (ys0, ys1)

    def gather(chunk, dst):
        base = tile_base + chunk * MOE_CHUNK
        for r in range(MOE_CHUNK):
            src = jnp.minimum(perm_ref[base + r], MOE_TILE - 1)
            dst[r:r + 1, :] = h_ref[pl.ds(src, 1), :]

    def scatter(chunk, src):
        base = tile_base + chunk * MOE_CHUNK
        for r in range(MOE_CHUNK):
            y_ref[pl.ds(perm_ref[base + r], 1), :] = src[r:r + 1, :]

    def ffn(src, dst):
        hb = src[:, 0:D_MODEL].astype(BF16)
        cw = src[:, D_MODEL:H2X_W]
        hid = []
        for e in range(EXPERTS_PER_GROUP):
            a = jnp.dot(hb, wg_ref[e], preferred_element_type=F32)
            u = jnp.dot(hb, wu_ref[e], preferred_element_type=F32)
            hid.append((_silu(a) * u * cw[:, e:e + 1]).astype(BF16))
        dst[...] = jnp.dot(jnp.concatenate(hid, axis=1), wd_ref[...], preferred_element_type=F32)

    @pl.when(s == 0)
    def _():
        y_ref[MOE_TILE:MOE_TILE + 8, :] = jnp.zeros((8, D_MODEL), F32)
        ys1[...] = jnp.zeros_like(ys1)
        gather(0, hs0)

    for par in (0, 1):
        @pl.when((s < n_act) & (s % 2 == par))
        def _():
            gather(s + 1, hs[1 - par])
            ffn(hs[par], ys[par])
            scatter(jnp.maximum(s - 1, 0), ys[1 - par])

        @pl.when((s == n_act) & (s % 2 == par))
        def _():
            scatter(s - 1, ys[1 - par])


def _moe(h2x, perm, cgrp, nact, wg, wu, wd):
    n = h2x.shape[0]
    n_tiles = n // MOE_TILE

    def w_idx(i, s, perm_ref, cg_ref, nact_ref):
        return (jnp.minimum(cg_ref[i * MOE_STEPS + s], N_GROUPS - 1), 0, 0)

    def h_idx(i, s, perm_ref, cg_ref, nact_ref):
        done = (s >= jnp.maximum(nact_ref[i] - 1, 1)).astype(jnp.int32)
        return (jnp.minimum(i + done, n_tiles - 1), 0)

    return pl.pallas_call(
        _moe_kernel,
        grid_spec=pltpu.PrefetchScalarGridSpec(
            num_scalar_prefetch=3,
            grid=(n_tiles, MOE_STEPS),
            in_specs=[pl.BlockSpec((MOE_TILE, H2X_W), h_idx),
                      pl.BlockSpec((EXPERTS_PER_GROUP, D_MODEL, EXPERT_FF), w_idx),
                      pl.BlockSpec((EXPERTS_PER_GROUP, D_MODEL, EXPERT_FF), w_idx),
                      pl.BlockSpec((EXPERTS_PER_GROUP * EXPERT_FF, D_MODEL),
                                   lambda i, s, p, c, a: w_idx(i, s, p, c, a)[:2])],
            out_specs=pl.BlockSpec((None, MOE_TILE + 8, D_MODEL), lambda i, s, p, c, a: (i, 0, 0)),
            scratch_shapes=[pltpu.VMEM((MOE_CHUNK, H2X_W), F32), pltpu.VMEM((MOE_CHUNK, H2X_W), F32),
                            pltpu.VMEM((MOE_CHUNK, D_MODEL), F32), pltpu.VMEM((MOE_CHUNK, D_MODEL), F32)]),
        out_shape=jax.ShapeDtypeStruct((n_tiles, MOE_TILE + 8, D_MODEL), F32),
        compiler_params=_cparams(("parallel", "arbitrary")),
        name="moe_experts",
    )(perm, cgrp, nact, h2x, wg, wu, wd)


def _final_kernel(y_ref, x1_ref, mods_ref, fg_ref, o_ref):
    gate2 = mods_ref[:, 5 * D_MODEL:6 * D_MODEL]
    x2 = x1_ref[...] + gate2 * y_ref[...]
    o_ref[...] = x2 * lax.rsqrt(jnp.mean(x2 * x2, axis=-1, keepdims=True) + EPS) * fg_ref[...]


def _final(y, x1, mods3, mod_row0, mod_tokens, fg):
    n = x1.shape[0]
    tm = 512
    per_mod = mod_tokens // tm
    per_tile = MOE_TILE // tm
    return pl.pallas_call(
        _final_kernel,
        grid=(n // tm,),
        in_specs=[pl.BlockSpec((None, tm, D_MODEL), lambda j: (j // per_tile, j % per_tile, 0)),
                  pl.BlockSpec((tm, D_MODEL), lambda j: (j, 0)),
                  pl.BlockSpec((None, 1, 6 * D_MODEL), lambda j: (mod_row0 + j // per_mod, 0, 0)),
                  pl.BlockSpec((1, D_MODEL), lambda j: (0, 0))],
        out_specs=pl.BlockSpec((tm, D_MODEL), lambda j: (j, 0)),
        out_shape=jax.ShapeDtypeStruct((n, D_MODEL), F32),
        compiler_params=_cparams(("parallel",)),
        name="final_norm",
    )(y, x1, mods3, fg)


def _rope_tables(t):
    n_freq = QK_DIM // 4
    n_rows = t // GRID_W
    freqs = ROPE_BASE ** (-jnp.arange(n_freq, dtype=F32) / n_freq)
    ang_r = jnp.arange(n_rows, dtype=F32)[:, None] * freqs
    ang_c = jnp.arange(GRID_W, dtype=F32)[:, None] * freqs
    cr, sr, cc, sc = lax.optimization_barrier((jnp.cos(ang_r), jnp.sin(ang_r), jnp.cos(ang_c), jnp.sin(ang_c)))
    j = np.arange(LANES) % QK_DIM
    f_idx = j % n_freq
    by_row = (j < QK_DIM // 2)[None, None, :]
    first = ((j % (QK_DIM // 2)) < n_freq)[None, None, :]

    def table(r_small, c_small):
        return jnp.where(by_row, r_small[:, f_idx][:, None, :], c_small[:, f_idx][None, :, :])

    cos = table(cr, cc)
    sin = table(sr, sc)
    return (cos.reshape(t, LANES), jnp.where(first, -sin, 0.0).reshape(t, LANES),
            jnp.where(first, 0.0, sin).reshape(t, LANES))


def _layer(x, mods3, mod_row0, mod_tokens, rope_tabs, ctx_k, ctx_v, h0f, h0b, lw, layer):
    b, t, _ = x.shape
    n = b * t
    xf = x.reshape(n, D_MODEL)
    res = _inproj(xf, mods3, mod_row0, mod_tokens, t, lw["norm_mix_g"], lw["w_main"], lw["w_dt"], lw["alog"],
                  lw["dtb"], rope_tabs)
    q, kb, vt, z, xbc, cols, hrow = res[:7]
    if ctx_k is None:
        cache = None
        k3 = res[7].reshape(b, 1, t, ATT_HEADS, 2, LANES)[..., :QK_DIM]
        v3 = res[8].reshape(b, 1, t, ATT_HEADS, V_DIM)
    else:
        cache = (ctx_k.astype(BF16), jnp.swapaxes(ctx_v, 1, 2).astype(BF16))
        k3 = v3 = None
    lam0 = 0.8 - 0.6 * math.exp(-0.3 * layer)
    tq = 512 if t % 512 == 0 else 256
    tk = 512 if t % 512 == 0 else 256
    att = _attention(q.reshape(b, t, D_QK), kb.reshape(b, t, D_QK), vt, cache, lw["lamp"], lw["attn_subln_g"],
                     lam0, tq, tk, ATT_HEADS if t <= 512 else 1)
    ssm, hf, hb = _ssd(xbc, cols.reshape(b, t, LANES), hrow.reshape(b, t // CHUNK * HROWS, LANES),
                       z.reshape(b, t, D_SSM), h0f, h0b, lw["conv_w"], lw["conv_b"], lw["d_x"], lw["ssm_norm_g"])
    x1, h2x = _outproj(att.reshape(n, D_ATT), ssm.reshape(n, D_SSM), xf, mods3, mod_row0, mod_tokens,
                       lw["wo_att"], lw["wo_ssm"], lw["norm_ffn_g"], lw["w_router"], lw["b_router"])
    meta = _route(h2x)
    perm = meta[:, 0, :].reshape(-1)
    cgrp = meta[:, 1, :MOE_STEPS].reshape(-1)
    nact = meta[:, 2, 0]
    y = _moe(h2x, perm, cgrp, nact, lw["wg"], lw["wu"], lw["wd"])
    out = _final(y, x1, mods3, mod_row0, mod_tokens, lw["final_g"])
    return out.reshape(b, t, D_MODEL), k3, v3, hf, hb


def _pad_lanes(v, width=LANES):
    return jnp.pad(v, [(0, 0)] * (v.ndim - 1) + [(0, width - v.shape[-1])])


def kernel(x_prompt, x_sample, cache_k, cache_v, state_ssm_fwd, state_ssm_bwd, c, c_ctx, w_ada, b_ada, norm_mix_g, w_in, w_out, lambda_q1, lambda_k1, lambda_q2, lambda_k2, attn_subln_g, conv_w, conv_b, a_log_fwd, a_log_bwd, dt_bias_fwd, dt_bias_bwd, ssm_d, ssm_norm_g, norm_ffn_g, w_group_router, b_group_router, w_expert_router, b_expert_router, w_exp_gate, w_exp_up, w_exp_down, final_norm_g):
    depth = w_in.shape[0]
    assert depth == 1, "single trunk layer"
    bp, tp, _ = x_prompt.shape
    bs, ts, _ = x_sample.shape
    l = 0
    cond = jnp.concatenate([c_ctx[None], c], axis=0)
    condT = _pad_lanes(cond.T, 8)
    mods = _ada(condT, w_ada[l], b_ada[l][None])
    mods3 = mods.reshape(8, 1, 6 * D_MODEL)

    w_router = _pad_lanes(jnp.concatenate([w_group_router[l], w_expert_router[l]], axis=1))
    wr_hi = w_router.astype(BF16)
    wr_lo = (w_router - wr_hi.astype(F32)).astype(BF16)
    lw = dict(
        norm_mix_g=norm_mix_g[l][None],
        w_main=w_in[l].astype(BF16),
        w_dt=_pad_lanes(w_in[l][:, MAIN_COLS:]).astype(BF16),
        lamp=jnp.stack([lambda_q1[l], lambda_k1[l], lambda_q2[l], lambda_k2[l]]),
        attn_subln_g=attn_subln_g[l][None],
        conv_w=conv_w[l].reshape(CONV_W, N_COLBLK, LANES), conv_b=conv_b[l].reshape(N_COLBLK, LANES),
        alog=jnp.broadcast_to(jnp.concatenate([a_log_fwd[l], a_log_bwd[l]])[:, None], (2 * SSM_HEADS, CHUNK)),
        dtb=jnp.broadcast_to(jnp.concatenate([dt_bias_fwd[l], dt_bias_bwd[l]])[:, None], (2 * SSM_HEADS, CHUNK)),
        d_x=jnp.repeat(ssm_d[l], SSM_HEADDIM)[None], ssm_norm_g=ssm_norm_g[l][None],
        wo_att=w_out[l][:D_ATT].astype(BF16), wo_ssm=w_out[l][D_ATT:].astype(BF16),
        norm_ffn_g=norm_ffn_g[l][None],
        w_router=jnp.concatenate([wr_hi, wr_lo], axis=1),
        b_router=_pad_lanes(jnp.concatenate([b_group_router[l], b_expert_router[l]])[None]),
        wg=w_exp_gate[l].astype(BF16), wu=w_exp_up[l].astype(BF16), wd=w_exp_down[l].astype(BF16).reshape(N_EXPERTS * EXPERT_FF, D_MODEL),
        final_g=final_norm_g[None],
    )
    n_state = SSM_HEADS * SSM_HEADDIM
    zeros_state = jnp.zeros((bp, n_state, D_STATE), F32)
    yp, ck, cv, hf, hb = _layer(x_prompt, mods3, 0, bp * tp, None, None, None, zeros_state, zeros_state, lw, l)
    ys, _, _, _, _ = _layer(x_sample, mods3, 1, ts, _rope_tables(ts),
                            cache_k[:, l].reshape(bs, -1, D_QK), cache_v[:, l].reshape(bs, -1, D_ATT),
                            state_ssm_fwd[:, l].reshape(bs, n_state, D_STATE),
                            state_ssm_bwd[:, l].reshape(bs, n_state, D_STATE), lw, l)
    new_k, new_v = ck, cv
    new_hf = hf.reshape(bp, 1, SSM_HEADS, SSM_HEADDIM, D_STATE)
    new_hb = hb.reshape(bp, 1, SSM_HEADS, SSM_HEADDIM, D_STATE)
    return yp, ys, new_k, new_v, new_hf, new_hb
```

```python
import functools
import math

import numpy as np
import jax
import jax.numpy as jnp
from jax import lax
from jax.experimental import pallas as pl
from jax.experimental.pallas import tpu as pltpu

D_MODEL = 1024
GRID_W = 64
ATT_HEADS = 4
QK_DIM = 64
V_DIM = 128
D_QK = 512
D_ATT = 512
ROPE_BASE = 10000.0
D_SSM = 512
SSM_HEADDIM = 64
SSM_HEADS = 8
SSM_GROUPS = 2
D_STATE = 128
CONV_W = 5
CHUNK = 128
XBC_DIM = 1024
N_GROUPS = 4
EXPERTS_PER_GROUP = 4
N_EXPERTS = 16
EXPERT_FF = 256
EPS = 1e-6
MAIN_COLS = 2 * D_QK + D_ATT + D_SSM + XBC_DIM
H2X_W = D_MODEL + 128
MOE_TILE = 2048
MOE_CHUNK = 256
MOE_SLOTS = MOE_TILE // MOE_CHUNK + N_GROUPS
MOE_ROWS = MOE_SLOTS * MOE_CHUNK
MOE_STEPS = MOE_SLOTS + 1
META_ROWS = MOE_STEPS * MOE_CHUNK
COL_CUM, COL_DT, COL_ECUM, COL_TOEND = 0, 16, 32, 48
HROWS = 32
LANES = 128
HALO = 8
VMEM_LIMIT = 56 * 1024 * 1024

LOG2E = math.log2(math.e)
SUM_ROWS = 16
F32 = jnp.float32
BF16 = jnp.bfloat16


def _cparams(sem):
    return pltpu.CompilerParams(dimension_semantics=sem, vmem_limit_bytes=VMEM_LIMIT)


def _sigmoid(x):
    return 1.0 / (1.0 + jnp.exp(-x))


def _silu(x):
    return x * _sigmoid(x)


def _ada_kernel(condT_ref, w_ref, b_ref, o_ref):
    s = _silu(condT_ref[...])
    w = w_ref[...]
    b = b_ref[...]
    o_ref[...] = jnp.zeros_like(o_ref)
    for r in range(3):
        o_ref[r:r + 1, :] = jnp.sum(w * s[:, r:r + 1], axis=0, keepdims=True) + b


def _ada(condT, w_ada, b_ada):
    bn = 1024
    n = w_ada.shape[1]
    return pl.pallas_call(
        _ada_kernel,
        grid=(n // bn,),
        in_specs=[pl.BlockSpec((D_MODEL, 8), lambda j: (0, 0)),
                  pl.BlockSpec((D_MODEL, bn), lambda j: (0, j)),
                  pl.BlockSpec((1, bn), lambda j: (0, j))],
        out_specs=pl.BlockSpec((8, bn), lambda j: (0, j)),
        out_shape=jax.ShapeDtypeStruct((8, n), F32),
        compiler_params=_cparams(("arbitrary",)),
        name="ada",
    )(condT, w_ada, b_ada)


def _split3(x):
    hi = x.astype(BF16)
    r1 = x - hi.astype(F32)
    mid = r1.astype(BF16)
    lo = (r1 - mid.astype(F32)).astype(BF16)
    return jnp.concatenate([hi, mid, lo], axis=-1)


def _scan_matrices():
    t = np.arange(CHUNK)
    pre = (t[:, None] <= t[None, :]).astype(np.float32)
    suf = (t[:, None] >= t[None, :]).astype(np.float32)
    return jnp.asarray(np.stack([np.concatenate([pre] * 3, axis=0), np.concatenate([suf] * 3, axis=0)]), BF16)


def _head_scalars(dt_raw, alog_ref, dtb_ref, scan_ref):
    nh2 = 2 * SSM_HEADS
    xv = dt_raw.T[0:nh2, :] + dtb_ref[...]
    dt = jnp.maximum(xv, 0.0) + jnp.log(1.0 + jnp.exp(-jnp.abs(xv)))
    la3 = _split3(dt * (-jnp.exp(alog_ref[...])))
    fwd = lax.broadcasted_iota(jnp.int32, (nh2, CHUNK), 0) < SSM_HEADS
    cum = jnp.where(fwd, jnp.dot(la3, scan_ref[0], preferred_element_type=F32),
                    jnp.dot(la3, scan_ref[1], preferred_element_type=F32))
    cum_end = jnp.where(fwd, cum[:, CHUNK - 1:CHUNK], cum[:, 0:1])
    packed = jnp.concatenate([cum, dt, jnp.exp(cum), jnp.exp(cum_end - cum),
                              jnp.zeros((LANES - 4 * nh2, CHUNK), F32)], axis=0)
    return packed.T, jnp.concatenate([cum, jnp.exp(cum_end)], axis=0)


def _inproj_kernel(rope, x_ref, mods_ref, g_ref, w_ref, wdt_ref, alog_ref, dtb_ref, scan_ref, *rest):
    if rope:
        cos_ref, sa_ref, sb_ref, q_ref, kb_ref, vt_ref, z_ref, xbc_ref, cols_ref, hrow_ref = rest
    else:
        q_ref, kb_ref, vt_ref, z_ref, xbc_ref, cols_ref, hrow_ref, kf_ref, vf_ref = rest
    x = x_ref[...]
    shift = mods_ref[:, 0:D_MODEL]
    scale = mods_ref[:, D_MODEL:2 * D_MODEL]
    y = x * lax.rsqrt(jnp.mean(x * x, axis=-1, keepdims=True) + EPS) * g_ref[...]
    h = (y * (1.0 + scale) + shift).astype(BF16)
    dt_raw = jnp.dot(h, wdt_ref[...], preferred_element_type=F32)
    for ci in range(x.shape[0] // CHUNK):
        cols, hrow = _head_scalars(dt_raw[ci * CHUNK:(ci + 1) * CHUNK, :], alog_ref, dtb_ref, scan_ref)
        cols_ref[ci * CHUNK:(ci + 1) * CHUNK, :] = cols
        hrow_ref[ci * HROWS:(ci + 1) * HROWS, :] = hrow
    r = jnp.dot(h, w_ref[...], preferred_element_type=F32)
    q = r[:, 0:D_QK]
    k = r[:, D_QK:2 * D_QK]
    if rope:
        cos = cos_ref[...]
        sa = sa_ref[...]
        sb = sb_ref[...]

        def rot(t):
            parts = []
            for hh in range(ATT_HEADS):
                th = t[:, hh * LANES:(hh + 1) * LANES]
                parts.append(th * cos + pltpu.roll(th, LANES - 16, 1) * sa + pltpu.roll(th, 16, 1) * sb)
            return jnp.concatenate(parts, axis=1)

        q = rot(q)
        k = rot(k)
    v = r[:, 2 * D_QK:2 * D_QK + D_ATT]
    q_ref[...] = q
    kb_ref[...] = k.astype(BF16)
    vt_ref[...] = v.T.astype(BF16)
    if not rope:
        tm = k.shape[0]
        for hh in range(ATT_HEADS):
            k_h = k[:, hh * LANES:(hh + 1) * LANES]
            vf_ref[pl.ds(hh, tm, stride=ATT_HEADS), :] = v[:, hh * LANES:(hh + 1) * LANES]
            kf_ref[pl.ds(2 * hh, tm, stride=2 * ATT_HEADS), :] = k_h
            kf_ref[pl.ds(2 * hh + 1, tm, stride=2 * ATT_HEADS), :] = pltpu.roll(k_h, QK_DIM, 1)
    z_ref[...] = r[:, 2 * D_QK + D_ATT:2 * D_QK + D_ATT + D_SSM]
    x0 = 2 * D_QK + D_ATT + D_SSM
    for cb in range(XBC_DIM // LANES):
        xbc_ref[cb] = r[:, x0 + cb * LANES:x0 + (cb + 1) * LANES]


def _inproj(x, mods3, mod_row0, mod_tokens, seq_len, g, w_main, w_dt, alog, dtb, rope_tabs):
    n = x.shape[0]
    tm = 512 if seq_len % 512 == 0 else 256
    per_seq = seq_len // tm
    per_mod = mod_tokens // tm
    rope = rope_tabs is not None
    in_specs = [pl.BlockSpec((tm, D_MODEL), lambda i: (i, 0)),
                pl.BlockSpec((None, 1, 6 * D_MODEL), lambda i: (mod_row0 + i // per_mod, 0, 0)),
                pl.BlockSpec((1, D_MODEL), lambda i: (0, 0)),
                pl.BlockSpec((D_MODEL, MAIN_COLS), lambda i: (0, 0)),
                pl.BlockSpec((D_MODEL, LANES), lambda i: (0, 0)),
                pl.BlockSpec((2 * SSM_HEADS, CHUNK), lambda i: (0, 0)),
                pl.BlockSpec((2 * SSM_HEADS, CHUNK), lambda i: (0, 0)),
                pl.BlockSpec((2, 3 * CHUNK, CHUNK), lambda i: (0, 0, 0))]
    args = [x, mods3, g, w_main, w_dt, alog, dtb, _scan_matrices()]
    if rope:
        tab_spec = pl.BlockSpec((tm, LANES), lambda i: (i % per_seq, 0))
        in_specs += [tab_spec] * 3
        args += list(rope_tabs)
    def rows(wd, dtype=F32):
        return pl.BlockSpec((tm, wd), lambda i: (i, 0)), jax.ShapeDtypeStruct((n, wd), dtype)

    hr = tm // CHUNK * HROWS
    outs = [rows(D_QK), rows(D_QK, BF16),
            (pl.BlockSpec((None, D_ATT, tm), lambda i: (i // per_seq, 0, i % per_seq)),
             jax.ShapeDtypeStruct((n // seq_len, D_ATT, seq_len), BF16)),
            rows(D_SSM),
            (pl.BlockSpec((None, XBC_DIM // LANES, tm, LANES), lambda i: (i // per_seq, 0, i % per_seq, 0)),
             jax.ShapeDtypeStruct((n // seq_len, XBC_DIM // LANES, seq_len, LANES), F32)),
            rows(LANES),
            (pl.BlockSpec((hr, LANES), lambda i: (i, 0)), jax.ShapeDtypeStruct((n // CHUNK * HROWS, LANES), F32))]
    if not rope:
        outs += [(pl.BlockSpec((tm * 2 * ATT_HEADS, LANES), lambda i: (i, 0)),
                  jax.ShapeDtypeStruct((n * 2 * ATT_HEADS, LANES), F32)),
                 (pl.BlockSpec((tm * ATT_HEADS, LANES), lambda i: (i, 0)),
                  jax.ShapeDtypeStruct((n * ATT_HEADS, LANES), F32))]
    return pl.pallas_call(
        functools.partial(_inproj_kernel, rope),
        grid=(n // tm,),
        in_specs=in_specs,
        out_specs=[o[0] for o in outs],
        out_shape=[o[1] for o in outs],
        compiler_params=_cparams(("parallel",)),
        name="inproj_rope" if rope else "inproj",
    )(*args)


def _attn_kernel(tk, lam0, has_cache, q_ref, k_ref, vt_ref, *rest):
    if has_cache:
        ck_ref, cvt_ref, lamp_ref, g_ref, o_ref = rest
    else:
        lamp_ref, g_ref, o_ref = rest
    tq = q_ref.shape[0]
    lp = lamp_ref[...]
    lam = (jnp.exp(jnp.sum(lp[0:1] * lp[1:2], axis=-1, keepdims=True))
           - jnp.exp(jnp.sum(lp[2:3] * lp[3:4], axis=-1, keepdims=True)) + lam0)
    ones_rows = jnp.ones((SUM_ROWS, tk), BF16)
    for hh in range(q_ref.shape[1] // LANES):
        hs = slice(hh * LANES, (hh + 1) * LANES)
        chunks = [(k_ref, vt_ref, c * tk) for c in range(k_ref.shape[0] // tk)]
        if has_cache:
            chunks += [(ck_ref, cvt_ref, c * tk) for c in range(ck_ref.shape[0] // tk)]
        q = q_ref[:, hs] * (QK_DIM ** -0.5 * LOG2E)
        lane = lax.broadcasted_iota(jnp.int32, q.shape, 1)
        qq_t = jnp.concatenate([jnp.where(lane < QK_DIM, q, 0.0), jnp.where(lane >= QK_DIM, q, 0.0)],
                               axis=0).T.astype(BF16)

        def scores(chunk):
            kr, _, start = chunk
            return jnp.dot(kr[start:start + tk, hs], qq_t, preferred_element_type=F32)

        def update(s, chunk, m, acc):
            _, vr, start = chunk
            m_new = jnp.maximum(m, jnp.max(s, axis=0, keepdims=True))
            alpha = jnp.exp2(m - m_new)
            p = jnp.exp2(s - m_new).astype(BF16)
            v_ext = jnp.concatenate([vr[hs, start:start + tk], ones_rows], axis=0)
            acc = alpha * acc + jnp.dot(v_ext, p, preferred_element_type=F32)
            return m_new, acc

        m = jnp.full((1, 2 * tq), -jnp.inf, F32)
        acc = jnp.zeros((V_DIM + SUM_ROWS, 2 * tq), F32)
        s = scores(chunks[0])
        for c, chunk in enumerate(chunks):
            s_next = scores(chunks[c + 1]) if c + 1 < len(chunks) else None
            m, acc = update(s, chunk, m, acc)
            s = s_next
        o = acc[0:V_DIM] / acc[V_DIM:V_DIM + 1]
        o = (o[:, 0:tq] - lam * o[:, tq:2 * tq]).T
        o = o * lax.rsqrt(jnp.mean(o * o, axis=-1, keepdims=True) + EPS)
        o_ref[:, hs] = o * g_ref[...] * (1.0 - lam0)


def _attention(q, k, vt, cache, lamp, g, lam0, tq, tk, nh):
    b, t, _ = q.shape
    wd = nh * LANES

    def kv_specs(length):
        return [pl.BlockSpec((None, length, wd), lambda bi, h, i: (bi, 0, h)),
                pl.BlockSpec((None, wd, length), lambda bi, h, i: (bi, h, 0))]

    in_specs = [pl.BlockSpec((None, tq, wd), lambda bi, h, i: (bi, i, h))] + kv_specs(t)
    args = [q, k, vt]
    if cache is not None:
        assert cache[0].shape[1] % tk == 0
        in_specs += kv_specs(cache[0].shape[1])
        args += list(cache)
    in_specs += [pl.BlockSpec((4, QK_DIM), lambda bi, h, i: (0, 0)),
                 pl.BlockSpec((1, V_DIM), lambda bi, h, i: (0, 0))]
    return pl.pallas_call(
        functools.partial(_attn_kernel, tk, lam0, cache is not None),
        grid=(b, ATT_HEADS // nh, t // tq),
        in_specs=in_specs,
        out_specs=pl.BlockSpec((None, tq, wd), lambda bi, h, i: (bi, i, h)),
        out_shape=jax.ShapeDtypeStruct((b, t, D_ATT), F32),
        compiler_params=_cparams(("parallel", "parallel", "arbitrary")),
        name="diff_attn",
    )(*args, lamp, g)


N_COLBLK = XBC_DIM // LANES
def _expand_matrices():
    out = []
    for d in range(2):
        e = np.zeros((LANES, 3 * D_SSM), np.float32)
        for blk, lane0 in enumerate((COL_DT, COL_ECUM, COL_TOEND)):
            for h in range(SSM_HEADS):
                e[lane0 + d * SSM_HEADS + h,
                  blk * D_SSM + h * SSM_HEADDIM:blk * D_SSM + (h + 1) * SSM_HEADDIM] = 1.0
        out.append(np.concatenate([e, e, e], axis=0))
    return jnp.asarray(np.stack(out), BF16)


def _ssd_chunk(reverse, xs, bm, cm, state, hrow, cols, e_ref):
    d0 = SSM_HEADS if reverse else 0
    cum = hrow[d0:d0 + SSM_HEADS, :]
    dec = hrow[2 * SSM_HEADS + d0:2 * SSM_HEADS + d0 + SSM_HEADS, 0:1]
    row = lax.broadcasted_iota(jnp.int32, (CHUNK, LANES), 0)
    lane = lax.broadcasted_iota(jnp.int32, (CHUNK, LANES), 1)
    causal = (row <= lane) if reverse else (row >= lane)
    lane_g = lax.broadcasted_iota(jnp.int32, (CHUNK, 2 * LANES), 1)
    spread = jnp.dot(_split3(cols), e_ref[1 if reverse else 0], preferred_element_type=F32)
    xd = xs * spread[:, 0:D_SSM]
    xdw = (xd * spread[:, 2 * D_SSM:3 * D_SSM]).astype(BF16)
    xd = xd.astype(BF16)
    ecum_x = spread[:, D_SSM:2 * D_SSM]
    rep = SSM_HEADS // SSM_GROUPS
    y_parts = []
    new_state = []
    for g in range(SSM_GROUPS):
        bg = bm[:, g * D_STATE:(g + 1) * D_STATE]
        cg = cm[:, g * D_STATE:(g + 1) * D_STATE]
        cbt = lax.dot_general(cg, bg, (((1,), (1,)), ((), ())), preferred_element_type=F32)
        rows = slice(g * rep * SSM_HEADDIM, (g + 1) * rep * SSM_HEADDIM)
        st_g = state[rows, :]
        y_off = lax.dot_general(cg, st_g.astype(BF16), (((1,), (1,)), ((), ())),
                                preferred_element_type=F32)
        cst = lax.dot_general(xdw[:, rows], bg, (((0,), (0,)), ((), ())), preferred_element_type=F32)
        xd_g = xd[:, rows]
        scs = []
        blocks = []
        for hh in range(rep):
            h = g * rep + hh
            seg = cols[:, COL_CUM + d0 + h:COL_CUM + d0 + h + 1] - cum[h:h + 1, :]
            decay = jnp.exp(jnp.where(causal, seg, -jnp.inf))
            scs.append((cbt * decay).astype(BF16))
            blocks.append(jnp.where(lane_g // SSM_HEADDIM == hh, xd_g, jnp.zeros_like(xd_g)))
        y_diag = jnp.dot(jnp.concatenate(scs, axis=1), jnp.concatenate(blocks, axis=0),
                         preferred_element_type=F32)
        dec_rows = jnp.concatenate(
            [jnp.broadcast_to(dec[g * rep + hh:g * rep + hh + 1, :], (SSM_HEADDIM, D_STATE)) for hh in range(rep)],
            axis=0)
        new_state.append(st_g * dec_rows + cst)
        y_parts.append(y_diag + y_off * ecum_x[:, rows])
    return jnp.concatenate(y_parts, axis=1), jnp.concatenate(new_state, axis=0)


def _ssd_kernel(nc, xc_ref, xp_ref, xn_ref, cols_ref, hrow_ref, z_ref, h0f_ref, h0b_ref, cw_ref, cb_ref,
                d_ref, ng_ref, e_ref, y_ref, hf_ref, hb_ref, state_ref, slab_ref, xs_ref, bc_ref, yf_ref):
    ps = pl.program_id(1)
    c = pl.program_id(2)
    n_bc = SSM_GROUPS * D_STATE

    @pl.when((ps == 0) & (c == 0))
    def _():
        state_ref[...] = h0f_ref[...]

    @pl.when(ps == 0)
    def _():
        t0 = pl.multiple_of(c * CHUNK, CHUNK)
        for bb in range(state_ref.shape[0]):
            slab_ref[bb, :, 0:HALO, :] = jnp.where(c > 0, xp_ref[bb], 0.0)
            slab_ref[bb, :, HALO:HALO + CHUNK, :] = xc_ref[bb]
            slab_ref[bb, :, HALO + CHUNK:2 * HALO + CHUNK, :] = jnp.where(c < nc - 1, xn_ref[bb], 0.0)
            blocks = []
            for cb in range(N_COLBLK):
                conv = cb_ref[cb:cb + 1, :]
                for kk in range(CONV_W):
                    off = HALO - CONV_W // 2 + kk
                    conv = conv + slab_ref[bb, cb, off:off + CHUNK, :] * cw_ref[kk, cb:cb + 1, :]
                blocks.append(_silu(conv))
            xs = jnp.concatenate(blocks[0:D_SSM // LANES], axis=1)
            bc = jnp.concatenate(blocks[D_SSM // LANES:], axis=1).astype(BF16)
            y, state = _ssd_chunk(False, xs, bc[:, 0:n_bc], bc[:, n_bc:], state_ref[bb], hrow_ref[bb],
                                  cols_ref[bb], e_ref)
            xs_ref[bb, pl.ds(t0, CHUNK), :] = xs
            bc_ref[bb, pl.ds(t0, CHUNK), :] = bc
            yf_ref[bb, pl.ds(t0, CHUNK), :] = y
            state_ref[bb] = state

    @pl.when((ps == 0) & (c == nc - 1))
    def _():
        hf_ref[...] = state_ref[...]
        state_ref[...] = h0b_ref[...]

    @pl.when(ps == 1)
    def _():
        t0 = pl.multiple_of((nc - 1 - c) * CHUNK, CHUNK)
        for bb in range(state_ref.shape[0]):
            xs = xs_ref[bb, pl.ds(t0, CHUNK), :]
            bc = bc_ref[bb, pl.ds(t0, CHUNK), :]
            y, state = _ssd_chunk(True, xs, bc[:, 0:n_bc], bc[:, n_bc:], state_ref[bb], hrow_ref[bb],
                                  cols_ref[bb], e_ref)
            state_ref[bb] = state
            y = y + yf_ref[bb, pl.ds(t0, CHUNK), :] + xs * d_ref[...]
            y = y * _silu(z_ref[bb])
            y = y * lax.rsqrt(jnp.mean(y * y, axis=-1, keepdims=True) + EPS)
            y_ref[bb] = y * ng_ref[...]

    @pl.when((ps == 1) & (c == nc - 1))
    def _():
        hb_ref[...] = state_ref[...]


def _ssd(xbc, cols, hrow, z, h0f, h0b, conv_w, conv_b, d_x, norm_g):
    b, _, l, _ = xbc.shape
    nb = 2
    assert b % nb == 0
    nc = l // CHUNK
    hb = CHUNK // HALO
    n_state = SSM_HEADS * SSM_HEADDIM
    last = nc - 1

    def fwd_chunk(ps, ci):
        return jnp.where(ps == 0, ci, last)

    def any_chunk(ps, ci):
        return jnp.where(ps == 0, ci, last - ci)

    def bwd_chunk(ps, ci):
        return jnp.where(ps == 0, last, last - ci)

    const2 = lambda bi, ps, ci: (0, 0)
    state_spec = pl.BlockSpec((nb, n_state, D_STATE), lambda bi, ps, ci: (bi, 0, 0))
    in_specs = [pl.BlockSpec((nb, N_COLBLK, CHUNK, LANES), lambda bi, ps, ci: (bi, 0, fwd_chunk(ps, ci), 0)),
                pl.BlockSpec((nb, N_COLBLK, HALO, LANES),
                             lambda bi, ps, ci: (bi, 0, jnp.maximum(fwd_chunk(ps, ci) * hb - 1, 0), 0)),
                pl.BlockSpec((nb, N_COLBLK, HALO, LANES),
                             lambda bi, ps, ci: (bi, 0, jnp.minimum((fwd_chunk(ps, ci) + 1) * hb, l // HALO - 1), 0)),
                pl.BlockSpec((nb, CHUNK, LANES), lambda bi, ps, ci: (bi, any_chunk(ps, ci), 0)),
                pl.BlockSpec((nb, HROWS, LANES), lambda bi, ps, ci: (bi, any_chunk(ps, ci), 0)),
                pl.BlockSpec((nb, CHUNK, D_SSM), lambda bi, ps, ci: (bi, bwd_chunk(ps, ci), 0)),
                state_spec, state_spec,
                pl.BlockSpec((CONV_W, N_COLBLK, LANES), lambda bi, ps, ci: (0, 0, 0)),
                pl.BlockSpec((N_COLBLK, LANES), const2),
                pl.BlockSpec((1, D_SSM), const2),
                pl.BlockSpec((1, D_SSM), const2),
                pl.BlockSpec((2, 3 * LANES, 3 * D_SSM), lambda bi, ps, ci: (0, 0, 0))]
    return pl.pallas_call(
        functools.partial(_ssd_kernel, nc),
        grid=(b // nb, 2, nc),
        in_specs=in_specs,
        out_specs=[pl.BlockSpec((nb, CHUNK, D_SSM), lambda bi, ps, ci: (bi, bwd_chunk(ps, ci), 0)),
                   state_spec, state_spec],
        out_shape=[jax.ShapeDtypeStruct((b, l, D_SSM), F32),
                   jax.ShapeDtypeStruct((b, n_state, D_STATE), F32),
                   jax.ShapeDtypeStruct((b, n_state, D_STATE), F32)],
        scratch_shapes=[pltpu.VMEM((nb, n_state, D_STATE), F32),
                        pltpu.VMEM((nb, N_COLBLK, CHUNK + 2 * HALO, LANES), F32),
                        pltpu.VMEM((nb, l, D_SSM), F32),
                        pltpu.VMEM((nb, l, 2 * SSM_GROUPS * D_STATE), BF16),
                        pltpu.VMEM((nb, l, D_SSM), F32)],
        compiler_params=_cparams(("parallel", "arbitrary", "arbitrary")),
        name="ssd",
    )(xbc, xbc, xbc, cols, hrow, z, h0f, h0b, conv_w, conv_b, d_x, norm_g, _expand_matrices())


def _outproj_kernel(att_ref, ssm_ref, x_ref, mods_ref, wa_ref, ws_ref, g_ref, wr_ref, br_ref,
                    x1_ref, h2x_ref):
    mix = (jnp.dot(att_ref[...].astype(BF16), wa_ref[...], preferred_element_type=F32)
           + jnp.dot(ssm_ref[...].astype(BF16), ws_ref[...], preferred_element_type=F32))
    gate1 = mods_ref[:, 2 * D_MODEL:3 * D_MODEL]
    shift2 = mods_ref[:, 3 * D_MODEL:4 * D_MODEL]
    scale2 = mods_ref[:, 4 * D_MODEL:5 * D_MODEL]
    x1 = x_ref[...] + gate1 * mix
    x1_ref[...] = x1
    y = x1 * lax.rsqrt(jnp.mean(x1 * x1, axis=-1, keepdims=True) + EPS) * g_ref[...]
    h2 = y * (1.0 + scale2) + shift2
    h_hi = h2.astype(BF16)
    h2x_ref[:, 0:D_MODEL] = h2
    h_lo = (h2 - h_hi.astype(F32)).astype(BF16)
    both = jnp.dot(h_hi, wr_ref[...], preferred_element_type=F32)
    logits = (both[:, 0:LANES] + both[:, LANES:2 * LANES]
              + jnp.dot(h_lo, wr_ref[:, 0:LANES], preferred_element_type=F32)) + br_ref[...]
    lane = lax.broadcasted_iota(jnp.int32, logits.shape, 1).astype(F32)
    neg = -jnp.inf
    big = float(1 << 20)
    is_g = lane < N_GROUPS
    gl = jnp.where(is_g, logits, neg)
    gmax = jnp.max(gl, axis=-1, keepdims=True)
    g_idx = jnp.min(jnp.where(gl == gmax, lane, big), axis=-1, keepdims=True)
    p_g = 1.0 / jnp.sum(jnp.where(is_g, jnp.exp(gl - gmax), 0.0), axis=-1, keepdims=True)
    e_lo = N_GROUPS + g_idx * EXPERTS_PER_GROUP
    in_grp = (lane >= e_lo) & (lane < e_lo + EXPERTS_PER_GROUP)
    el = jnp.where(in_grp, logits, neg)
    m1 = jnp.max(el, axis=-1, keepdims=True)
    i1 = jnp.min(jnp.where(el == m1, lane, big), axis=-1, keepdims=True)
    el2 = jnp.where(lane == i1, neg, el)
    m2 = jnp.max(el2, axis=-1, keepdims=True)
    i2 = jnp.min(jnp.where(el2 == m2, lane, big), axis=-1, keepdims=True)
    e2 = jnp.exp(m2 - m1)
    w1 = p_g / (1.0 + e2)
    w2 = p_g * e2 / (1.0 + e2)
    slab = (jnp.where(lane == i1 - e_lo, w1, 0.0) + jnp.where(lane == i2 - e_lo, w2, 0.0)
            + jnp.where(lane == EXPERTS_PER_GROUP, g_idx, 0.0))
    h2x_ref[:, D_MODEL:H2X_W] = slab


def _outproj(att, ssm, x, mods3, mod_row0, mod_tokens, wo_att, wo_ssm, g, w_router, b_router):
    n = x.shape[0]
    tm = 512
    per_mod = mod_tokens // tm
    return pl.pallas_call(
        _outproj_kernel,
        grid=(n // tm,),
        in_specs=[pl.BlockSpec((tm, D_ATT), lambda i: (i, 0)),
                  pl.BlockSpec((tm, D_SSM), lambda i: (i, 0)),
                  pl.BlockSpec((tm, D_MODEL), lambda i: (i, 0)),
                  pl.BlockSpec((None, 1, 6 * D_MODEL), lambda i: (mod_row0 + i // per_mod, 0, 0)),
                  pl.BlockSpec((D_ATT, D_MODEL), lambda i: (0, 0)),
                  pl.BlockSpec((D_SSM, D_MODEL), lambda i: (0, 0)),
                  pl.BlockSpec((1, D_MODEL), lambda i: (0, 0)),
                  pl.BlockSpec((D_MODEL, 2 * LANES), lambda i: (0, 0)),
                  pl.BlockSpec((1, LANES), lambda i: (0, 0))],
        out_specs=[pl.BlockSpec((tm, D_MODEL), lambda i: (i, 0)),
                   pl.BlockSpec((tm, H2X_W), lambda i: (i, 0))],
        out_shape=[jax.ShapeDtypeStruct((n, D_MODEL), F32),
                   jax.ShapeDtypeStruct((n, H2X_W), F32)],
        compiler_params=_cparams(("parallel",)),
        name="outproj_router",
    )(att, ssm, x, mods3, wo_att, wo_ssm, g, w_router, b_router)


def _route_kernel(slab_ref, meta_ref):
    t_n = MOE_TILE
    blk = LANES
    slab = slab_ref[...]
    lane = lax.broadcasted_iota(jnp.int32, (t_n, LANES), 1)
    gcol = jnp.sum(jnp.where(lane == EXPERTS_PER_GROUP, slab, 0.0), axis=-1, keepdims=True)
    member = (lane.astype(F32) == gcol) & (lane < N_GROUPS)
    a = jnp.where(member, 1.0, 0.0).astype(BF16)
    r_i = lax.broadcasted_iota(jnp.int32, (blk, blk), 0)
    c_i = lax.broadcasted_iota(jnp.int32, (blk, blk), 1)
    lower = jnp.where(c_i < r_i, 1.0, 0.0).astype(BF16)
    upper = jnp.where(r_i < c_i, 1.0, 0.0).astype(BF16)
    offs = jnp.zeros((1, LANES), F32)
    ranks = []
    for b in range(t_n // blk):
        ab = a[b * blk:(b + 1) * blk]
        rb = jnp.dot(lower, ab, preferred_element_type=F32)
        ranks.append(rb + offs)
        offs = offs + rb[blk - 1:blk] + ab[blk - 1:blk].astype(F32)
    rank = jnp.concatenate(ranks, axis=0)
    n_chunk = jnp.floor((offs + (MOE_CHUNK - 1)) * (1.0 / MOE_CHUNK))
    start = jnp.dot(jnp.broadcast_to(n_chunk, (8, LANES)).astype(BF16), upper,
                    preferred_element_type=F32)[0:1]
    end = start + n_chunk
    dest = jnp.sum(jnp.where(member, start * MOE_CHUNK + rank, 0.0), axis=-1, keepdims=True)
    tok = lax.broadcasted_iota(jnp.int32, (t_n, LANES), 0)
    digits = jnp.where(lane == 0, (tok // blk).astype(F32),
                       jnp.where(lane == 1, (tok % blk).astype(F32), jnp.where(lane == 2, 1.0, 0.0))).astype(BF16)
    sw = 512
    pieces = []
    for sc in range(MOE_ROWS // sw):
        s_id = (lax.broadcasted_iota(jnp.int32, (t_n, sw), 1) + sc * sw).astype(F32)
        hit = jnp.where(dest == s_id, 1.0, 0.0).astype(BF16)
        r = lax.dot_general(digits, hit, (((0,), (0,)), ((), ())), preferred_element_type=F32)
        tok_of = r[0:1] * blk + r[1:2]
        pieces.append(jnp.where(r[2:3] > 0.5, tok_of, float(t_n)))
    pieces.append(jnp.full((1, META_ROWS - MOE_ROWS), float(t_n), F32))
    perm = jnp.concatenate(pieces, axis=1)
    slot = lax.broadcasted_iota(jnp.int32, (1, META_ROWS), 1).astype(F32)
    lane1 = lax.broadcasted_iota(jnp.int32, (1, LANES), 1)
    cg = jnp.zeros((1, META_ROWS), F32)
    for g in range(N_GROUPS):
        end_g = jnp.sum(jnp.where(lane1 == g, end, 0.0), axis=-1, keepdims=True)
        cg = cg + jnp.where(slot >= end_g, 1.0, 0.0)
    n_act = jnp.broadcast_to(end_g, (1, META_ROWS))
    meta_ref[...] = jnp.concatenate([perm, cg, n_act, jnp.zeros((5, META_ROWS), F32)], axis=0).astype(jnp.int32)


def _route(h2x):
    n = h2x.shape[0]
    n_tiles = n // MOE_TILE
    return pl.pallas_call(
        _route_kernel,
        grid=(n_tiles,),
        in_specs=[pl.BlockSpec((MOE_TILE, LANES), lambda i: (i, D_MODEL // LANES))],
        out_specs=pl.BlockSpec((None, 8, META_ROWS), lambda i: (i, 0, 0)),
        out_shape=jax.ShapeDtypeStruct((n_tiles, 8, META_ROWS), jnp.int32),
        compiler_params=_cparams(("parallel",)),
        name="moe_route",
    )(h2x)


def _moe_kernel(perm_ref, cg_ref, nact_ref, run_ref, h_ref, wg_hbm, wu_hbm, wd_hbm, y_ref,
                hs0, hs1, ys0, ys1, wg_buf, wu_buf, wd_buf, w_sem):
    i = pl.program_id(0)
    s = pl.program_id(1)
    n_act = nact_ref[i]
    tile_base = i * META_ROWS
    hs = (hs0, hs1)
    ys = (ys0, ys1)
    n_steps = pl.num_programs(0) * MOE_STEPS
    f = i * MOE_STEPS + s
    slot = run_ref[n_steps + f]

    def weight_copies(g, sl):
        e0 = g * EXPERTS_PER_GROUP
        r0 = g * (EXPERTS_PER_GROUP * EXPERT_FF)
        return (pltpu.make_async_copy(wg_hbm.at[pl.ds(e0, EXPERTS_PER_GROUP)], wg_buf.at[sl], w_sem.at[sl, 0]),
                pltpu.make_async_copy(wu_hbm.at[pl.ds(e0, EXPERTS_PER_GROUP)], wu_buf.at[sl], w_sem.at[sl, 1]),
                pltpu.make_async_copy(wd_hbm.at[pl.ds(r0, EXPERTS_PER_GROUP * EXPERT_FF)], wd_buf.at[sl],
                                      w_sem.at[sl, 2]))

    @pl.when(run_ref[f] == 1)
    def _():
        g = jnp.minimum(cg_ref[f], N_GROUPS - 1)

        @pl.when(run_ref[3 * n_steps + f] == 1)
        def _():
            for cp in weight_copies(g, slot):
                cp.start()

        for cp in weight_copies(g, slot):
            cp.wait()
        nxt = run_ref[2 * n_steps + f]

        @pl.when(nxt >= 0)
        def _():
            for cp in weight_copies(nxt, 1 - slot):
                cp.start()

    def gather(chunk, dst):
        base = tile_base + chunk * MOE_CHUNK
        for r in range(MOE_CHUNK):
            src = jnp.minimum(perm_ref[base + r], MOE_TILE - 1)
            dst[r:r + 1, :] = h_ref[pl.ds(src, 1), :]

    def scatter(chunk, src):
        base = tile_base + chunk * MOE_CHUNK
        for r in range(MOE_CHUNK):
            y_ref[pl.ds(perm_ref[base + r], 1), :] = src[r:r + 1, :]

    def ffn(src, dst):
        hb = src[:, 0:D_MODEL].astype(BF16)
        cw = src[:, D_MODEL:H2X_W]
        hid = []
        for e in range(EXPERTS_PER_GROUP):
            a = jnp.dot(hb, wg_buf[slot, e], preferred_element_type=F32)
            u = jnp.dot(hb, wu_buf[slot, e], preferred_element_type=F32)
            hid.append((_silu(a) * u * cw[:, e:e + 1]).astype(BF16))
        dst[...] = jnp.dot(jnp.concatenate(hid, axis=1), wd_buf[slot], preferred_element_type=F32)

    @pl.when(s == 0)
    def _():
        y_ref[MOE_TILE:MOE_TILE + 8, :] = jnp.zeros((8, D_MODEL), F32)
        ys1[...] = jnp.zeros_like(ys1)
        gather(0, hs0)

    for par in (0, 1):
        @pl.when((s < n_act) & (s % 2 == par))
        def _():
            gather(s + 1, hs[1 - par])
            ffn(hs[par], ys[par])
            scatter(jnp.maximum(s - 1, 0), ys[1 - par])

        @pl.when((s == n_act) & (s % 2 == par))
        def _():
            scatter(s - 1, ys[1 - par])


def _moe(h2x, perm, cgrp, nact, wg, wu, wd):
    n = h2x.shape[0]
    n_tiles = n // MOE_TILE
    n_steps = n_tiles * MOE_STEPS

    step = jnp.arange(n_steps)
    active = (step % MOE_STEPS) < jnp.repeat(nact, MOE_STEPS)
    prev = jnp.concatenate([jnp.full((1,), -1, jnp.int32), cgrp[:-1]])
    first = active & ((step % MOE_STEPS == 0) | (cgrp != prev))
    run_id = jnp.cumsum(first.astype(jnp.int32)) - 1
    later_first = lax.cummin(jnp.where(first, step, n_steps), reverse=True)
    nxt_step = jnp.concatenate([later_first[1:], jnp.full((1,), n_steps, jnp.int32)])
    nxt_group = jnp.where(nxt_step < n_steps, cgrp[jnp.minimum(nxt_step, n_steps - 1)], -1)
    runs = jnp.concatenate([first.astype(jnp.int32), run_id % 2, nxt_group.astype(jnp.int32),
                            (first & (run_id == 0)).astype(jnp.int32)])

    def h_idx(i, s, *_):
        done = (s >= jnp.maximum(_[2][i] - 1, 1)).astype(jnp.int32)
        return (jnp.minimum(i + done, n_tiles - 1), 0)

    hbm = pl.BlockSpec(memory_space=pl.ANY)
    return pl.pallas_call(
        _moe_kernel,
        grid_spec=pltpu.PrefetchScalarGridSpec(
            num_scalar_prefetch=4,
            grid=(n_tiles, MOE_STEPS),
            in_specs=[pl.BlockSpec((MOE_TILE, H2X_W), h_idx), hbm, hbm, hbm],
            out_specs=pl.BlockSpec((None, MOE_TILE + 8, D_MODEL), lambda i, s, *_: (i, 0, 0)),
            scratch_shapes=[pltpu.VMEM((MOE_CHUNK, H2X_W), F32), pltpu.VMEM((MOE_CHUNK, H2X_W), F32),
                            pltpu.VMEM((MOE_CHUNK, D_MODEL), F32), pltpu.VMEM((MOE_CHUNK, D_MODEL), F32),
                            pltpu.VMEM((2, EXPERTS_PER_GROUP, D_MODEL, EXPERT_FF), BF16),
                            pltpu.VMEM((2, EXPERTS_PER_GROUP, D_MODEL, EXPERT_FF), BF16),
                            pltpu.VMEM((2, EXPERTS_PER_GROUP * EXPERT_FF, D_MODEL), BF16),
                            pltpu.SemaphoreType.DMA((2, 3))]),
        out_shape=jax.ShapeDtypeStruct((n_tiles, MOE_TILE + 8, D_MODEL), F32),
        compiler_params=_cparams(("arbitrary", "arbitrary")),
        name="moe_experts",
    )(perm, cgrp, nact, runs, h2x, wg, wu, wd)


def _final_kernel(y_ref, x1_ref, mods_ref, fg_ref, o_ref):
    gate2 = mods_ref[:, 5 * D_MODEL:6 * D_MODEL]
    x2 = x1_ref[...] + gate2 * y_ref[...]
    o_ref[...] = x2 * lax.rsqrt(jnp.mean(x2 * x2, axis=-1, keepdims=True) + EPS) * fg_ref[...]


def _final(y, x1, mods3, mod_row0, mod_tokens, fg):
    n = x1.shape[0]
    tm = 512
    per_mod = mod_tokens // tm
    per_tile = MOE_TILE // tm
    return pl.pallas_call(
        _final_kernel,
        grid=(n // tm,),
        in_specs=[pl.BlockSpec((None, tm, D_MODEL), lambda j: (j // per_tile, j % per_tile, 0)),
                  pl.BlockSpec((tm, D_MODEL), lambda j: (j, 0)),
                  pl.BlockSpec((None, 1, 6 * D_MODEL), lambda j: (mod_row0 + j // per_mod, 0, 0)),
                  pl.BlockSpec((1, D_MODEL), lambda j: (0, 0))],
        out_specs=pl.BlockSpec((tm, D_MODEL), lambda j: (j, 0)),
        out_shape=jax.ShapeDtypeStruct((n, D_MODEL), F32),
        compiler_params=_cparams(("parallel",)),
        name="final_norm",
    )(y, x1, mods3, fg)


def _rope_tables(t):
    n_freq = QK_DIM // 4
    n_rows = t // GRID_W
    freqs = ROPE_BASE ** (-jnp.arange(n_freq, dtype=F32) / n_freq)
    ang_r = jnp.arange(n_rows, dtype=F32)[:, None] * freqs
    ang_c = jnp.arange(GRID_W, dtype=F32)[:, None] * freqs
    cr, sr, cc, sc = lax.optimization_barrier((jnp.cos(ang_r), jnp.sin(ang_r), jnp.cos(ang_c), jnp.sin(ang_c)))
    j = np.arange(LANES) % QK_DIM
    f_idx = j % n_freq
    by_row = (j < QK_DIM // 2)[None, None, :]
    first = ((j % (QK_DIM // 2)) < n_freq)[None, None, :]

    def table(r_small, c_small):
        return jnp.where(by_row, r_small[:, f_idx][:, None, :], c_small[:, f_idx][None, :, :])

    cos = table(cr, cc)
    sin = table(sr, sc)
    return (cos.reshape(t, LANES), jnp.where(first, -sin, 0.0).reshape(t, LANES),
            jnp.where(first, 0.0, sin).reshape(t, LANES))


def _layer(x, mods3, mod_row0, mod_tokens, rope_tabs, ctx_k, ctx_v, h0f, h0b, lw, layer):
    b, t, _ = x.shape
    n = b * t
    xf = x.reshape(n, D_MODEL)
    res = _inproj(xf, mods3, mod_row0, mod_tokens, t, lw["norm_mix_g"], lw["w_main"], lw["w_dt"], lw["alog"],
                  lw["dtb"], rope_tabs)
    q, kb, vt, z, xbc, cols, hrow = res[:7]
    if ctx_k is None:
        cache = None
        k3 = res[7].reshape(b, 1, t, ATT_HEADS, 2, LANES)[..., :QK_DIM]
        v3 = res[8].reshape(b, 1, t, ATT_HEADS, V_DIM)
    else:
        cache = (ctx_k.astype(BF16), jnp.swapaxes(ctx_v, 1, 2).astype(BF16))
        k3 = v3 = None
    lam0 = 0.8 - 0.6 * math.exp(-0.3 * layer)
    tq = 512 if t % 512 == 0 else 256
    tk = 512 if t % 512 == 0 else 256
    att = _attention(q.reshape(b, t, D_QK), kb.reshape(b, t, D_QK), vt, cache, lw["lamp"], lw["attn_subln_g"],
                     lam0, tq, tk, ATT_HEADS if t <= 512 else 1)
    ssm, hf, hb = _ssd(xbc, cols.reshape(b, t, LANES), hrow.reshape(b, t // CHUNK * HROWS, LANES),
                       z.reshape(b, t, D_SSM), h0f, h0b, lw["conv_w"], lw["conv_b"], lw["d_x"], lw["ssm_norm_g"])
    x1, h2x = _outproj(att.reshape(n, D_ATT), ssm.reshape(n, D_SSM), xf, mods3, mod_row0, mod_tokens,
                       lw["wo_att"], lw["wo_ssm"], lw["norm_ffn_g"], lw["w_router"], lw["b_router"])
    meta = _route(h2x)
    perm = meta[:, 0, :].reshape(-1)
    cgrp = meta[:, 1, :MOE_STEPS].reshape(-1)
    nact = meta[:, 2, 0]
    y = _moe(h2x, perm, cgrp, nact, lw["wg"], lw["wu"], lw["wd"])
    out = _final(y, x1, mods3, mod_row0, mod_tokens, lw["final_g"])
    return out.reshape(b, t, D_MODEL), k3, v3, hf, hb


def _pad_lanes(v, width=LANES):
    return jnp.pad(v, [(0, 0)] * (v.ndim - 1) + [(0, width - v.shape[-1])])


def kernel(x_prompt, x_sample, cache_k, cache_v, state_ssm_fwd, state_ssm_bwd, c, c_ctx, w_ada, b_ada, norm_mix_g, w_in, w_out, lambda_q1, lambda_k1, lambda_q2, lambda_k2, attn_subln_g, conv_w, conv_b, a_log_fwd, a_log_bwd, dt_bias_fwd, dt_bias_bwd, ssm_d, ssm_norm_g, norm_ffn_g, w_group_router, b_group_router, w_expert_router, b_expert_router, w_exp_gate, w_exp_up, w_exp_down, final_norm_g):
    depth = w_in.shape[0]
    assert depth == 1, "single trunk layer"
    bp, tp, _ = x_prompt.shape
    bs, ts, _ = x_sample.shape
    l = 0
    cond = jnp.concatenate([c_ctx[None], c], axis=0)
    condT = _pad_lanes(cond.T, 8)
    mods = _ada(condT, w_ada[l], b_ada[l][None])
    mods3 = mods.reshape(8, 1, 6 * D_MODEL)

    w_router = _pad_lanes(jnp.concatenate([w_group_router[l], w_expert_router[l]], axis=1))
    wr_hi = w_router.astype(BF16)
    wr_lo = (w_router - wr_hi.astype(F32)).astype(BF16)
    lw = dict(
        norm_mix_g=norm_mix_g[l][None],
        w_main=w_in[l].astype(BF16),
        w_dt=_pad_lanes(w_in[l][:, MAIN_COLS:]).astype(BF16),
        lamp=jnp.stack([lambda_q1[l], lambda_k1[l], lambda_q2[l], lambda_k2[l]]),
        attn_subln_g=attn_subln_g[l][None],
        conv_w=conv_w[l].reshape(CONV_W, N_COLBLK, LANES), conv_b=conv_b[l].reshape(N_COLBLK, LANES),
        alog=jnp.broadcast_to(jnp.concatenate([a_log_fwd[l], a_log_bwd[l]])[:, None], (2 * SSM_HEADS, CHUNK)),
        dtb=jnp.broadcast_to(jnp.concatenate([dt_bias_fwd[l], dt_bias_bwd[l]])[:, None], (2 * SSM_HEADS, CHUNK)),
        d_x=jnp.repeat(ssm_d[l], SSM_HEADDIM)[None], ssm_norm_g=ssm_norm_g[l][None],
        wo_att=w_out[l][:D_ATT].astype(BF16), wo_ssm=w_out[l][D_ATT:].astype(BF16),
        norm_ffn_g=norm_ffn_g[l][None],
        w_router=jnp.concatenate([wr_hi, wr_lo], axis=1),
        b_router=_pad_lanes(jnp.concatenate([b_group_router[l], b_expert_router[l]])[None]),
        wg=w_exp_gate[l].astype(BF16), wu=w_exp_up[l].astype(BF16), wd=w_exp_down[l].astype(BF16).reshape(N_EXPERTS * EXPERT_FF, D_MODEL),
        final_g=final_norm_g[None],
    )
    n_state = SSM_HEADS * SSM_HEADDIM
    zeros_state = jnp.zeros((bp, n_state, D_STATE), F32)
    yp, ck, cv, hf, hb = _layer(x_prompt, mods3, 0, bp * tp, None, None, None, zeros_state, zeros_state, lw, l)
    ys, _, _, _, _ = _layer(x_sample, mods3, 1, ts, _rope_tables(ts),
                            cache_k[:, l].reshape(bs, -1, D_QK), cache_v[:, l].reshape(bs, -1, D_ATT),
                            state_ssm_fwd[:, l].reshape(bs, n_state, D_STATE),
                            state_ssm_bwd[:, l].reshape(bs, n_state, D_STATE), lw, l)
    new_k, new_v = ck, cv
    new_hf = hf.reshape(bp, 1, SSM_HEADS, SSM_HEADDIM, D_STATE)
    new_hb = hb.reshape(bp, 1, SSM_HEADS, SSM_HEADDIM, D_STATE)
    return yp, ys, new_k, new_v, new_hf, new_hb
```

```python
import functools
import math

import numpy as np
import jax
import jax.numpy as jnp
from jax import lax
from jax.experimental import pallas as pl
from jax.experimental.pallas import tpu as pltpu

D_MODEL = 1024
GRID_W = 64
ATT_HEADS = 4
QK_DIM = 64
V_DIM = 128
D_QK = 512
D_ATT = 512
ROPE_BASE = 10000.0
D_SSM = 512
SSM_HEADDIM = 64
SSM_HEADS = 8
SSM_GROUPS = 2
D_STATE = 128
CONV_W = 5
CHUNK = 128
XBC_DIM = 1024
N_GROUPS = 4
EXPERTS_PER_GROUP = 4
N_EXPERTS = 16
EXPERT_FF = 256
EPS = 1e-6
MAIN_COLS = 2 * D_QK + D_ATT + D_SSM + XBC_DIM
H2X_W = D_MODEL + 128
MOE_TILE = 2048
MOE_CHUNK = 256
MOE_SLOTS = MOE_TILE // MOE_CHUNK + N_GROUPS
MOE_ROWS = MOE_SLOTS * MOE_CHUNK
MOE_STEPS = MOE_SLOTS + 1
META_ROWS = MOE_STEPS * MOE_CHUNK
COL_CUM, COL_DT, COL_ECUM, COL_TOEND = 0, 16, 32, 48
HROWS = 32
LANES = 128
HALO = 8
VMEM_LIMIT = 56 * 1024 * 1024

LOG2E = math.log2(math.e)
SUM_ROWS = 16
F32 = jnp.float32
BF16 = jnp.bfloat16


def _cparams(sem):
    return pltpu.CompilerParams(dimension_semantics=sem, vmem_limit_bytes=VMEM_LIMIT)


def _sigmoid(x):
    return 1.0 / (1.0 + jnp.exp(-x))


def _silu(x):
    return x * _sigmoid(x)


def _ada_kernel(condT_ref, w_ref, b_ref, o_ref):
    s = _silu(condT_ref[...])
    w = w_ref[...]
    b = b_ref[...]
    o_ref[...] = jnp.zeros_like(o_ref)
    for r in range(3):
        o_ref[r:r + 1, :] = jnp.sum(w * s[:, r:r + 1], axis=0, keepdims=True) + b


def _ada(condT, w_ada, b_ada):
    bn = 1024
    n = w_ada.shape[1]
    return pl.pallas_call(
        _ada_kernel,
        grid=(n // bn,),
        in_specs=[pl.BlockSpec((D_MODEL, 8), lambda j: (0, 0)),
                  pl.BlockSpec((D_MODEL, bn), lambda j: (0, j)),
                  pl.BlockSpec((1, bn), lambda j: (0, j))],
        out_specs=pl.BlockSpec((8, bn), lambda j: (0, j)),
        out_shape=jax.ShapeDtypeStruct((8, n), F32),
        compiler_params=_cparams(("arbitrary",)),
        name="ada",
    )(condT, w_ada, b_ada)


def _split3(x):
    hi = x.astype(BF16)
    r1 = x - hi.astype(F32)
    mid = r1.astype(BF16)
    lo = (r1 - mid.astype(F32)).astype(BF16)
    return jnp.concatenate([hi, mid, lo], axis=-1)


def _scan_matrices():
    t = np.arange(CHUNK)
    pre = (t[:, None] <= t[None, :]).astype(np.float32)
    suf = (t[:, None] >= t[None, :]).astype(np.float32)
    return jnp.asarray(np.stack([np.concatenate([pre] * 3, axis=0), np.concatenate([suf] * 3, axis=0)]), BF16)


def _head_scalars(dt_raw, alog_ref, dtb_ref, scan_ref):
    nh2 = 2 * SSM_HEADS
    xv = dt_raw.T[0:nh2, :] + dtb_ref[...]
    dt = jnp.maximum(xv, 0.0) + jnp.log(1.0 + jnp.exp(-jnp.abs(xv)))
    la3 = _split3(dt * (-jnp.exp(alog_ref[...])))
    fwd = lax.broadcasted_iota(jnp.int32, (nh2, CHUNK), 0) < SSM_HEADS
    cum = jnp.where(fwd, jnp.dot(la3, scan_ref[0], preferred_element_type=F32),
                    jnp.dot(la3, scan_ref[1], preferred_element_type=F32))
    cum_end = jnp.where(fwd, cum[:, CHUNK - 1:CHUNK], cum[:, 0:1])
    packed = jnp.concatenate([cum, dt, jnp.exp(cum), jnp.exp(cum_end - cum),
                              jnp.zeros((LANES - 4 * nh2, CHUNK), F32)], axis=0)
    return packed.T, jnp.concatenate([cum, jnp.exp(cum_end)], axis=0)


def _inproj_kernel(rope, x_ref, mods_ref, g_ref, w_ref, wdt_ref, alog_ref, dtb_ref, scan_ref, *rest):
    if rope:
        cos_ref, sa_ref, sb_ref, q_ref, kb_ref, vt_ref, z_ref, xbc_ref, cols_ref, hrow_ref = rest
    else:
        q_ref, kb_ref, vt_ref, z_ref, xbc_ref, cols_ref, hrow_ref, kf_ref, vf_ref = rest
    x = x_ref[...]
    shift = mods_ref[:, 0:D_MODEL]
    scale = mods_ref[:, D_MODEL:2 * D_MODEL]
    y = x * lax.rsqrt(jnp.mean(x * x, axis=-1, keepdims=True) + EPS) * g_ref[...]
    h = (y * (1.0 + scale) + shift).astype(BF16)
    dt_raw = jnp.dot(h, wdt_ref[...], preferred_element_type=F32)
    for ci in range(x.shape[0] // CHUNK):
        cols, hrow = _head_scalars(dt_raw[ci * CHUNK:(ci + 1) * CHUNK, :], alog_ref, dtb_ref, scan_ref)
        cols_ref[ci * CHUNK:(ci + 1) * CHUNK, :] = cols
        hrow_ref[ci * HROWS:(ci + 1) * HROWS, :] = hrow
    r = jnp.dot(h, w_ref[...], preferred_element_type=F32)
    q = r[:, 0:D_QK]
    k = r[:, D_QK:2 * D_QK]
    if rope:
        cos = cos_ref[...]
        sa = sa_ref[...]
        sb = sb_ref[...]

        def rot(t):
            parts = []
            for hh in range(ATT_HEADS):
                th = t[:, hh * LANES:(hh + 1) * LANES]
                parts.append(th * cos + pltpu.roll(th, LANES - 16, 1) * sa + pltpu.roll(th, 16, 1) * sb)
            return jnp.concatenate(parts, axis=1)

        q = rot(q)
        k = rot(k)
    v = r[:, 2 * D_QK:2 * D_QK + D_ATT]
    q_ref[...] = q
    kb_ref[...] = k.astype(BF16)
    n_seq, _, t_seq = vt_ref.shape
    v_t = v.T.astype(BF16)
    for sq in range(n_seq):
        vt_ref[sq] = v_t[:, sq * t_seq:(sq + 1) * t_seq]
    if not rope:
        tm = k.shape[0]
        for hh in range(ATT_HEADS):
            k_h = k[:, hh * LANES:(hh + 1) * LANES]
            vf_ref[pl.ds(hh, tm, stride=ATT_HEADS), :] = v[:, hh * LANES:(hh + 1) * LANES]
            kf_ref[pl.ds(2 * hh, tm, stride=2 * ATT_HEADS), :] = k_h
            kf_ref[pl.ds(2 * hh + 1, tm, stride=2 * ATT_HEADS), :] = pltpu.roll(k_h, QK_DIM, 1)
    z_ref[...] = r[:, 2 * D_QK + D_ATT:2 * D_QK + D_ATT + D_SSM]
    x0 = 2 * D_QK + D_ATT + D_SSM
    for cb in range(XBC_DIM // LANES):
        for sq in range(n_seq):
            xbc_ref[sq, cb] = r[sq * t_seq:(sq + 1) * t_seq, x0 + cb * LANES:x0 + (cb + 1) * LANES]


def _inproj(x, mods3, mod_row0, mod_tokens, seq_len, g, w_main, w_dt, alog, dtb, rope_tabs):
    n = x.shape[0]
    tm = 512
    assert seq_len % tm == 0 or tm % seq_len == 0
    per_seq = max(seq_len // tm, 1)
    n_seq = max(tm // seq_len, 1)
    t_seq = tm // n_seq
    per_mod = mod_tokens // tm
    rope = rope_tabs is not None
    in_specs = [pl.BlockSpec((tm, D_MODEL), lambda i: (i, 0)),
                pl.BlockSpec((None, 1, 6 * D_MODEL), lambda i: (mod_row0 + i // per_mod, 0, 0)),
                pl.BlockSpec((1, D_MODEL), lambda i: (0, 0)),
                pl.BlockSpec((D_MODEL, MAIN_COLS), lambda i: (0, 0)),
                pl.BlockSpec((D_MODEL, LANES), lambda i: (0, 0)),
                pl.BlockSpec((2 * SSM_HEADS, CHUNK), lambda i: (0, 0)),
                pl.BlockSpec((2 * SSM_HEADS, CHUNK), lambda i: (0, 0)),
                pl.BlockSpec((2, 3 * CHUNK, CHUNK), lambda i: (0, 0, 0))]
    args = [x, mods3, g, w_main, w_dt, alog, dtb, _scan_matrices()]
    if rope:
        tab_spec = pl.BlockSpec((tm, LANES), lambda i: (i % per_seq, 0))
        in_specs += [tab_spec] * 3
        args += list(rope_tabs)
    def rows(wd, dtype=F32):
        return pl.BlockSpec((tm, wd), lambda i: (i, 0)), jax.ShapeDtypeStruct((n, wd), dtype)

    hr = tm // CHUNK * HROWS
    outs = [rows(D_QK), rows(D_QK, BF16),
            (pl.BlockSpec((n_seq, D_ATT, t_seq), lambda i: (i // per_seq, 0, i % per_seq)),
             jax.ShapeDtypeStruct((n // seq_len, D_ATT, seq_len), BF16)),
            rows(D_SSM),
            (pl.BlockSpec((n_seq, XBC_DIM // LANES, t_seq, LANES), lambda i: (i // per_seq, 0, i % per_seq, 0)),
             jax.ShapeDtypeStruct((n // seq_len, XBC_DIM // LANES, seq_len, LANES), F32)),
            rows(LANES),
            (pl.BlockSpec((hr, LANES), lambda i: (i, 0)), jax.ShapeDtypeStruct((n // CHUNK * HROWS, LANES), F32))]
    if not rope:
        outs += [(pl.BlockSpec((tm * 2 * ATT_HEADS, LANES), lambda i: (i, 0)),
                  jax.ShapeDtypeStruct((n * 2 * ATT_HEADS, LANES), F32)),
                 (pl.BlockSpec((tm * ATT_HEADS, LANES), lambda i: (i, 0)),
                  jax.ShapeDtypeStruct((n * ATT_HEADS, LANES), F32))]
    return pl.pallas_call(
        functools.partial(_inproj_kernel, rope),
        grid=(n // tm,),
        in_specs=in_specs,
        out_specs=[o[0] for o in outs],
        out_shape=[o[1] for o in outs],
        compiler_params=_cparams(("parallel",)),
        name="inproj_rope" if rope else "inproj",
    )(*args)


def _attn_kernel(tk, lam0, has_cache, q_ref, k_ref, vt_ref, *rest):
    if has_cache:
        ck_ref, cvt_ref, lamp_ref, g_ref, o_ref = rest
    else:
        lamp_ref, g_ref, o_ref = rest
    tq = q_ref.shape[1]
    lp = lamp_ref[...]
    lam = (jnp.exp(jnp.sum(lp[0:1] * lp[1:2], axis=-1, keepdims=True))
           - jnp.exp(jnp.sum(lp[2:3] * lp[3:4], axis=-1, keepdims=True)) + lam0)
    ones_rows = jnp.ones((SUM_ROWS, tk), BF16)
    for bb, hh in [(b_, h_) for b_ in range(q_ref.shape[0]) for h_ in range(q_ref.shape[2] // LANES)]:
        hs = slice(hh * LANES, (hh + 1) * LANES)
        chunks = [(k_ref, vt_ref, c * tk) for c in range(k_ref.shape[1] // tk)]
        if has_cache:
            chunks += [(ck_ref, cvt_ref, c * tk) for c in range(ck_ref.shape[1] // tk)]
        q = q_ref[bb, :, hs] * (QK_DIM ** -0.5 * LOG2E)
        lane = lax.broadcasted_iota(jnp.int32, q.shape, 1)
        qq_t = jnp.concatenate([jnp.where(lane < QK_DIM, q, 0.0), jnp.where(lane >= QK_DIM, q, 0.0)],
                               axis=0).T.astype(BF16)

        def scores(chunk):
            kr, _, start = chunk
            return jnp.dot(kr[bb, start:start + tk, hs], qq_t, preferred_element_type=F32)

        def update(s, chunk, m, acc):
            _, vr, start = chunk
            m_new = jnp.maximum(m, jnp.max(s, axis=0, keepdims=True))
            alpha = jnp.exp2(m - m_new)
            p = jnp.exp2(s - m_new).astype(BF16)
            v_ext = jnp.concatenate([vr[bb, hs, start:start + tk], ones_rows], axis=0)
            acc = alpha * acc + jnp.dot(v_ext, p, preferred_element_type=F32)
            return m_new, acc

        m = jnp.full((1, 2 * tq), -jnp.inf, F32)
        acc = jnp.zeros((V_DIM + SUM_ROWS, 2 * tq), F32)
        s = scores(chunks[0])
        for c, chunk in enumerate(chunks):
            s_next = scores(chunks[c + 1]) if c + 1 < len(chunks) else None
            m, acc = update(s, chunk, m, acc)
            s = s_next
        o = acc[0:V_DIM] / acc[V_DIM:V_DIM + 1]
        o = (o[:, 0:tq] - lam * o[:, tq:2 * tq]).T
        o = o * lax.rsqrt(jnp.mean(o * o, axis=-1, keepdims=True) + EPS)
        o_ref[bb, :, hs] = o * g_ref[...] * (1.0 - lam0)


def _attention(q, k, vt, cache, lamp, g, lam0, tq, tk, nh, nb):
    b, t, _ = q.shape
    wd = nh * LANES

    def kv_specs(length):
        return [pl.BlockSpec((nb, length, wd), lambda bi, h, i: (bi, 0, h)),
                pl.BlockSpec((nb, wd, length), lambda bi, h, i: (bi, h, 0))]

    in_specs = [pl.BlockSpec((nb, tq, wd), lambda bi, h, i: (bi, i, h))] + kv_specs(t)
    args = [q, k, vt]
    if cache is not None:
        assert cache[0].shape[1] % tk == 0
        in_specs += kv_specs(cache[0].shape[1])
        args += list(cache)
    in_specs += [pl.BlockSpec((4, QK_DIM), lambda bi, h, i: (0, 0)),
                 pl.BlockSpec((1, V_DIM), lambda bi, h, i: (0, 0))]
    return pl.pallas_call(
        functools.partial(_attn_kernel, tk, lam0, cache is not None),
        grid=(b // nb, ATT_HEADS // nh, t // tq),
        in_specs=in_specs,
        out_specs=pl.BlockSpec((nb, tq, wd), lambda bi, h, i: (bi, i, h)),
        out_shape=jax.ShapeDtypeStruct((b, t, D_ATT), F32),
        compiler_params=_cparams(("parallel", "parallel", "arbitrary")),
        name="diff_attn",
    )(*args, lamp, g)


N_COLBLK = XBC_DIM // LANES
def _expand_matrices():
    out = []
    for d in range(2):
        e = np.zeros((LANES, 3 * D_SSM), np.float32)
        for blk, lane0 in enumerate((COL_DT, COL_ECUM, COL_TOEND)):
            for h in range(SSM_HEADS):
                e[lane0 + d * SSM_HEADS + h,
                  blk * D_SSM + h * SSM_HEADDIM:blk * D_SSM + (h + 1) * SSM_HEADDIM] = 1.0
        out.append(np.concatenate([e, e, e], axis=0))
    return jnp.asarray(np.stack(out), BF16)


def _ssd_chunk(reverse, xs, bm, cm, state, hrow, cols, e_ref):
    d0 = SSM_HEADS if reverse else 0
    cum = hrow[d0:d0 + SSM_HEADS, :]
    dec = hrow[2 * SSM_HEADS + d0:2 * SSM_HEADS + d0 + SSM_HEADS, 0:1]
    row = lax.broadcasted_iota(jnp.int32, (CHUNK, LANES), 0)
    lane = lax.broadcasted_iota(jnp.int32, (CHUNK, LANES), 1)
    causal = (row <= lane) if reverse else (row >= lane)
    lane_g = lax.broadcasted_iota(jnp.int32, (CHUNK, 2 * LANES), 1)
    spread = jnp.dot(_split3(cols), e_ref[1 if reverse else 0], preferred_element_type=F32)
    xd = xs * spread[:, 0:D_SSM]
    xdw = (xd * spread[:, 2 * D_SSM:3 * D_SSM]).astype(BF16)
    xd = xd.astype(BF16)
    ecum_x = spread[:, D_SSM:2 * D_SSM]
    rep = SSM_HEADS // SSM_GROUPS
    y_parts = []
    new_state = []
    for g in range(SSM_GROUPS):
        bg = bm[:, g * D_STATE:(g + 1) * D_STATE]
        cg = cm[:, g * D_STATE:(g + 1) * D_STATE]
        cbt = lax.dot_general(cg, bg, (((1,), (1,)), ((), ())), preferred_element_type=F32)
        rows = slice(g * rep * SSM_HEADDIM, (g + 1) * rep * SSM_HEADDIM)
        st_g = state[rows, :]
        y_off = lax.dot_general(cg, st_g.astype(BF16), (((1,), (1,)), ((), ())),
                                preferred_element_type=F32)
        cst = lax.dot_general(xdw[:, rows], bg, (((0,), (0,)), ((), ())), preferred_element_type=F32)
        xd_g = xd[:, rows]
        scs = []
        blocks = []
        for hh in range(rep):
            h = g * rep + hh
            seg = cols[:, COL_CUM + d0 + h:COL_CUM + d0 + h + 1] - cum[h:h + 1, :]
            decay = jnp.exp(jnp.where(causal, seg, -jnp.inf))
            scs.append((cbt * decay).astype(BF16))
            blocks.append(jnp.where(lane_g // SSM_HEADDIM == hh, xd_g, jnp.zeros_like(xd_g)))
        y_diag = jnp.dot(jnp.concatenate(scs, axis=1), jnp.concatenate(blocks, axis=0),
                         preferred_element_type=F32)
        dec_rows = jnp.concatenate(
            [jnp.broadcast_to(dec[g * rep + hh:g * rep + hh + 1, :], (SSM_HEADDIM, D_STATE)) for hh in range(rep)],
            axis=0)
        new_state.append(st_g * dec_rows + cst)
        y_parts.append(y_diag + y_off * ecum_x[:, rows])
    return jnp.concatenate(y_parts, axis=1), jnp.concatenate(new_state, axis=0)


def _ssd_kernel(nc, xc_ref, xp_ref, xn_ref, cols_ref, hrow_ref, z_ref, h0f_ref, h0b_ref, cw_ref, cb_ref,
                d_ref, ng_ref, e_ref, y_ref, hf_ref, hb_ref, state_ref, slab_ref, xs_ref, bc_ref, yf_ref):
    ps = pl.program_id(1)
    c = pl.program_id(2)
    n_bc = SSM_GROUPS * D_STATE

    @pl.when((ps == 0) & (c == 0))
    def _():
        state_ref[...] = h0f_ref[...]

    @pl.when(ps == 0)
    def _():
        t0 = pl.multiple_of(c * CHUNK, CHUNK)
        for bb in range(state_ref.shape[0]):
            slab_ref[bb, :, 0:HALO, :] = jnp.where(c > 0, xp_ref[bb], 0.0)
            slab_ref[bb, :, HALO:HALO + CHUNK, :] = xc_ref[bb]
            slab_ref[bb, :, HALO + CHUNK:2 * HALO + CHUNK, :] = jnp.where(c < nc - 1, xn_ref[bb], 0.0)
            blocks = []
            for cb in range(N_COLBLK):
                conv = cb_ref[cb:cb + 1, :]
                for kk in range(CONV_W):
                    off = HALO - CONV_W // 2 + kk
                    conv = conv + slab_ref[bb, cb, off:off + CHUNK, :] * cw_ref[kk, cb:cb + 1, :]
                blocks.append(_silu(conv))
            xs = jnp.concatenate(blocks[0:D_SSM // LANES], axis=1)
            bc = jnp.concatenate(blocks[D_SSM // LANES:], axis=1).astype(BF16)
            y, state = _ssd_chunk(False, xs, bc[:, 0:n_bc], bc[:, n_bc:], state_ref[bb], hrow_ref[bb],
                                  cols_ref[bb], e_ref)
            xs_ref[bb, pl.ds(t0, CHUNK), :] = xs
            bc_ref[bb, pl.ds(t0, CHUNK), :] = bc
            yf_ref[bb, pl.ds(t0, CHUNK), :] = y
            state_ref[bb] = state

    @pl.when((ps == 0) & (c == nc - 1))
    def _():
        hf_ref[...] = state_ref[...]
        state_ref[...] = h0b_ref[...]

    @pl.when(ps == 1)
    def _():
        t0 = pl.multiple_of((nc - 1 - c) * CHUNK, CHUNK)
        for bb in range(state_ref.shape[0]):
            xs = xs_ref[bb, pl.ds(t0, CHUNK), :]
            bc = bc_ref[bb, pl.ds(t0, CHUNK), :]
            y, state = _ssd_chunk(True, xs, bc[:, 0:n_bc], bc[:, n_bc:], state_ref[bb], hrow_ref[bb],
                                  cols_ref[bb], e_ref)
            state_ref[bb] = state
            y = y + yf_ref[bb, pl.ds(t0, CHUNK), :] + xs * d_ref[...]
            y = y * _silu(z_ref[bb])
            y = y * lax.rsqrt(jnp.mean(y * y, axis=-1, keepdims=True) + EPS)
            y_ref[bb] = y * ng_ref[...]

    @pl.when((ps == 1) & (c == nc - 1))
    def _():
        hb_ref[...] = state_ref[...]


def _ssd(xbc, cols, hrow, z, h0f, h0b, conv_w, conv_b, d_x, norm_g):
    b, _, l, _ = xbc.shape
    nb = 2
    assert b % nb == 0
    nc = l // CHUNK
    hb = CHUNK // HALO
    n_state = SSM_HEADS * SSM_HEADDIM
    last = nc - 1

    def fwd_chunk(ps, ci):
        return jnp.where(ps == 0, ci, last)

    def any_chunk(ps, ci):
        return jnp.where(ps == 0, ci, last - ci)

    def bwd_chunk(ps, ci):
        return jnp.where(ps == 0, last, last - ci)

    const2 = lambda bi, ps, ci: (0, 0)
    state_spec = pl.BlockSpec((nb, n_state, D_STATE), lambda bi, ps, ci: (bi, 0, 0))
    in_specs = [pl.BlockSpec((nb, N_COLBLK, CHUNK, LANES), lambda bi, ps, ci: (bi, 0, fwd_chunk(ps, ci), 0)),
                pl.BlockSpec((nb, N_COLBLK, HALO, LANES),
                             lambda bi, ps, ci: (bi, 0, jnp.maximum(fwd_chunk(ps, ci) * hb - 1, 0), 0)),
                pl.BlockSpec((nb, N_COLBLK, HALO, LANES),
                             lambda bi, ps, ci: (bi, 0, jnp.minimum((fwd_chunk(ps, ci) + 1) * hb, l // HALO - 1), 0)),
                pl.BlockSpec((nb, CHUNK, LANES), lambda bi, ps, ci: (bi, any_chunk(ps, ci), 0)),
                pl.BlockSpec((nb, HROWS, LANES), lambda bi, ps, ci: (bi, any_chunk(ps, ci), 0)),
                pl.BlockSpec((nb, CHUNK, D_SSM), lambda bi, ps, ci: (bi, bwd_chunk(ps, ci), 0)),
                state_spec, state_spec,
                pl.BlockSpec((CONV_W, N_COLBLK, LANES), lambda bi, ps, ci: (0, 0, 0)),
                pl.BlockSpec((N_COLBLK, LANES), const2),
                pl.BlockSpec((1, D_SSM), const2),
                pl.BlockSpec((1, D_SSM), const2),
                pl.BlockSpec((2, 3 * LANES, 3 * D_SSM), lambda bi, ps, ci: (0, 0, 0))]
    return pl.pallas_call(
        functools.partial(_ssd_kernel, nc),
        grid=(b // nb, 2, nc),
        in_specs=in_specs,
        out_specs=[pl.BlockSpec((nb, CHUNK, D_SSM), lambda bi, ps, ci: (bi, bwd_chunk(ps, ci), 0)),
                   state_spec, state_spec],
        out_shape=[jax.ShapeDtypeStruct((b, l, D_SSM), F32),
                   jax.ShapeDtypeStruct((b, n_state, D_STATE), F32),
                   jax.ShapeDtypeStruct((b, n_state, D_STATE), F32)],
        scratch_shapes=[pltpu.VMEM((nb, n_state, D_STATE), F32),
                        pltpu.VMEM((nb, N_COLBLK, CHUNK + 2 * HALO, LANES), F32),
                        pltpu.VMEM((nb, l, D_SSM), F32),
                        pltpu.VMEM((nb, l, 2 * SSM_GROUPS * D_STATE), BF16),
                        pltpu.VMEM((nb, l, D_SSM), F32)],
        compiler_params=_cparams(("parallel", "arbitrary", "arbitrary")),
        name="ssd",
    )(xbc, xbc, xbc, cols, hrow, z, h0f, h0b, conv_w, conv_b, d_x, norm_g, _expand_matrices())


def _outproj_kernel(att_ref, ssm_ref, x_ref, mods_ref, wa_ref, ws_ref, g_ref, wr_ref, br_ref,
                    x1_ref, h2x_ref):
    mix = (jnp.dot(att_ref[...].astype(BF16), wa_ref[...], preferred_element_type=F32)
           + jnp.dot(ssm_ref[...].astype(BF16), ws_ref[...], preferred_element_type=F32))
    gate1 = mods_ref[:, 2 * D_MODEL:3 * D_MODEL]
    shift2 = mods_ref[:, 3 * D_MODEL:4 * D_MODEL]
    scale2 = mods_ref[:, 4 * D_MODEL:5 * D_MODEL]
    x1 = x_ref[...] + gate1 * mix
    x1_ref[...] = x1
    y = x1 * lax.rsqrt(jnp.mean(x1 * x1, axis=-1, keepdims=True) + EPS) * g_ref[...]
    h2 = y * (1.0 + scale2) + shift2
    h_hi = h2.astype(BF16)
    h2x_ref[:, 0:D_MODEL] = h2
    h_lo = (h2 - h_hi.astype(F32)).astype(BF16)
    both = jnp.dot(h_hi, wr_ref[...], preferred_element_type=F32)
    logits = (both[:, 0:LANES] + both[:, LANES:2 * LANES]
              + jnp.dot(h_lo, wr_ref[:, 0:LANES], preferred_element_type=F32)) + br_ref[...]
    lane = lax.broadcasted_iota(jnp.int32, logits.shape, 1).astype(F32)
    neg = -jnp.inf
    big = float(1 << 20)
    is_g = lane < N_GROUPS
    gl = jnp.where(is_g, logits, neg)
    gmax = jnp.max(gl, axis=-1, keepdims=True)
    g_idx = jnp.min(jnp.where(gl == gmax, lane, big), axis=-1, keepdims=True)
    p_g = 1.0 / jnp.sum(jnp.where(is_g, jnp.exp(gl - gmax), 0.0), axis=-1, keepdims=True)
    e_lo = N_GROUPS + g_idx * EXPERTS_PER_GROUP
    in_grp = (lane >= e_lo) & (lane < e_lo + EXPERTS_PER_GROUP)
    el = jnp.where(in_grp, logits, neg)
    m1 = jnp.max(el, axis=-1, keepdims=True)
    i1 = jnp.min(jnp.where(el == m1, lane, big), axis=-1, keepdims=True)
    el2 = jnp.where(lane == i1, neg, el)
    m2 = jnp.max(el2, axis=-1, keepdims=True)
    i2 = jnp.min(jnp.where(el2 == m2, lane, big), axis=-1, keepdims=True)
    e2 = jnp.exp(m2 - m1)
    w1 = p_g / (1.0 + e2)
    w2 = p_g * e2 / (1.0 + e2)
    slab = (jnp.where(lane == i1 - e_lo, w1, 0.0) + jnp.where(lane == i2 - e_lo, w2, 0.0)
            + jnp.where(lane == EXPERTS_PER_GROUP, g_idx, 0.0))
    h2x_ref[:, D_MODEL:H2X_W] = slab


def _outproj(att, ssm, x, mods3, mod_row0, mod_tokens, wo_att, wo_ssm, g, w_router, b_router):
    n = x.shape[0]
    tm = 512
    per_mod = mod_tokens // tm
    return pl.pallas_call(
        _outproj_kernel,
        grid=(n // tm,),
        in_specs=[pl.BlockSpec((tm, D_ATT), lambda i: (i, 0)),
                  pl.BlockSpec((tm, D_SSM), lambda i: (i, 0)),
                  pl.BlockSpec((tm, D_MODEL), lambda i: (i, 0)),
                  pl.BlockSpec((None, 1, 6 * D_MODEL), lambda i: (mod_row0 + i // per_mod, 0, 0)),
                  pl.BlockSpec((D_ATT, D_MODEL), lambda i: (0, 0)),
                  pl.BlockSpec((D_SSM, D_MODEL), lambda i: (0, 0)),
                  pl.BlockSpec((1, D_MODEL), lambda i: (0, 0)),
                  pl.BlockSpec((D_MODEL, 2 * LANES), lambda i: (0, 0)),
                  pl.BlockSpec((1, LANES), lambda i: (0, 0))],
        out_specs=[pl.BlockSpec((tm, D_MODEL), lambda i: (i, 0)),
                   pl.BlockSpec((tm, H2X_W), lambda i: (i, 0))],
        out_shape=[jax.ShapeDtypeStruct((n, D_MODEL), F32),
                   jax.ShapeDtypeStruct((n, H2X_W), F32)],
        compiler_params=_cparams(("parallel",)),
        name="outproj_router",
    )(att, ssm, x, mods3, wo_att, wo_ssm, g, w_router, b_router)


def _route_kernel(slab_ref, meta_ref):
    t_n = MOE_TILE
    blk = LANES
    slab = slab_ref[...]
    lane = lax.broadcasted_iota(jnp.int32, (t_n, LANES), 1)
    gcol = jnp.sum(jnp.where(lane == EXPERTS_PER_GROUP, slab, 0.0), axis=-1, keepdims=True)
    member = (lane.astype(F32) == gcol) & (lane < N_GROUPS)
    a = jnp.where(member, 1.0, 0.0).astype(BF16)
    r_i = lax.broadcasted_iota(jnp.int32, (blk, blk), 0)
    c_i = lax.broadcasted_iota(jnp.int32, (blk, blk), 1)
    lower = jnp.where(c_i < r_i, 1.0, 0.0).astype(BF16)
    upper = jnp.where(r_i < c_i, 1.0, 0.0).astype(BF16)
    offs = jnp.zeros((1, LANES), F32)
    ranks = []
    for b in range(t_n // blk):
        ab = a[b * blk:(b + 1) * blk]
        rb = jnp.dot(lower, ab, preferred_element_type=F32)
        ranks.append(rb + offs)
        offs = offs + rb[blk - 1:blk] + ab[blk - 1:blk].astype(F32)
    rank = jnp.concatenate(ranks, axis=0)
    n_chunk = jnp.floor((offs + (MOE_CHUNK - 1)) * (1.0 / MOE_CHUNK))
    start = jnp.dot(jnp.broadcast_to(n_chunk, (8, LANES)).astype(BF16), upper,
                    preferred_element_type=F32)[0:1]
    end = start + n_chunk
    dest = jnp.sum(jnp.where(member, start * MOE_CHUNK + rank, 0.0), axis=-1, keepdims=True)
    tok = lax.broadcasted_iota(jnp.int32, (t_n, LANES), 0)
    digits = jnp.where(lane == 0, (tok // blk).astype(F32),
                       jnp.where(lane == 1, (tok % blk).astype(F32), jnp.where(lane == 2, 1.0, 0.0))).astype(BF16)
    sw = 512
    pieces = []
    for sc in range(MOE_ROWS // sw):
        s_id = (lax.broadcasted_iota(jnp.int32, (t_n, sw), 1) + sc * sw).astype(F32)
        hit = jnp.where(dest == s_id, 1.0, 0.0).astype(BF16)
        r = lax.dot_general(digits, hit, (((0,), (0,)), ((), ())), preferred_element_type=F32)
        tok_of = r[0:1] * blk + r[1:2]
        pieces.append(jnp.where(r[2:3] > 0.5, tok_of, float(t_n)))
    pieces.append(jnp.full((1, META_ROWS - MOE_ROWS), float(t_n), F32))
    perm = jnp.concatenate(pieces, axis=1)
    slot = lax.broadcasted_iota(jnp.int32, (1, META_ROWS), 1).astype(F32)
    lane1 = lax.broadcasted_iota(jnp.int32, (1, LANES), 1)
    cg = jnp.zeros((1, META_ROWS), F32)
    for g in range(N_GROUPS):
        end_g = jnp.sum(jnp.where(lane1 == g, end, 0.0), axis=-1, keepdims=True)
        cg = cg + jnp.where(slot >= end_g, 1.0, 0.0)
    n_act = jnp.broadcast_to(end_g, (1, META_ROWS))
    meta_ref[...] = jnp.concatenate([perm, cg, n_act, jnp.zeros((5, META_ROWS), F32)], axis=0).astype(jnp.int32)


def _route(h2x):
    n = h2x.shape[0]
    n_tiles = n // MOE_TILE
    return pl.pallas_call(
        _route_kernel,
        grid=(n_tiles,),
        in_specs=[pl.BlockSpec((MOE_TILE, LANES), lambda i: (i, D_MODEL // LANES))],
        out_specs=pl.BlockSpec((None, 8, META_ROWS), lambda i: (i, 0, 0)),
        out_shape=jax.ShapeDtypeStruct((n_tiles, 8, META_ROWS), jnp.int32),
        compiler_params=_cparams(("parallel",)),
        name="moe_route",
    )(h2x)


def _moe_kernel(perm_ref, cg_ref, nact_ref, run_ref, h_ref, wg_hbm, wu_hbm, wd_hbm, y_ref,
                hs0, hs1, ys0, ys1, wg_buf, wu_buf, wd_buf, w_sem):
    i = pl.program_id(0)
    s = pl.program_id(1)
    n_act = nact_ref[i]
    tile_base = i * META_ROWS
    hs = (hs0, hs1)
    ys = (ys0, ys1)
    n_steps = pl.num_programs(0) * MOE_STEPS
    f = i * MOE_STEPS + s
    slot = run_ref[n_steps + f]

    def weight_copies(g, sl):
        e0 = g * EXPERTS_PER_GROUP
        r0 = g * (EXPERTS_PER_GROUP * EXPERT_FF)
        return (pltpu.make_async_copy(wg_hbm.at[pl.ds(e0, EXPERTS_PER_GROUP)], wg_buf.at[sl], w_sem.at[sl, 0]),
                pltpu.make_async_copy(wu_hbm.at[pl.ds(e0, EXPERTS_PER_GROUP)], wu_buf.at[sl], w_sem.at[sl, 1]),
                pltpu.make_async_copy(wd_hbm.at[pl.ds(r0, EXPERTS_PER_GROUP * EXPERT_FF)], wd_buf.at[sl],
                                      w_sem.at[sl, 2]))

    @pl.when(run_ref[f] == 1)
    def _():
        g = jnp.minimum(cg_ref[f], N_GROUPS - 1)

        @pl.when(run_ref[3 * n_steps + f] == 1)
        def _():
            for cp in weight_copies(g, slot):
                cp.start()

        for cp in weight_copies(g, slot):
            cp.wait()
        nxt = run_ref[2 * n_steps + f]

        @pl.when(nxt >= 0)
        def _():
            for cp in weight_copies(nxt, 1 - slot):
                cp.start()

    def gather(chunk, dst):
        base = tile_base + chunk * MOE_CHUNK
        for r in range(MOE_CHUNK):
            src = jnp.minimum(perm_ref[base + r], MOE_TILE - 1)
            dst[r:r + 1, :] = h_ref[pl.ds(src, 1), :]

    def scatter(chunk, src):
        base = tile_base + chunk * MOE_CHUNK
        for r in range(MOE_CHUNK):
            y_ref[pl.ds(perm_ref[base + r], 1), :] = src[r:r + 1, :]

    def ffn(src, dst):
        hb = src[:, 0:D_MODEL].astype(BF16)
        cw = src[:, D_MODEL:H2X_W]
        hid = []
        for e in range(EXPERTS_PER_GROUP):
            a = jnp.dot(hb, wg_buf[slot, e], preferred_element_type=F32)
            u = jnp.dot(hb, wu_buf[slot, e], preferred_element_type=F32)
            hid.append((_silu(a) * u * cw[:, e:e + 1]).astype(BF16))
        dst[...] = jnp.dot(jnp.concatenate(hid, axis=1), wd_buf[slot], preferred_element_type=F32)

    @pl.when(s == 0)
    def _():
        y_ref[MOE_TILE:MOE_TILE + 8, :] = jnp.zeros((8, D_MODEL), F32)
        ys1[...] = jnp.zeros_like(ys1)
        gather(0, hs0)

    for par in (0, 1):
        @pl.when((s < n_act) & (s % 2 == par))
        def _():
            gather(s + 1, hs[1 - par])
            ffn(hs[par], ys[par])
            scatter(jnp.maximum(s - 1, 0), ys[1 - par])

        @pl.when((s == n_act) & (s % 2 == par))
        def _():
            scatter(s - 1, ys[1 - par])


def _moe(h2x, perm, cgrp, nact, wg, wu, wd):
    n = h2x.shape[0]
    n_tiles = n // MOE_TILE
    n_steps = n_tiles * MOE_STEPS

    step = jnp.arange(n_steps)
    active = (step % MOE_STEPS) < jnp.repeat(nact, MOE_STEPS)
    prev = jnp.concatenate([jnp.full((1,), -1, jnp.int32), cgrp[:-1]])
    first = active & ((step % MOE_STEPS == 0) | (cgrp != prev))
    run_id = jnp.cumsum(first.astype(jnp.int32)) - 1
    later_first = lax.cummin(jnp.where(first, step, n_steps), reverse=True)
    nxt_step = jnp.concatenate([later_first[1:], jnp.full((1,), n_steps, jnp.int32)])
    nxt_group = jnp.where(nxt_step < n_steps, cgrp[jnp.minimum(nxt_step, n_steps - 1)], -1)
    runs = jnp.concatenate([first.astype(jnp.int32), run_id % 2, nxt_group.astype(jnp.int32),
                            (first & (run_id == 0)).astype(jnp.int32)])

    def h_idx(i, s, *_):
        done = (s >= jnp.maximum(_[2][i] - 1, 1)).astype(jnp.int32)
        return (jnp.minimum(i + done, n_tiles - 1), 0)

    hbm = pl.BlockSpec(memory_space=pl.ANY)
    return pl.pallas_call(
        _moe_kernel,
        grid_spec=pltpu.PrefetchScalarGridSpec(
            num_scalar_prefetch=4,
            grid=(n_tiles, MOE_STEPS),
            in_specs=[pl.BlockSpec((MOE_TILE, H2X_W), h_idx), hbm, hbm, hbm],
            out_specs=pl.BlockSpec((None, MOE_TILE + 8, D_MODEL), lambda i, s, *_: (i, 0, 0)),
            scratch_shapes=[pltpu.VMEM((MOE_CHUNK, H2X_W), F32), pltpu.VMEM((MOE_CHUNK, H2X_W), F32),
                            pltpu.VMEM((MOE_CHUNK, D_MODEL), F32), pltpu.VMEM((MOE_CHUNK, D_MODEL), F32),
                            pltpu.VMEM((2, EXPERTS_PER_GROUP, D_MODEL, EXPERT_FF), BF16),
                            pltpu.VMEM((2, EXPERTS_PER_GROUP, D_MODEL, EXPERT_FF), BF16),
                            pltpu.VMEM((2, EXPERTS_PER_GROUP * EXPERT_FF, D_MODEL), BF16),
                            pltpu.SemaphoreType.DMA((2, 3))]),
        out_shape=jax.ShapeDtypeStruct((n_tiles, MOE_TILE + 8, D_MODEL), F32),
        compiler_params=_cparams(("arbitrary", "arbitrary")),
        name="moe_experts",
    )(perm, cgrp, nact, runs, h2x, wg, wu, wd)


def _final_kernel(y_ref, x1_ref, mods_ref, fg_ref, o_ref):
    gate2 = mods_ref[:, 5 * D_MODEL:6 * D_MODEL]
    x2 = x1_ref[...] + gate2 * y_ref[...]
    o_ref[...] = x2 * lax.rsqrt(jnp.mean(x2 * x2, axis=-1, keepdims=True) + EPS) * fg_ref[...]


def _final(y, x1, mods3, mod_row0, mod_tokens, fg):
    n = x1.shape[0]
    tm = 512
    per_mod = mod_tokens // tm
    per_tile = MOE_TILE // tm
    return pl.pallas_call(
        _final_kernel,
        grid=(n // tm,),
        in_specs=[pl.BlockSpec((None, tm, D_MODEL), lambda j: (j // per_tile, j % per_tile, 0)),
                  pl.BlockSpec((tm, D_MODEL), lambda j: (j, 0)),
                  pl.BlockSpec((None, 1, 6 * D_MODEL), lambda j: (mod_row0 + j // per_mod, 0, 0)),
                  pl.BlockSpec((1, D_MODEL), lambda j: (0, 0))],
        out_specs=pl.BlockSpec((tm, D_MODEL), lambda j: (j, 0)),
        out_shape=jax.ShapeDtypeStruct((n, D_MODEL), F32),
        compiler_params=_cparams(("parallel",)),
        name="final_norm",
    )(y, x1, mods3, fg)


def _rope_tables(t):
    n_freq = QK_DIM // 4
    n_rows = t // GRID_W
    freqs = ROPE_BASE ** (-jnp.arange(n_freq, dtype=F32) / n_freq)
    ang_r = jnp.arange(n_rows, dtype=F32)[:, None] * freqs
    ang_c = jnp.arange(GRID_W, dtype=F32)[:, None] * freqs
    cr, sr, cc, sc = lax.optimization_barrier((jnp.cos(ang_r), jnp.sin(ang_r), jnp.cos(ang_c), jnp.sin(ang_c)))
    j = np.arange(LANES) % QK_DIM
    f_idx = j % n_freq
    by_row = (j < QK_DIM // 2)[None, None, :]
    first = ((j % (QK_DIM // 2)) < n_freq)[None, None, :]

    def table(r_small, c_small):
        return jnp.where(by_row, r_small[:, f_idx][:, None, :], c_small[:, f_idx][None, :, :])

    cos = table(cr, cc)
    sin = table(sr, sc)
    return (cos.reshape(t, LANES), jnp.where(first, -sin, 0.0).reshape(t, LANES),
            jnp.where(first, 0.0, sin).reshape(t, LANES))


def _layer(x, mods3, mod_row0, mod_tokens, rope_tabs, ctx_k, ctx_v, h0f, h0b, lw, layer):
    b, t, _ = x.shape
    n = b * t
    xf = x.reshape(n, D_MODEL)
    res = _inproj(xf, mods3, mod_row0, mod_tokens, t, lw["norm_mix_g"], lw["w_main"], lw["w_dt"], lw["alog"],
                  lw["dtb"], rope_tabs)
    q, kb, vt, z, xbc, cols, hrow = res[:7]
    if ctx_k is None:
        cache = None
        k3 = res[7].reshape(b, 1, t, ATT_HEADS, 2, LANES)[..., :QK_DIM]
        v3 = res[8].reshape(b, 1, t, ATT_HEADS, V_DIM)
    else:
        cache = (ctx_k.astype(BF16), jnp.swapaxes(ctx_v, 1, 2).astype(BF16))
        k3 = v3 = None
    lam0 = 0.8 - 0.6 * math.exp(-0.3 * layer)
    tq = 512 if t % 512 == 0 else 256
    tk = 512 if t % 512 == 0 else 256
    att = _attention(q.reshape(b, t, D_QK), kb.reshape(b, t, D_QK), vt, cache, lw["lamp"], lw["attn_subln_g"],
                     lam0, tq, tk, *((ATT_HEADS, 2) if t <= 512 else (1, 1)))
    ssm, hf, hb = _ssd(xbc, cols.reshape(b, t, LANES), hrow.reshape(b, t // CHUNK * HROWS, LANES),
                       z.reshape(b, t, D_SSM), h0f, h0b, lw["conv_w"], lw["conv_b"], lw["d_x"], lw["ssm_norm_g"])
    x1, h2x = _outproj(att.reshape(n, D_ATT), ssm.reshape(n, D_SSM), xf, mods3, mod_row0, mod_tokens,
                       lw["wo_att"], lw["wo_ssm"], lw["norm_ffn_g"], lw["w_router"], lw["b_router"])
    meta = _route(h2x)
    perm = meta[:, 0, :].reshape(-1)
    cgrp = meta[:, 1, :MOE_STEPS].reshape(-1)
    nact = meta[:, 2, 0]
    y = _moe(h2x, perm, cgrp, nact, lw["wg"], lw["wu"], lw["wd"])
    out = _final(y, x1, mods3, mod_row0, mod_tokens, lw["final_g"])
    return out.reshape(b, t, D_MODEL), k3, v3, hf, hb


def _pad_lanes(v, width=LANES):
    return jnp.pad(v, [(0, 0)] * (v.ndim - 1) + [(0, width - v.shape[-1])])


def kernel(x_prompt, x_sample, cache_k, cache_v, state_ssm_fwd, state_ssm_bwd, c, c_ctx, w_ada, b_ada, norm_mix_g, w_in, w_out, lambda_q1, lambda_k1, lambda_q2, lambda_k2, attn_subln_g, conv_w, conv_b, a_log_fwd, a_log_bwd, dt_bias_fwd, dt_bias_bwd, ssm_d, ssm_norm_g, norm_ffn_g, w_group_router, b_group_router, w_expert_router, b_expert_router, w_exp_gate, w_exp_up, w_exp_down, final_norm_g):
    depth = w_in.shape[0]
    assert depth == 1, "single trunk layer"
    bp, tp, _ = x_prompt.shape
    bs, ts, _ = x_sample.shape
    l = 0
    cond = jnp.concatenate([c_ctx[None], c], axis=0)
    condT = _pad_lanes(cond.T, 8)
    mods = _ada(condT, w_ada[l], b_ada[l][None])
    mods3 = mods.reshape(8, 1, 6 * D_MODEL)

    w_router = _pad_lanes(jnp.concatenate([w_group_router[l], w_expert_router[l]], axis=1))
    wr_hi = w_router.astype(BF16)
    wr_lo = (w_router - wr_hi.astype(F32)).astype(BF16)
    lw = dict(
        norm_mix_g=norm_mix_g[l][None],
        w_main=w_in[l].astype(BF16),
        w_dt=_pad_lanes(w_in[l][:, MAIN_COLS:]).astype(BF16),
        lamp=jnp.stack([lambda_q1[l], lambda_k1[l], lambda_q2[l], lambda_k2[l]]),
        attn_subln_g=attn_subln_g[l][None],
        conv_w=conv_w[l].reshape(CONV_W, N_COLBLK, LANES), conv_b=conv_b[l].reshape(N_COLBLK, LANES),
        alog=jnp.broadcast_to(jnp.concatenate([a_log_fwd[l], a_log_bwd[l]])[:, None], (2 * SSM_HEADS, CHUNK)),
        dtb=jnp.broadcast_to(jnp.concatenate([dt_bias_fwd[l], dt_bias_bwd[l]])[:, None], (2 * SSM_HEADS, CHUNK)),
        d_x=jnp.repeat(ssm_d[l], SSM_HEADDIM)[None], ssm_norm_g=ssm_norm_g[l][None],
        wo_att=w_out[l][:D_ATT].astype(BF16), wo_ssm=w_out[l][D_ATT:].astype(BF16),
        norm_ffn_g=norm_ffn_g[l][None],
        w_router=jnp.concatenate([wr_hi, wr_lo], axis=1),
        b_router=_pad_lanes(jnp.concatenate([b_group_router[l], b_expert_router[l]])[None]),
        wg=w_exp_gate[l].astype(BF16), wu=w_exp_up[l].astype(BF16), wd=w_exp_down[l].astype(BF16).reshape(N_EXPERTS * EXPERT_FF, D_MODEL),
        final_g=final_norm_g[None],
    )
    n_state = SSM_HEADS * SSM_HEADDIM
    zeros_state = jnp.zeros((bp, n_state, D_STATE), F32)
    yp, ck, cv, hf, hb = _layer(x_prompt, mods3, 0, bp * tp, None, None, None, zeros_state, zeros_state, lw, l)
    ys, _, _, _, _ = _layer(x_sample, mods3, 1, ts, _rope_tables(ts),
                            cache_k[:, l].reshape(bs, -1, D_QK), cache_v[:, l].reshape(bs, -1, D_ATT),
                            state_ssm_fwd[:, l].reshape(bs, n_state, D_STATE),
                            state_ssm_bwd[:, l].reshape(bs, n_state, D_STATE), lw, l)
    new_k, new_v = ck, cv
    new_hf = hf.reshape(bp, 1, SSM_HEADS, SSM_HEADDIM, D_STATE)
    new_hb = hb.reshape(bp, 1, SSM_HEADS, SSM_HEADDIM, D_STATE)
    return yp, ys, new_k, new_v, new_hf, new_hb
```

```python
import functools
import math

import numpy as np
import jax
import jax.numpy as jnp
from jax import lax
from jax.experimental import pallas as pl
from jax.experimental.pallas import tpu as pltpu

D_MODEL = 1024
GRID_W = 64
ATT_HEADS = 4
QK_DIM = 64
V_DIM = 128
D_QK = 512
D_ATT = 512
ROPE_BASE = 10000.0
D_SSM = 512
SSM_HEADDIM = 64
SSM_HEADS = 8
SSM_GROUPS = 2
D_STATE = 128
CONV_W = 5
CHUNK = 128
XBC_DIM = 1024
N_GROUPS = 4
EXPERTS_PER_GROUP = 4
N_EXPERTS = 16
EXPERT_FF = 256
EPS = 1e-6
MAIN_COLS = 2 * D_QK + D_ATT + D_SSM + XBC_DIM
H2X_W = D_MODEL + 128
MOE_TILE = 2048
MOE_CHUNK = 256
MOE_SLOTS = MOE_TILE // MOE_CHUNK + N_GROUPS
MOE_ROWS = MOE_SLOTS * MOE_CHUNK
MOE_STEPS = MOE_SLOTS + 1
META_ROWS = MOE_STEPS * MOE_CHUNK
COL_CUM, COL_DT, COL_ECUM, COL_TOEND = 0, 16, 32, 48
HROWS = 32
LANES = 128
HALO = 8
VMEM_LIMIT = 56 * 1024 * 1024

LOG2E = math.log2(math.e)
SUM_ROWS = 16
F32 = jnp.float32
BF16 = jnp.bfloat16


def _cparams(sem):
    return pltpu.CompilerParams(dimension_semantics=sem, vmem_limit_bytes=VMEM_LIMIT)


def _sigmoid(x):
    return 1.0 / (1.0 + jnp.exp(-x))


def _silu(x):
    return x * _sigmoid(x)


def _ada_kernel(condT_ref, w_ref, b_ref, o_ref):
    s = _silu(condT_ref[...])
    w = w_ref[...]
    b = b_ref[...]
    o_ref[...] = jnp.zeros_like(o_ref)
    for r in range(3):
        o_ref[r:r + 1, :] = jnp.sum(w * s[:, r:r + 1], axis=0, keepdims=True) + b


def _ada(condT, w_ada, b_ada):
    bn = 1024
    n = w_ada.shape[1]
    return pl.pallas_call(
        _ada_kernel,
        grid=(n // bn,),
        in_specs=[pl.BlockSpec((D_MODEL, 8), lambda j: (0, 0)),
                  pl.BlockSpec((D_MODEL, bn), lambda j: (0, j)),
                  pl.BlockSpec((1, bn), lambda j: (0, j))],
        out_specs=pl.BlockSpec((8, bn), lambda j: (0, j)),
        out_shape=jax.ShapeDtypeStruct((8, n), F32),
        compiler_params=_cparams(("arbitrary",)),
        name="ada",
    )(condT, w_ada, b_ada)


def _split3(x):
    hi = x.astype(BF16)
    r1 = x - hi.astype(F32)
    mid = r1.astype(BF16)
    lo = (r1 - mid.astype(F32)).astype(BF16)
    return jnp.concatenate([hi, mid, lo], axis=-1)


def _scan_matrices():
    t = np.arange(CHUNK)
    pre = (t[:, None] <= t[None, :]).astype(np.float32)
    suf = (t[:, None] >= t[None, :]).astype(np.float32)
    return jnp.asarray(np.stack([np.concatenate([pre] * 3, axis=0), np.concatenate([suf] * 3, axis=0)]), BF16)


def _head_scalars(dt_raw, alog_ref, dtb_ref, scan_ref):
    nh2 = 2 * SSM_HEADS
    xv = dt_raw.T[0:nh2, :] + dtb_ref[...]
    dt = jnp.maximum(xv, 0.0) + jnp.log(1.0 + jnp.exp(-jnp.abs(xv)))
    la3 = _split3(dt * (-jnp.exp(alog_ref[...])))
    fwd = lax.broadcasted_iota(jnp.int32, (nh2, CHUNK), 0) < SSM_HEADS
    cum = jnp.where(fwd, jnp.dot(la3, scan_ref[0], preferred_element_type=F32),
                    jnp.dot(la3, scan_ref[1], preferred_element_type=F32))
    cum_end = jnp.where(fwd, cum[:, CHUNK - 1:CHUNK], cum[:, 0:1])
    packed = jnp.concatenate([cum, dt, jnp.exp(cum), jnp.exp(cum_end - cum),
                              jnp.zeros((LANES - 4 * nh2, CHUNK), F32)], axis=0)
    return packed.T, jnp.concatenate([cum, jnp.exp(cum_end)], axis=0)


def _inproj_kernel(rope, x_ref, mods_ref, g_ref, w_ref, wdt_ref, alog_ref, dtb_ref, scan_ref, *rest):
    if rope:
        cos_ref, sa_ref, sb_ref, q_ref, kb_ref, vt_ref, z_ref, xbc_ref, cols_ref, hrow_ref = rest
    else:
        q_ref, kb_ref, vt_ref, z_ref, xbc_ref, cols_ref, hrow_ref, kf_ref, vf_ref = rest
    x = x_ref[...]
    shift = mods_ref[:, 0:D_MODEL]
    scale = mods_ref[:, D_MODEL:2 * D_MODEL]
    y = x * lax.rsqrt(jnp.mean(x * x, axis=-1, keepdims=True) + EPS) * g_ref[...]
    h = (y * (1.0 + scale) + shift).astype(BF16)
    dt_raw = jnp.dot(h, wdt_ref[...], preferred_element_type=F32)
    for ci in range(x.shape[0] // CHUNK):
        cols, hrow = _head_scalars(dt_raw[ci * CHUNK:(ci + 1) * CHUNK, :], alog_ref, dtb_ref, scan_ref)
        cols_ref[ci * CHUNK:(ci + 1) * CHUNK, :] = cols
        hrow_ref[ci * HROWS:(ci + 1) * HROWS, :] = hrow
    r = jnp.dot(h, w_ref[...], preferred_element_type=F32)
    q = r[:, 0:D_QK]
    k = r[:, D_QK:2 * D_QK]
    if rope:
        cos = cos_ref[...]
        sa = sa_ref[...]
        sb = sb_ref[...]

        def rot(t):
            parts = []
            for hh in range(ATT_HEADS):
                th = t[:, hh * LANES:(hh + 1) * LANES]
                parts.append(th * cos + pltpu.roll(th, LANES - 16, 1) * sa + pltpu.roll(th, 16, 1) * sb)
            return jnp.concatenate(parts, axis=1)

        q = rot(q)
        k = rot(k)
    v = r[:, 2 * D_QK:2 * D_QK + D_ATT]
    q_ref[...] = q
    kb_ref[...] = k.astype(BF16)
    n_seq, _, t_seq = vt_ref.shape
    v_t = v.T.astype(BF16)
    for sq in range(n_seq):
        vt_ref[sq] = v_t[:, sq * t_seq:(sq + 1) * t_seq]
    if not rope:
        tm = k.shape[0]
        for hh in range(ATT_HEADS):
            k_h = k[:, hh * LANES:(hh + 1) * LANES]
            vf_ref[pl.ds(hh, tm, stride=ATT_HEADS), :] = v[:, hh * LANES:(hh + 1) * LANES]
            kf_ref[pl.ds(2 * hh, tm, stride=2 * ATT_HEADS), :] = k_h
            kf_ref[pl.ds(2 * hh + 1, tm, stride=2 * ATT_HEADS), :] = pltpu.roll(k_h, QK_DIM, 1)
    z_ref[...] = r[:, 2 * D_QK + D_ATT:2 * D_QK + D_ATT + D_SSM]
    x0 = 2 * D_QK + D_ATT + D_SSM
    for cb in range(XBC_DIM // LANES):
        for sq in range(n_seq):
            xbc_ref[sq, cb] = r[sq * t_seq:(sq + 1) * t_seq, x0 + cb * LANES:x0 + (cb + 1) * LANES]


def _inproj(x, mods3, mod_row0, mod_tokens, seq_len, g, w_main, w_dt, alog, dtb, rope_tabs):
    n = x.shape[0]
    tm = 512
    assert seq_len % tm == 0 or tm % seq_len == 0
    per_seq = max(seq_len // tm, 1)
    n_seq = max(tm // seq_len, 1)
    t_seq = tm // n_seq
    per_mod = mod_tokens // tm
    rope = rope_tabs is not None
    in_specs = [pl.BlockSpec((tm, D_MODEL), lambda i: (i, 0)),
                pl.BlockSpec((None, 1, 6 * D_MODEL), lambda i: (mod_row0 + i // per_mod, 0, 0)),
                pl.BlockSpec((1, D_MODEL), lambda i: (0, 0)),
                pl.BlockSpec((D_MODEL, MAIN_COLS), lambda i: (0, 0)),
                pl.BlockSpec((D_MODEL, LANES), lambda i: (0, 0)),
                pl.BlockSpec((2 * SSM_HEADS, CHUNK), lambda i: (0, 0)),
                pl.BlockSpec((2 * SSM_HEADS, CHUNK), lambda i: (0, 0)),
                pl.BlockSpec((2, 3 * CHUNK, CHUNK), lambda i: (0, 0, 0))]
    args = [x, mods3, g, w_main, w_dt, alog, dtb, _scan_matrices()]
    if rope:
        tab_spec = pl.BlockSpec((tm, LANES), lambda i: (i % per_seq, 0))
        in_specs += [tab_spec] * 3
        args += list(rope_tabs)
    def rows(wd, dtype=F32):
        return pl.BlockSpec((tm, wd), lambda i: (i, 0)), jax.ShapeDtypeStruct((n, wd), dtype)

    hr = tm // CHUNK * HROWS
    outs = [rows(D_QK), rows(D_QK, BF16),
            (pl.BlockSpec((n_seq, D_ATT, t_seq), lambda i: (i // per_seq, 0, i % per_seq)),
             jax.ShapeDtypeStruct((n // seq_len, D_ATT, seq_len), BF16)),
            rows(D_SSM),
            (pl.BlockSpec((n_seq, XBC_DIM // LANES, t_seq, LANES), lambda i: (i // per_seq, 0, i % per_seq, 0)),
             jax.ShapeDtypeStruct((n // seq_len, XBC_DIM // LANES, seq_len, LANES), F32)),
            rows(LANES),
            (pl.BlockSpec((hr, LANES), lambda i: (i, 0)), jax.ShapeDtypeStruct((n // CHUNK * HROWS, LANES), F32))]
    if not rope:
        outs += [(pl.BlockSpec((tm * 2 * ATT_HEADS, LANES), lambda i: (i, 0)),
                  jax.ShapeDtypeStruct((n * 2 * ATT_HEADS, LANES), F32)),
                 (pl.BlockSpec((tm * ATT_HEADS, LANES), lambda i: (i, 0)),
                  jax.ShapeDtypeStruct((n * ATT_HEADS, LANES), F32))]
    return pl.pallas_call(
        functools.partial(_inproj_kernel, rope),
        grid=(n // tm,),
        in_specs=in_specs,
        out_specs=[o[0] for o in outs],
        out_shape=[o[1] for o in outs],
        compiler_params=_cparams(("parallel",)),
        name="inproj_rope" if rope else "inproj",
    )(*args)


def _attn_kernel(tk, lam0, has_cache, q_ref, k_ref, vt_ref, *rest):
    if has_cache:
        ck_ref, cvt_ref, lamp_ref, g_ref, o_ref = rest
    else:
        lamp_ref, g_ref, o_ref = rest
    tq = q_ref.shape[1]
    lp = lamp_ref[...]
    lam = (jnp.exp(jnp.sum(lp[0:1] * lp[1:2], axis=-1, keepdims=True))
           - jnp.exp(jnp.sum(lp[2:3] * lp[3:4], axis=-1, keepdims=True)) + lam0)
    ones_rows = jnp.ones((SUM_ROWS, tk), BF16)
    for bb, hh in [(b_, h_) for b_ in range(q_ref.shape[0]) for h_ in range(q_ref.shape[2] // LANES)]:
        hs = slice(hh * LANES, (hh + 1) * LANES)
        chunks = [(k_ref, vt_ref, c * tk) for c in range(k_ref.shape[1] // tk)]
        if has_cache:
            chunks += [(ck_ref, cvt_ref, c * tk) for c in range(ck_ref.shape[1] // tk)]
        q = q_ref[bb, :, hs] * (QK_DIM ** -0.5 * LOG2E)
        lane = lax.broadcasted_iota(jnp.int32, q.shape, 1)
        qq_t = jnp.concatenate([jnp.where(lane < QK_DIM, q, 0.0), jnp.where(lane >= QK_DIM, q, 0.0)],
                               axis=0).T.astype(BF16)

        def scores(chunk):
            kr, _, start = chunk
            return jnp.dot(kr[bb, start:start + tk, hs], qq_t, preferred_element_type=F32)

        def update(s, chunk, m, acc):
            _, vr, start = chunk
            m_new = jnp.maximum(m, jnp.max(s, axis=0, keepdims=True))
            alpha = jnp.exp2(m - m_new)
            p = jnp.exp2(s - m_new).astype(BF16)
            v_ext = jnp.concatenate([vr[bb, hs, start:start + tk], ones_rows], axis=0)
            acc = alpha * acc + jnp.dot(v_ext, p, preferred_element_type=F32)
            return m_new, acc

        m = jnp.full((1, 2 * tq), -jnp.inf, F32)
        acc = jnp.zeros((V_DIM + SUM_ROWS, 2 * tq), F32)
        s = scores(chunks[0])
        for c, chunk in enumerate(chunks):
            s_next = scores(chunks[c + 1]) if c + 1 < len(chunks) else None
            m, acc = update(s, chunk, m, acc)
            s = s_next
        o = acc[0:V_DIM] / acc[V_DIM:V_DIM + 1]
        o = (o[:, 0:tq] - lam * o[:, tq:2 * tq]).T
        o = o * lax.rsqrt(jnp.mean(o * o, axis=-1, keepdims=True) + EPS)
        o_ref[bb, :, hs] = o * g_ref[...] * (1.0 - lam0)


def _attention(q, k, vt, cache, lamp, g, lam0, tq, tk, nh, nb):
    b, t, _ = q.shape
    wd = nh * LANES

    def kv_specs(length):
        return [pl.BlockSpec((nb, length, wd), lambda bi, h, i: (bi, 0, h)),
                pl.BlockSpec((nb, wd, length), lambda bi, h, i: (bi, h, 0))]

    in_specs = [pl.BlockSpec((nb, tq, wd), lambda bi, h, i: (bi, i, h))] + kv_specs(t)
    args = [q, k, vt]
    if cache is not None:
        assert cache[0].shape[1] % tk == 0
        in_specs += kv_specs(cache[0].shape[1])
        args += list(cache)
    in_specs += [pl.BlockSpec((4, QK_DIM), lambda bi, h, i: (0, 0)),
                 pl.BlockSpec((1, V_DIM), lambda bi, h, i: (0, 0))]
    return pl.pallas_call(
        functools.partial(_attn_kernel, tk, lam0, cache is not None),
        grid=(b // nb, ATT_HEADS // nh, t // tq),
        in_specs=in_specs,
        out_specs=pl.BlockSpec((nb, tq, wd), lambda bi, h, i: (bi, i, h)),
        out_shape=jax.ShapeDtypeStruct((b, t, D_ATT), F32),
        compiler_params=_cparams(("parallel", "parallel", "arbitrary")),
        name="diff_attn",
    )(*args, lamp, g)


N_COLBLK = XBC_DIM // LANES
def _expand_matrices():
    out = []
    for d in range(2):
        e = np.zeros((LANES, 3 * D_SSM), np.float32)
        for blk, lane0 in enumerate((COL_DT, COL_ECUM, COL_TOEND)):
            for h in range(SSM_HEADS):
                e[lane0 + d * SSM_HEADS + h,
                  blk * D_SSM + h * SSM_HEADDIM:blk * D_SSM + (h + 1) * SSM_HEADDIM] = 1.0
        out.append(np.concatenate([e, e, e], axis=0))
    return jnp.asarray(np.stack(out), BF16)


def _ssd_chunk(reverse, xs, bm, cm, state, hrow, cols, e_ref):
    d0 = SSM_HEADS if reverse else 0
    cum = hrow[d0:d0 + SSM_HEADS, :]
    dec = hrow[2 * SSM_HEADS + d0:2 * SSM_HEADS + d0 + SSM_HEADS, 0:1]
    row = lax.broadcasted_iota(jnp.int32, (CHUNK, LANES), 0)
    lane = lax.broadcasted_iota(jnp.int32, (CHUNK, LANES), 1)
    causal = (row <= lane) if reverse else (row >= lane)
    lane_g = lax.broadcasted_iota(jnp.int32, (CHUNK, 2 * LANES), 1)
    spread = jnp.dot(_split3(cols), e_ref[1 if reverse else 0], preferred_element_type=F32)
    xd = xs * spread[:, 0:D_SSM]
    xdw = (xd * spread[:, 2 * D_SSM:3 * D_SSM]).astype(BF16)
    xd = xd.astype(BF16)
    ecum_x = spread[:, D_SSM:2 * D_SSM]
    rep = SSM_HEADS // SSM_GROUPS
    y_parts = []
    new_state = []
    for g in range(SSM_GROUPS):
        bg = bm[:, g * D_STATE:(g + 1) * D_STATE]
        cg = cm[:, g * D_STATE:(g + 1) * D_STATE]
        cbt = lax.dot_general(cg, bg, (((1,), (1,)), ((), ())), preferred_element_type=F32)
        rows = slice(g * rep * SSM_HEADDIM, (g + 1) * rep * SSM_HEADDIM)
        st_g = state[rows, :]
        y_off = lax.dot_general(cg, st_g.astype(BF16), (((1,), (1,)), ((), ())),
                                preferred_element_type=F32)
        cst = lax.dot_general(xdw[:, rows], bg, (((0,), (0,)), ((), ())), preferred_element_type=F32)
        xd_g = xd[:, rows]
        scs = []
        blocks = []
        for hh in range(rep):
            h = g * rep + hh
            seg = cols[:, COL_CUM + d0 + h:COL_CUM + d0 + h + 1] - cum[h:h + 1, :]
            decay = jnp.exp(jnp.where(causal, seg, -jnp.inf))
            scs.append((cbt * decay).astype(BF16))
            blocks.append(jnp.where(lane_g // SSM_HEADDIM == hh, xd_g, jnp.zeros_like(xd_g)))
        y_diag = jnp.dot(jnp.concatenate(scs, axis=1), jnp.concatenate(blocks, axis=0),
                         preferred_element_type=F32)
        dec_rows = jnp.concatenate(
            [jnp.broadcast_to(dec[g * rep + hh:g * rep + hh + 1, :], (SSM_HEADDIM, D_STATE)) for hh in range(rep)],
            axis=0)
        new_state.append(st_g * dec_rows + cst)
        y_parts.append(y_diag + y_off * ecum_x[:, rows])
    return jnp.concatenate(y_parts, axis=1), jnp.concatenate(new_state, axis=0)


def _ssd_kernel(nc, xc_ref, xp_ref, xn_ref, cols_ref, hrow_ref, z_ref, h0f_ref, h0b_ref, cw_ref, cb_ref,
                d_ref, ng_ref, e_ref, y_ref, hf_ref, hb_ref, state_ref, slab_ref, xs_ref, bc_ref, yf_ref):
    ps = pl.program_id(1)
    c = pl.program_id(2)
    n_bc = SSM_GROUPS * D_STATE

    @pl.when((ps == 0) & (c == 0))
    def _():
        state_ref[...] = h0f_ref[...]

    @pl.when(ps == 0)
    def _():
        t0 = pl.multiple_of(c * CHUNK, CHUNK)
        for bb in range(state_ref.shape[0]):
            slab_ref[bb, :, 0:HALO, :] = jnp.where(c > 0, xp_ref[bb], 0.0)
            slab_ref[bb, :, HALO:HALO + CHUNK, :] = xc_ref[bb]
            slab_ref[bb, :, HALO + CHUNK:2 * HALO + CHUNK, :] = jnp.where(c < nc - 1, xn_ref[bb], 0.0)
            blocks = []
            for cb in range(N_COLBLK):
                conv = cb_ref[cb:cb + 1, :]
                for kk in range(CONV_W):
                    off = HALO - CONV_W // 2 + kk
                    conv = conv + slab_ref[bb, cb, off:off + CHUNK, :] * cw_ref[kk, cb:cb + 1, :]
                blocks.append(_silu(conv))
            xs = jnp.concatenate(blocks[0:D_SSM // LANES], axis=1)
            bc = jnp.concatenate(blocks[D_SSM // LANES:], axis=1).astype(BF16)
            y, state = _ssd_chunk(False, xs, bc[:, 0:n_bc], bc[:, n_bc:], state_ref[bb], hrow_ref[bb],
                                  cols_ref[bb], e_ref)
            xs_ref[bb, pl.ds(t0, CHUNK), :] = xs
            bc_ref[bb, pl.ds(t0, CHUNK), :] = bc
            yf_ref[bb, pl.ds(t0, CHUNK), :] = y
            state_ref[bb] = state

    @pl.when((ps == 0) & (c == nc - 1))
    def _():
        hf_ref[...] = state_ref[...]
        state_ref[...] = h0b_ref[...]

    @pl.when(ps == 1)
    def _():
        t0 = pl.multiple_of((nc - 1 - c) * CHUNK, CHUNK)
        for bb in range(state_ref.shape[0]):
            xs = xs_ref[bb, pl.ds(t0, CHUNK), :]
            bc = bc_ref[bb, pl.ds(t0, CHUNK), :]
            y, state = _ssd_chunk(True, xs, bc[:, 0:n_bc], bc[:, n_bc:], state_ref[bb], hrow_ref[bb],
                                  cols_ref[bb], e_ref)
            state_ref[bb] = state
            y = y + yf_ref[bb, pl.ds(t0, CHUNK), :] + xs * d_ref[...]
            y = y * _silu(z_ref[bb])
            y = y * lax.rsqrt(jnp.mean(y * y, axis=-1, keepdims=True) + EPS)
            y_ref[bb] = y * ng_ref[...]

    @pl.when((ps == 1) & (c == nc - 1))
    def _():
        hb_ref[...] = state_ref[...]


def _ssd(xbc, cols, hrow, z, h0f, h0b, conv_w, conv_b, d_x, norm_g):
    b, _, l, _ = xbc.shape
    nb = 2
    assert b % nb == 0
    nc = l // CHUNK
    hb = CHUNK // HALO
    n_state = SSM_HEADS * SSM_HEADDIM
    last = nc - 1

    def fwd_chunk(ps, ci):
        return jnp.where(ps == 0, ci, last)

    def any_chunk(ps, ci):
        return jnp.where(ps == 0, ci, last - ci)

    def bwd_chunk(ps, ci):
        return jnp.where(ps == 0, last, last - ci)

    const2 = lambda bi, ps, ci: (0, 0)
    state_spec = pl.BlockSpec((nb, n_state, D_STATE), lambda bi, ps, ci: (bi, 0, 0))
    in_specs = [pl.BlockSpec((nb, N_COLBLK, CHUNK, LANES), lambda bi, ps, ci: (bi, 0, fwd_chunk(ps, ci), 0)),
                pl.BlockSpec((nb, N_COLBLK, HALO, LANES),
                             lambda bi, ps, ci: (bi, 0, jnp.maximum(fwd_chunk(ps, ci) * hb - 1, 0), 0)),
                pl.BlockSpec((nb, N_COLBLK, HALO, LANES),
                             lambda bi, ps, ci: (bi, 0, jnp.minimum((fwd_chunk(ps, ci) + 1) * hb, l // HALO - 1), 0)),
                pl.BlockSpec((nb, CHUNK, LANES), lambda bi, ps, ci: (bi, any_chunk(ps, ci), 0)),
                pl.BlockSpec((nb, HROWS, LANES), lambda bi, ps, ci: (bi, any_chunk(ps, ci), 0)),
                pl.BlockSpec((nb, CHUNK, D_SSM), lambda bi, ps, ci: (bi, bwd_chunk(ps, ci), 0)),
                state_spec, state_spec,
                pl.BlockSpec((CONV_W, N_COLBLK, LANES), lambda bi, ps, ci: (0, 0, 0)),
                pl.BlockSpec((N_COLBLK, LANES), const2),
                pl.BlockSpec((1, D_SSM), const2),
                pl.BlockSpec((1, D_SSM), const2),
                pl.BlockSpec((2, 3 * LANES, 3 * D_SSM), lambda bi, ps, ci: (0, 0, 0))]
    return pl.pallas_call(
        functools.partial(_ssd_kernel, nc),
        grid=(b // nb, 2, nc),
        in_specs=in_specs,
        out_specs=[pl.BlockSpec((nb, CHUNK, D_SSM), lambda bi, ps, ci: (bi, bwd_chunk(ps, ci), 0)),
                   state_spec, state_spec],
        out_shape=[jax.ShapeDtypeStruct((b, l, D_SSM), F32),
                   jax.ShapeDtypeStruct((b, n_state, D_STATE), F32),
                   jax.ShapeDtypeStruct((b, n_state, D_STATE), F32)],
        scratch_shapes=[pltpu.VMEM((nb, n_state, D_STATE), F32),
                        pltpu.VMEM((nb, N_COLBLK, CHUNK + 2 * HALO, LANES), F32),
                        pltpu.VMEM((nb, l, D_SSM), F32),
                        pltpu.VMEM((nb, l, 2 * SSM_GROUPS * D_STATE), BF16),
                        pltpu.VMEM((nb, l, D_SSM), F32)],
        compiler_params=_cparams(("parallel", "arbitrary", "arbitrary")),
        name="ssd",
    )(xbc, xbc, xbc, cols, hrow, z, h0f, h0b, conv_w, conv_b, d_x, norm_g, _expand_matrices())


def _outproj_kernel(att_ref, ssm_ref, x_ref, mods_ref, wa_ref, ws_ref, g_ref, wr_ref, br_ref,
                    x1_ref, h2x_ref):
    mix = (jnp.dot(att_ref[...].astype(BF16), wa_ref[...], preferred_element_type=F32)
           + jnp.dot(ssm_ref[...].astype(BF16), ws_ref[...], preferred_element_type=F32))
    gate1 = mods_ref[:, 2 * D_MODEL:3 * D_MODEL]
    shift2 = mods_ref[:, 3 * D_MODEL:4 * D_MODEL]
    scale2 = mods_ref[:, 4 * D_MODEL:5 * D_MODEL]
    x1 = x_ref[...] + gate1 * mix
    x1_ref[...] = x1
    y = x1 * lax.rsqrt(jnp.mean(x1 * x1, axis=-1, keepdims=True) + EPS) * g_ref[...]
    h2 = y * (1.0 + scale2) + shift2
    h_hi = h2.astype(BF16)
    h2x_ref[:, 0:D_MODEL] = h2
    h_lo = (h2 - h_hi.astype(F32)).astype(BF16)
    both = jnp.dot(h_hi, wr_ref[...], preferred_element_type=F32)
    logits = (both[:, 0:LANES] + both[:, LANES:2 * LANES]
              + jnp.dot(h_lo, wr_ref[:, 0:LANES], preferred_element_type=F32)) + br_ref[...]
    lane = lax.broadcasted_iota(jnp.int32, logits.shape, 1).astype(F32)
    neg = -jnp.inf
    big = float(1 << 20)
    is_g = lane < N_GROUPS
    gl = jnp.where(is_g, logits, neg)
    gmax = jnp.max(gl, axis=-1, keepdims=True)
    g_idx = jnp.min(jnp.where(gl == gmax, lane, big), axis=-1, keepdims=True)
    p_g = 1.0 / jnp.sum(jnp.where(is_g, jnp.exp(gl - gmax), 0.0), axis=-1, keepdims=True)
    e_lo = N_GROUPS + g_idx * EXPERTS_PER_GROUP
    in_grp = (lane >= e_lo) & (lane < e_lo + EXPERTS_PER_GROUP)
    el = jnp.where(in_grp, logits, neg)
    m1 = jnp.max(el, axis=-1, keepdims=True)
    i1 = jnp.min(jnp.where(el == m1, lane, big), axis=-1, keepdims=True)
    el2 = jnp.where(lane == i1, neg, el)
    m2 = jnp.max(el2, axis=-1, keepdims=True)
    i2 = jnp.min(jnp.where(el2 == m2, lane, big), axis=-1, keepdims=True)
    e2 = jnp.exp(m2 - m1)
    w1 = p_g / (1.0 + e2)
    w2 = p_g * e2 / (1.0 + e2)
    slab = (jnp.where(lane == i1 - e_lo, w1, 0.0) + jnp.where(lane == i2 - e_lo, w2, 0.0)
            + jnp.where(lane == EXPERTS_PER_GROUP, g_idx, 0.0))
    h2x_ref[:, D_MODEL:H2X_W] = slab


def _outproj(att, ssm, x, mods3, mod_row0, mod_tokens, wo_att, wo_ssm, g, w_router, b_router):
    n = x.shape[0]
    tm = 512
    per_mod = mod_tokens // tm
    return pl.pallas_call(
        _outproj_kernel,
        grid=(n // tm,),
        in_specs=[pl.BlockSpec((tm, D_ATT), lambda i: (i, 0)),
                  pl.BlockSpec((tm, D_SSM), lambda i: (i, 0)),
                  pl.BlockSpec((tm, D_MODEL), lambda i: (i, 0)),
                  pl.BlockSpec((None, 1, 6 * D_MODEL), lambda i: (mod_row0 + i // per_mod, 0, 0)),
                  pl.BlockSpec((D_ATT, D_MODEL), lambda i: (0, 0)),
                  pl.BlockSpec((D_SSM, D_MODEL), lambda i: (0, 0)),
                  pl.BlockSpec((1, D_MODEL), lambda i: (0, 0)),
                  pl.BlockSpec((D_MODEL, 2 * LANES), lambda i: (0, 0)),
                  pl.BlockSpec((1, LANES), lambda i: (0, 0))],
        out_specs=[pl.BlockSpec((tm, D_MODEL), lambda i: (i, 0)),
                   pl.BlockSpec((tm, H2X_W), lambda i: (i, 0))],
        out_shape=[jax.ShapeDtypeStruct((n, D_MODEL), F32),
                   jax.ShapeDtypeStruct((n, H2X_W), F32)],
        compiler_params=_cparams(("parallel",)),
        name="outproj_router",
    )(att, ssm, x, mods3, wo_att, wo_ssm, g, w_router, b_router)


def _route_kernel(slab_ref, meta_ref):
    t_n = MOE_TILE
    blk = LANES
    slab = slab_ref[...]
    lane = lax.broadcasted_iota(jnp.int32, (t_n, LANES), 1)
    gcol = jnp.sum(jnp.where(lane == EXPERTS_PER_GROUP, slab, 0.0), axis=-1, keepdims=True)
    member = (lane.astype(F32) == gcol) & (lane < N_GROUPS)
    a = jnp.where(member, 1.0, 0.0).astype(BF16)
    r_i = lax.broadcasted_iota(jnp.int32, (blk, blk), 0)
    c_i = lax.broadcasted_iota(jnp.int32, (blk, blk), 1)
    lower = jnp.where(c_i < r_i, 1.0, 0.0).astype(BF16)
    upper = jnp.where(r_i < c_i, 1.0, 0.0).astype(BF16)
    offs = jnp.zeros((1, LANES), F32)
    ranks = []
    for b in range(t_n // blk):
        ab = a[b * blk:(b + 1) * blk]
        rb = jnp.dot(lower, ab, preferred_element_type=F32)
        ranks.append(rb + offs)
        offs = offs + rb[blk - 1:blk] + ab[blk - 1:blk].astype(F32)
    rank = jnp.concatenate(ranks, axis=0)
    n_chunk = jnp.floor((offs + (MOE_CHUNK - 1)) * (1.0 / MOE_CHUNK))
    start = jnp.dot(jnp.broadcast_to(n_chunk, (8, LANES)).astype(BF16), upper,
                    preferred_element_type=F32)[0:1]
    end = start + n_chunk
    dest = jnp.sum(jnp.where(member, start * MOE_CHUNK + rank, 0.0), axis=-1, keepdims=True)
    tok = lax.broadcasted_iota(jnp.int32, (t_n, LANES), 0)
    digits = jnp.where(lane == 0, (tok // blk).astype(F32),
                       jnp.where(lane == 1, (tok % blk).astype(F32), jnp.where(lane == 2, 1.0, 0.0))).astype(BF16)
    sw = 512
    pieces = []
    for sc in range(MOE_ROWS // sw):
        s_id = (lax.broadcasted_iota(jnp.int32, (t_n, sw), 1) + sc * sw).astype(F32)
        hit = jnp.where(dest == s_id, 1.0, 0.0).astype(BF16)
        r = lax.dot_general(digits, hit, (((0,), (0,)), ((), ())), preferred_element_type=F32)
        tok_of = r[0:1] * blk + r[1:2]
        pieces.append(jnp.where(r[2:3] > 0.5, tok_of, float(t_n)))
    pieces.append(jnp.full((1, META_ROWS - MOE_ROWS), float(t_n), F32))
    perm = jnp.concatenate(pieces, axis=1)
    slot = lax.broadcasted_iota(jnp.int32, (1, META_ROWS), 1).astype(F32)
    lane1 = lax.broadcasted_iota(jnp.int32, (1, LANES), 1)
    cg = jnp.zeros((1, META_ROWS), F32)
    for g in range(N_GROUPS):
        end_g = jnp.sum(jnp.where(lane1 == g, end, 0.0), axis=-1, keepdims=True)
        cg = cg + jnp.where(slot >= end_g, 1.0, 0.0)
    n_act = jnp.broadcast_to(end_g, (1, META_ROWS))
    meta_ref[...] = jnp.concatenate([perm, cg, n_act, jnp.zeros((5, META_ROWS), F32)], axis=0).astype(jnp.int32)


def _route(h2x):
    n = h2x.shape[0]
    n_tiles = n // MOE_TILE
    return pl.pallas_call(
        _route_kernel,
        grid=(n_tiles,),
        in_specs=[pl.BlockSpec((MOE_TILE, LANES), lambda i: (i, D_MODEL // LANES))],
        out_specs=pl.BlockSpec((None, 8, META_ROWS), lambda i: (i, 0, 0)),
        out_shape=jax.ShapeDtypeStruct((n_tiles, 8, META_ROWS), jnp.int32),
        compiler_params=_cparams(("parallel",)),
        name="moe_route",
    )(h2x)


def _moe_kernel(perm_ref, cg_ref, nact_ref, run_ref, h_ref, wg_hbm, wu_hbm, wd_hbm, y_ref,
                hs0, hs1, ys0, ys1, wg_buf, wu_buf, wd_buf, w_sem):
    i = pl.program_id(0)
    s = pl.program_id(1)
    n_act = nact_ref[i]
    tile_base = i * META_ROWS
    hs = (hs0, hs1)
    ys = (ys0, ys1)
    n_steps = pl.num_programs(0) * MOE_STEPS
    f = i * MOE_STEPS + s
    slot = run_ref[n_steps + f]

    def weight_copies(g, sl):
        e0 = g * EXPERTS_PER_GROUP
        r0 = g * (EXPERTS_PER_GROUP * EXPERT_FF)
        return (pltpu.make_async_copy(wg_hbm.at[pl.ds(e0, EXPERTS_PER_GROUP)], wg_buf.at[sl], w_sem.at[sl, 0]),
                pltpu.make_async_copy(wu_hbm.at[pl.ds(e0, EXPERTS_PER_GROUP)], wu_buf.at[sl], w_sem.at[sl, 1]),
                pltpu.make_async_copy(wd_hbm.at[pl.ds(r0, EXPERTS_PER_GROUP * EXPERT_FF)], wd_buf.at[sl],
                                      w_sem.at[sl, 2]))

    @pl.when(run_ref[f] == 1)
    def _():
        g = jnp.minimum(cg_ref[f], N_GROUPS - 1)

        @pl.when(run_ref[3 * n_steps + f] == 1)
        def _():
            for cp in weight_copies(g, slot):
                cp.start()

        for cp in weight_copies(g, slot):
            cp.wait()
        nxt = run_ref[2 * n_steps + f]

        @pl.when(nxt >= 0)
        def _():
            for cp in weight_copies(nxt, 1 - slot):
                cp.start()

    def gather(chunk, dst):
        base = tile_base + chunk * MOE_CHUNK
        for r in range(MOE_CHUNK):
            src = jnp.minimum(perm_ref[base + r], MOE_TILE - 1)
            dst[r:r + 1, :] = h_ref[pl.ds(src, 1), :]

    def scatter(chunk, src):
        base = tile_base + chunk * MOE_CHUNK
        for r in range(MOE_CHUNK):
            y_ref[pl.ds(perm_ref[base + r], 1), :] = src[r:r + 1, :]

    def ffn(src, dst):
        hb = src[:, 0:D_MODEL].astype(BF16)
        cw = src[:, D_MODEL:H2X_W]
        hid = []
        for e in range(EXPERTS_PER_GROUP):
            a = jnp.dot(hb, wg_buf[slot, e], preferred_element_type=F32)
            u = jnp.dot(hb, wu_buf[slot, e], preferred_element_type=F32)
            hid.append((_silu(a) * u * cw[:, e:e + 1]).astype(BF16))
        dst[...] = jnp.dot(jnp.concatenate(hid, axis=1), wd_buf[slot], preferred_element_type=F32)

    @pl.when(s == 0)
    def _():
        y_ref[MOE_TILE:MOE_TILE + 8, :] = jnp.zeros((8, D_MODEL), F32)
        ys1[...] = jnp.zeros_like(ys1)
        gather(0, hs0)

    for par in (0, 1):
        @pl.when((s < n_act) & (s % 2 == par))
        def _():
            gather(s + 1, hs[1 - par])
            ffn(hs[par], ys[par])
            scatter(jnp.maximum(s - 1, 0), ys[1 - par])

        @pl.when((s == n_act) & (s % 2 == par))
        def _():
            scatter(s - 1, ys[1 - par])


def _moe(h2x, perm, cgrp, nact, wg, wu, wd):
    n = h2x.shape[0]
    n_tiles = n // MOE_TILE
    n_steps = n_tiles * MOE_STEPS

    step = jnp.arange(n_steps)
    active = (step % MOE_STEPS) < jnp.repeat(nact, MOE_STEPS)
    prev = jnp.concatenate([jnp.full((1,), -1, jnp.int32), cgrp[:-1]])
    first = active & ((step % MOE_STEPS == 0) | (cgrp != prev))
    run_id = jnp.cumsum(first.astype(jnp.int32)) - 1
    later_first = lax.cummin(jnp.where(first, step, n_steps), reverse=True)
    nxt_step = jnp.concatenate([later_first[1:], jnp.full((1,), n_steps, jnp.int32)])
    nxt_group = jnp.where(nxt_step < n_steps, cgrp[jnp.minimum(nxt_step, n_steps - 1)], -1)
    runs = jnp.concatenate([first.astype(jnp.int32), run_id % 2, nxt_group.astype(jnp.int32),
                            (first & (run_id == 0)).astype(jnp.int32)])

    def h_idx(i, s, *_):
        done = (s >= jnp.maximum(_[2][i] - 1, 1)).astype(jnp.int32)
        return (jnp.minimum(i + done, n_tiles - 1), 0)

    hbm = pl.BlockSpec(memory_space=pl.ANY)
    return pl.pallas_call(
        _moe_kernel,
        grid_spec=pltpu.PrefetchScalarGridSpec(
            num_scalar_prefetch=4,
            grid=(n_tiles, MOE_STEPS),
            in_specs=[pl.BlockSpec((MOE_TILE, H2X_W), h_idx), hbm, hbm, hbm],
            out_specs=pl.BlockSpec((None, MOE_TILE + 8, D_MODEL), lambda i, s, *_: (i, 0, 0)),
            scratch_shapes=[pltpu.VMEM((MOE_CHUNK, H2X_W), F32), pltpu.VMEM((MOE_CHUNK, H2X_W), F32),
                            pltpu.VMEM((MOE_CHUNK, D_MODEL), F32), pltpu.VMEM((MOE_CHUNK, D_MODEL), F32),
                            pltpu.VMEM((2, EXPERTS_PER_GROUP, D_MODEL, EXPERT_FF), BF16),
                            pltpu.VMEM((2, EXPERTS_PER_GROUP, D_MODEL, EXPERT_FF), BF16),
                            pltpu.VMEM((2, EXPERTS_PER_GROUP * EXPERT_FF, D_MODEL), BF16),
                            pltpu.SemaphoreType.DMA((2, 3))]),
        out_shape=jax.ShapeDtypeStruct((n_tiles, MOE_TILE + 8, D_MODEL), F32),
        compiler_params=_cparams(("arbitrary", "arbitrary")),
        name="moe_experts",
    )(perm, cgrp, nact, runs, h2x, wg, wu, wd)


def _final_kernel(y_ref, x1_ref, mods_ref, fg_ref, o_ref):
    gate2 = mods_ref[:, 5 * D_MODEL:6 * D_MODEL]
    x2 = x1_ref[...] + gate2 * y_ref[...]
    o_ref[...] = x2 * lax.rsqrt(jnp.mean(x2 * x2, axis=-1, keepdims=True) + EPS) * fg_ref[...]


def _final(y, x1, mods3, mod_row0, mod_tokens, fg):
    n = x1.shape[0]
    tm = 512
    per_mod = mod_tokens // tm
    per_tile = MOE_TILE // tm
    return pl.pallas_call(
        _final_kernel,
        grid=(n // tm,),
        in_specs=[pl.BlockSpec((None, tm, D_MODEL), lambda j: (j // per_tile, j % per_tile, 0)),
                  pl.BlockSpec((tm, D_MODEL), lambda j: (j, 0)),
                  pl.BlockSpec((None, 1, 6 * D_MODEL), lambda j: (mod_row0 + j // per_mod, 0, 0)),
                  pl.BlockSpec((1, D_MODEL), lambda j: (0, 0))],
        out_specs=pl.BlockSpec((tm, D_MODEL), lambda j: (j, 0)),
        out_shape=jax.ShapeDtypeStruct((n, D_MODEL), F32),
        compiler_params=_cparams(("parallel",)),
        name="final_norm",
    )(y, x1, mods3, fg)


def _rope_tables(t):
    n_freq = QK_DIM // 4
    n_rows = t // GRID_W
    freqs = ROPE_BASE ** (-jnp.arange(n_freq, dtype=F32) / n_freq)
    ang_r = jnp.arange(n_rows, dtype=F32)[:, None] * freqs
    ang_c = jnp.arange(GRID_W, dtype=F32)[:, None] * freqs
    cr, sr, cc, sc = lax.optimization_barrier((jnp.cos(ang_r), jnp.sin(ang_r), jnp.cos(ang_c), jnp.sin(ang_c)))
    j = np.arange(LANES) % QK_DIM
    f_idx = j % n_freq
    by_row = (j < QK_DIM // 2)[None, None, :]
    first = ((j % (QK_DIM // 2)) < n_freq)[None, None, :]

    def table(r_small, c_small):
        return jnp.where(by_row, r_small[:, f_idx][:, None, :], c_small[:, f_idx][None, :, :])

    cos = table(cr, cc)
    sin = table(sr, sc)
    return (cos.reshape(t, LANES), jnp.where(first, -sin, 0.0).reshape(t, LANES),
            jnp.where(first, 0.0, sin).reshape(t, LANES))


def _layer(x, mods3, mod_row0, mod_tokens, rope_tabs, ctx_k, ctx_v, h0f, h0b, lw, layer):
    b, t, _ = x.shape
    n = b * t
    xf = x.reshape(n, D_MODEL)
    res = _inproj(xf, mods3, mod_row0, mod_tokens, t, lw["norm_mix_g"], lw["w_main"], lw["w_dt"], lw["alog"],
                  lw["dtb"], rope_tabs)
    q, kb, vt, z, xbc, cols, hrow = res[:7]
    if ctx_k is None:
        cache = None
        k3 = res[7].reshape(b, 1, t, ATT_HEADS, 2, LANES)[..., :QK_DIM]
        v3 = res[8].reshape(b, 1, t, ATT_HEADS, V_DIM)
    else:
        cache = (ctx_k.astype(BF16), jnp.swapaxes(ctx_v, 1, 2).astype(BF16))
        k3 = v3 = None
    lam0 = 0.8 - 0.6 * math.exp(-0.3 * layer)
    tq = 1024 if t % 1024 == 0 else 256
    tk = 512 if t % 512 == 0 else 256
    att = _attention(q.reshape(b, t, D_QK), kb.reshape(b, t, D_QK), vt, cache, lw["lamp"], lw["attn_subln_g"],
                     lam0, tq, tk, *((ATT_HEADS, 2) if t <= 512 else (1, 1)))
    ssm, hf, hb = _ssd(xbc, cols.reshape(b, t, LANES), hrow.reshape(b, t // CHUNK * HROWS, LANES),
                       z.reshape(b, t, D_SSM), h0f, h0b, lw["conv_w"], lw["conv_b"], lw["d_x"], lw["ssm_norm_g"])
    x1, h2x = _outproj(att.reshape(n, D_ATT), ssm.reshape(n, D_SSM), xf, mods3, mod_row0, mod_tokens,
                       lw["wo_att"], lw["wo_ssm"], lw["norm_ffn_g"], lw["w_router"], lw["b_router"])
    meta = _route(h2x)
    perm = meta[:, 0, :].reshape(-1)
    cgrp = meta[:, 1, :MOE_STEPS].reshape(-1)
    nact = meta[:, 2, 0]
    y = _moe(h2x, perm, cgrp, nact, lw["wg"], lw["wu"], lw["wd"])
    out = _final(y, x1, mods3, mod_row0, mod_tokens, lw["final_g"])
    return out.reshape(b, t, D_MODEL), k3, v3, hf, hb


def _pad_lanes(v, width=LANES):
    return jnp.pad(v, [(0, 0)] * (v.ndim - 1) + [(0, width - v.shape[-1])])


def kernel(x_prompt, x_sample, cache_k, cache_v, state_ssm_fwd, state_ssm_bwd, c, c_ctx, w_ada, b_ada, norm_mix_g, w_in, w_out, lambda_q1, lambda_k1, lambda_q2, lambda_k2, attn_subln_g, conv_w, conv_b, a_log_fwd, a_log_bwd, dt_bias_fwd, dt_bias_bwd, ssm_d, ssm_norm_g, norm_ffn_g, w_group_router, b_group_router, w_expert_router, b_expert_router, w_exp_gate, w_exp_up, w_exp_down, final_norm_g):
    depth = w_in.shape[0]
    assert depth == 1, "single trunk layer"
    bp, tp, _ = x_prompt.shape
    bs, ts, _ = x_sample.shape
    l = 0
    cond = jnp.concatenate([c_ctx[None], c], axis=0)
    condT = _pad_lanes(cond.T, 8)
    mods = _ada(condT, w_ada[l], b_ada[l][None])
    mods3 = mods.reshape(8, 1, 6 * D_MODEL)

    w_router = _pad_lanes(jnp.concatenate([w_group_router[l], w_expert_router[l]], axis=1))
    wr_hi = w_router.astype(BF16)
    wr_lo = (w_router - wr_hi.astype(F32)).astype(BF16)
    lw = dict(
        norm_mix_g=norm_mix_g[l][None],
        w_main=w_in[l].astype(BF16),
        w_dt=_pad_lanes(w_in[l][:, MAIN_COLS:]).astype(BF16),
        lamp=jnp.stack([lambda_q1[l], lambda_k1[l], lambda_q2[l], lambda_k2[l]]),
        attn_subln_g=attn_subln_g[l][None],
        conv_w=conv_w[l].reshape(CONV_W, N_COLBLK, LANES), conv_b=conv_b[l].reshape(N_COLBLK, LANES),
        alog=jnp.broadcast_to(jnp.concatenate([a_log_fwd[l], a_log_bwd[l]])[:, None], (2 * SSM_HEADS, CHUNK)),
        dtb=jnp.broadcast_to(jnp.concatenate([dt_bias_fwd[l], dt_bias_bwd[l]])[:, None], (2 * SSM_HEADS, CHUNK)),
        d_x=jnp.repeat(ssm_d[l], SSM_HEADDIM)[None], ssm_norm_g=ssm_norm_g[l][None],
        wo_att=w_out[l][:D_ATT].astype(BF16), wo_ssm=w_out[l][D_ATT:].astype(BF16),
        norm_ffn_g=norm_ffn_g[l][None],
        w_router=jnp.concatenate([wr_hi, wr_lo], axis=1),
        b_router=_pad_lanes(jnp.concatenate([b_group_router[l], b_expert_router[l]])[None]),
        wg=w_exp_gate[l].astype(BF16), wu=w_exp_up[l].astype(BF16), wd=w_exp_down[l].astype(BF16).reshape(N_EXPERTS * EXPERT_FF, D_MODEL),
        final_g=final_norm_g[None],
    )
    n_state = SSM_HEADS * SSM_HEADDIM
    zeros_state = jnp.zeros((bp, n_state, D_STATE), F32)
    yp, ck, cv, hf, hb = _layer(x_prompt, mods3, 0, bp * tp, None, None, None, zeros_state, zeros_state, lw, l)
    ys, _, _, _, _ = _layer(x_sample, mods3, 1, ts, _rope_tables(ts),
                            cache_k[:, l].reshape(bs, -1, D_QK), cache_v[:, l].reshape(bs, -1, D_ATT),
                            state_ssm_fwd[:, l].reshape(bs, n_state, D_STATE),
                            state_ssm_bwd[:, l].reshape(bs, n_state, D_STATE), lw, l)
    new_k, new_v = ck, cv
    new_hf = hf.reshape(bp, 1, SSM_HEADS, SSM_HEADDIM, D_STATE)
    new_hb = hb.reshape(bp, 1, SSM_HEADS, SSM_HEADDIM, D_STATE)
    return yp, ys, new_k, new_v, new_hf, new_hb
```

```python
import functools
import math

import numpy as np
import jax
import jax.numpy as jnp
from jax import lax
from jax.experimental import pallas as pl
from jax.experimental.pallas import tpu as pltpu

D_MODEL = 1024
GRID_W = 64
ATT_HEADS = 4
QK_DIM = 64
V_DIM = 128
D_QK = 512
D_ATT = 512
ROPE_BASE = 10000.0
D_SSM = 512
SSM_HEADDIM = 64
SSM_HEADS = 8
SSM_GROUPS = 2
D_STATE = 128
CONV_W = 5
CHUNK = 128
XBC_DIM = 1024
N_GROUPS = 4
EXPERTS_PER_GROUP = 4
N_EXPERTS = 16
EXPERT_FF = 256
EPS = 1e-6
MAIN_COLS = 2 * D_QK + D_ATT + D_SSM + XBC_DIM
H2X_W = D_MODEL + 128
MOE_TILE = 2048
MOE_CHUNK = 256
MOE_SLOTS = MOE_TILE // MOE_CHUNK + N_GROUPS
MOE_ROWS = MOE_SLOTS * MOE_CHUNK
MOE_STEPS = MOE_SLOTS + 1
META_ROWS = MOE_STEPS * MOE_CHUNK
COL_CUM, COL_DT, COL_ECUM, COL_TOEND = 0, 16, 32, 48
HROWS = 32
LANES = 128
HALO = 8
VMEM_LIMIT = 56 * 1024 * 1024

LOG2E = math.log2(math.e)
SUM_ROWS = 16
F32 = jnp.float32
BF16 = jnp.bfloat16


def _cparams(sem):
    return pltpu.CompilerParams(dimension_semantics=sem, vmem_limit_bytes=VMEM_LIMIT)


def _sigmoid(x):
    return 1.0 / (1.0 + jnp.exp(-x))


def _silu(x):
    return x * _sigmoid(x)


def _ada_kernel(condT_ref, w_ref, b_ref, o_ref):
    s = _silu(condT_ref[...])
    w = w_ref[...]
    b = b_ref[...]
    o_ref[...] = jnp.zeros_like(o_ref)
    for r in range(3):
        o_ref[r:r + 1, :] = jnp.sum(w * s[:, r:r + 1], axis=0, keepdims=True) + b


def _ada(condT, w_ada, b_ada):
    bn = 1024
    n = w_ada.shape[1]
    return pl.pallas_call(
        _ada_kernel,
        grid=(n // bn,),
        in_specs=[pl.BlockSpec((D_MODEL, 8), lambda j: (0, 0)),
                  pl.BlockSpec((D_MODEL, bn), lambda j: (0, j)),
                  pl.BlockSpec((1, bn), lambda j: (0, j))],
        out_specs=pl.BlockSpec((8, bn), lambda j: (0, j)),
        out_shape=jax.ShapeDtypeStruct((8, n), F32),
        compiler_params=_cparams(("arbitrary",)),
        name="ada",
    )(condT, w_ada, b_ada)


def _split3(x):
    hi = x.astype(BF16)
    r1 = x - hi.astype(F32)
    mid = r1.astype(BF16)
    lo = (r1 - mid.astype(F32)).astype(BF16)
    return jnp.concatenate([hi, mid, lo], axis=-1)


def _scan_matrices():
    t = np.arange(CHUNK)
    pre = (t[:, None] <= t[None, :]).astype(np.float32)
    suf = (t[:, None] >= t[None, :]).astype(np.float32)
    return jnp.asarray(np.stack([np.concatenate([pre] * 3, axis=0), np.concatenate([suf] * 3, axis=0)]), BF16)


def _head_scalars(dt_raw, alog_ref, dtb_ref, scan_ref):
    nh2 = 2 * SSM_HEADS
    xv = dt_raw.T[0:nh2, :] + dtb_ref[...]
    dt = jnp.maximum(xv, 0.0) + jnp.log(1.0 + jnp.exp(-jnp.abs(xv)))
    la3 = _split3(dt * (-jnp.exp(alog_ref[...])))
    fwd = lax.broadcasted_iota(jnp.int32, (nh2, CHUNK), 0) < SSM_HEADS
    cum = jnp.where(fwd, jnp.dot(la3, scan_ref[0], preferred_element_type=F32),
                    jnp.dot(la3, scan_ref[1], preferred_element_type=F32))
    cum_end = jnp.where(fwd, cum[:, CHUNK - 1:CHUNK], cum[:, 0:1])
    packed = jnp.concatenate([cum, dt, jnp.exp(cum), jnp.exp(cum_end - cum),
                              jnp.zeros((LANES - 4 * nh2, CHUNK), F32)], axis=0)
    return packed.T, jnp.concatenate([cum, jnp.exp(cum_end)], axis=0)


def _inproj_kernel(rope, x_ref, mods_ref, g_ref, w_ref, wdt_ref, alog_ref, dtb_ref, scan_ref, *rest):
    if rope:
        cos_ref, sa_ref, sb_ref, q_ref, kb_ref, vt_ref, z_ref, xbc_ref, cols_ref, hrow_ref = rest
    else:
        q_ref, kb_ref, vt_ref, z_ref, xbc_ref, cols_ref, hrow_ref, kf_ref, vf_ref = rest
    x = x_ref[...]
    shift = mods_ref[:, 0:D_MODEL]
    scale = mods_ref[:, D_MODEL:2 * D_MODEL]
    y = x * lax.rsqrt(jnp.mean(x * x, axis=-1, keepdims=True) + EPS) * g_ref[...]
    h = (y * (1.0 + scale) + shift).astype(BF16)
    dt_raw = jnp.dot(h, wdt_ref[...], preferred_element_type=F32)
    for ci in range(x.shape[0] // CHUNK):
        cols, hrow = _head_scalars(dt_raw[ci * CHUNK:(ci + 1) * CHUNK, :], alog_ref, dtb_ref, scan_ref)
        cols_ref[ci * CHUNK:(ci + 1) * CHUNK, :] = cols
        hrow_ref[ci * HROWS:(ci + 1) * HROWS, :] = hrow
    r = jnp.dot(h, w_ref[...], preferred_element_type=F32)
    q = r[:, 0:D_QK]
    k = r[:, D_QK:2 * D_QK]
    if rope:
        cos = cos_ref[...]
        sa = sa_ref[...]
        sb = sb_ref[...]

        def rot(t):
            parts = []
            for hh in range(ATT_HEADS):
                th = t[:, hh * LANES:(hh + 1) * LANES]
                parts.append(th * cos + pltpu.roll(th, LANES - 16, 1) * sa + pltpu.roll(th, 16, 1) * sb)
            return jnp.concatenate(parts, axis=1)

        q = rot(q)
        k = rot(k)
    v = r[:, 2 * D_QK:2 * D_QK + D_ATT]
    q_ref[...] = q
    kb_ref[...] = k.astype(BF16)
    n_seq, _, t_seq = vt_ref.shape
    v_t = v.T.astype(BF16)
    for sq in range(n_seq):
        vt_ref[sq] = v_t[:, sq * t_seq:(sq + 1) * t_seq]
    if not rope:
        tm = k.shape[0]
        for hh in range(ATT_HEADS):
            k_h = k[:, hh * LANES:(hh + 1) * LANES]
            vf_ref[pl.ds(hh, tm, stride=ATT_HEADS), :] = v[:, hh * LANES:(hh + 1) * LANES]
            kf_ref[pl.ds(2 * hh, tm, stride=2 * ATT_HEADS), :] = k_h
            kf_ref[pl.ds(2 * hh + 1, tm, stride=2 * ATT_HEADS), :] = pltpu.roll(k_h, QK_DIM, 1)
    z_ref[...] = r[:, 2 * D_QK + D_ATT:2 * D_QK + D_ATT + D_SSM]
    x0 = 2 * D_QK + D_ATT + D_SSM
    for cb in range(XBC_DIM // LANES):
        for sq in range(n_seq):
            xbc_ref[sq, cb] = r[sq * t_seq:(sq + 1) * t_seq, x0 + cb * LANES:x0 + (cb + 1) * LANES]


def _inproj(x, mods3, mod_row0, mod_tokens, seq_len, g, w_main, w_dt, alog, dtb, rope_tabs):
    n = x.shape[0]
    tm = 512
    assert seq_len % tm == 0 or tm % seq_len == 0
    per_seq = max(seq_len // tm, 1)
    n_seq = max(tm // seq_len, 1)
    t_seq = tm // n_seq
    per_mod = mod_tokens // tm
    rope = rope_tabs is not None
    in_specs = [pl.BlockSpec((tm, D_MODEL), lambda i: (i, 0)),
                pl.BlockSpec((None, 1, 6 * D_MODEL), lambda i: (mod_row0 + i // per_mod, 0, 0)),
                pl.BlockSpec((1, D_MODEL), lambda i: (0, 0)),
                pl.BlockSpec((D_MODEL, MAIN_COLS), lambda i: (0, 0)),
                pl.BlockSpec((D_MODEL, LANES), lambda i: (0, 0)),
                pl.BlockSpec((2 * SSM_HEADS, CHUNK), lambda i: (0, 0)),
                pl.BlockSpec((2 * SSM_HEADS, CHUNK), lambda i: (0, 0)),
                pl.BlockSpec((2, 3 * CHUNK, CHUNK), lambda i: (0, 0, 0))]
    args = [x, mods3, g, w_main, w_dt, alog, dtb, _scan_matrices()]
    if rope:
        tab_spec = pl.BlockSpec((tm, LANES), lambda i: (i % per_seq, 0))
        in_specs += [tab_spec] * 3
        args += list(rope_tabs)
    def rows(wd, dtype=F32):
        return pl.BlockSpec((tm, wd), lambda i: (i, 0)), jax.ShapeDtypeStruct((n, wd), dtype)

    hr = tm // CHUNK * HROWS
    outs = [rows(D_QK), rows(D_QK, BF16),
            (pl.BlockSpec((n_seq, D_ATT, t_seq), lambda i: (i // per_seq, 0, i % per_seq)),
             jax.ShapeDtypeStruct((n // seq_len, D_ATT, seq_len), BF16)),
            rows(D_SSM),
            (pl.BlockSpec((n_seq, XBC_DIM // LANES, t_seq, LANES), lambda i: (i // per_seq, 0, i % per_seq, 0)),
             jax.ShapeDtypeStruct((n // seq_len, XBC_DIM // LANES, seq_len, LANES), F32)),
            rows(LANES),
            (pl.BlockSpec((hr, LANES), lambda i: (i, 0)), jax.ShapeDtypeStruct((n // CHUNK * HROWS, LANES), F32))]
    if not rope:
        outs += [(pl.BlockSpec((tm * 2 * ATT_HEADS, LANES), lambda i: (i, 0)),
                  jax.ShapeDtypeStruct((n * 2 * ATT_HEADS, LANES), F32)),
                 (pl.BlockSpec((tm * ATT_HEADS, LANES), lambda i: (i, 0)),
                  jax.ShapeDtypeStruct((n * ATT_HEADS, LANES), F32))]
    return pl.pallas_call(
        functools.partial(_inproj_kernel, rope),
        grid=(n // tm,),
        in_specs=in_specs,
        out_specs=[o[0] for o in outs],
        out_shape=[o[1] for o in outs],
        compiler_params=_cparams(("parallel",)),
        name="inproj_rope" if rope else "inproj",
    )(*args)


def _attn_kernel(tk, lam0, has_cache, q_ref, k_ref, vt_ref, *rest):
    if has_cache:
        ck_ref, cvt_ref, lamp_ref, g_ref, o_ref = rest
    else:
        lamp_ref, g_ref, o_ref = rest
    tq = q_ref.shape[1]
    lp = lamp_ref[...]
    lam = (jnp.exp(jnp.sum(lp[0:1] * lp[1:2], axis=-1, keepdims=True))
           - jnp.exp(jnp.sum(lp[2:3] * lp[3:4], axis=-1, keepdims=True)) + lam0)
    for bb, hh in [(b_, h_) for b_ in range(q_ref.shape[0]) for h_ in range(q_ref.shape[2] // LANES)]:
        hs = slice(hh * LANES, (hh + 1) * LANES)
        chunks = [(k_ref, vt_ref, c * tk, tk) for c in range(k_ref.shape[1] // tk)]
        if has_cache:
            ck = min(tk, ck_ref.shape[1])
            chunks += [(ck_ref, cvt_ref, c * ck, ck) for c in range(ck_ref.shape[1] // ck)]
        q = q_ref[bb, :, hs] * (QK_DIM ** -0.5 * LOG2E)
        lane = lax.broadcasted_iota(jnp.int32, q.shape, 1)
        qq_t = jnp.concatenate([jnp.where(lane < QK_DIM, q, 0.0), jnp.where(lane >= QK_DIM, q, 0.0)],
                               axis=0).T.astype(BF16)

        def scores(chunk):
            kr, _, start, size = chunk
            return jnp.dot(kr[bb, start:start + size, hs], qq_t, preferred_element_type=F32)

        def update(s, chunk, m, acc):
            _, vr, start, size = chunk
            m_new = jnp.maximum(m, jnp.max(s, axis=0, keepdims=True))
            alpha = jnp.exp2(m - m_new)
            p = jnp.exp2(s - m_new).astype(BF16)
            v_ext = jnp.concatenate([vr[bb, hs, start:start + size], jnp.ones((SUM_ROWS, size), BF16)], axis=0)
            acc = alpha * acc + jnp.dot(v_ext, p, preferred_element_type=F32)
            return m_new, acc

        m = jnp.full((1, 2 * tq), -jnp.inf, F32)
        acc = jnp.zeros((V_DIM + SUM_ROWS, 2 * tq), F32)
        s = scores(chunks[0])
        for c, chunk in enumerate(chunks):
            s_next = scores(chunks[c + 1]) if c + 1 < len(chunks) else None
            m, acc = update(s, chunk, m, acc)
            s = s_next
        o = acc[0:V_DIM] / acc[V_DIM:V_DIM + 1]
        o = (o[:, 0:tq] - lam * o[:, tq:2 * tq]).T
        o = o * lax.rsqrt(jnp.mean(o * o, axis=-1, keepdims=True) + EPS)
        o_ref[bb, :, hs] = o * g_ref[...] * (1.0 - lam0)


def _attention(q, k, vt, cache, lamp, g, lam0, tq, tk, nh, nb):
    b, t, _ = q.shape
    wd = nh * LANES

    def kv_specs(length):
        return [pl.BlockSpec((nb, length, wd), lambda bi, h, i: (bi, 0, h)),
                pl.BlockSpec((nb, wd, length), lambda bi, h, i: (bi, h, 0))]

    in_specs = [pl.BlockSpec((nb, tq, wd), lambda bi, h, i: (bi, i, h))] + kv_specs(t)
    args = [q, k, vt]
    if cache is not None:
        assert cache[0].shape[1] % min(tk, cache[0].shape[1]) == 0
        in_specs += kv_specs(cache[0].shape[1])
        args += list(cache)
    in_specs += [pl.BlockSpec((4, QK_DIM), lambda bi, h, i: (0, 0)),
                 pl.BlockSpec((1, V_DIM), lambda bi, h, i: (0, 0))]
    return pl.pallas_call(
        functools.partial(_attn_kernel, tk, lam0, cache is not None),
        grid=(b // nb, ATT_HEADS // nh, t // tq),
        in_specs=in_specs,
        out_specs=pl.BlockSpec((nb, tq, wd), lambda bi, h, i: (bi, i, h)),
        out_shape=jax.ShapeDtypeStruct((b, t, D_ATT), F32),
        compiler_params=_cparams(("parallel", "parallel", "arbitrary")),
        name="diff_attn",
    )(*args, lamp, g)


N_COLBLK = XBC_DIM // LANES
def _expand_matrices():
    out = []
    for d in range(2):
        e = np.zeros((LANES, 3 * D_SSM), np.float32)
        for blk, lane0 in enumerate((COL_DT, COL_ECUM, COL_TOEND)):
            for h in range(SSM_HEADS):
                e[lane0 + d * SSM_HEADS + h,
                  blk * D_SSM + h * SSM_HEADDIM:blk * D_SSM + (h + 1) * SSM_HEADDIM] = 1.0
        out.append(np.concatenate([e, e, e], axis=0))
    return jnp.asarray(np.stack(out), BF16)


def _ssd_chunk(reverse, xs, bm, cm, state, hrow, cols, e_ref):
    d0 = SSM_HEADS if reverse else 0
    cum = hrow[d0:d0 + SSM_HEADS, :]
    dec = hrow[2 * SSM_HEADS + d0:2 * SSM_HEADS + d0 + SSM_HEADS, 0:1]
    row = lax.broadcasted_iota(jnp.int32, (CHUNK, LANES), 0)
    lane = lax.broadcasted_iota(jnp.int32, (CHUNK, LANES), 1)
    causal = (row <= lane) if reverse else (row >= lane)
    lane_g = lax.broadcasted_iota(jnp.int32, (CHUNK, 2 * LANES), 1)
    spread = jnp.dot(_split3(cols), e_ref[1 if reverse else 0], preferred_element_type=F32)
    xd = xs * spread[:, 0:D_SSM]
    xdw = (xd * spread[:, 2 * D_SSM:3 * D_SSM]).astype(BF16)
    xd = xd.astype(BF16)
    ecum_x = spread[:, D_SSM:2 * D_SSM]
    rep = SSM_HEADS // SSM_GROUPS
    y_parts = []
    new_state = []
    for g in range(SSM_GROUPS):
        bg = bm[:, g * D_STATE:(g + 1) * D_STATE]
        cg = cm[:, g * D_STATE:(g + 1) * D_STATE]
        cbt = lax.dot_general(cg, bg, (((1,), (1,)), ((), ())), preferred_element_type=F32)
        rows = slice(g * rep * SSM_HEADDIM, (g + 1) * rep * SSM_HEADDIM)
        st_g = state[rows, :]
        y_off = lax.dot_general(cg, st_g.astype(BF16), (((1,), (1,)), ((), ())),
                                preferred_element_type=F32)
        cst = lax.dot_general(xdw[:, rows], bg, (((0,), (0,)), ((), ())), preferred_element_type=F32)
        xd_g = xd[:, rows]
        scs = []
        blocks = []
        for hh in range(rep):
            h = g * rep + hh
            seg = cols[:, COL_CUM + d0 + h:COL_CUM + d0 + h + 1] - cum[h:h + 1, :]
            decay = jnp.exp(jnp.where(causal, seg, -jnp.inf))
            scs.append((cbt * decay).astype(BF16))
            blocks.append(jnp.where(lane_g // SSM_HEADDIM == hh, xd_g, jnp.zeros_like(xd_g)))
        y_diag = jnp.dot(jnp.concatenate(scs, axis=1), jnp.concatenate(blocks, axis=0),
                         preferred_element_type=F32)
        dec_rows = jnp.concatenate(
            [jnp.broadcast_to(dec[g * rep + hh:g * rep + hh + 1, :], (SSM_HEADDIM, D_STATE)) for hh in range(rep)],
            axis=0)
        new_state.append(st_g * dec_rows + cst)
        y_parts.append(y_diag + y_off * ecum_x[:, rows])
    return jnp.concatenate(y_parts, axis=1), jnp.concatenate(new_state, axis=0)


def _ssd_kernel(nc, has_h0, xc_ref, xp_ref, xn_ref, cols_ref, hrow_ref, z_ref, *rest):
    if has_h0:
        h0f_ref, h0b_ref = rest[:2]
        rest = rest[2:]
    cw_ref, cb_ref, d_ref, ng_ref, e_ref, y_ref, hf_ref, hb_ref, state_ref, slab_ref, xs_ref, bc_ref, yf_ref = rest
    ps = pl.program_id(1)
    c = pl.program_id(2)
    n_bc = SSM_GROUPS * D_STATE

    @pl.when((ps == 0) & (c == 0))
    def _():
        state_ref[...] = h0f_ref[...] if has_h0 else jnp.zeros_like(state_ref)

    @pl.when(ps == 0)
    def _():
        t0 = pl.multiple_of(c * CHUNK, CHUNK)
        for bb in range(state_ref.shape[0]):
            slab_ref[bb, :, 0:HALO, :] = jnp.where(c > 0, xp_ref[bb], 0.0)
            slab_ref[bb, :, HALO:HALO + CHUNK, :] = xc_ref[bb]
            slab_ref[bb, :, HALO + CHUNK:2 * HALO + CHUNK, :] = jnp.where(c < nc - 1, xn_ref[bb], 0.0)
            blocks = []
            for cb in range(N_COLBLK):
                conv = cb_ref[cb:cb + 1, :]
                for kk in range(CONV_W):
                    off = HALO - CONV_W // 2 + kk
                    conv = conv + slab_ref[bb, cb, off:off + CHUNK, :] * cw_ref[kk, cb:cb + 1, :]
                blocks.append(_silu(conv))
            xs = jnp.concatenate(blocks[0:D_SSM // LANES], axis=1)
            bc = jnp.concatenate(blocks[D_SSM // LANES:], axis=1).astype(BF16)
            y, state = _ssd_chunk(False, xs, bc[:, 0:n_bc], bc[:, n_bc:], state_ref[bb], hrow_ref[bb],
                                  cols_ref[bb], e_ref)
            xs_ref[bb, pl.ds(t0, CHUNK), :] = xs
            bc_ref[bb, pl.ds(t0, CHUNK), :] = bc
            yf_ref[bb, pl.ds(t0, CHUNK), :] = y
            state_ref[bb] = state

    @pl.when((ps == 0) & (c == nc - 1))
    def _():
        hf_ref[...] = state_ref[...]
        state_ref[...] = h0b_ref[...] if has_h0 else jnp.zeros_like(state_ref)

    @pl.when(ps == 1)
    def _():
        t0 = pl.multiple_of((nc - 1 - c) * CHUNK, CHUNK)
        for bb in range(state_ref.shape[0]):
            xs = xs_ref[bb, pl.ds(t0, CHUNK), :]
            bc = bc_ref[bb, pl.ds(t0, CHUNK), :]
            y, state = _ssd_chunk(True, xs, bc[:, 0:n_bc], bc[:, n_bc:], state_ref[bb], hrow_ref[bb],
                                  cols_ref[bb], e_ref)
            state_ref[bb] = state
            y = y + yf_ref[bb, pl.ds(t0, CHUNK), :] + xs * d_ref[...]
            y = y * _silu(z_ref[bb])
            y = y * lax.rsqrt(jnp.mean(y * y, axis=-1, keepdims=True) + EPS)
            y_ref[bb] = y * ng_ref[...]

    @pl.when((ps == 1) & (c == nc - 1))
    def _():
        hb_ref[...] = state_ref[...]


def _ssd(xbc, cols, hrow, z, h0f, h0b, conv_w, conv_b, d_x, norm_g):
    b, _, l, _ = xbc.shape
    has_h0 = h0f is not None
    nb = 2
    assert b % nb == 0
    nc = l // CHUNK
    hb = CHUNK // HALO
    n_state = SSM_HEADS * SSM_HEADDIM
    last = nc - 1

    def fwd_chunk(ps, ci):
        return jnp.where(ps == 0, ci, last)

    def any_chunk(ps, ci):
        return jnp.where(ps == 0, ci, last - ci)

    def bwd_chunk(ps, ci):
        return jnp.where(ps == 0, last, last - ci)

    const2 = lambda bi, ps, ci: (0, 0)
    state_spec = pl.BlockSpec((nb, n_state, D_STATE), lambda bi, ps, ci: (bi, 0, 0))
    in_specs = [pl.BlockSpec((nb, N_COLBLK, CHUNK, LANES), lambda bi, ps, ci: (bi, 0, fwd_chunk(ps, ci), 0)),
                pl.BlockSpec((nb, N_COLBLK, HALO, LANES),
                             lambda bi, ps, ci: (bi, 0, jnp.maximum(fwd_chunk(ps, ci) * hb - 1, 0), 0)),
                pl.BlockSpec((nb, N_COLBLK, HALO, LANES),
                             lambda bi, ps, ci: (bi, 0, jnp.minimum((fwd_chunk(ps, ci) + 1) * hb, l // HALO - 1), 0)),
                pl.BlockSpec((nb, CHUNK, LANES), lambda bi, ps, ci: (bi, any_chunk(ps, ci), 0)),
                pl.BlockSpec((nb, HROWS, LANES), lambda bi, ps, ci: (bi, any_chunk(ps, ci), 0)),
                pl.BlockSpec((nb, CHUNK, D_SSM), lambda bi, ps, ci: (bi, bwd_chunk(ps, ci), 0)),
                *([state_spec, state_spec] if has_h0 else []),
                pl.BlockSpec((CONV_W, N_COLBLK, LANES), lambda bi, ps, ci: (0, 0, 0)),
                pl.BlockSpec((N_COLBLK, LANES), const2),
                pl.BlockSpec((1, D_SSM), const2),
                pl.BlockSpec((1, D_SSM), const2),
                pl.BlockSpec((2, 3 * LANES, 3 * D_SSM), lambda bi, ps, ci: (0, 0, 0))]
    return pl.pallas_call(
        functools.partial(_ssd_kernel, nc, has_h0),
        grid=(b // nb, 2, nc),
        in_specs=in_specs,
        out_specs=[pl.BlockSpec((nb, CHUNK, D_SSM), lambda bi, ps, ci: (bi, bwd_chunk(ps, ci), 0)),
                   state_spec, state_spec],
        out_shape=[jax.ShapeDtypeStruct((b, l, D_SSM), F32),
                   jax.ShapeDtypeStruct((b, n_state, D_STATE), F32),
                   jax.ShapeDtypeStruct((b, n_state, D_STATE), F32)],
        scratch_shapes=[pltpu.VMEM((nb, n_state, D_STATE), F32),
                        pltpu.VMEM((nb, N_COLBLK, CHUNK + 2 * HALO, LANES), F32),
                        pltpu.VMEM((nb, l, D_SSM), F32),
                        pltpu.VMEM((nb, l, 2 * SSM_GROUPS * D_STATE), BF16),
                        pltpu.VMEM((nb, l, D_SSM), F32)],
        compiler_params=_cparams(("parallel", "arbitrary", "arbitrary")),
        name="ssd",
    )(xbc, xbc, xbc, cols, hrow, z, *([h0f, h0b] if has_h0 else []), conv_w, conv_b, d_x, norm_g,
      _expand_matrices())


def _outproj_kernel(att_ref, ssm_ref, x_ref, mods_ref, wa_ref, ws_ref, g_ref, wr_ref, br_ref,
                    x1_ref, h2x_ref):
    mix = (jnp.dot(att_ref[...].astype(BF16), wa_ref[...], preferred_element_type=F32)
           + jnp.dot(ssm_ref[...].astype(BF16), ws_ref[...], preferred_element_type=F32))
    gate1 = mods_ref[:, 2 * D_MODEL:3 * D_MODEL]
    shift2 = mods_ref[:, 3 * D_MODEL:4 * D_MODEL]
    scale2 = mods_ref[:, 4 * D_MODEL:5 * D_MODEL]
    x1 = x_ref[...] + gate1 * mix
    x1_ref[...] = x1
    y = x1 * lax.rsqrt(jnp.mean(x1 * x1, axis=-1, keepdims=True) + EPS) * g_ref[...]
    h2 = y * (1.0 + scale2) + shift2
    h_hi = h2.astype(BF16)
    h2x_ref[:, 0:D_MODEL] = h2
    h_lo = (h2 - h_hi.astype(F32)).astype(BF16)
    both = jnp.dot(h_hi, wr_ref[...], preferred_element_type=F32)
    logits = (both[:, 0:LANES] + both[:, LANES:2 * LANES]
              + jnp.dot(h_lo, wr_ref[:, 0:LANES], preferred_element_type=F32)) + br_ref[...]
    lane = lax.broadcasted_iota(jnp.int32, logits.shape, 1).astype(F32)
    neg = -jnp.inf
    big = float(1 << 20)
    is_g = lane < N_GROUPS
    gl = jnp.where(is_g, logits, neg)
    gmax = jnp.max(gl, axis=-1, keepdims=True)
    g_idx = jnp.min(jnp.where(gl == gmax, lane, big), axis=-1, keepdims=True)
    p_g = 1.0 / jnp.sum(jnp.where(is_g, jnp.exp(gl - gmax), 0.0), axis=-1, keepdims=True)
    e_lo = N_GROUPS + g_idx * EXPERTS_PER_GROUP
    in_grp = (lane >= e_lo) & (lane < e_lo + EXPERTS_PER_GROUP)
    el = jnp.where(in_grp, logits, neg)
    m1 = jnp.max(el, axis=-1, keepdims=True)
    i1 = jnp.min(jnp.where(el == m1, lane, big), axis=-1, keepdims=True)
    el2 = jnp.where(lane == i1, neg, el)
    m2 = jnp.max(el2, axis=-1, keepdims=True)
    i2 = jnp.min(jnp.where(el2 == m2, lane, big), axis=-1, keepdims=True)
    e2 = jnp.exp(m2 - m1)
    w1 = p_g / (1.0 + e2)
    w2 = p_g * e2 / (1.0 + e2)
    slab = (jnp.where(lane == i1 - e_lo, w1, 0.0) + jnp.where(lane == i2 - e_lo, w2, 0.0)
            + jnp.where(lane == EXPERTS_PER_GROUP, g_idx, 0.0))
    h2x_ref[:, D_MODEL:H2X_W] = slab


def _outproj(att, ssm, x, mods3, mod_row0, mod_tokens, wo_att, wo_ssm, g, w_router, b_router):
    n = x.shape[0]
    tm = 512
    per_mod = mod_tokens // tm
    return pl.pallas_call(
        _outproj_kernel,
        grid=(n // tm,),
        in_specs=[pl.BlockSpec((tm, D_ATT), lambda i: (i, 0)),
                  pl.BlockSpec((tm, D_SSM), lambda i: (i, 0)),
                  pl.BlockSpec((tm, D_MODEL), lambda i: (i, 0)),
                  pl.BlockSpec((None, 1, 6 * D_MODEL), lambda i: (mod_row0 + i // per_mod, 0, 0)),
                  pl.BlockSpec((D_ATT, D_MODEL), lambda i: (0, 0)),
                  pl.BlockSpec((D_SSM, D_MODEL), lambda i: (0, 0)),
                  pl.BlockSpec((1, D_MODEL), lambda i: (0, 0)),
                  pl.BlockSpec((D_MODEL, 2 * LANES), lambda i: (0, 0)),
                  pl.BlockSpec((1, LANES), lambda i: (0, 0))],
        out_specs=[pl.BlockSpec((tm, D_MODEL), lambda i: (i, 0)),
                   pl.BlockSpec((tm, H2X_W), lambda i: (i, 0))],
        out_shape=[jax.ShapeDtypeStruct((n, D_MODEL), F32),
                   jax.ShapeDtypeStruct((n, H2X_W), F32)],
        compiler_params=_cparams(("parallel",)),
        name="outproj_router",
    )(att, ssm, x, mods3, wo_att, wo_ssm, g, w_router, b_router)


def _route_kernel(slab_ref, meta_ref):
    t_n = MOE_TILE
    blk = LANES
    slab = slab_ref[...]
    lane = lax.broadcasted_iota(jnp.int32, (t_n, LANES), 1)
    gcol = jnp.sum(jnp.where(lane == EXPERTS_PER_GROUP, slab, 0.0), axis=-1, keepdims=True)
    member = (lane.astype(F32) == gcol) & (lane < N_GROUPS)
    a = jnp.where(member, 1.0, 0.0).astype(BF16)
    r_i = lax.broadcasted_iota(jnp.int32, (blk, blk), 0)
    c_i = lax.broadcasted_iota(jnp.int32, (blk, blk), 1)
    lower = jnp.where(c_i < r_i, 1.0, 0.0).astype(BF16)
    upper = jnp.where(r_i < c_i, 1.0, 0.0).astype(BF16)
    offs = jnp.zeros((1, LANES), F32)
    ranks = []
    for b in range(t_n // blk):
        ab = a[b * blk:(b + 1) * blk]
        rb = jnp.dot(lower, ab, preferred_element_type=F32)
        ranks.append(rb + offs)
        offs = offs + rb[blk - 1:blk] + ab[blk - 1:blk].astype(F32)
    rank = jnp.concatenate(ranks, axis=0)
    n_chunk = jnp.floor((offs + (MOE_CHUNK - 1)) * (1.0 / MOE_CHUNK))
    start = jnp.dot(jnp.broadcast_to(n_chunk, (8, LANES)).astype(BF16), upper,
                    preferred_element_type=F32)[0:1]
    end = start + n_chunk
    dest = jnp.sum(jnp.where(member, start * MOE_CHUNK + rank, 0.0), axis=-1, keepdims=True)
    tok = lax.broadcasted_iota(jnp.int32, (t_n, LANES), 0)
    digits = jnp.where(lane == 0, (tok // blk).astype(F32),
                       jnp.where(lane == 1, (tok % blk).astype(F32), jnp.where(lane == 2, 1.0, 0.0))).astype(BF16)
    sw = 512
    pieces = []
    for sc in range(MOE_ROWS // sw):
        s_id = (lax.broadcasted_iota(jnp.int32, (t_n, sw), 1) + sc * sw).astype(F32)
        hit = jnp.where(dest == s_id, 1.0, 0.0).astype(BF16)
        r = lax.dot_general(digits, hit, (((0,), (0,)), ((), ())), preferred_element_type=F32)
        tok_of = r[0:1] * blk + r[1:2]
        pieces.append(jnp.where(r[2:3] > 0.5, tok_of, float(t_n)))
    pieces.append(jnp.full((1, META_ROWS - MOE_ROWS), float(t_n), F32))
    perm = jnp.concatenate(pieces, axis=1)
    slot = lax.broadcasted_iota(jnp.int32, (1, META_ROWS), 1).astype(F32)
    lane1 = lax.broadcasted_iota(jnp.int32, (1, LANES), 1)
    cg = jnp.zeros((1, META_ROWS), F32)
    for g in range(N_GROUPS):
        end_g = jnp.sum(jnp.where(lane1 == g, end, 0.0), axis=-1, keepdims=True)
        cg = cg + jnp.where(slot >= end_g, 1.0, 0.0)
    n_act = jnp.broadcast_to(end_g, (1, META_ROWS))
    meta_ref[...] = jnp.concatenate([perm, cg, n_act, jnp.zeros((5, META_ROWS), F32)], axis=0).astype(jnp.int32)


def _route(h2x):
    n = h2x.shape[0]
    n_tiles = n // MOE_TILE
    return pl.pallas_call(
        _route_kernel,
        grid=(n_tiles,),
        in_specs=[pl.BlockSpec((MOE_TILE, LANES), lambda i: (i, D_MODEL // LANES))],
        out_specs=pl.BlockSpec((None, 8, META_ROWS), lambda i: (i, 0, 0)),
        out_shape=jax.ShapeDtypeStruct((n_tiles, 8, META_ROWS), jnp.int32),
        compiler_params=_cparams(("parallel",)),
        name="moe_route",
    )(h2x)


def _moe_kernel(perm_ref, cg_ref, nact_ref, run_ref, h_ref, wg_hbm, wu_hbm, wd_hbm, y_ref,
                hs0, hs1, ys0, ys1, wg_buf, wu_buf, wd_buf, w_sem):
    i = pl.program_id(0)
    s = pl.program_id(1)
    n_act = nact_ref[i]
    tile_base = i * META_ROWS
    hs = (hs0, hs1)
    ys = (ys0, ys1)
    n_steps = pl.num_programs(0) * MOE_STEPS
    f = i * MOE_STEPS + s
    slot = run_ref[n_steps + f]

    def weight_copies(g, sl):
        e0 = g * EXPERTS_PER_GROUP
        r0 = g * (EXPERTS_PER_GROUP * EXPERT_FF)
        return (pltpu.make_async_copy(wg_hbm.at[pl.ds(e0, EXPERTS_PER_GROUP)], wg_buf.at[sl], w_sem.at[sl, 0]),
                pltpu.make_async_copy(wu_hbm.at[pl.ds(e0, EXPERTS_PER_GROUP)], wu_buf.at[sl], w_sem.at[sl, 1]),
                pltpu.make_async_copy(wd_hbm.at[pl.ds(r0, EXPERTS_PER_GROUP * EXPERT_FF)], wd_buf.at[sl],
                                      w_sem.at[sl, 2]))

    @pl.when(run_ref[f] == 1)
    def _():
        g = jnp.minimum(cg_ref[f], N_GROUPS - 1)

        @pl.when(run_ref[3 * n_steps + f] == 1)
        def _():
            for cp in weight_copies(g, slot):
                cp.start()

        for cp in weight_copies(g, slot):
            cp.wait()
        nxt = run_ref[2 * n_steps + f]

        @pl.when(nxt >= 0)
        def _():
            for cp in weight_copies(nxt, 1 - slot):
                cp.start()

    def gather(chunk, dst):
        base = tile_base + chunk * MOE_CHUNK
        for r in range(MOE_CHUNK):
            src = jnp.minimum(perm_ref[base + r], MOE_TILE - 1)
            dst[r:r + 1, :] = h_ref[pl.ds(src, 1), :]

    def scatter(chunk, src):
        base = tile_base + chunk * MOE_CHUNK
        for r in range(MOE_CHUNK):
            y_ref[pl.ds(perm_ref[base + r], 1), :] = src[r:r + 1, :]

    def ffn(src, dst):
        hb = src[:, 0:D_MODEL].astype(BF16)
        cw = src[:, D_MODEL:H2X_W]
        hid = []
        for e in range(EXPERTS_PER_GROUP):
            a = jnp.dot(hb, wg_buf[slot, e], preferred_element_type=F32)
            u = jnp.dot(hb, wu_buf[slot, e], preferred_element_type=F32)
            hid.append((_silu(a) * u * cw[:, e:e + 1]).astype(BF16))
        dst[...] = jnp.dot(jnp.concatenate(hid, axis=1), wd_buf[slot], preferred_element_type=F32)

    @pl.when(s == 0)
    def _():
        y_ref[MOE_TILE:MOE_TILE + 8, :] = jnp.zeros((8, D_MODEL), F32)
        ys1[...] = jnp.zeros_like(ys1)
        gather(0, hs0)

    for par in (0, 1):
        @pl.when((s < n_act) & (s % 2 == par))
        def _():
            gather(s + 1, hs[1 - par])
            ffn(hs[par], ys[par])
            scatter(jnp.maximum(s - 1, 0), ys[1 - par])

        @pl.when((s == n_act) & (s % 2 == par))
        def _():
            scatter(s - 1, ys[1 - par])


def _moe(h2x, perm, cgrp, nact, wg, wu, wd):
    n = h2x.shape[0]
    n_tiles = n // MOE_TILE
    n_steps = n_tiles * MOE_STEPS

    step = jnp.arange(n_steps)
    active = (step % MOE_STEPS) < jnp.repeat(nact, MOE_STEPS)
    prev = jnp.concatenate([jnp.full((1,), -1, jnp.int32), cgrp[:-1]])
    first = active & ((step % MOE_STEPS == 0) | (cgrp != prev))
    run_id = jnp.cumsum(first.astype(jnp.int32)) - 1
    later_first = lax.cummin(jnp.where(first, step, n_steps), reverse=True)
    nxt_step = jnp.concatenate([later_first[1:], jnp.full((1,), n_steps, jnp.int32)])
    nxt_group = jnp.where(nxt_step < n_steps, cgrp[jnp.minimum(nxt_step, n_steps - 1)], -1)
    runs = jnp.concatenate([first.astype(jnp.int32), run_id % 2, nxt_group.astype(jnp.int32),
                            (first & (run_id == 0)).astype(jnp.int32)])

    def h_idx(i, s, *_):
        done = (s >= jnp.maximum(_[2][i] - 1, 1)).astype(jnp.int32)
        return (jnp.minimum(i + done, n_tiles - 1), 0)

    hbm = pl.BlockSpec(memory_space=pl.ANY)
    return pl.pallas_call(
        _moe_kernel,
        grid_spec=pltpu.PrefetchScalarGridSpec(
            num_scalar_prefetch=4,
            grid=(n_tiles, MOE_STEPS),
            in_specs=[pl.BlockSpec((MOE_TILE, H2X_W), h_idx), hbm, hbm, hbm],
            out_specs=pl.BlockSpec((None, MOE_TILE + 8, D_MODEL), lambda i, s, *_: (i, 0, 0)),
            scratch_shapes=[pltpu.VMEM((MOE_CHUNK, H2X_W), F32), pltpu.VMEM((MOE_CHUNK, H2X_W), F32),
                            pltpu.VMEM((MOE_CHUNK, D_MODEL), F32), pltpu.VMEM((MOE_CHUNK, D_MODEL), F32),
                            pltpu.VMEM((2, EXPERTS_PER_GROUP, D_MODEL, EXPERT_FF), BF16),
                            pltpu.VMEM((2, EXPERTS_PER_GROUP, D_MODEL, EXPERT_FF), BF16),
                            pltpu.VMEM((2, EXPERTS_PER_GROUP * EXPERT_FF, D_MODEL), BF16),
                            pltpu.SemaphoreType.DMA((2, 3))]),
        out_shape=jax.ShapeDtypeStruct((n_tiles, MOE_TILE + 8, D_MODEL), F32),
        compiler_params=_cparams(("arbitrary", "arbitrary")),
        name="moe_experts",
    )(perm, cgrp, nact, runs, h2x, wg, wu, wd)


def _final_kernel(y_ref, x1_ref, mods_ref, fg_ref, o_ref):
    gate2 = mods_ref[:, 5 * D_MODEL:6 * D_MODEL]
    x2 = x1_ref[...] + gate2 * y_ref[...]
    o_ref[...] = x2 * lax.rsqrt(jnp.mean(x2 * x2, axis=-1, keepdims=True) + EPS) * fg_ref[...]


def _final(y, x1, mods3, mod_row0, mod_tokens, fg):
    n = x1.shape[0]
    tm = 512
    per_mod = mod_tokens // tm
    per_tile = MOE_TILE // tm
    return pl.pallas_call(
        _final_kernel,
        grid=(n // tm,),
        in_specs=[pl.BlockSpec((None, tm, D_MODEL), lambda j: (j // per_tile, j % per_tile, 0)),
                  pl.BlockSpec((tm, D_MODEL), lambda j: (j, 0)),
                  pl.BlockSpec((None, 1, 6 * D_MODEL), lambda j: (mod_row0 + j // per_mod, 0, 0)),
                  pl.BlockSpec((1, D_MODEL), lambda j: (0, 0))],
        out_specs=pl.BlockSpec((tm, D_MODEL), lambda j: (j, 0)),
        out_shape=jax.ShapeDtypeStruct((n, D_MODEL), F32),
        compiler_params=_cparams(("parallel",)),
        name="final_norm",
    )(y, x1, mods3, fg)


def _rope_tables(t):
    n_freq = QK_DIM // 4
    n_rows = t // GRID_W
    freqs = ROPE_BASE ** (-jnp.arange(n_freq, dtype=F32) / n_freq)
    ang_r = jnp.arange(n_rows, dtype=F32)[:, None] * freqs
    ang_c = jnp.arange(GRID_W, dtype=F32)[:, None] * freqs
    cr, sr, cc, sc = lax.optimization_barrier((jnp.cos(ang_r), jnp.sin(ang_r), jnp.cos(ang_c), jnp.sin(ang_c)))
    j = np.arange(LANES) % QK_DIM
    f_idx = j % n_freq
    by_row = (j < QK_DIM // 2)[None, None, :]
    first = ((j % (QK_DIM // 2)) < n_freq)[None, None, :]

    def table(r_small, c_small):
        return jnp.where(by_row, r_small[:, f_idx][:, None, :], c_small[:, f_idx][None, :, :])

    cos = table(cr, cc)
    sin = table(sr, sc)
    return (cos.reshape(t, LANES), jnp.where(first, -sin, 0.0).reshape(t, LANES),
            jnp.where(first, 0.0, sin).reshape(t, LANES))


def _layer(x, mods3, mod_row0, mod_tokens, rope_tabs, ctx_k, ctx_v, h0f, h0b, lw, layer):
    b, t, _ = x.shape
    n = b * t
    xf = x.reshape(n, D_MODEL)
    res = _inproj(xf, mods3, mod_row0, mod_tokens, t, lw["norm_mix_g"], lw["w_main"], lw["w_dt"], lw["alog"],
                  lw["dtb"], rope_tabs)
    q, kb, vt, z, xbc, cols, hrow = res[:7]
    if ctx_k is None:
        cache = None
        k3 = res[7].reshape(b, 1, t, ATT_HEADS, 2, LANES)[..., :QK_DIM]
        v3 = res[8].reshape(b, 1, t, ATT_HEADS, V_DIM)
    else:
        cache = (ctx_k.astype(BF16), jnp.swapaxes(ctx_v, 1, 2).astype(BF16))
        k3 = v3 = None
    lam0 = 0.8 - 0.6 * math.exp(-0.3 * layer)
    tq = 1024 if t % 1024 == 0 else 256
    tk = 512 if t % 512 == 0 else 256
    att = _attention(q.reshape(b, t, D_QK), kb.reshape(b, t, D_QK), vt, cache, lw["lamp"], lw["attn_subln_g"],
                     lam0, tq, tk, *((ATT_HEADS, 2) if t <= 512 else (1, 1)))
    ssm, hf, hb = _ssd(xbc, cols.reshape(b, t, LANES), hrow.reshape(b, t // CHUNK * HROWS, LANES),
                       z.reshape(b, t, D_SSM), h0f, h0b, lw["conv_w"], lw["conv_b"], lw["d_x"], lw["ssm_norm_g"])
    x1, h2x = _outproj(att.reshape(n, D_ATT), ssm.reshape(n, D_SSM), xf, mods3, mod_row0, mod_tokens,
                       lw["wo_att"], lw["wo_ssm"], lw["norm_ffn_g"], lw["w_router"], lw["b_router"])
    meta = _route(h2x)
    perm = meta[:, 0, :].reshape(-1)
    cgrp = meta[:, 1, :MOE_STEPS].reshape(-1)
    nact = meta[:, 2, 0]
    y = _moe(h2x, perm, cgrp, nact, lw["wg"], lw["wu"], lw["wd"])
    out = _final(y, x1, mods3, mod_row0, mod_tokens, lw["final_g"])
    return out.reshape(b, t, D_MODEL), k3, v3, hf, hb


def _pad_lanes(v, width=LANES):
    return jnp.pad(v, [(0, 0)] * (v.ndim - 1) + [(0, width - v.shape[-1])])


def kernel(x_prompt, x_sample, cache_k, cache_v, state_ssm_fwd, state_ssm_bwd, c, c_ctx, w_ada, b_ada, norm_mix_g, w_in, w_out, lambda_q1, lambda_k1, lambda_q2, lambda_k2, attn_subln_g, conv_w, conv_b, a_log_fwd, a_log_bwd, dt_bias_fwd, dt_bias_bwd, ssm_d, ssm_norm_g, norm_ffn_g, w_group_router, b_group_router, w_expert_router, b_expert_router, w_exp_gate, w_exp_up, w_exp_down, final_norm_g):
    depth = w_in.shape[0]
    assert depth == 1, "single trunk layer"
    bp, tp, _ = x_prompt.shape
    bs, ts, _ = x_sample.shape
    l = 0
    cond = jnp.concatenate([c_ctx[None], c], axis=0)
    condT = _pad_lanes(cond.T, 8)
    mods = _ada(condT, w_ada[l], b_ada[l][None])
    mods3 = mods.reshape(8, 1, 6 * D_MODEL)

    w_router = _pad_lanes(jnp.concatenate([w_group_router[l], w_expert_router[l]], axis=1))
    wr_hi = w_router.astype(BF16)
    wr_lo = (w_router - wr_hi.astype(F32)).astype(BF16)
    lw = dict(
        norm_mix_g=norm_mix_g[l][None],
        w_main=w_in[l].astype(BF16),
        w_dt=_pad_lanes(w_in[l][:, MAIN_COLS:]).astype(BF16),
        lamp=jnp.stack([lambda_q1[l], lambda_k1[l], lambda_q2[l], lambda_k2[l]]),
        attn_subln_g=attn_subln_g[l][None],
        conv_w=conv_w[l].reshape(CONV_W, N_COLBLK, LANES), conv_b=conv_b[l].reshape(N_COLBLK, LANES),
        alog=jnp.broadcast_to(jnp.concatenate([a_log_fwd[l], a_log_bwd[l]])[:, None], (2 * SSM_HEADS, CHUNK)),
        dtb=jnp.broadcast_to(jnp.concatenate([dt_bias_fwd[l], dt_bias_bwd[l]])[:, None], (2 * SSM_HEADS, CHUNK)),
        d_x=jnp.repeat(ssm_d[l], SSM_HEADDIM)[None], ssm_norm_g=ssm_norm_g[l][None],
        wo_att=w_out[l][:D_ATT].astype(BF16), wo_ssm=w_out[l][D_ATT:].astype(BF16),
        norm_ffn_g=norm_ffn_g[l][None],
        w_router=jnp.concatenate([wr_hi, wr_lo], axis=1),
        b_router=_pad_lanes(jnp.concatenate([b_group_router[l], b_expert_router[l]])[None]),
        wg=w_exp_gate[l].astype(BF16), wu=w_exp_up[l].astype(BF16), wd=w_exp_down[l].astype(BF16).reshape(N_EXPERTS * EXPERT_FF, D_MODEL),
        final_g=final_norm_g[None],
    )
    n_state = SSM_HEADS * SSM_HEADDIM
    yp, ck, cv, hf, hb = _layer(x_prompt, mods3, 0, bp * tp, None, None, None, None, None, lw, l)
    ys, _, _, _, _ = _layer(x_sample, mods3, 1, ts, _rope_tables(ts),
                            cache_k[:, l].reshape(bs, -1, D_QK), cache_v[:, l].reshape(bs, -1, D_ATT),
                            state_ssm_fwd[:, l].reshape(bs, n_state, D_STATE),
                            state_ssm_bwd[:, l].reshape(bs, n_state, D_STATE), lw, l)
    new_k, new_v = ck, cv
    new_hf = hf.reshape(bp, 1, SSM_HEADS, SSM_HEADDIM, D_STATE)
    new_hb = hb.reshape(bp, 1, SSM_HEADS, SSM_HEADDIM, D_STATE)
    return yp, ys, new_k, new_v, new_hf, new_hb
```

```python
import functools
import math

import numpy as np
import jax
import jax.numpy as jnp
from jax import lax
from jax.experimental import pallas as pl
from jax.experimental.pallas import tpu as pltpu

D_MODEL = 1024
GRID_W = 64
ATT_HEADS = 4
QK_DIM = 64
V_DIM = 128
D_QK = 512
D_ATT = 512
ROPE_BASE = 10000.0
D_SSM = 512
SSM_HEADDIM = 64
SSM_HEADS = 8
SSM_GROUPS = 2
D_STATE = 128
CONV_W = 5
CHUNK = 128
XBC_DIM = 1024
N_GROUPS = 4
EXPERTS_PER_GROUP = 4
N_EXPERTS = 16
EXPERT_FF = 256
EPS = 1e-6
MAIN_COLS = 2 * D_QK + D_ATT + D_SSM + XBC_DIM
H2X_W = D_MODEL + 128
MOE_TILE = 2048
MOE_CHUNK = 256
MOE_SLOTS = MOE_TILE // MOE_CHUNK + N_GROUPS
MOE_ROWS = MOE_SLOTS * MOE_CHUNK
MOE_STEPS = MOE_SLOTS + 1
META_ROWS = MOE_STEPS * MOE_CHUNK
OUT_PART = 512
COL_CUM, COL_DT, COL_ECUM, COL_TOEND = 0, 16, 32, 48
HROWS = 32
LANES = 128
HALO = 8
VMEM_LIMIT = 56 * 1024 * 1024

LOG2E = math.log2(math.e)
SUM_ROWS = 16
F32 = jnp.float32
BF16 = jnp.bfloat16


def _cparams(sem):
    return pltpu.CompilerParams(dimension_semantics=sem, vmem_limit_bytes=VMEM_LIMIT)


def _sigmoid(x):
    return 1.0 / (1.0 + jnp.exp(-x))


def _silu(x):
    return x * _sigmoid(x)


def _ada_kernel(condT_ref, w_ref, b_ref, o_ref):
    s = _silu(condT_ref[...])
    w = w_ref[...]
    b = b_ref[...]
    o_ref[...] = jnp.zeros_like(o_ref)
    for r in range(3):
        o_ref[r:r + 1, :] = jnp.sum(w * s[:, r:r + 1], axis=0, keepdims=True) + b


def _ada(condT, w_ada, b_ada):
    bn = 1024
    n = w_ada.shape[1]
    return pl.pallas_call(
        _ada_kernel,
        grid=(n // bn,),
        in_specs=[pl.BlockSpec((D_MODEL, 8), lambda j: (0, 0)),
                  pl.BlockSpec((D_MODEL, bn), lambda j: (0, j)),
                  pl.BlockSpec((1, bn), lambda j: (0, j))],
        out_specs=pl.BlockSpec((8, bn), lambda j: (0, j)),
        out_shape=jax.ShapeDtypeStruct((8, n), F32),
        compiler_params=_cparams(("arbitrary",)),
        name="ada",
    )(condT, w_ada, b_ada)


def _split3(x):
    hi = x.astype(BF16)
    r1 = x - hi.astype(F32)
    mid = r1.astype(BF16)
    lo = (r1 - mid.astype(F32)).astype(BF16)
    return jnp.concatenate([hi, mid, lo], axis=-1)


def _scan_matrices():
    t = np.arange(CHUNK)
    pre = (t[:, None] <= t[None, :]).astype(np.float32)
    suf = (t[:, None] >= t[None, :]).astype(np.float32)
    return jnp.asarray(np.stack([np.concatenate([pre] * 3, axis=0), np.concatenate([suf] * 3, axis=0)]), BF16)


def _head_scalars(dt_raw, alog_ref, dtb_ref, scan_ref):
    nh2 = 2 * SSM_HEADS
    xv = dt_raw.T[0:nh2, :] + dtb_ref[...]
    dt = jnp.maximum(xv, 0.0) + jnp.log(1.0 + jnp.exp(-jnp.abs(xv)))
    la3 = _split3(dt * (-jnp.exp(alog_ref[...])))
    fwd = lax.broadcasted_iota(jnp.int32, (nh2, CHUNK), 0) < SSM_HEADS
    cum = jnp.where(fwd, jnp.dot(la3, scan_ref[0], preferred_element_type=F32),
                    jnp.dot(la3, scan_ref[1], preferred_element_type=F32))
    cum_end = jnp.where(fwd, cum[:, CHUNK - 1:CHUNK], cum[:, 0:1])
    packed = jnp.concatenate([cum, dt, jnp.exp(cum), jnp.exp(cum_end - cum),
                              jnp.zeros((LANES - 4 * nh2, CHUNK), F32)], axis=0)
    return packed.T, jnp.concatenate([cum, jnp.exp(cum_end)], axis=0)


def _inproj_kernel(rope, x_ref, mods_ref, g_ref, w_ref, wdt_ref, alog_ref, dtb_ref, scan_ref, *rest):
    if rope:
        cos_ref, sa_ref, sb_ref, q_ref, kb_ref, vt_ref, z_ref, xbc_ref, cols_ref, hrow_ref = rest
    else:
        q_ref, kb_ref, vt_ref, z_ref, xbc_ref, cols_ref, hrow_ref, kf_ref, vf_ref = rest
    x = x_ref[...]
    shift = mods_ref[:, 0:D_MODEL]
    scale = mods_ref[:, D_MODEL:2 * D_MODEL]
    y = x * lax.rsqrt(jnp.mean(x * x, axis=-1, keepdims=True) + EPS) * g_ref[...]
    h = (y * (1.0 + scale) + shift).astype(BF16)
    dt_raw = jnp.dot(h, wdt_ref[...], preferred_element_type=F32)
    for ci in range(x.shape[0] // CHUNK):
        cols, hrow = _head_scalars(dt_raw[ci * CHUNK:(ci + 1) * CHUNK, :], alog_ref, dtb_ref, scan_ref)
        cols_ref[ci * CHUNK:(ci + 1) * CHUNK, :] = cols
        hrow_ref[ci * HROWS:(ci + 1) * HROWS, :] = hrow
    r = jnp.dot(h, w_ref[...], preferred_element_type=F32)
    q = r[:, 0:D_QK]
    k = r[:, D_QK:2 * D_QK]
    if rope:
        cos = cos_ref[...]
        sa = sa_ref[...]
        sb = sb_ref[...]

        def rot(t):
            parts = []
            for hh in range(ATT_HEADS):
                th = t[:, hh * LANES:(hh + 1) * LANES]
                parts.append(th * cos + pltpu.roll(th, LANES - 16, 1) * sa + pltpu.roll(th, 16, 1) * sb)
            return jnp.concatenate(parts, axis=1)

        q = rot(q)
        k = rot(k)
    v = r[:, 2 * D_QK:2 * D_QK + D_ATT]
    q_ref[...] = q
    kb_ref[...] = k.astype(BF16)
    n_seq, _, t_seq = vt_ref.shape
    v_t = v.T.astype(BF16)
    for sq in range(n_seq):
        vt_ref[sq] = v_t[:, sq * t_seq:(sq + 1) * t_seq]
    if not rope:
        tm = k.shape[0]
        for hh in range(ATT_HEADS):
            k_h = k[:, hh * LANES:(hh + 1) * LANES]
            vf_ref[pl.ds(hh, tm, stride=ATT_HEADS), :] = v[:, hh * LANES:(hh + 1) * LANES]
            kf_ref[pl.ds(2 * hh, tm, stride=2 * ATT_HEADS), :] = k_h
            kf_ref[pl.ds(2 * hh + 1, tm, stride=2 * ATT_HEADS), :] = pltpu.roll(k_h, QK_DIM, 1)
    z_ref[...] = r[:, 2 * D_QK + D_ATT:2 * D_QK + D_ATT + D_SSM]
    x0 = 2 * D_QK + D_ATT + D_SSM
    for cb in range(XBC_DIM // LANES):
        for sq in range(n_seq):
            xbc_ref[sq, cb] = r[sq * t_seq:(sq + 1) * t_seq, x0 + cb * LANES:x0 + (cb + 1) * LANES]


def _inproj(x, mods3, mod_row0, mod_tokens, seq_len, g, w_main, w_dt, alog, dtb, rope_tabs):
    n = x.shape[0]
    tm = 512
    assert seq_len % tm == 0 or tm % seq_len == 0
    per_seq = max(seq_len // tm, 1)
    n_seq = max(tm // seq_len, 1)
    t_seq = tm // n_seq
    per_mod = mod_tokens // tm
    rope = rope_tabs is not None
    in_specs = [pl.BlockSpec((tm, D_MODEL), lambda i: (i, 0)),
                pl.BlockSpec((None, 1, 6 * D_MODEL), lambda i: (mod_row0 + i // per_mod, 0, 0)),
                pl.BlockSpec((1, D_MODEL), lambda i: (0, 0)),
                pl.BlockSpec((D_MODEL, MAIN_COLS), lambda i: (0, 0)),
                pl.BlockSpec((D_MODEL, LANES), lambda i: (0, 0)),
                pl.BlockSpec((2 * SSM_HEADS, CHUNK), lambda i: (0, 0)),
                pl.BlockSpec((2 * SSM_HEADS, CHUNK), lambda i: (0, 0)),
                pl.BlockSpec((2, 3 * CHUNK, CHUNK), lambda i: (0, 0, 0))]
    args = [x, mods3, g, w_main, w_dt, alog, dtb, _scan_matrices()]
    if rope:
        tab_spec = pl.BlockSpec((tm, LANES), lambda i: (i % per_seq, 0))
        in_specs += [tab_spec] * 3
        args += list(rope_tabs)
    def rows(wd, dtype=F32):
        return pl.BlockSpec((tm, wd), lambda i: (i, 0)), jax.ShapeDtypeStruct((n, wd), dtype)

    hr = tm // CHUNK * HROWS
    outs = [rows(D_QK), rows(D_QK, BF16),
            (pl.BlockSpec((n_seq, D_ATT, t_seq), lambda i: (i // per_seq, 0, i % per_seq)),
             jax.ShapeDtypeStruct((n // seq_len, D_ATT, seq_len), BF16)),
            rows(D_SSM),
            (pl.BlockSpec((n_seq, XBC_DIM // LANES, t_seq, LANES), lambda i: (i // per_seq, 0, i % per_seq, 0)),
             jax.ShapeDtypeStruct((n // seq_len, XBC_DIM // LANES, seq_len, LANES), F32)),
            rows(LANES),
            (pl.BlockSpec((hr, LANES), lambda i: (i, 0)), jax.ShapeDtypeStruct((n // CHUNK * HROWS, LANES), F32))]
    if not rope:
        outs += [(pl.BlockSpec((tm * 2 * ATT_HEADS, LANES), lambda i: (i, 0)),
                  jax.ShapeDtypeStruct((n * 2 * ATT_HEADS, LANES), F32)),
                 (pl.BlockSpec((tm * ATT_HEADS, LANES), lambda i: (i, 0)),
                  jax.ShapeDtypeStruct((n * ATT_HEADS, LANES), F32))]
    return pl.pallas_call(
        functools.partial(_inproj_kernel, rope),
        grid=(n // tm,),
        in_specs=in_specs,
        out_specs=[o[0] for o in outs],
        out_shape=[o[1] for o in outs],
        compiler_params=_cparams(("parallel",)),
        name="inproj_rope" if rope else "inproj",
    )(*args)


def _attn_kernel(tk, lam0, has_cache, q_ref, k_ref, vt_ref, *rest):
    if has_cache:
        ck_ref, cvt_ref, lamp_ref, g_ref, o_ref = rest
    else:
        lamp_ref, g_ref, o_ref = rest
    tq = q_ref.shape[1]
    lp = lamp_ref[...]
    lam = (jnp.exp(jnp.sum(lp[0:1] * lp[1:2], axis=-1, keepdims=True))
           - jnp.exp(jnp.sum(lp[2:3] * lp[3:4], axis=-1, keepdims=True)) + lam0)
    for bb, hh in [(b_, h_) for b_ in range(q_ref.shape[0]) for h_ in range(q_ref.shape[2] // LANES)]:
        hs = slice(hh * LANES, (hh + 1) * LANES)
        chunks = [(k_ref, vt_ref, c * tk, tk) for c in range(k_ref.shape[1] // tk)]
        if has_cache:
            ck = min(tk, ck_ref.shape[1])
            chunks += [(ck_ref, cvt_ref, c * ck, ck) for c in range(ck_ref.shape[1] // ck)]
        q = q_ref[bb, :, hs] * (QK_DIM ** -0.5 * LOG2E)
        lane = lax.broadcasted_iota(jnp.int32, q.shape, 1)
        qq_t = jnp.concatenate([jnp.where(lane < QK_DIM, q, 0.0), jnp.where(lane >= QK_DIM, q, 0.0)],
                               axis=0).T.astype(BF16)

        def scores(chunk):
            kr, _, start, size = chunk
            return jnp.dot(kr[bb, start:start + size, hs], qq_t, preferred_element_type=F32)

        def update(s, chunk, m, acc):
            _, vr, start, size = chunk
            m_new = jnp.maximum(m, jnp.max(s, axis=0, keepdims=True))
            alpha = jnp.exp2(m - m_new)
            p = jnp.exp2(s - m_new).astype(BF16)
            v_ext = jnp.concatenate([vr[bb, hs, start:start + size], jnp.ones((SUM_ROWS, size), BF16)], axis=0)
            acc = alpha * acc + jnp.dot(v_ext, p, preferred_element_type=F32)
            return m_new, acc

        m = jnp.full((1, 2 * tq), -jnp.inf, F32)
        acc = jnp.zeros((V_DIM + SUM_ROWS, 2 * tq), F32)
        s = scores(chunks[0])
        for c, chunk in enumerate(chunks):
            s_next = scores(chunks[c + 1]) if c + 1 < len(chunks) else None
            m, acc = update(s, chunk, m, acc)
            s = s_next
        o = acc[0:V_DIM] / acc[V_DIM:V_DIM + 1]
        o = (o[:, 0:tq] - lam * o[:, tq:2 * tq]).T
        o = o * lax.rsqrt(jnp.mean(o * o, axis=-1, keepdims=True) + EPS)
        o_ref[bb, :, hs] = o * g_ref[...] * (1.0 - lam0)


def _attention(q, k, vt, cache, lamp, g, lam0, tq, tk, nh, nb):
    b, t, _ = q.shape
    wd = nh * LANES

    def kv_specs(length):
        return [pl.BlockSpec((nb, length, wd), lambda bi, h, i: (bi, 0, h)),
                pl.BlockSpec((nb, wd, length), lambda bi, h, i: (bi, h, 0))]

    in_specs = [pl.BlockSpec((nb, tq, wd), lambda bi, h, i: (bi, i, h))] + kv_specs(t)
    args = [q, k, vt]
    if cache is not None:
        assert cache[0].shape[1] % min(tk, cache[0].shape[1]) == 0
        in_specs += kv_specs(cache[0].shape[1])
        args += list(cache)
    in_specs += [pl.BlockSpec((4, QK_DIM), lambda bi, h, i: (0, 0)),
                 pl.BlockSpec((1, V_DIM), lambda bi, h, i: (0, 0))]
    return pl.pallas_call(
        functools.partial(_attn_kernel, tk, lam0, cache is not None),
        grid=(b // nb, ATT_HEADS // nh, t // tq),
        in_specs=in_specs,
        out_specs=pl.BlockSpec((nb, tq, wd), lambda bi, h, i: (bi, i, h)),
        out_shape=jax.ShapeDtypeStruct((b, t, D_ATT), F32),
        compiler_params=_cparams(("parallel", "parallel", "arbitrary")),
        name="diff_attn",
    )(*args, lamp, g)


N_COLBLK = XBC_DIM // LANES
def _expand_matrices():
    out = []
    for d in range(2):
        e = np.zeros((LANES, 3 * D_SSM), np.float32)
        for blk, lane0 in enumerate((COL_DT, COL_ECUM, COL_TOEND)):
            for h in range(SSM_HEADS):
                e[lane0 + d * SSM_HEADS + h,
                  blk * D_SSM + h * SSM_HEADDIM:blk * D_SSM + (h + 1) * SSM_HEADDIM] = 1.0
        out.append(np.concatenate([e, e, e], axis=0))
    return jnp.asarray(np.stack(out), BF16)


def _ssd_chunk(reverse, xs, bm, cm, state, hrow, cols, e_ref):
    d0 = SSM_HEADS if reverse else 0
    cum = hrow[d0:d0 + SSM_HEADS, :]
    dec = hrow[2 * SSM_HEADS + d0:2 * SSM_HEADS + d0 + SSM_HEADS, 0:1]
    row = lax.broadcasted_iota(jnp.int32, (CHUNK, LANES), 0)
    lane = lax.broadcasted_iota(jnp.int32, (CHUNK, LANES), 1)
    causal = (row <= lane) if reverse else (row >= lane)
    lane_g = lax.broadcasted_iota(jnp.int32, (CHUNK, 2 * LANES), 1)
    spread = jnp.dot(_split3(cols), e_ref[1 if reverse else 0], preferred_element_type=F32)
    xd = xs * spread[:, 0:D_SSM]
    xdw = (xd * spread[:, 2 * D_SSM:3 * D_SSM]).astype(BF16)
    xd = xd.astype(BF16)
    ecum_x = spread[:, D_SSM:2 * D_SSM]
    rep = SSM_HEADS // SSM_GROUPS
    y_parts = []
    new_state = []
    for g in range(SSM_GROUPS):
        bg = bm[:, g * D_STATE:(g + 1) * D_STATE]
        cg = cm[:, g * D_STATE:(g + 1) * D_STATE]
        cbt = lax.dot_general(cg, bg, (((1,), (1,)), ((), ())), preferred_element_type=F32)
        rows = slice(g * rep * SSM_HEADDIM, (g + 1) * rep * SSM_HEADDIM)
        st_g = state[rows, :]
        y_off = lax.dot_general(cg, st_g.astype(BF16), (((1,), (1,)), ((), ())),
                                preferred_element_type=F32)
        cst = lax.dot_general(xdw[:, rows], bg, (((0,), (0,)), ((), ())), preferred_element_type=F32)
        xd_g = xd[:, rows]
        scs = []
        blocks = []
        for hh in range(rep):
            h = g * rep + hh
            seg = cols[:, COL_CUM + d0 + h:COL_CUM + d0 + h + 1] - cum[h:h + 1, :]
            decay = jnp.exp(jnp.where(causal, seg, -jnp.inf))
            scs.append((cbt * decay).astype(BF16))
            blocks.append(jnp.where(lane_g // SSM_HEADDIM == hh, xd_g, jnp.zeros_like(xd_g)))
        y_diag = jnp.dot(jnp.concatenate(scs, axis=1), jnp.concatenate(blocks, axis=0),
                         preferred_element_type=F32)
        dec_rows = jnp.concatenate(
            [jnp.broadcast_to(dec[g * rep + hh:g * rep + hh + 1, :], (SSM_HEADDIM, D_STATE)) for hh in range(rep)],
            axis=0)
        new_state.append(st_g * dec_rows + cst)
        y_parts.append(y_diag + y_off * ecum_x[:, rows])
    return jnp.concatenate(y_parts, axis=1), jnp.concatenate(new_state, axis=0)


def _ssd_kernel(nc, has_h0, xc_ref, xp_ref, xn_ref, cols_ref, hrow_ref, z_ref, *rest):
    if has_h0:
        h0f_ref, h0b_ref = rest[:2]
        rest = rest[2:]
    cw_ref, cb_ref, d_ref, ng_ref, e_ref, y_ref, hf_ref, hb_ref, state_ref, slab_ref, xs_ref, bc_ref, yf_ref = rest
    ps = pl.program_id(1)
    c = pl.program_id(2)
    n_bc = SSM_GROUPS * D_STATE

    @pl.when((ps == 0) & (c == 0))
    def _():
        state_ref[...] = h0f_ref[...] if has_h0 else jnp.zeros_like(state_ref)

    @pl.when(ps == 0)
    def _():
        t0 = pl.multiple_of(c * CHUNK, CHUNK)
        for bb in range(state_ref.shape[0]):
            slab_ref[bb, :, 0:HALO, :] = jnp.where(c > 0, xp_ref[bb], 0.0)
            slab_ref[bb, :, HALO:HALO + CHUNK, :] = xc_ref[bb]
            slab_ref[bb, :, HALO + CHUNK:2 * HALO + CHUNK, :] = jnp.where(c < nc - 1, xn_ref[bb], 0.0)
            blocks = []
            for cb in range(N_COLBLK):
                conv = cb_ref[cb:cb + 1, :]
                for kk in range(CONV_W):
                    off = HALO - CONV_W // 2 + kk
                    conv = conv + slab_ref[bb, cb, off:off + CHUNK, :] * cw_ref[kk, cb:cb + 1, :]
                blocks.append(_silu(conv))
            xs = jnp.concatenate(blocks[0:D_SSM // LANES], axis=1)
            bc = jnp.concatenate(blocks[D_SSM // LANES:], axis=1).astype(BF16)
            y, state = _ssd_chunk(False, xs, bc[:, 0:n_bc], bc[:, n_bc:], state_ref[bb], hrow_ref[bb],
                                  cols_ref[bb], e_ref)
            xs_ref[bb, pl.ds(t0, CHUNK), :] = xs
            bc_ref[bb, pl.ds(t0, CHUNK), :] = bc
            yf_ref[bb, pl.ds(t0, CHUNK), :] = y
            state_ref[bb] = state

    @pl.when((ps == 0) & (c == nc - 1))
    def _():
        hf_ref[...] = state_ref[...]
        state_ref[...] = h0b_ref[...] if has_h0 else jnp.zeros_like(state_ref)

    @pl.when(ps == 1)
    def _():
        t0 = pl.multiple_of((nc - 1 - c) * CHUNK, CHUNK)
        for bb in range(state_ref.shape[0]):
            xs = xs_ref[bb, pl.ds(t0, CHUNK), :]
            bc = bc_ref[bb, pl.ds(t0, CHUNK), :]
            y, state = _ssd_chunk(True, xs, bc[:, 0:n_bc], bc[:, n_bc:], state_ref[bb], hrow_ref[bb],
                                  cols_ref[bb], e_ref)
            state_ref[bb] = state
            y = y + yf_ref[bb, pl.ds(t0, CHUNK), :] + xs * d_ref[...]
            y = y * _silu(z_ref[bb])
            y = y * lax.rsqrt(jnp.mean(y * y, axis=-1, keepdims=True) + EPS)
            y_ref[bb] = y * ng_ref[...]

    @pl.when((ps == 1) & (c == nc - 1))
    def _():
        hb_ref[...] = state_ref[...]


def _ssd(xbc, cols, hrow, z, h0f, h0b, conv_w, conv_b, d_x, norm_g):
    b, _, l, _ = xbc.shape
    has_h0 = h0f is not None
    nb = 2
    assert b % nb == 0
    nc = l // CHUNK
    hb = CHUNK // HALO
    n_state = SSM_HEADS * SSM_HEADDIM
    last = nc - 1

    def fwd_chunk(ps, ci):
        return jnp.where(ps == 0, ci, last)

    def any_chunk(ps, ci):
        return jnp.where(ps == 0, ci, last - ci)

    def bwd_chunk(ps, ci):
        return jnp.where(ps == 0, last, last - ci)

    const2 = lambda bi, ps, ci: (0, 0)
    state_spec = pl.BlockSpec((nb, n_state, D_STATE), lambda bi, ps, ci: (bi, 0, 0))
    in_specs = [pl.BlockSpec((nb, N_COLBLK, CHUNK, LANES), lambda bi, ps, ci: (bi, 0, fwd_chunk(ps, ci), 0)),
                pl.BlockSpec((nb, N_COLBLK, HALO, LANES),
                             lambda bi, ps, ci: (bi, 0, jnp.maximum(fwd_chunk(ps, ci) * hb - 1, 0), 0)),
                pl.BlockSpec((nb, N_COLBLK, HALO, LANES),
                             lambda bi, ps, ci: (bi, 0, jnp.minimum((fwd_chunk(ps, ci) + 1) * hb, l // HALO - 1), 0)),
                pl.BlockSpec((nb, CHUNK, LANES), lambda bi, ps, ci: (bi, any_chunk(ps, ci), 0)),
                pl.BlockSpec((nb, HROWS, LANES), lambda bi, ps, ci: (bi, any_chunk(ps, ci), 0)),
                pl.BlockSpec((nb, CHUNK, D_SSM), lambda bi, ps, ci: (bi, bwd_chunk(ps, ci), 0)),
                *([state_spec, state_spec] if has_h0 else []),
                pl.BlockSpec((CONV_W, N_COLBLK, LANES), lambda bi, ps, ci: (0, 0, 0)),
                pl.BlockSpec((N_COLBLK, LANES), const2),
                pl.BlockSpec((1, D_SSM), const2),
                pl.BlockSpec((1, D_SSM), const2),
                pl.BlockSpec((2, 3 * LANES, 3 * D_SSM), lambda bi, ps, ci: (0, 0, 0))]
    return pl.pallas_call(
        functools.partial(_ssd_kernel, nc, has_h0),
        grid=(b // nb, 2, nc),
        in_specs=in_specs,
        out_specs=[pl.BlockSpec((nb, CHUNK, D_SSM), lambda bi, ps, ci: (bi, bwd_chunk(ps, ci), 0)),
                   state_spec, state_spec],
        out_shape=[jax.ShapeDtypeStruct((b, l, D_SSM), F32),
                   jax.ShapeDtypeStruct((b, n_state, D_STATE), F32),
                   jax.ShapeDtypeStruct((b, n_state, D_STATE), F32)],
        scratch_shapes=[pltpu.VMEM((nb, n_state, D_STATE), F32),
                        pltpu.VMEM((nb, N_COLBLK, CHUNK + 2 * HALO, LANES), F32),
                        pltpu.VMEM((nb, l, D_SSM), F32),
                        pltpu.VMEM((nb, l, 2 * SSM_GROUPS * D_STATE), BF16),
                        pltpu.VMEM((nb, l, D_SSM), F32)],
        compiler_params=_cparams(("parallel", "arbitrary", "arbitrary")),
        name="ssd",
    )(xbc, xbc, xbc, cols, hrow, z, *([h0f, h0b] if has_h0 else []), conv_w, conv_b, d_x, norm_g,
      _expand_matrices())


def _outproj_kernel(att_ref, ssm_ref, x_ref, mods_ref, wa_ref, ws_ref, g_ref, wr_ref, br_ref,
                    x1_ref, h2x_ref):
    for part in range(x_ref.shape[0] // OUT_PART):
        rows = slice(part * OUT_PART, (part + 1) * OUT_PART)
        _outproj_rows(rows, att_ref, ssm_ref, x_ref, mods_ref, wa_ref, ws_ref, g_ref, wr_ref, br_ref,
                      x1_ref, h2x_ref)


def _outproj_rows(rows, att_ref, ssm_ref, x_ref, mods_ref, wa_ref, ws_ref, g_ref, wr_ref, br_ref,
                  x1_ref, h2x_ref):
    mix = (jnp.dot(att_ref[rows, :].astype(BF16), wa_ref[...], preferred_element_type=F32)
           + jnp.dot(ssm_ref[rows, :].astype(BF16), ws_ref[...], preferred_element_type=F32))
    gate1 = mods_ref[:, 2 * D_MODEL:3 * D_MODEL]
    shift2 = mods_ref[:, 3 * D_MODEL:4 * D_MODEL]
    scale2 = mods_ref[:, 4 * D_MODEL:5 * D_MODEL]
    x1 = x_ref[rows, :] + gate1 * mix
    x1_ref[rows, :] = x1
    y = x1 * lax.rsqrt(jnp.mean(x1 * x1, axis=-1, keepdims=True) + EPS) * g_ref[...]
    h2 = y * (1.0 + scale2) + shift2
    h_hi = h2.astype(BF16)
    h2x_ref[rows, 0:D_MODEL] = h2
    h_lo = (h2 - h_hi.astype(F32)).astype(BF16)
    both = jnp.dot(h_hi, wr_ref[...], preferred_element_type=F32)
    logits = (both[:, 0:LANES] + both[:, LANES:2 * LANES]
              + jnp.dot(h_lo, wr_ref[:, 0:LANES], preferred_element_type=F32)) + br_ref[...]
    lane = lax.broadcasted_iota(jnp.int32, logits.shape, 1).astype(F32)
    neg = -jnp.inf
    big = float(1 << 20)
    is_g = lane < N_GROUPS
    gl = jnp.where(is_g, logits, neg)
    gmax = jnp.max(gl, axis=-1, keepdims=True)
    g_idx = jnp.min(jnp.where(gl == gmax, lane, big), axis=-1, keepdims=True)
    p_g = 1.0 / jnp.sum(jnp.where(is_g, jnp.exp(gl - gmax), 0.0), axis=-1, keepdims=True)
    e_lo = N_GROUPS + g_idx * EXPERTS_PER_GROUP
    in_grp = (lane >= e_lo) & (lane < e_lo + EXPERTS_PER_GROUP)
    el = jnp.where(in_grp, logits, neg)
    m1 = jnp.max(el, axis=-1, keepdims=True)
    i1 = jnp.min(jnp.where(el == m1, lane, big), axis=-1, keepdims=True)
    el2 = jnp.where(lane == i1, neg, el)
    m2 = jnp.max(el2, axis=-1, keepdims=True)
    i2 = jnp.min(jnp.where(el2 == m2, lane, big), axis=-1, keepdims=True)
    e2 = jnp.exp(m2 - m1)
    w1 = p_g / (1.0 + e2)
    w2 = p_g * e2 / (1.0 + e2)
    slab = (jnp.where(lane == i1 - e_lo, w1, 0.0) + jnp.where(lane == i2 - e_lo, w2, 0.0)
            + jnp.where(lane == EXPERTS_PER_GROUP, g_idx, 0.0))
    h2x_ref[rows, D_MODEL:H2X_W] = slab


def _outproj(att, ssm, x, mods3, mod_row0, mod_tokens, wo_att, wo_ssm, g, w_router, b_router):
    n = x.shape[0]
    tm = 2 * OUT_PART
    per_mod = mod_tokens // tm
    return pl.pallas_call(
        _outproj_kernel,
        grid=(n // tm,),
        in_specs=[pl.BlockSpec((tm, D_ATT), lambda i: (i, 0)),
                  pl.BlockSpec((tm, D_SSM), lambda i: (i, 0)),
                  pl.BlockSpec((tm, D_MODEL), lambda i: (i, 0)),
                  pl.BlockSpec((None, 1, 6 * D_MODEL), lambda i: (mod_row0 + i // per_mod, 0, 0)),
                  pl.BlockSpec((D_ATT, D_MODEL), lambda i: (0, 0)),
                  pl.BlockSpec((D_SSM, D_MODEL), lambda i: (0, 0)),
                  pl.BlockSpec((1, D_MODEL), lambda i: (0, 0)),
                  pl.BlockSpec((D_MODEL, 2 * LANES), lambda i: (0, 0)),
                  pl.BlockSpec((1, LANES), lambda i: (0, 0))],
        out_specs=[pl.BlockSpec((tm, D_MODEL), lambda i: (i, 0)),
                   pl.BlockSpec((tm, H2X_W), lambda i: (i, 0))],
        out_shape=[jax.ShapeDtypeStruct((n, D_MODEL), F32),
                   jax.ShapeDtypeStruct((n, H2X_W), F32)],
        compiler_params=_cparams(("parallel",)),
        name="outproj_router",
    )(att, ssm, x, mods3, wo_att, wo_ssm, g, w_router, b_router)


def _route_kernel(slab_ref, meta_ref):
    t_n = MOE_TILE
    blk = LANES
    slab = slab_ref[...]
    lane = lax.broadcasted_iota(jnp.int32, (t_n, LANES), 1)
    gcol = jnp.sum(jnp.where(lane == EXPERTS_PER_GROUP, slab, 0.0), axis=-1, keepdims=True)
    member = (lane.astype(F32) == gcol) & (lane < N_GROUPS)
    a = jnp.where(member, 1.0, 0.0).astype(BF16)
    r_i = lax.broadcasted_iota(jnp.int32, (blk, blk), 0)
    c_i = lax.broadcasted_iota(jnp.int32, (blk, blk), 1)
    lower = jnp.where(c_i < r_i, 1.0, 0.0).astype(BF16)
    upper = jnp.where(r_i < c_i, 1.0, 0.0).astype(BF16)
    offs = jnp.zeros((1, LANES), F32)
    ranks = []
    for b in range(t_n // blk):
        ab = a[b * blk:(b + 1) * blk]
        rb = jnp.dot(lower, ab, preferred_element_type=F32)
        ranks.append(rb + offs)
        offs = offs + rb[blk - 1:blk] + ab[blk - 1:blk].astype(F32)
    rank = jnp.concatenate(ranks, axis=0)
    n_chunk = jnp.floor((offs + (MOE_CHUNK - 1)) * (1.0 / MOE_CHUNK))
    start = jnp.dot(jnp.broadcast_to(n_chunk, (8, LANES)).astype(BF16), upper,
                    preferred_element_type=F32)[0:1]
    end = start + n_chunk
    dest = jnp.sum(jnp.where(member, start * MOE_CHUNK + rank, 0.0), axis=-1, keepdims=True)
    tok = lax.broadcasted_iota(jnp.int32, (t_n, LANES), 0)
    digits = jnp.where(lane == 0, (tok // blk).astype(F32),
                       jnp.where(lane == 1, (tok % blk).astype(F32), jnp.where(lane == 2, 1.0, 0.0))).astype(BF16)
    sw = 512
    pieces = []
    for sc in range(MOE_ROWS // sw):
        s_id = (lax.broadcasted_iota(jnp.int32, (t_n, sw), 1) + sc * sw).astype(F32)
        hit = jnp.where(dest == s_id, 1.0, 0.0).astype(BF16)
        r = lax.dot_general(digits, hit, (((0,), (0,)), ((), ())), preferred_element_type=F32)
        tok_of = r[0:1] * blk + r[1:2]
        pieces.append(jnp.where(r[2:3] > 0.5, tok_of, float(t_n)))
    pieces.append(jnp.full((1, META_ROWS - MOE_ROWS), float(t_n), F32))
    perm = jnp.concatenate(pieces, axis=1)
    slot = lax.broadcasted_iota(jnp.int32, (1, META_ROWS), 1).astype(F32)
    lane1 = lax.broadcasted_iota(jnp.int32, (1, LANES), 1)
    cg = jnp.zeros((1, META_ROWS), F32)
    for g in range(N_GROUPS):
        end_g = jnp.sum(jnp.where(lane1 == g, end, 0.0), axis=-1, keepdims=True)
        cg = cg + jnp.where(slot >= end_g, 1.0, 0.0)
    n_act = jnp.broadcast_to(end_g, (1, META_ROWS))
    meta_ref[...] = jnp.concatenate([perm, cg, n_act, jnp.zeros((5, META_ROWS), F32)], axis=0).astype(jnp.int32)


def _route(h2x):
    n = h2x.shape[0]
    n_tiles = n // MOE_TILE
    return pl.pallas_call(
        _route_kernel,
        grid=(n_tiles,),
        in_specs=[pl.BlockSpec((MOE_TILE, LANES), lambda i: (i, D_MODEL // LANES))],
        out_specs=pl.BlockSpec((None, 8, META_ROWS), lambda i: (i, 0, 0)),
        out_shape=jax.ShapeDtypeStruct((n_tiles, 8, META_ROWS), jnp.int32),
        compiler_params=_cparams(("parallel",)),
        name="moe_route",
    )(h2x)


def _moe_kernel(perm_ref, cg_ref, nact_ref, run_ref, h_ref, wg_hbm, wu_hbm, wd_hbm, y_ref,
                hs0, hs1, ys0, ys1, wg_buf, wu_buf, wd_buf, w_sem):
    i = pl.program_id(0)
    s = pl.program_id(1)
    n_act = nact_ref[i]
    tile_base = i * META_ROWS
    hs = (hs0, hs1)
    ys = (ys0, ys1)
    n_steps = pl.num_programs(0) * MOE_STEPS
    f = i * MOE_STEPS + s
    slot = run_ref[n_steps + f]

    def weight_copies(g, sl):
        e0 = g * EXPERTS_PER_GROUP
        r0 = g * (EXPERTS_PER_GROUP * EXPERT_FF)
        return (pltpu.make_async_copy(wg_hbm.at[pl.ds(e0, EXPERTS_PER_GROUP)], wg_buf.at[sl], w_sem.at[sl, 0]),
                pltpu.make_async_copy(wu_hbm.at[pl.ds(e0, EXPERTS_PER_GROUP)], wu_buf.at[sl], w_sem.at[sl, 1]),
                pltpu.make_async_copy(wd_hbm.at[pl.ds(r0, EXPERTS_PER_GROUP * EXPERT_FF)], wd_buf.at[sl],
                                      w_sem.at[sl, 2]))

    @pl.when(run_ref[f] == 1)
    def _():
        g = jnp.minimum(cg_ref[f], N_GROUPS - 1)

        @pl.when(run_ref[3 * n_steps + f] == 1)
        def _():
            for cp in weight_copies(g, slot):
                cp.start()

        for cp in weight_copies(g, slot):
            cp.wait()
        nxt = run_ref[2 * n_steps + f]

        @pl.when(nxt >= 0)
        def _():
            for cp in weight_copies(nxt, 1 - slot):
                cp.start()

    def gather(chunk, dst):
        base = tile_base + chunk * MOE_CHUNK
        for r in range(MOE_CHUNK):
            src = jnp.minimum(perm_ref[base + r], MOE_TILE - 1)
            dst[r:r + 1, :] = h_ref[pl.ds(src, 1), :]

    def scatter(chunk, src):
        base = tile_base + chunk * MOE_CHUNK
        for r in range(MOE_CHUNK):
            y_ref[pl.ds(perm_ref[base + r], 1), :] = src[r:r + 1, :]

    def ffn(src, dst):
        hb = src[:, 0:D_MODEL].astype(BF16)
        cw = src[:, D_MODEL:H2X_W]
        hid = []
        for e in range(EXPERTS_PER_GROUP):
            a = jnp.dot(hb, wg_buf[slot, e], preferred_element_type=F32)
            u = jnp.dot(hb, wu_buf[slot, e], preferred_element_type=F32)
            hid.append((_silu(a) * u * cw[:, e:e + 1]).astype(BF16))
        dst[...] = jnp.dot(jnp.concatenate(hid, axis=1), wd_buf[slot], preferred_element_type=F32)

    @pl.when(s == 0)
    def _():
        y_ref[MOE_TILE:MOE_TILE + 8, :] = jnp.zeros((8, D_MODEL), F32)
        ys1[...] = jnp.zeros_like(ys1)
        gather(0, hs0)

    for par in (0, 1):
        @pl.when((s < n_act) & (s % 2 == par))
        def _():
            gather(s + 1, hs[1 - par])
            ffn(hs[par], ys[par])
            scatter(jnp.maximum(s - 1, 0), ys[1 - par])

        @pl.when((s == n_act) & (s % 2 == par))
        def _():
            scatter(s - 1, ys[1 - par])


def _moe(h2x, perm, cgrp, nact, wg, wu, wd):
    n = h2x.shape[0]
    n_tiles = n // MOE_TILE
    n_steps = n_tiles * MOE_STEPS

    step = jnp.arange(n_steps)
    active = (step % MOE_STEPS) < jnp.repeat(nact, MOE_STEPS)
    prev = jnp.concatenate([jnp.full((1,), -1, jnp.int32), cgrp[:-1]])
    first = active & ((step % MOE_STEPS == 0) | (cgrp != prev))
    run_id = jnp.cumsum(first.astype(jnp.int32)) - 1
    later_first = lax.cummin(jnp.where(first, step, n_steps), reverse=True)
    nxt_step = jnp.concatenate([later_first[1:], jnp.full((1,), n_steps, jnp.int32)])
    nxt_group = jnp.where(nxt_step < n_steps, cgrp[jnp.minimum(nxt_step, n_steps - 1)], -1)
    runs = jnp.concatenate([first.astype(jnp.int32), run_id % 2, nxt_group.astype(jnp.int32),
                            (first & (run_id == 0)).astype(jnp.int32)])

    def h_idx(i, s, *_):
        done = (s >= jnp.maximum(_[2][i] - 1, 1)).astype(jnp.int32)
        return (jnp.minimum(i + done, n_tiles - 1), 0)

    hbm = pl.BlockSpec(memory_space=pl.ANY)
    return pl.pallas_call(
        _moe_kernel,
        grid_spec=pltpu.PrefetchScalarGridSpec(
            num_scalar_prefetch=4,
            grid=(n_tiles, MOE_STEPS),
            in_specs=[pl.BlockSpec((MOE_TILE, H2X_W), h_idx), hbm, hbm, hbm],
            out_specs=pl.BlockSpec((None, MOE_TILE + 8, D_MODEL), lambda i, s, *_: (i, 0, 0)),
            scratch_shapes=[pltpu.VMEM((MOE_CHUNK, H2X_W), F32), pltpu.VMEM((MOE_CHUNK, H2X_W), F32),
                            pltpu.VMEM((MOE_CHUNK, D_MODEL), F32), pltpu.VMEM((MOE_CHUNK, D_MODEL), F32),
                            pltpu.VMEM((2, EXPERTS_PER_GROUP, D_MODEL, EXPERT_FF), BF16),
                            pltpu.VMEM((2, EXPERTS_PER_GROUP, D_MODEL, EXPERT_FF), BF16),
                            pltpu.VMEM((2, EXPERTS_PER_GROUP * EXPERT_FF, D_MODEL), BF16),
                            pltpu.SemaphoreType.DMA((2, 3))]),
        out_shape=jax.ShapeDtypeStruct((n_tiles, MOE_TILE + 8, D_MODEL), F32),
        compiler_params=_cparams(("arbitrary", "arbitrary")),
        name="moe_experts",
    )(perm, cgrp, nact, runs, h2x, wg, wu, wd)


def _final_kernel(y_ref, x1_ref, mods_ref, fg_ref, o_ref):
    gate2 = mods_ref[:, 5 * D_MODEL:6 * D_MODEL]
    x2 = x1_ref[...] + gate2 * y_ref[...]
    o_ref[...] = x2 * lax.rsqrt(jnp.mean(x2 * x2, axis=-1, keepdims=True) + EPS) * fg_ref[...]


def _final(y, x1, mods3, mod_row0, mod_tokens, fg):
    n = x1.shape[0]
    tm = 512
    per_mod = mod_tokens // tm
    per_tile = MOE_TILE // tm
    return pl.pallas_call(
        _final_kernel,
        grid=(n // tm,),
        in_specs=[pl.BlockSpec((None, tm, D_MODEL), lambda j: (j // per_tile, j % per_tile, 0)),
                  pl.BlockSpec((tm, D_MODEL), lambda j: (j, 0)),
                  pl.BlockSpec((None, 1, 6 * D_MODEL), lambda j: (mod_row0 + j // per_mod, 0, 0)),
                  pl.BlockSpec((1, D_MODEL), lambda j: (0, 0))],
        out_specs=pl.BlockSpec((tm, D_MODEL), lambda j: (j, 0)),
        out_shape=jax.ShapeDtypeStruct((n, D_MODEL), F32),
        compiler_params=_cparams(("parallel",)),
        name="final_norm",
    )(y, x1, mods3, fg)


def _rope_tables(t):
    n_freq = QK_DIM // 4
    n_rows = t // GRID_W
    freqs = ROPE_BASE ** (-jnp.arange(n_freq, dtype=F32) / n_freq)
    ang_r = jnp.arange(n_rows, dtype=F32)[:, None] * freqs
    ang_c = jnp.arange(GRID_W, dtype=F32)[:, None] * freqs
    cr, sr, cc, sc = lax.optimization_barrier((jnp.cos(ang_r), jnp.sin(ang_r), jnp.cos(ang_c), jnp.sin(ang_c)))
    j = np.arange(LANES) % QK_DIM
    f_idx = j % n_freq
    by_row = (j < QK_DIM // 2)[None, None, :]
    first = ((j % (QK_DIM // 2)) < n_freq)[None, None, :]

    def table(r_small, c_small):
        return jnp.where(by_row, r_small[:, f_idx][:, None, :], c_small[:, f_idx][None, :, :])

    cos = table(cr, cc)
    sin = table(sr, sc)
    return (cos.reshape(t, LANES), jnp.where(first, -sin, 0.0).reshape(t, LANES),
            jnp.where(first, 0.0, sin).reshape(t, LANES))


def _layer(x, mods3, mod_row0, mod_tokens, rope_tabs, ctx_k, ctx_v, h0f, h0b, lw, layer):
    b, t, _ = x.shape
    n = b * t
    xf = x.reshape(n, D_MODEL)
    res = _inproj(xf, mods3, mod_row0, mod_tokens, t, lw["norm_mix_g"], lw["w_main"], lw["w_dt"], lw["alog"],
                  lw["dtb"], rope_tabs)
    q, kb, vt, z, xbc, cols, hrow = res[:7]
    if ctx_k is None:
        cache = None
        k3 = res[7].reshape(b, 1, t, ATT_HEADS, 2, LANES)[..., :QK_DIM]
        v3 = res[8].reshape(b, 1, t, ATT_HEADS, V_DIM)
    else:
        cache = (ctx_k.astype(BF16), jnp.swapaxes(ctx_v, 1, 2).astype(BF16))
        k3 = v3 = None
    lam0 = 0.8 - 0.6 * math.exp(-0.3 * layer)
    tq = 1024 if t % 1024 == 0 else 256
    tk = 512 if t % 512 == 0 else 256
    att = _attention(q.reshape(b, t, D_QK), kb.reshape(b, t, D_QK), vt, cache, lw["lamp"], lw["attn_subln_g"],
                     lam0, tq, tk, *((ATT_HEADS, 2) if t <= 512 else (1, 1)))
    ssm, hf, hb = _ssd(xbc, cols.reshape(b, t, LANES), hrow.reshape(b, t // CHUNK * HROWS, LANES),
                       z.reshape(b, t, D_SSM), h0f, h0b, lw["conv_w"], lw["conv_b"], lw["d_x"], lw["ssm_norm_g"])
    x1, h2x = _outproj(att.reshape(n, D_ATT), ssm.reshape(n, D_SSM), xf, mods3, mod_row0, mod_tokens,
                       lw["wo_att"], lw["wo_ssm"], lw["norm_ffn_g"], lw["w_router"], lw["b_router"])
    meta = _route(h2x)
    perm = meta[:, 0, :].reshape(-1)
    cgrp = meta[:, 1, :MOE_STEPS].reshape(-1)
    nact = meta[:, 2, 0]
    y = _moe(h2x, perm, cgrp, nact, lw["wg"], lw["wu"], lw["wd"])
    out = _final(y, x1, mods3, mod_row0, mod_tokens, lw["final_g"])
    return out.reshape(b, t, D_MODEL), k3, v3, hf, hb


def _pad_lanes(v, width=LANES):
    return jnp.pad(v, [(0, 0)] * (v.ndim - 1) + [(0, width - v.shape[-1])])


def kernel(x_prompt, x_sample, cache_k, cache_v, state_ssm_fwd, state_ssm_bwd, c, c_ctx, w_ada, b_ada, norm_mix_g, w_in, w_out, lambda_q1, lambda_k1, lambda_q2, lambda_k2, attn_subln_g, conv_w, conv_b, a_log_fwd, a_log_bwd, dt_bias_fwd, dt_bias_bwd, ssm_d, ssm_norm_g, norm_ffn_g, w_group_router, b_group_router, w_expert_router, b_expert_router, w_exp_gate, w_exp_up, w_exp_down, final_norm_g):
    depth = w_in.shape[0]
    assert depth == 1, "single trunk layer"
    bp, tp, _ = x_prompt.shape
    bs, ts, _ = x_sample.shape
    l = 0
    cond = jnp.concatenate([c_ctx[None], c], axis=0)
    condT = _pad_lanes(cond.T, 8)
    mods = _ada(condT, w_ada[l], b_ada[l][None])
    mods3 = mods.reshape(8, 1, 6 * D_MODEL)

    w_router = _pad_lanes(jnp.concatenate([w_group_router[l], w_expert_router[l]], axis=1))
    wr_hi = w_router.astype(BF16)
    wr_lo = (w_router - wr_hi.astype(F32)).astype(BF16)
    lw = dict(
        norm_mix_g=norm_mix_g[l][None],
        w_main=w_in[l].astype(BF16),
        w_dt=_pad_lanes(w_in[l][:, MAIN_COLS:]).astype(BF16),
        lamp=jnp.stack([lambda_q1[l], lambda_k1[l], lambda_q2[l], lambda_k2[l]]),
        attn_subln_g=attn_subln_g[l][None],
        conv_w=conv_w[l].reshape(CONV_W, N_COLBLK, LANES), conv_b=conv_b[l].reshape(N_COLBLK, LANES),
        alog=jnp.broadcast_to(jnp.concatenate([a_log_fwd[l], a_log_bwd[l]])[:, None], (2 * SSM_HEADS, CHUNK)),
        dtb=jnp.broadcast_to(jnp.concatenate([dt_bias_fwd[l], dt_bias_bwd[l]])[:, None], (2 * SSM_HEADS, CHUNK)),
        d_x=jnp.repeat(ssm_d[l], SSM_HEADDIM)[None], ssm_norm_g=ssm_norm_g[l][None],
        wo_att=w_out[l][:D_ATT].astype(BF16), wo_ssm=w_out[l][D_ATT:].astype(BF16),
        norm_ffn_g=norm_ffn_g[l][None],
        w_router=jnp.concatenate([wr_hi, wr_lo], axis=1),
        b_router=_pad_lanes(jnp.concatenate([b_group_router[l], b_expert_router[l]])[None]),
        wg=w_exp_gate[l].astype(BF16), wu=w_exp_up[l].astype(BF16), wd=w_exp_down[l].astype(BF16).reshape(N_EXPERTS * EXPERT_FF, D_MODEL),
        final_g=final_norm_g[None],
    )
    n_state = SSM_HEADS * SSM_HEADDIM
    yp, ck, cv, hf, hb = _layer(x_prompt, mods3, 0, bp * tp, None, None, None, None, None, lw, l)
    ys, _, _, _, _ = _layer(x_sample, mods3, 1, ts, _rope_tables(ts),
                            cache_k[:, l].reshape(bs, -1, D_QK), cache_v[:, l].reshape(bs, -1, D_ATT),
                            state_ssm_fwd[:, l].reshape(bs, n_state, D_STATE),
                            state_ssm_bwd[:, l].reshape(bs, n_state, D_STATE), lw, l)
    new_k, new_v = ck, cv
    new_hf = hf.reshape(bp, 1, SSM_HEADS, SSM_HEADDIM, D_STATE)
    new_hb = hb.reshape(bp, 1, SSM_HEADS, SSM_HEADDIM, D_STATE)
    return yp, ys, new_k, new_v, new_hf, new_hb
```

```python
import functools
import math

import numpy as np
import jax
import jax.numpy as jnp
from jax import lax
from jax.experimental import pallas as pl
from jax.experimental.pallas import tpu as pltpu

D_MODEL = 1024
GRID_W = 64
ATT_HEADS = 4
QK_DIM = 64
V_DIM = 128
D_QK = 512
D_ATT = 512
ROPE_BASE = 10000.0
D_SSM = 512
SSM_HEADDIM = 64
SSM_HEADS = 8
SSM_GROUPS = 2
D_STATE = 128
CONV_W = 5
CHUNK = 128
XBC_DIM = 1024
N_GROUPS = 4
EXPERTS_PER_GROUP = 4
N_EXPERTS = 16
EXPERT_FF = 256
EPS = 1e-6
MAIN_COLS = 2 * D_QK + D_ATT + D_SSM + XBC_DIM
H2X_W = D_MODEL + 128
MOE_TILE = 2048
MOE_CHUNK = 256
MOE_SLOTS = MOE_TILE // MOE_CHUNK + N_GROUPS
MOE_ROWS = MOE_SLOTS * MOE_CHUNK
MOE_STEPS = MOE_SLOTS + 1
META_ROWS = MOE_STEPS * MOE_CHUNK
OUT_PART = 512
ROW_TILE = 512
ADA_COLS = 1024
SSD_SEQS = 2
COL_CUM, COL_DT, COL_ECUM, COL_TOEND = 0, 16, 32, 48
HROWS = 32
LANES = 128
HALO = 8
VMEM_LIMIT = 56 * 1024 * 1024

LOG2E = math.log2(math.e)
SUM_ROWS = 16
F32 = jnp.float32
BF16 = jnp.bfloat16


def _cparams(sem):
    return pltpu.CompilerParams(dimension_semantics=sem, vmem_limit_bytes=VMEM_LIMIT)


def _sigmoid(x):
    return 1.0 / (1.0 + jnp.exp(-x))


def _silu(x):
    return x * _sigmoid(x)


def _ada_kernel(condT_ref, w_ref, b_ref, o_ref):
    s = _silu(condT_ref[...])
    w = w_ref[...]
    b = b_ref[...]
    o_ref[...] = jnp.zeros_like(o_ref)
    for r in range(3):
        o_ref[r:r + 1, :] = jnp.sum(w * s[:, r:r + 1], axis=0, keepdims=True) + b


def _ada(condT, w_ada, b_ada):
    bn = ADA_COLS
    n = w_ada.shape[1]
    return pl.pallas_call(
        _ada_kernel,
        grid=(n // bn,),
        in_specs=[pl.BlockSpec((D_MODEL, 8), lambda j: (0, 0)),
                  pl.BlockSpec((D_MODEL, bn), lambda j: (0, j)),
                  pl.BlockSpec((1, bn), lambda j: (0, j))],
        out_specs=pl.BlockSpec((8, bn), lambda j: (0, j)),
        out_shape=jax.ShapeDtypeStruct((8, n), F32),
        compiler_params=_cparams(("arbitrary",)),
        name="ada",
    )(condT, w_ada, b_ada)


def _split3(x):
    hi = x.astype(BF16)
    r1 = x - hi.astype(F32)
    mid = r1.astype(BF16)
    lo = (r1 - mid.astype(F32)).astype(BF16)
    return jnp.concatenate([hi, mid, lo], axis=-1)


def _scan_matrices():
    t = np.arange(CHUNK)
    pre = (t[:, None] <= t[None, :]).astype(np.float32)
    suf = (t[:, None] >= t[None, :]).astype(np.float32)
    return jnp.asarray(np.stack([np.concatenate([pre] * 3, axis=0), np.concatenate([suf] * 3, axis=0)]), BF16)


def _head_scalars(dt_raw, alog_ref, dtb_ref, scan_ref):
    nh2 = 2 * SSM_HEADS
    xv = dt_raw.T[0:nh2, :] + dtb_ref[...]
    dt = jnp.maximum(xv, 0.0) + jnp.log(1.0 + jnp.exp(-jnp.abs(xv)))
    la3 = _split3(dt * (-jnp.exp(alog_ref[...])))
    fwd = lax.broadcasted_iota(jnp.int32, (nh2, CHUNK), 0) < SSM_HEADS
    cum = jnp.where(fwd, jnp.dot(la3, scan_ref[0], preferred_element_type=F32),
                    jnp.dot(la3, scan_ref[1], preferred_element_type=F32))
    cum_end = jnp.where(fwd, cum[:, CHUNK - 1:CHUNK], cum[:, 0:1])
    packed = jnp.concatenate([cum, dt, jnp.exp(cum), jnp.exp(cum_end - cum),
                              jnp.zeros((LANES - 4 * nh2, CHUNK), F32)], axis=0)
    return packed.T, jnp.concatenate([cum, jnp.exp(cum_end)], axis=0)


def _inproj_kernel(rope, x_ref, mods_ref, g_ref, w_ref, wdt_ref, alog_ref, dtb_ref, scan_ref, *rest):
    if rope:
        cos_ref, sa_ref, sb_ref, q_ref, kb_ref, vt_ref, z_ref, xbc_ref, cols_ref, hrow_ref = rest
    else:
        q_ref, kb_ref, vt_ref, z_ref, xbc_ref, cols_ref, hrow_ref, kf_ref, vf_ref = rest
    x = x_ref[...]
    shift = mods_ref[:, 0:D_MODEL]
    scale = mods_ref[:, D_MODEL:2 * D_MODEL]
    y = x * lax.rsqrt(jnp.mean(x * x, axis=-1, keepdims=True) + EPS) * g_ref[...]
    h = (y * (1.0 + scale) + shift).astype(BF16)
    dt_raw = jnp.dot(h, wdt_ref[...], preferred_element_type=F32)
    for ci in range(x.shape[0] // CHUNK):
        cols, hrow = _head_scalars(dt_raw[ci * CHUNK:(ci + 1) * CHUNK, :], alog_ref, dtb_ref, scan_ref)
        cols_ref[ci * CHUNK:(ci + 1) * CHUNK, :] = cols
        hrow_ref[ci * HROWS:(ci + 1) * HROWS, :] = hrow
    r = jnp.dot(h, w_ref[...], preferred_element_type=F32)
    q = r[:, 0:D_QK]
    k = r[:, D_QK:2 * D_QK]
    if rope:
        cos = cos_ref[...]
        sa = sa_ref[...]
        sb = sb_ref[...]

        def rot(t):
            parts = []
            for hh in range(ATT_HEADS):
                th = t[:, hh * LANES:(hh + 1) * LANES]
                parts.append(th * cos + pltpu.roll(th, LANES - 16, 1) * sa + pltpu.roll(th, 16, 1) * sb)
            return jnp.concatenate(parts, axis=1)

        q = rot(q)
        k = rot(k)
    v = r[:, 2 * D_QK:2 * D_QK + D_ATT]
    q_ref[...] = q
    kb_ref[...] = k.astype(BF16)
    n_seq, _, t_seq = vt_ref.shape
    v_t = v.T.astype(BF16)
    for sq in range(n_seq):
        vt_ref[sq] = v_t[:, sq * t_seq:(sq + 1) * t_seq]
    if not rope:
        tm = k.shape[0]
        for hh in range(ATT_HEADS):
            k_h = k[:, hh * LANES:(hh + 1) * LANES]
            vf_ref[pl.ds(hh, tm, stride=ATT_HEADS), :] = v[:, hh * LANES:(hh + 1) * LANES]
            kf_ref[pl.ds(2 * hh, tm, stride=2 * ATT_HEADS), :] = k_h
            kf_ref[pl.ds(2 * hh + 1, tm, stride=2 * ATT_HEADS), :] = pltpu.roll(k_h, QK_DIM, 1)
    z_ref[...] = r[:, 2 * D_QK + D_ATT:2 * D_QK + D_ATT + D_SSM]
    x0 = 2 * D_QK + D_ATT + D_SSM
    for cb in range(XBC_DIM // LANES):
        for sq in range(n_seq):
            xbc_ref[sq, cb] = r[sq * t_seq:(sq + 1) * t_seq, x0 + cb * LANES:x0 + (cb + 1) * LANES]


def _inproj(x, mods3, mod_row0, mod_tokens, seq_len, g, w_main, w_dt, alog, dtb, rope_tabs):
    n = x.shape[0]
    tm = ROW_TILE
    assert seq_len % tm == 0 or tm % seq_len == 0
    per_seq = max(seq_len // tm, 1)
    n_seq = max(tm // seq_len, 1)
    t_seq = tm // n_seq
    per_mod = mod_tokens // tm
    rope = rope_tabs is not None
    in_specs = [pl.BlockSpec((tm, D_MODEL), lambda i: (i, 0)),
                pl.BlockSpec((None, 1, 6 * D_MODEL), lambda i: (mod_row0 + i // per_mod, 0, 0)),
                pl.BlockSpec((1, D_MODEL), lambda i: (0, 0)),
                pl.BlockSpec((D_MODEL, MAIN_COLS), lambda i: (0, 0)),
                pl.BlockSpec((D_MODEL, LANES), lambda i: (0, 0)),
                pl.BlockSpec((2 * SSM_HEADS, CHUNK), lambda i: (0, 0)),
                pl.BlockSpec((2 * SSM_HEADS, CHUNK), lambda i: (0, 0)),
                pl.BlockSpec((2, 3 * CHUNK, CHUNK), lambda i: (0, 0, 0))]
    args = [x, mods3, g, w_main, w_dt, alog, dtb, _scan_matrices()]
    if rope:
        tab_spec = pl.BlockSpec((tm, LANES), lambda i: (i % per_seq, 0))
        in_specs += [tab_spec] * 3
        args += list(rope_tabs)
    def rows(wd, dtype=F32):
        return pl.BlockSpec((tm, wd), lambda i: (i, 0)), jax.ShapeDtypeStruct((n, wd), dtype)

    hr = tm // CHUNK * HROWS
    outs = [rows(D_QK), rows(D_QK, BF16),
            (pl.BlockSpec((n_seq, D_ATT, t_seq), lambda i: (i // per_seq, 0, i % per_seq)),
             jax.ShapeDtypeStruct((n // seq_len, D_ATT, seq_len), BF16)),
            rows(D_SSM),
            (pl.BlockSpec((n_seq, XBC_DIM // LANES, t_seq, LANES), lambda i: (i // per_seq, 0, i % per_seq, 0)),
             jax.ShapeDtypeStruct((n // seq_len, XBC_DIM // LANES, seq_len, LANES), F32)),
            rows(LANES),
            (pl.BlockSpec((hr, LANES), lambda i: (i, 0)), jax.ShapeDtypeStruct((n // CHUNK * HROWS, LANES), F32))]
    if not rope:
        outs += [(pl.BlockSpec((tm * 2 * ATT_HEADS, LANES), lambda i: (i, 0)),
                  jax.ShapeDtypeStruct((n * 2 * ATT_HEADS, LANES), F32)),
                 (pl.BlockSpec((tm * ATT_HEADS, LANES), lambda i: (i, 0)),
                  jax.ShapeDtypeStruct((n * ATT_HEADS, LANES), F32))]
    return pl.pallas_call(
        functools.partial(_inproj_kernel, rope),
        grid=(n // tm,),
        in_specs=in_specs,
        out_specs=[o[0] for o in outs],
        out_shape=[o[1] for o in outs],
        compiler_params=_cparams(("parallel",)),
        name="inproj_rope" if rope else "inproj",
    )(*args)


def _attn_kernel(tk, lam0, has_cache, q_ref, k_ref, vt_ref, *rest):
    if has_cache:
        ck_ref, cvt_ref, lamp_ref, g_ref, o_ref = rest
    else:
        lamp_ref, g_ref, o_ref = rest
    tq = q_ref.shape[1]
    lp = lamp_ref[...]
    lam = (jnp.exp(jnp.sum(lp[0:1] * lp[1:2], axis=-1, keepdims=True))
           - jnp.exp(jnp.sum(lp[2:3] * lp[3:4], axis=-1, keepdims=True)) + lam0)
    for bb, hh in [(b_, h_) for b_ in range(q_ref.shape[0]) for h_ in range(q_ref.shape[2] // LANES)]:
        hs = slice(hh * LANES, (hh + 1) * LANES)
        chunks = [(k_ref, vt_ref, c * tk, tk) for c in range(k_ref.shape[1] // tk)]
        if has_cache:
            ck = min(tk, ck_ref.shape[1])
            chunks += [(ck_ref, cvt_ref, c * ck, ck) for c in range(ck_ref.shape[1] // ck)]
        q = q_ref[bb, :, hs] * (QK_DIM ** -0.5 * LOG2E)
        lane = lax.broadcasted_iota(jnp.int32, q.shape, 1)
        qq_t = jnp.concatenate([jnp.where(lane < QK_DIM, q, 0.0), jnp.where(lane >= QK_DIM, q, 0.0)],
                               axis=0).T.astype(BF16)

        def scores(chunk):
            kr, _, start, size = chunk
            return jnp.dot(kr[bb, start:start + size, hs], qq_t, preferred_element_type=F32)

        def update(s, chunk, m, acc):
            _, vr, start, size = chunk
            m_new = jnp.maximum(m, jnp.max(s, axis=0, keepdims=True))
            alpha = jnp.exp2(m - m_new)
            p = jnp.exp2(s - m_new).astype(BF16)
            v_ext = jnp.concatenate([vr[bb, hs, start:start + size], jnp.ones((SUM_ROWS, size), BF16)], axis=0)
            acc = alpha * acc + jnp.dot(v_ext, p, preferred_element_type=F32)
            return m_new, acc

        m = jnp.full((1, 2 * tq), -jnp.inf, F32)
        acc = jnp.zeros((V_DIM + SUM_ROWS, 2 * tq), F32)
        s = scores(chunks[0])
        for c, chunk in enumerate(chunks):
            s_next = scores(chunks[c + 1]) if c + 1 < len(chunks) else None
            m, acc = update(s, chunk, m, acc)
            s = s_next
        o = acc[0:V_DIM] / acc[V_DIM:V_DIM + 1]
        o = (o[:, 0:tq] - lam * o[:, tq:2 * tq]).T
        o = o * lax.rsqrt(jnp.mean(o * o, axis=-1, keepdims=True) + EPS)
        o_ref[bb, :, hs] = o * g_ref[...] * (1.0 - lam0)


def _attn_tiling(b, t):
    if t % 1024 == 0:
        return 1024, 512, 1, 1
    assert t % 256 == 0 and b % 2 == 0
    return 256, 256, ATT_HEADS, 2


def _attention(q, k, vt, cache, lamp, g, lam0, tq, tk, nh, nb):
    b, t, _ = q.shape
    wd = nh * LANES

    def kv_specs(length):
        return [pl.BlockSpec((nb, length, wd), lambda bi, h, i: (bi, 0, h)),
                pl.BlockSpec((nb, wd, length), lambda bi, h, i: (bi, h, 0))]

    in_specs = [pl.BlockSpec((nb, tq, wd), lambda bi, h, i: (bi, i, h))] + kv_specs(t)
    args = [q, k, vt]
    if cache is not None:
        assert cache[0].shape[1] % min(tk, cache[0].shape[1]) == 0
        in_specs += kv_specs(cache[0].shape[1])
        args += list(cache)
    in_specs += [pl.BlockSpec((4, QK_DIM), lambda bi, h, i: (0, 0)),
                 pl.BlockSpec((1, V_DIM), lambda bi, h, i: (0, 0))]
    return pl.pallas_call(
        functools.partial(_attn_kernel, tk, lam0, cache is not None),
        grid=(b // nb, ATT_HEADS // nh, t // tq),
        in_specs=in_specs,
        out_specs=pl.BlockSpec((nb, tq, wd), lambda bi, h, i: (bi, i, h)),
        out_shape=jax.ShapeDtypeStruct((b, t, D_ATT), F32),
        compiler_params=_cparams(("parallel", "parallel", "arbitrary")),
        name="diff_attn",
    )(*args, lamp, g)


N_COLBLK = XBC_DIM // LANES
def _expand_matrices():
    out = []
    for d in range(2):
        e = np.zeros((LANES, 3 * D_SSM), np.float32)
        for blk, lane0 in enumerate((COL_DT, COL_ECUM, COL_TOEND)):
            for h in range(SSM_HEADS):
                e[lane0 + d * SSM_HEADS + h,
                  blk * D_SSM + h * SSM_HEADDIM:blk * D_SSM + (h + 1) * SSM_HEADDIM] = 1.0
        out.append(np.concatenate([e, e, e], axis=0))
    return jnp.asarray(np.stack(out), BF16)


def _ssd_chunk(reverse, xs, bm, cm, state, hrow, cols, e_ref):
    d0 = SSM_HEADS if reverse else 0
    cum = hrow[d0:d0 + SSM_HEADS, :]
    dec = hrow[2 * SSM_HEADS + d0:2 * SSM_HEADS + d0 + SSM_HEADS, 0:1]
    row = lax.broadcasted_iota(jnp.int32, (CHUNK, LANES), 0)
    lane = lax.broadcasted_iota(jnp.int32, (CHUNK, LANES), 1)
    causal = (row <= lane) if reverse else (row >= lane)
    lane_g = lax.broadcasted_iota(jnp.int32, (CHUNK, 2 * LANES), 1)
    spread = jnp.dot(_split3(cols), e_ref[1 if reverse else 0], preferred_element_type=F32)
    xd = xs * spread[:, 0:D_SSM]
    xdw = (xd * spread[:, 2 * D_SSM:3 * D_SSM]).astype(BF16)
    xd = xd.astype(BF16)
    ecum_x = spread[:, D_SSM:2 * D_SSM]
    rep = SSM_HEADS // SSM_GROUPS
    y_parts = []
    new_state = []
    for g in range(SSM_GROUPS):
        bg = bm[:, g * D_STATE:(g + 1) * D_STATE]
        cg = cm[:, g * D_STATE:(g + 1) * D_STATE]
        cbt = lax.dot_general(cg, bg, (((1,), (1,)), ((), ())), preferred_element_type=F32)
        rows = slice(g * rep * SSM_HEADDIM, (g + 1) * rep * SSM_HEADDIM)
        st_g = state[rows, :]
        y_off = lax.dot_general(cg, st_g.astype(BF16), (((1,), (1,)), ((), ())),
                                preferred_element_type=F32)
        cst = lax.dot_general(xdw[:, rows], bg, (((0,), (0,)), ((), ())), preferred_element_type=F32)
        xd_g = xd[:, rows]
        scs = []
        blocks = []
        for hh in range(rep):
            h = g * rep + hh
            seg = cols[:, COL_CUM + d0 + h:COL_CUM + d0 + h + 1] - cum[h:h + 1, :]
            decay = jnp.exp(jnp.where(causal, seg, -jnp.inf))
            scs.append((cbt * decay).astype(BF16))
            blocks.append(jnp.where(lane_g // SSM_HEADDIM == hh, xd_g, jnp.zeros_like(xd_g)))
        y_diag = jnp.dot(jnp.concatenate(scs, axis=1), jnp.concatenate(blocks, axis=0),
                         preferred_element_type=F32)
        dec_rows = jnp.concatenate(
            [jnp.broadcast_to(dec[g * rep + hh:g * rep + hh + 1, :], (SSM_HEADDIM, D_STATE)) for hh in range(rep)],
            axis=0)
        new_state.append(st_g * dec_rows + cst)
        y_parts.append(y_diag + y_off * ecum_x[:, rows])
    return jnp.concatenate(y_parts, axis=1), jnp.concatenate(new_state, axis=0)


def _ssd_kernel(nc, has_h0, xc_ref, xp_ref, xn_ref, cols_ref, hrow_ref, z_ref, *rest):
    if has_h0:
        h0f_ref, h0b_ref = rest[:2]
        rest = rest[2:]
    cw_ref, cb_ref, d_ref, ng_ref, e_ref, y_ref, hf_ref, hb_ref, state_ref, slab_ref, xs_ref, bc_ref, yf_ref = rest
    ps = pl.program_id(1)
    c = pl.program_id(2)
    n_bc = SSM_GROUPS * D_STATE

    @pl.when((ps == 0) & (c == 0))
    def _():
        state_ref[...] = h0f_ref[...] if has_h0 else jnp.zeros_like(state_ref)

    @pl.when(ps == 0)
    def _():
        t0 = pl.multiple_of(c * CHUNK, CHUNK)
        for bb in range(state_ref.shape[0]):
            slab_ref[bb, :, 0:HALO, :] = jnp.where(c > 0, xp_ref[bb], 0.0)
            slab_ref[bb, :, HALO:HALO + CHUNK, :] = xc_ref[bb]
            slab_ref[bb, :, HALO + CHUNK:2 * HALO + CHUNK, :] = jnp.where(c < nc - 1, xn_ref[bb], 0.0)
            blocks = []
            for cb in range(N_COLBLK):
                conv = cb_ref[cb:cb + 1, :]
                for kk in range(CONV_W):
                    off = HALO - CONV_W // 2 + kk
                    conv = conv + slab_ref[bb, cb, off:off + CHUNK, :] * cw_ref[kk, cb:cb + 1, :]
                blocks.append(_silu(conv))
            xs = jnp.concatenate(blocks[0:D_SSM // LANES], axis=1)
            bc = jnp.concatenate(blocks[D_SSM // LANES:], axis=1).astype(BF16)
            y, state = _ssd_chunk(False, xs, bc[:, 0:n_bc], bc[:, n_bc:], state_ref[bb], hrow_ref[bb],
                                  cols_ref[bb], e_ref)
            xs_ref[bb, pl.ds(t0, CHUNK), :] = xs
            bc_ref[bb, pl.ds(t0, CHUNK), :] = bc
            yf_ref[bb, pl.ds(t0, CHUNK), :] = y
            state_ref[bb] = state

    @pl.when((ps == 0) & (c == nc - 1))
    def _():
        hf_ref[...] = state_ref[...]
        state_ref[...] = h0b_ref[...] if has_h0 else jnp.zeros_like(state_ref)

    @pl.when(ps == 1)
    def _():
        t0 = pl.multiple_of((nc - 1 - c) * CHUNK, CHUNK)
        for bb in range(state_ref.shape[0]):
            xs = xs_ref[bb, pl.ds(t0, CHUNK), :]
            bc = bc_ref[bb, pl.ds(t0, CHUNK), :]
            y, state = _ssd_chunk(True, xs, bc[:, 0:n_bc], bc[:, n_bc:], state_ref[bb], hrow_ref[bb],
                                  cols_ref[bb], e_ref)
            state_ref[bb] = state
            y = y + yf_ref[bb, pl.ds(t0, CHUNK), :] + xs * d_ref[...]
            y = y * _silu(z_ref[bb])
            y = y * lax.rsqrt(jnp.mean(y * y, axis=-1, keepdims=True) + EPS)
            y_ref[bb] = y * ng_ref[...]

    @pl.when((ps == 1) & (c == nc - 1))
    def _():
        hb_ref[...] = state_ref[...]


def _ssd(xbc, cols, hrow, z, h0f, h0b, conv_w, conv_b, d_x, norm_g):
    b, _, l, _ = xbc.shape
    has_h0 = h0f is not None
    nb = SSD_SEQS
    assert b % nb == 0
    nc = l // CHUNK
    hb = CHUNK // HALO
    n_state = SSM_HEADS * SSM_HEADDIM
    last = nc - 1

    def fwd_chunk(ps, ci):
        return jnp.where(ps == 0, ci, last)

    def any_chunk(ps, ci):
        return jnp.where(ps == 0, ci, last - ci)

    def bwd_chunk(ps, ci):
        return jnp.where(ps == 0, last, last - ci)

    const2 = lambda bi, ps, ci: (0, 0)
    state_spec = pl.BlockSpec((nb, n_state, D_STATE), lambda bi, ps, ci: (bi, 0, 0))
    in_specs = [pl.BlockSpec((nb, N_COLBLK, CHUNK, LANES), lambda bi, ps, ci: (bi, 0, fwd_chunk(ps, ci), 0)),
                pl.BlockSpec((nb, N_COLBLK, HALO, LANES),
                             lambda bi, ps, ci: (bi, 0, jnp.maximum(fwd_chunk(ps, ci) * hb - 1, 0), 0)),
                pl.BlockSpec((nb, N_COLBLK, HALO, LANES),
                             lambda bi, ps, ci: (bi, 0, jnp.minimum((fwd_chunk(ps, ci) + 1) * hb, l // HALO - 1), 0)),
                pl.BlockSpec((nb, CHUNK, LANES), lambda bi, ps, ci: (bi, any_chunk(ps, ci), 0)),
                pl.BlockSpec((nb, HROWS, LANES), lambda bi, ps, ci: (bi, any_chunk(ps, ci), 0)),
                pl.BlockSpec((nb, CHUNK, D_SSM), lambda bi, ps, ci: (bi, bwd_chunk(ps, ci), 0)),
                *([state_spec, state_spec] if has_h0 else []),
                pl.BlockSpec((CONV_W, N_COLBLK, LANES), lambda bi, ps, ci: (0, 0, 0)),
                pl.BlockSpec((N_COLBLK, LANES), const2),
                pl.BlockSpec((1, D_SSM), const2),
                pl.BlockSpec((1, D_SSM), const2),
                pl.BlockSpec((2, 3 * LANES, 3 * D_SSM), lambda bi, ps, ci: (0, 0, 0))]
    return pl.pallas_call(
        functools.partial(_ssd_kernel, nc, has_h0),
        grid=(b // nb, 2, nc),
        in_specs=in_specs,
        out_specs=[pl.BlockSpec((nb, CHUNK, D_SSM), lambda bi, ps, ci: (bi, bwd_chunk(ps, ci), 0)),
                   state_spec, state_spec],
        out_shape=[jax.ShapeDtypeStruct((b, l, D_SSM), F32),
                   jax.ShapeDtypeStruct((b, n_state, D_STATE), F32),
                   jax.ShapeDtypeStruct((b, n_state, D_STATE), F32)],
        scratch_shapes=[pltpu.VMEM((nb, n_state, D_STATE), F32),
                        pltpu.VMEM((nb, N_COLBLK, CHUNK + 2 * HALO, LANES), F32),
                        pltpu.VMEM((nb, l, D_SSM), F32),
                        pltpu.VMEM((nb, l, 2 * SSM_GROUPS * D_STATE), BF16),
                        pltpu.VMEM((nb, l, D_SSM), F32)],
        compiler_params=_cparams(("parallel", "arbitrary", "arbitrary")),
        name="ssd",
    )(xbc, xbc, xbc, cols, hrow, z, *([h0f, h0b] if has_h0 else []), conv_w, conv_b, d_x, norm_g,
      _expand_matrices())


def _outproj_kernel(att_ref, ssm_ref, x_ref, mods_ref, wa_ref, ws_ref, g_ref, wr_ref, br_ref,
                    x1_ref, h2x_ref):
    for part in range(x_ref.shape[0] // OUT_PART):
        rows = slice(part * OUT_PART, (part + 1) * OUT_PART)
        _outproj_rows(rows, att_ref, ssm_ref, x_ref, mods_ref, wa_ref, ws_ref, g_ref, wr_ref, br_ref,
                      x1_ref, h2x_ref)


def _outproj_rows(rows, att_ref, ssm_ref, x_ref, mods_ref, wa_ref, ws_ref, g_ref, wr_ref, br_ref,
                  x1_ref, h2x_ref):
    mix = (jnp.dot(att_ref[rows, :].astype(BF16), wa_ref[...], preferred_element_type=F32)
           + jnp.dot(ssm_ref[rows, :].astype(BF16), ws_ref[...], preferred_element_type=F32))
    gate1 = mods_ref[:, 2 * D_MODEL:3 * D_MODEL]
    shift2 = mods_ref[:, 3 * D_MODEL:4 * D_MODEL]
    scale2 = mods_ref[:, 4 * D_MODEL:5 * D_MODEL]
    x1 = x_ref[rows, :] + gate1 * mix
    x1_ref[rows, :] = x1
    y = x1 * lax.rsqrt(jnp.mean(x1 * x1, axis=-1, keepdims=True) + EPS) * g_ref[...]
    h2 = y * (1.0 + scale2) + shift2
    h_hi = h2.astype(BF16)
    h2x_ref[rows, 0:D_MODEL] = h2
    h_lo = (h2 - h_hi.astype(F32)).astype(BF16)
    both = jnp.dot(h_hi, wr_ref[...], preferred_element_type=F32)
    logits = (both[:, 0:LANES] + both[:, LANES:2 * LANES]
              + jnp.dot(h_lo, wr_ref[:, 0:LANES], preferred_element_type=F32)) + br_ref[...]
    lane = lax.broadcasted_iota(jnp.int32, logits.shape, 1).astype(F32)
    neg = -jnp.inf
    big = float(1 << 20)
    is_g = lane < N_GROUPS
    gl = jnp.where(is_g, logits, neg)
    gmax = jnp.max(gl, axis=-1, keepdims=True)
    g_idx = jnp.min(jnp.where(gl == gmax, lane, big), axis=-1, keepdims=True)
    p_g = 1.0 / jnp.sum(jnp.where(is_g, jnp.exp(gl - gmax), 0.0), axis=-1, keepdims=True)
    e_lo = N_GROUPS + g_idx * EXPERTS_PER_GROUP
    in_grp = (lane >= e_lo) & (lane < e_lo + EXPERTS_PER_GROUP)
    el = jnp.where(in_grp, logits, neg)
    m1 = jnp.max(el, axis=-1, keepdims=True)
    i1 = jnp.min(jnp.where(el == m1, lane, big), axis=-1, keepdims=True)
    el2 = jnp.where(lane == i1, neg, el)
    m2 = jnp.max(el2, axis=-1, keepdims=True)
    i2 = jnp.min(jnp.where(el2 == m2, lane, big), axis=-1, keepdims=True)
    e2 = jnp.exp(m2 - m1)
    w1 = p_g / (1.0 + e2)
    w2 = p_g * e2 / (1.0 + e2)
    slab = (jnp.where(lane == i1 - e_lo, w1, 0.0) + jnp.where(lane == i2 - e_lo, w2, 0.0)
            + jnp.where(lane == EXPERTS_PER_GROUP, g_idx, 0.0))
    h2x_ref[rows, D_MODEL:H2X_W] = slab


def _outproj(att, ssm, x, mods3, mod_row0, mod_tokens, wo_att, wo_ssm, g, w_router, b_router):
    n = x.shape[0]
    tm = 2 * OUT_PART
    per_mod = mod_tokens // tm
    return pl.pallas_call(
        _outproj_kernel,
        grid=(n // tm,),
        in_specs=[pl.BlockSpec((tm, D_ATT), lambda i: (i, 0)),
                  pl.BlockSpec((tm, D_SSM), lambda i: (i, 0)),
                  pl.BlockSpec((tm, D_MODEL), lambda i: (i, 0)),
                  pl.BlockSpec((None, 1, 6 * D_MODEL), lambda i: (mod_row0 + i // per_mod, 0, 0)),
                  pl.BlockSpec((D_ATT, D_MODEL), lambda i: (0, 0)),
                  pl.BlockSpec((D_SSM, D_MODEL), lambda i: (0, 0)),
                  pl.BlockSpec((1, D_MODEL), lambda i: (0, 0)),
                  pl.BlockSpec((D_MODEL, 2 * LANES), lambda i: (0, 0)),
                  pl.BlockSpec((1, LANES), lambda i: (0, 0))],
        out_specs=[pl.BlockSpec((tm, D_MODEL), lambda i: (i, 0)),
                   pl.BlockSpec((tm, H2X_W), lambda i: (i, 0))],
        out_shape=[jax.ShapeDtypeStruct((n, D_MODEL), F32),
                   jax.ShapeDtypeStruct((n, H2X_W), F32)],
        compiler_params=_cparams(("parallel",)),
        name="outproj_router",
    )(att, ssm, x, mods3, wo_att, wo_ssm, g, w_router, b_router)


def _route_kernel(slab_ref, meta_ref):
    t_n = MOE_TILE
    blk = LANES
    slab = slab_ref[...]
    lane = lax.broadcasted_iota(jnp.int32, (t_n, LANES), 1)
    gcol = jnp.sum(jnp.where(lane == EXPERTS_PER_GROUP, slab, 0.0), axis=-1, keepdims=True)
    member = (lane.astype(F32) == gcol) & (lane < N_GROUPS)
    a = jnp.where(member, 1.0, 0.0).astype(BF16)
    r_i = lax.broadcasted_iota(jnp.int32, (blk, blk), 0)
    c_i = lax.broadcasted_iota(jnp.int32, (blk, blk), 1)
    lower = jnp.where(c_i < r_i, 1.0, 0.0).astype(BF16)
    upper = jnp.where(r_i < c_i, 1.0, 0.0).astype(BF16)
    offs = jnp.zeros((1, LANES), F32)
    ranks = []
    for b in range(t_n // blk):
        ab = a[b * blk:(b + 1) * blk]
        rb = jnp.dot(lower, ab, preferred_element_type=F32)
        ranks.append(rb + offs)
        offs = offs + rb[blk - 1:blk] + ab[blk - 1:blk].astype(F32)
    rank = jnp.concatenate(ranks, axis=0)
    n_chunk = jnp.floor((offs + (MOE_CHUNK - 1)) * (1.0 / MOE_CHUNK))
    start = jnp.dot(jnp.broadcast_to(n_chunk, (8, LANES)).astype(BF16), upper,
                    preferred_element_type=F32)[0:1]
    end = start + n_chunk
    dest = jnp.sum(jnp.where(member, start * MOE_CHUNK + rank, 0.0), axis=-1, keepdims=True)
    tok = lax.broadcasted_iota(jnp.int32, (t_n, LANES), 0)
    digits = jnp.where(lane == 0, (tok // blk).astype(F32),
                       jnp.where(lane == 1, (tok % blk).astype(F32), jnp.where(lane == 2, 1.0, 0.0))).astype(BF16)
    sw = 512
    pieces = []
    for sc in range(MOE_ROWS // sw):
        s_id = (lax.broadcasted_iota(jnp.int32, (t_n, sw), 1) + sc * sw).astype(F32)
        hit = jnp.where(dest == s_id, 1.0, 0.0).astype(BF16)
        r = lax.dot_general(digits, hit, (((0,), (0,)), ((), ())), preferred_element_type=F32)
        tok_of = r[0:1] * blk + r[1:2]
        pieces.append(jnp.where(r[2:3] > 0.5, tok_of, float(t_n)))
    pieces.append(jnp.full((1, META_ROWS - MOE_ROWS), float(t_n), F32))
    perm = jnp.concatenate(pieces, axis=1)
    slot = lax.broadcasted_iota(jnp.int32, (1, META_ROWS), 1).astype(F32)
    lane1 = lax.broadcasted_iota(jnp.int32, (1, LANES), 1)
    cg = jnp.zeros((1, META_ROWS), F32)
    for g in range(N_GROUPS):
        end_g = jnp.sum(jnp.where(lane1 == g, end, 0.0), axis=-1, keepdims=True)
        cg = cg + jnp.where(slot >= end_g, 1.0, 0.0)
    n_act = jnp.broadcast_to(end_g, (1, META_ROWS))
    meta_ref[...] = jnp.concatenate([perm, cg, n_act, jnp.zeros((5, META_ROWS), F32)], axis=0).astype(jnp.int32)


def _route(h2x):
    n = h2x.shape[0]
    n_tiles = n // MOE_TILE
    return pl.pallas_call(
        _route_kernel,
        grid=(n_tiles,),
        in_specs=[pl.BlockSpec((MOE_TILE, LANES), lambda i: (i, D_MODEL // LANES))],
        out_specs=pl.BlockSpec((None, 8, META_ROWS), lambda i: (i, 0, 0)),
        out_shape=jax.ShapeDtypeStruct((n_tiles, 8, META_ROWS), jnp.int32),
        compiler_params=_cparams(("parallel",)),
        name="moe_route",
    )(h2x)


def _moe_kernel(perm_ref, cg_ref, nact_ref, run_ref, h_ref, wg_hbm, wu_hbm, wd_hbm, y_ref,
                hs0, hs1, ys0, ys1, wg_buf, wu_buf, wd_buf, w_sem):
    i = pl.program_id(0)
    s = pl.program_id(1)
    n_act = nact_ref[i]
    tile_base = i * META_ROWS
    hs = (hs0, hs1)
    ys = (ys0, ys1)
    n_steps = pl.num_programs(0) * MOE_STEPS
    f = i * MOE_STEPS + s
    slot = run_ref[n_steps + f]

    def weight_copies(g, sl):
        e0 = g * EXPERTS_PER_GROUP
        r0 = g * (EXPERTS_PER_GROUP * EXPERT_FF)
        return (pltpu.make_async_copy(wg_hbm.at[pl.ds(e0, EXPERTS_PER_GROUP)], wg_buf.at[sl], w_sem.at[sl, 0]),
                pltpu.make_async_copy(wu_hbm.at[pl.ds(e0, EXPERTS_PER_GROUP)], wu_buf.at[sl], w_sem.at[sl, 1]),
                pltpu.make_async_copy(wd_hbm.at[pl.ds(r0, EXPERTS_PER_GROUP * EXPERT_FF)], wd_buf.at[sl],
                                      w_sem.at[sl, 2]))

    @pl.when(run_ref[f] == 1)
    def _():
        g = jnp.minimum(cg_ref[f], N_GROUPS - 1)

        @pl.when(run_ref[3 * n_steps + f] == 1)
        def _():
            for cp in weight_copies(g, slot):
                cp.start()

        for cp in weight_copies(g, slot):
            cp.wait()
        nxt = run_ref[2 * n_steps + f]

        @pl.when(nxt >= 0)
        def _():
            for cp in weight_copies(nxt, 1 - slot):
                cp.start()

    def gather(chunk, dst):
        base = tile_base + chunk * MOE_CHUNK
        for r in range(MOE_CHUNK):
            src = jnp.minimum(perm_ref[base + r], MOE_TILE - 1)
            dst[r:r + 1, :] = h_ref[pl.ds(src, 1), :]

    def scatter(chunk, src):
        base = tile_base + chunk * MOE_CHUNK
        for r in range(MOE_CHUNK):
            y_ref[pl.ds(perm_ref[base + r], 1), :] = src[r:r + 1, :]

    def ffn(src, dst):
        hb = src[:, 0:D_MODEL].astype(BF16)
        cw = src[:, D_MODEL:H2X_W]
        hid = []
        for e in range(EXPERTS_PER_GROUP):
            a = jnp.dot(hb, wg_buf[slot, e], preferred_element_type=F32)
            u = jnp.dot(hb, wu_buf[slot, e], preferred_element_type=F32)
            hid.append((_silu(a) * u * cw[:, e:e + 1]).astype(BF16))
        dst[...] = jnp.dot(jnp.concatenate(hid, axis=1), wd_buf[slot], preferred_element_type=F32)

    @pl.when(s == 0)
    def _():
        y_ref[MOE_TILE:MOE_TILE + 8, :] = jnp.zeros((8, D_MODEL), F32)
        ys1[...] = jnp.zeros_like(ys1)
        gather(0, hs0)

    for par in (0, 1):
        @pl.when((s < n_act) & (s % 2 == par))
        def _():
            gather(s + 1, hs[1 - par])
            ffn(hs[par], ys[par])
            scatter(jnp.maximum(s - 1, 0), ys[1 - par])

        @pl.when((s == n_act) & (s % 2 == par))
        def _():
            scatter(s - 1, ys[1 - par])


def _moe(h2x, perm, cgrp, nact, wg, wu, wd):
    n = h2x.shape[0]
    n_tiles = n // MOE_TILE
    n_steps = n_tiles * MOE_STEPS

    step = jnp.arange(n_steps)
    active = (step % MOE_STEPS) < jnp.repeat(nact, MOE_STEPS)
    prev = jnp.concatenate([jnp.full((1,), -1, jnp.int32), cgrp[:-1]])
    first = active & ((step % MOE_STEPS == 0) | (cgrp != prev))
    run_id = jnp.cumsum(first.astype(jnp.int32)) - 1
    later_first = lax.cummin(jnp.where(first, step, n_steps), reverse=True)
    nxt_step = jnp.concatenate([later_first[1:], jnp.full((1,), n_steps, jnp.int32)])
    nxt_group = jnp.where(nxt_step < n_steps, cgrp[jnp.minimum(nxt_step, n_steps - 1)], -1)
    runs = jnp.concatenate([first.astype(jnp.int32), run_id % 2, nxt_group.astype(jnp.int32),
                            (first & (run_id == 0)).astype(jnp.int32)])

    def h_idx(i, s, *_):
        done = (s >= jnp.maximum(_[2][i] - 1, 1)).astype(jnp.int32)
        return (jnp.minimum(i + done, n_tiles - 1), 0)

    hbm = pl.BlockSpec(memory_space=pl.ANY)
    return pl.pallas_call(
        _moe_kernel,
        grid_spec=pltpu.PrefetchScalarGridSpec(
            num_scalar_prefetch=4,
            grid=(n_tiles, MOE_STEPS),
            in_specs=[pl.BlockSpec((MOE_TILE, H2X_W), h_idx), hbm, hbm, hbm],
            out_specs=pl.BlockSpec((None, MOE_TILE + 8, D_MODEL), lambda i, s, *_: (i, 0, 0)),
            scratch_shapes=[pltpu.VMEM((MOE_CHUNK, H2X_W), F32), pltpu.VMEM((MOE_CHUNK, H2X_W), F32),
                            pltpu.VMEM((MOE_CHUNK, D_MODEL), F32), pltpu.VMEM((MOE_CHUNK, D_MODEL), F32),
                            pltpu.VMEM((2, EXPERTS_PER_GROUP, D_MODEL, EXPERT_FF), BF16),
                            pltpu.VMEM((2, EXPERTS_PER_GROUP, D_MODEL, EXPERT_FF), BF16),
                            pltpu.VMEM((2, EXPERTS_PER_GROUP * EXPERT_FF, D_MODEL), BF16),
                            pltpu.SemaphoreType.DMA((2, 3))]),
        out_shape=jax.ShapeDtypeStruct((n_tiles, MOE_TILE + 8, D_MODEL), F32),
        compiler_params=_cparams(("arbitrary", "arbitrary")),
        name="moe_experts",
    )(perm, cgrp, nact, runs, h2x, wg, wu, wd)


def _final_kernel(y_ref, x1_ref, mods_ref, fg_ref, o_ref):
    gate2 = mods_ref[:, 5 * D_MODEL:6 * D_MODEL]
    x2 = x1_ref[...] + gate2 * y_ref[...]
    o_ref[...] = x2 * lax.rsqrt(jnp.mean(x2 * x2, axis=-1, keepdims=True) + EPS) * fg_ref[...]


def _final(y, x1, mods3, mod_row0, mod_tokens, fg):
    n = x1.shape[0]
    tm = ROW_TILE
    per_mod = mod_tokens // tm
    per_tile = MOE_TILE // tm
    return pl.pallas_call(
        _final_kernel,
        grid=(n // tm,),
        in_specs=[pl.BlockSpec((None, tm, D_MODEL), lambda j: (j // per_tile, j % per_tile, 0)),
                  pl.BlockSpec((tm, D_MODEL), lambda j: (j, 0)),
                  pl.BlockSpec((None, 1, 6 * D_MODEL), lambda j: (mod_row0 + j // per_mod, 0, 0)),
                  pl.BlockSpec((1, D_MODEL), lambda j: (0, 0))],
        out_specs=pl.BlockSpec((tm, D_MODEL), lambda j: (j, 0)),
        out_shape=jax.ShapeDtypeStruct((n, D_MODEL), F32),
        compiler_params=_cparams(("parallel",)),
        name="final_norm",
    )(y, x1, mods3, fg)


def _rope_tables(t):
    n_freq = QK_DIM // 4
    n_rows = t // GRID_W
    freqs = ROPE_BASE ** (-jnp.arange(n_freq, dtype=F32) / n_freq)
    ang_r = jnp.arange(n_rows, dtype=F32)[:, None] * freqs
    ang_c = jnp.arange(GRID_W, dtype=F32)[:, None] * freqs
    cr, sr, cc, sc = lax.optimization_barrier((jnp.cos(ang_r), jnp.sin(ang_r), jnp.cos(ang_c), jnp.sin(ang_c)))
    j = np.arange(LANES) % QK_DIM
    f_idx = j % n_freq
    by_row = (j < QK_DIM // 2)[None, None, :]
    first = ((j % (QK_DIM // 2)) < n_freq)[None, None, :]

    def table(r_small, c_small):
        return jnp.where(by_row, r_small[:, f_idx][:, None, :], c_small[:, f_idx][None, :, :])

    cos = table(cr, cc)
    sin = table(sr, sc)
    return (cos.reshape(t, LANES), jnp.where(first, -sin, 0.0).reshape(t, LANES),
            jnp.where(first, 0.0, sin).reshape(t, LANES))


def _layer(x, mods3, mod_row0, mod_tokens, rope_tabs, ctx_k, ctx_v, h0f, h0b, lw, layer):
    b, t, _ = x.shape
    n = b * t
    xf = x.reshape(n, D_MODEL)
    res = _inproj(xf, mods3, mod_row0, mod_tokens, t, lw["norm_mix_g"], lw["w_main"], lw["w_dt"], lw["alog"],
                  lw["dtb"], rope_tabs)
    q, kb, vt, z, xbc, cols, hrow = res[:7]
    if ctx_k is None:
        cache = None
        k3 = res[7].reshape(b, 1, t, ATT_HEADS, 2, LANES)[..., :QK_DIM]
        v3 = res[8].reshape(b, 1, t, ATT_HEADS, V_DIM)
    else:
        cache = (ctx_k.astype(BF16), jnp.swapaxes(ctx_v, 1, 2).astype(BF16))
        k3 = v3 = None
    lam0 = 0.8 - 0.6 * math.exp(-0.3 * layer)
    att = _attention(q.reshape(b, t, D_QK), kb.reshape(b, t, D_QK), vt, cache, lw["lamp"], lw["attn_subln_g"],
                     lam0, *_attn_tiling(b, t))
    ssm, hf, hb = _ssd(xbc, cols.reshape(b, t, LANES), hrow.reshape(b, t // CHUNK * HROWS, LANES),
                       z.reshape(b, t, D_SSM), h0f, h0b, lw["conv_w"], lw["conv_b"], lw["d_x"], lw["ssm_norm_g"])
    x1, h2x = _outproj(att.reshape(n, D_ATT), ssm.reshape(n, D_SSM), xf, mods3, mod_row0, mod_tokens,
                       lw["wo_att"], lw["wo_ssm"], lw["norm_ffn_g"], lw["w_router"], lw["b_router"])
    meta = _route(h2x)
    perm = meta[:, 0, :].reshape(-1)
    cgrp = meta[:, 1, :MOE_STEPS].reshape(-1)
    nact = meta[:, 2, 0]
    y = _moe(h2x, perm, cgrp, nact, lw["wg"], lw["wu"], lw["wd"])
    out = _final(y, x1, mods3, mod_row0, mod_tokens, lw["final_g"])
    return out.reshape(b, t, D_MODEL), k3, v3, hf, hb


def _pad_lanes(v, width=LANES):
    return jnp.pad(v, [(0, 0)] * (v.ndim - 1) + [(0, width - v.shape[-1])])


def kernel(x_prompt, x_sample, cache_k, cache_v, state_ssm_fwd, state_ssm_bwd, c, c_ctx, w_ada, b_ada, norm_mix_g, w_in, w_out, lambda_q1, lambda_k1, lambda_q2, lambda_k2, attn_subln_g, conv_w, conv_b, a_log_fwd, a_log_bwd, dt_bias_fwd, dt_bias_bwd, ssm_d, ssm_norm_g, norm_ffn_g, w_group_router, b_group_router, w_expert_router, b_expert_router, w_exp_gate, w_exp_up, w_exp_down, final_norm_g):
    depth = w_in.shape[0]
    assert depth == 1, "single trunk layer"
    bp, tp, _ = x_prompt.shape
    bs, ts, _ = x_sample.shape
    l = 0
    cond = jnp.concatenate([c_ctx[None], c], axis=0)
    condT = _pad_lanes(cond.T, 8)
    mods = _ada(condT, w_ada[l], b_ada[l][None])
    mods3 = mods.reshape(8, 1, 6 * D_MODEL)

    w_router = _pad_lanes(jnp.concatenate([w_group_router[l], w_expert_router[l]], axis=1))
    wr_hi = w_router.astype(BF16)
    wr_lo = (w_router - wr_hi.astype(F32)).astype(BF16)
    lw = dict(
        norm_mix_g=norm_mix_g[l][None],
        w_main=w_in[l].astype(BF16),
        w_dt=_pad_lanes(w_in[l][:, MAIN_COLS:]).astype(BF16),
        lamp=jnp.stack([lambda_q1[l], lambda_k1[l], lambda_q2[l], lambda_k2[l]]),
        attn_subln_g=attn_subln_g[l][None],
        conv_w=conv_w[l].reshape(CONV_W, N_COLBLK, LANES), conv_b=conv_b[l].reshape(N_COLBLK, LANES),
        alog=jnp.broadcast_to(jnp.concatenate([a_log_fwd[l], a_log_bwd[l]])[:, None], (2 * SSM_HEADS, CHUNK)),
        dtb=jnp.broadcast_to(jnp.concatenate([dt_bias_fwd[l], dt_bias_bwd[l]])[:, None], (2 * SSM_HEADS, CHUNK)),
        d_x=jnp.repeat(ssm_d[l], SSM_HEADDIM)[None], ssm_norm_g=ssm_norm_g[l][None],
        wo_att=w_out[l][:D_ATT].astype(BF16), wo_ssm=w_out[l][D_ATT:].astype(BF16),
        norm_ffn_g=norm_ffn_g[l][None],
        w_router=jnp.concatenate([wr_hi, wr_lo], axis=1),
        b_router=_pad_lanes(jnp.concatenate([b_group_router[l], b_expert_router[l]])[None]),
        wg=w_exp_gate[l].astype(BF16), wu=w_exp_up[l].astype(BF16), wd=w_exp_down[l].astype(BF16).reshape(N_EXPERTS * EXPERT_FF, D_MODEL),
        final_g=final_norm_g[None],
    )
    n_state = SSM_HEADS * SSM_HEADDIM
    yp, ck, cv, hf, hb = _layer(x_prompt, mods3, 0, bp * tp, None, None, None, None, None, lw, l)
    ys, _, _, _, _ = _layer(x_sample, mods3, 1, ts, _rope_tables(ts),
                            cache_k[:, l].reshape(bs, -1, D_QK), cache_v[:, l].reshape(bs, -1, D_ATT),
                            state_ssm_fwd[:, l].reshape(bs, n_state, D_STATE),
                            state_ssm_bwd[:, l].reshape(bs, n_state, D_STATE), lw, l)
    new_k, new_v = ck, cv
    new_hf = hf.reshape(bp, 1, SSM_HEADS, SSM_HEADDIM, D_STATE)
    new_hb = hb.reshape(bp, 1, SSM_HEADS, SSM_HEADDIM, D_STATE)
    return yp, ys, new_k, new_v, new_hf, new_hb
```

```python
import functools
import math

import numpy as np
import jax
import jax.numpy as jnp
from jax import lax
from jax.experimental import pallas as pl
from jax.experimental.pallas import tpu as pltpu

D_MODEL = 1024
GRID_W = 64
ATT_HEADS = 4
QK_DIM = 64
V_DIM = 128
D_QK = 512
D_ATT = 512
ROPE_BASE = 10000.0
D_SSM = 512
SSM_HEADDIM = 64
SSM_HEADS = 8
SSM_GROUPS = 2
D_STATE = 128
CONV_W = 5
CHUNK = 128
XBC_DIM = 1024
N_GROUPS = 4
EXPERTS_PER_GROUP = 4
N_EXPERTS = 16
EXPERT_FF = 256
EPS = 1e-6
MAIN_COLS = 2 * D_QK + D_ATT + D_SSM + XBC_DIM
H2X_W = D_MODEL + 128
MOE_TILE = 2048
MOE_CHUNK = 256
MOE_SLOTS = MOE_TILE // MOE_CHUNK + N_GROUPS
MOE_ROWS = MOE_SLOTS * MOE_CHUNK
MOE_STEPS = MOE_SLOTS + 1
META_ROWS = MOE_STEPS * MOE_CHUNK
OUT_PART = 512
ROW_TILE = 512
ADA_COLS = 1024
SSD_SEQS = 2
COL_CUM, COL_DT, COL_ECUM, COL_TOEND = 0, 16, 32, 48
HROWS = 32
LANES = 128
HALO = 8
VMEM_LIMIT = 56 * 1024 * 1024

LOG2E = math.log2(math.e)
SUM_ROWS = 16
F32 = jnp.float32
BF16 = jnp.bfloat16


def _cparams(sem):
    return pltpu.CompilerParams(dimension_semantics=sem, vmem_limit_bytes=VMEM_LIMIT)


def _sigmoid(x):
    return 1.0 / (1.0 + jnp.exp(-x))


def _silu(x):
    return x * _sigmoid(x)


def _ada_kernel(condT_ref, w_ref, b_ref, o_ref):
    s = _silu(condT_ref[...])
    w = w_ref[...]
    b = b_ref[...]
    o_ref[...] = jnp.zeros_like(o_ref)
    for r in range(3):
        o_ref[r:r + 1, :] = jnp.sum(w * s[:, r:r + 1], axis=0, keepdims=True) + b


def _ada(condT, w_ada, b_ada):
    bn = ADA_COLS
    n = w_ada.shape[1]
    return pl.pallas_call(
        _ada_kernel,
        grid=(n // bn,),
        in_specs=[pl.BlockSpec((D_MODEL, 8), lambda j: (0, 0)),
                  pl.BlockSpec((D_MODEL, bn), lambda j: (0, j)),
                  pl.BlockSpec((1, bn), lambda j: (0, j))],
        out_specs=pl.BlockSpec((8, bn), lambda j: (0, j)),
        out_shape=jax.ShapeDtypeStruct((8, n), F32),
        compiler_params=_cparams(("arbitrary",)),
        name="ada",
    )(condT, w_ada, b_ada)


def _split3(x):
    hi = x.astype(BF16)
    r1 = x - hi.astype(F32)
    mid = r1.astype(BF16)
    lo = (r1 - mid.astype(F32)).astype(BF16)
    return jnp.concatenate([hi, mid, lo], axis=-1)


def _scan_matrices():
    t = np.arange(CHUNK)
    pre = (t[:, None] <= t[None, :]).astype(np.float32)
    suf = (t[:, None] >= t[None, :]).astype(np.float32)
    return jnp.asarray(np.stack([np.concatenate([pre] * 3, axis=0), np.concatenate([suf] * 3, axis=0)]), BF16)


def _head_scalars(dt_raw, alog_ref, dtb_ref, scan_ref):
    nh2 = 2 * SSM_HEADS
    xv = dt_raw.T[0:nh2, :] + dtb_ref[...]
    dt = jnp.maximum(xv, 0.0) + jnp.log(1.0 + jnp.exp(-jnp.abs(xv)))
    la3 = _split3(dt * (-jnp.exp(alog_ref[...])))
    fwd = lax.broadcasted_iota(jnp.int32, (nh2, CHUNK), 0) < SSM_HEADS
    cum = jnp.where(fwd, jnp.dot(la3, scan_ref[0], preferred_element_type=F32),
                    jnp.dot(la3, scan_ref[1], preferred_element_type=F32))
    cum_end = jnp.where(fwd, cum[:, CHUNK - 1:CHUNK], cum[:, 0:1])
    packed = jnp.concatenate([cum, dt, jnp.exp(cum), jnp.exp(cum_end - cum),
                              jnp.zeros((LANES - 4 * nh2, CHUNK), F32)], axis=0)
    return packed.T, jnp.concatenate([cum, jnp.exp(cum_end)], axis=0)


def _inproj_kernel(rope, n_part, x_ref, *refs):
    rows = x_ref.shape[0] // n_part
    for part in range(n_part):
        _inproj_rows(rope, part * rows, rows, x_ref, *refs)


def _inproj_rows(rope, r0, n_rows, x_ref, mods_ref, g_ref, w_ref, wdt_ref, alog_ref, dtb_ref, scan_ref, *rest):
    if rope:
        cos_ref, sa_ref, sb_ref, q_ref, kb_ref, vt_ref, z_ref, xbc_ref, cols_ref, hrow_ref = rest
    else:
        q_ref, kb_ref, vt_ref, z_ref, xbc_ref, cols_ref, hrow_ref, kf_ref, vf_ref = rest
    rs = slice(r0, r0 + n_rows)
    x = x_ref[rs, :]
    shift = mods_ref[:, 0:D_MODEL]
    scale = mods_ref[:, D_MODEL:2 * D_MODEL]
    y = x * lax.rsqrt(jnp.mean(x * x, axis=-1, keepdims=True) + EPS) * g_ref[...]
    h = (y * (1.0 + scale) + shift).astype(BF16)
    dt_raw = jnp.dot(h, wdt_ref[...], preferred_element_type=F32)
    for ci in range(n_rows // CHUNK):
        cols, hrow = _head_scalars(dt_raw[ci * CHUNK:(ci + 1) * CHUNK, :], alog_ref, dtb_ref, scan_ref)
        cols_ref[r0 + ci * CHUNK:r0 + (ci + 1) * CHUNK, :] = cols
        c_abs = r0 // CHUNK + ci
        hrow_ref[c_abs * HROWS:(c_abs + 1) * HROWS, :] = hrow
    r = jnp.dot(h, w_ref[...], preferred_element_type=F32)
    q = r[:, 0:D_QK]
    k = r[:, D_QK:2 * D_QK]
    if rope:
        cos = cos_ref[rs, :]
        sa = sa_ref[rs, :]
        sb = sb_ref[rs, :]

        def rot(t):
            parts = []
            for hh in range(ATT_HEADS):
                th = t[:, hh * LANES:(hh + 1) * LANES]
                parts.append(th * cos + pltpu.roll(th, LANES - 16, 1) * sa + pltpu.roll(th, 16, 1) * sb)
            return jnp.concatenate(parts, axis=1)

        q = rot(q)
        k = rot(k)
    v = r[:, 2 * D_QK:2 * D_QK + D_ATT]
    q_ref[rs, :] = q
    kb_ref[rs, :] = k.astype(BF16)
    t_seq = vt_ref.shape[2]
    if t_seq >= n_rows:
        spans = [(r0 // t_seq, slice(r0 % t_seq, r0 % t_seq + n_rows), slice(0, n_rows))]
    else:
        spans = [(r0 // t_seq + j, slice(0, t_seq), slice(j * t_seq, (j + 1) * t_seq))
                 for j in range(n_rows // t_seq)]
    v_t = v.T.astype(BF16)
    for sq, ts, ls in spans:
        vt_ref[sq, :, ts] = v_t[:, ls]
    if not rope:
        for hh in range(ATT_HEADS):
            k_h = k[:, hh * LANES:(hh + 1) * LANES]
            vf_ref[pl.ds(r0 * ATT_HEADS + hh, n_rows, stride=ATT_HEADS), :] = v[:, hh * LANES:(hh + 1) * LANES]
            kf_ref[pl.ds(r0 * 2 * ATT_HEADS + 2 * hh, n_rows, stride=2 * ATT_HEADS), :] = k_h
            kf_ref[pl.ds(r0 * 2 * ATT_HEADS + 2 * hh + 1, n_rows, stride=2 * ATT_HEADS), :] = (
                pltpu.roll(k_h, QK_DIM, 1))
    z_ref[rs, :] = r[:, 2 * D_QK + D_ATT:2 * D_QK + D_ATT + D_SSM]
    x0 = 2 * D_QK + D_ATT + D_SSM
    for cb in range(XBC_DIM // LANES):
        for sq, ts, ls in spans:
            xbc_ref[sq, cb, ts, :] = r[ls, x0 + cb * LANES:x0 + (cb + 1) * LANES]


def _inproj(x, mods3, mod_row0, mod_tokens, seq_len, g, w_main, w_dt, alog, dtb, rope_tabs):
    n = x.shape[0]
    n_part = 2
    tm = 2 * ROW_TILE if seq_len % (2 * ROW_TILE) == 0 else ROW_TILE
    assert seq_len % tm == 0 or tm % seq_len == 0
    per_seq = max(seq_len // tm, 1)
    n_seq = max(tm // seq_len, 1)
    t_seq = tm // n_seq
    per_mod = mod_tokens // tm
    rope = rope_tabs is not None
    in_specs = [pl.BlockSpec((tm, D_MODEL), lambda i: (i, 0)),
                pl.BlockSpec((None, 1, 6 * D_MODEL), lambda i: (mod_row0 + i // per_mod, 0, 0)),
                pl.BlockSpec((1, D_MODEL), lambda i: (0, 0)),
                pl.BlockSpec((D_MODEL, MAIN_COLS), lambda i: (0, 0)),
                pl.BlockSpec((D_MODEL, LANES), lambda i: (0, 0)),
                pl.BlockSpec((2 * SSM_HEADS, CHUNK), lambda i: (0, 0)),
                pl.BlockSpec((2 * SSM_HEADS, CHUNK), lambda i: (0, 0)),
                pl.BlockSpec((2, 3 * CHUNK, CHUNK), lambda i: (0, 0, 0))]
    args = [x, mods3, g, w_main, w_dt, alog, dtb, _scan_matrices()]
    if rope:
        tab_spec = pl.BlockSpec((tm, LANES), lambda i: (i % per_seq, 0))
        in_specs += [tab_spec] * 3
        args += list(rope_tabs)
    def rows(wd, dtype=F32):
        return pl.BlockSpec((tm, wd), lambda i: (i, 0)), jax.ShapeDtypeStruct((n, wd), dtype)

    hr = tm // CHUNK * HROWS
    outs = [rows(D_QK), rows(D_QK, BF16),
            (pl.BlockSpec((n_seq, D_ATT, t_seq), lambda i: (i // per_seq, 0, i % per_seq)),
             jax.ShapeDtypeStruct((n // seq_len, D_ATT, seq_len), BF16)),
            rows(D_SSM),
            (pl.BlockSpec((n_seq, XBC_DIM // LANES, t_seq, LANES), lambda i: (i // per_seq, 0, i % per_seq, 0)),
             jax.ShapeDtypeStruct((n // seq_len, XBC_DIM // LANES, seq_len, LANES), F32)),
            rows(LANES),
            (pl.BlockSpec((hr, LANES), lambda i: (i, 0)), jax.ShapeDtypeStruct((n // CHUNK * HROWS, LANES), F32))]
    if not rope:
        outs += [(pl.BlockSpec((tm * 2 * ATT_HEADS, LANES), lambda i: (i, 0)),
                  jax.ShapeDtypeStruct((n * 2 * ATT_HEADS, LANES), F32)),
                 (pl.BlockSpec((tm * ATT_HEADS, LANES), lambda i: (i, 0)),
                  jax.ShapeDtypeStruct((n * ATT_HEADS, LANES), F32))]
    return pl.pallas_call(
        functools.partial(_inproj_kernel, rope, n_part),
        grid=(n // tm,),
        in_specs=in_specs,
        out_specs=[o[0] for o in outs],
        out_shape=[o[1] for o in outs],
        compiler_params=_cparams(("parallel",)),
        name="inproj_rope" if rope else "inproj",
    )(*args)


def _attn_kernel(tk, lam0, has_cache, q_ref, k_ref, vt_ref, *rest):
    if has_cache:
        ck_ref, cvt_ref, lamp_ref, g_ref, o_ref = rest
    else:
        lamp_ref, g_ref, o_ref = rest
    tq = q_ref.shape[1]
    lp = lamp_ref[...]
    lam = (jnp.exp(jnp.sum(lp[0:1] * lp[1:2], axis=-1, keepdims=True))
           - jnp.exp(jnp.sum(lp[2:3] * lp[3:4], axis=-1, keepdims=True)) + lam0)
    for bb, hh in [(b_, h_) for b_ in range(q_ref.shape[0]) for h_ in range(q_ref.shape[2] // LANES)]:
        hs = slice(hh * LANES, (hh + 1) * LANES)
        chunks = [(k_ref, vt_ref, c * tk, tk) for c in range(k_ref.shape[1] // tk)]
        if has_cache:
            ck = min(tk, ck_ref.shape[1])
            chunks += [(ck_ref, cvt_ref, c * ck, ck) for c in range(ck_ref.shape[1] // ck)]
        q = q_ref[bb, :, hs] * (QK_DIM ** -0.5 * LOG2E)
        lane = lax.broadcasted_iota(jnp.int32, q.shape, 1)
        qq_t = jnp.concatenate([jnp.where(lane < QK_DIM, q, 0.0), jnp.where(lane >= QK_DIM, q, 0.0)],
                               axis=0).T.astype(BF16)

        def scores(chunk):
            kr, _, start, size = chunk
            return jnp.dot(kr[bb, start:start + size, hs], qq_t, preferred_element_type=F32)

        def update(s, chunk, m, acc):
            _, vr, start, size = chunk
            m_new = jnp.maximum(m, jnp.max(s, axis=0, keepdims=True))
            alpha = jnp.exp2(m - m_new)
            p = jnp.exp2(s - m_new).astype(BF16)
            v_ext = jnp.concatenate([vr[bb, hs, start:start + size], jnp.ones((SUM_ROWS, size), BF16)], axis=0)
            acc = alpha * acc + jnp.dot(v_ext, p, preferred_element_type=F32)
            return m_new, acc

        m = jnp.full((1, 2 * tq), -jnp.inf, F32)
        acc = jnp.zeros((V_DIM + SUM_ROWS, 2 * tq), F32)
        s = scores(chunks[0])
        for c, chunk in enumerate(chunks):
            s_next = scores(chunks[c + 1]) if c + 1 < len(chunks) else None
            m, acc = update(s, chunk, m, acc)
            s = s_next
        o = acc[0:V_DIM] / acc[V_DIM:V_DIM + 1]
        o = (o[:, 0:tq] - lam * o[:, tq:2 * tq]).T
        o = o * lax.rsqrt(jnp.mean(o * o, axis=-1, keepdims=True) + EPS)
        o_ref[bb, :, hs] = o * g_ref[...] * (1.0 - lam0)


def _attn_tiling(b, t):
    if t % 1024 == 0:
        return 1024, 512, 1, 1
    assert t % 256 == 0 and b % 2 == 0
    return 256, 256, ATT_HEADS, 2


def _attention(q, k, vt, cache, lamp, g, lam0, tq, tk, nh, nb):
    b, t, _ = q.shape
    wd = nh * LANES

    def kv_specs(length):
        return [pl.BlockSpec((nb, length, wd), lambda bi, h, i: (bi, 0, h)),
                pl.BlockSpec((nb, wd, length), lambda bi, h, i: (bi, h, 0))]

    in_specs = [pl.BlockSpec((nb, tq, wd), lambda bi, h, i: (bi, i, h))] + kv_specs(t)
    args = [q, k, vt]
    if cache is not None:
        assert cache[0].shape[1] % min(tk, cache[0].shape[1]) == 0
        in_specs += kv_specs(cache[0].shape[1])
        args += list(cache)
    in_specs += [pl.BlockSpec((4, QK_DIM), lambda bi, h, i: (0, 0)),
                 pl.BlockSpec((1, V_DIM), lambda bi, h, i: (0, 0))]
    return pl.pallas_call(
        functools.partial(_attn_kernel, tk, lam0, cache is not None),
        grid=(b // nb, ATT_HEADS // nh, t // tq),
        in_specs=in_specs,
        out_specs=pl.BlockSpec((nb, tq, wd), lambda bi, h, i: (bi, i, h)),
        out_shape=jax.ShapeDtypeStruct((b, t, D_ATT), F32),
        compiler_params=_cparams(("parallel", "parallel", "arbitrary")),
        name="diff_attn",
    )(*args, lamp, g)


N_COLBLK = XBC_DIM // LANES
def _expand_matrices():
    out = []
    for d in range(2):
        e = np.zeros((LANES, 3 * D_SSM), np.float32)
        for blk, lane0 in enumerate((COL_DT, COL_ECUM, COL_TOEND)):
            for h in range(SSM_HEADS):
                e[lane0 + d * SSM_HEADS + h,
                  blk * D_SSM + h * SSM_HEADDIM:blk * D_SSM + (h + 1) * SSM_HEADDIM] = 1.0
        out.append(np.concatenate([e, e, e], axis=0))
    return jnp.asarray(np.stack(out), BF16)


def _ssd_chunk(reverse, xs, bm, cm, state, hrow, cols, e_ref):
    d0 = SSM_HEADS if reverse else 0
    cum = hrow[d0:d0 + SSM_HEADS, :]
    dec = hrow[2 * SSM_HEADS + d0:2 * SSM_HEADS + d0 + SSM_HEADS, 0:1]
    row = lax.broadcasted_iota(jnp.int32, (CHUNK, LANES), 0)
    lane = lax.broadcasted_iota(jnp.int32, (CHUNK, LANES), 1)
    causal = (row <= lane) if reverse else (row >= lane)
    lane_g = lax.broadcasted_iota(jnp.int32, (CHUNK, 2 * LANES), 1)
    spread = jnp.dot(_split3(cols), e_ref[1 if reverse else 0], preferred_element_type=F32)
    xd = xs * spread[:, 0:D_SSM]
    xdw = (xd * spread[:, 2 * D_SSM:3 * D_SSM]).astype(BF16)
    xd = xd.astype(BF16)
    ecum_x = spread[:, D_SSM:2 * D_SSM]
    rep = SSM_HEADS // SSM_GROUPS
    y_parts = []
    new_state = []
    for g in range(SSM_GROUPS):
        bg = bm[:, g * D_STATE:(g + 1) * D_STATE]
        cg = cm[:, g * D_STATE:(g + 1) * D_STATE]
        cbt = lax.dot_general(cg, bg, (((1,), (1,)), ((), ())), preferred_element_type=F32)
        rows = slice(g * rep * SSM_HEADDIM, (g + 1) * rep * SSM_HEADDIM)
        st_g = state[rows, :]
        y_off = lax.dot_general(cg, st_g.astype(BF16), (((1,), (1,)), ((), ())),
                                preferred_element_type=F32)
        cst = lax.dot_general(xdw[:, rows], bg, (((0,), (0,)), ((), ())), preferred_element_type=F32)
        xd_g = xd[:, rows]
        scs = []
        blocks = []
        for hh in range(rep):
            h = g * rep + hh
            seg = cols[:, COL_CUM + d0 + h:COL_CUM + d0 + h + 1] - cum[h:h + 1, :]
            decay = jnp.exp(jnp.where(causal, seg, -jnp.inf))
            scs.append((cbt * decay).astype(BF16))
            blocks.append(jnp.where(lane_g // SSM_HEADDIM == hh, xd_g, jnp.zeros_like(xd_g)))
        y_diag = jnp.dot(jnp.concatenate(scs, axis=1), jnp.concatenate(blocks, axis=0),
                         preferred_element_type=F32)
        dec_rows = jnp.concatenate(
            [jnp.broadcast_to(dec[g * rep + hh:g * rep + hh + 1, :], (SSM_HEADDIM, D_STATE)) for hh in range(rep)],
            axis=0)
        new_state.append(st_g * dec_rows + cst)
        y_parts.append(y_diag + y_off * ecum_x[:, rows])
    return jnp.concatenate(y_parts, axis=1), jnp.concatenate(new_state, axis=0)


def _ssd_kernel(nc, has_h0, xc_ref, xp_ref, xn_ref, cols_ref, hrow_ref, z_ref, *rest):
    if has_h0:
        h0f_ref, h0b_ref = rest[:2]
        rest = rest[2:]
    cw_ref, cb_ref, d_ref, ng_ref, e_ref, y_ref, hf_ref, hb_ref, state_ref, slab_ref, xs_ref, bc_ref, yf_ref = rest
    ps = pl.program_id(1)
    c = pl.program_id(2)
    n_bc = SSM_GROUPS * D_STATE

    @pl.when((ps == 0) & (c == 0))
    def _():
        state_ref[...] = h0f_ref[...] if has_h0 else jnp.zeros_like(state_ref)

    @pl.when(ps == 0)
    def _():
        t0 = pl.multiple_of(c * CHUNK, CHUNK)
        for bb in range(state_ref.shape[0]):
            slab_ref[bb, :, 0:HALO, :] = jnp.where(c > 0, xp_ref[bb], 0.0)
            slab_ref[bb, :, HALO:HALO + CHUNK, :] = xc_ref[bb]
            slab_ref[bb, :, HALO + CHUNK:2 * HALO + CHUNK, :] = jnp.where(c < nc - 1, xn_ref[bb], 0.0)
            blocks = []
            for cb in range(N_COLBLK):
                conv = cb_ref[cb:cb + 1, :]
                for kk in range(CONV_W):
                    off = HALO - CONV_W // 2 + kk
                    conv = conv + slab_ref[bb, cb, off:off + CHUNK, :] * cw_ref[kk, cb:cb + 1, :]
                blocks.append(_silu(conv))
            xs = jnp.concatenate(blocks[0:D_SSM // LANES], axis=1)
            bc = jnp.concatenate(blocks[D_SSM // LANES:], axis=1).astype(BF16)
            y, state = _ssd_chunk(False, xs, bc[:, 0:n_bc], bc[:, n_bc:], state_ref[bb], hrow_ref[bb],
                                  cols_ref[bb], e_ref)
            xs_ref[bb, pl.ds(t0, CHUNK), :] = xs
            bc_ref[bb, pl.ds(t0, CHUNK), :] = bc
            yf_ref[bb, pl.ds(t0, CHUNK), :] = y
            state_ref[bb] = state

    @pl.when((ps == 0) & (c == nc - 1))
    def _():
        hf_ref[...] = state_ref[...]
        state_ref[...] = h0b_ref[...] if has_h0 else jnp.zeros_like(state_ref)

    @pl.when(ps == 1)
    def _():
        t0 = pl.multiple_of((nc - 1 - c) * CHUNK, CHUNK)
        for bb in range(state_ref.shape[0]):
            xs = xs_ref[bb, pl.ds(t0, CHUNK), :]
            bc = bc_ref[bb, pl.ds(t0, CHUNK), :]
            y, state = _ssd_chunk(True, xs, bc[:, 0:n_bc], bc[:, n_bc:], state_ref[bb], hrow_ref[bb],
                                  cols_ref[bb], e_ref)
            state_ref[bb] = state
            y = y + yf_ref[bb, pl.ds(t0, CHUNK), :] + xs * d_ref[...]
            y = y * _silu(z_ref[bb])
            y = y * lax.rsqrt(jnp.mean(y * y, axis=-1, keepdims=True) + EPS)
            y_ref[bb] = y * ng_ref[...]

    @pl.when((ps == 1) & (c == nc - 1))
    def _():
        hb_ref[...] = state_ref[...]


def _ssd(xbc, cols, hrow, z, h0f, h0b, conv_w, conv_b, d_x, norm_g):
    b, _, l, _ = xbc.shape
    has_h0 = h0f is not None
    nb = SSD_SEQS
    assert b % nb == 0
    nc = l // CHUNK
    hb = CHUNK // HALO
    n_state = SSM_HEADS * SSM_HEADDIM
    last = nc - 1

    def fwd_chunk(ps, ci):
        return jnp.where(ps == 0, ci, last)

    def any_chunk(ps, ci):
        return jnp.where(ps == 0, ci, last - ci)

    def bwd_chunk(ps, ci):
        return jnp.where(ps == 0, last, last - ci)

    const2 = lambda bi, ps, ci: (0, 0)
    state_spec = pl.BlockSpec((nb, n_state, D_STATE), lambda bi, ps, ci: (bi, 0, 0))
    in_specs = [pl.BlockSpec((nb, N_COLBLK, CHUNK, LANES), lambda bi, ps, ci: (bi, 0, fwd_chunk(ps, ci), 0)),
                pl.BlockSpec((nb, N_COLBLK, HALO, LANES),
                             lambda bi, ps, ci: (bi, 0, jnp.maximum(fwd_chunk(ps, ci) * hb - 1, 0), 0)),
                pl.BlockSpec((nb, N_COLBLK, HALO, LANES),
                             lambda bi, ps, ci: (bi, 0, jnp.minimum((fwd_chunk(ps, ci) + 1) * hb, l // HALO - 1), 0)),
                pl.BlockSpec((nb, CHUNK, LANES), lambda bi, ps, ci: (bi, any_chunk(ps, ci), 0)),
                pl.BlockSpec((nb, HROWS, LANES), lambda bi, ps, ci: (bi, any_chunk(ps, ci), 0)),
                pl.BlockSpec((nb, CHUNK, D_SSM), lambda bi, ps, ci: (bi, bwd_chunk(ps, ci), 0)),
                *([state_spec, state_spec] if has_h0 else []),
                pl.BlockSpec((CONV_W, N_COLBLK, LANES), lambda bi, ps, ci: (0, 0, 0)),
                pl.BlockSpec((N_COLBLK, LANES), const2),
                pl.BlockSpec((1, D_SSM), const2),
                pl.BlockSpec((1, D_SSM), const2),
                pl.BlockSpec((2, 3 * LANES, 3 * D_SSM), lambda bi, ps, ci: (0, 0, 0))]
    return pl.pallas_call(
        functools.partial(_ssd_kernel, nc, has_h0),
        grid=(b // nb, 2, nc),
        in_specs=in_specs,
        out_specs=[pl.BlockSpec((nb, CHUNK, D_SSM), lambda bi, ps, ci: (bi, bwd_chunk(ps, ci), 0)),
                   state_spec, state_spec],
        out_shape=[jax.ShapeDtypeStruct((b, l, D_SSM), F32),
                   jax.ShapeDtypeStruct((b, n_state, D_STATE), F32),
                   jax.ShapeDtypeStruct((b, n_state, D_STATE), F32)],
        scratch_shapes=[pltpu.VMEM((nb, n_state, D_STATE), F32),
                        pltpu.VMEM((nb, N_COLBLK, CHUNK + 2 * HALO, LANES), F32),
                        pltpu.VMEM((nb, l, D_SSM), F32),
                        pltpu.VMEM((nb, l, 2 * SSM_GROUPS * D_STATE), BF16),
                        pltpu.VMEM((nb, l, D_SSM), F32)],
        compiler_params=_cparams(("parallel", "arbitrary", "arbitrary")),
        name="ssd",
    )(xbc, xbc, xbc, cols, hrow, z, *([h0f, h0b] if has_h0 else []), conv_w, conv_b, d_x, norm_g,
      _expand_matrices())


def _outproj_kernel(att_ref, ssm_ref, x_ref, mods_ref, wa_ref, ws_ref, g_ref, wr_ref, br_ref,
                    x1_ref, h2x_ref):
    for part in range(x_ref.shape[0] // OUT_PART):
        rows = slice(part * OUT_PART, (part + 1) * OUT_PART)
        _outproj_rows(rows, att_ref, ssm_ref, x_ref, mods_ref, wa_ref, ws_ref, g_ref, wr_ref, br_ref,
                      x1_ref, h2x_ref)


def _outproj_rows(rows, att_ref, ssm_ref, x_ref, mods_ref, wa_ref, ws_ref, g_ref, wr_ref, br_ref,
                  x1_ref, h2x_ref):
    mix = (jnp.dot(att_ref[rows, :].astype(BF16), wa_ref[...], preferred_element_type=F32)
           + jnp.dot(ssm_ref[rows, :].astype(BF16), ws_ref[...], preferred_element_type=F32))
    gate1 = mods_ref[:, 2 * D_MODEL:3 * D_MODEL]
    shift2 = mods_ref[:, 3 * D_MODEL:4 * D_MODEL]
    scale2 = mods_ref[:, 4 * D_MODEL:5 * D_MODEL]
    x1 = x_ref[rows, :] + gate1 * mix
    x1_ref[rows, :] = x1
    y = x1 * lax.rsqrt(jnp.mean(x1 * x1, axis=-1, keepdims=True) + EPS) * g_ref[...]
    h2 = y * (1.0 + scale2) + shift2
    h_hi = h2.astype(BF16)
    h2x_ref[rows, 0:D_MODEL] = h2
    h_lo = (h2 - h_hi.astype(F32)).astype(BF16)
    both = jnp.dot(h_hi, wr_ref[...], preferred_element_type=F32)
    logits = (both[:, 0:LANES] + both[:, LANES:2 * LANES]
              + jnp.dot(h_lo, wr_ref[:, 0:LANES], preferred_element_type=F32)) + br_ref[...]
    lane = lax.broadcasted_iota(jnp.int32, logits.shape, 1).astype(F32)
    neg = -jnp.inf
    big = float(1 << 20)
    is_g = lane < N_GROUPS
    gl = jnp.where(is_g, logits, neg)
    gmax = jnp.max(gl, axis=-1, keepdims=True)
    g_idx = jnp.min(jnp.where(gl == gmax, lane, big), axis=-1, keepdims=True)
    p_g = 1.0 / jnp.sum(jnp.where(is_g, jnp.exp(gl - gmax), 0.0), axis=-1, keepdims=True)
    e_lo = N_GROUPS + g_idx * EXPERTS_PER_GROUP
    in_grp = (lane >= e_lo) & (lane < e_lo + EXPERTS_PER_GROUP)
    el = jnp.where(in_grp, logits, neg)
    m1 = jnp.max(el, axis=-1, keepdims=True)
    i1 = jnp.min(jnp.where(el == m1, lane, big), axis=-1, keepdims=True)
    el2 = jnp.where(lane == i1, neg, el)
    m2 = jnp.max(el2, axis=-1, keepdims=True)
    i2 = jnp.min(jnp.where(el2 == m2, lane, big), axis=-1, keepdims=True)
    e2 = jnp.exp(m2 - m1)
    w1 = p_g / (1.0 + e2)
    w2 = p_g * e2 / (1.0 + e2)
    slab = (jnp.where(lane == i1 - e_lo, w1, 0.0) + jnp.where(lane == i2 - e_lo, w2, 0.0)
            + jnp.where(lane == EXPERTS_PER_GROUP, g_idx, 0.0))
    h2x_ref[rows, D_MODEL:H2X_W] = slab


def _outproj(att, ssm, x, mods3, mod_row0, mod_tokens, wo_att, wo_ssm, g, w_router, b_router):
    n = x.shape[0]
    tm = 2 * OUT_PART
    per_mod = mod_tokens // tm
    return pl.pallas_call(
        _outproj_kernel,
        grid=(n // tm,),
        in_specs=[pl.BlockSpec((tm, D_ATT), lambda i: (i, 0)),
                  pl.BlockSpec((tm, D_SSM), lambda i: (i, 0)),
                  pl.BlockSpec((tm, D_MODEL), lambda i: (i, 0)),
                  pl.BlockSpec((None, 1, 6 * D_MODEL), lambda i: (mod_row0 + i // per_mod, 0, 0)),
                  pl.BlockSpec((D_ATT, D_MODEL), lambda i: (0, 0)),
                  pl.BlockSpec((D_SSM, D_MODEL), lambda i: (0, 0)),
                  pl.BlockSpec((1, D_MODEL), lambda i: (0, 0)),
                  pl.BlockSpec((D_MODEL, 2 * LANES), lambda i: (0, 0)),
                  pl.BlockSpec((1, LANES), lambda i: (0, 0))],
        out_specs=[pl.BlockSpec((tm, D_MODEL), lambda i: (i, 0)),
                   pl.BlockSpec((tm, H2X_W), lambda i: (i, 0))],
        out_shape=[jax.ShapeDtypeStruct((n, D_MODEL), F32),
                   jax.ShapeDtypeStruct((n, H2X_W), F32)],
        compiler_params=_cparams(("parallel",)),
        name="outproj_router",
    )(att, ssm, x, mods3, wo_att, wo_ssm, g, w_router, b_router)


def _route_kernel(slab_ref, meta_ref):
    t_n = MOE_TILE
    blk = LANES
    slab = slab_ref[...]
    lane = lax.broadcasted_iota(jnp.int32, (t_n, LANES), 1)
    gcol = jnp.sum(jnp.where(lane == EXPERTS_PER_GROUP, slab, 0.0), axis=-1, keepdims=True)
    member = (lane.astype(F32) == gcol) & (lane < N_GROUPS)
    a = jnp.where(member, 1.0, 0.0).astype(BF16)
    r_i = lax.broadcasted_iota(jnp.int32, (blk, blk), 0)
    c_i = lax.broadcasted_iota(jnp.int32, (blk, blk), 1)
    lower = jnp.where(c_i < r_i, 1.0, 0.0).astype(BF16)
    upper = jnp.where(r_i < c_i, 1.0, 0.0).astype(BF16)
    offs = jnp.zeros((1, LANES), F32)
    ranks = []
    for b in range(t_n // blk):
        ab = a[b * blk:(b + 1) * blk]
        rb = jnp.dot(lower, ab, preferred_element_type=F32)
        ranks.append(rb + offs)
        offs = offs + rb[blk - 1:blk] + ab[blk - 1:blk].astype(F32)
    rank = jnp.concatenate(ranks, axis=0)
    n_chunk = jnp.floor((offs + (MOE_CHUNK - 1)) * (1.0 / MOE_CHUNK))
    start = jnp.dot(jnp.broadcast_to(n_chunk, (8, LANES)).astype(BF16), upper,
                    preferred_element_type=F32)[0:1]
    end = start + n_chunk
    dest = jnp.sum(jnp.where(member, start * MOE_CHUNK + rank, 0.0), axis=-1, keepdims=True)
    tok = lax.broadcasted_iota(jnp.int32, (t_n, LANES), 0)
    digits = jnp.where(lane == 0, (tok // blk).astype(F32),
                       jnp.where(lane == 1, (tok % blk).astype(F32), jnp.where(lane == 2, 1.0, 0.0))).astype(BF16)
    sw = 512
    pieces = []
    for sc in range(MOE_ROWS // sw):
        s_id = (lax.broadcasted_iota(jnp.int32, (t_n, sw), 1) + sc * sw).astype(F32)
        hit = jnp.where(dest == s_id, 1.0, 0.0).astype(BF16)
        r = lax.dot_general(digits, hit, (((0,), (0,)), ((), ())), preferred_element_type=F32)
        tok_of = r[0:1] * blk + r[1:2]
        pieces.append(jnp.where(r[2:3] > 0.5, tok_of, float(t_n)))
    pieces.append(jnp.full((1, META_ROWS - MOE_ROWS), float(t_n), F32))
    perm = jnp.concatenate(pieces, axis=1)
    slot = lax.broadcasted_iota(jnp.int32, (1, META_ROWS), 1).astype(F32)
    lane1 = lax.broadcasted_iota(jnp.int32, (1, LANES), 1)
    cg = jnp.zeros((1, META_ROWS), F32)
    for g in range(N_GROUPS):
        end_g = jnp.sum(jnp.where(lane1 == g, end, 0.0), axis=-1, keepdims=True)
        cg = cg + jnp.where(slot >= end_g, 1.0, 0.0)
    n_act = jnp.broadcast_to(end_g, (1, META_ROWS))
    meta_ref[...] = jnp.concatenate([perm, cg, n_act, jnp.zeros((5, META_ROWS), F32)], axis=0).astype(jnp.int32)


def _route(h2x):
    n = h2x.shape[0]
    n_tiles = n // MOE_TILE
    return pl.pallas_call(
        _route_kernel,
        grid=(n_tiles,),
        in_specs=[pl.BlockSpec((MOE_TILE, LANES), lambda i: (i, D_MODEL // LANES))],
        out_specs=pl.BlockSpec((None, 8, META_ROWS), lambda i: (i, 0, 0)),
        out_shape=jax.ShapeDtypeStruct((n_tiles, 8, META_ROWS), jnp.int32),
        compiler_params=_cparams(("parallel",)),
        name="moe_route",
    )(h2x)


def _moe_kernel(perm_ref, cg_ref, nact_ref, run_ref, h_ref, wg_hbm, wu_hbm, wd_hbm, y_ref,
                hs0, hs1, ys0, ys1, wg_buf, wu_buf, wd_buf, w_sem):
    i = pl.program_id(0)
    s = pl.program_id(1)
    n_act = nact_ref[i]
    tile_base = i * META_ROWS
    hs = (hs0, hs1)
    ys = (ys0, ys1)
    n_steps = pl.num_programs(0) * MOE_STEPS
    f = i * MOE_STEPS + s
    slot = run_ref[n_steps + f]

    def weight_copies(g, sl):
        e0 = g * EXPERTS_PER_GROUP
        r0 = g * (EXPERTS_PER_GROUP * EXPERT_FF)
        return (pltpu.make_async_copy(wg_hbm.at[pl.ds(e0, EXPERTS_PER_GROUP)], wg_buf.at[sl], w_sem.at[sl, 0]),
                pltpu.make_async_copy(wu_hbm.at[pl.ds(e0, EXPERTS_PER_GROUP)], wu_buf.at[sl], w_sem.at[sl, 1]),
                pltpu.make_async_copy(wd_hbm.at[pl.ds(r0, EXPERTS_PER_GROUP * EXPERT_FF)], wd_buf.at[sl],
                                      w_sem.at[sl, 2]))

    @pl.when(run_ref[f] == 1)
    def _():
        g = jnp.minimum(cg_ref[f], N_GROUPS - 1)

        @pl.when(run_ref[3 * n_steps + f] == 1)
        def _():
            for cp in weight_copies(g, slot):
                cp.start()

        for cp in weight_copies(g, slot):
            cp.wait()
        nxt = run_ref[2 * n_steps + f]

        @pl.when(nxt >= 0)
        def _():
            for cp in weight_copies(nxt, 1 - slot):
                cp.start()

    def gather(chunk, dst):
        base = tile_base + chunk * MOE_CHUNK
        for r in range(MOE_CHUNK):
            src = jnp.minimum(perm_ref[base + r], MOE_TILE - 1)
            dst[r:r + 1, :] = h_ref[pl.ds(src, 1), :]

    def scatter(chunk, src):
        base = tile_base + chunk * MOE_CHUNK
        for r in range(MOE_CHUNK):
            y_ref[pl.ds(perm_ref[base + r], 1), :] = src[r:r + 1, :]

    def ffn(src, dst):
        hb = src[:, 0:D_MODEL].astype(BF16)
        cw = src[:, D_MODEL:H2X_W]
        hid = []
        for e in range(EXPERTS_PER_GROUP):
            a = jnp.dot(hb, wg_buf[slot, e], preferred_element_type=F32)
            u = jnp.dot(hb, wu_buf[slot, e], preferred_element_type=F32)
            hid.append((_silu(a) * u * cw[:, e:e + 1]).astype(BF16))
        dst[...] = jnp.dot(jnp.concatenate(hid, axis=1), wd_buf[slot], preferred_element_type=F32)

    @pl.when(s == 0)
    def _():
        y_ref[MOE_TILE:MOE_TILE + 8, :] = jnp.zeros((8, D_MODEL), F32)
        ys1[...] = jnp.zeros_like(ys1)
        gather(0, hs0)

    for par in (0, 1):
        @pl.when((s < n_act) & (s % 2 == par))
        def _():
            gather(s + 1, hs[1 - par])
            ffn(hs[par], ys[par])
            scatter(jnp.maximum(s - 1, 0), ys[1 - par])

        @pl.when((s == n_act) & (s % 2 == par))
        def _():
            scatter(s - 1, ys[1 - par])


def _moe(h2x, perm, cgrp, nact, wg, wu, wd):
    n = h2x.shape[0]
    n_tiles = n // MOE_TILE
    n_steps = n_tiles * MOE_STEPS

    step = jnp.arange(n_steps)
    active = (step % MOE_STEPS) < jnp.repeat(nact, MOE_STEPS)
    prev = jnp.concatenate([jnp.full((1,), -1, jnp.int32), cgrp[:-1]])
    first = active & ((step % MOE_STEPS == 0) | (cgrp != prev))
    run_id = jnp.cumsum(first.astype(jnp.int32)) - 1
    later_first = lax.cummin(jnp.where(first, step, n_steps), reverse=True)
    nxt_step = jnp.concatenate([later_first[1:], jnp.full((1,), n_steps, jnp.int32)])
    nxt_group = jnp.where(nxt_step < n_steps, cgrp[jnp.minimum(nxt_step, n_steps - 1)], -1)
    runs = jnp.concatenate([first.astype(jnp.int32), run_id % 2, nxt_group.astype(jnp.int32),
                            (first & (run_id == 0)).astype(jnp.int32)])

    def h_idx(i, s, *_):
        done = (s >= jnp.maximum(_[2][i] - 1, 1)).astype(jnp.int32)
        return (jnp.minimum(i + done, n_tiles - 1), 0)

    hbm = pl.BlockSpec(memory_space=pl.ANY)
    return pl.pallas_call(
        _moe_kernel,
        grid_spec=pltpu.PrefetchScalarGridSpec(
            num_scalar_prefetch=4,
            grid=(n_tiles, MOE_STEPS),
            in_specs=[pl.BlockSpec((MOE_TILE, H2X_W), h_idx), hbm, hbm, hbm],
            out_specs=pl.BlockSpec((None, MOE_TILE + 8, D_MODEL), lambda i, s, *_: (i, 0, 0)),
            scratch_shapes=[pltpu.VMEM((MOE_CHUNK, H2X_W), F32), pltpu.VMEM((MOE_CHUNK, H2X_W), F32),
                            pltpu.VMEM((MOE_CHUNK, D_MODEL), F32), pltpu.VMEM((MOE_CHUNK, D_MODEL), F32),
                            pltpu.VMEM((2, EXPERTS_PER_GROUP, D_MODEL, EXPERT_FF), BF16),
                            pltpu.VMEM((2, EXPERTS_PER_GROUP, D_MODEL, EXPERT_FF), BF16),
                            pltpu.VMEM((2, EXPERTS_PER_GROUP * EXPERT_FF, D_MODEL), BF16),
                            pltpu.SemaphoreType.DMA((2, 3))]),
        out_shape=jax.ShapeDtypeStruct((n_tiles, MOE_TILE + 8, D_MODEL), F32),
        compiler_params=_cparams(("arbitrary", "arbitrary")),
        name="moe_experts",
    )(perm, cgrp, nact, runs, h2x, wg, wu, wd)


def _final_kernel(y_ref, x1_ref, mods_ref, fg_ref, o_ref):
    gate2 = mods_ref[:, 5 * D_MODEL:6 * D_MODEL]
    x2 = x1_ref[...] + gate2 * y_ref[...]
    o_ref[...] = x2 * lax.rsqrt(jnp.mean(x2 * x2, axis=-1, keepdims=True) + EPS) * fg_ref[...]


def _final(y, x1, mods3, mod_row0, mod_tokens, fg):
    n = x1.shape[0]
    tm = ROW_TILE
    per_mod = mod_tokens // tm
    per_tile = MOE_TILE // tm
    return pl.pallas_call(
        _final_kernel,
        grid=(n // tm,),
        in_specs=[pl.BlockSpec((None, tm, D_MODEL), lambda j: (j // per_tile, j % per_tile, 0)),
                  pl.BlockSpec((tm, D_MODEL), lambda j: (j, 0)),
                  pl.BlockSpec((None, 1, 6 * D_MODEL), lambda j: (mod_row0 + j // per_mod, 0, 0)),
                  pl.BlockSpec((1, D_MODEL), lambda j: (0, 0))],
        out_specs=pl.BlockSpec((tm, D_MODEL), lambda j: (j, 0)),
        out_shape=jax.ShapeDtypeStruct((n, D_MODEL), F32),
        compiler_params=_cparams(("parallel",)),
        name="final_norm",
    )(y, x1, mods3, fg)


def _rope_tables(t):
    n_freq = QK_DIM // 4
    n_rows = t // GRID_W
    freqs = ROPE_BASE ** (-jnp.arange(n_freq, dtype=F32) / n_freq)
    ang_r = jnp.arange(n_rows, dtype=F32)[:, None] * freqs
    ang_c = jnp.arange(GRID_W, dtype=F32)[:, None] * freqs
    cr, sr, cc, sc = lax.optimization_barrier((jnp.cos(ang_r), jnp.sin(ang_r), jnp.cos(ang_c), jnp.sin(ang_c)))
    j = np.arange(LANES) % QK_DIM
    f_idx = j % n_freq
    by_row = (j < QK_DIM // 2)[None, None, :]
    first = ((j % (QK_DIM // 2)) < n_freq)[None, None, :]

    def table(r_small, c_small):
        return jnp.where(by_row, r_small[:, f_idx][:, None, :], c_small[:, f_idx][None, :, :])

    cos = table(cr, cc)
    sin = table(sr, sc)
    return (cos.reshape(t, LANES), jnp.where(first, -sin, 0.0).reshape(t, LANES),
            jnp.where(first, 0.0, sin).reshape(t, LANES))


def _layer(x, mods3, mod_row0, mod_tokens, rope_tabs, ctx_k, ctx_v, h0f, h0b, lw, layer):
    b, t, _ = x.shape
    n = b * t
    xf = x.reshape(n, D_MODEL)
    res = _inproj(xf, mods3, mod_row0, mod_tokens, t, lw["norm_mix_g"], lw["w_main"], lw["w_dt"], lw["alog"],
                  lw["dtb"], rope_tabs)
    q, kb, vt, z, xbc, cols, hrow = res[:7]
    if ctx_k is None:
        cache = None
        k3 = res[7].reshape(b, 1, t, ATT_HEADS, 2, LANES)[..., :QK_DIM]
        v3 = res[8].reshape(b, 1, t, ATT_HEADS, V_DIM)
    else:
        cache = (ctx_k.astype(BF16), jnp.swapaxes(ctx_v, 1, 2).astype(BF16))
        k3 = v3 = None
    lam0 = 0.8 - 0.6 * math.exp(-0.3 * layer)
    att = _attention(q.reshape(b, t, D_QK), kb.reshape(b, t, D_QK), vt, cache, lw["lamp"], lw["attn_subln_g"],
                     lam0, *_attn_tiling(b, t))
    ssm, hf, hb = _ssd(xbc, cols.reshape(b, t, LANES), hrow.reshape(b, t // CHUNK * HROWS, LANES),
                       z.reshape(b, t, D_SSM), h0f, h0b, lw["conv_w"], lw["conv_b"], lw["d_x"], lw["ssm_norm_g"])
    x1, h2x = _outproj(att.reshape(n, D_ATT), ssm.reshape(n, D_SSM), xf, mods3, mod_row0, mod_tokens,
                       lw["wo_att"], lw["wo_ssm"], lw["norm_ffn_g"], lw["w_router"], lw["b_router"])
    meta = _route(h2x)
    perm = meta[:, 0, :].reshape(-1)
    cgrp = meta[:, 1, :MOE_STEPS].reshape(-1)
    nact = meta[:, 2, 0]
    y = _moe(h2x, perm, cgrp, nact, lw["wg"], lw["wu"], lw["wd"])
    out = _final(y, x1, mods3, mod_row0, mod_tokens, lw["final_g"])
    return out.reshape(b, t, D_MODEL), k3, v3, hf, hb


def _pad_lanes(v, width=LANES):
    return jnp.pad(v, [(0, 0)] * (v.ndim - 1) + [(0, width - v.shape[-1])])


def kernel(x_prompt, x_sample, cache_k, cache_v, state_ssm_fwd, state_ssm_bwd, c, c_ctx, w_ada, b_ada, norm_mix_g, w_in, w_out, lambda_q1, lambda_k1, lambda_q2, lambda_k2, attn_subln_g, conv_w, conv_b, a_log_fwd, a_log_bwd, dt_bias_fwd, dt_bias_bwd, ssm_d, ssm_norm_g, norm_ffn_g, w_group_router, b_group_router, w_expert_router, b_expert_router, w_exp_gate, w_exp_up, w_exp_down, final_norm_g):
    depth = w_in.shape[0]
    assert depth == 1, "single trunk layer"
    bp, tp, _ = x_prompt.shape
    bs, ts, _ = x_sample.shape
    l = 0
    cond = jnp.concatenate([c_ctx[None], c], axis=0)
    condT = _pad_lanes(cond.T, 8)
    mods = _ada(condT, w_ada[l], b_ada[l][None])
    mods3 = mods.reshape(8, 1, 6 * D_MODEL)

    w_router = _pad_lanes(jnp.concatenate([w_group_router[l], w_expert_router[l]], axis=1))
    wr_hi = w_router.astype(BF16)
    wr_lo = (w_router - wr_hi.astype(F32)).astype(BF16)
    lw = dict(
        norm_mix_g=norm_mix_g[l][None],
        w_main=w_in[l].astype(BF16),
        w_dt=_pad_lanes(w_in[l][:, MAIN_COLS:]).astype(BF16),
        lamp=jnp.stack([lambda_q1[l], lambda_k1[l], lambda_q2[l], lambda_k2[l]]),
        attn_subln_g=attn_subln_g[l][None],
        conv_w=conv_w[l].reshape(CONV_W, N_COLBLK, LANES), conv_b=conv_b[l].reshape(N_COLBLK, LANES),
        alog=jnp.broadcast_to(jnp.concatenate([a_log_fwd[l], a_log_bwd[l]])[:, None], (2 * SSM_HEADS, CHUNK)),
        dtb=jnp.broadcast_to(jnp.concatenate([dt_bias_fwd[l], dt_bias_bwd[l]])[:, None], (2 * SSM_HEADS, CHUNK)),
        d_x=jnp.repeat(ssm_d[l], SSM_HEADDIM)[None], ssm_norm_g=ssm_norm_g[l][None],
        wo_att=w_out[l][:D_ATT].astype(BF16), wo_ssm=w_out[l][D_ATT:].astype(BF16),
        norm_ffn_g=norm_ffn_g[l][None],
        w_router=jnp.concatenate([wr_hi, wr_lo], axis=1),
        b_router=_pad_lanes(jnp.concatenate([b_group_router[l], b_expert_router[l]])[None]),
        wg=w_exp_gate[l].astype(BF16), wu=w_exp_up[l].astype(BF16), wd=w_exp_down[l].astype(BF16).reshape(N_EXPERTS * EXPERT_FF, D_MODEL),
        final_g=final_norm_g[None],
    )
    n_state = SSM_HEADS * SSM_HEADDIM
    yp, ck, cv, hf, hb = _layer(x_prompt, mods3, 0, bp * tp, None, None, None, None, None, lw, l)
    ys, _, _, _, _ = _layer(x_sample, mods3, 1, ts, _rope_tables(ts),
                            cache_k[:, l].reshape(bs, -1, D_QK), cache_v[:, l].reshape(bs, -1, D_ATT),
                            state_ssm_fwd[:, l].reshape(bs, n_state, D_STATE),
                            state_ssm_bwd[:, l].reshape(bs, n_state, D_STATE), lw, l)
    new_k, new_v = ck, cv
    new_hf = hf.reshape(bp, 1, SSM_HEADS, SSM_HEADDIM, D_STATE)
    new_hb = hb.reshape(bp, 1, SSM_HEADS, SSM_HEADDIM, D_STATE)
    return yp, ys, new_k, new_v, new_hf, new_hb
```

```python
import functools
import math

import numpy as np
import jax
import jax.numpy as jnp
from jax import lax
from jax.experimental import pallas as pl
from jax.experimental.pallas import tpu as pltpu

D_MODEL = 1024
GRID_W = 64
ATT_HEADS = 4
QK_DIM = 64
V_DIM = 128
D_QK = 512
D_ATT = 512
ROPE_BASE = 10000.0
D_SSM = 512
SSM_HEADDIM = 64
SSM_HEADS = 8
SSM_GROUPS = 2
D_STATE = 128
CONV_W = 5
CHUNK = 128
XBC_DIM = 1024
N_GROUPS = 4
EXPERTS_PER_GROUP = 4
N_EXPERTS = 16
EXPERT_FF = 256
EPS = 1e-6
MAIN_COLS = 2 * D_QK + D_ATT + D_SSM + XBC_DIM
H2X_W = D_MODEL + 128
MOE_TILE = 2048
MOE_CHUNK = 256
MOE_SLOTS = MOE_TILE // MOE_CHUNK + N_GROUPS
MOE_ROWS = MOE_SLOTS * MOE_CHUNK
MOE_STEPS = MOE_SLOTS + 1
META_ROWS = MOE_STEPS * MOE_CHUNK
OUT_PART = 512
ROW_TILE = 512
ADA_COLS = 1024
SSD_SEQS = 2
COL_CUM, COL_DT, COL_ECUM, COL_TOEND = 0, 16, 32, 48
HROWS = 32
LANES = 128
HALO = 8
VMEM_LIMIT = 56 * 1024 * 1024

LOG2E = math.log2(math.e)
SUM_ROWS = 16
F32 = jnp.float32
BF16 = jnp.bfloat16


def _cparams(sem):
    return pltpu.CompilerParams(dimension_semantics=sem, vmem_limit_bytes=VMEM_LIMIT)


def _sigmoid(x):
    return 1.0 / (1.0 + jnp.exp(-x))


def _silu(x):
    return x * _sigmoid(x)


def _ada_kernel(condT_ref, w_ref, b_ref, o_ref):
    s = _silu(condT_ref[...])
    w = w_ref[...]
    b = b_ref[...]
    o_ref[...] = jnp.zeros_like(o_ref)
    for r in range(3):
        o_ref[r:r + 1, :] = jnp.sum(w * s[:, r:r + 1], axis=0, keepdims=True) + b


def _ada(condT, w_ada, b_ada):
    bn = ADA_COLS
    n = w_ada.shape[1]
    return pl.pallas_call(
        _ada_kernel,
        grid=(n // bn,),
        in_specs=[pl.BlockSpec((D_MODEL, 8), lambda j: (0, 0)),
                  pl.BlockSpec((D_MODEL, bn), lambda j: (0, j)),
                  pl.BlockSpec((1, bn), lambda j: (0, j))],
        out_specs=pl.BlockSpec((8, bn), lambda j: (0, j)),
        out_shape=jax.ShapeDtypeStruct((8, n), F32),
        compiler_params=_cparams(("arbitrary",)),
        name="ada",
    )(condT, w_ada, b_ada)


def _split3(x):
    hi = x.astype(BF16)
    r1 = x - hi.astype(F32)
    mid = r1.astype(BF16)
    lo = (r1 - mid.astype(F32)).astype(BF16)
    return jnp.concatenate([hi, mid, lo], axis=-1)


def _scan_matrices():
    t = np.arange(CHUNK)
    pre = (t[:, None] <= t[None, :]).astype(np.float32)
    suf = (t[:, None] >= t[None, :]).astype(np.float32)
    return jnp.asarray(np.stack([np.concatenate([pre] * 3, axis=0), np.concatenate([suf] * 3, axis=0)]), BF16)


def _head_scalars(dt_raw, alog_ref, dtb_ref, scan_ref):
    nh2 = 2 * SSM_HEADS
    xv = dt_raw.T[0:nh2, :] + dtb_ref[...]
    dt = jnp.maximum(xv, 0.0) + jnp.log(1.0 + jnp.exp(-jnp.abs(xv)))
    la3 = _split3(dt * (-jnp.exp(alog_ref[...])))
    fwd = lax.broadcasted_iota(jnp.int32, (nh2, CHUNK), 0) < SSM_HEADS
    cum = jnp.where(fwd, jnp.dot(la3, scan_ref[0], preferred_element_type=F32),
                    jnp.dot(la3, scan_ref[1], preferred_element_type=F32))
    cum_end = jnp.where(fwd, cum[:, CHUNK - 1:CHUNK], cum[:, 0:1])
    packed = jnp.concatenate([cum, dt, jnp.exp(cum), jnp.exp(cum_end - cum),
                              jnp.zeros((LANES - 4 * nh2, CHUNK), F32)], axis=0)
    return packed.T, jnp.concatenate([cum, jnp.exp(cum_end)], axis=0)


def _inproj_kernel(rope, x_ref, mods_ref, g_ref, w_ref, wdt_ref, alog_ref, dtb_ref, scan_ref, *rest):
    if rope:
        cos_ref, sa_ref, sb_ref, q_ref, kb_ref, vt_ref, z_ref, xbc_ref, cols_ref, hrow_ref = rest
    else:
        q_ref, kb_ref, vt_ref, z_ref, xbc_ref, cols_ref, hrow_ref, kf_ref, vf_ref = rest
    x = x_ref[...]
    shift = mods_ref[:, 0:D_MODEL]
    scale = mods_ref[:, D_MODEL:2 * D_MODEL]
    y = x * lax.rsqrt(jnp.mean(x * x, axis=-1, keepdims=True) + EPS) * g_ref[...]
    h = (y * (1.0 + scale) + shift).astype(BF16)
    dt_raw = jnp.dot(h, wdt_ref[...], preferred_element_type=F32)
    for ci in range(x.shape[0] // CHUNK):
        cols, hrow = _head_scalars(dt_raw[ci * CHUNK:(ci + 1) * CHUNK, :], alog_ref, dtb_ref, scan_ref)
        cols_ref[ci * CHUNK:(ci + 1) * CHUNK, :] = cols
        hrow_ref[ci * HROWS:(ci + 1) * HROWS, :] = hrow
    r = jnp.dot(h, w_ref[...], preferred_element_type=F32)
    q = r[:, 0:D_QK]
    k = r[:, D_QK:2 * D_QK]
    if rope:
        cos = cos_ref[...]
        sa = sa_ref[...]
        sb = sb_ref[...]

        def rot(t):
            parts = []
            for hh in range(ATT_HEADS):
                th = t[:, hh * LANES:(hh + 1) * LANES]
                parts.append(th * cos + pltpu.roll(th, LANES - 16, 1) * sa + pltpu.roll(th, 16, 1) * sb)
            return jnp.concatenate(parts, axis=1)

        q = rot(q)
        k = rot(k)
    v = r[:, 2 * D_QK:2 * D_QK + D_ATT]
    q_ref[...] = q
    kb_ref[...] = k.astype(BF16)
    n_seq, _, t_seq = vt_ref.shape
    v_t = v.T.astype(BF16)
    for sq in range(n_seq):
        vt_ref[sq] = v_t[:, sq * t_seq:(sq + 1) * t_seq]
    if not rope:
        tm = k.shape[0]
        for hh in range(ATT_HEADS):
            k_h = k[:, hh * LANES:(hh + 1) * LANES]
            vf_ref[pl.ds(hh, tm, stride=ATT_HEADS), :] = v[:, hh * LANES:(hh + 1) * LANES]
            kf_ref[pl.ds(2 * hh, tm, stride=2 * ATT_HEADS), :] = k_h
            kf_ref[pl.ds(2 * hh + 1, tm, stride=2 * ATT_HEADS), :] = pltpu.roll(k_h, QK_DIM, 1)
    z_ref[...] = r[:, 2 * D_QK + D_ATT:2 * D_QK + D_ATT + D_SSM]
    x0 = 2 * D_QK + D_ATT + D_SSM
    for cb in range(XBC_DIM // LANES):
        for sq in range(n_seq):
            xbc_ref[sq, cb] = r[sq * t_seq:(sq + 1) * t_seq, x0 + cb * LANES:x0 + (cb + 1) * LANES]


def _inproj(x, mods3, mod_row0, mod_tokens, seq_len, g, w_main, w_dt, alog, dtb, rope_tabs):
    n = x.shape[0]
    tm = ROW_TILE
    assert seq_len % tm == 0 or tm % seq_len == 0
    per_seq = max(seq_len // tm, 1)
    n_seq = max(tm // seq_len, 1)
    t_seq = tm // n_seq
    per_mod = mod_tokens // tm
    rope = rope_tabs is not None
    in_specs = [pl.BlockSpec((tm, D_MODEL), lambda i: (i, 0)),
                pl.BlockSpec((None, 1, 6 * D_MODEL), lambda i: (mod_row0 + i // per_mod, 0, 0)),
                pl.BlockSpec((1, D_MODEL), lambda i: (0, 0)),
                pl.BlockSpec((D_MODEL, MAIN_COLS), lambda i: (0, 0)),
                pl.BlockSpec((D_MODEL, LANES), lambda i: (0, 0)),
                pl.BlockSpec((2 * SSM_HEADS, CHUNK), lambda i: (0, 0)),
                pl.BlockSpec((2 * SSM_HEADS, CHUNK), lambda i: (0, 0)),
                pl.BlockSpec((2, 3 * CHUNK, CHUNK), lambda i: (0, 0, 0))]
    args = [x, mods3, g, w_main, w_dt, alog, dtb, _scan_matrices()]
    if rope:
        tab_spec = pl.BlockSpec((tm, LANES), lambda i: (i % per_seq, 0))
        in_specs += [tab_spec] * 3
        args += list(rope_tabs)
    def rows(wd, dtype=F32):
        return pl.BlockSpec((tm, wd), lambda i: (i, 0)), jax.ShapeDtypeStruct((n, wd), dtype)

    hr = tm // CHUNK * HROWS
    outs = [rows(D_QK), rows(D_QK, BF16),
            (pl.BlockSpec((n_seq, D_ATT, t_seq), lambda i: (i // per_seq, 0, i % per_seq)),
             jax.ShapeDtypeStruct((n // seq_len, D_ATT, seq_len), BF16)),
            rows(D_SSM),
            (pl.BlockSpec((n_seq, XBC_DIM // LANES, t_seq, LANES), lambda i: (i // per_seq, 0, i % per_seq, 0)),
             jax.ShapeDtypeStruct((n // seq_len, XBC_DIM // LANES, seq_len, LANES), F32)),
            rows(LANES),
            (pl.BlockSpec((hr, LANES), lambda i: (i, 0)), jax.ShapeDtypeStruct((n // CHUNK * HROWS, LANES), F32))]
    if not rope:
        outs += [(pl.BlockSpec((tm * 2 * ATT_HEADS, LANES), lambda i: (i, 0)),
                  jax.ShapeDtypeStruct((n * 2 * ATT_HEADS, LANES), F32)),
                 (pl.BlockSpec((tm * ATT_HEADS, LANES), lambda i: (i, 0)),
                  jax.ShapeDtypeStruct((n * ATT_HEADS, LANES), F32))]
    return pl.pallas_call(
        functools.partial(_inproj_kernel, rope),
        grid=(n // tm,),
        in_specs=in_specs,
        out_specs=[o[0] for o in outs],
        out_shape=[o[1] for o in outs],
        compiler_params=_cparams(("parallel",)),
        name="inproj_rope" if rope else "inproj",
    )(*args)


def _attn_kernel(tk, lam0, has_cache, q_ref, k_ref, vt_ref, *rest):
    if has_cache:
        ck_ref, cvt_ref, lamp_ref, g_ref, o_ref = rest
    else:
        lamp_ref, g_ref, o_ref = rest
    tq = q_ref.shape[1]
    lp = lamp_ref[...]
    lam = (jnp.exp(jnp.sum(lp[0:1] * lp[1:2], axis=-1, keepdims=True))
           - jnp.exp(jnp.sum(lp[2:3] * lp[3:4], axis=-1, keepdims=True)) + lam0)
    for bb, hh in [(b_, h_) for b_ in range(q_ref.shape[0]) for h_ in range(q_ref.shape[2] // LANES)]:
        hs = slice(hh * LANES, (hh + 1) * LANES)
        chunks = [(k_ref, vt_ref, c * tk, tk) for c in range(k_ref.shape[1] // tk)]
        if has_cache:
            ck = min(tk, ck_ref.shape[1])
            chunks += [(ck_ref, cvt_ref, c * ck, ck) for c in range(ck_ref.shape[1] // ck)]
        q = q_ref[bb, :, hs] * (QK_DIM ** -0.5 * LOG2E)
        lane = lax.broadcasted_iota(jnp.int32, q.shape, 1)
        qq_t = jnp.concatenate([jnp.where(lane < QK_DIM, q, 0.0), jnp.where(lane >= QK_DIM, q, 0.0)],
                               axis=0).T.astype(BF16)

        def scores(chunk):
            kr, _, start, size = chunk
            return jnp.dot(kr[bb, start:start + size, hs], qq_t, preferred_element_type=F32)

        def update(s, chunk, m, acc):
            _, vr, start, size = chunk
            m_new = jnp.maximum(m, jnp.max(s, axis=0, keepdims=True))
            alpha = jnp.exp2(m - m_new)
            p = jnp.exp2(s - m_new).astype(BF16)
            v_ext = jnp.concatenate([vr[bb, hs, start:start + size], jnp.ones((SUM_ROWS, size), BF16)], axis=0)
            acc = alpha * acc + jnp.dot(v_ext, p, preferred_element_type=F32)
            return m_new, acc

        m = jnp.full((1, 2 * tq), -jnp.inf, F32)
        acc = jnp.zeros((V_DIM + SUM_ROWS, 2 * tq), F32)
        s = scores(chunks[0])
        for c, chunk in enumerate(chunks):
            s_next = scores(chunks[c + 1]) if c + 1 < len(chunks) else None
            m, acc = update(s, chunk, m, acc)
            s = s_next
        o = acc[0:V_DIM] / acc[V_DIM:V_DIM + 1]
        o = (o[:, 0:tq] - lam * o[:, tq:2 * tq]).T
        o = o * lax.rsqrt(jnp.mean(o * o, axis=-1, keepdims=True) + EPS)
        o_ref[bb, :, hs] = o * g_ref[...] * (1.0 - lam0)


def _attn_tiling(b, t):
    if t % 1024 == 0:
        return 1024, 512, 1, 1
    assert t % 256 == 0 and b % 2 == 0
    return 256, 256, ATT_HEADS, 2


def _attention(q, k, vt, cache, lamp, g, lam0, tq, tk, nh, nb):
    b, t, _ = q.shape
    wd = nh * LANES

    def kv_specs(length):
        return [pl.BlockSpec((nb, length, wd), lambda bi, h, i: (bi, 0, h)),
                pl.BlockSpec((nb, wd, length), lambda bi, h, i: (bi, h, 0))]

    in_specs = [pl.BlockSpec((nb, tq, wd), lambda bi, h, i: (bi, i, h))] + kv_specs(t)
    args = [q, k, vt]
    if cache is not None:
        assert cache[0].shape[1] % min(tk, cache[0].shape[1]) == 0
        in_specs += kv_specs(cache[0].shape[1])
        args += list(cache)
    in_specs += [pl.BlockSpec((4, QK_DIM), lambda bi, h, i: (0, 0)),
                 pl.BlockSpec((1, V_DIM), lambda bi, h, i: (0, 0))]
    return pl.pallas_call(
        functools.partial(_attn_kernel, tk, lam0, cache is not None),
        grid=(b // nb, ATT_HEADS // nh, t // tq),
        in_specs=in_specs,
        out_specs=pl.BlockSpec((nb, tq, wd), lambda bi, h, i: (bi, i, h)),
        out_shape=jax.ShapeDtypeStruct((b, t, D_ATT), F32),
        compiler_params=_cparams(("parallel", "parallel", "arbitrary")),
        name="diff_attn",
    )(*args, lamp, g)


N_COLBLK = XBC_DIM // LANES
def _expand_matrices():
    out = []
    for d in range(2):
        e = np.zeros((LANES, 3 * D_SSM), np.float32)
        for blk, lane0 in enumerate((COL_DT, COL_ECUM, COL_TOEND)):
            for h in range(SSM_HEADS):
                e[lane0 + d * SSM_HEADS + h,
                  blk * D_SSM + h * SSM_HEADDIM:blk * D_SSM + (h + 1) * SSM_HEADDIM] = 1.0
        out.append(np.concatenate([e, e, e], axis=0))
    return jnp.asarray(np.stack(out), BF16)


def _ssd_chunk(reverse, xs, bm, cm, state, hrow, cols, e_ref):
    d0 = SSM_HEADS if reverse else 0
    cum = hrow[d0:d0 + SSM_HEADS, :]
    dec = hrow[2 * SSM_HEADS + d0:2 * SSM_HEADS + d0 + SSM_HEADS, 0:1]
    row = lax.broadcasted_iota(jnp.int32, (CHUNK, LANES), 0)
    lane = lax.broadcasted_iota(jnp.int32, (CHUNK, LANES), 1)
    causal = (row <= lane) if reverse else (row >= lane)
    lane_g = lax.broadcasted_iota(jnp.int32, (CHUNK, 2 * LANES), 1)
    spread = jnp.dot(_split3(cols), e_ref[1 if reverse else 0], preferred_element_type=F32)
    xd = xs * spread[:, 0:D_SSM]
    xdw = (xd * spread[:, 2 * D_SSM:3 * D_SSM]).astype(BF16)
    xd = xd.astype(BF16)
    ecum_x = spread[:, D_SSM:2 * D_SSM]
    rep = SSM_HEADS // SSM_GROUPS
    y_parts = []
    new_state = []
    for g in range(SSM_GROUPS):
        bg = bm[:, g * D_STATE:(g + 1) * D_STATE]
        cg = cm[:, g * D_STATE:(g + 1) * D_STATE]
        cbt = lax.dot_general(cg, bg, (((1,), (1,)), ((), ())), preferred_element_type=F32)
        rows = slice(g * rep * SSM_HEADDIM, (g + 1) * rep * SSM_HEADDIM)
        st_g = state[rows, :]
        y_off = lax.dot_general(cg, st_g.astype(BF16), (((1,), (1,)), ((), ())),
                                preferred_element_type=F32)
        cst = lax.dot_general(xdw[:, rows], bg, (((0,), (0,)), ((), ())), preferred_element_type=F32)
        xd_g = xd[:, rows]
        scs = []
        blocks = []
        for hh in range(rep):
            h = g * rep + hh
            seg = cols[:, COL_CUM + d0 + h:COL_CUM + d0 + h + 1] - cum[h:h + 1, :]
            decay = jnp.exp(jnp.where(causal, seg, -jnp.inf))
            scs.append((cbt * decay).astype(BF16))
            blocks.append(jnp.where(lane_g // SSM_HEADDIM == hh, xd_g, jnp.zeros_like(xd_g)))
        y_diag = jnp.dot(jnp.concatenate(scs, axis=1), jnp.concatenate(blocks, axis=0),
                         preferred_element_type=F32)
        dec_rows = jnp.concatenate(
            [jnp.broadcast_to(dec[g * rep + hh:g * rep + hh + 1, :], (SSM_HEADDIM, D_STATE)) for hh in range(rep)],
            axis=0)
        new_state.append(st_g * dec_rows + cst)
        y_parts.append(y_diag + y_off * ecum_x[:, rows])
    return jnp.concatenate(y_parts, axis=1), jnp.concatenate(new_state, axis=0)


def _ssd_kernel(nc, has_h0, xc_ref, xp_ref, xn_ref, cols_ref, hrow_ref, z_ref, *rest):
    if has_h0:
        h0f_ref, h0b_ref = rest[:2]
        rest = rest[2:]
    cw_ref, cb_ref, d_ref, ng_ref, e_ref, y_ref, hf_ref, hb_ref, state_ref, slab_ref, xs_ref, bc_ref, yf_ref = rest
    ps = pl.program_id(1)
    c = pl.program_id(2)
    n_bc = SSM_GROUPS * D_STATE

    @pl.when((ps == 0) & (c == 0))
    def _():
        state_ref[...] = h0f_ref[...] if has_h0 else jnp.zeros_like(state_ref)

    @pl.when(ps == 0)
    def _():
        t0 = pl.multiple_of(c * CHUNK, CHUNK)
        for bb in range(state_ref.shape[0]):
            slab_ref[bb, :, 0:HALO, :] = jnp.where(c > 0, xp_ref[bb], 0.0)
            slab_ref[bb, :, HALO:HALO + CHUNK, :] = xc_ref[bb]
            slab_ref[bb, :, HALO + CHUNK:2 * HALO + CHUNK, :] = jnp.where(c < nc - 1, xn_ref[bb], 0.0)
            blocks = []
            for cb in range(N_COLBLK):
                conv = cb_ref[cb:cb + 1, :]
                for kk in range(CONV_W):
                    off = HALO - CONV_W // 2 + kk
                    conv = conv + slab_ref[bb, cb, off:off + CHUNK, :] * cw_ref[kk, cb:cb + 1, :]
                blocks.append(_silu(conv))
            xs = jnp.concatenate(blocks[0:D_SSM // LANES], axis=1)
            bc = jnp.concatenate(blocks[D_SSM // LANES:], axis=1).astype(BF16)
            y, state = _ssd_chunk(False, xs, bc[:, 0:n_bc], bc[:, n_bc:], state_ref[bb], hrow_ref[bb],
                                  cols_ref[bb], e_ref)
            xs_ref[bb, pl.ds(t0, CHUNK), :] = xs
            bc_ref[bb, pl.ds(t0, CHUNK), :] = bc
            yf_ref[bb, pl.ds(t0, CHUNK), :] = y
            state_ref[bb] = state

    @pl.when((ps == 0) & (c == nc - 1))
    def _():
        hf_ref[...] = state_ref[...]
        state_ref[...] = h0b_ref[...] if has_h0 else jnp.zeros_like(state_ref)

    @pl.when(ps == 1)
    def _():
        t0 = pl.multiple_of((nc - 1 - c) * CHUNK, CHUNK)
        for bb in range(state_ref.shape[0]):
            xs = xs_ref[bb, pl.ds(t0, CHUNK), :]
            bc = bc_ref[bb, pl.ds(t0, CHUNK), :]
            y, state = _ssd_chunk(True, xs, bc[:, 0:n_bc], bc[:, n_bc:], state_ref[bb], hrow_ref[bb],
                                  cols_ref[bb], e_ref)
            state_ref[bb] = state
            y = y + yf_ref[bb, pl.ds(t0, CHUNK), :] + xs * d_ref[...]
            y = y * _silu(z_ref[bb])
            y = y * lax.rsqrt(jnp.mean(y * y, axis=-1, keepdims=True) + EPS)
            y_ref[bb] = y * ng_ref[...]

    @pl.when((ps == 1) & (c == nc - 1))
    def _():
        hb_ref[...] = state_ref[...]


def _ssd(xbc, cols, hrow, z, h0f, h0b, conv_w, conv_b, d_x, norm_g):
    b, _, l, _ = xbc.shape
    has_h0 = h0f is not None
    nb = SSD_SEQS
    assert b % nb == 0
    nc = l // CHUNK
    hb = CHUNK // HALO
    n_state = SSM_HEADS * SSM_HEADDIM
    last = nc - 1

    def fwd_chunk(ps, ci):
        return jnp.where(ps == 0, ci, last)

    def any_chunk(ps, ci):
        return jnp.where(ps == 0, ci, last - ci)

    def bwd_chunk(ps, ci):
        return jnp.where(ps == 0, last, last - ci)

    const2 = lambda bi, ps, ci: (0, 0)
    state_spec = pl.BlockSpec((nb, n_state, D_STATE), lambda bi, ps, ci: (bi, 0, 0))
    in_specs = [pl.BlockSpec((nb, N_COLBLK, CHUNK, LANES), lambda bi, ps, ci: (bi, 0, fwd_chunk(ps, ci), 0)),
                pl.BlockSpec((nb, N_COLBLK, HALO, LANES),
                             lambda bi, ps, ci: (bi, 0, jnp.maximum(fwd_chunk(ps, ci) * hb - 1, 0), 0)),
                pl.BlockSpec((nb, N_COLBLK, HALO, LANES),
                             lambda bi, ps, ci: (bi, 0, jnp.minimum((fwd_chunk(ps, ci) + 1) * hb, l // HALO - 1), 0)),
                pl.BlockSpec((nb, CHUNK, LANES), lambda bi, ps, ci: (bi, any_chunk(ps, ci), 0)),
                pl.BlockSpec((nb, HROWS, LANES), lambda bi, ps, ci: (bi, any_chunk(ps, ci), 0)),
                pl.BlockSpec((nb, CHUNK, D_SSM), lambda bi, ps, ci: (bi, bwd_chunk(ps, ci), 0)),
                *([state_spec, state_spec] if has_h0 else []),
                pl.BlockSpec((CONV_W, N_COLBLK, LANES), lambda bi, ps, ci: (0, 0, 0)),
                pl.BlockSpec((N_COLBLK, LANES), const2),
                pl.BlockSpec((1, D_SSM), const2),
                pl.BlockSpec((1, D_SSM), const2),
                pl.BlockSpec((2, 3 * LANES, 3 * D_SSM), lambda bi, ps, ci: (0, 0, 0))]
    return pl.pallas_call(
        functools.partial(_ssd_kernel, nc, has_h0),
        grid=(b // nb, 2, nc),
        in_specs=in_specs,
        out_specs=[pl.BlockSpec((nb, CHUNK, D_SSM), lambda bi, ps, ci: (bi, bwd_chunk(ps, ci), 0)),
                   state_spec, state_spec],
        out_shape=[jax.ShapeDtypeStruct((b, l, D_SSM), F32),
                   jax.ShapeDtypeStruct((b, n_state, D_STATE), F32),
                   jax.ShapeDtypeStruct((b, n_state, D_STATE), F32)],
        scratch_shapes=[pltpu.VMEM((nb, n_state, D_STATE), F32),
                        pltpu.VMEM((nb, N_COLBLK, CHUNK + 2 * HALO, LANES), F32),
                        pltpu.VMEM((nb, l, D_SSM), F32),
                        pltpu.VMEM((nb, l, 2 * SSM_GROUPS * D_STATE), BF16),
                        pltpu.VMEM((nb, l, D_SSM), F32)],
        compiler_params=_cparams(("parallel", "arbitrary", "arbitrary")),
        name="ssd",
    )(xbc, xbc, xbc, cols, hrow, z, *([h0f, h0b] if has_h0 else []), conv_w, conv_b, d_x, norm_g,
      _expand_matrices())


def _outproj_kernel(att_ref, ssm_ref, x_ref, mods_ref, wa_ref, ws_ref, g_ref, wr_ref, br_ref,
                    x1_ref, h2x_ref):
    for part in range(x_ref.shape[0] // OUT_PART):
        rows = slice(part * OUT_PART, (part + 1) * OUT_PART)
        _outproj_rows(rows, att_ref, ssm_ref, x_ref, mods_ref, wa_ref, ws_ref, g_ref, wr_ref, br_ref,
                      x1_ref, h2x_ref)


def _outproj_rows(rows, att_ref, ssm_ref, x_ref, mods_ref, wa_ref, ws_ref, g_ref, wr_ref, br_ref,
                  x1_ref, h2x_ref):
    mix = (jnp.dot(att_ref[rows, :].astype(BF16), wa_ref[...], preferred_element_type=F32)
           + jnp.dot(ssm_ref[rows, :].astype(BF16), ws_ref[...], preferred_element_type=F32))
    gate1 = mods_ref[:, 2 * D_MODEL:3 * D_MODEL]
    shift2 = mods_ref[:, 3 * D_MODEL:4 * D_MODEL]
    scale2 = mods_ref[:, 4 * D_MODEL:5 * D_MODEL]
    x1 = x_ref[rows, :] + gate1 * mix
    x1_ref[rows, :] = x1
    y = x1 * lax.rsqrt(jnp.mean(x1 * x1, axis=-1, keepdims=True) + EPS) * g_ref[...]
    h2 = y * (1.0 + scale2) + shift2
    h_hi = h2.astype(BF16)
    h2x_ref[rows, 0:D_MODEL] = h2
    h_lo = (h2 - h_hi.astype(F32)).astype(BF16)
    both = jnp.dot(h_hi, wr_ref[...], preferred_element_type=F32)
    logits = (both[:, 0:LANES] + both[:, LANES:2 * LANES]
              + jnp.dot(h_lo, wr_ref[:, 0:LANES], preferred_element_type=F32)) + br_ref[...]
    lane = lax.broadcasted_iota(jnp.int32, logits.shape, 1).astype(F32)
    neg = -jnp.inf
    big = float(1 << 20)
    is_g = lane < N_GROUPS
    gl = jnp.where(is_g, logits, neg)
    gmax = jnp.max(gl, axis=-1, keepdims=True)
    g_idx = jnp.min(jnp.where(gl == gmax, lane, big), axis=-1, keepdims=True)
    p_g = 1.0 / jnp.sum(jnp.where(is_g, jnp.exp(gl - gmax), 0.0), axis=-1, keepdims=True)
    e_lo = N_GROUPS + g_idx * EXPERTS_PER_GROUP
    in_grp = (lane >= e_lo) & (lane < e_lo + EXPERTS_PER_GROUP)
    el = jnp.where(in_grp, logits, neg)
    m1 = jnp.max(el, axis=-1, keepdims=True)
    i1 = jnp.min(jnp.where(el == m1, lane, big), axis=-1, keepdims=True)
    el2 = jnp.where(lane == i1, neg, el)
    m2 = jnp.max(el2, axis=-1, keepdims=True)
    i2 = jnp.min(jnp.where(el2 == m2, lane, big), axis=-1, keepdims=True)
    e2 = jnp.exp(m2 - m1)
    w1 = p_g / (1.0 + e2)
    w2 = p_g * e2 / (1.0 + e2)
    slab = (jnp.where(lane == i1 - e_lo, w1, 0.0) + jnp.where(lane == i2 - e_lo, w2, 0.0)
            + jnp.where(lane == EXPERTS_PER_GROUP, g_idx, 0.0))
    h2x_ref[rows, D_MODEL:H2X_W] = slab


def _outproj(att, ssm, x, mods3, mod_row0, mod_tokens, wo_att, wo_ssm, g, w_router, b_router):
    n = x.shape[0]
    tm = 2 * OUT_PART
    per_mod = mod_tokens // tm
    return pl.pallas_call(
        _outproj_kernel,
        grid=(n // tm,),
        in_specs=[pl.BlockSpec((tm, D_ATT), lambda i: (i, 0)),
                  pl.BlockSpec((tm, D_SSM), lambda i: (i, 0)),
                  pl.BlockSpec((tm, D_MODEL), lambda i: (i, 0)),
                  pl.BlockSpec((None, 1, 6 * D_MODEL), lambda i: (mod_row0 + i // per_mod, 0, 0)),
                  pl.BlockSpec((D_ATT, D_MODEL), lambda i: (0, 0)),
                  pl.BlockSpec((D_SSM, D_MODEL), lambda i: (0, 0)),
                  pl.BlockSpec((1, D_MODEL), lambda i: (0, 0)),
                  pl.BlockSpec((D_MODEL, 2 * LANES), lambda i: (0, 0)),
                  pl.BlockSpec((1, LANES), lambda i: (0, 0))],
        out_specs=[pl.BlockSpec((tm, D_MODEL), lambda i: (i, 0)),
                   pl.BlockSpec((tm, H2X_W), lambda i: (i, 0))],
        out_shape=[jax.ShapeDtypeStruct((n, D_MODEL), F32),
                   jax.ShapeDtypeStruct((n, H2X_W), F32)],
        compiler_params=_cparams(("parallel",)),
        name="outproj_router",
    )(att, ssm, x, mods3, wo_att, wo_ssm, g, w_router, b_router)


def _route_kernel(slab_ref, meta_ref):
    t_n = MOE_TILE
    blk = LANES
    slab = slab_ref[...]
    lane = lax.broadcasted_iota(jnp.int32, (t_n, LANES), 1)
    gcol = jnp.sum(jnp.where(lane == EXPERTS_PER_GROUP, slab, 0.0), axis=-1, keepdims=True)
    member = (lane.astype(F32) == gcol) & (lane < N_GROUPS)
    a = jnp.where(member, 1.0, 0.0).astype(BF16)
    r_i = lax.broadcasted_iota(jnp.int32, (blk, blk), 0)
    c_i = lax.broadcasted_iota(jnp.int32, (blk, blk), 1)
    lower = jnp.where(c_i < r_i, 1.0, 0.0).astype(BF16)
    upper = jnp.where(r_i < c_i, 1.0, 0.0).astype(BF16)
    offs = jnp.zeros((1, LANES), F32)
    ranks = []
    for b in range(t_n // blk):
        ab = a[b * blk:(b + 1) * blk]
        rb = jnp.dot(lower, ab, preferred_element_type=F32)
        ranks.append(rb + offs)
        offs = offs + rb[blk - 1:blk] + ab[blk - 1:blk].astype(F32)
    rank = jnp.concatenate(ranks, axis=0)
    n_chunk = jnp.floor((offs + (MOE_CHUNK - 1)) * (1.0 / MOE_CHUNK))
    start = jnp.dot(jnp.broadcast_to(n_chunk, (8, LANES)).astype(BF16), upper,
                    preferred_element_type=F32)[0:1]
    end = start + n_chunk
    dest = jnp.sum(jnp.where(member, start * MOE_CHUNK + rank, 0.0), axis=-1, keepdims=True)
    tok = lax.broadcasted_iota(jnp.int32, (t_n, LANES), 0)
    digits = jnp.where(lane == 0, (tok // blk).astype(F32),
                       jnp.where(lane == 1, (tok % blk).astype(F32), jnp.where(lane == 2, 1.0, 0.0))).astype(BF16)
    sw = 512
    pieces = []
    for sc in range(MOE_ROWS // sw):
        s_id = (lax.broadcasted_iota(jnp.int32, (t_n, sw), 1) + sc * sw).astype(F32)
        hit = jnp.where(dest == s_id, 1.0, 0.0).astype(BF16)
        r = lax.dot_general(digits, hit, (((0,), (0,)), ((), ())), preferred_element_type=F32)
        tok_of = r[0:1] * blk + r[1:2]
        pieces.append(jnp.where(r[2:3] > 0.5, tok_of, float(t_n)))
    pieces.append(jnp.full((1, META_ROWS - MOE_ROWS), float(t_n), F32))
    perm = jnp.concatenate(pieces, axis=1)
    slot = lax.broadcasted_iota(jnp.int32, (1, META_ROWS), 1).astype(F32)
    lane1 = lax.broadcasted_iota(jnp.int32, (1, LANES), 1)
    cg = jnp.zeros((1, META_ROWS), F32)
    for g in range(N_GROUPS):
        end_g = jnp.sum(jnp.where(lane1 == g, end, 0.0), axis=-1, keepdims=True)
        cg = cg + jnp.where(slot >= end_g, 1.0, 0.0)
    n_act = jnp.broadcast_to(end_g, (1, META_ROWS))
    meta_ref[...] = jnp.concatenate([perm, cg, n_act, jnp.zeros((5, META_ROWS), F32)], axis=0).astype(jnp.int32)


def _route(h2x):
    n = h2x.shape[0]
    n_tiles = n // MOE_TILE
    return pl.pallas_call(
        _route_kernel,
        grid=(n_tiles,),
        in_specs=[pl.BlockSpec((MOE_TILE, LANES), lambda i: (i, D_MODEL // LANES))],
        out_specs=pl.BlockSpec((None, 8, META_ROWS), lambda i: (i, 0, 0)),
        out_shape=jax.ShapeDtypeStruct((n_tiles, 8, META_ROWS), jnp.int32),
        compiler_params=_cparams(("parallel",)),
        name="moe_route",
    )(h2x)


def _moe_kernel(perm_ref, cg_ref, nact_ref, run_ref, h_ref, wg_hbm, wu_hbm, wd_hbm, y_ref,
                hs0, hs1, ys0, ys1, wg_buf, wu_buf, wd_buf, w_sem):
    i = pl.program_id(0)
    s = pl.program_id(1)
    n_act = nact_ref[i]
    tile_base = i * META_ROWS
    hs = (hs0, hs1)
    ys = (ys0, ys1)
    n_steps = pl.num_programs(0) * MOE_STEPS
    f = i * MOE_STEPS + s
    slot = run_ref[n_steps + f]

    def weight_copies(g, sl):
        e0 = g * EXPERTS_PER_GROUP
        r0 = g * (EXPERTS_PER_GROUP * EXPERT_FF)
        return (pltpu.make_async_copy(wg_hbm.at[pl.ds(e0, EXPERTS_PER_GROUP)], wg_buf.at[sl], w_sem.at[sl, 0]),
                pltpu.make_async_copy(wu_hbm.at[pl.ds(e0, EXPERTS_PER_GROUP)], wu_buf.at[sl], w_sem.at[sl, 1]),
                pltpu.make_async_copy(wd_hbm.at[pl.ds(r0, EXPERTS_PER_GROUP * EXPERT_FF)], wd_buf.at[sl],
                                      w_sem.at[sl, 2]))

    @pl.when(run_ref[f] == 1)
    def _():
        g = jnp.minimum(cg_ref[f], N_GROUPS - 1)

        @pl.when(run_ref[3 * n_steps + f] == 1)
        def _():
            for cp in weight_copies(g, slot):
                cp.start()

        for cp in weight_copies(g, slot):
            cp.wait()
        nxt = run_ref[2 * n_steps + f]

        @pl.when(nxt >= 0)
        def _():
            for cp in weight_copies(nxt, 1 - slot):
                cp.start()

    def gather(chunk, dst, rows=range(MOE_CHUNK)):
        base = tile_base + chunk * MOE_CHUNK
        for r in rows:
            src = jnp.minimum(perm_ref[base + r], MOE_TILE - 1)
            dst[r:r + 1, :] = h_ref[pl.ds(src, 1), :]

    def scatter(chunk, src, rows=range(MOE_CHUNK)):
        base = tile_base + chunk * MOE_CHUNK
        for r in rows:
            y_ref[pl.ds(perm_ref[base + r], 1), :] = src[r:r + 1, :]

    def ffn(src, dst, before_expert=lambda e: None):
        hb = src[:, 0:D_MODEL].astype(BF16)
        cw = src[:, D_MODEL:H2X_W]
        hid = []
        for e in range(EXPERTS_PER_GROUP):
            before_expert(e)
            a = jnp.dot(hb, wg_buf[slot, e], preferred_element_type=F32)
            u = jnp.dot(hb, wu_buf[slot, e], preferred_element_type=F32)
            hid.append((_silu(a) * u * cw[:, e:e + 1]).astype(BF16))
        dst[...] = jnp.dot(jnp.concatenate(hid, axis=1), wd_buf[slot], preferred_element_type=F32)

    @pl.when(s == 0)
    def _():
        y_ref[MOE_TILE:MOE_TILE + 8, :] = jnp.zeros((8, D_MODEL), F32)
        ys1[...] = jnp.zeros_like(ys1)
        gather(0, hs0)

    for par in (0, 1):
        @pl.when((s < n_act) & (s % 2 == par))
        def _():
            quarter = MOE_CHUNK // EXPERTS_PER_GROUP

            def copies(e):
                rows = range(e * quarter, (e + 1) * quarter)
                scatter(jnp.maximum(s - 1, 0), ys[1 - par], rows)
                gather(s + 1, hs[1 - par], rows)

            ffn(hs[par], ys[par], copies)

        @pl.when((s == n_act) & (s % 2 == par))
        def _():
            scatter(s - 1, ys[1 - par])


def _moe(h2x, perm, cgrp, nact, wg, wu, wd):
    n = h2x.shape[0]
    n_tiles = n // MOE_TILE
    n_steps = n_tiles * MOE_STEPS

    step = jnp.arange(n_steps)
    active = (step % MOE_STEPS) < jnp.repeat(nact, MOE_STEPS)
    prev = jnp.concatenate([jnp.full((1,), -1, jnp.int32), cgrp[:-1]])
    first = active & ((step % MOE_STEPS == 0) | (cgrp != prev))
    run_id = jnp.cumsum(first.astype(jnp.int32)) - 1
    later_first = lax.cummin(jnp.where(first, step, n_steps), reverse=True)
    nxt_step = jnp.concatenate([later_first[1:], jnp.full((1,), n_steps, jnp.int32)])
    nxt_group = jnp.where(nxt_step < n_steps, cgrp[jnp.minimum(nxt_step, n_steps - 1)], -1)
    runs = jnp.concatenate([first.astype(jnp.int32), run_id % 2, nxt_group.astype(jnp.int32),
                            (first & (run_id == 0)).astype(jnp.int32)])

    def h_idx(i, s, *_):
        done = (s >= jnp.maximum(_[2][i] - 1, 1)).astype(jnp.int32)
        return (jnp.minimum(i + done, n_tiles - 1), 0)

    hbm = pl.BlockSpec(memory_space=pl.ANY)
    return pl.pallas_call(
        _moe_kernel,
        grid_spec=pltpu.PrefetchScalarGridSpec(
            num_scalar_prefetch=4,
            grid=(n_tiles, MOE_STEPS),
            in_specs=[pl.BlockSpec((MOE_TILE, H2X_W), h_idx), hbm, hbm, hbm],
            out_specs=pl.BlockSpec((None, MOE_TILE + 8, D_MODEL), lambda i, s, *_: (i, 0, 0)),
            scratch_shapes=[pltpu.VMEM((MOE_CHUNK, H2X_W), F32), pltpu.VMEM((MOE_CHUNK, H2X_W), F32),
                            pltpu.VMEM((MOE_CHUNK, D_MODEL), F32), pltpu.VMEM((MOE_CHUNK, D_MODEL), F32),
                            pltpu.VMEM((2, EXPERTS_PER_GROUP, D_MODEL, EXPERT_FF), BF16),
                            pltpu.VMEM((2, EXPERTS_PER_GROUP, D_MODEL, EXPERT_FF), BF16),
                            pltpu.VMEM((2, EXPERTS_PER_GROUP * EXPERT_FF, D_MODEL), BF16),
                            pltpu.SemaphoreType.DMA((2, 3))]),
        out_shape=jax.ShapeDtypeStruct((n_tiles, MOE_TILE + 8, D_MODEL), F32),
        compiler_params=_cparams(("arbitrary", "arbitrary")),
        name="moe_experts",
    )(perm, cgrp, nact, runs, h2x, wg, wu, wd)


def _final_kernel(y_ref, x1_ref, mods_ref, fg_ref, o_ref):
    gate2 = mods_ref[:, 5 * D_MODEL:6 * D_MODEL]
    x2 = x1_ref[...] + gate2 * y_ref[...]
    o_ref[...] = x2 * lax.rsqrt(jnp.mean(x2 * x2, axis=-1, keepdims=True) + EPS) * fg_ref[...]


def _final(y, x1, mods3, mod_row0, mod_tokens, fg):
    n = x1.shape[0]
    tm = ROW_TILE
    per_mod = mod_tokens // tm
    per_tile = MOE_TILE // tm
    return pl.pallas_call(
        _final_kernel,
        grid=(n // tm,),
        in_specs=[pl.BlockSpec((None, tm, D_MODEL), lambda j: (j // per_tile, j % per_tile, 0)),
                  pl.BlockSpec((tm, D_MODEL), lambda j: (j, 0)),
                  pl.BlockSpec((None, 1, 6 * D_MODEL), lambda j: (mod_row0 + j // per_mod, 0, 0)),
                  pl.BlockSpec((1, D_MODEL), lambda j: (0, 0))],
        out_specs=pl.BlockSpec((tm, D_MODEL), lambda j: (j, 0)),
        out_shape=jax.ShapeDtypeStruct((n, D_MODEL), F32),
        compiler_params=_cparams(("parallel",)),
        name="final_norm",
    )(y, x1, mods3, fg)


def _rope_tables(t):
    n_freq = QK_DIM // 4
    n_rows = t // GRID_W
    freqs = ROPE_BASE ** (-jnp.arange(n_freq, dtype=F32) / n_freq)
    ang_r = jnp.arange(n_rows, dtype=F32)[:, None] * freqs
    ang_c = jnp.arange(GRID_W, dtype=F32)[:, None] * freqs
    cr, sr, cc, sc = lax.optimization_barrier((jnp.cos(ang_r), jnp.sin(ang_r), jnp.cos(ang_c), jnp.sin(ang_c)))
    j = np.arange(LANES) % QK_DIM
    f_idx = j % n_freq
    by_row = (j < QK_DIM // 2)[None, None, :]
    first = ((j % (QK_DIM // 2)) < n_freq)[None, None, :]

    def table(r_small, c_small):
        return jnp.where(by_row, r_small[:, f_idx][:, None, :], c_small[:, f_idx][None, :, :])

    cos = table(cr, cc)
    sin = table(sr, sc)
    return (cos.reshape(t, LANES), jnp.where(first, -sin, 0.0).reshape(t, LANES),
            jnp.where(first, 0.0, sin).reshape(t, LANES))


def _layer(x, mods3, mod_row0, mod_tokens, rope_tabs, ctx_k, ctx_v, h0f, h0b, lw, layer):
    b, t, _ = x.shape
    n = b * t
    xf = x.reshape(n, D_MODEL)
    res = _inproj(xf, mods3, mod_row0, mod_tokens, t, lw["norm_mix_g"], lw["w_main"], lw["w_dt"], lw["alog"],
                  lw["dtb"], rope_tabs)
    q, kb, vt, z, xbc, cols, hrow = res[:7]
    if ctx_k is None:
        cache = None
        k3 = res[7].reshape(b, 1, t, ATT_HEADS, 2, LANES)[..., :QK_DIM]
        v3 = res[8].reshape(b, 1, t, ATT_HEADS, V_DIM)
    else:
        cache = (ctx_k.astype(BF16), jnp.swapaxes(ctx_v, 1, 2).astype(BF16))
        k3 = v3 = None
    lam0 = 0.8 - 0.6 * math.exp(-0.3 * layer)
    att = _attention(q.reshape(b, t, D_QK), kb.reshape(b, t, D_QK), vt, cache, lw["lamp"], lw["attn_subln_g"],
                     lam0, *_attn_tiling(b, t))
    ssm, hf, hb = _ssd(xbc, cols.reshape(b, t, LANES), hrow.reshape(b, t // CHUNK * HROWS, LANES),
                       z.reshape(b, t, D_SSM), h0f, h0b, lw["conv_w"], lw["conv_b"], lw["d_x"], lw["ssm_norm_g"])
    x1, h2x = _outproj(att.reshape(n, D_ATT), ssm.reshape(n, D_SSM), xf, mods3, mod_row0, mod_tokens,
                       lw["wo_att"], lw["wo_ssm"], lw["norm_ffn_g"], lw["w_router"], lw["b_router"])
    meta = _route(h2x)
    perm = meta[:, 0, :].reshape(-1)
    cgrp = meta[:, 1, :MOE_STEPS].reshape(-1)
    nact = meta[:, 2, 0]
    y = _moe(h2x, perm, cgrp, nact, lw["wg"], lw["wu"], lw["wd"])
    out = _final(y, x1, mods3, mod_row0, mod_tokens, lw["final_g"])
    return out.reshape(b, t, D_MODEL), k3, v3, hf, hb


def _pad_lanes(v, width=LANES):
    return jnp.pad(v, [(0, 0)] * (v.ndim - 1) + [(0, width - v.shape[-1])])


def kernel(x_prompt, x_sample, cache_k, cache_v, state_ssm_fwd, state_ssm_bwd, c, c_ctx, w_ada, b_ada, norm_mix_g, w_in, w_out, lambda_q1, lambda_k1, lambda_q2, lambda_k2, attn_subln_g, conv_w, conv_b, a_log_fwd, a_log_bwd, dt_bias_fwd, dt_bias_bwd, ssm_d, ssm_norm_g, norm_ffn_g, w_group_router, b_group_router, w_expert_router, b_expert_router, w_exp_gate, w_exp_up, w_exp_down, final_norm_g):
    depth = w_in.shape[0]
    assert depth == 1, "single trunk layer"
    bp, tp, _ = x_prompt.shape
    bs, ts, _ = x_sample.shape
    l = 0
    cond = jnp.concatenate([c_ctx[None], c], axis=0)
    condT = _pad_lanes(cond.T, 8)
    mods = _ada(condT, w_ada[l], b_ada[l][None])
    mods3 = mods.reshape(8, 1, 6 * D_MODEL)

    w_router = _pad_lanes(jnp.concatenate([w_group_router[l], w_expert_router[l]], axis=1))
    wr_hi = w_router.astype(BF16)
    wr_lo = (w_router - wr_hi.astype(F32)).astype(BF16)
    lw = dict(
        norm_mix_g=norm_mix_g[l][None],
        w_main=w_in[l].astype(BF16),
        w_dt=_pad_lanes(w_in[l][:, MAIN_COLS:]).astype(BF16),
        lamp=jnp.stack([lambda_q1[l], lambda_k1[l], lambda_q2[l], lambda_k2[l]]),
        attn_subln_g=attn_subln_g[l][None],
        conv_w=conv_w[l].reshape(CONV_W, N_COLBLK, LANES), conv_b=conv_b[l].reshape(N_COLBLK, LANES),
        alog=jnp.broadcast_to(jnp.concatenate([a_log_fwd[l], a_log_bwd[l]])[:, None], (2 * SSM_HEADS, CHUNK)),
        dtb=jnp.broadcast_to(jnp.concatenate([dt_bias_fwd[l], dt_bias_bwd[l]])[:, None], (2 * SSM_HEADS, CHUNK)),
        d_x=jnp.repeat(ssm_d[l], SSM_HEADDIM)[None], ssm_norm_g=ssm_norm_g[l][None],
        wo_att=w_out[l][:D_ATT].astype(BF16), wo_ssm=w_out[l][D_ATT:].astype(BF16),
        norm_ffn_g=norm_ffn_g[l][None],
        w_router=jnp.concatenate([wr_hi, wr_lo], axis=1),
        b_router=_pad_lanes(jnp.concatenate([b_group_router[l], b_expert_router[l]])[None]),
        wg=w_exp_gate[l].astype(BF16), wu=w_exp_up[l].astype(BF16), wd=w_exp_down[l].astype(BF16).reshape(N_EXPERTS * EXPERT_FF, D_MODEL),
        final_g=final_norm_g[None],
    )
    n_state = SSM_HEADS * SSM_HEADDIM
    yp, ck, cv, hf, hb = _layer(x_prompt, mods3, 0, bp * tp, None, None, None, None, None, lw, l)
    ys, _, _, _, _ = _layer(x_sample, mods3, 1, ts, _rope_tables(ts),
                            cache_k[:, l].reshape(bs, -1, D_QK), cache_v[:, l].reshape(bs, -1, D_ATT),
                            state_ssm_fwd[:, l].reshape(bs, n_state, D_STATE),
                            state_ssm_bwd[:, l].reshape(bs, n_state, D_STATE), lw, l)
    new_k, new_v = ck, cv
    new_hf = hf.reshape(bp, 1, SSM_HEADS, SSM_HEADDIM, D_STATE)
    new_hb = hb.reshape(bp, 1, SSM_HEADS, SSM_HEADDIM, D_STATE)
    return yp, ys, new_k, new_v, new_hf, new_hb
```

```python
import functools
import math

import numpy as np
import jax
import jax.numpy as jnp
from jax import lax
from jax.experimental import pallas as pl
from jax.experimental.pallas import tpu as pltpu

D_MODEL = 1024
GRID_W = 64
ATT_HEADS = 4
QK_DIM = 64
V_DIM = 128
D_QK = 512
D_ATT = 512
ROPE_BASE = 10000.0
D_SSM = 512
SSM_HEADDIM = 64
SSM_HEADS = 8
SSM_GROUPS = 2
D_STATE = 128
CONV_W = 5
CHUNK = 128
XBC_DIM = 1024
N_GROUPS = 4
EXPERTS_PER_GROUP = 4
N_EXPERTS = 16
EXPERT_FF = 256
EPS = 1e-6
MAIN_COLS = 2 * D_QK + D_ATT + D_SSM + XBC_DIM
H2X_W = D_MODEL + 128
MOE_TILE = 2048
MOE_CHUNK = 256
MOE_SLOTS = MOE_TILE // MOE_CHUNK + N_GROUPS
MOE_ROWS = MOE_SLOTS * MOE_CHUNK
MOE_STEPS = MOE_SLOTS + 1
META_ROWS = MOE_STEPS * MOE_CHUNK
OUT_PART = 512
ROW_TILE = 512
ADA_COLS = 1024
SSD_SEQS = 2
COL_CUM, COL_DT, COL_ECUM, COL_TOEND = 0, 16, 32, 48
HROWS = 32
LANES = 128
HALO = 8
VMEM_LIMIT = 56 * 1024 * 1024

LOG2E = math.log2(math.e)
SUM_ROWS = 16
F32 = jnp.float32
BF16 = jnp.bfloat16


def _cparams(sem):
    return pltpu.CompilerParams(dimension_semantics=sem, vmem_limit_bytes=VMEM_LIMIT)


def _sigmoid(x):
    return 1.0 / (1.0 + jnp.exp(-x))


def _silu(x):
    return x * _sigmoid(x)


def _ada_kernel(condT_ref, w_ref, b_ref, o_ref):
    s = _silu(condT_ref[...])
    w = w_ref[...]
    b = b_ref[...]
    o_ref[...] = jnp.zeros_like(o_ref)
    for r in range(3):
        o_ref[r:r + 1, :] = jnp.sum(w * s[:, r:r + 1], axis=0, keepdims=True) + b


def _ada(condT, w_ada, b_ada):
    bn = ADA_COLS
    n = w_ada.shape[1]
    return pl.pallas_call(
        _ada_kernel,
        grid=(n // bn,),
        in_specs=[pl.BlockSpec((D_MODEL, 8), lambda j: (0, 0)),
                  pl.BlockSpec((D_MODEL, bn), lambda j: (0, j)),
                  pl.BlockSpec((1, bn), lambda j: (0, j))],
        out_specs=pl.BlockSpec((8, bn), lambda j: (0, j)),
        out_shape=jax.ShapeDtypeStruct((8, n), F32),
        compiler_params=_cparams(("arbitrary",)),
        name="ada",
    )(condT, w_ada, b_ada)


def _split3(x):
    hi = x.astype(BF16)
    r1 = x - hi.astype(F32)
    mid = r1.astype(BF16)
    lo = (r1 - mid.astype(F32)).astype(BF16)
    return jnp.concatenate([hi, mid, lo], axis=-1)


def _scan_matrices():
    t = np.arange(CHUNK)
    pre = (t[:, None] <= t[None, :]).astype(np.float32)
    suf = (t[:, None] >= t[None, :]).astype(np.float32)
    return jnp.asarray(np.stack([np.concatenate([pre] * 3, axis=0), np.concatenate([suf] * 3, axis=0)]), BF16)


def _head_scalars(dt_raw, alog_ref, dtb_ref, scan_ref):
    nh2 = 2 * SSM_HEADS
    xv = dt_raw.T[0:nh2, :] + dtb_ref[...]
    dt = jnp.maximum(xv, 0.0) + jnp.log(1.0 + jnp.exp(-jnp.abs(xv)))
    la3 = _split3(dt * (-jnp.exp(alog_ref[...])))
    fwd = lax.broadcasted_iota(jnp.int32, (nh2, CHUNK), 0) < SSM_HEADS
    cum = jnp.where(fwd, jnp.dot(la3, scan_ref[0], preferred_element_type=F32),
                    jnp.dot(la3, scan_ref[1], preferred_element_type=F32))
    cum_end = jnp.where(fwd, cum[:, CHUNK - 1:CHUNK], cum[:, 0:1])
    packed = jnp.concatenate([cum, dt, jnp.exp(cum), jnp.exp(cum_end - cum),
                              jnp.zeros((LANES - 4 * nh2, CHUNK), F32)], axis=0)
    return packed.T, jnp.concatenate([cum, jnp.exp(cum_end)], axis=0)


def _inproj_kernel(rope, x_ref, mods_ref, g_ref, w_ref, wdt_ref, alog_ref, dtb_ref, scan_ref, *rest):
    if rope:
        cos_ref, sa_ref, sb_ref, q_ref, kb_ref, vt_ref, z_ref, xbc_ref, cols_ref, hrow_ref = rest
    else:
        q_ref, kb_ref, vt_ref, z_ref, xbc_ref, cols_ref, hrow_ref, kf_ref, vf_ref = rest
    x = x_ref[...]
    shift = mods_ref[:, 0:D_MODEL]
    scale = mods_ref[:, D_MODEL:2 * D_MODEL]
    y = x * lax.rsqrt(jnp.mean(x * x, axis=-1, keepdims=True) + EPS) * g_ref[...]
    h = (y * (1.0 + scale) + shift).astype(BF16)
    dt_raw = jnp.dot(h, wdt_ref[...], preferred_element_type=F32)
    for ci in range(x.shape[0] // CHUNK):
        cols, hrow = _head_scalars(dt_raw[ci * CHUNK:(ci + 1) * CHUNK, :], alog_ref, dtb_ref, scan_ref)
        cols_ref[ci * CHUNK:(ci + 1) * CHUNK, :] = cols
        hrow_ref[ci * HROWS:(ci + 1) * HROWS, :] = hrow
    r = jnp.dot(h, w_ref[...], preferred_element_type=F32)
    q = r[:, 0:D_QK]
    k = r[:, D_QK:2 * D_QK]
    if rope:
        cos = cos_ref[...]
        sa = sa_ref[...]
        sb = sb_ref[...]

        def rot(t):
            parts = []
            for hh in range(ATT_HEADS):
                th = t[:, hh * LANES:(hh + 1) * LANES]
                parts.append(th * cos + pltpu.roll(th, LANES - 16, 1) * sa + pltpu.roll(th, 16, 1) * sb)
            return jnp.concatenate(parts, axis=1)

        q = rot(q)
        k = rot(k)
    v = r[:, 2 * D_QK:2 * D_QK + D_ATT]
    q_ref[...] = q
    kb_ref[...] = k.astype(BF16)
    n_seq, _, t_seq = vt_ref.shape
    v_t = v.T.astype(BF16)
    for sq in range(n_seq):
        vt_ref[sq] = v_t[:, sq * t_seq:(sq + 1) * t_seq]
    if not rope:
        tm = k.shape[0]
        for hh in range(ATT_HEADS):
            k_h = k[:, hh * LANES:(hh + 1) * LANES]
            vf_ref[pl.ds(hh, tm, stride=ATT_HEADS), :] = v[:, hh * LANES:(hh + 1) * LANES]
            kf_ref[pl.ds(2 * hh, tm, stride=2 * ATT_HEADS), :] = k_h
            kf_ref[pl.ds(2 * hh + 1, tm, stride=2 * ATT_HEADS), :] = pltpu.roll(k_h, QK_DIM, 1)
    z_ref[...] = r[:, 2 * D_QK + D_ATT:2 * D_QK + D_ATT + D_SSM]
    x0 = 2 * D_QK + D_ATT + D_SSM
    for cb in range(XBC_DIM // LANES):
        for sq in range(n_seq):
            xbc_ref[sq, cb] = r[sq * t_seq:(sq + 1) * t_seq, x0 + cb * LANES:x0 + (cb + 1) * LANES]


def _inproj(x, mods3, mod_row0, mod_tokens, seq_len, g, w_main, w_dt, alog, dtb, rope_tabs):
    n = x.shape[0]
    tm = ROW_TILE
    assert seq_len % tm == 0 or tm % seq_len == 0
    per_seq = max(seq_len // tm, 1)
    n_seq = max(tm // seq_len, 1)
    t_seq = tm // n_seq
    per_mod = mod_tokens // tm
    rope = rope_tabs is not None
    in_specs = [pl.BlockSpec((tm, D_MODEL), lambda i: (i, 0)),
                pl.BlockSpec((None, 1, 6 * D_MODEL), lambda i: (mod_row0 + i // per_mod, 0, 0)),
                pl.BlockSpec((1, D_MODEL), lambda i: (0, 0)),
                pl.BlockSpec((D_MODEL, MAIN_COLS), lambda i: (0, 0)),
                pl.BlockSpec((D_MODEL, LANES), lambda i: (0, 0)),
                pl.BlockSpec((2 * SSM_HEADS, CHUNK), lambda i: (0, 0)),
                pl.BlockSpec((2 * SSM_HEADS, CHUNK), lambda i: (0, 0)),
                pl.BlockSpec((2, 3 * CHUNK, CHUNK), lambda i: (0, 0, 0))]
    args = [x, mods3, g, w_main, w_dt, alog, dtb, _scan_matrices()]
    if rope:
        tab_spec = pl.BlockSpec((tm, LANES), lambda i: (i % per_seq, 0))
        in_specs += [tab_spec] * 3
        args += list(rope_tabs)
    def rows(wd, dtype=F32):
        return pl.BlockSpec((tm, wd), lambda i: (i, 0)), jax.ShapeDtypeStruct((n, wd), dtype)

    hr = tm // CHUNK * HROWS
    outs = [rows(D_QK), rows(D_QK, BF16),
            (pl.BlockSpec((n_seq, D_ATT, t_seq), lambda i: (i // per_seq, 0, i % per_seq)),
             jax.ShapeDtypeStruct((n // seq_len, D_ATT, seq_len), BF16)),
            rows(D_SSM),
            (pl.BlockSpec((n_seq, XBC_DIM // LANES, t_seq, LANES), lambda i: (i // per_seq, 0, i % per_seq, 0)),
             jax.ShapeDtypeStruct((n // seq_len, XBC_DIM // LANES, seq_len, LANES), F32)),
            rows(LANES),
            (pl.BlockSpec((hr, LANES), lambda i: (i, 0)), jax.ShapeDtypeStruct((n // CHUNK * HROWS, LANES), F32))]
    if not rope:
        outs += [(pl.BlockSpec((tm * 2 * ATT_HEADS, LANES), lambda i: (i, 0)),
                  jax.ShapeDtypeStruct((n * 2 * ATT_HEADS, LANES), F32)),
                 (pl.BlockSpec((tm * ATT_HEADS, LANES), lambda i: (i, 0)),
                  jax.ShapeDtypeStruct((n * ATT_HEADS, LANES), F32))]
    return pl.pallas_call(
        functools.partial(_inproj_kernel, rope),
        grid=(n // tm,),
        in_specs=in_specs,
        out_specs=[o[0] for o in outs],
        out_shape=[o[1] for o in outs],
        compiler_params=_cparams(("parallel",)),
        name="inproj_rope" if rope else "inproj",
    )(*args)


def _attn_kernel(tk, lam0, has_cache, q_ref, k_ref, vt_ref, *rest):
    if has_cache:
        ck_ref, cvt_ref, lamp_ref, g_ref, o_ref = rest
    else:
        lamp_ref, g_ref, o_ref = rest
    tq = q_ref.shape[1]
    lp = lamp_ref[...]
    lam = (jnp.exp(jnp.sum(lp[0:1] * lp[1:2], axis=-1, keepdims=True))
           - jnp.exp(jnp.sum(lp[2:3] * lp[3:4], axis=-1, keepdims=True)) + lam0)
    for bb, hh in [(b_, h_) for b_ in range(q_ref.shape[0]) for h_ in range(q_ref.shape[2] // LANES)]:
        hs = slice(hh * LANES, (hh + 1) * LANES)
        chunks = [(k_ref, vt_ref, c * tk, tk) for c in range(k_ref.shape[1] // tk)]
        if has_cache:
            ck = min(tk, ck_ref.shape[1])
            chunks += [(ck_ref, cvt_ref, c * ck, ck) for c in range(ck_ref.shape[1] // ck)]
        q = q_ref[bb, :, hs] * (QK_DIM ** -0.5 * LOG2E)
        lane = lax.broadcasted_iota(jnp.int32, q.shape, 1)
        qq_t = jnp.concatenate([jnp.where(lane < QK_DIM, q, 0.0), jnp.where(lane >= QK_DIM, q, 0.0)],
                               axis=0).T.astype(BF16)

        def scores(chunk):
            kr, _, start, size = chunk
            return jnp.dot(kr[bb, start:start + size, hs], qq_t, preferred_element_type=F32)

        def update(s, chunk, m, acc):
            _, vr, start, size = chunk
            m_new = jnp.maximum(m, jnp.max(s, axis=0, keepdims=True))
            alpha = jnp.exp2(m - m_new)
            p = jnp.exp2(s - m_new).astype(BF16)
            v_ext = jnp.concatenate([vr[bb, hs, start:start + size], jnp.ones((SUM_ROWS, size), BF16)], axis=0)
            acc = alpha * acc + jnp.dot(v_ext, p, preferred_element_type=F32)
            return m_new, acc

        m = jnp.full((1, 2 * tq), -jnp.inf, F32)
        acc = jnp.zeros((V_DIM + SUM_ROWS, 2 * tq), F32)
        s = scores(chunks[0])
        for c, chunk in enumerate(chunks):
            s_next = scores(chunks[c + 1]) if c + 1 < len(chunks) else None
            m, acc = update(s, chunk, m, acc)
            s = s_next
        o = acc[0:V_DIM] / acc[V_DIM:V_DIM + 1]
        o = (o[:, 0:tq] - lam * o[:, tq:2 * tq]).T
        o = o * lax.rsqrt(jnp.mean(o * o, axis=-1, keepdims=True) + EPS)
        o_ref[bb, :, hs] = o * g_ref[...] * (1.0 - lam0)


def _attn_tiling(b, t):
    if t % 1024 == 0:
        return 1024, 512, 1, 1
    assert t % 256 == 0 and b % 2 == 0
    return 256, 256, ATT_HEADS, 2


def _attention(q, k, vt, cache, lamp, g, lam0, tq, tk, nh, nb):
    b, t, _ = q.shape
    wd = nh * LANES

    def kv_specs(length):
        return [pl.BlockSpec((nb, length, wd), lambda bi, h, i: (bi, 0, h)),
                pl.BlockSpec((nb, wd, length), lambda bi, h, i: (bi, h, 0))]

    in_specs = [pl.BlockSpec((nb, tq, wd), lambda bi, h, i: (bi, i, h))] + kv_specs(t)
    args = [q, k, vt]
    if cache is not None:
        assert cache[0].shape[1] % min(tk, cache[0].shape[1]) == 0
        in_specs += kv_specs(cache[0].shape[1])
        args += list(cache)
    in_specs += [pl.BlockSpec((4, QK_DIM), lambda bi, h, i: (0, 0)),
                 pl.BlockSpec((1, V_DIM), lambda bi, h, i: (0, 0))]
    return pl.pallas_call(
        functools.partial(_attn_kernel, tk, lam0, cache is not None),
        grid=(b // nb, ATT_HEADS // nh, t // tq),
        in_specs=in_specs,
        out_specs=pl.BlockSpec((nb, tq, wd), lambda bi, h, i: (bi, i, h)),
        out_shape=jax.ShapeDtypeStruct((b, t, D_ATT), F32),
        compiler_params=_cparams(("parallel", "parallel", "arbitrary")),
        name="diff_attn",
    )(*args, lamp, g)


N_COLBLK = XBC_DIM // LANES
def _expand_matrices():
    out = []
    for d in range(2):
        e = np.zeros((LANES, 3 * D_SSM), np.float32)
        for blk, lane0 in enumerate((COL_DT, COL_ECUM, COL_TOEND)):
            for h in range(SSM_HEADS):
                e[lane0 + d * SSM_HEADS + h,
                  blk * D_SSM + h * SSM_HEADDIM:blk * D_SSM + (h + 1) * SSM_HEADDIM] = 1.0
        out.append(np.concatenate([e, e, e], axis=0))
    return jnp.asarray(np.stack(out), BF16)


def _ssd_chunk(reverse, xs, bm, cm, state, hrow, cols, e_ref):
    d0 = SSM_HEADS if reverse else 0
    cum = hrow[d0:d0 + SSM_HEADS, :]
    dec = hrow[2 * SSM_HEADS + d0:2 * SSM_HEADS + d0 + SSM_HEADS, 0:1]
    row = lax.broadcasted_iota(jnp.int32, (CHUNK, LANES), 0)
    lane = lax.broadcasted_iota(jnp.int32, (CHUNK, LANES), 1)
    causal = (row <= lane) if reverse else (row >= lane)
    lane_g = lax.broadcasted_iota(jnp.int32, (CHUNK, 2 * LANES), 1)
    spread = jnp.dot(_split3(cols), e_ref[1 if reverse else 0], preferred_element_type=F32)
    xd = xs * spread[:, 0:D_SSM]
    xdw = (xd * spread[:, 2 * D_SSM:3 * D_SSM]).astype(BF16)
    xd = xd.astype(BF16)
    ecum_x = spread[:, D_SSM:2 * D_SSM]
    rep = SSM_HEADS // SSM_GROUPS
    y_parts = []
    new_state = []
    for g in range(SSM_GROUPS):
        bg = bm[:, g * D_STATE:(g + 1) * D_STATE]
        cg = cm[:, g * D_STATE:(g + 1) * D_STATE]
        cbt = lax.dot_general(cg, bg, (((1,), (1,)), ((), ())), preferred_element_type=F32)
        rows = slice(g * rep * SSM_HEADDIM, (g + 1) * rep * SSM_HEADDIM)
        st_g = state[rows, :]
        y_off = lax.dot_general(cg, st_g.astype(BF16), (((1,), (1,)), ((), ())),
                                preferred_element_type=F32)
        cst = lax.dot_general(xdw[:, rows], bg, (((0,), (0,)), ((), ())), preferred_element_type=F32)
        xd_g = xd[:, rows]
        scs = []
        blocks = []
        for hh in range(rep):
            h = g * rep + hh
            seg = cols[:, COL_CUM + d0 + h:COL_CUM + d0 + h + 1] - cum[h:h + 1, :]
            decay = jnp.exp(jnp.where(causal, seg, -jnp.inf))
            scs.append((cbt * decay).astype(BF16))
            blocks.append(jnp.where(lane_g // SSM_HEADDIM == hh, xd_g, jnp.zeros_like(xd_g)))
        y_diag = jnp.dot(jnp.concatenate(scs, axis=1), jnp.concatenate(blocks, axis=0),
                         preferred_element_type=F32)
        dec_rows = jnp.concatenate(
            [jnp.broadcast_to(dec[g * rep + hh:g * rep + hh + 1, :], (SSM_HEADDIM, D_STATE)) for hh in range(rep)],
            axis=0)
        new_state.append(st_g * dec_rows + cst)
        y_parts.append(y_diag + y_off * ecum_x[:, rows])
    return jnp.concatenate(y_parts, axis=1), jnp.concatenate(new_state, axis=0)


def _ssd_kernel(nc, has_h0, xc_ref, xp_ref, xn_ref, cols_ref, hrow_ref, z_ref, *rest):
    if has_h0:
        h0f_ref, h0b_ref = rest[:2]
        rest = rest[2:]
    cw_ref, cb_ref, d_ref, ng_ref, e_ref, y_ref, hf_ref, hb_ref, state_ref, slab_ref, xs_ref, bc_ref, yf_ref = rest
    ps = pl.program_id(1)
    c = pl.program_id(2)
    n_bc = SSM_GROUPS * D_STATE

    @pl.when((ps == 0) & (c == 0))
    def _():
        state_ref[...] = h0f_ref[...] if has_h0 else jnp.zeros_like(state_ref)

    @pl.when(ps == 0)
    def _():
        t0 = pl.multiple_of(c * CHUNK, CHUNK)
        for bb in range(state_ref.shape[0]):
            slab_ref[bb, :, 0:HALO, :] = jnp.where(c > 0, xp_ref[bb], 0.0)
            slab_ref[bb, :, HALO:HALO + CHUNK, :] = xc_ref[bb]
            slab_ref[bb, :, HALO + CHUNK:2 * HALO + CHUNK, :] = jnp.where(c < nc - 1, xn_ref[bb], 0.0)
            blocks = []
            for cb in range(N_COLBLK):
                conv = cb_ref[cb:cb + 1, :]
                for kk in range(CONV_W):
                    off = HALO - CONV_W // 2 + kk
                    conv = conv + slab_ref[bb, cb, off:off + CHUNK, :] * cw_ref[kk, cb:cb + 1, :]
                blocks.append(_silu(conv))
            xs = jnp.concatenate(blocks[0:D_SSM // LANES], axis=1)
            bc = jnp.concatenate(blocks[D_SSM // LANES:], axis=1).astype(BF16)
            y, state = _ssd_chunk(False, xs, bc[:, 0:n_bc], bc[:, n_bc:], state_ref[bb], hrow_ref[bb],
                                  cols_ref[bb], e_ref)
            xs_ref[bb, pl.ds(t0, CHUNK), :] = xs
            bc_ref[bb, pl.ds(t0, CHUNK), :] = bc
            yf_ref[bb, pl.ds(t0, CHUNK), :] = y
            state_ref[bb] = state

    @pl.when((ps == 0) & (c == nc - 1))
    def _():
        hf_ref[...] = state_ref[...]
        state_ref[...] = h0b_ref[...] if has_h0 else jnp.zeros_like(state_ref)

    @pl.when(ps == 1)
    def _():
        t0 = pl.multiple_of((nc - 1 - c) * CHUNK, CHUNK)
        for bb in range(state_ref.shape[0]):
            xs = xs_ref[bb, pl.ds(t0, CHUNK), :]
            bc = bc_ref[bb, pl.ds(t0, CHUNK), :]
            y, state = _ssd_chunk(True, xs, bc[:, 0:n_bc], bc[:, n_bc:], state_ref[bb], hrow_ref[bb],
                                  cols_ref[bb], e_ref)
            state_ref[bb] = state
            y = y + yf_ref[bb, pl.ds(t0, CHUNK), :] + xs * d_ref[...]
            y = y * _silu(z_ref[bb])
            y = y * lax.rsqrt(jnp.mean(y * y, axis=-1, keepdims=True) + EPS)
            y_ref[bb] = y * ng_ref[...]

    @pl.when((ps == 1) & (c == nc - 1))
    def _():
        hb_ref[...] = state_ref[...]


def _ssd(xbc, cols, hrow, z, h0f, h0b, conv_w, conv_b, d_x, norm_g):
    b, _, l, _ = xbc.shape
    has_h0 = h0f is not None
    nb = SSD_SEQS
    assert b % nb == 0
    nc = l // CHUNK
    hb = CHUNK // HALO
    n_state = SSM_HEADS * SSM_HEADDIM
    last = nc - 1

    def fwd_chunk(ps, ci):
        return jnp.where(ps == 0, ci, last)

    def any_chunk(ps, ci):
        return jnp.where(ps == 0, ci, last - ci)

    def bwd_chunk(ps, ci):
        return jnp.where(ps == 0, last, last - ci)

    const2 = lambda bi, ps, ci: (0, 0)
    state_spec = pl.BlockSpec((nb, n_state, D_STATE), lambda bi, ps, ci: (bi, 0, 0))
    in_specs = [pl.BlockSpec((nb, N_COLBLK, CHUNK, LANES), lambda bi, ps, ci: (bi, 0, fwd_chunk(ps, ci), 0)),
                pl.BlockSpec((nb, N_COLBLK, HALO, LANES),
                             lambda bi, ps, ci: (bi, 0, jnp.maximum(fwd_chunk(ps, ci) * hb - 1, 0), 0)),
                pl.BlockSpec((nb, N_COLBLK, HALO, LANES),
                             lambda bi, ps, ci: (bi, 0, jnp.minimum((fwd_chunk(ps, ci) + 1) * hb, l // HALO - 1), 0)),
                pl.BlockSpec((nb, CHUNK, LANES), lambda bi, ps, ci: (bi, any_chunk(ps, ci), 0)),
                pl.BlockSpec((nb, HROWS, LANES), lambda bi, ps, ci: (bi, any_chunk(ps, ci), 0)),
                pl.BlockSpec((nb, CHUNK, D_SSM), lambda bi, ps, ci: (bi, bwd_chunk(ps, ci), 0)),
                *([state_spec, state_spec] if has_h0 else []),
                pl.BlockSpec((CONV_W, N_COLBLK, LANES), lambda bi, ps, ci: (0, 0, 0)),
                pl.BlockSpec((N_COLBLK, LANES), const2),
                pl.BlockSpec((1, D_SSM), const2),
                pl.BlockSpec((1, D_SSM), const2),
                pl.BlockSpec((2, 3 * LANES, 3 * D_SSM), lambda bi, ps, ci: (0, 0, 0))]
    return pl.pallas_call(
        functools.partial(_ssd_kernel, nc, has_h0),
        grid=(b // nb, 2, nc),
        in_specs=in_specs,
        out_specs=[pl.BlockSpec((nb, CHUNK, D_SSM), lambda bi, ps, ci: (bi, bwd_chunk(ps, ci), 0)),
                   state_spec, state_spec],
        out_shape=[jax.ShapeDtypeStruct((b, l, D_SSM), F32),
                   jax.ShapeDtypeStruct((b, n_state, D_STATE), F32),
                   jax.ShapeDtypeStruct((b, n_state, D_STATE), F32)],
        scratch_shapes=[pltpu.VMEM((nb, n_state, D_STATE), F32),
                        pltpu.VMEM((nb, N_COLBLK, CHUNK + 2 * HALO, LANES), F32),
                        pltpu.VMEM((nb, l, D_SSM), F32),
                        pltpu.VMEM((nb, l, 2 * SSM_GROUPS * D_STATE), BF16),
                        pltpu.VMEM((nb, l, D_SSM), F32)],
        compiler_params=_cparams(("parallel", "arbitrary", "arbitrary")),
        name="ssd",
    )(xbc, xbc, xbc, cols, hrow, z, *([h0f, h0b] if has_h0 else []), conv_w, conv_b, d_x, norm_g,
      _expand_matrices())


def _outproj_kernel(att_ref, ssm_ref, x_ref, mods_ref, wa_ref, ws_ref, g_ref, wr_ref, br_ref,
                    x1_ref, h2x_ref):
    for part in range(x_ref.shape[0] // OUT_PART):
        rows = slice(part * OUT_PART, (part + 1) * OUT_PART)
        _outproj_rows(rows, att_ref, ssm_ref, x_ref, mods_ref, wa_ref, ws_ref, g_ref, wr_ref, br_ref,
                      x1_ref, h2x_ref)


def _outproj_rows(rows, att_ref, ssm_ref, x_ref, mods_ref, wa_ref, ws_ref, g_ref, wr_ref, br_ref,
                  x1_ref, h2x_ref):
    mix = (jnp.dot(att_ref[rows, :].astype(BF16), wa_ref[...], preferred_element_type=F32)
           + jnp.dot(ssm_ref[rows, :].astype(BF16), ws_ref[...], preferred_element_type=F32))
    gate1 = mods_ref[:, 2 * D_MODEL:3 * D_MODEL]
    shift2 = mods_ref[:, 3 * D_MODEL:4 * D_MODEL]
    scale2 = mods_ref[:, 4 * D_MODEL:5 * D_MODEL]
    x1 = x_ref[rows, :] + gate1 * mix
    x1_ref[rows, :] = x1
    y = x1 * lax.rsqrt(jnp.mean(x1 * x1, axis=-1, keepdims=True) + EPS) * g_ref[...]
    h2 = y * (1.0 + scale2) + shift2
    h_hi = h2.astype(BF16)
    h2x_ref[rows, 0:D_MODEL] = h2
    h_lo = (h2 - h_hi.astype(F32)).astype(BF16)
    both = jnp.dot(h_hi, wr_ref[...], preferred_element_type=F32)
    logits = (both[:, 0:LANES] + both[:, LANES:2 * LANES]
              + jnp.dot(h_lo, wr_ref[:, 0:LANES], preferred_element_type=F32)) + br_ref[...]
    lane = lax.broadcasted_iota(jnp.int32, logits.shape, 1).astype(F32)
    neg = -jnp.inf
    big = float(1 << 20)
    is_g = lane < N_GROUPS
    gl = jnp.where(is_g, logits, neg)
    gmax = jnp.max(gl, axis=-1, keepdims=True)
    g_idx = jnp.min(jnp.where(gl == gmax, lane, big), axis=-1, keepdims=True)
    p_g = 1.0 / jnp.sum(jnp.where(is_g, jnp.exp(gl - gmax), 0.0), axis=-1, keepdims=True)
    e_lo = N_GROUPS + g_idx * EXPERTS_PER_GROUP
    in_grp = (lane >= e_lo) & (lane < e_lo + EXPERTS_PER_GROUP)
    el = jnp.where(in_grp, logits, neg)
    m1 = jnp.max(el, axis=-1, keepdims=True)
    i1 = jnp.min(jnp.where(el == m1, lane, big), axis=-1, keepdims=True)
    el2 = jnp.where(lane == i1, neg, el)
    m2 = jnp.max(el2, axis=-1, keepdims=True)
    i2 = jnp.min(jnp.where(el2 == m2, lane, big), axis=-1, keepdims=True)
    e2 = jnp.exp(m2 - m1)
    w1 = p_g / (1.0 + e2)
    w2 = p_g * e2 / (1.0 + e2)
    slab = (jnp.where(lane == i1 - e_lo, w1, 0.0) + jnp.where(lane == i2 - e_lo, w2, 0.0)
            + jnp.where(lane == EXPERTS_PER_GROUP, g_idx, 0.0))
    h2x_ref[rows, D_MODEL:H2X_W] = slab


def _outproj(att, ssm, x, mods3, mod_row0, mod_tokens, wo_att, wo_ssm, g, w_router, b_router):
    n = x.shape[0]
    tm = 2 * OUT_PART
    per_mod = mod_tokens // tm
    return pl.pallas_call(
        _outproj_kernel,
        grid=(n // tm,),
        in_specs=[pl.BlockSpec((tm, D_ATT), lambda i: (i, 0)),
                  pl.BlockSpec((tm, D_SSM), lambda i: (i, 0)),
                  pl.BlockSpec((tm, D_MODEL), lambda i: (i, 0)),
                  pl.BlockSpec((None, 1, 6 * D_MODEL), lambda i: (mod_row0 + i // per_mod, 0, 0)),
                  pl.BlockSpec((D_ATT, D_MODEL), lambda i: (0, 0)),
                  pl.BlockSpec((D_SSM, D_MODEL), lambda i: (0, 0)),
                  pl.BlockSpec((1, D_MODEL), lambda i: (0, 0)),
                  pl.BlockSpec((D_MODEL, 2 * LANES), lambda i: (0, 0)),
                  pl.BlockSpec((1, LANES), lambda i: (0, 0))],
        out_specs=[pl.BlockSpec((tm, D_MODEL), lambda i: (i, 0)),
                   pl.BlockSpec((tm, H2X_W), lambda i: (i, 0))],
        out_shape=[jax.ShapeDtypeStruct((n, D_MODEL), F32),
                   jax.ShapeDtypeStruct((n, H2X_W), F32)],
        compiler_params=_cparams(("parallel",)),
        name="outproj_router",
    )(att, ssm, x, mods3, wo_att, wo_ssm, g, w_router, b_router)


def _route_kernel(slab_ref, meta_ref):
    t_n = MOE_TILE
    blk = LANES
    slab = slab_ref[...]
    lane = lax.broadcasted_iota(jnp.int32, (t_n, LANES), 1)
    gcol = jnp.sum(jnp.where(lane == EXPERTS_PER_GROUP, slab, 0.0), axis=-1, keepdims=True)
    member = (lane.astype(F32) == gcol) & (lane < N_GROUPS)
    a = jnp.where(member, 1.0, 0.0).astype(BF16)
    r_i = lax.broadcasted_iota(jnp.int32, (blk, blk), 0)
    c_i = lax.broadcasted_iota(jnp.int32, (blk, blk), 1)
    lower = jnp.where(c_i < r_i, 1.0, 0.0).astype(BF16)
    upper = jnp.where(r_i < c_i, 1.0, 0.0).astype(BF16)
    offs = jnp.zeros((1, LANES), F32)
    ranks = []
    for b in range(t_n // blk):
        ab = a[b * blk:(b + 1) * blk]
        rb = jnp.dot(lower, ab, preferred_element_type=F32)
        ranks.append(rb + offs)
        offs = offs + rb[blk - 1:blk] + ab[blk - 1:blk].astype(F32)
    rank = jnp.concatenate(ranks, axis=0)
    n_chunk = jnp.floor((offs + (MOE_CHUNK - 1)) * (1.0 / MOE_CHUNK))
    start = jnp.dot(jnp.broadcast_to(n_chunk, (8, LANES)).astype(BF16), upper,
                    preferred_element_type=F32)[0:1]
    end = start + n_chunk
    dest = jnp.sum(jnp.where(member, start * MOE_CHUNK + rank, 0.0), axis=-1, keepdims=True)
    tok = lax.broadcasted_iota(jnp.int32, (t_n, LANES), 0)
    digits = jnp.where(lane == 0, (tok // blk).astype(F32),
                       jnp.where(lane == 1, (tok % blk).astype(F32), jnp.where(lane == 2, 1.0, 0.0))).astype(BF16)
    sw = 512
    pieces = []
    for sc in range(MOE_ROWS // sw):
        s_id = (lax.broadcasted_iota(jnp.int32, (t_n, sw), 1) + sc * sw).astype(F32)
        hit = jnp.where(dest == s_id, 1.0, 0.0).astype(BF16)
        r = lax.dot_general(digits, hit, (((0,), (0,)), ((), ())), preferred_element_type=F32)
        tok_of = r[0:1] * blk + r[1:2]
        pieces.append(jnp.where(r[2:3] > 0.5, tok_of, float(t_n)))
    pieces.append(jnp.full((1, META_ROWS - MOE_ROWS), float(t_n), F32))
    perm = jnp.concatenate(pieces, axis=1)
    slot = lax.broadcasted_iota(jnp.int32, (1, META_ROWS), 1).astype(F32)
    lane1 = lax.broadcasted_iota(jnp.int32, (1, LANES), 1)
    cg = jnp.zeros((1, META_ROWS), F32)
    for g in range(N_GROUPS):
        end_g = jnp.sum(jnp.where(lane1 == g, end, 0.0), axis=-1, keepdims=True)
        cg = cg + jnp.where(slot >= end_g, 1.0, 0.0)
    n_act = jnp.broadcast_to(end_g, (1, META_ROWS))
    meta_ref[...] = jnp.concatenate([perm, cg, n_act, jnp.zeros((5, META_ROWS), F32)], axis=0).astype(jnp.int32)


def _route(h2x):
    n = h2x.shape[0]
    n_tiles = n // MOE_TILE
    return pl.pallas_call(
        _route_kernel,
        grid=(n_tiles,),
        in_specs=[pl.BlockSpec((MOE_TILE, LANES), lambda i: (i, D_MODEL // LANES))],
        out_specs=pl.BlockSpec((None, 8, META_ROWS), lambda i: (i, 0, 0)),
        out_shape=jax.ShapeDtypeStruct((n_tiles, 8, META_ROWS), jnp.int32),
        compiler_params=_cparams(("parallel",)),
        name="moe_route",
    )(h2x)


def _moe_kernel(perm_ref, cg_ref, nact_ref, run_ref, h_ref, wg_hbm, wu_hbm, wd_hbm, y_ref,
                hs0, hs1, ys0, ys1, wg_buf, wu_buf, wd_buf, w_sem):
    i = pl.program_id(0)
    s = pl.program_id(1)
    n_act = nact_ref[i]
    tile_base = i * META_ROWS
    hs = (hs0, hs1)
    ys = (ys0, ys1)
    n_steps = pl.num_programs(0) * MOE_STEPS
    f = i * MOE_STEPS + s
    slot = run_ref[n_steps + f]

    def weight_copies(g, sl):
        e0 = g * EXPERTS_PER_GROUP
        r0 = g * (EXPERTS_PER_GROUP * EXPERT_FF)
        return (pltpu.make_async_copy(wg_hbm.at[pl.ds(e0, EXPERTS_PER_GROUP)], wg_buf.at[sl], w_sem.at[sl, 0]),
                pltpu.make_async_copy(wu_hbm.at[pl.ds(e0, EXPERTS_PER_GROUP)], wu_buf.at[sl], w_sem.at[sl, 1]),
                pltpu.make_async_copy(wd_hbm.at[pl.ds(r0, EXPERTS_PER_GROUP * EXPERT_FF)], wd_buf.at[sl],
                                      w_sem.at[sl, 2]))

    @pl.when(run_ref[f] == 1)
    def _():
        g = jnp.minimum(cg_ref[f], N_GROUPS - 1)

        @pl.when(run_ref[3 * n_steps + f] == 1)
        def _():
            for cp in weight_copies(g, slot):
                cp.start()

        for cp in weight_copies(g, slot):
            cp.wait()
        nxt = run_ref[2 * n_steps + f]

        @pl.when(nxt >= 0)
        def _():
            for cp in weight_copies(nxt, 1 - slot):
                cp.start()

    def gather(chunk, dst, rows=range(MOE_CHUNK)):
        base = tile_base + chunk * MOE_CHUNK
        for r in rows:
            src = jnp.minimum(perm_ref[base + r], MOE_TILE - 1)
            dst[r:r + 1, :] = h_ref[pl.ds(src, 1), :]

    def scatter(chunk, src, rows=range(MOE_CHUNK)):
        base = tile_base + chunk * MOE_CHUNK
        for r in rows:
            y_ref[pl.ds(perm_ref[base + r], 1), :] = src[r:r + 1, :]

    def ffn(src, dst, before_expert=lambda e: None):
        hb = src[:, 0:D_MODEL].astype(BF16)
        cw = src[:, D_MODEL:H2X_W]
        hid = []
        for e in range(EXPERTS_PER_GROUP):
            before_expert(e)
            a = jnp.dot(hb, wg_buf[slot, e], preferred_element_type=F32)
            u = jnp.dot(hb, wu_buf[slot, e], preferred_element_type=F32)
            hid.append((_silu(a) * u * cw[:, e:e + 1]).astype(BF16))
        dst[...] = jnp.dot(jnp.concatenate(hid, axis=1), wd_buf[slot], preferred_element_type=F32)

    @pl.when(s == 0)
    def _():
        y_ref[MOE_TILE:MOE_TILE + 8, :] = jnp.zeros((8, D_MODEL), F32)
        ys1[...] = jnp.zeros_like(ys1)
        gather(0, hs0)

    for par in (0, 1):
        @pl.when((s < n_act) & (s % 2 == par))
        def _():
            quarter = MOE_CHUNK // EXPERTS_PER_GROUP

            def copies(e):
                rows = range(e * quarter, (e + 1) * quarter)
                scatter(jnp.maximum(s - 1, 0), ys[1 - par], rows)
                gather(s + 1, hs[1 - par], rows)

            ffn(hs[par], ys[par], copies)

        @pl.when((s == n_act) & (s % 2 == par))
        def _():
            scatter(s - 1, ys[1 - par])


def _moe(h2x, perm, cgrp, nact, wg, wu, wd):
    n = h2x.shape[0]
    n_tiles = n // MOE_TILE
    n_steps = n_tiles * MOE_STEPS

    step = jnp.arange(n_steps)
    active = (step % MOE_STEPS) < jnp.repeat(nact, MOE_STEPS)
    prev = jnp.concatenate([jnp.full((1,), -1, jnp.int32), cgrp[:-1]])
    first = active & ((step % MOE_STEPS == 0) | (cgrp != prev))
    run_id = jnp.cumsum(first.astype(jnp.int32)) - 1
    later_first = lax.cummin(jnp.where(first, step, n_steps), reverse=True)
    nxt_step = jnp.concatenate([later_first[1:], jnp.full((1,), n_steps, jnp.int32)])
    nxt_group = jnp.where(nxt_step < n_steps, cgrp[jnp.minimum(nxt_step, n_steps - 1)], -1)
    runs = jnp.concatenate([first.astype(jnp.int32), run_id % 2, nxt_group.astype(jnp.int32),
                            (first & (run_id == 0)).astype(jnp.int32)])

    def h_idx(i, s, *_):
        done = (s >= jnp.maximum(_[2][i] - 1, 1)).astype(jnp.int32)
        return (jnp.minimum(i + done, n_tiles - 1), 0)

    hbm = pl.BlockSpec(memory_space=pl.ANY)
    return pl.pallas_call(
        _moe_kernel,
        grid_spec=pltpu.PrefetchScalarGridSpec(
            num_scalar_prefetch=4,
            grid=(n_tiles, MOE_STEPS),
            in_specs=[pl.BlockSpec((MOE_TILE, H2X_W), h_idx), hbm, hbm, hbm],
            out_specs=pl.BlockSpec((None, MOE_TILE + 8, D_MODEL), lambda i, s, *_: (i, 0, 0)),
            scratch_shapes=[pltpu.VMEM((MOE_CHUNK, H2X_W), F32), pltpu.VMEM((MOE_CHUNK, H2X_W), F32),
                            pltpu.VMEM((MOE_CHUNK, D_MODEL), F32), pltpu.VMEM((MOE_CHUNK, D_MODEL), F32),
                            pltpu.VMEM((2, EXPERTS_PER_GROUP, D_MODEL, EXPERT_FF), BF16),
                            pltpu.VMEM((2, EXPERTS_PER_GROUP, D_MODEL, EXPERT_FF), BF16),
                            pltpu.VMEM((2, EXPERTS_PER_GROUP * EXPERT_FF, D_MODEL), BF16),
                            pltpu.SemaphoreType.DMA((2, 3))]),
        out_shape=jax.ShapeDtypeStruct((n_tiles, MOE_TILE + 8, D_MODEL), F32),
        compiler_params=_cparams(("arbitrary", "arbitrary")),
        name="moe_experts",
    )(perm, cgrp, nact, runs, h2x, wg, wu, wd)


def _final_kernel(y_ref, x1_ref, mods_ref, fg_ref, o_ref):
    gate2 = mods_ref[:, 5 * D_MODEL:6 * D_MODEL]
    x2 = x1_ref[...] + gate2 * y_ref[...]
    o_ref[...] = x2 * lax.rsqrt(jnp.mean(x2 * x2, axis=-1, keepdims=True) + EPS) * fg_ref[...]


def _final(y, x1, mods3, mod_row0, mod_tokens, fg):
    n = x1.shape[0]
    tm = 2 * ROW_TILE
    per_mod = mod_tokens // tm
    per_tile = MOE_TILE // tm
    return pl.pallas_call(
        _final_kernel,
        grid=(n // tm,),
        in_specs=[pl.BlockSpec((None, tm, D_MODEL), lambda j: (j // per_tile, j % per_tile, 0)),
                  pl.BlockSpec((tm, D_MODEL), lambda j: (j, 0)),
                  pl.BlockSpec((None, 1, 6 * D_MODEL), lambda j: (mod_row0 + j // per_mod, 0, 0)),
                  pl.BlockSpec((1, D_MODEL), lambda j: (0, 0))],
        out_specs=pl.BlockSpec((tm, D_MODEL), lambda j: (j, 0)),
        out_shape=jax.ShapeDtypeStruct((n, D_MODEL), F32),
        compiler_params=_cparams(("parallel",)),
        name="final_norm",
    )(y, x1, mods3, fg)


def _rope_tables(t):
    n_freq = QK_DIM // 4
    n_rows = t // GRID_W
    freqs = ROPE_BASE ** (-jnp.arange(n_freq, dtype=F32) / n_freq)
    ang_r = jnp.arange(n_rows, dtype=F32)[:, None] * freqs
    ang_c = jnp.arange(GRID_W, dtype=F32)[:, None] * freqs
    cr, sr, cc, sc = lax.optimization_barrier((jnp.cos(ang_r), jnp.sin(ang_r), jnp.cos(ang_c), jnp.sin(ang_c)))
    j = np.arange(LANES) % QK_DIM
    f_idx = j % n_freq
    by_row = (j < QK_DIM // 2)[None, None, :]
    first = ((j % (QK_DIM // 2)) < n_freq)[None, None, :]

    def table(r_small, c_small):
        return jnp.where(by_row, r_small[:, f_idx][:, None, :], c_small[:, f_idx][None, :, :])

    cos = table(cr, cc)
    sin = table(sr, sc)
    return (cos.reshape(t, LANES), jnp.where(first, -sin, 0.0).reshape(t, LANES),
            jnp.where(first, 0.0, sin).reshape(t, LANES))


def _layer(x, mods3, mod_row0, mod_tokens, rope_tabs, ctx_k, ctx_v, h0f, h0b, lw, layer):
    b, t, _ = x.shape
    n = b * t
    xf = x.reshape(n, D_MODEL)
    res = _inproj(xf, mods3, mod_row0, mod_tokens, t, lw["norm_mix_g"], lw["w_main"], lw["w_dt"], lw["alog"],
                  lw["dtb"], rope_tabs)
    q, kb, vt, z, xbc, cols, hrow = res[:7]
    if ctx_k is None:
        cache = None
        k3 = res[7].reshape(b, 1, t, ATT_HEADS, 2, LANES)[..., :QK_DIM]
        v3 = res[8].reshape(b, 1, t, ATT_HEADS, V_DIM)
    else:
        cache = (ctx_k.astype(BF16), jnp.swapaxes(ctx_v, 1, 2).astype(BF16))
        k3 = v3 = None
    lam0 = 0.8 - 0.6 * math.exp(-0.3 * layer)
    att = _attention(q.reshape(b, t, D_QK), kb.reshape(b, t, D_QK), vt, cache, lw["lamp"], lw["attn_subln_g"],
                     lam0, *_attn_tiling(b, t))
    ssm, hf, hb = _ssd(xbc, cols.reshape(b, t, LANES), hrow.reshape(b, t // CHUNK * HROWS, LANES),
                       z.reshape(b, t, D_SSM), h0f, h0b, lw["conv_w"], lw["conv_b"], lw["d_x"], lw["ssm_norm_g"])
    x1, h2x = _outproj(att.reshape(n, D_ATT), ssm.reshape(n, D_SSM), xf, mods3, mod_row0, mod_tokens,
                       lw["wo_att"], lw["wo_ssm"], lw["norm_ffn_g"], lw["w_router"], lw["b_router"])
    meta = _route(h2x)
    perm = meta[:, 0, :].reshape(-1)
    cgrp = meta[:, 1, :MOE_STEPS].reshape(-1)
    nact = meta[:, 2, 0]
    y = _moe(h2x, perm, cgrp, nact, lw["wg"], lw["wu"], lw["wd"])
    out = _final(y, x1, mods3, mod_row0, mod_tokens, lw["final_g"])
    return out.reshape(b, t, D_MODEL), k3, v3, hf, hb


def _pad_lanes(v, width=LANES):
    return jnp.pad(v, [(0, 0)] * (v.ndim - 1) + [(0, width - v.shape[-1])])


def kernel(x_prompt, x_sample, cache_k, cache_v, state_ssm_fwd, state_ssm_bwd, c, c_ctx, w_ada, b_ada, norm_mix_g, w_in, w_out, lambda_q1, lambda_k1, lambda_q2, lambda_k2, attn_subln_g, conv_w, conv_b, a_log_fwd, a_log_bwd, dt_bias_fwd, dt_bias_bwd, ssm_d, ssm_norm_g, norm_ffn_g, w_group_router, b_group_router, w_expert_router, b_expert_router, w_exp_gate, w_exp_up, w_exp_down, final_norm_g):
    depth = w_in.shape[0]
    assert depth == 1, "single trunk layer"
    bp, tp, _ = x_prompt.shape
    bs, ts, _ = x_sample.shape
    l = 0
    cond = jnp.concatenate([c_ctx[None], c], axis=0)
    condT = _pad_lanes(cond.T, 8)
    mods = _ada(condT, w_ada[l], b_ada[l][None])
    mods3 = mods.reshape(8, 1, 6 * D_MODEL)

    w_router = _pad_lanes(jnp.concatenate([w_group_router[l], w_expert_router[l]], axis=1))
    wr_hi = w_router.astype(BF16)
    wr_lo = (w_router - wr_hi.astype(F32)).astype(BF16)
    lw = dict(
        norm_mix_g=norm_mix_g[l][None],
        w_main=w_in[l].astype(BF16),
        w_dt=_pad_lanes(w_in[l][:, MAIN_COLS:]).astype(BF16),
        lamp=jnp.stack([lambda_q1[l], lambda_k1[l], lambda_q2[l], lambda_k2[l]]),
        attn_subln_g=attn_subln_g[l][None],
        conv_w=conv_w[l].reshape(CONV_W, N_COLBLK, LANES), conv_b=conv_b[l].reshape(N_COLBLK, LANES),
        alog=jnp.broadcast_to(jnp.concatenate([a_log_fwd[l], a_log_bwd[l]])[:, None], (2 * SSM_HEADS, CHUNK)),
        dtb=jnp.broadcast_to(jnp.concatenate([dt_bias_fwd[l], dt_bias_bwd[l]])[:, None], (2 * SSM_HEADS, CHUNK)),
        d_x=jnp.repeat(ssm_d[l], SSM_HEADDIM)[None], ssm_norm_g=ssm_norm_g[l][None],
        wo_att=w_out[l][:D_ATT].astype(BF16), wo_ssm=w_out[l][D_ATT:].astype(BF16),
        norm_ffn_g=norm_ffn_g[l][None],
        w_router=jnp.concatenate([wr_hi, wr_lo], axis=1),
        b_router=_pad_lanes(jnp.concatenate([b_group_router[l], b_expert_router[l]])[None]),
        wg=w_exp_gate[l].astype(BF16), wu=w_exp_up[l].astype(BF16), wd=w_exp_down[l].astype(BF16).reshape(N_EXPERTS * EXPERT_FF, D_MODEL),
        final_g=final_norm_g[None],
    )
    n_state = SSM_HEADS * SSM_HEADDIM
    yp, ck, cv, hf, hb = _layer(x_prompt, mods3, 0, bp * tp, None, None, None, None, None, lw, l)
    ys, _, _, _, _ = _layer(x_sample, mods3, 1, ts, _rope_tables(ts),
                            cache_k[:, l].reshape(bs, -1, D_QK), cache_v[:, l].reshape(bs, -1, D_ATT),
                            state_ssm_fwd[:, l].reshape(bs, n_state, D_STATE),
                            state_ssm_bwd[:, l].reshape(bs, n_state, D_STATE), lw, l)
    new_k, new_v = ck, cv
    new_hf = hf.reshape(bp, 1, SSM_HEADS, SSM_HEADDIM, D_STATE)
    new_hb = hb.reshape(bp, 1, SSM_HEADS, SSM_HEADDIM, D_STATE)
    return yp, ys, new_k, new_v, new_hf, new_hb
```

```python
import functools
import math

import numpy as np
import jax
import jax.numpy as jnp
from jax import lax
from jax.experimental import pallas as pl
from jax.experimental.pallas import tpu as pltpu

D_MODEL = 1024
GRID_W = 64
ATT_HEADS = 4
QK_DIM = 64
V_DIM = 128
D_QK = 512
D_ATT = 512
ROPE_BASE = 10000.0
D_SSM = 512
SSM_HEADDIM = 64
SSM_HEADS = 8
SSM_GROUPS = 2
D_STATE = 128
CONV_W = 5
CHUNK = 128
XBC_DIM = 1024
N_GROUPS = 4
EXPERTS_PER_GROUP = 4
N_EXPERTS = 16
EXPERT_FF = 256
EPS = 1e-6
MAIN_COLS = 2 * D_QK + D_ATT + D_SSM + XBC_DIM
H2X_W = D_MODEL + 128
MOE_TILE = 2048
MOE_CHUNK = 256
MOE_SLOTS = MOE_TILE // MOE_CHUNK + N_GROUPS
MOE_ROWS = MOE_SLOTS * MOE_CHUNK
MOE_STEPS = MOE_SLOTS + 1
META_ROWS = MOE_STEPS * MOE_CHUNK
OUT_PART = 512
ROW_TILE = 512
ADA_COLS = 2048
SSD_SEQS = 2
COL_CUM, COL_DT, COL_ECUM, COL_TOEND = 0, 16, 32, 48
HROWS = 32
LANES = 128
HALO = 8
VMEM_LIMIT = 56 * 1024 * 1024

LOG2E = math.log2(math.e)
SUM_ROWS = 16
F32 = jnp.float32
BF16 = jnp.bfloat16


def _cparams(sem):
    return pltpu.CompilerParams(dimension_semantics=sem, vmem_limit_bytes=VMEM_LIMIT)


def _sigmoid(x):
    return 1.0 / (1.0 + jnp.exp(-x))


def _silu(x):
    return x * _sigmoid(x)


def _ada_kernel(condT_ref, w_ref, b_ref, o_ref):
    s = _silu(condT_ref[...])
    w = w_ref[...]
    b = b_ref[...]
    o_ref[...] = jnp.zeros_like(o_ref)
    for r in range(3):
        o_ref[r:r + 1, :] = jnp.sum(w * s[:, r:r + 1], axis=0, keepdims=True) + b


def _ada(condT, w_ada, b_ada):
    bn = ADA_COLS
    n = w_ada.shape[1]
    return pl.pallas_call(
        _ada_kernel,
        grid=(n // bn,),
        in_specs=[pl.BlockSpec((D_MODEL, 8), lambda j: (0, 0)),
                  pl.BlockSpec((D_MODEL, bn), lambda j: (0, j)),
                  pl.BlockSpec((1, bn), lambda j: (0, j))],
        out_specs=pl.BlockSpec((8, bn), lambda j: (0, j)),
        out_shape=jax.ShapeDtypeStruct((8, n), F32),
        compiler_params=_cparams(("arbitrary",)),
        name="ada",
    )(condT, w_ada, b_ada)


def _split3(x):
    hi = x.astype(BF16)
    r1 = x - hi.astype(F32)
    mid = r1.astype(BF16)
    lo = (r1 - mid.astype(F32)).astype(BF16)
    return jnp.concatenate([hi, mid, lo], axis=-1)


def _scan_matrices():
    t = np.arange(CHUNK)
    pre = (t[:, None] <= t[None, :]).astype(np.float32)
    suf = (t[:, None] >= t[None, :]).astype(np.float32)
    return jnp.asarray(np.stack([np.concatenate([pre] * 3, axis=0), np.concatenate([suf] * 3, axis=0)]), BF16)


def _head_scalars(dt_raw, alog_ref, dtb_ref, scan_ref):
    nh2 = 2 * SSM_HEADS
    xv = dt_raw.T[0:nh2, :] + dtb_ref[...]
    dt = jnp.maximum(xv, 0.0) + jnp.log(1.0 + jnp.exp(-jnp.abs(xv)))
    la3 = _split3(dt * (-jnp.exp(alog_ref[...])))
    fwd = lax.broadcasted_iota(jnp.int32, (nh2, CHUNK), 0) < SSM_HEADS
    cum = jnp.where(fwd, jnp.dot(la3, scan_ref[0], preferred_element_type=F32),
                    jnp.dot(la3, scan_ref[1], preferred_element_type=F32))
    cum_end = jnp.where(fwd, cum[:, CHUNK - 1:CHUNK], cum[:, 0:1])
    packed = jnp.concatenate([cum, dt, jnp.exp(cum), jnp.exp(cum_end - cum),
                              jnp.zeros((LANES - 4 * nh2, CHUNK), F32)], axis=0)
    return packed.T, jnp.concatenate([cum, jnp.exp(cum_end)], axis=0)


def _inproj_kernel(rope, x_ref, mods_ref, g_ref, w_ref, wdt_ref, alog_ref, dtb_ref, scan_ref, *rest):
    if rope:
        cos_ref, sa_ref, sb_ref, q_ref, kb_ref, vt_ref, z_ref, xbc_ref, cols_ref, hrow_ref = rest
    else:
        q_ref, kb_ref, vt_ref, z_ref, xbc_ref, cols_ref, hrow_ref, kf_ref, vf_ref = rest
    x = x_ref[...]
    shift = mods_ref[:, 0:D_MODEL]
    scale = mods_ref[:, D_MODEL:2 * D_MODEL]
    y = x * lax.rsqrt(jnp.mean(x * x, axis=-1, keepdims=True) + EPS) * g_ref[...]
    h = (y * (1.0 + scale) + shift).astype(BF16)
    dt_raw = jnp.dot(h, wdt_ref[...], preferred_element_type=F32)
    for ci in range(x.shape[0] // CHUNK):
        cols, hrow = _head_scalars(dt_raw[ci * CHUNK:(ci + 1) * CHUNK, :], alog_ref, dtb_ref, scan_ref)
        cols_ref[ci * CHUNK:(ci + 1) * CHUNK, :] = cols
        hrow_ref[ci * HROWS:(ci + 1) * HROWS, :] = hrow
    r = jnp.dot(h, w_ref[...], preferred_element_type=F32)
    q = r[:, 0:D_QK]
    k = r[:, D_QK:2 * D_QK]
    if rope:
        cos = cos_ref[...]
        sa = sa_ref[...]
        sb = sb_ref[...]

        def rot(t):
            parts = []
            for hh in range(ATT_HEADS):
                th = t[:, hh * LANES:(hh + 1) * LANES]
                parts.append(th * cos + pltpu.roll(th, LANES - 16, 1) * sa + pltpu.roll(th, 16, 1) * sb)
            return jnp.concatenate(parts, axis=1)

        q = rot(q)
        k = rot(k)
    v = r[:, 2 * D_QK:2 * D_QK + D_ATT]
    q_ref[...] = q
    kb_ref[...] = k.astype(BF16)
    n_seq, _, t_seq = vt_ref.shape
    v_t = v.T.astype(BF16)
    for sq in range(n_seq):
        vt_ref[sq] = v_t[:, sq * t_seq:(sq + 1) * t_seq]
    if not rope:
        tm = k.shape[0]
        for hh in range(ATT_HEADS):
            k_h = k[:, hh * LANES:(hh + 1) * LANES]
            vf_ref[pl.ds(hh, tm, stride=ATT_HEADS), :] = v[:, hh * LANES:(hh + 1) * LANES]
            kf_ref[pl.ds(2 * hh, tm, stride=2 * ATT_HEADS), :] = k_h
            kf_ref[pl.ds(2 * hh + 1, tm, stride=2 * ATT_HEADS), :] = pltpu.roll(k_h, QK_DIM, 1)
    z_ref[...] = r[:, 2 * D_QK + D_ATT:2 * D_QK + D_ATT + D_SSM]
    x0 = 2 * D_QK + D_ATT + D_SSM
    for cb in range(XBC_DIM // LANES):
        for sq in range(n_seq):
            xbc_ref[sq, cb] = r[sq * t_seq:(sq + 1) * t_seq, x0 + cb * LANES:x0 + (cb + 1) * LANES]


def _inproj(x, mods3, mod_row0, mod_tokens, seq_len, g, w_main, w_dt, alog, dtb, rope_tabs):
    n = x.shape[0]
    tm = ROW_TILE
    assert seq_len % tm == 0 or tm % seq_len == 0
    per_seq = max(seq_len // tm, 1)
    n_seq = max(tm // seq_len, 1)
    t_seq = tm // n_seq
    per_mod = mod_tokens // tm
    rope = rope_tabs is not None
    in_specs = [pl.BlockSpec((tm, D_MODEL), lambda i: (i, 0)),
                pl.BlockSpec((None, 1, 6 * D_MODEL), lambda i: (mod_row0 + i // per_mod, 0, 0)),
                pl.BlockSpec((1, D_MODEL), lambda i: (0, 0)),
                pl.BlockSpec((D_MODEL, MAIN_COLS), lambda i: (0, 0)),
                pl.BlockSpec((D_MODEL, LANES), lambda i: (0, 0)),
                pl.BlockSpec((2 * SSM_HEADS, CHUNK), lambda i: (0, 0)),
                pl.BlockSpec((2 * SSM_HEADS, CHUNK), lambda i: (0, 0)),
                pl.BlockSpec((2, 3 * CHUNK, CHUNK), lambda i: (0, 0, 0))]
    args = [x, mods3, g, w_main, w_dt, alog, dtb, _scan_matrices()]
    if rope:
        tab_spec = pl.BlockSpec((tm, LANES), lambda i: (i % per_seq, 0))
        in_specs += [tab_spec] * 3
        args += list(rope_tabs)
    def rows(wd, dtype=F32):
        return pl.BlockSpec((tm, wd), lambda i: (i, 0)), jax.ShapeDtypeStruct((n, wd), dtype)

    hr = tm // CHUNK * HROWS
    outs = [rows(D_QK), rows(D_QK, BF16),
            (pl.BlockSpec((n_seq, D_ATT, t_seq), lambda i: (i // per_seq, 0, i % per_seq)),
             jax.ShapeDtypeStruct((n // seq_len, D_ATT, seq_len), BF16)),
            rows(D_SSM),
            (pl.BlockSpec((n_seq, XBC_DIM // LANES, t_seq, LANES), lambda i: (i // per_seq, 0, i % per_seq, 0)),
             jax.ShapeDtypeStruct((n // seq_len, XBC_DIM // LANES, seq_len, LANES), F32)),
            rows(LANES),
            (pl.BlockSpec((hr, LANES), lambda i: (i, 0)), jax.ShapeDtypeStruct((n // CHUNK * HROWS, LANES), F32))]
    if not rope:
        outs += [(pl.BlockSpec((tm * 2 * ATT_HEADS, LANES), lambda i: (i, 0)),
                  jax.ShapeDtypeStruct((n * 2 * ATT_HEADS, LANES), F32)),
                 (pl.BlockSpec((tm * ATT_HEADS, LANES), lambda i: (i, 0)),
                  jax.ShapeDtypeStruct((n * ATT_HEADS, LANES), F32))]
    return pl.pallas_call(
        functools.partial(_inproj_kernel, rope),
        grid=(n // tm,),
        in_specs=in_specs,
        out_specs=[o[0] for o in outs],
        out_shape=[o[1] for o in outs],
        compiler_params=_cparams(("parallel",)),
        name="inproj_rope" if rope else "inproj",
    )(*args)


def _attn_kernel(tk, lam0, has_cache, q_ref, k_ref, vt_ref, *rest):
    if has_cache:
        ck_ref, cvt_ref, lamp_ref, g_ref, o_ref = rest
    else:
        lamp_ref, g_ref, o_ref = rest
    tq = q_ref.shape[1]
    lp = lamp_ref[...]
    lam = (jnp.exp(jnp.sum(lp[0:1] * lp[1:2], axis=-1, keepdims=True))
           - jnp.exp(jnp.sum(lp[2:3] * lp[3:4], axis=-1, keepdims=True)) + lam0)
    for bb, hh in [(b_, h_) for b_ in range(q_ref.shape[0]) for h_ in range(q_ref.shape[2] // LANES)]:
        hs = slice(hh * LANES, (hh + 1) * LANES)
        chunks = [(k_ref, vt_ref, c * tk, tk) for c in range(k_ref.shape[1] // tk)]
        if has_cache:
            ck = min(tk, ck_ref.shape[1])
            chunks += [(ck_ref, cvt_ref, c * ck, ck) for c in range(ck_ref.shape[1] // ck)]
        q = q_ref[bb, :, hs] * (QK_DIM ** -0.5 * LOG2E)
        lane = lax.broadcasted_iota(jnp.int32, q.shape, 1)
        qq_t = jnp.concatenate([jnp.where(lane < QK_DIM, q, 0.0), jnp.where(lane >= QK_DIM, q, 0.0)],
                               axis=0).T.astype(BF16)

        def scores(chunk):
            kr, _, start, size = chunk
            return jnp.dot(kr[bb, start:start + size, hs], qq_t, preferred_element_type=F32)

        def update(s, chunk, m, acc):
            _, vr, start, size = chunk
            m_new = jnp.maximum(m, jnp.max(s, axis=0, keepdims=True))
            alpha = jnp.exp2(m - m_new)
            p = jnp.exp2(s - m_new).astype(BF16)
            v_ext = jnp.concatenate([vr[bb, hs, start:start + size], jnp.ones((SUM_ROWS, size), BF16)], axis=0)
            acc = alpha * acc + jnp.dot(v_ext, p, preferred_element_type=F32)
            return m_new, acc

        m = jnp.full((1, 2 * tq), -jnp.inf, F32)
        acc = jnp.zeros((V_DIM + SUM_ROWS, 2 * tq), F32)
        s = scores(chunks[0])
        for c, chunk in enumerate(chunks):
            s_next = scores(chunks[c + 1]) if c + 1 < len(chunks) else None
            m, acc = update(s, chunk, m, acc)
            s = s_next
        o = acc[0:V_DIM] / acc[V_DIM:V_DIM + 1]
        o = (o[:, 0:tq] - lam * o[:, tq:2 * tq]).T
        o = o * lax.rsqrt(jnp.mean(o * o, axis=-1, keepdims=True) + EPS)
        o_ref[bb, :, hs] = o * g_ref[...] * (1.0 - lam0)


def _attn_tiling(b, t):
    if t % 1024 == 0:
        return 1024, 512, 1, 1
    assert t % 256 == 0 and b % 4 == 0
    return 256, 256, ATT_HEADS, 4


def _attention(q, k, vt, cache, lamp, g, lam0, tq, tk, nh, nb):
    b, t, _ = q.shape
    wd = nh * LANES

    def kv_specs(length):
        return [pl.BlockSpec((nb, length, wd), lambda bi, h, i: (bi, 0, h)),
                pl.BlockSpec((nb, wd, length), lambda bi, h, i: (bi, h, 0))]

    in_specs = [pl.BlockSpec((nb, tq, wd), lambda bi, h, i: (bi, i, h))] + kv_specs(t)
    args = [q, k, vt]
    if cache is not None:
        assert cache[0].shape[1] % min(tk, cache[0].shape[1]) == 0
        in_specs += kv_specs(cache[0].shape[1])
        args += list(cache)
    in_specs += [pl.BlockSpec((4, QK_DIM), lambda bi, h, i: (0, 0)),
                 pl.BlockSpec((1, V_DIM), lambda bi, h, i: (0, 0))]
    return pl.pallas_call(
        functools.partial(_attn_kernel, tk, lam0, cache is not None),
        grid=(b // nb, ATT_HEADS // nh, t // tq),
        in_specs=in_specs,
        out_specs=pl.BlockSpec((nb, tq, wd), lambda bi, h, i: (bi, i, h)),
        out_shape=jax.ShapeDtypeStruct((b, t, D_ATT), F32),
        compiler_params=_cparams(("parallel", "parallel", "arbitrary")),
        name="diff_attn",
    )(*args, lamp, g)


N_COLBLK = XBC_DIM // LANES
def _expand_matrices():
    out = []
    for d in range(2):
        e = np.zeros((LANES, 3 * D_SSM), np.float32)
        for blk, lane0 in enumerate((COL_DT, COL_ECUM, COL_TOEND)):
            for h in range(SSM_HEADS):
                e[lane0 + d * SSM_HEADS + h,
                  blk * D_SSM + h * SSM_HEADDIM:blk * D_SSM + (h + 1) * SSM_HEADDIM] = 1.0
        out.append(np.concatenate([e, e, e], axis=0))
    return jnp.asarray(np.stack(out), BF16)


def _ssd_chunk(reverse, xs, bm, cm, state, hrow, cols, e_ref):
    d0 = SSM_HEADS if reverse else 0
    cum = hrow[d0:d0 + SSM_HEADS, :]
    dec = hrow[2 * SSM_HEADS + d0:2 * SSM_HEADS + d0 + SSM_HEADS, 0:1]
    row = lax.broadcasted_iota(jnp.int32, (CHUNK, LANES), 0)
    lane = lax.broadcasted_iota(jnp.int32, (CHUNK, LANES), 1)
    causal = (row <= lane) if reverse else (row >= lane)
    lane_g = lax.broadcasted_iota(jnp.int32, (CHUNK, 2 * LANES), 1)
    spread = jnp.dot(_split3(cols), e_ref[1 if reverse else 0], preferred_element_type=F32)
    xd = xs * spread[:, 0:D_SSM]
    xdw = (xd * spread[:, 2 * D_SSM:3 * D_SSM]).astype(BF16)
    xd = xd.astype(BF16)
    ecum_x = spread[:, D_SSM:2 * D_SSM]
    rep = SSM_HEADS // SSM_GROUPS
    y_parts = []
    new_state = []
    for g in range(SSM_GROUPS):
        bg = bm[:, g * D_STATE:(g + 1) * D_STATE]
        cg = cm[:, g * D_STATE:(g + 1) * D_STATE]
        cbt = lax.dot_general(cg, bg, (((1,), (1,)), ((), ())), preferred_element_type=F32)
        rows = slice(g * rep * SSM_HEADDIM, (g + 1) * rep * SSM_HEADDIM)
        st_g = state[rows, :]
        y_off = lax.dot_general(cg, st_g.astype(BF16), (((1,), (1,)), ((), ())),
                                preferred_element_type=F32)
        cst = lax.dot_general(xdw[:, rows], bg, (((0,), (0,)), ((), ())), preferred_element_type=F32)
        xd_g = xd[:, rows]
        scs = []
        blocks = []
        for hh in range(rep):
            h = g * rep + hh
            seg = cols[:, COL_CUM + d0 + h:COL_CUM + d0 + h + 1] - cum[h:h + 1, :]
            decay = jnp.exp(jnp.where(causal, seg, -jnp.inf))
            scs.append((cbt * decay).astype(BF16))
            blocks.append(jnp.where(lane_g // SSM_HEADDIM == hh, xd_g, jnp.zeros_like(xd_g)))
        y_diag = jnp.dot(jnp.concatenate(scs, axis=1), jnp.concatenate(blocks, axis=0),
                         preferred_element_type=F32)
        dec_rows = jnp.concatenate(
            [jnp.broadcast_to(dec[g * rep + hh:g * rep + hh + 1, :], (SSM_HEADDIM, D_STATE)) for hh in range(rep)],
            axis=0)
        new_state.append(st_g * dec_rows + cst)
        y_parts.append(y_diag + y_off * ecum_x[:, rows])
    return jnp.concatenate(y_parts, axis=1), jnp.concatenate(new_state, axis=0)


def _ssd_kernel(nc, has_h0, xc_ref, xp_ref, xn_ref, cols_ref, hrow_ref, z_ref, *rest):
    if has_h0:
        h0f_ref, h0b_ref = rest[:2]
        rest = rest[2:]
    cw_ref, cb_ref, d_ref, ng_ref, e_ref, y_ref, hf_ref, hb_ref, state_ref, slab_ref, xs_ref, bc_ref, yf_ref = rest
    ps = pl.program_id(1)
    c = pl.program_id(2)
    n_bc = SSM_GROUPS * D_STATE

    @pl.when((ps == 0) & (c == 0))
    def _():
        state_ref[...] = h0f_ref[...] if has_h0 else jnp.zeros_like(state_ref)

    @pl.when(ps == 0)
    def _():
        t0 = pl.multiple_of(c * CHUNK, CHUNK)
        for bb in range(state_ref.shape[0]):
            slab_ref[bb, :, 0:HALO, :] = jnp.where(c > 0, xp_ref[bb], 0.0)
            slab_ref[bb, :, HALO:HALO + CHUNK, :] = xc_ref[bb]
            slab_ref[bb, :, HALO + CHUNK:2 * HALO + CHUNK, :] = jnp.where(c < nc - 1, xn_ref[bb], 0.0)
            blocks = []
            for cb in range(N_COLBLK):
                conv = cb_ref[cb:cb + 1, :]
                for kk in range(CONV_W):
                    off = HALO - CONV_W // 2 + kk
                    conv = conv + slab_ref[bb, cb, off:off + CHUNK, :] * cw_ref[kk, cb:cb + 1, :]
                blocks.append(_silu(conv))
            xs = jnp.concatenate(blocks[0:D_SSM // LANES], axis=1)
            bc = jnp.concatenate(blocks[D_SSM // LANES:], axis=1).astype(BF16)
            y, state = _ssd_chunk(False, xs, bc[:, 0:n_bc], bc[:, n_bc:], state_ref[bb], hrow_ref[bb],
                                  cols_ref[bb], e_ref)
            xs_ref[bb, pl.ds(t0, CHUNK), :] = xs
            bc_ref[bb, pl.ds(t0, CHUNK), :] = bc
            yf_ref[bb, pl.ds(t0, CHUNK), :] = y
            state_ref[bb] = state

    @pl.when((ps == 0) & (c == nc - 1))
    def _():
        hf_ref[...] = state_ref[...]
        state_ref[...] = h0b_ref[...] if has_h0 else jnp.zeros_like(state_ref)

    @pl.when(ps == 1)
    def _():
        t0 = pl.multiple_of((nc - 1 - c) * CHUNK, CHUNK)
        for bb in range(state_ref.shape[0]):
            xs = xs_ref[bb, pl.ds(t0, CHUNK), :]
            bc = bc_ref[bb, pl.ds(t0, CHUNK), :]
            y, state = _ssd_chunk(True, xs, bc[:, 0:n_bc], bc[:, n_bc:], state_ref[bb], hrow_ref[bb],
                                  cols_ref[bb], e_ref)
            state_ref[bb] = state
            y = y + yf_ref[bb, pl.ds(t0, CHUNK), :] + xs * d_ref[...]
            y = y * _silu(z_ref[bb])
            y = y * lax.rsqrt(jnp.mean(y * y, axis=-1, keepdims=True) + EPS)
            y_ref[bb] = y * ng_ref[...]

    @pl.when((ps == 1) & (c == nc - 1))
    def _():
        hb_ref[...] = state_ref[...]


def _ssd(xbc, cols, hrow, z, h0f, h0b, conv_w, conv_b, d_x, norm_g):
    b, _, l, _ = xbc.shape
    has_h0 = h0f is not None
    nb = SSD_SEQS
    assert b % nb == 0
    nc = l // CHUNK
    hb = CHUNK // HALO
    n_state = SSM_HEADS * SSM_HEADDIM
    last = nc - 1

    def fwd_chunk(ps, ci):
        return jnp.where(ps == 0, ci, last)

    def any_chunk(ps, ci):
        return jnp.where(ps == 0, ci, last - ci)

    def bwd_chunk(ps, ci):
        return jnp.where(ps == 0, last, last - ci)

    const2 = lambda bi, ps, ci: (0, 0)
    state_spec = pl.BlockSpec((nb, n_state, D_STATE), lambda bi, ps, ci: (bi, 0, 0))
    in_specs = [pl.BlockSpec((nb, N_COLBLK, CHUNK, LANES), lambda bi, ps, ci: (bi, 0, fwd_chunk(ps, ci), 0)),
                pl.BlockSpec((nb, N_COLBLK, HALO, LANES),
                             lambda bi, ps, ci: (bi, 0, jnp.maximum(fwd_chunk(ps, ci) * hb - 1, 0), 0)),
                pl.BlockSpec((nb, N_COLBLK, HALO, LANES),
                             lambda bi, ps, ci: (bi, 0, jnp.minimum((fwd_chunk(ps, ci) + 1) * hb, l // HALO - 1), 0)),
                pl.BlockSpec((nb, CHUNK, LANES), lambda bi, ps, ci: (bi, any_chunk(ps, ci), 0)),
                pl.BlockSpec((nb, HROWS, LANES), lambda bi, ps, ci: (bi, any_chunk(ps, ci), 0)),
                pl.BlockSpec((nb, CHUNK, D_SSM), lambda bi, ps, ci: (bi, bwd_chunk(ps, ci), 0)),
                *([state_spec, state_spec] if has_h0 else []),
                pl.BlockSpec((CONV_W, N_COLBLK, LANES), lambda bi, ps, ci: (0, 0, 0)),
                pl.BlockSpec((N_COLBLK, LANES), const2),
                pl.BlockSpec((1, D_SSM), const2),
                pl.BlockSpec((1, D_SSM), const2),
                pl.BlockSpec((2, 3 * LANES, 3 * D_SSM), lambda bi, ps, ci: (0, 0, 0))]
    return pl.pallas_call(
        functools.partial(_ssd_kernel, nc, has_h0),
        grid=(b // nb, 2, nc),
        in_specs=in_specs,
        out_specs=[pl.BlockSpec((nb, CHUNK, D_SSM), lambda bi, ps, ci: (bi, bwd_chunk(ps, ci), 0)),
                   state_spec, state_spec],
        out_shape=[jax.ShapeDtypeStruct((b, l, D_SSM), F32),
                   jax.ShapeDtypeStruct((b, n_state, D_STATE), F32),
                   jax.ShapeDtypeStruct((b, n_state, D_STATE), F32)],
        scratch_shapes=[pltpu.VMEM((nb, n_state, D_STATE), F32),
                        pltpu.VMEM((nb, N_COLBLK, CHUNK + 2 * HALO, LANES), F32),
                        pltpu.VMEM((nb, l, D_SSM), F32),
                        pltpu.VMEM((nb, l, 2 * SSM_GROUPS * D_STATE), BF16),
                        pltpu.VMEM((nb, l, D_SSM), F32)],
        compiler_params=_cparams(("parallel", "arbitrary", "arbitrary")),
        name="ssd",
    )(xbc, xbc, xbc, cols, hrow, z, *([h0f, h0b] if has_h0 else []), conv_w, conv_b, d_x, norm_g,
      _expand_matrices())


def _outproj_kernel(att_ref, ssm_ref, x_ref, mods_ref, wa_ref, ws_ref, g_ref, wr_ref, br_ref,
                    x1_ref, h2x_ref):
    for part in range(x_ref.shape[0] // OUT_PART):
        rows = slice(part * OUT_PART, (part + 1) * OUT_PART)
        _outproj_rows(rows, att_ref, ssm_ref, x_ref, mods_ref, wa_ref, ws_ref, g_ref, wr_ref, br_ref,
                      x1_ref, h2x_ref)


def _outproj_rows(rows, att_ref, ssm_ref, x_ref, mods_ref, wa_ref, ws_ref, g_ref, wr_ref, br_ref,
                  x1_ref, h2x_ref):
    mix = (jnp.dot(att_ref[rows, :].astype(BF16), wa_ref[...], preferred_element_type=F32)
           + jnp.dot(ssm_ref[rows, :].astype(BF16), ws_ref[...], preferred_element_type=F32))
    gate1 = mods_ref[:, 2 * D_MODEL:3 * D_MODEL]
    shift2 = mods_ref[:, 3 * D_MODEL:4 * D_MODEL]
    scale2 = mods_ref[:, 4 * D_MODEL:5 * D_MODEL]
    x1 = x_ref[rows, :] + gate1 * mix
    x1_ref[rows, :] = x1
    y = x1 * lax.rsqrt(jnp.mean(x1 * x1, axis=-1, keepdims=True) + EPS) * g_ref[...]
    h2 = y * (1.0 + scale2) + shift2
    h_hi = h2.astype(BF16)
    h2x_ref[rows, 0:D_MODEL] = h2
    h_lo = (h2 - h_hi.astype(F32)).astype(BF16)
    both = jnp.dot(h_hi, wr_ref[...], preferred_element_type=F32)
    logits = (both[:, 0:LANES] + both[:, LANES:2 * LANES]
              + jnp.dot(h_lo, wr_ref[:, 0:LANES], preferred_element_type=F32)) + br_ref[...]
    lane = lax.broadcasted_iota(jnp.int32, logits.shape, 1).astype(F32)
    neg = -jnp.inf
    big = float(1 << 20)
    is_g = lane < N_GROUPS
    gl = jnp.where(is_g, logits, neg)
    gmax = jnp.max(gl, axis=-1, keepdims=True)
    g_idx = jnp.min(jnp.where(gl == gmax, lane, big), axis=-1, keepdims=True)
    p_g = 1.0 / jnp.sum(jnp.where(is_g, jnp.exp(gl - gmax), 0.0), axis=-1, keepdims=True)
    e_lo = N_GROUPS + g_idx * EXPERTS_PER_GROUP
    in_grp = (lane >= e_lo) & (lane < e_lo + EXPERTS_PER_GROUP)
    el = jnp.where(in_grp, logits, neg)
    m1 = jnp.max(el, axis=-1, keepdims=True)
    i1 = jnp.min(jnp.where(el == m1, lane, big), axis=-1, keepdims=True)
    el2 = jnp.where(lane == i1, neg, el)
    m2 = jnp.max(el2, axis=-1, keepdims=True)
    i2 = jnp.min(jnp.where(el2 == m2, lane, big), axis=-1, keepdims=True)
    e2 = jnp.exp(m2 - m1)
    w1 = p_g / (1.0 + e2)
    w2 = p_g * e2 / (1.0 + e2)
    slab = (jnp.where(lane == i1 - e_lo, w1, 0.0) + jnp.where(lane == i2 - e_lo, w2, 0.0)
            + jnp.where(lane == EXPERTS_PER_GROUP, g_idx, 0.0))
    h2x_ref[rows, D_MODEL:H2X_W] = slab


def _outproj(att, ssm, x, mods3, mod_row0, mod_tokens, wo_att, wo_ssm, g, w_router, b_router):
    n = x.shape[0]
    tm = 2 * OUT_PART
    per_mod = mod_tokens // tm
    return pl.pallas_call(
        _outproj_kernel,
        grid=(n // tm,),
        in_specs=[pl.BlockSpec((tm, D_ATT), lambda i: (i, 0)),
                  pl.BlockSpec((tm, D_SSM), lambda i: (i, 0)),
                  pl.BlockSpec((tm, D_MODEL), lambda i: (i, 0)),
                  pl.BlockSpec((None, 1, 6 * D_MODEL), lambda i: (mod_row0 + i // per_mod, 0, 0)),
                  pl.BlockSpec((D_ATT, D_MODEL), lambda i: (0, 0)),
                  pl.BlockSpec((D_SSM, D_MODEL), lambda i: (0, 0)),
                  pl.BlockSpec((1, D_MODEL), lambda i: (0, 0)),
                  pl.BlockSpec((D_MODEL, 2 * LANES), lambda i: (0, 0)),
                  pl.BlockSpec((1, LANES), lambda i: (0, 0))],
        out_specs=[pl.BlockSpec((tm, D_MODEL), lambda i: (i, 0)),
                   pl.BlockSpec((tm, H2X_W), lambda i: (i, 0))],
        out_shape=[jax.ShapeDtypeStruct((n, D_MODEL), F32),
                   jax.ShapeDtypeStruct((n, H2X_W), F32)],
        compiler_params=_cparams(("parallel",)),
        name="outproj_router",
    )(att, ssm, x, mods3, wo_att, wo_ssm, g, w_router, b_router)


def _route_kernel(slab_ref, meta_ref):
    t_n = MOE_TILE
    blk = LANES
    slab = slab_ref[...]
    lane = lax.broadcasted_iota(jnp.int32, (t_n, LANES), 1)
    gcol = jnp.sum(jnp.where(lane == EXPERTS_PER_GROUP, slab, 0.0), axis=-1, keepdims=True)
    member = (lane.astype(F32) == gcol) & (lane < N_GROUPS)
    a = jnp.where(member, 1.0, 0.0).astype(BF16)
    r_i = lax.broadcasted_iota(jnp.int32, (blk, blk), 0)
    c_i = lax.broadcasted_iota(jnp.int32, (blk, blk), 1)
    lower = jnp.where(c_i < r_i, 1.0, 0.0).astype(BF16)
    upper = jnp.where(r_i < c_i, 1.0, 0.0).astype(BF16)
    offs = jnp.zeros((1, LANES), F32)
    ranks = []
    for b in range(t_n // blk):
        ab = a[b * blk:(b + 1) * blk]
        rb = jnp.dot(lower, ab, preferred_element_type=F32)
        ranks.append(rb + offs)
        offs = offs + rb[blk - 1:blk] + ab[blk - 1:blk].astype(F32)
    rank = jnp.concatenate(ranks, axis=0)
    n_chunk = jnp.floor((offs + (MOE_CHUNK - 1)) * (1.0 / MOE_CHUNK))
    start = jnp.dot(jnp.broadcast_to(n_chunk, (8, LANES)).astype(BF16), upper,
                    preferred_element_type=F32)[0:1]
    end = start + n_chunk
    dest = jnp.sum(jnp.where(member, start * MOE_CHUNK + rank, 0.0), axis=-1, keepdims=True)
    tok = lax.broadcasted_iota(jnp.int32, (t_n, LANES), 0)
    digits = jnp.where(lane == 0, (tok // blk).astype(F32),
                       jnp.where(lane == 1, (tok % blk).astype(F32), jnp.where(lane == 2, 1.0, 0.0))).astype(BF16)
    sw = 512
    pieces = []
    for sc in range(MOE_ROWS // sw):
        s_id = (lax.broadcasted_iota(jnp.int32, (t_n, sw), 1) + sc * sw).astype(F32)
        hit = jnp.where(dest == s_id, 1.0, 0.0).astype(BF16)
        r = lax.dot_general(digits, hit, (((0,), (0,)), ((), ())), preferred_element_type=F32)
        tok_of = r[0:1] * blk + r[1:2]
        pieces.append(jnp.where(r[2:3] > 0.5, tok_of, float(t_n)))
    pieces.append(jnp.full((1, META_ROWS - MOE_ROWS), float(t_n), F32))
    perm = jnp.concatenate(pieces, axis=1)
    slot = lax.broadcasted_iota(jnp.int32, (1, META_ROWS), 1).astype(F32)
    lane1 = lax.broadcasted_iota(jnp.int32, (1, LANES), 1)
    cg = jnp.zeros((1, META_ROWS), F32)
    for g in range(N_GROUPS):
        end_g = jnp.sum(jnp.where(lane1 == g, end, 0.0), axis=-1, keepdims=True)
        cg = cg + jnp.where(slot >= end_g, 1.0, 0.0)
    n_act = jnp.broadcast_to(end_g, (1, META_ROWS))
    meta_ref[...] = jnp.concatenate([perm, cg, n_act, jnp.zeros((5, META_ROWS), F32)], axis=0).astype(jnp.int32)


def _route(h2x):
    n = h2x.shape[0]
    n_tiles = n // MOE_TILE
    return pl.pallas_call(
        _route_kernel,
        grid=(n_tiles,),
        in_specs=[pl.BlockSpec((MOE_TILE, LANES), lambda i: (i, D_MODEL // LANES))],
        out_specs=pl.BlockSpec((None, 8, META_ROWS), lambda i: (i, 0, 0)),
        out_shape=jax.ShapeDtypeStruct((n_tiles, 8, META_ROWS), jnp.int32),
        compiler_params=_cparams(("parallel",)),
        name="moe_route",
    )(h2x)


def _moe_kernel(perm_ref, cg_ref, nact_ref, run_ref, h_ref, wg_hbm, wu_hbm, wd_hbm, y_ref,
                hs0, hs1, ys0, ys1, wg_buf, wu_buf, wd_buf, w_sem):
    i = pl.program_id(0)
    s = pl.program_id(1)
    n_act = nact_ref[i]
    tile_base = i * META_ROWS
    hs = (hs0, hs1)
    ys = (ys0, ys1)
    n_steps = pl.num_programs(0) * MOE_STEPS
    f = i * MOE_STEPS + s
    slot = run_ref[n_steps + f]

    def weight_copies(g, sl):
        e0 = g * EXPERTS_PER_GROUP
        r0 = g * (EXPERTS_PER_GROUP * EXPERT_FF)
        return (pltpu.make_async_copy(wg_hbm.at[pl.ds(e0, EXPERTS_PER_GROUP)], wg_buf.at[sl], w_sem.at[sl, 0]),
                pltpu.make_async_copy(wu_hbm.at[pl.ds(e0, EXPERTS_PER_GROUP)], wu_buf.at[sl], w_sem.at[sl, 1]),
                pltpu.make_async_copy(wd_hbm.at[pl.ds(r0, EXPERTS_PER_GROUP * EXPERT_FF)], wd_buf.at[sl],
                                      w_sem.at[sl, 2]))

    @pl.when(run_ref[f] == 1)
    def _():
        g = jnp.minimum(cg_ref[f], N_GROUPS - 1)

        @pl.when(run_ref[3 * n_steps + f] == 1)
        def _():
            for cp in weight_copies(g, slot):
                cp.start()

        for cp in weight_copies(g, slot):
            cp.wait()
        nxt = run_ref[2 * n_steps + f]

        @pl.when(nxt >= 0)
        def _():
            for cp in weight_copies(nxt, 1 - slot):
                cp.start()

    def gather(chunk, dst, rows=range(MOE_CHUNK)):
        base = tile_base + chunk * MOE_CHUNK
        for r in rows:
            src = jnp.minimum(perm_ref[base + r], MOE_TILE - 1)
            dst[r:r + 1, :] = h_ref[pl.ds(src, 1), :]

    def scatter(chunk, src, rows=range(MOE_CHUNK)):
        base = tile_base + chunk * MOE_CHUNK
        for r in rows:
            y_ref[pl.ds(perm_ref[base + r], 1), :] = src[r:r + 1, :]

    def ffn(src, dst, before_expert=lambda e: None):
        hb = src[:, 0:D_MODEL].astype(BF16)
        cw = src[:, D_MODEL:H2X_W]
        hid = []
        for e in range(EXPERTS_PER_GROUP):
            before_expert(e)
            a = jnp.dot(hb, wg_buf[slot, e], preferred_element_type=F32)
            u = jnp.dot(hb, wu_buf[slot, e], preferred_element_type=F32)
            hid.append((_silu(a) * u * cw[:, e:e + 1]).astype(BF16))
        dst[...] = jnp.dot(jnp.concatenate(hid, axis=1), wd_buf[slot], preferred_element_type=F32)

    @pl.when(s == 0)
    def _():
        y_ref[MOE_TILE:MOE_TILE + 8, :] = jnp.zeros((8, D_MODEL), F32)
        ys1[...] = jnp.zeros_like(ys1)
        gather(0, hs0)

    for par in (0, 1):
        @pl.when((s < n_act) & (s % 2 == par))
        def _():
            quarter = MOE_CHUNK // EXPERTS_PER_GROUP

            def copies(e):
                rows = range(e * quarter, (e + 1) * quarter)
                scatter(jnp.maximum(s - 1, 0), ys[1 - par], rows)
                gather(s + 1, hs[1 - par], rows)

            ffn(hs[par], ys[par], copies)

        @pl.when((s == n_act) & (s % 2 == par))
        def _():
            scatter(s - 1, ys[1 - par])


def _moe(h2x, perm, cgrp, nact, wg, wu, wd):
    n = h2x.shape[0]
    n_tiles = n // MOE_TILE
    n_steps = n_tiles * MOE_STEPS

    step = jnp.arange(n_steps)
    active = (step % MOE_STEPS) < jnp.repeat(nact, MOE_STEPS)
    prev = jnp.concatenate([jnp.full((1,), -1, jnp.int32), cgrp[:-1]])
    first = active & ((step % MOE_STEPS == 0) | (cgrp != prev))
    run_id = jnp.cumsum(first.astype(jnp.int32)) - 1
    later_first = lax.cummin(jnp.where(first, step, n_steps), reverse=True)
    nxt_step = jnp.concatenate([later_first[1:], jnp.full((1,), n_steps, jnp.int32)])
    nxt_group = jnp.where(nxt_step < n_steps, cgrp[jnp.minimum(nxt_step, n_steps - 1)], -1)
    runs = jnp.concatenate([first.astype(jnp.int32), run_id % 2, nxt_group.astype(jnp.int32),
                            (first & (run_id == 0)).astype(jnp.int32)])

    def h_idx(i, s, *_):
        done = (s >= jnp.maximum(_[2][i] - 1, 1)).astype(jnp.int32)
        return (jnp.minimum(i + done, n_tiles - 1), 0)

    hbm = pl.BlockSpec(memory_space=pl.ANY)
    return pl.pallas_call(
        _moe_kernel,
        grid_spec=pltpu.PrefetchScalarGridSpec(
            num_scalar_prefetch=4,
            grid=(n_tiles, MOE_STEPS),
            in_specs=[pl.BlockSpec((MOE_TILE, H2X_W), h_idx), hbm, hbm, hbm],
            out_specs=pl.BlockSpec((None, MOE_TILE + 8, D_MODEL), lambda i, s, *_: (i, 0, 0)),
            scratch_shapes=[pltpu.VMEM((MOE_CHUNK, H2X_W), F32), pltpu.VMEM((MOE_CHUNK, H2X_W), F32),
                            pltpu.VMEM((MOE_CHUNK, D_MODEL), F32), pltpu.VMEM((MOE_CHUNK, D_MODEL), F32),
                            pltpu.VMEM((2, EXPERTS_PER_GROUP, D_MODEL, EXPERT_FF), BF16),
                            pltpu.VMEM((2, EXPERTS_PER_GROUP, D_MODEL, EXPERT_FF), BF16),
                            pltpu.VMEM((2, EXPERTS_PER_GROUP * EXPERT_FF, D_MODEL), BF16),
                            pltpu.SemaphoreType.DMA((2, 3))]),
        out_shape=jax.ShapeDtypeStruct((n_tiles, MOE_TILE + 8, D_MODEL), F32),
        compiler_params=_cparams(("arbitrary", "arbitrary")),
        name="moe_experts",
    )(perm, cgrp, nact, runs, h2x, wg, wu, wd)


def _final_kernel(y_ref, x1_ref, mods_ref, fg_ref, o_ref):
    gate2 = mods_ref[:, 5 * D_MODEL:6 * D_MODEL]
    x2 = x1_ref[...] + gate2 * y_ref[...]
    o_ref[...] = x2 * lax.rsqrt(jnp.mean(x2 * x2, axis=-1, keepdims=True) + EPS) * fg_ref[...]


def _final(y, x1, mods3, mod_row0, mod_tokens, fg):
    n = x1.shape[0]
    tm = 2 * ROW_TILE
    per_mod = mod_tokens // tm
    per_tile = MOE_TILE // tm
    return pl.pallas_call(
        _final_kernel,
        grid=(n // tm,),
        in_specs=[pl.BlockSpec((None, tm, D_MODEL), lambda j: (j // per_tile, j % per_tile, 0)),
                  pl.BlockSpec((tm, D_MODEL), lambda j: (j, 0)),
                  pl.BlockSpec((None, 1, 6 * D_MODEL), lambda j: (mod_row0 + j // per_mod, 0, 0)),
                  pl.BlockSpec((1, D_MODEL), lambda j: (0, 0))],
        out_specs=pl.BlockSpec((tm, D_MODEL), lambda j: (j, 0)),
        out_shape=jax.ShapeDtypeStruct((n, D_MODEL), F32),
        compiler_params=_cparams(("parallel",)),
        name="final_norm",
    )(y, x1, mods3, fg)


def _rope_tables(t):
    n_freq = QK_DIM // 4
    n_rows = t // GRID_W
    freqs = ROPE_BASE ** (-jnp.arange(n_freq, dtype=F32) / n_freq)
    ang_r = jnp.arange(n_rows, dtype=F32)[:, None] * freqs
    ang_c = jnp.arange(GRID_W, dtype=F32)[:, None] * freqs
    cr, sr, cc, sc = lax.optimization_barrier((jnp.cos(ang_r), jnp.sin(ang_r), jnp.cos(ang_c), jnp.sin(ang_c)))
    j = np.arange(LANES) % QK_DIM
    f_idx = j % n_freq
    by_row = (j < QK_DIM // 2)[None, None, :]
    first = ((j % (QK_DIM // 2)) < n_freq)[None, None, :]

    def table(r_small, c_small):
        return jnp.where(by_row, r_small[:, f_idx][:, None, :], c_small[:, f_idx][None, :, :])

    cos = table(cr, cc)
    sin = table(sr, sc)
    return (cos.reshape(t, LANES), jnp.where(first, -sin, 0.0).reshape(t, LANES),
            jnp.where(first, 0.0, sin).reshape(t, LANES))


def _layer(x, mods3, mod_row0, mod_tokens, rope_tabs, ctx_k, ctx_v, h0f, h0b, lw, layer):
    b, t, _ = x.shape
    n = b * t
    xf = x.reshape(n, D_MODEL)
    res = _inproj(xf, mods3, mod_row0, mod_tokens, t, lw["norm_mix_g"], lw["w_main"], lw["w_dt"], lw["alog"],
                  lw["dtb"], rope_tabs)
    q, kb, vt, z, xbc, cols, hrow = res[:7]
    if ctx_k is None:
        cache = None
        k3 = res[7].reshape(b, 1, t, ATT_HEADS, 2, LANES)[..., :QK_DIM]
        v3 = res[8].reshape(b, 1, t, ATT_HEADS, V_DIM)
    else:
        cache = (ctx_k.astype(BF16), jnp.swapaxes(ctx_v, 1, 2).astype(BF16))
        k3 = v3 = None
    lam0 = 0.8 - 0.6 * math.exp(-0.3 * layer)
    att = _attention(q.reshape(b, t, D_QK), kb.reshape(b, t, D_QK), vt, cache, lw["lamp"], lw["attn_subln_g"],
                     lam0, *_attn_tiling(b, t))
    ssm, hf, hb = _ssd(xbc, cols.reshape(b, t, LANES), hrow.reshape(b, t // CHUNK * HROWS, LANES),
                       z.reshape(b, t, D_SSM), h0f, h0b, lw["conv_w"], lw["conv_b"], lw["d_x"], lw["ssm_norm_g"])
    x1, h2x = _outproj(att.reshape(n, D_ATT), ssm.reshape(n, D_SSM), xf, mods3, mod_row0, mod_tokens,
                       lw["wo_att"], lw["wo_ssm"], lw["norm_ffn_g"], lw["w_router"], lw["b_router"])
    meta = _route(h2x)
    perm = meta[:, 0, :].reshape(-1)
    cgrp = meta[:, 1, :MOE_STEPS].reshape(-1)
    nact = meta[:, 2, 0]
    y = _moe(h2x, perm, cgrp, nact, lw["wg"], lw["wu"], lw["wd"])
    out = _final(y, x1, mods3, mod_row0, mod_tokens, lw["final_g"])
    return out.reshape(b, t, D_MODEL), k3, v3, hf, hb


def _pad_lanes(v, width=LANES):
    return jnp.pad(v, [(0, 0)] * (v.ndim - 1) + [(0, width - v.shape[-1])])


def kernel(x_prompt, x_sample, cache_k, cache_v, state_ssm_fwd, state_ssm_bwd, c, c_ctx, w_ada, b_ada, norm_mix_g, w_in, w_out, lambda_q1, lambda_k1, lambda_q2, lambda_k2, attn_subln_g, conv_w, conv_b, a_log_fwd, a_log_bwd, dt_bias_fwd, dt_bias_bwd, ssm_d, ssm_norm_g, norm_ffn_g, w_group_router, b_group_router, w_expert_router, b_expert_router, w_exp_gate, w_exp_up, w_exp_down, final_norm_g):
    depth = w_in.shape[0]
    assert depth == 1, "single trunk layer"
    bp, tp, _ = x_prompt.shape
    bs, ts, _ = x_sample.shape
    l = 0
    cond = jnp.concatenate([c_ctx[None], c], axis=0)
    condT = _pad_lanes(cond.T, 8)
    mods = _ada(condT, w_ada[l], b_ada[l][None])
    mods3 = mods.reshape(8, 1, 6 * D_MODEL)

    w_router = _pad_lanes(jnp.concatenate([w_group_router[l], w_expert_router[l]], axis=1))
    wr_hi = w_router.astype(BF16)
    wr_lo = (w_router - wr_hi.astype(F32)).astype(BF16)
    lw = dict(
        norm_mix_g=norm_mix_g[l][None],
        w_main=w_in[l].astype(BF16),
        w_dt=_pad_lanes(w_in[l][:, MAIN_COLS:]).astype(BF16),
        lamp=jnp.stack([lambda_q1[l], lambda_k1[l], lambda_q2[l], lambda_k2[l]]),
        attn_subln_g=attn_subln_g[l][None],
        conv_w=conv_w[l].reshape(CONV_W, N_COLBLK, LANES), conv_b=conv_b[l].reshape(N_COLBLK, LANES),
        alog=jnp.broadcast_to(jnp.concatenate([a_log_fwd[l], a_log_bwd[l]])[:, None], (2 * SSM_HEADS, CHUNK)),
        dtb=jnp.broadcast_to(jnp.concatenate([dt_bias_fwd[l], dt_bias_bwd[l]])[:, None], (2 * SSM_HEADS, CHUNK)),
        d_x=jnp.repeat(ssm_d[l], SSM_HEADDIM)[None], ssm_norm_g=ssm_norm_g[l][None],
        wo_att=w_out[l][:D_ATT].astype(BF16), wo_ssm=w_out[l][D_ATT:].astype(BF16),
        norm_ffn_g=norm_ffn_g[l][None],
        w_router=jnp.concatenate([wr_hi, wr_lo], axis=1),
        b_router=_pad_lanes(jnp.concatenate([b_group_router[l], b_expert_router[l]])[None]),
        wg=w_exp_gate[l].astype(BF16), wu=w_exp_up[l].astype(BF16), wd=w_exp_down[l].astype(BF16).reshape(N_EXPERTS * EXPERT_FF, D_MODEL),
        final_g=final_norm_g[None],
    )
    n_state = SSM_HEADS * SSM_HEADDIM
    yp, ck, cv, hf, hb = _layer(x_prompt, mods3, 0, bp * tp, None, None, None, None, None, lw, l)
    ys, _, _, _, _ = _layer(x_sample, mods3, 1, ts, _rope_tables(ts),
                            cache_k[:, l].reshape(bs, -1, D_QK), cache_v[:, l].reshape(bs, -1, D_ATT),
                            state_ssm_fwd[:, l].reshape(bs, n_state, D_STATE),
                            state_ssm_bwd[:, l].reshape(bs, n_state, D_STATE), lw, l)
    new_k, new_v = ck, cv
    new_hf = hf.reshape(bp, 1, SSM_HEADS, SSM_HEADDIM, D_STATE)
    new_hb = hb.reshape(bp, 1, SSM_HEADS, SSM_HEADDIM, D_STATE)
    return yp, ys, new_k, new_v, new_hf, new_hb
```
